```python
import jax
import jax.numpy as jnp
from jax import lax
import numpy as np

D_MODEL = 1024
BATCH = 8
SEQ = 16384
DEPTH = 2

CHUNK = 64
EPS = 1e-6
NEG_INF = -1e30

RET_HEADS = 4
RET_QK_DIM = 128
RET_V_DIM = 256
ATT_HEADS = 8
ATT_HEAD_DIM = 64
ATT_PAST_CHUNKS = 8
MAX_REL = 128
SGU_BLOCK = 128
SGU_GROUPS = 8
SGU_WIDTH = 2048
FFN_HIDDEN = 2816
CONV_WIDTH = 3

RET_QK_W = RET_HEADS * RET_QK_DIM
RET_V_W = RET_HEADS * RET_V_DIM
ATT_W = ATT_HEADS * ATT_HEAD_DIM
AB_IN_W = 2 * RET_QK_W + 2 * RET_V_W + 3 * ATT_W
AB_OUT_W = RET_V_W + ATT_W
N_EVEN = (DEPTH + 1) // 2
N_ODD = DEPTH // 2

kernel_name = "hybrid_retention_chunkattn_gmlp_convffn"


def rms_norm(x, g):
    xf = x.astype(jnp.float32)
    y = xf * lax.rsqrt(jnp.mean(xf * xf, axis=-1, keepdims=True) + EPS)
    return (y * g.astype(jnp.float32)).astype(x.dtype)


def layer_norm(x, g, b):
    xf = x.astype(jnp.float32)
    mu = jnp.mean(xf, axis=-1, keepdims=True)
    var = jnp.mean(jnp.square(xf - mu), axis=-1, keepdims=True)
    y = (xf - mu) * lax.rsqrt(var + EPS)
    return (y * g.astype(jnp.float32) + b.astype(jnp.float32)).astype(x.dtype)


def rotary(x, pos):
    half = x.shape[-1] // 2
    inv = 1.0 / (10000.0 ** jnp.linspace(0.0, 1.0, half, dtype=jnp.float32))
    ang = pos.astype(jnp.float32)[:, None] * inv[None, :]
    cos = jnp.cos(ang)[None, :, None, :]
    sin = jnp.sin(ang)[None, :, None, :]
    xf = x.astype(jnp.float32)
    x1, x2 = xf[..., :half], xf[..., half:]
    return jnp.concatenate([x1 * cos - x2 * sin, x1 * sin + x2 * cos], axis=-1)


def retention(q, k, v):
    B, T, H, dk = q.shape
    dv = v.shape[-1]
    nc = T // CHUNK
    f32 = jnp.float32
    log_g = jnp.log1p(-jnp.exp2(-5.0 - jnp.arange(H, dtype=f32)))
    qc = q.astype(f32).reshape(B, nc, CHUNK, H, dk)
    kc = (k.astype(f32) * dk ** -0.5).reshape(B, nc, CHUNK, H, dk)
    vc = v.astype(f32).reshape(B, nc, CHUNK, H, dv)
    idx = jnp.arange(CHUNK, dtype=f32)
    d_intra = jnp.exp(log_g[:, None, None] * jnp.abs(idx[:, None] - idx[None, :]))
    s = jnp.einsum('bnihd,bnjhd->bnhij', qc, kc) * d_intra
    o_intra = jnp.einsum('bnhij,bnjhe->bnihe', s, vc)
    k_dec = jnp.exp(log_g[None, :] * (CHUNK - 1 - idx)[:, None])
    q_dec = jnp.exp(log_g[None, :] * (idx + 1.0)[:, None])
    chunk_dec = jnp.exp(log_g * CHUNK)

    def step(state, inp):
        qn, kn, vn = inp
        o = jnp.einsum('bihd,bhde->bihe', qn * q_dec[None, :, :, None], state)
        state = state * chunk_dec[None, :, None, None] + jnp.einsum(
            'bjhd,bjhe->bhde', kn * k_dec[None, :, :, None], vn)
        return state, o

    init = jnp.zeros((B, H, dk, dv), f32)
    _, o_inter = lax.scan(step, init, (qc.swapaxes(0, 1), kc.swapaxes(0, 1), vc.swapaxes(0, 1)))
    o = o_intra + o_inter.swapaxes(0, 1)
    return o.reshape(B, T, H, dv)


def _chunk_attn_one_head(args):
    q, k, v, rel_bias = args
    B, T, d = q.shape
    nc = T // CHUNK
    nb = ATT_PAST_CHUNKS + 1
    qc = q.reshape(B, nc, CHUNK, d)
    pad = ((0, 0), (ATT_PAST_CHUNKS * CHUNK, 0), (0, 0))
    kp = jnp.pad(k, pad).reshape(B, nc + ATT_PAST_CHUNKS, CHUNK, d)
    vp = jnp.pad(v, pad).reshape(B, nc + ATT_PAST_CHUNKS, CHUNK, d)
    band_idx = jnp.arange(nc)[:, None] + jnp.arange(nb)[None, :]
    kb = kp[:, band_idx].reshape(B, nc, nb * CHUNK, d)
    vb = vp[:, band_idx].reshape(B, nc, nb * CHUNK, d)
    s = jnp.einsum('bnid,bnjd->bnij', qc, kb).astype(jnp.float32) * (d ** -0.5)
    qpos = jnp.arange(CHUNK) + ATT_PAST_CHUNKS * CHUNK
    kpos = jnp.arange(nb * CHUNK)
    rel = jnp.clip(qpos[:, None] - kpos[None, :], -MAX_REL, MAX_REL) + MAX_REL
    s = s + rel_bias.astype(jnp.float32)[rel][None, None]
    valid = jnp.repeat(band_idx >= ATT_PAST_CHUNKS, CHUNK, axis=1)
    s = jnp.where(valid[None, :, None, :], s, NEG_INF)
    p = jax.nn.softmax(s, axis=-1).astype(v.dtype)
    o = jnp.einsum('bnij,bnjd->bnid', p, vb)
    return o.reshape(B, T, d)


def chunk_rel_attention(q, k, v, rel_bias):
    o = lax.map(_chunk_attn_one_head,
                (q.transpose(2, 0, 1, 3), k.transpose(2, 0, 1, 3), v.transpose(2, 0, 1, 3), rel_bias))
    return o.transpose(1, 2, 0, 3)


def ab_mixer(h, w_in, w_out, rel_bias, pos):
    B, T, _ = h.shape
    z = h @ w_in
    splits = [RET_QK_W, 2 * RET_QK_W, 2 * RET_QK_W + RET_V_W, 2 * RET_QK_W + 2 * RET_V_W,
              2 * RET_QK_W + 2 * RET_V_W + ATT_W, 2 * RET_QK_W + 2 * RET_V_W + 2 * ATT_W]
    q_a, k_a, v_a, g_a, q_b, k_b, v_b = jnp.split(z, splits, axis=-1)
    q_a = rotary(q_a.reshape(B, T, RET_HEADS, RET_QK_DIM), pos)
    k_a = rotary(k_a.reshape(B, T, RET_HEADS, RET_QK_DIM), pos)
    r = retention(q_a, k_a, v_a.reshape(B, T, RET_HEADS, RET_V_DIM))
    mu = jnp.mean(r, axis=-1, keepdims=True)
    var = jnp.mean(jnp.square(r - mu), axis=-1, keepdims=True)
    r = ((r - mu) * lax.rsqrt(var + EPS)).reshape(B, T, RET_V_W)
    y_a = (jax.nn.silu(g_a.astype(jnp.float32)) * r).astype(h.dtype)
    y_b = chunk_rel_attention(q_b.reshape(B, T, ATT_HEADS, ATT_HEAD_DIM),
                              k_b.reshape(B, T, ATT_HEADS, ATT_HEAD_DIM),
                              v_b.reshape(B, T, ATT_HEADS, ATT_HEAD_DIM),
                              rel_bias).reshape(B, T, ATT_W)
    return jnp.concatenate([y_a, y_b.astype(h.dtype)], axis=-1) @ w_out


def sgu_mixer(h, w_in, ln_g, ln_b, w_s, b_s, w_out):
    B, T, _ = h.shape
    z = jax.nn.gelu(h @ w_in)
    u, v = jnp.split(z, 2, axis=-1)
    v = layer_norm(v, ln_g, ln_b)
    nb = T // SGU_BLOCK
    vg = v.reshape(B, nb, SGU_BLOCK, SGU_GROUPS, SGU_WIDTH // SGU_GROUPS)
    i = jnp.arange(SGU_BLOCK)
    mask = (i[None, :] // CHUNK) <= (i[:, None] // CHUNK)
    w = jnp.where(mask[None], w_s, jnp.zeros_like(w_s))
    mixed = jnp.einsum('gij,bnjgc->bnigc', w, vg) + b_s.T[None, None, :, :, None]
    y = u * mixed.reshape(B, T, SGU_WIDTH)
    return y @ w_out


def conv_ffn(h, w_up, conv_w, conv_b, w_down):
    z = h @ w_up
    c = z.shape[-1]
    z = lax.conv_general_dilated(z, conv_w[:, None, :].astype(z.dtype), window_strides=(1,),
                                 padding=[(CONV_WIDTH - 1, 0)],
                                 dimension_numbers=('NWC', 'WIO', 'NWC'),
                                 feature_group_count=c) + conv_b
    gate, up = jnp.split(z, 2, axis=-1)
    return (jax.nn.gelu(gate) * up) @ w_down


def _fwd_setup_inputs(seed: int = 0) -> dict:
    key = jax.random.key(seed)
    ks = jax.random.split(key, 20)
    n = jax.random.normal
    f32 = jnp.float32
    nr = 2 * MAX_REL + 1
    return {
        "x": n(ks[0], (BATCH, SEQ, D_MODEL), f32),
        "attn_norm_g": 1.0 + 0.02 * n(ks[1], (DEPTH, D_MODEL), f32),
        "ffn_norm_g": 1.0 + 0.02 * n(ks[2], (DEPTH, D_MODEL), f32),
        "ab_w_in": n(ks[3], (N_EVEN, D_MODEL, AB_IN_W), f32) * D_MODEL ** -0.5,
        "ab_w_out": n(ks[4], (N_EVEN, AB_OUT_W, D_MODEL), f32) * AB_OUT_W ** -0.5,
        "ab_rel_bias": 0.1 * n(ks[5], (N_EVEN, ATT_HEADS, nr), f32),
        "c_w_in": n(ks[6], (N_ODD, D_MODEL, 2 * SGU_WIDTH), f32) * D_MODEL ** -0.5,
        "c_ln_g": 1.0 + 0.02 * n(ks[7], (N_ODD, SGU_WIDTH), f32),
        "c_ln_b": 0.02 * n(ks[8], (N_ODD, SGU_WIDTH), f32),
        "c_w_s": n(ks[9], (N_ODD, SGU_GROUPS, SGU_BLOCK, SGU_BLOCK), f32) * SGU_BLOCK ** -0.5,
        "c_b_s": 1.0 + 0.02 * n(ks[10], (N_ODD, SGU_GROUPS, SGU_BLOCK), f32),
        "c_w_out": n(ks[11], (N_ODD, SGU_WIDTH, D_MODEL), f32) * SGU_WIDTH ** -0.5,
        "ffn_w_up": n(ks[12], (DEPTH, D_MODEL, 2 * FFN_HIDDEN), f32) * D_MODEL ** -0.5,
        "ffn_conv_w": n(ks[13], (DEPTH, CONV_WIDTH, 2 * FFN_HIDDEN), f32) * CONV_WIDTH ** -0.5,
        "ffn_conv_b": 0.02 * n(ks[14], (DEPTH, 2 * FFN_HIDDEN), f32),
        "ffn_w_down": n(ks[15], (DEPTH, FFN_HIDDEN, D_MODEL), f32) * FFN_HIDDEN ** -0.5,
        "final_norm_g": 1.0 + 0.02 * n(ks[16], (D_MODEL,), f32),
    }


def _fwd_reference(x, attn_norm_g, ffn_norm_g, ab_w_in, ab_w_out, ab_rel_bias, c_w_in, c_ln_g,
              c_ln_b, c_w_s, c_b_s, c_w_out, ffn_w_up, ffn_conv_w, ffn_conv_b, ffn_w_down,
              final_norm_g):
    h = x
    pos = jnp.arange(x.shape[1])
    for layer in range(DEPTH):
        hn = rms_norm(h, attn_norm_g[layer])
        i = layer // 2
        if layer % 2 == 0:
            h = h + ab_mixer(hn, ab_w_in[i], ab_w_out[i], ab_rel_bias[i], pos).astype(h.dtype)
        else:
            h = h + sgu_mixer(hn, c_w_in[i], c_ln_g[i], c_ln_b[i], c_w_s[i], c_b_s[i],
                              c_w_out[i]).astype(h.dtype)
        h = h + conv_ffn(rms_norm(h, ffn_norm_g[layer]), ffn_w_up[layer], ffn_conv_w[layer],
                         ffn_conv_b[layer], ffn_w_down[layer]).astype(h.dtype)
    return rms_norm(h, final_norm_g)


import jax as _jax
import jax.numpy as _jnp

TWIN_FORMAT = 'train_step'
FWD_PARAMS = ['x', 'attn_norm_g', 'ffn_norm_g', 'ab_w_in', 'ab_w_out', 'ab_rel_bias', 'c_w_in', 'c_ln_g', 'c_ln_b', 'c_w_s', 'c_b_s', 'c_w_out', 'ffn_w_up', 'ffn_conv_w', 'ffn_conv_b', 'ffn_w_down', 'final_norm_g']
TWIN_WEIGHTS = ['attn_norm_g', 'ffn_norm_g', 'ab_w_in', 'ab_w_out', 'ab_rel_bias', 'c_w_in', 'c_ln_g', 'c_ln_b', 'c_w_s', 'c_b_s', 'c_w_out', 'ffn_w_up', 'ffn_conv_w', 'ffn_conv_b', 'ffn_w_down', 'final_norm_g']
TWIN_DIFF_INPUT = 'x'
TWIN_INPUTS = ['x', 'attn_norm_g', 'ffn_norm_g', 'ab_w_in', 'ab_w_out', 'ab_rel_bias', 'c_w_in', 'c_ln_g', 'c_ln_b', 'c_w_s', 'c_b_s', 'c_w_out', 'ffn_w_up', 'ffn_conv_w', 'ffn_conv_b', 'ffn_w_down', 'final_norm_g', 'loss_target', 'm_attn_norm_g', 'm_ffn_norm_g', 'm_ab_w_in', 'm_ab_w_out', 'm_ab_rel_bias', 'm_c_w_in', 'm_c_ln_g', 'm_c_ln_b', 'm_c_w_s', 'm_c_b_s', 'm_c_w_out', 'm_ffn_w_up', 'm_ffn_conv_w', 'm_ffn_conv_b', 'm_ffn_w_down', 'm_final_norm_g', 'v_attn_norm_g', 'v_ffn_norm_g', 'v_ab_w_in', 'v_ab_w_out', 'v_ab_rel_bias', 'v_c_w_in', 'v_c_ln_g', 'v_c_ln_b', 'v_c_w_s', 'v_c_b_s', 'v_c_w_out', 'v_ffn_w_up', 'v_ffn_conv_w', 'v_ffn_conv_b', 'v_ffn_w_down', 'v_final_norm_g']
TWIN_OUTPUTS = ['loss', 'grad_x', 'grad_attn_norm_g', 'grad_ffn_norm_g', 'grad_ab_w_in', 'grad_ab_w_out', 'grad_ab_rel_bias', 'grad_c_w_in', 'grad_c_ln_g', 'grad_c_ln_b', 'grad_c_w_s', 'grad_c_b_s', 'grad_c_w_out', 'grad_ffn_w_up', 'grad_ffn_conv_w', 'grad_ffn_conv_b', 'grad_ffn_w_down', 'grad_final_norm_g', 'delta_attn_norm_g', 'delta_ffn_norm_g', 'delta_ab_w_in', 'delta_ab_w_out', 'delta_ab_rel_bias', 'delta_c_w_in', 'delta_c_ln_g', 'delta_c_ln_b', 'delta_c_w_s', 'delta_c_b_s', 'delta_c_w_out', 'delta_ffn_w_up', 'delta_ffn_conv_w', 'delta_ffn_conv_b', 'delta_ffn_w_down', 'delta_final_norm_g', 'new_m_attn_norm_g', 'new_m_ffn_norm_g', 'new_m_ab_w_in', 'new_m_ab_w_out', 'new_m_ab_rel_bias', 'new_m_c_w_in', 'new_m_c_ln_g', 'new_m_c_ln_b', 'new_m_c_w_s', 'new_m_c_b_s', 'new_m_c_w_out', 'new_m_ffn_w_up', 'new_m_ffn_conv_w', 'new_m_ffn_conv_b', 'new_m_ffn_w_down', 'new_m_final_norm_g', 'new_v_attn_norm_g', 'new_v_ffn_norm_g', 'new_v_ab_w_in', 'new_v_ab_w_out', 'new_v_ab_rel_bias', 'new_v_c_w_in', 'new_v_c_ln_g', 'new_v_c_ln_b', 'new_v_c_w_s', 'new_v_c_b_s', 'new_v_c_w_out', 'new_v_ffn_w_up', 'new_v_ffn_conv_w', 'new_v_ffn_conv_b', 'new_v_ffn_w_down', 'new_v_final_norm_g']
TWIN_LEAF_KINDS = {'loss': 'loss', 'grad_x': 'grad_x', 'grad_attn_norm_g': 'grad_w', 'grad_ffn_norm_g': 'grad_w', 'grad_ab_w_in': 'grad_w', 'grad_ab_w_out': 'grad_w', 'grad_ab_rel_bias': 'grad_w', 'grad_c_w_in': 'grad_w', 'grad_c_ln_g': 'grad_w', 'grad_c_ln_b': 'grad_w', 'grad_c_w_s': 'grad_w', 'grad_c_b_s': 'grad_w', 'grad_c_w_out': 'grad_w', 'grad_ffn_w_up': 'grad_w', 'grad_ffn_conv_w': 'grad_w', 'grad_ffn_conv_b': 'grad_w', 'grad_ffn_w_down': 'grad_w', 'grad_final_norm_g': 'grad_w', 'delta_attn_norm_g': 'delta_w', 'delta_ffn_norm_g': 'delta_w', 'delta_ab_w_in': 'delta_w', 'delta_ab_w_out': 'delta_w', 'delta_ab_rel_bias': 'delta_w', 'delta_c_w_in': 'delta_w', 'delta_c_ln_g': 'delta_w', 'delta_c_ln_b': 'delta_w', 'delta_c_w_s': 'delta_w', 'delta_c_b_s': 'delta_w', 'delta_c_w_out': 'delta_w', 'delta_ffn_w_up': 'delta_w', 'delta_ffn_conv_w': 'delta_w', 'delta_ffn_conv_b': 'delta_w', 'delta_ffn_w_down': 'delta_w', 'delta_final_norm_g': 'delta_w', 'new_m_attn_norm_g': 'new_m', 'new_m_ffn_norm_g': 'new_m', 'new_m_ab_w_in': 'new_m', 'new_m_ab_w_out': 'new_m', 'new_m_ab_rel_bias': 'new_m', 'new_m_c_w_in': 'new_m', 'new_m_c_ln_g': 'new_m', 'new_m_c_ln_b': 'new_m', 'new_m_c_w_s': 'new_m', 'new_m_c_b_s': 'new_m', 'new_m_c_w_out': 'new_m', 'new_m_ffn_w_up': 'new_m', 'new_m_ffn_conv_w': 'new_m', 'new_m_ffn_conv_b': 'new_m', 'new_m_ffn_w_down': 'new_m', 'new_m_final_norm_g': 'new_m', 'new_v_attn_norm_g': 'new_v', 'new_v_ffn_norm_g': 'new_v', 'new_v_ab_w_in': 'new_v', 'new_v_ab_w_out': 'new_v', 'new_v_ab_rel_bias': 'new_v', 'new_v_c_w_in': 'new_v', 'new_v_c_ln_g': 'new_v', 'new_v_c_ln_b': 'new_v', 'new_v_c_w_s': 'new_v', 'new_v_c_b_s': 'new_v', 'new_v_c_w_out': 'new_v', 'new_v_ffn_w_up': 'new_v', 'new_v_ffn_conv_w': 'new_v', 'new_v_ffn_conv_b': 'new_v', 'new_v_ffn_w_down': 'new_v', 'new_v_final_norm_g': 'new_v'}


def _forward(args):
    return _fwd_reference(*[args[k] for k in FWD_PARAMS])


def _output_shape():
    def fwd():
        inp = _fwd_setup_inputs(0)
        return _fwd_reference(*[inp[k] for k in FWD_PARAMS])
    out = _jax.eval_shape(fwd)
    return out.shape, out.dtype

N_MICROBATCH = 1
ADAM_LR = 0.001
ADAM_B1 = 0.9
ADAM_B2 = 0.999
ADAM_EPS = 1e-08
ADAM_WD = 0.01
ADAM_STEP = 10
PER_EXAMPLE_BATCH_AXIS = {'x': 0, 'loss_target': 0}
SHARED_INPUTS = []
_WEIGHT_DTYPES = {'attn_norm_g': _jnp.float32, 'ffn_norm_g': _jnp.float32, 'ab_w_in': _jnp.float32, 'ab_w_out': _jnp.float32, 'ab_rel_bias': _jnp.float32, 'c_w_in': _jnp.float32, 'c_ln_g': _jnp.float32, 'c_ln_b': _jnp.float32, 'c_w_s': _jnp.float32, 'c_b_s': _jnp.float32, 'c_w_out': _jnp.float32, 'ffn_w_up': _jnp.float32, 'ffn_conv_w': _jnp.float32, 'ffn_conv_b': _jnp.float32, 'ffn_w_down': _jnp.float32, 'final_norm_g': _jnp.float32}
MOMENT_SCALE = {'attn_norm_g': 3.594475e-01, 'ffn_norm_g': 2.758070e-01, 'ab_w_in': 1.992225e-01, 'ab_w_out': 2.084557e-01, 'ab_rel_bias': 2.302073e-02, 'c_w_in': 1.350461e-01, 'c_ln_g': 9.874132e-02, 'c_ln_b': 9.469946e-02, 'c_w_s': 1.381333e-01, 'c_b_s': 1.744001e-01, 'c_w_out': 2.197387e-01, 'ffn_w_up': 1.124681e-01, 'ffn_conv_w': 1.124571e-01, 'ffn_conv_b': 1.201944e-01, 'ffn_w_down': 1.841943e-01, 'final_norm_g': 1.282405e+02}


def _to_microbatches(a, axis):
    t = _jnp.moveaxis(a, axis, 0)
    t = t.reshape((N_MICROBATCH, t.shape[0] // N_MICROBATCH) + t.shape[1:])
    return _jnp.moveaxis(t, 1, axis + 1)


def setup_inputs(seed: int = 0) -> dict:
    inp = _fwd_setup_inputs(seed)
    key = _jax.random.fold_in(_jax.random.key(seed), 7919)
    shape, _ = _output_shape()
    out = dict(inp)
    out["loss_target"] = _jax.random.normal(_jax.random.fold_in(key, 0), shape, _jnp.float32)
    for i, name in enumerate(TWIN_WEIGHTS):
        w = inp[name].astype(_jnp.float32)
        if MOMENT_SCALE is None:
            s = _jnp.sqrt(_jnp.mean(_jnp.square(w)) + 1e-30)
        else:
            s = MOMENT_SCALE[name]
        km, kv = _jax.random.split(_jax.random.fold_in(key, i + 1))
        out[name] = w
        out["m_" + name] = s * _jax.random.normal(km, w.shape, _jnp.float32)
        out["v_" + name] = (s * s) * _jax.random.uniform(kv, w.shape, _jnp.float32, 0.5, 1.5)
    if N_MICROBATCH > 1:
        for name, axis in PER_EXAMPLE_BATCH_AXIS.items():
            out[name] = _to_microbatches(out[name], axis)
    return {'x': out['x'], 'attn_norm_g': out['attn_norm_g'], 'ffn_norm_g': out['ffn_norm_g'], 'ab_w_in': out['ab_w_in'], 'ab_w_out': out['ab_w_out'], 'ab_rel_bias': out['ab_rel_bias'], 'c_w_in': out['c_w_in'], 'c_ln_g': out['c_ln_g'], 'c_ln_b': out['c_ln_b'], 'c_w_s': out['c_w_s'], 'c_b_s': out['c_b_s'], 'c_w_out': out['c_w_out'], 'ffn_w_up': out['ffn_w_up'], 'ffn_conv_w': out['ffn_conv_w'], 'ffn_conv_b': out['ffn_conv_b'], 'ffn_w_down': out['ffn_w_down'], 'final_norm_g': out['final_norm_g'], 'loss_target': out['loss_target'], 'm_attn_norm_g': out['m_attn_norm_g'], 'm_ffn_norm_g': out['m_ffn_norm_g'], 'm_ab_w_in': out['m_ab_w_in'], 'm_ab_w_out': out['m_ab_w_out'], 'm_ab_rel_bias': out['m_ab_rel_bias'], 'm_c_w_in': out['m_c_w_in'], 'm_c_ln_g': out['m_c_ln_g'], 'm_c_ln_b': out['m_c_ln_b'], 'm_c_w_s': out['m_c_w_s'], 'm_c_b_s': out['m_c_b_s'], 'm_c_w_out': out['m_c_w_out'], 'm_ffn_w_up': out['m_ffn_w_up'], 'm_ffn_conv_w': out['m_ffn_conv_w'], 'm_ffn_conv_b': out['m_ffn_conv_b'], 'm_ffn_w_down': out['m_ffn_w_down'], 'm_final_norm_g': out['m_final_norm_g'], 'v_attn_norm_g': out['v_attn_norm_g'], 'v_ffn_norm_g': out['v_ffn_norm_g'], 'v_ab_w_in': out['v_ab_w_in'], 'v_ab_w_out': out['v_ab_w_out'], 'v_ab_rel_bias': out['v_ab_rel_bias'], 'v_c_w_in': out['v_c_w_in'], 'v_c_ln_g': out['v_c_ln_g'], 'v_c_ln_b': out['v_c_ln_b'], 'v_c_w_s': out['v_c_w_s'], 'v_c_b_s': out['v_c_b_s'], 'v_c_w_out': out['v_c_w_out'], 'v_ffn_w_up': out['v_ffn_w_up'], 'v_ffn_conv_w': out['v_ffn_conv_w'], 'v_ffn_conv_b': out['v_ffn_conv_b'], 'v_ffn_w_down': out['v_ffn_w_down'], 'v_final_norm_g': out['v_final_norm_g']}


def _loss(weights, diff, rest, loss_target):
    with _jax.named_scope("forward"):
        args = {**rest, TWIN_DIFF_INPUT: diff, **{k: w.astype(_WEIGHT_DTYPES[k]) for k, w in weights.items()}}
        y = _forward(args)
    with _jax.named_scope("loss_head"):
        err = _jnp.square(y.astype(_jnp.float32) - loss_target)
        return 0.5 * _jnp.sum(_jnp.mean(err, axis=-1)) if err.ndim else 0.5 * err


def _adamw(w, g, m, v):
    m = ADAM_B1 * m + (1.0 - ADAM_B1) * g
    v = ADAM_B2 * v + (1.0 - ADAM_B2) * _jnp.square(g)
    m_hat = m / (1.0 - ADAM_B1 ** ADAM_STEP)
    v_hat = v / (1.0 - ADAM_B2 ** ADAM_STEP)
    delta = -ADAM_LR * (m_hat / (_jnp.sqrt(v_hat) + ADAM_EPS) + ADAM_WD * w)
    return delta, m, v


def reference(x, attn_norm_g, ffn_norm_g, ab_w_in, ab_w_out, ab_rel_bias, c_w_in, c_ln_g, c_ln_b, c_w_s, c_b_s, c_w_out, ffn_w_up, ffn_conv_w, ffn_conv_b, ffn_w_down, final_norm_g, loss_target, m_attn_norm_g, m_ffn_norm_g, m_ab_w_in, m_ab_w_out, m_ab_rel_bias, m_c_w_in, m_c_ln_g, m_c_ln_b, m_c_w_s, m_c_b_s, m_c_w_out, m_ffn_w_up, m_ffn_conv_w, m_ffn_conv_b, m_ffn_w_down, m_final_norm_g, v_attn_norm_g, v_ffn_norm_g, v_ab_w_in, v_ab_w_out, v_ab_rel_bias, v_c_w_in, v_c_ln_g, v_c_ln_b, v_c_w_s, v_c_b_s, v_c_w_out, v_ffn_w_up, v_ffn_conv_w, v_ffn_conv_b, v_ffn_w_down, v_final_norm_g):
    given = dict(x=x, attn_norm_g=attn_norm_g, ffn_norm_g=ffn_norm_g, ab_w_in=ab_w_in, ab_w_out=ab_w_out, ab_rel_bias=ab_rel_bias, c_w_in=c_w_in, c_ln_g=c_ln_g, c_ln_b=c_ln_b, c_w_s=c_w_s, c_b_s=c_b_s, c_w_out=c_w_out, ffn_w_up=ffn_w_up, ffn_conv_w=ffn_conv_w, ffn_conv_b=ffn_conv_b, ffn_w_down=ffn_w_down, final_norm_g=final_norm_g, loss_target=loss_target, m_attn_norm_g=m_attn_norm_g, m_ffn_norm_g=m_ffn_norm_g, m_ab_w_in=m_ab_w_in, m_ab_w_out=m_ab_w_out, m_ab_rel_bias=m_ab_rel_bias, m_c_w_in=m_c_w_in, m_c_ln_g=m_c_ln_g, m_c_ln_b=m_c_ln_b, m_c_w_s=m_c_w_s, m_c_b_s=m_c_b_s, m_c_w_out=m_c_w_out, m_ffn_w_up=m_ffn_w_up, m_ffn_conv_w=m_ffn_conv_w, m_ffn_conv_b=m_ffn_conv_b, m_ffn_w_down=m_ffn_w_down, m_final_norm_g=m_final_norm_g, v_attn_norm_g=v_attn_norm_g, v_ffn_norm_g=v_ffn_norm_g, v_ab_w_in=v_ab_w_in, v_ab_w_out=v_ab_w_out, v_ab_rel_bias=v_ab_rel_bias, v_c_w_in=v_c_w_in, v_c_ln_g=v_c_ln_g, v_c_ln_b=v_c_ln_b, v_c_w_s=v_c_w_s, v_c_b_s=v_c_b_s, v_c_w_out=v_c_w_out, v_ffn_w_up=v_ffn_w_up, v_ffn_conv_w=v_ffn_conv_w, v_ffn_conv_b=v_ffn_conv_b, v_ffn_w_down=v_ffn_w_down, v_final_norm_g=v_final_norm_g)
    weights = {n: given[n] for n in TWIN_WEIGHTS}
    shared = {n: given[n] for n in SHARED_INPUTS}
    per_example = {n: given[n] for n in ['x']}
    grad_fn = _jax.value_and_grad(_loss, argnums=(0, 1))

    def one_microbatch(ex, loss_target):
        ex = dict(ex)
        diff = ex.pop(TWIN_DIFF_INPUT)
        return grad_fn(weights, diff, {**shared, **ex}, loss_target)

    if N_MICROBATCH == 1:
        loss, (grad_w, grad_x) = one_microbatch(per_example, given["loss_target"])
    else:
        def body(carry, xs):
            loss_sum, grad_sum = carry
            l_k, (gw_k, gx_k) = one_microbatch(xs[0], xs[1])
            with _jax.named_scope("update"):
                return (loss_sum + l_k, _jax.tree.map(_jnp.add, grad_sum, gw_k)), gx_k

        init = (_jnp.zeros((), _jnp.float32), _jax.tree.map(_jnp.zeros_like, weights))
        (loss, grad_w), grad_x = _jax.lax.scan(body, init, (per_example, given["loss_target"]))
    with _jax.named_scope("update"):
        delta_w, new_m, new_v = {}, {}, {}
        for n in TWIN_WEIGHTS:
            delta_w[n], new_m[n], new_v[n] = _adamw(weights[n], grad_w[n], given["m_" + n], given["v_" + n])
    return (loss, grad_x, *[grad_w[n] for n in TWIN_WEIGHTS], *[delta_w[n] for n in TWIN_WEIGHTS],
            *[new_m[n] for n in TWIN_WEIGHTS], *[new_v[n] for n in TWIN_WEIGHTS])
```

```python
import functools

import numpy as np
import jax
import jax.numpy as jnp
from jax import lax
from jax.experimental import pallas as pl
from jax.experimental.pallas import tpu as pltpu

F32 = jnp.float32
BF16 = jnp.bfloat16

D_MODEL = 1024
CHUNK = 64
EPS = 1e-6
NEG_INF = -1e30
RET_HEADS = 4
RET_QK_DIM = 128
RET_V_DIM = 256
ATT_HEADS = 8
ATT_HEAD_DIM = 64
ATT_PAST = 8
ATT_BAND = (ATT_PAST + 1) * CHUNK
MAX_REL = 128
N_REL = 2 * MAX_REL + 1
N_REL_PAD = 384
SGU_BLOCK = 128
SGU_GROUPS = 8
SGU_WIDTH = 2048
SGU_GW = SGU_WIDTH // SGU_GROUPS
FFN_HIDDEN = 2816
RET_W = 2 * RET_HEADS * RET_QK_DIM + 2 * RET_HEADS * RET_V_DIM
ATT_W = 3 * ATT_HEADS * ATT_HEAD_DIM
N_DEV = 8

ADAM_LR = 0.001
ADAM_B1 = 0.9
ADAM_B2 = 0.999
ADAM_EPS = 1e-08
ADAM_WD = 0.01
ADAM_STEP = 10

VMEM_LIMIT = 52 * 1024 * 1024


def _cp(*sem):
    return pltpu.CompilerParams(dimension_semantics=sem if sem else None, vmem_limit_bytes=VMEM_LIMIT)


def _tile(n, target):
    if n <= target:
        return n
    best = None
    for t in range(128, target + 1, 128):
        if n % t == 0:
            best = t
    assert best is not None, (n, target)
    return best


def _gelu(x):
    c = 0.7978845608028654
    return 0.5 * x * (1.0 + jnp.tanh(c * (x + 0.044715 * x * x * x)))


def _gelu_and_grad(x):
    c = 0.7978845608028654
    x2 = x * x
    t = jnp.tanh(c * (x + 0.044715 * x * x2))
    cdf = 0.5 * (1.0 + t)
    grad = cdf + x * (0.5 * c) * (1.0 - t * t) * (1.0 + 3.0 * 0.044715 * x2)
    return x * cdf, grad


def _dot(a, b, dims):
    return lax.dot_general(a, b, (dims, ((), ())), preferred_element_type=F32)


NN = ((1,), (0,))
NT = ((1,), (1,))
TN = ((0,), (0,))


def _mm(a, b, mode, out_dtype, res=None, name="mm", tm_t=1024, tn_t=1408, tk_t=512):
    if mode == "nt":
        (M, K), N = a.shape, b.shape[0]
    elif mode == "nn":
        (M, K), N = a.shape, b.shape[1]
    else:
        (K, M), N = a.shape, b.shape[1]
    tm, tn, tk = _tile(M, tm_t), _tile(N, tn_t), _tile(K, tk_t)
    nk = K // tk
    dims = {"nt": NT, "nn": NN, "tn": TN}[mode]
    a_spec = pl.BlockSpec((tk, tm), lambda i, j, k: (k, i)) if mode == "tn" else pl.BlockSpec((tm, tk), lambda i, j, k: (i, k))
    b_spec = pl.BlockSpec((tn, tk), lambda i, j, k: (j, k)) if mode == "nt" else pl.BlockSpec((tk, tn), lambda i, j, k: (k, j))
    o_spec = pl.BlockSpec((tm, tn), lambda i, j, k: (i, j))
    has_res = res is not None

    def body(*refs):
        if has_res:
            a_ref, b_ref, r_ref, o_ref = refs[:4]
        else:
            a_ref, b_ref, o_ref = refs[:3]
        p = _dot(a_ref[...].astype(BF16), b_ref[...].astype(BF16), dims)
        if nk == 1:
            if has_res:
                p = p + r_ref[...]
            o_ref[...] = p.astype(out_dtype)
            return
        acc = refs[-1]
        k = pl.program_id(2)

        @pl.when(k == 0)
        def _():
            acc[...] = p

        @pl.when(k > 0)
        def _():
            acc[...] += p

        @pl.when(k == nk - 1)
        def _():
            t = acc[...]
            if has_res:
                t = t + r_ref[...]
            o_ref[...] = t.astype(out_dtype)

    in_specs = [a_spec, b_spec] + ([o_spec] if has_res else [])
    args = (a, b) + ((res,) if has_res else ())
    return pl.pallas_call(
        body,
        grid=(M // tm, N // tn, nk),
        in_specs=in_specs,
        out_specs=o_spec,
        out_shape=jax.ShapeDtypeStruct((M, N), out_dtype),
        scratch_shapes=[pltpu.VMEM((tm, tn), F32)] if nk > 1 else [],
        compiler_params=_cp("parallel", "parallel", "arbitrary"),
        name=name,
    )(*args)


def _rms_fwd(h, g, name="rms_fwd", tm=512):
    T, Dm = h.shape
    tm = min(tm, T)

    def body(h_ref, g_ref, o_ref):
        x = h_ref[...]
        r = lax.rsqrt(jnp.mean(x * x, axis=-1, keepdims=True) + EPS)
        o_ref[...] = (x * r * g_ref[...]).astype(o_ref.dtype)

    return pl.pallas_call(
        body,
        grid=(T // tm,),
        in_specs=[pl.BlockSpec((tm, Dm), lambda i: (i, 0)), pl.BlockSpec((1, Dm), lambda i: (0, 0))],
        out_specs=pl.BlockSpec((tm, Dm), lambda i: (i, 0)),
        out_shape=jax.ShapeDtypeStruct((T, Dm), BF16),
        compiler_params=_cp("parallel"),
        name=name,
    )(h, g.reshape(1, Dm))


def _rms_bwd(dhn, h, g, dres, name="rms_bwd", tm=512):
    T, Dm = h.shape
    tm = min(tm, T)

    def body(dhn_ref, h_ref, g_ref, dres_ref, dh_ref, dg_ref):
        i = pl.program_id(0)
        x = h_ref[...]
        r = lax.rsqrt(jnp.mean(x * x, axis=-1, keepdims=True) + EPS)
        xhat = x * r
        d = dhn_ref[...].astype(F32)

        @pl.when(i == 0)
        def _():
            dg_ref[...] = jnp.zeros_like(dg_ref)

        dg_ref[...] += jnp.sum(d * xhat, axis=0, keepdims=True)
        dx = d * g_ref[...]
        m = jnp.mean(dx * xhat, axis=-1, keepdims=True)
        dh_ref[...] = dres_ref[...] + r * (dx - xhat * m)

    row = pl.BlockSpec((tm, Dm), lambda i: (i, 0))
    vec = pl.BlockSpec((1, Dm), lambda i: (0, 0))
    return pl.pallas_call(
        body,
        grid=(T // tm,),
        in_specs=[row, row, vec, row],
        out_specs=[row, vec],
        out_shape=[jax.ShapeDtypeStruct((T, Dm), F32), jax.ShapeDtypeStruct((1, Dm), F32)],
        compiler_params=_cp("arbitrary"),
        name=name,
    )(dhn, h, g.reshape(1, Dm), dres)


def _final_loss(h, g, target, name="final_loss", tm=512):
    T, Dm = h.shape
    tm = min(tm, T)

    def body(h_ref, g_ref, t_ref, loss_ref, dh_ref, dg_ref):
        i = pl.program_id(0)
        x = h_ref[...]
        gv = g_ref[...]
        r = lax.rsqrt(jnp.mean(x * x, axis=-1, keepdims=True) + EPS)
        xhat = x * r
        e = xhat * gv - t_ref[...]

        @pl.when(i == 0)
        def _():
            loss_ref[...] = jnp.zeros_like(loss_ref)
            dg_ref[...] = jnp.zeros_like(dg_ref)

        loss_ref[...] += jnp.full((1, 128), 0.5 / Dm, F32) * jnp.sum(e * e)
        dy = e * (1.0 / Dm)
        dg_ref[...] += jnp.sum(dy * xhat, axis=0, keepdims=True)
        dx = dy * gv
        m = jnp.mean(dx * xhat, axis=-1, keepdims=True)
        dh_ref[...] = r * (dx - xhat * m)

    row = pl.BlockSpec((tm, Dm), lambda i: (i, 0))
    vec = pl.BlockSpec((1, Dm), lambda i: (0, 0))
    return pl.pallas_call(
        body,
        grid=(T // tm,),
        in_specs=[row, vec, row],
        out_specs=[pl.BlockSpec((1, 128), lambda i: (0, 0)), row, vec],
        out_shape=[jax.ShapeDtypeStruct((1, 128), F32), jax.ShapeDtypeStruct((T, Dm), F32), jax.ShapeDtypeStruct((1, Dm), F32)],
        compiler_params=_cp("arbitrary"),
        name=name,
    )(h, g.reshape(1, Dm), target)


HALO = 16


def _conv3(ext, w_ref, b_ref):
    return w_ref[0:1, :] * pltpu.roll(ext, 2, 0) + w_ref[1:2, :] * pltpu.roll(ext, 1, 0) + w_ref[2:3, :] * ext + b_ref[...]


def _ffn_mid_fwd(zg, zu, cw, cb, name="ffn_mid_fwd", tm=512, tc=256):
    T, Fh = zg.shape
    tm = min(tm, T)
    nj = Fh // tc
    hb = tm // HALO

    def body(zg_ref, zu_ref, zgp_ref, zup_ref, wg_ref, wu_ref, bg_ref, bu_ref, a_ref):
        i = pl.program_id(1)
        first = i == 0

        def conv(z_ref, zp_ref, w_ref, b_ref):
            prev = jnp.where(first, 0.0, zp_ref[...].astype(F32))
            ext = jnp.concatenate([prev, z_ref[...].astype(F32)], axis=0)
            return _conv3(ext, w_ref, b_ref)[HALO:]

        cg = conv(zg_ref, zgp_ref, wg_ref, bg_ref)
        cu = conv(zu_ref, zup_ref, wu_ref, bu_ref)
        a_ref[...] = (_gelu(cg) * cu).astype(a_ref.dtype)

    cur = pl.BlockSpec((tm, tc), lambda j, i: (i, j))
    prev = pl.BlockSpec((HALO, tc), lambda j, i: (jnp.maximum(i * hb - 1, 0), j))
    wg = pl.BlockSpec((3, tc), lambda j, i: (0, j))
    wu = pl.BlockSpec((3, tc), lambda j, i: (0, j + nj))
    bg = pl.BlockSpec((1, tc), lambda j, i: (0, j))
    bu = pl.BlockSpec((1, tc), lambda j, i: (0, j + nj))
    return pl.pallas_call(
        body,
        grid=(nj, T // tm),
        in_specs=[cur, cur, prev, prev, wg, wu, bg, bu],
        out_specs=cur,
        out_shape=jax.ShapeDtypeStruct((T, Fh), BF16),
        compiler_params=_cp("parallel", "parallel"),
        name=name,
    )(zg, zu, zg, zu, cw, cw, cb, cb)


def _ffn_mid_bwd(zg, zu, da, cw, cb, name="ffn_mid_bwd", tm=512, tc=256):
    T, Fh = zg.shape
    tm = min(tm, T)
    nj = Fh // tc
    hb = tm // HALO
    nhb = T // HALO

    def body(zg_ref, zu_ref, zgp_ref, zup_ref, zgn_ref, zun_ref, da_ref, dan_ref, wg_ref, wu_ref, bg_ref, bu_ref,
             dzg_ref, dzu_ref, dwg_ref, dwu_ref, dbg_ref, dbu_ref):
        i = pl.program_id(1)
        first = i == 0
        last = i == T // tm - 1

        @pl.when(first)
        def _():
            dwg_ref[...] = jnp.zeros_like(dwg_ref)
            dwu_ref[...] = jnp.zeros_like(dwu_ref)
            dbg_ref[...] = jnp.zeros_like(dbg_ref)
            dbu_ref[...] = jnp.zeros_like(dbu_ref)

        def ext_of(p_ref, c_ref, n_ref):
            p = jnp.where(first, 0.0, p_ref[...].astype(F32))
            return jnp.concatenate([p, c_ref[...].astype(F32), n_ref[...].astype(F32)], axis=0)

        zge = ext_of(zgp_ref, zg_ref, zgn_ref)
        zue = ext_of(zup_ref, zu_ref, zun_ref)
        dan = jnp.where(last, 0.0, dan_ref[...].astype(F32))
        dae = jnp.concatenate([jnp.zeros((HALO, tc), F32), da_ref[...].astype(F32), dan], axis=0)
        cg = _conv3(zge, wg_ref, bg_ref)
        cu = _conv3(zue, wu_ref, bu_ref)
        gel, dgel = _gelu_and_grad(cg)
        dcg = dae * cu * dgel
        dcu = dae * gel
        lo, hi = HALO, HALO + tm

        def back(dc, ze, w_ref, dz_ref, dw_ref, db_ref):
            n = dc.shape[0]
            dz = w_ref[2:3, :] * dc + w_ref[1:2, :] * pltpu.roll(dc, n - 1, 0) + w_ref[0:1, :] * pltpu.roll(dc, n - 2, 0)
            dz_ref[...] = dz[lo:hi].astype(dz_ref.dtype)
            dcc = dc[lo:hi]
            db_ref[...] += jnp.sum(dcc, axis=0, keepdims=True)
            dw_ref[0:1, :] += jnp.sum(dcc * pltpu.roll(ze, 2, 0)[lo:hi], axis=0, keepdims=True)
            dw_ref[1:2, :] += jnp.sum(dcc * pltpu.roll(ze, 1, 0)[lo:hi], axis=0, keepdims=True)
            dw_ref[2:3, :] += jnp.sum(dcc * ze[lo:hi], axis=0, keepdims=True)

        back(dcg, zge, wg_ref, dzg_ref, dwg_ref, dbg_ref)
        back(dcu, zue, wu_ref, dzu_ref, dwu_ref, dbu_ref)

    cur = pl.BlockSpec((tm, tc), lambda j, i: (i, j))
    prev = pl.BlockSpec((HALO, tc), lambda j, i: (jnp.maximum(i * hb - 1, 0), j))
    nxt = pl.BlockSpec((HALO, tc), lambda j, i: (jnp.minimum((i + 1) * hb, nhb - 1), j))
    wg = pl.BlockSpec((3, tc), lambda j, i: (0, j))
    wu = pl.BlockSpec((3, tc), lambda j, i: (0, j + nj))
    bg = pl.BlockSpec((1, tc), lambda j, i: (0, j))
    bu = pl.BlockSpec((1, tc), lambda j, i: (0, j + nj))
    dw = pl.BlockSpec((3, tc), lambda j, i: (0, j))
    db = pl.BlockSpec((1, tc), lambda j, i: (0, j))
    dzg, dzu, dwg, dwu, dbg, dbu = pl.pallas_call(
        body,
        grid=(nj, T // tm),
        in_specs=[cur, cur, prev, prev, nxt, nxt, cur, nxt, wg, wu, bg, bu],
        out_specs=[cur, cur, dw, dw, db, db],
        out_shape=[jax.ShapeDtypeStruct((T, Fh), BF16), jax.ShapeDtypeStruct((T, Fh), BF16),
                   jax.ShapeDtypeStruct((3, Fh), F32), jax.ShapeDtypeStruct((3, Fh), F32),
                   jax.ShapeDtypeStruct((1, Fh), F32), jax.ShapeDtypeStruct((1, Fh), F32)],
        compiler_params=_cp("parallel", "arbitrary"),
        name=name,
    )(zg, zu, zg, zu, zg, zu, da, da, cw, cw, cb, cb)
    return dzg, dzu, jnp.concatenate([dwg, dwu], axis=1), jnp.concatenate([dbg, dbu], axis=1)


def _sgu_mask():
    r = lax.broadcasted_iota(jnp.int32, (SGU_BLOCK, SGU_BLOCK), 0)
    c = lax.broadcasted_iota(jnp.int32, (SGU_BLOCK, SGU_BLOCK), 1)
    return (c < CHUNK) | (r >= CHUNK)


def _sgu_fwd(zz, ln_g, ln_b, w_s, b_s, name="sgu_fwd", tm=256):
    T = zz.shape[0]
    tm = min(tm, T)
    W = SGU_WIDTH

    def body(zu_ref, zv_ref, g_ref, b_ref, ws_ref, bs_ref, y_ref):
        u = _gelu(zu_ref[...].astype(F32))
        v = _gelu(zv_ref[...].astype(F32))
        mu = jnp.mean(v, axis=-1, keepdims=True)
        xc = v - mu
        rstd = lax.rsqrt(jnp.mean(xc * xc, axis=-1, keepdims=True) + EPS)
        vn = (xc * rstd * g_ref[...] + b_ref[...]).astype(BF16)
        mask = _sgu_mask()
        for g in range(SGU_GROUPS):
            wm = jnp.where(mask, ws_ref[g], 0.0).astype(BF16)
            cs = slice(g * SGU_GW, (g + 1) * SGU_GW)
            for blk in range(tm // SGU_BLOCK):
                rs = slice(blk * SGU_BLOCK, (blk + 1) * SGU_BLOCK)
                mixed = _dot(wm, vn[rs, cs], NN) + bs_ref[g]
                y_ref[rs, cs] = (u[rs, cs] * mixed).astype(y_ref.dtype)

    return pl.pallas_call(
        body,
        grid=(T // tm,),
        in_specs=[pl.BlockSpec((tm, W), lambda i: (i, 0)), pl.BlockSpec((tm, W), lambda i: (i, 1)),
                  pl.BlockSpec((1, W), lambda i: (0, 0)), pl.BlockSpec((1, W), lambda i: (0, 0)),
                  pl.BlockSpec((SGU_GROUPS, SGU_BLOCK, SGU_BLOCK), lambda i: (0, 0, 0)),
                  pl.BlockSpec((SGU_GROUPS, SGU_BLOCK, 1), lambda i: (0, 0, 0))],
        out_specs=pl.BlockSpec((tm, W), lambda i: (i, 0)),
        out_shape=jax.ShapeDtypeStruct((T, W), BF16),
        compiler_params=_cp("parallel"),
        name=name,
    )(zz, zz, ln_g.reshape(1, W), ln_b.reshape(1, W), w_s, b_s.reshape(SGU_GROUPS, SGU_BLOCK, 1))


def _sgu_bwd(zz, dy, ln_g, ln_b, w_s, b_s, name="sgu_bwd", tm=256):
    T = zz.shape[0]
    tm = min(tm, T)
    W = SGU_WIDTH

    def body(zu_ref, zv_ref, dy_ref, g_ref, b_ref, ws_ref, bs_ref, dzz_ref, dws_ref, dbs_ref, dg_ref, db_ref, dvn_ref):
        i = pl.program_id(0)

        @pl.when(i == 0)
        def _():
            dws_ref[...] = jnp.zeros_like(dws_ref)
            dbs_ref[...] = jnp.zeros_like(dbs_ref)
            dg_ref[...] = jnp.zeros_like(dg_ref)
            db_ref[...] = jnp.zeros_like(db_ref)

        u, du_dz = _gelu_and_grad(zu_ref[...].astype(F32))
        v, dv_dz = _gelu_and_grad(zv_ref[...].astype(F32))
        mu = jnp.mean(v, axis=-1, keepdims=True)
        xc = v - mu
        rstd = lax.rsqrt(jnp.mean(xc * xc, axis=-1, keepdims=True) + EPS)
        xhat = xc * rstd
        gv = g_ref[...]
        vn = (xhat * gv + b_ref[...]).astype(BF16)
        dyv = dy_ref[...].astype(F32)
        mask = _sgu_mask()
        for g in range(SGU_GROUPS):
            wm = jnp.where(mask, ws_ref[g], 0.0).astype(BF16)
            cs = slice(g * SGU_GW, (g + 1) * SGU_GW)
            dw_acc = jnp.zeros((SGU_BLOCK, SGU_BLOCK), F32)
            db_acc = jnp.zeros((SGU_BLOCK, 1), F32)
            for blk in range(tm // SGU_BLOCK):
                rs = slice(blk * SGU_BLOCK, (blk + 1) * SGU_BLOCK)
                vn_bg = vn[rs, cs]
                mixed = _dot(wm, vn_bg, NN) + bs_ref[g]
                dy_bg = dyv[rs, cs]
                dmixed = dy_bg * u[rs, cs]
                dmb = dmixed.astype(BF16)
                dw_acc += _dot(dmb, vn_bg, NT)
                db_acc += jnp.sum(dmixed, axis=1, keepdims=True)
                dvn_ref[rs, cs] = _dot(wm, dmb, TN)
                dzz_ref[rs, cs] = (dy_bg * mixed * du_dz[rs, cs]).astype(dzz_ref.dtype)
            dws_ref[g] += jnp.where(mask, dw_acc, 0.0)
            dbs_ref[g] += db_acc
        dvn = dvn_ref[...]
        dg_ref[...] += jnp.sum(dvn * xhat, axis=0, keepdims=True)
        db_ref[...] += jnp.sum(dvn, axis=0, keepdims=True)
        dxh = dvn * gv
        m1 = jnp.mean(dxh, axis=-1, keepdims=True)
        m2 = jnp.mean(dxh * xhat, axis=-1, keepdims=True)
        dv = rstd * (dxh - m1 - xhat * m2)
        dzz_ref[:, W:] = (dv * dv_dz).astype(dzz_ref.dtype)

    vec = pl.BlockSpec((1, W), lambda i: (0, 0))
    ws_spec = pl.BlockSpec((SGU_GROUPS, SGU_BLOCK, SGU_BLOCK), lambda i: (0, 0, 0))
    bs_spec = pl.BlockSpec((SGU_GROUPS, SGU_BLOCK, 1), lambda i: (0, 0, 0))
    return pl.pallas_call(
        body,
        grid=(T // tm,),
        in_specs=[pl.BlockSpec((tm, W), lambda i: (i, 0)), pl.BlockSpec((tm, W), lambda i: (i, 1)),
                  pl.BlockSpec((tm, W), lambda i: (i, 0)), vec, vec, ws_spec, bs_spec],
        out_specs=[pl.BlockSpec((tm, 2 * W), lambda i: (i, 0)), ws_spec, bs_spec, vec, vec],
        out_shape=[jax.ShapeDtypeStruct((T, 2 * W), BF16),
                   jax.ShapeDtypeStruct((SGU_GROUPS, SGU_BLOCK, SGU_BLOCK), F32),
                   jax.ShapeDtypeStruct((SGU_GROUPS, SGU_BLOCK, 1), F32),
                   jax.ShapeDtypeStruct((1, W), F32), jax.ShapeDtypeStruct((1, W), F32)],
        scratch_shapes=[pltpu.VMEM((tm, W), F32)],
        compiler_params=_cp("arbitrary"),
        name=name,
    )(zz, zz, dy, ln_g.reshape(1, W), ln_b.reshape(1, W), w_s, b_s.reshape(SGU_GROUPS, SGU_BLOCK, 1))


RET_TR = 256
QK_SCALE = RET_QK_DIM ** -0.5


def _ret_tables(T):
    half = RET_QK_DIM // 2
    inv = 1.0 / (10000.0 ** jnp.linspace(0.0, 1.0, half, dtype=F32))
    ang = jnp.arange(T).astype(F32)[:, None] * inv[None, :]
    cos, sin = jnp.cos(ang), jnp.sin(ang)
    rot_c = jnp.concatenate([cos, cos], axis=1)
    rot_s = jnp.concatenate([-sin, sin], axis=1)
    log_g = jnp.log1p(-jnp.exp2(-5.0 - jnp.arange(RET_HEADS, dtype=F32)))
    idx = jnp.arange(CHUNK, dtype=F32)
    d_intra = jnp.exp(log_g[:, None, None] * jnp.abs(idx[:, None] - idx[None, :]))
    k_dec = jnp.exp(log_g[:, None] * (CHUNK - 1 - idx)[None, :])[:, :, None]
    q_dec = jnp.exp(log_g[:, None] * (idx + 1.0)[None, :])[:, :, None]
    c_dec = jnp.exp(log_g * CHUNK)[:, None, None]
    return rot_c, rot_s, d_intra, q_dec, k_dec, c_dec


def _rot(x, c, s):
    return x * c + pltpu.roll(x, RET_QK_DIM // 2, 1) * s


def _ret_specs(tr, rev, nb):
    ix = (lambda n: nb - 1 - n) if rev else (lambda n: n)
    tab = pl.BlockSpec((tr, RET_QK_DIM), lambda n: (ix(n), 0))
    dm = pl.BlockSpec((RET_HEADS, CHUNK, CHUNK), lambda n: (0, 0, 0))
    dv = pl.BlockSpec((RET_HEADS, CHUNK, 1), lambda n: (0, 0, 0))
    dc = pl.BlockSpec((RET_HEADS, 1, 1), lambda n: (0, 0, 0))
    return ix, [tab, tab, dm, dv, dv, dc]


def _ret_fwd(z_a, tables, name="ret_fwd"):
    T = z_a.shape[0]
    tr = min(RET_TR, T)
    cpb = tr // CHUNK
    nb = T // tr
    QW, VW = RET_HEADS * RET_QK_DIM, RET_HEADS * RET_V_DIM
    ix, tab_specs = _ret_specs(tr, False, nb)

    def body(z_ref, c_ref, s_ref, dm_ref, qd_ref, kd_ref, cd_ref, y_ref, st_ref, state):
        @pl.when(pl.program_id(0) == 0)
        def _():
            state[...] = jnp.zeros_like(state)

        for h in range(RET_HEADS):
            for c in range(cpb):
                rs = slice(c * CHUNK, (c + 1) * CHUNK)
                cc, ss = c_ref[rs, :], s_ref[rs, :]
                q = z_ref[rs, h * RET_QK_DIM:(h + 1) * RET_QK_DIM].astype(F32)
                k = z_ref[rs, QW + h * RET_QK_DIM:QW + (h + 1) * RET_QK_DIM].astype(F32)
                v = z_ref[rs, 2 * QW + h * RET_V_DIM:2 * QW + (h + 1) * RET_V_DIM]
                gt = z_ref[rs, 2 * QW + VW + h * RET_V_DIM:2 * QW + VW + (h + 1) * RET_V_DIM].astype(F32)
                qr = _rot(q, cc, ss)
                kr = _rot(k, cc, ss) * QK_SCALE
                s_old = state[h]
                sb = s_old.astype(BF16)
                st_ref[c, h] = sb
                s = _dot(qr.astype(BF16), kr.astype(BF16), NT) * dm_ref[h]
                o = _dot(s.astype(BF16), v, NN) + _dot((qr * qd_ref[h]).astype(BF16), sb, NN)
                state[h] = s_old * cd_ref[h] + _dot((kr * kd_ref[h]).astype(BF16), v, TN)
                mu = jnp.mean(o, axis=-1, keepdims=True)
                oc = o - mu
                rn = oc * lax.rsqrt(jnp.mean(oc * oc, axis=-1, keepdims=True) + EPS)
                silu = gt / (1.0 + jnp.exp(-gt))
                y_ref[rs, h * RET_V_DIM:(h + 1) * RET_V_DIM] = (silu * rn).astype(y_ref.dtype)

    return pl.pallas_call(
        body,
        grid=(nb,),
        in_specs=[pl.BlockSpec((tr, RET_W), lambda n: (n, 0))] + tab_specs,
        out_specs=[pl.BlockSpec((tr, VW), lambda n: (n, 0)),
                   pl.BlockSpec((cpb, RET_HEADS, RET_QK_DIM, RET_V_DIM), lambda n: (n, 0, 0, 0))],
        out_shape=[jax.ShapeDtypeStruct((T, VW), BF16),
                   jax.ShapeDtypeStruct((T // CHUNK, RET_HEADS, RET_QK_DIM, RET_V_DIM), BF16)],
        scratch_shapes=[pltpu.VMEM((RET_HEADS, RET_QK_DIM, RET_V_DIM), F32)],
        compiler_params=_cp("arbitrary"),
        name=name,
    )(z_a, *tables)


def _ret_bwd(z_a, dy, states, tables, name="ret_bwd"):
    T = z_a.shape[0]
    tr = min(RET_TR, T)
    cpb = tr // CHUNK
    nb = T // tr
    QW, VW = RET_HEADS * RET_QK_DIM, RET_HEADS * RET_V_DIM
    ix, tab_specs = _ret_specs(tr, True, nb)

    def body(z_ref, dy_ref, st_ref, c_ref, s_ref, dm_ref, qd_ref, kd_ref, cd_ref, dz_ref, dstate):
        @pl.when(pl.program_id(0) == 0)
        def _():
            dstate[...] = jnp.zeros_like(dstate)

        for h in range(RET_HEADS):
            for c in reversed(range(cpb)):
                rs = slice(c * CHUNK, (c + 1) * CHUNK)
                cc, ss = c_ref[rs, :], s_ref[rs, :]
                q = z_ref[rs, h * RET_QK_DIM:(h + 1) * RET_QK_DIM].astype(F32)
                k = z_ref[rs, QW + h * RET_QK_DIM:QW + (h + 1) * RET_QK_DIM].astype(F32)
                v = z_ref[rs, 2 * QW + h * RET_V_DIM:2 * QW + (h + 1) * RET_V_DIM]
                gt = z_ref[rs, 2 * QW + VW + h * RET_V_DIM:2 * QW + VW + (h + 1) * RET_V_DIM].astype(F32)
                dyv = dy_ref[rs, h * RET_V_DIM:(h + 1) * RET_V_DIM].astype(F32)
                dmat, qd, kd = dm_ref[h], qd_ref[h], kd_ref[h]
                qr = _rot(q, cc, ss)
                kr = _rot(k, cc, ss) * QK_SCALE
                qrb, krb = qr.astype(BF16), kr.astype(BF16)
                sb = st_ref[c, h]
                sd = (_dot(qrb, krb, NT) * dmat).astype(BF16)
                qdb = (qr * qd).astype(BF16)
                kdb = (kr * kd).astype(BF16)
                o = _dot(sd, v, NN) + _dot(qdb, sb, NN)
                mu = jnp.mean(o, axis=-1, keepdims=True)
                oc = o - mu
                rstd = lax.rsqrt(jnp.mean(oc * oc, axis=-1, keepdims=True) + EPS)
                rn = oc * rstd
                sg = 1.0 / (1.0 + jnp.exp(-gt))
                dgt = dyv * rn * (sg * (1.0 + gt * (1.0 - sg)))
                drn = dyv * (gt * sg)
                do = rstd * (drn - jnp.mean(drn, axis=-1, keepdims=True) - rn * jnp.mean(drn * rn, axis=-1, keepdims=True))
                dob = do.astype(BF16)
                dsn = dstate[h]
                dsnb = dsn.astype(BF16)
                ds_raw = (_dot(dob, v, NT) * dmat).astype(BF16)
                dv = _dot(sd, dob, TN) + _dot(kdb, dsnb, NN)
                dqr = _dot(ds_raw, krb, NN) + qd * _dot(dob, sb, NT)
                dkr = (_dot(ds_raw, qrb, TN) + kd * _dot(v, dsnb, NT)) * QK_SCALE
                dstate[h] = dsn * cd_ref[h] + _dot(qdb, dob, TN)
                dq = dqr * cc + pltpu.roll(dqr * ss, RET_QK_DIM // 2, 1)
                dk = dkr * cc + pltpu.roll(dkr * ss, RET_QK_DIM // 2, 1)
                dz_ref[rs, h * RET_QK_DIM:(h + 1) * RET_QK_DIM] = dq.astype(dz_ref.dtype)
                dz_ref[rs, QW + h * RET_QK_DIM:QW + (h + 1) * RET_QK_DIM] = dk.astype(dz_ref.dtype)
                dz_ref[rs, 2 * QW + h * RET_V_DIM:2 * QW + (h + 1) * RET_V_DIM] = dv.astype(dz_ref.dtype)
                dz_ref[rs, 2 * QW + VW + h * RET_V_DIM:2 * QW + VW + (h + 1) * RET_V_DIM] = dgt.astype(dz_ref.dtype)

    return pl.pallas_call(
        body,
        grid=(nb,),
        in_specs=[pl.BlockSpec((tr, RET_W), lambda n: (ix(n), 0)),
                  pl.BlockSpec((tr, VW), lambda n: (ix(n), 0)),
                  pl.BlockSpec((cpb, RET_HEADS, RET_QK_DIM, RET_V_DIM), lambda n: (ix(n), 0, 0, 0))] + tab_specs,
        out_specs=pl.BlockSpec((tr, RET_W), lambda n: (ix(n), 0)),
        out_shape=jax.ShapeDtypeStruct((T, RET_W), BF16),
        scratch_shapes=[pltpu.VMEM((RET_HEADS, RET_QK_DIM, RET_V_DIM), F32)],
        compiler_params=_cp("arbitrary"),
        name=name,
    )(z_a, dy, states, *tables)


ATT_TQ = 256
ATT_CPB = ATT_TQ // CHUNK
ATT_SCALE = ATT_HEAD_DIM ** -0.5


def _rel_index():
    i = np.arange(CHUNK)[:, None]
    j = np.arange(ATT_BAND)[None, :]
    rel = np.clip(i + ATT_PAST * CHUNK - j, -MAX_REL, MAX_REL) + MAX_REL
    return jnp.asarray(rel.reshape(1, CHUNK * ATT_BAND).astype(np.int32))


def _split3(x):
    hi = x.astype(BF16)
    r1 = x - hi.astype(F32)
    mid = r1.astype(BF16)
    lo = (r1 - mid.astype(F32)).astype(BF16)
    return hi, mid, lo


REL_TILE = 4608


def _bias_expand(rel_bias, name="bias_expand"):
    H = rel_bias.shape[0]
    n = CHUNK * ATT_BAND
    padded = jnp.pad(rel_bias, ((0, 0), (0, N_REL_PAD - N_REL)))

    def body(rb_ref, idx_ref, o_ref):
        onehot = (lax.broadcasted_iota(jnp.int32, (N_REL_PAD, REL_TILE), 0) == idx_ref[...]).astype(BF16)
        hi, mid, lo = _split3(rb_ref[...])
        o_ref[...] = _dot(hi, onehot, NN) + _dot(mid, onehot, NN) + _dot(lo, onehot, NN)

    out = pl.pallas_call(
        body,
        grid=(n // REL_TILE,),
        in_specs=[pl.BlockSpec((H, N_REL_PAD), lambda t: (0, 0)), pl.BlockSpec((1, REL_TILE), lambda t: (0, t))],
        out_specs=pl.BlockSpec((H, REL_TILE), lambda t: (0, t)),
        out_shape=jax.ShapeDtypeStruct((H, n), F32),
        compiler_params=_cp("parallel"),
        name=name,
    )(padded, _rel_index())
    return out.reshape(H, CHUNK, ATT_BAND)


def _bias_reduce(dbias, name="bias_reduce"):
    H = dbias.shape[0]
    n = CHUNK * ATT_BAND

    def body(db_ref, idx_ref, o_ref):
        @pl.when(pl.program_id(0) == 0)
        def _():
            o_ref[...] = jnp.zeros_like(o_ref)

        onehot = (lax.broadcasted_iota(jnp.int32, (N_REL_PAD, REL_TILE), 0) == idx_ref[...]).astype(BF16)
        hi, mid, lo = _split3(db_ref[...])
        o_ref[...] += _dot(hi, onehot, NT) + _dot(mid, onehot, NT) + _dot(lo, onehot, NT)

    out = pl.pallas_call(
        body,
        grid=(n // REL_TILE,),
        in_specs=[pl.BlockSpec((H, REL_TILE), lambda t: (0, t)), pl.BlockSpec((1, REL_TILE), lambda t: (0, t))],
        out_specs=pl.BlockSpec((H, N_REL_PAD), lambda t: (0, 0)),
        out_shape=jax.ShapeDtypeStruct((H, N_REL_PAD), F32),
        compiler_params=_cp("arbitrary"),
        name=name,
    )(dbias.reshape(H, n), _rel_index())
    return out[:, :N_REL]


def _att_probs(q, kband, bias, key_row0):
    s = _dot(q, kband, NT) * ATT_SCALE + bias
    col = lax.broadcasted_iota(jnp.int32, (CHUNK, ATT_BAND), 1)
    s = jnp.where(col + key_row0 >= 0, s, NEG_INF)
    e = jnp.exp(s - jnp.max(s, axis=-1, keepdims=True))
    return e / jnp.sum(e, axis=-1, keepdims=True)


def _att_specs(tq, nq, clip_q):
    qi = (lambda h, m: (h, jnp.minimum(m, nq - 1), 0)) if clip_q else (lambda h, m: (h, m, 0))
    q = pl.BlockSpec((None, tq, ATT_HEAD_DIM), qi)
    win = [pl.BlockSpec((None, tq, ATT_HEAD_DIM), functools.partial(lambda h, m, back: (h, jnp.clip(m - back, 0, nq - 1), 0), back=b))
           for b in (2, 1, 0)]
    bias = pl.BlockSpec((None, CHUNK, ATT_BAND), lambda h, m: (h, 0, 0))
    return q, win, bias


def _att_fwd(q, k, v, bias, name="att_fwd"):
    H, T, d = q.shape
    tq = ATT_TQ
    nq = T // tq
    qs, win, bs = _att_specs(tq, nq, False)

    def body(q_ref, k0, k1, k2, v0, v1, v2, b_ref, o_ref):
        m = pl.program_id(1)
        kw = jnp.concatenate([k0[...], k1[...], k2[...]], axis=0)
        vw = jnp.concatenate([v0[...], v1[...], v2[...]], axis=0)
        bias_v = b_ref[...]
        for a in range(ATT_CPB):
            ws = slice(a * CHUNK, a * CHUNK + ATT_BAND)
            p = _att_probs(q_ref[a * CHUNK:(a + 1) * CHUNK, :], kw[ws], bias_v, (m - 2) * tq + a * CHUNK)
            o_ref[a * CHUNK:(a + 1) * CHUNK, :] = _dot(p.astype(BF16), vw[ws], NN).astype(o_ref.dtype)

    return pl.pallas_call(
        body,
        grid=(H, nq),
        in_specs=[qs] + win + win + [bs],
        out_specs=qs,
        out_shape=jax.ShapeDtypeStruct((H, T, d), BF16),
        compiler_params=_cp("parallel", "parallel"),
        name=name,
    )(q, k, k, k, v, v, v, bias)


def _att_bwd(q, k, v, bias, do, name="att_bwd"):
    H, T, d = q.shape
    tq = ATT_TQ
    nq = T // tq
    qs, win, bs = _att_specs(tq, nq, True)
    kv_out = pl.BlockSpec((None, tq, d), lambda h, m: (h, jnp.maximum(m - 2, 0), 0))
    W3 = 3 * tq

    def body(q_ref, k0, k1, k2, v0, v1, v2, b_ref, do_ref, dq_ref, dk_ref, dv_ref, db_ref, dkc, dvc, dkw, dvw):
        m = pl.program_id(1)

        @pl.when(m == 0)
        def _():
            dkc[...] = jnp.zeros_like(dkc)
            dvc[...] = jnp.zeros_like(dvc)
            db_ref[...] = jnp.zeros_like(db_ref)

        dkw[...] = jnp.zeros_like(dkw)
        dvw[...] = jnp.zeros_like(dvw)

        @pl.when(m < nq)
        def _():
            kw = jnp.concatenate([k0[...], k1[...], k2[...]], axis=0)
            vw = jnp.concatenate([v0[...], v1[...], v2[...]], axis=0)
            bias_v = b_ref[...]
            dbias = jnp.zeros((CHUNK, ATT_BAND), F32)
            for a in range(ATT_CPB):
                ws = slice(a * CHUNK, a * CHUNK + ATT_BAND)
                qa = q_ref[a * CHUNK:(a + 1) * CHUNK, :]
                doa = do_ref[a * CHUNK:(a + 1) * CHUNK, :]
                kb, vb = kw[ws], vw[ws]
                p = _att_probs(qa, kb, bias_v, (m - 2) * tq + a * CHUNK)
                dp = _dot(doa, vb, NT)
                ds = p * (dp - jnp.sum(dp * p, axis=-1, keepdims=True))
                dbias += ds
                dsb = ds.astype(BF16)
                dq_ref[a * CHUNK:(a + 1) * CHUNK, :] = (_dot(dsb, kb, NN) * ATT_SCALE).astype(dq_ref.dtype)
                dkw[ws, :] += _dot(dsb, qa, TN) * ATT_SCALE
                dvw[ws, :] += _dot(p.astype(BF16), doa, TN)
            db_ref[...] += dbias

        dk_ref[...] = (dkc[0:tq, :] + dkw[0:tq, :]).astype(dk_ref.dtype)
        dv_ref[...] = (dvc[0:tq, :] + dvw[0:tq, :]).astype(dv_ref.dtype)
        dkc[0:tq, :] = dkc[tq:2 * tq, :] + dkw[tq:2 * tq, :]
        dvc[0:tq, :] = dvc[tq:2 * tq, :] + dvw[tq:2 * tq, :]
        dkc[tq:2 * tq, :] = dkw[2 * tq:W3, :]
        dvc[tq:2 * tq, :] = dvw[2 * tq:W3, :]

    hd = jax.ShapeDtypeStruct((H, T, d), BF16)
    return pl.pallas_call(
        body,
        grid=(H, nq + 2),
        in_specs=[qs] + win + win + [bs, qs],
        out_specs=[qs, kv_out, kv_out, bs],
        out_shape=[hd, hd, hd, jax.ShapeDtypeStruct((H, CHUNK, ATT_BAND), F32)],
        scratch_shapes=[pltpu.VMEM((2 * tq, d), F32), pltpu.VMEM((2 * tq, d), F32),
                        pltpu.VMEM((W3, d), F32), pltpu.VMEM((W3, d), F32)],
        compiler_params=_cp("parallel", "arbitrary"),
        name=name,
    )(q, k, k, k, v, v, v, bias, do)


def _to_heads(z):
    T, w = z.shape
    n = w // (ATT_HEADS * ATT_HEAD_DIM)
    return z.reshape(T, n, ATT_HEADS, ATT_HEAD_DIM).transpose(1, 2, 0, 3)


def _from_heads(x):
    n, H, T, d = x.shape
    return x.transpose(2, 0, 1, 3).reshape(T, n * H * d)


def _local_step(x, target, wt, small):
    T = x.shape[0]
    Fh = FFN_HIDDEN
    tables = _ret_tables(T)
    gw, gs = {}, {}

    hn0 = _rms_fwd(x, small["attn_norm_g"][0], name="rms_fwd")
    z_a = _mm(hn0, wt["ab_in_t"][:RET_W], "nt", BF16, name="mm_ab_in_a")
    z_b = _mm(hn0, wt["ab_in_t"][RET_W:], "nt", BF16, name="mm_ab_in_b")
    y_a, states = _ret_fwd(z_a, tables)
    qkv = _to_heads(z_b)
    bias = _bias_expand(small["rel_bias"])
    o_b = _att_fwd(qkv[0], qkv[1], qkv[2], bias)
    y = jnp.concatenate([y_a, _from_heads(o_b[None])], axis=1)
    h1 = _mm(y, wt["ab_out"], "nn", F32, res=x, name="mm_ab_out")

    def ffn_fwd(h, layer):
        hf = _rms_fwd(h, small["ffn_norm_g"][layer], name="rms_fwd")
        zg = _mm(hf, wt["up_t"][layer][:Fh], "nt", BF16, name="mm_up")
        zu = _mm(hf, wt["up_t"][layer][Fh:], "nt", BF16, name="mm_up")
        a = _ffn_mid_fwd(zg, zu, small["conv_w"][layer], small["conv_b"][layer][None, :])
        h_out = _mm(a, wt["down"][layer], "nn", F32, res=h, name="mm_down")
        return h_out, (hf, zg, zu, a)

    def ffn_bwd(dh_out, h, layer, saved):
        hf, zg, zu, a = saved
        da = _mm(dh_out, wt["down"][layer], "nt", BF16, name="mm_d_a")
        d_down = _mm(a, dh_out, "tn", BF16, name="mm_dw_down")
        dzg, dzu, dcw, dcb = _ffn_mid_bwd(zg, zu, da, small["conv_w"][layer], small["conv_b"][layer][None, :])
        dhf = _mm(dzg, wt["up_t"][layer][:Fh], "nn", F32, name="mm_d_hf")
        dhf = _mm(dzu, wt["up_t"][layer][Fh:], "nn", F32, res=dhf, name="mm_d_hf_acc")
        d_up = jnp.concatenate([_mm(dzg, hf, "tn", BF16, name="mm_dw_up"), _mm(dzu, hf, "tn", BF16, name="mm_dw_up")], axis=0)
        dh, dg = _rms_bwd(dhf, h, small["ffn_norm_g"][layer], dh_out)
        return dh, dg, d_up, d_down, dcw, dcb

    h2, ffn0 = ffn_fwd(h1, 0)

    hn1 = _rms_fwd(h2, small["attn_norm_g"][1], name="rms_fwd")
    zz = _mm(hn1, wt["c_in_t"], "nt", BF16, name="mm_c_in")
    ys = _sgu_fwd(zz, small["ln_g"], small["ln_b"], small["w_s"], small["b_s"])
    h3 = _mm(ys, wt["c_out"], "nn", F32, res=h2, name="mm_c_out")
    h4, ffn1 = ffn_fwd(h3, 1)

    loss_vec, dh4, gs["final_g"] = _final_loss(h4, small["final_g"], target)

    dh3, dgf1, d_up1, d_down1, dcw1, dcb1 = ffn_bwd(dh4, h3, 1, ffn1)
    dys = _mm(dh3, wt["c_out"], "nt", BF16, name="mm_d_ys")
    gw["c_out"] = _mm(ys, dh3, "tn", BF16, name="mm_dw_c_out")
    dzz, gs["w_s"], dbs, dlg, dlb = _sgu_bwd(zz, dys, small["ln_g"], small["ln_b"], small["w_s"], small["b_s"])
    gs["b_s"], gs["ln_g"], gs["ln_b"] = dbs[:, :, 0], dlg[0], dlb[0]
    dhn1 = _mm(dzz, wt["c_in_t"], "nn", F32, name="mm_d_hn1")
    gw["c_in_t"] = _mm(dzz, hn1, "tn", BF16, name="mm_dw_c_in")
    dh2, dga1 = _rms_bwd(dhn1, h2, small["attn_norm_g"][1], dh3)

    dh1, dgf0, d_up0, d_down0, dcw0, dcb0 = ffn_bwd(dh2, h1, 0, ffn0)

    dy = _mm(dh1, wt["ab_out"], "nt", BF16, name="mm_d_y")
    gw["ab_out"] = _mm(y, dh1, "tn", BF16, name="mm_dw_ab_out")
    dz_a = _ret_bwd(z_a, dy, states, tables)
    do_b = _to_heads(dy[:, RET_HEADS * RET_V_DIM:])[0]
    dq, dk, dv, dbias = _att_bwd(qkv[0], qkv[1], qkv[2], bias, do_b)
    dz_b = _from_heads(jnp.stack([dq, dk, dv]))
    gs["rel_bias"] = _bias_reduce(dbias)
    dhn0 = _mm(dz_a, wt["ab_in_t"][:RET_W], "nn", F32, name="mm_d_hn0")
    dhn0 = _mm(dz_b, wt["ab_in_t"][RET_W:], "nn", F32, res=dhn0, name="mm_d_hn0_acc")
    gw["ab_in_t"] = jnp.concatenate([_mm(dz_a, hn0, "tn", BF16, name="mm_dw_ab_in_a"),
                                     _mm(dz_b, hn0, "tn", BF16, name="mm_dw_ab_in_b")], axis=0)
    grad_x, dga0 = _rms_bwd(dhn0, x, small["attn_norm_g"][0], dh1)

    gw["up_t"] = [d_up0, d_up1]
    gw["down"] = [d_down0, d_down1]
    gs["attn_norm_g"] = jnp.concatenate([dga0, dga1], axis=0)
    gs["ffn_norm_g"] = jnp.concatenate([dgf0, dgf1], axis=0)
    gs["conv_w"] = jnp.stack([dcw0, dcw1])
    gs["conv_b"] = jnp.concatenate([dcb0, dcb1], axis=0)
    gs["final_g"] = gs["final_g"][0]
    return loss_vec[0, 0], grad_x, gw, gs


MESH_ID = pl.DeviceIdType.MESH
ANY = pl.BlockSpec(memory_space=pl.ANY)


def _my_place():
    return lax.axis_index("x"), lax.axis_index("y"), lax.axis_index("c")


def _all_gather(shards, name="all_gather"):
    n = len(shards)

    def body(*refs):
        x_refs, o_refs = refs[:n], refs[n:2 * n]
        send_sems, recv_sems, local_sems = refs[2 * n:]
        x, y, c = _my_place()
        me, sibling = (x, y, c), (x, y, 1 - c)
        chips = [(1 - x, y), (x, 1 - y), (1 - x, 1 - y)]

        def rows(a, place):
            m = x_refs[a].shape[0]
            px, py, pc = place
            return o_refs[a].at[pl.ds((4 * px + 2 * py + pc) * m, m), :]

        def copy(a, k, block, to, own=False):
            return pltpu.make_async_remote_copy(
                src_ref=x_refs[a] if own else rows(a, block), dst_ref=rows(a, block),
                send_sem=send_sems.at[a, k], recv_sem=recv_sems.at[a, k], device_id=to, device_id_type=MESH_ID)

        mine = [pltpu.make_async_copy(x_refs[a], rows(a, me), local_sems.at[a]) for a in range(n)]
        for cp in mine:
            cp.start()
        first = []
        for a in range(n):
            first.append(copy(a, 0, me, sibling, own=True))
            first += [copy(a, 1 + j, me, (*chip, c), own=True) for j, chip in enumerate(chips)]
        for cp in first:
            cp.start()
        passed = []
        for j, chip in enumerate(chips):
            for a in range(n):
                copy(a, 1 + j, (*chip, c), me).wait_recv()
                fw = copy(a, 4 + j, (*chip, c), sibling)
                fw.start()
                passed.append(fw)
        for a in range(n):
            copy(a, 0, sibling, me).wait_recv()
            for j, chip in enumerate(chips):
                copy(a, 4 + j, (*chip, 1 - c), me).wait_recv()
        for cp in first + passed:
            cp.wait_send()
        for cp in mine:
            cp.wait()

    return pl.pallas_call(
        body,
        out_shape=[jax.ShapeDtypeStruct((N_DEV * s.shape[0], s.shape[1]), s.dtype) for s in shards],
        in_specs=[ANY] * n,
        out_specs=[ANY] * n,
        scratch_shapes=[pltpu.SemaphoreType.DMA((n, 7)), pltpu.SemaphoreType.DMA((n, 7)), pltpu.SemaphoreType.DMA((n,))],
        name=name,
    )(*shards)


def _all_to_all(fulls, name="all_to_all"):
    n = len(fulls)

    def body(*refs):
        g_refs, o_refs = refs[:n], refs[n:2 * n]
        send_sems, recv_sems, local_sems = refs[2 * n:]
        x, y, c = _my_place()
        me = 4 * x + 2 * y + c
        peers = [(x ^ ((k >> 2) & 1), y ^ ((k >> 1) & 1), c ^ (k & 1)) for k in range(1, N_DEV)]

        def block(a, idx):
            m = g_refs[a].shape[0] // N_DEV
            return g_refs[a].at[pl.ds(idx * m, m), :]

        def copy(a, k, slot):
            px, py, pc = peers[k]
            return pltpu.make_async_remote_copy(
                src_ref=block(a, 4 * px + 2 * py + pc), dst_ref=o_refs[a].at[slot],
                send_sem=send_sems.at[a, k], recv_sem=recv_sems.at[a, k], device_id=peers[k], device_id_type=MESH_ID)

        mine = [pltpu.make_async_copy(block(a, me), o_refs[a].at[me], local_sems.at[a]) for a in range(n)]
        for cp in mine:
            cp.start()
        sends = [copy(a, k, me) for k in range(N_DEV - 1) for a in range(n)]
        for cp in sends:
            cp.start()
        for k in range(N_DEV - 1):
            px, py, pc = peers[k]
            for a in range(n):
                copy(a, k, 4 * px + 2 * py + pc).wait_recv()
        for cp in sends:
            cp.wait_send()
        for cp in mine:
            cp.wait()

    return pl.pallas_call(
        body,
        out_shape=[jax.ShapeDtypeStruct((N_DEV, g.shape[0] // N_DEV, g.shape[1]), g.dtype) for g in fulls],
        in_specs=[ANY] * n,
        out_specs=[ANY] * n,
        scratch_shapes=[pltpu.SemaphoreType.DMA((n, 7)), pltpu.SemaphoreType.DMA((n, 7)), pltpu.SemaphoreType.DMA((n,))],
        name=name,
    )(*fulls)


def _row_tile(r, target=256):
    best = None
    for t in range(8, min(r, target) + 1, 8):
        if r % t == 0:
            best = t
    return best if best is not None else r


def _sum8(parts, name="sum8"):
    _, M, N = parts.shape
    tr = _row_tile(M, 128)

    def body(p_ref, o_ref):
        acc = p_ref[0].astype(F32)
        for d in range(1, N_DEV):
            acc = acc + p_ref[d].astype(F32)
        o_ref[...] = acc

    return pl.pallas_call(
        body,
        grid=(M // tr,),
        in_specs=[pl.BlockSpec((N_DEV, tr, N), lambda i: (0, i, 0))],
        out_specs=pl.BlockSpec((tr, N), lambda i: (i, 0)),
        out_shape=jax.ShapeDtypeStruct((M, N), F32),
        compiler_params=_cp("parallel"),
        name=name,
    )(parts)


def _adamw(w, g, m, v, name="adamw"):
    shape = w.shape
    if w.ndim == 1:
        r2 = (1, shape[0])
    else:
        r2 = (int(np.prod(shape[:-1])), shape[-1])
    R, C = r2
    tr = _row_tile(R)
    bc1 = 1.0 - ADAM_B1 ** ADAM_STEP
    bc2 = 1.0 - ADAM_B2 ** ADAM_STEP

    def body(w_ref, g_ref, m_ref, v_ref, d_ref, nm_ref, nv_ref):
        gv = g_ref[...]
        nm = ADAM_B1 * m_ref[...] + (1.0 - ADAM_B1) * gv
        nv = ADAM_B2 * v_ref[...] + (1.0 - ADAM_B2) * (gv * gv)
        d_ref[...] = -ADAM_LR * ((nm / bc1) / (jnp.sqrt(nv / bc2) + ADAM_EPS) + ADAM_WD * w_ref[...])
        nm_ref[...] = nm
        nv_ref[...] = nv

    spec = pl.BlockSpec((tr, C), lambda i: (i, 0))
    out = pl.pallas_call(
        body,
        grid=(R // tr,),
        in_specs=[spec] * 4,
        out_specs=[spec] * 3,
        out_shape=[jax.ShapeDtypeStruct(r2, F32)] * 3,
        compiler_params=_cp("parallel"),
        name=name,
    )(w.reshape(r2), g.reshape(r2), m.reshape(r2), v.reshape(r2))
    return [o.reshape(shape) for o in out]


WEIGHTS = ['attn_norm_g', 'ffn_norm_g', 'ab_w_in', 'ab_w_out', 'ab_rel_bias', 'c_w_in', 'c_ln_g', 'c_ln_b', 'c_w_s', 'c_b_s',
           'c_w_out', 'ffn_w_up', 'ffn_conv_w', 'ffn_conv_b', 'ffn_w_down', 'final_norm_g']
SMALL_ORDER = ["attn_norm_g", "ffn_norm_g", "rel_bias", "ln_g", "ln_b", "w_s", "b_s", "conv_w", "conv_b", "final_g"]
PACK_ROW = 1024


def _pack(arrs):
    flat = jnp.concatenate([a.reshape(-1) for a in arrs])
    n = flat.shape[0]
    padded = -(-n // PACK_ROW) * PACK_ROW
    return jnp.pad(flat, (0, padded - n)).reshape(padded // 128, 128)


def _unpack(flat, shapes):
    out, off = [], 0
    for s in shapes:
        n = int(np.prod(s))
        out.append(flat[off:off + n].reshape(s))
        off += n
    return out


def _step(P):
    x, target = P["x"][0], P["loss_target"][0]
    me = 4 * lax.axis_index("x") + 2 * lax.axis_index("y") + lax.axis_index("c")
    n_up = P["ffn_w_up"].shape[0]
    Fc = P["ffn_conv_w"].shape[-1]
    Lc = P["c_ln_g"].shape[-1]

    shards = [P["ab_w_in"][0].T.astype(BF16), P["ab_w_out"][0].astype(BF16), P["c_w_in"][0].T.astype(BF16),
              P["c_w_out"][0].astype(BF16)]
    shards += [P["ffn_w_up"][l].T.astype(BF16) for l in range(n_up)]
    shards += [P["ffn_w_down"][l].astype(BF16) for l in range(n_up)]
    shards.append(_pack([P["ffn_conv_w"], P["c_ln_g"], P["c_ln_b"]]))
    full = _all_gather(shards, name="gather_weights")
    wt = {"ab_in_t": full[0], "ab_out": full[1], "c_in_t": full[2], "c_out": full[3],
          "up_t": full[4:4 + n_up], "down": full[4 + n_up:4 + 2 * n_up]}
    sm = full[-1].reshape(N_DEV, -1)
    conv_w = sm[:, :n_up * 3 * Fc].reshape(N_DEV, n_up, 3, Fc).transpose(1, 2, 0, 3).reshape(n_up, 3, N_DEV * Fc)
    off = n_up * 3 * Fc
    ln_g = sm[:, off:off + Lc].reshape(N_DEV * Lc)
    ln_b = sm[:, off + Lc:off + 2 * Lc].reshape(N_DEV * Lc)
    small = {"attn_norm_g": P["attn_norm_g"], "ffn_norm_g": P["ffn_norm_g"], "rel_bias": P["ab_rel_bias"][0],
             "ln_g": ln_g, "ln_b": ln_b, "w_s": P["c_w_s"][0], "b_s": P["c_b_s"][0], "conv_w": conv_w,
             "conv_b": P["ffn_conv_b"], "final_g": P["final_norm_g"]}

    loss_part, grad_x, gw, gs = _local_step(x, target, wt, small)
    loss = lax.psum(loss_part, ("x", "y", "c"))

    big = [gw["ab_in_t"], gw["ab_out"], gw["c_in_t"], gw["c_out"]] + list(gw["up_t"]) + list(gw["down"])
    recv = _all_to_all(big, name="exchange_grads")
    summed = [_sum8(r, name="sum8") for r in recv]
    g_big = {"ab_w_in": summed[0].T[None], "ab_w_out": summed[1][None], "c_w_in": summed[2].T[None], "c_w_out": summed[3][None],
             "ffn_w_up": jnp.stack([s.T for s in summed[4:4 + n_up]]), "ffn_w_down": jnp.stack(summed[4 + n_up:4 + 2 * n_up])}

    packed = _pack([gs[k] for k in SMALL_ORDER])
    gathered = _all_gather([packed], name="gather_small_grads")[0]
    tot = _sum8(gathered.reshape(N_DEV, packed.shape[0], 128), name="sum8_small").reshape(-1)
    gsm = dict(zip(SMALL_ORDER, _unpack(tot, [gs[k].shape for k in SMALL_ORDER])))
    grads = dict(g_big)
    grads["attn_norm_g"] = gsm["attn_norm_g"]
    grads["ffn_norm_g"] = gsm["ffn_norm_g"]
    grads["ab_rel_bias"] = gsm["rel_bias"][None]
    grads["c_ln_g"] = lax.dynamic_slice(gsm["ln_g"], (me * Lc,), (Lc,))[None]
    grads["c_ln_b"] = lax.dynamic_slice(gsm["ln_b"], (me * Lc,), (Lc,))[None]
    grads["c_w_s"] = gsm["w_s"][None]
    grads["c_b_s"] = gsm["b_s"][None]
    grads["ffn_conv_w"] = lax.dynamic_slice(gsm["conv_w"], (0, 0, me * Fc), (n_up, 3, Fc))
    grads["ffn_conv_b"] = gsm["conv_b"]
    grads["final_norm_g"] = gsm["final_g"]

    delta, new_m, new_v = {}, {}, {}
    for k in WEIGHTS:
        delta[k], new_m[k], new_v[k] = _adamw(P[k], grads[k], P["m_" + k], P["v_" + k], name="adamw")
    return (loss, grad_x[None], *[grads[k] for k in WEIGHTS], *[delta[k] for k in WEIGHTS],
            *[new_m[k] for k in WEIGHTS], *[new_v[k] for k in WEIGHTS])


def kernel(x, attn_norm_g, ffn_norm_g, ab_w_in, ab_w_out, ab_rel_bias, c_w_in, c_ln_g, c_ln_b, c_w_s, c_b_s, c_w_out, ffn_w_up, ffn_conv_w, ffn_conv_b, ffn_w_down, final_norm_g, loss_target, m_attn_norm_g, m_ffn_norm_g, m_ab_w_in, m_ab_w_out, m_ab_rel_bias, m_c_w_in, m_c_ln_g, m_c_ln_b, m_c_w_s, m_c_b_s, m_c_w_out, m_ffn_w_up, m_ffn_conv_w, m_ffn_conv_b, m_ffn_w_down, m_final_norm_g, v_attn_norm_g, v_ffn_norm_g, v_ab_w_in, v_ab_w_out, v_ab_rel_bias, v_c_w_in, v_c_ln_g, v_c_ln_b, v_c_w_s, v_c_b_s, v_c_w_out, v_ffn_w_up, v_ffn_conv_w, v_ffn_conv_b, v_ffn_w_down, v_final_norm_g):
    return _step(dict(locals()))
```

```python
import functools

import numpy as np
import jax
import jax.numpy as jnp
from jax import lax
from jax.experimental import pallas as pl
from jax.experimental.pallas import tpu as pltpu

F32 = jnp.float32
BF16 = jnp.bfloat16

D_MODEL = 1024
CHUNK = 64
EPS = 1e-6
NEG_INF = -1e30
RET_HEADS = 4
RET_QK_DIM = 128
RET_V_DIM = 256
ATT_HEADS = 8
ATT_HEAD_DIM = 64
ATT_PAST = 8
ATT_BAND = (ATT_PAST + 1) * CHUNK
MAX_REL = 128
N_REL = 2 * MAX_REL + 1
N_REL_PAD = 384
SGU_BLOCK = 128
SGU_GROUPS = 8
SGU_WIDTH = 2048
SGU_GW = SGU_WIDTH // SGU_GROUPS
FFN_HIDDEN = 2816
RET_W = 2 * RET_HEADS * RET_QK_DIM + 2 * RET_HEADS * RET_V_DIM
ATT_W = 3 * ATT_HEADS * ATT_HEAD_DIM
N_DEV = 8

ADAM_LR = 0.001
ADAM_B1 = 0.9
ADAM_B2 = 0.999
ADAM_EPS = 1e-08
ADAM_WD = 0.01
ADAM_STEP = 10

VMEM_LIMIT = 52 * 1024 * 1024


def _cp(*sem):
    return pltpu.CompilerParams(dimension_semantics=sem if sem else None, vmem_limit_bytes=VMEM_LIMIT)


def _tile(n, target):
    if n <= target:
        return n
    best = None
    for t in range(128, target + 1, 128):
        if n % t == 0:
            best = t
    assert best is not None, (n, target)
    return best


def _gelu(x):
    c = 0.7978845608028654
    return 0.5 * x * (1.0 + jnp.tanh(c * (x + 0.044715 * x * x * x)))


def _gelu_and_grad(x):
    c = 0.7978845608028654
    x2 = x * x
    t = jnp.tanh(c * (x + 0.044715 * x * x2))
    cdf = 0.5 * (1.0 + t)
    grad = cdf + x * (0.5 * c) * (1.0 - t * t) * (1.0 + 3.0 * 0.044715 * x2)
    return x * cdf, grad


def _dot(a, b, dims):
    return lax.dot_general(a, b, (dims, ((), ())), preferred_element_type=F32)


NN = ((1,), (0,))
NT = ((1,), (1,))
TN = ((0,), (0,))


def _mm(a, b, mode, out_dtype, res=None, name="mm", tm_t=None, tn_t=None, tk_t=None):
    if mode == "nt":
        (M, K), N = a.shape, b.shape[0]
        dm, dn, dk = 1024, 1408, K
    elif mode == "nn":
        (M, K), N = a.shape, b.shape[1]
        dm, dn, dk = (1024 if K <= 3072 else 512), 1024, K
    else:
        (K, M), N = a.shape, b.shape[1]
        dm, dn, dk = 1536, 1024, 1024
    tm, tn, tk = _tile(M, tm_t or dm), _tile(N, tn_t or dn), _tile(K, tk_t or dk)
    nk = K // tk
    dims = {"nt": NT, "nn": NN, "tn": TN}[mode]
    a_spec = pl.BlockSpec((tk, tm), lambda i, j, k: (k, i)) if mode == "tn" else pl.BlockSpec((tm, tk), lambda i, j, k: (i, k))
    b_spec = pl.BlockSpec((tn, tk), lambda i, j, k: (j, k)) if mode == "nt" else pl.BlockSpec((tk, tn), lambda i, j, k: (k, j))
    o_spec = pl.BlockSpec((tm, tn), lambda i, j, k: (i, j))
    has_res = res is not None

    def body(*refs):
        if has_res:
            a_ref, b_ref, r_ref, o_ref = refs[:4]
        else:
            a_ref, b_ref, o_ref = refs[:3]
        p = _dot(a_ref[...].astype(BF16), b_ref[...].astype(BF16), dims)
        if nk == 1:
            if has_res:
                p = p + r_ref[...]
            o_ref[...] = p.astype(out_dtype)
            return
        acc = refs[-1]
        k = pl.program_id(2)

        @pl.when(k == 0)
        def _():
            acc[...] = p

        @pl.when(k > 0)
        def _():
            acc[...] += p

        @pl.when(k == nk - 1)
        def _():
            t = acc[...]
            if has_res:
                t = t + r_ref[...]
            o_ref[...] = t.astype(out_dtype)

    in_specs = [a_spec, b_spec] + ([o_spec] if has_res else [])
    args = (a, b) + ((res,) if has_res else ())
    return pl.pallas_call(
        body,
        grid=(M // tm, N // tn, nk),
        in_specs=in_specs,
        out_specs=o_spec,
        out_shape=jax.ShapeDtypeStruct((M, N), out_dtype),
        scratch_shapes=[pltpu.VMEM((tm, tn), F32)] if nk > 1 else [],
        compiler_params=_cp("parallel", "parallel", "arbitrary"),
        name=name,
    )(*args)


def _rms_fwd(h, g, name="rms_fwd", tm=512):
    T, Dm = h.shape
    tm = min(tm, T)

    def body(h_ref, g_ref, o_ref):
        x = h_ref[...]
        r = lax.rsqrt(jnp.mean(x * x, axis=-1, keepdims=True) + EPS)
        o_ref[...] = (x * r * g_ref[...]).astype(o_ref.dtype)

    return pl.pallas_call(
        body,
        grid=(T // tm,),
        in_specs=[pl.BlockSpec((tm, Dm), lambda i: (i, 0)), pl.BlockSpec((1, Dm), lambda i: (0, 0))],
        out_specs=pl.BlockSpec((tm, Dm), lambda i: (i, 0)),
        out_shape=jax.ShapeDtypeStruct((T, Dm), BF16),
        compiler_params=_cp("parallel"),
        name=name,
    )(h, g.reshape(1, Dm))


def _rms_bwd(dhn, h, g, dres, name="rms_bwd", tm=512):
    T, Dm = h.shape
    tm = min(tm, T)

    def body(dhn_ref, h_ref, g_ref, dres_ref, dh_ref, dg_ref):
        i = pl.program_id(0)
        x = h_ref[...]
        r = lax.rsqrt(jnp.mean(x * x, axis=-1, keepdims=True) + EPS)
        xhat = x * r
        d = dhn_ref[...].astype(F32)

        @pl.when(i == 0)
        def _():
            dg_ref[...] = jnp.zeros_like(dg_ref)

        dg_ref[...] += jnp.sum(d * xhat, axis=0, keepdims=True)
        dx = d * g_ref[...]
        m = jnp.mean(dx * xhat, axis=-1, keepdims=True)
        dh_ref[...] = dres_ref[...] + r * (dx - xhat * m)

    row = pl.BlockSpec((tm, Dm), lambda i: (i, 0))
    vec = pl.BlockSpec((1, Dm), lambda i: (0, 0))
    return pl.pallas_call(
        body,
        grid=(T // tm,),
        in_specs=[row, row, vec, row],
        out_specs=[row, vec],
        out_shape=[jax.ShapeDtypeStruct((T, Dm), F32), jax.ShapeDtypeStruct((1, Dm), F32)],
        compiler_params=_cp("arbitrary"),
        name=name,
    )(dhn, h, g.reshape(1, Dm), dres)


def _final_loss(h, g, target, name="final_loss", tm=512):
    T, Dm = h.shape
    tm = min(tm, T)

    def body(h_ref, g_ref, t_ref, loss_ref, dh_ref, dg_ref):
        i = pl.program_id(0)
        x = h_ref[...]
        gv = g_ref[...]
        r = lax.rsqrt(jnp.mean(x * x, axis=-1, keepdims=True) + EPS)
        xhat = x * r
        e = xhat * gv - t_ref[...]

        @pl.when(i == 0)
        def _():
            loss_ref[...] = jnp.zeros_like(loss_ref)
            dg_ref[...] = jnp.zeros_like(dg_ref)

        loss_ref[...] += jnp.full((1, 128), 0.5 / Dm, F32) * jnp.sum(e * e)
        dy = e * (1.0 / Dm)
        dg_ref[...] += jnp.sum(dy * xhat, axis=0, keepdims=True)
        dx = dy * gv
        m = jnp.mean(dx * xhat, axis=-1, keepdims=True)
        dh_ref[...] = r * (dx - xhat * m)

    row = pl.BlockSpec((tm, Dm), lambda i: (i, 0))
    vec = pl.BlockSpec((1, Dm), lambda i: (0, 0))
    return pl.pallas_call(
        body,
        grid=(T // tm,),
        in_specs=[row, vec, row],
        out_specs=[pl.BlockSpec((1, 128), lambda i: (0, 0)), row, vec],
        out_shape=[jax.ShapeDtypeStruct((1, 128), F32), jax.ShapeDtypeStruct((T, Dm), F32), jax.ShapeDtypeStruct((1, Dm), F32)],
        compiler_params=_cp("arbitrary"),
        name=name,
    )(h, g.reshape(1, Dm), target)


HALO = 16


def _conv3(ext, w_ref, b_ref):
    return w_ref[0:1, :] * pltpu.roll(ext, 2, 0) + w_ref[1:2, :] * pltpu.roll(ext, 1, 0) + w_ref[2:3, :] * ext + b_ref[...]


def _ffn_mid_fwd(zg, zu, cw, cb, name="ffn_mid_fwd", tm=256, tc=1408):
    T, Fh = zg.shape
    tm = min(tm, T)
    nj = Fh // tc
    hb = tm // HALO

    def body(zg_ref, zu_ref, zgp_ref, zup_ref, wg_ref, wu_ref, bg_ref, bu_ref, a_ref):
        i = pl.program_id(1)
        first = i == 0

        def conv(z_ref, zp_ref, w_ref, b_ref):
            prev = jnp.where(first, 0.0, zp_ref[...].astype(F32))
            ext = jnp.concatenate([prev, z_ref[...].astype(F32)], axis=0)
            return _conv3(ext, w_ref, b_ref)[HALO:]

        cg = conv(zg_ref, zgp_ref, wg_ref, bg_ref)
        cu = conv(zu_ref, zup_ref, wu_ref, bu_ref)
        a_ref[...] = (_gelu(cg) * cu).astype(a_ref.dtype)

    cur = pl.BlockSpec((tm, tc), lambda j, i: (i, j))
    prev = pl.BlockSpec((HALO, tc), lambda j, i: (jnp.maximum(i * hb - 1, 0), j))
    wg = pl.BlockSpec((3, tc), lambda j, i: (0, j))
    wu = pl.BlockSpec((3, tc), lambda j, i: (0, j + nj))
    bg = pl.BlockSpec((1, tc), lambda j, i: (0, j))
    bu = pl.BlockSpec((1, tc), lambda j, i: (0, j + nj))
    return pl.pallas_call(
        body,
        grid=(nj, T // tm),
        in_specs=[cur, cur, prev, prev, wg, wu, bg, bu],
        out_specs=cur,
        out_shape=jax.ShapeDtypeStruct((T, Fh), BF16),
        compiler_params=_cp("parallel", "parallel"),
        name=name,
    )(zg, zu, zg, zu, cw, cw, cb, cb)


def _ffn_mid_bwd(zg, zu, da, cw, cb, name="ffn_mid_bwd", tm=512, tc=256):
    T, Fh = zg.shape
    tm = min(tm, T)
    nj = Fh // tc
    hb = tm // HALO
    nhb = T // HALO

    def body(zg_ref, zu_ref, zgp_ref, zup_ref, zgn_ref, zun_ref, da_ref, dan_ref, wg_ref, wu_ref, bg_ref, bu_ref,
             dzg_ref, dzu_ref, dwg_ref, dwu_ref, dbg_ref, dbu_ref):
        i = pl.program_id(1)
        first = i == 0
        last = i == T // tm - 1

        @pl.when(first)
        def _():
            dwg_ref[...] = jnp.zeros_like(dwg_ref)
            dwu_ref[...] = jnp.zeros_like(dwu_ref)
            dbg_ref[...] = jnp.zeros_like(dbg_ref)
            dbu_ref[...] = jnp.zeros_like(dbu_ref)

        def ext_of(p_ref, c_ref, n_ref):
            p = jnp.where(first, 0.0, p_ref[...].astype(F32))
            return jnp.concatenate([p, c_ref[...].astype(F32), n_ref[...].astype(F32)], axis=0)

        zge = ext_of(zgp_ref, zg_ref, zgn_ref)
        zue = ext_of(zup_ref, zu_ref, zun_ref)
        dan = jnp.where(last, 0.0, dan_ref[...].astype(F32))
        dae = jnp.concatenate([jnp.zeros((HALO, tc), F32), da_ref[...].astype(F32), dan], axis=0)
        cg = _conv3(zge, wg_ref, bg_ref)
        cu = _conv3(zue, wu_ref, bu_ref)
        gel, dgel = _gelu_and_grad(cg)
        dcg = dae * cu * dgel
        dcu = dae * gel
        lo, hi = HALO, HALO + tm

        def back(dc, ze, w_ref, dz_ref, dw_ref, db_ref):
            n = dc.shape[0]
            dz = w_ref[2:3, :] * dc + w_ref[1:2, :] * pltpu.roll(dc, n - 1, 0) + w_ref[0:1, :] * pltpu.roll(dc, n - 2, 0)
            dz_ref[...] = dz[lo:hi].astype(dz_ref.dtype)
            dcc = dc[lo:hi]
            db_ref[...] += jnp.sum(dcc, axis=0, keepdims=True)
            dw_ref[0:1, :] += jnp.sum(dcc * pltpu.roll(ze, 2, 0)[lo:hi], axis=0, keepdims=True)
            dw_ref[1:2, :] += jnp.sum(dcc * pltpu.roll(ze, 1, 0)[lo:hi], axis=0, keepdims=True)
            dw_ref[2:3, :] += jnp.sum(dcc * ze[lo:hi], axis=0, keepdims=True)

        back(dcg, zge, wg_ref, dzg_ref, dwg_ref, dbg_ref)
        back(dcu, zue, wu_ref, dzu_ref, dwu_ref, dbu_ref)

    cur = pl.BlockSpec((tm, tc), lambda j, i: (i, j))
    prev = pl.BlockSpec((HALO, tc), lambda j, i: (jnp.maximum(i * hb - 1, 0), j))
    nxt = pl.BlockSpec((HALO, tc), lambda j, i: (jnp.minimum((i + 1) * hb, nhb - 1), j))
    wg = pl.BlockSpec((3, tc), lambda j, i: (0, j))
    wu = pl.BlockSpec((3, tc), lambda j, i: (0, j + nj))
    bg = pl.BlockSpec((1, tc), lambda j, i: (0, j))
    bu = pl.BlockSpec((1, tc), lambda j, i: (0, j + nj))
    dw = pl.BlockSpec((3, tc), lambda j, i: (0, j))
    db = pl.BlockSpec((1, tc), lambda j, i: (0, j))
    dzg, dzu, dwg, dwu, dbg, dbu = pl.pallas_call(
        body,
        grid=(nj, T // tm),
        in_specs=[cur, cur, prev, prev, nxt, nxt, cur, nxt, wg, wu, bg, bu],
        out_specs=[cur, cur, dw, dw, db, db],
        out_shape=[jax.ShapeDtypeStruct((T, Fh), BF16), jax.ShapeDtypeStruct((T, Fh), BF16),
                   jax.ShapeDtypeStruct((3, Fh), F32), jax.ShapeDtypeStruct((3, Fh), F32),
                   jax.ShapeDtypeStruct((1, Fh), F32), jax.ShapeDtypeStruct((1, Fh), F32)],
        compiler_params=_cp("parallel", "arbitrary"),
        name=name,
    )(zg, zu, zg, zu, zg, zu, da, da, cw, cw, cb, cb)
    return dzg, dzu, jnp.concatenate([dwg, dwu], axis=1), jnp.concatenate([dbg, dbu], axis=1)


def _sgu_mask():
    r = lax.broadcasted_iota(jnp.int32, (SGU_BLOCK, SGU_BLOCK), 0)
    c = lax.broadcasted_iota(jnp.int32, (SGU_BLOCK, SGU_BLOCK), 1)
    return (c < CHUNK) | (r >= CHUNK)


def _sgu_fwd(zz, ln_g, ln_b, w_s, b_s, name="sgu_fwd", tm=256):
    T = zz.shape[0]
    tm = min(tm, T)
    W = SGU_WIDTH

    def body(zu_ref, zv_ref, g_ref, b_ref, ws_ref, bs_ref, y_ref):
        u = _gelu(zu_ref[...].astype(F32))
        v = _gelu(zv_ref[...].astype(F32))
        mu = jnp.mean(v, axis=-1, keepdims=True)
        xc = v - mu
        rstd = lax.rsqrt(jnp.mean(xc * xc, axis=-1, keepdims=True) + EPS)
        vn = (xc * rstd * g_ref[...] + b_ref[...]).astype(BF16)
        mask = _sgu_mask()
        for g in range(SGU_GROUPS):
            wm = jnp.where(mask, ws_ref[g], 0.0).astype(BF16)
            cs = slice(g * SGU_GW, (g + 1) * SGU_GW)
            for blk in range(tm // SGU_BLOCK):
                rs = slice(blk * SGU_BLOCK, (blk + 1) * SGU_BLOCK)
                mixed = _dot(wm, vn[rs, cs], NN) + bs_ref[g]
                y_ref[rs, cs] = (u[rs, cs] * mixed).astype(y_ref.dtype)

    return pl.pallas_call(
        body,
        grid=(T // tm,),
        in_specs=[pl.BlockSpec((tm, W), lambda i: (i, 0)), pl.BlockSpec((tm, W), lambda i: (i, 1)),
                  pl.BlockSpec((1, W), lambda i: (0, 0)), pl.BlockSpec((1, W), lambda i: (0, 0)),
                  pl.BlockSpec((SGU_GROUPS, SGU_BLOCK, SGU_BLOCK), lambda i: (0, 0, 0)),
                  pl.BlockSpec((SGU_GROUPS, SGU_BLOCK, 1), lambda i: (0, 0, 0))],
        out_specs=pl.BlockSpec((tm, W), lambda i: (i, 0)),
        out_shape=jax.ShapeDtypeStruct((T, W), BF16),
        compiler_params=_cp("parallel"),
        name=name,
    )(zz, zz, ln_g.reshape(1, W), ln_b.reshape(1, W), w_s, b_s.reshape(SGU_GROUPS, SGU_BLOCK, 1))


def _sgu_bwd(zz, dy, ln_g, ln_b, w_s, b_s, name="sgu_bwd", tm=256):
    T = zz.shape[0]
    tm = min(tm, T)
    W = SGU_WIDTH

    def body(zu_ref, zv_ref, dy_ref, g_ref, b_ref, ws_ref, bs_ref, dzz_ref, dws_ref, dbs_ref, dg_ref, db_ref, dvn_ref):
        i = pl.program_id(0)

        @pl.when(i == 0)
        def _():
            dws_ref[...] = jnp.zeros_like(dws_ref)
            dbs_ref[...] = jnp.zeros_like(dbs_ref)
            dg_ref[...] = jnp.zeros_like(dg_ref)
            db_ref[...] = jnp.zeros_like(db_ref)

        u, du_dz = _gelu_and_grad(zu_ref[...].astype(F32))
        v, dv_dz = _gelu_and_grad(zv_ref[...].astype(F32))
        mu = jnp.mean(v, axis=-1, keepdims=True)
        xc = v - mu
        rstd = lax.rsqrt(jnp.mean(xc * xc, axis=-1, keepdims=True) + EPS)
        xhat = xc * rstd
        gv = g_ref[...]
        vn = (xhat * gv + b_ref[...]).astype(BF16)
        dyv = dy_ref[...].astype(F32)
        mask = _sgu_mask()
        for g in range(SGU_GROUPS):
            wm = jnp.where(mask, ws_ref[g], 0.0).astype(BF16)
            cs = slice(g * SGU_GW, (g + 1) * SGU_GW)
            dw_acc = jnp.zeros((SGU_BLOCK, SGU_BLOCK), F32)
            db_acc = jnp.zeros((SGU_BLOCK, 1), F32)
            for blk in range(tm // SGU_BLOCK):
                rs = slice(blk * SGU_BLOCK, (blk + 1) * SGU_BLOCK)
                vn_bg = vn[rs, cs]
                mixed = _dot(wm, vn_bg, NN) + bs_ref[g]
                dy_bg = dyv[rs, cs]
                dmixed = dy_bg * u[rs, cs]
                dmb = dmixed.astype(BF16)
                dw_acc += _dot(dmb, vn_bg, NT)
                db_acc += jnp.sum(dmixed, axis=1, keepdims=True)
                dvn_ref[rs, cs] = _dot(wm, dmb, TN)
                dzz_ref[rs, cs] = (dy_bg * mixed * du_dz[rs, cs]).astype(dzz_ref.dtype)
            dws_ref[g] += jnp.where(mask, dw_acc, 0.0)
            dbs_ref[g] += db_acc
        dvn = dvn_ref[...]
        dg_ref[...] += jnp.sum(dvn * xhat, axis=0, keepdims=True)
        db_ref[...] += jnp.sum(dvn, axis=0, keepdims=True)
        dxh = dvn * gv
        m1 = jnp.mean(dxh, axis=-1, keepdims=True)
        m2 = jnp.mean(dxh * xhat, axis=-1, keepdims=True)
        dv = rstd * (dxh - m1 - xhat * m2)
        dzz_ref[:, W:] = (dv * dv_dz).astype(dzz_ref.dtype)

    vec = pl.BlockSpec((1, W), lambda i: (0, 0))
    ws_spec = pl.BlockSpec((SGU_GROUPS, SGU_BLOCK, SGU_BLOCK), lambda i: (0, 0, 0))
    bs_spec = pl.BlockSpec((SGU_GROUPS, SGU_BLOCK, 1), lambda i: (0, 0, 0))
    return pl.pallas_call(
        body,
        grid=(T // tm,),
        in_specs=[pl.BlockSpec((tm, W), lambda i: (i, 0)), pl.BlockSpec((tm, W), lambda i: (i, 1)),
                  pl.BlockSpec((tm, W), lambda i: (i, 0)), vec, vec, ws_spec, bs_spec],
        out_specs=[pl.BlockSpec((tm, 2 * W), lambda i: (i, 0)), ws_spec, bs_spec, vec, vec],
        out_shape=[jax.ShapeDtypeStruct((T, 2 * W), BF16),
                   jax.ShapeDtypeStruct((SGU_GROUPS, SGU_BLOCK, SGU_BLOCK), F32),
                   jax.ShapeDtypeStruct((SGU_GROUPS, SGU_BLOCK, 1), F32),
                   jax.ShapeDtypeStruct((1, W), F32), jax.ShapeDtypeStruct((1, W), F32)],
        scratch_shapes=[pltpu.VMEM((tm, W), F32)],
        compiler_params=_cp("arbitrary"),
        name=name,
    )(zz, zz, dy, ln_g.reshape(1, W), ln_b.reshape(1, W), w_s, b_s.reshape(SGU_GROUPS, SGU_BLOCK, 1))


RET_TR = 256
RET_BLK = 256
QK_SCALE = RET_QK_DIM ** -0.5


def _ret_tables(T):
    half = RET_QK_DIM // 2
    inv = 1.0 / (10000.0 ** jnp.linspace(0.0, 1.0, half, dtype=F32))
    ang = jnp.arange(T).astype(F32)[:, None] * inv[None, :]
    cos, sin = jnp.cos(ang), jnp.sin(ang)
    rot_c = jnp.concatenate([cos, cos], axis=1)
    rot_s = jnp.concatenate([-sin, sin], axis=1)
    log_g = jnp.log1p(-jnp.exp2(-5.0 - jnp.arange(RET_HEADS, dtype=F32)))
    idx = jnp.arange(RET_BLK, dtype=F32)
    dist = idx[:, None] - idx[None, :]
    cq, ck = jnp.arange(RET_BLK)[:, None] // CHUNK, jnp.arange(RET_BLK)[None, :] // CHUNK
    expo = jnp.where(ck == cq, jnp.abs(dist), dist)
    d_blk = jnp.where((ck <= cq)[None], jnp.exp(log_g[:, None, None] * expo[None]), 0.0)
    k_dec = jnp.exp(log_g[:, None] * (RET_BLK - 1 - idx)[None, :])[:, :, None]
    q_dec = jnp.exp(log_g[:, None] * (idx + 1.0)[None, :])[:, :, None]
    c_dec = jnp.exp(log_g * RET_BLK)[:, None, None]
    return rot_c, rot_s, d_blk, q_dec, k_dec, c_dec


def _rot(x, c, s):
    return x * c + pltpu.roll(x, RET_QK_DIM // 2, 1) * s


def _ret_specs(tr, rev, nb):
    ix = (lambda n: nb - 1 - n) if rev else (lambda n: n)
    tab = pl.BlockSpec((tr, RET_QK_DIM), lambda n: (ix(n), 0))
    dm = pl.BlockSpec((RET_HEADS, RET_BLK, RET_BLK), lambda n: (0, 0, 0))
    dv = pl.BlockSpec((RET_HEADS, RET_BLK, 1), lambda n: (0, 0, 0))
    dc = pl.BlockSpec((RET_HEADS, 1, 1), lambda n: (0, 0, 0))
    return ix, [tab, tab, dm, dv, dv, dc]


def _ret_fwd(z_a, tables, name="ret_fwd"):
    T = z_a.shape[0]
    tr = min(RET_TR, T)
    cpb = tr // RET_BLK
    nb = T // tr
    QW, VW = RET_HEADS * RET_QK_DIM, RET_HEADS * RET_V_DIM
    ix, tab_specs = _ret_specs(tr, False, nb)

    def body(z_ref, c_ref, s_ref, dm_ref, qd_ref, kd_ref, cd_ref, y_ref, st_ref, state):
        @pl.when(pl.program_id(0) == 0)
        def _():
            state[...] = jnp.zeros_like(state)

        for c in range(cpb):
            for h in range(RET_HEADS):
                rs = slice(c * RET_BLK, (c + 1) * RET_BLK)
                cc, ss = c_ref[rs, :], s_ref[rs, :]
                q = z_ref[rs, h * RET_QK_DIM:(h + 1) * RET_QK_DIM].astype(F32)
                k = z_ref[rs, QW + h * RET_QK_DIM:QW + (h + 1) * RET_QK_DIM].astype(F32)
                v = z_ref[rs, 2 * QW + h * RET_V_DIM:2 * QW + (h + 1) * RET_V_DIM]
                gt = z_ref[rs, 2 * QW + VW + h * RET_V_DIM:2 * QW + VW + (h + 1) * RET_V_DIM].astype(F32)
                qr = _rot(q, cc, ss)
                kr = _rot(k, cc, ss) * QK_SCALE
                s_old = state[h]
                sb = s_old.astype(BF16)
                st_ref[c, h] = sb
                s = _dot(qr.astype(BF16), kr.astype(BF16), NT) * dm_ref[h]
                o = _dot(s.astype(BF16), v, NN) + _dot((qr * qd_ref[h]).astype(BF16), sb, NN)
                state[h] = s_old * cd_ref[h] + _dot((kr * kd_ref[h]).astype(BF16), v, TN)
                mu = jnp.mean(o, axis=-1, keepdims=True)
                oc = o - mu
                rn = oc * lax.rsqrt(jnp.mean(oc * oc, axis=-1, keepdims=True) + EPS)
                silu = gt / (1.0 + jnp.exp(-gt))
                y_ref[rs, h * RET_V_DIM:(h + 1) * RET_V_DIM] = (silu * rn).astype(y_ref.dtype)

    return pl.pallas_call(
        body,
        grid=(nb,),
        in_specs=[pl.BlockSpec((tr, RET_W), lambda n: (n, 0))] + tab_specs,
        out_specs=[pl.BlockSpec((tr, VW), lambda n: (n, 0)),
                   pl.BlockSpec((cpb, RET_HEADS, RET_QK_DIM, RET_V_DIM), lambda n: (n, 0, 0, 0))],
        out_shape=[jax.ShapeDtypeStruct((T, VW), BF16),
                   jax.ShapeDtypeStruct((T // RET_BLK, RET_HEADS, RET_QK_DIM, RET_V_DIM), BF16)],
        scratch_shapes=[pltpu.VMEM((RET_HEADS, RET_QK_DIM, RET_V_DIM), F32)],
        compiler_params=_cp("arbitrary"),
        name=name,
    )(z_a, *tables)


def _ret_bwd(z_a, dy, states, tables, name="ret_bwd"):
    T = z_a.shape[0]
    tr = min(RET_TR, T)
    cpb = tr // RET_BLK
    nb = T // tr
    QW, VW = RET_HEADS * RET_QK_DIM, RET_HEADS * RET_V_DIM
    ix, tab_specs = _ret_specs(tr, True, nb)

    def body(z_ref, dy_ref, st_ref, c_ref, s_ref, dm_ref, qd_ref, kd_ref, cd_ref, dz_ref, dstate):
        @pl.when(pl.program_id(0) == 0)
        def _():
            dstate[...] = jnp.zeros_like(dstate)

        for c in reversed(range(cpb)):
            for h in range(RET_HEADS):
                rs = slice(c * RET_BLK, (c + 1) * RET_BLK)
                cc, ss = c_ref[rs, :], s_ref[rs, :]
                q = z_ref[rs, h * RET_QK_DIM:(h + 1) * RET_QK_DIM].astype(F32)
                k = z_ref[rs, QW + h * RET_QK_DIM:QW + (h + 1) * RET_QK_DIM].astype(F32)
                v = z_ref[rs, 2 * QW + h * RET_V_DIM:2 * QW + (h + 1) * RET_V_DIM]
                gt = z_ref[rs, 2 * QW + VW + h * RET_V_DIM:2 * QW + VW + (h + 1) * RET_V_DIM].astype(F32)
                dyv = dy_ref[rs, h * RET_V_DIM:(h + 1) * RET_V_DIM].astype(F32)
                dmat, qd, kd = dm_ref[h], qd_ref[h], kd_ref[h]
                qr = _rot(q, cc, ss)
                kr = _rot(k, cc, ss) * QK_SCALE
                qrb, krb = qr.astype(BF16), kr.astype(BF16)
                sb = st_ref[c, h]
                sd = (_dot(qrb, krb, NT) * dmat).astype(BF16)
                qdb = (qr * qd).astype(BF16)
                kdb = (kr * kd).astype(BF16)
                o = _dot(sd, v, NN) + _dot(qdb, sb, NN)
                mu = jnp.mean(o, axis=-1, keepdims=True)
                oc = o - mu
                rstd = lax.rsqrt(jnp.mean(oc * oc, axis=-1, keepdims=True) + EPS)
                rn = oc * rstd
                sg = 1.0 / (1.0 + jnp.exp(-gt))
                dgt = dyv * rn * (sg * (1.0 + gt * (1.0 - sg)))
                drn = dyv * (gt * sg)
                do = rstd * (drn - jnp.mean(drn, axis=-1, keepdims=True) - rn * jnp.mean(drn * rn, axis=-1, keepdims=True))
                dob = do.astype(BF16)
                dsn = dstate[h]
                dsnb = dsn.astype(BF16)
                ds_raw = (_dot(dob, v, NT) * dmat).astype(BF16)
                dv = _dot(sd, dob, TN) + _dot(kdb, dsnb, NN)
                dqr = _dot(ds_raw, krb, NN) + qd * _dot(dob, sb, NT)
                dkr = (_dot(ds_raw, qrb, TN) + kd * _dot(v, dsnb, NT)) * QK_SCALE
                dstate[h] = dsn * cd_ref[h] + _dot(qdb, dob, TN)
                dq = dqr * cc + pltpu.roll(dqr * ss, RET_QK_DIM // 2, 1)
                dk = dkr * cc + pltpu.roll(dkr * ss, RET_QK_DIM // 2, 1)
                dz_ref[rs, h * RET_QK_DIM:(h + 1) * RET_QK_DIM] = dq.astype(dz_ref.dtype)
                dz_ref[rs, QW + h * RET_QK_DIM:QW + (h + 1) * RET_QK_DIM] = dk.astype(dz_ref.dtype)
                dz_ref[rs, 2 * QW + h * RET_V_DIM:2 * QW + (h + 1) * RET_V_DIM] = dv.astype(dz_ref.dtype)
                dz_ref[rs, 2 * QW + VW + h * RET_V_DIM:2 * QW + VW + (h + 1) * RET_V_DIM] = dgt.astype(dz_ref.dtype)

    return pl.pallas_call(
        body,
        grid=(nb,),
        in_specs=[pl.BlockSpec((tr, RET_W), lambda n: (ix(n), 0)),
                  pl.BlockSpec((tr, VW), lambda n: (ix(n), 0)),
                  pl.BlockSpec((cpb, RET_HEADS, RET_QK_DIM, RET_V_DIM), lambda n: (ix(n), 0, 0, 0))] + tab_specs,
        out_specs=pl.BlockSpec((tr, RET_W), lambda n: (ix(n), 0)),
        out_shape=jax.ShapeDtypeStruct((T, RET_W), BF16),
        scratch_shapes=[pltpu.VMEM((RET_HEADS, RET_QK_DIM, RET_V_DIM), F32)],
        compiler_params=_cp("arbitrary"),
        name=name,
    )(z_a, dy, states, *tables)


ATT_TQ = 256
ATT_CPB = ATT_TQ // CHUNK
ATT_SCALE = ATT_HEAD_DIM ** -0.5


ATT_WIN = 3 * ATT_TQ
ATT_NB = CHUNK * ATT_BAND


def _rel_index():
    i = np.arange(CHUNK)[:, None]
    j = np.arange(ATT_BAND)[None, :]
    rel = np.clip(i + ATT_PAST * CHUNK - j, -MAX_REL, MAX_REL) + MAX_REL
    return jnp.asarray(rel.reshape(1, ATT_NB).astype(np.int32))


def _split3(x):
    hi = x.astype(BF16)
    r1 = x - hi.astype(F32)
    mid = r1.astype(BF16)
    lo = (r1 - mid.astype(F32)).astype(BF16)
    return hi, mid, lo


REL_TILE = 4608


def _bias_expand(rel_bias, name="bias_expand"):
    H = rel_bias.shape[0]
    n = ATT_NB
    padded = jnp.pad(rel_bias, ((0, 0), (0, N_REL_PAD - N_REL)))

    def body(rb_ref, idx_ref, o_ref):
        onehot = (lax.broadcasted_iota(jnp.int32, (N_REL_PAD, REL_TILE), 0) == idx_ref[...]).astype(BF16)
        hi, mid, lo = _split3(rb_ref[...])
        o_ref[...] = _dot(hi, onehot, NN) + _dot(mid, onehot, NN) + _dot(lo, onehot, NN)

    out = pl.pallas_call(
        body,
        grid=(n // REL_TILE,),
        in_specs=[pl.BlockSpec((H, N_REL_PAD), lambda t: (0, 0)), pl.BlockSpec((1, REL_TILE), lambda t: (0, t))],
        out_specs=pl.BlockSpec((H, REL_TILE), lambda t: (0, t)),
        out_shape=jax.ShapeDtypeStruct((H, n), F32),
        compiler_params=_cp("parallel"),
        name=name,
    )(padded, _rel_index())
    return out.reshape(H, CHUNK, ATT_BAND)


def _bias_tile(band, name="bias_tile"):
    H = band.shape[0]
    padded = jnp.pad(band, ((0, 0), (0, 0), (0, ATT_WIN - ATT_BAND)), constant_values=NEG_INF)

    def body(b_ref, o_ref):
        b = b_ref[...]
        for a in range(ATT_CPB):
            o_ref[a * CHUNK:(a + 1) * CHUNK, :] = pltpu.roll(b, a * CHUNK, 1) if a else b

    return pl.pallas_call(
        body,
        grid=(H,),
        in_specs=[pl.BlockSpec((None, CHUNK, ATT_WIN), lambda h: (h, 0, 0))],
        out_specs=pl.BlockSpec((None, ATT_TQ, ATT_WIN), lambda h: (h, 0, 0)),
        out_shape=jax.ShapeDtypeStruct((H, ATT_TQ, ATT_WIN), F32),
        compiler_params=_cp("parallel"),
        name=name,
    )(padded)


def _bias_untile(dtile, name="bias_untile"):
    H = dtile.shape[0]

    def body(d_ref, o_ref):
        acc = d_ref[0:CHUNK, :]
        for a in range(1, ATT_CPB):
            acc = acc + pltpu.roll(d_ref[a * CHUNK:(a + 1) * CHUNK, :], ATT_WIN - a * CHUNK, 1)
        o_ref[...] = acc

    out = pl.pallas_call(
        body,
        grid=(H,),
        in_specs=[pl.BlockSpec((None, ATT_TQ, ATT_WIN), lambda h: (h, 0, 0))],
        out_specs=pl.BlockSpec((None, CHUNK, ATT_WIN), lambda h: (h, 0, 0)),
        out_shape=jax.ShapeDtypeStruct((H, CHUNK, ATT_WIN), F32),
        compiler_params=_cp("parallel"),
        name=name,
    )(dtile)
    return out[:, :, :ATT_BAND]


def _bias_reduce(dbias, name="bias_reduce"):
    H = dbias.shape[0]
    n = ATT_NB

    def body(db_ref, idx_ref, o_ref):
        @pl.when(pl.program_id(0) == 0)
        def _():
            o_ref[...] = jnp.zeros_like(o_ref)

        onehot = (lax.broadcasted_iota(jnp.int32, (N_REL_PAD, REL_TILE), 0) == idx_ref[...]).astype(BF16)
        hi, mid, lo = _split3(db_ref[...])
        o_ref[...] += _dot(hi, onehot, NT) + _dot(mid, onehot, NT) + _dot(lo, onehot, NT)

    out = pl.pallas_call(
        body,
        grid=(n // REL_TILE,),
        in_specs=[pl.BlockSpec((H, REL_TILE), lambda t: (0, t)), pl.BlockSpec((1, REL_TILE), lambda t: (0, t))],
        out_specs=pl.BlockSpec((H, N_REL_PAD), lambda t: (0, 0)),
        out_shape=jax.ShapeDtypeStruct((H, N_REL_PAD), F32),
        compiler_params=_cp("arbitrary"),
        name=name,
    )(dbias.reshape(H, n), _rel_index())
    return out[:, :N_REL]


def _att_probs(q, kwin, bias, key_row0):
    s = _dot(q, kwin, NT) * ATT_SCALE + bias
    col = lax.broadcasted_iota(jnp.int32, (ATT_TQ, ATT_WIN), 1)
    s = jnp.where(col + key_row0 >= 0, s, NEG_INF)
    e = jnp.exp(s - jnp.max(s, axis=-1, keepdims=True))
    return e * (1.0 / jnp.sum(e, axis=-1, keepdims=True))


def _att_specs(tq, nq, clip_q):
    qi = (lambda h, m: (h, jnp.minimum(m, nq - 1), 0)) if clip_q else (lambda h, m: (h, m, 0))
    q = pl.BlockSpec((None, tq, ATT_HEAD_DIM), qi)
    win = [pl.BlockSpec((None, tq, ATT_HEAD_DIM), functools.partial(lambda h, m, back: (h, jnp.clip(m - back, 0, nq - 1), 0), back=b))
           for b in (2, 1, 0)]
    bias = pl.BlockSpec((None, ATT_TQ, ATT_WIN), lambda h, m: (h, 0, 0))
    return q, win, bias


def _att_fwd(q, k, v, bias, name="att_fwd"):
    H, T, d = q.shape
    tq = ATT_TQ
    nq = T // tq
    qs, win, bs = _att_specs(tq, nq, False)

    def body(q_ref, k0, k1, k2, v0, v1, v2, b_ref, o_ref):
        m = pl.program_id(1)
        kw = jnp.concatenate([k0[...], k1[...], k2[...]], axis=0)
        vw = jnp.concatenate([v0[...], v1[...], v2[...]], axis=0)
        p = _att_probs(q_ref[...], kw, b_ref[...], (m - 2) * tq)
        o_ref[...] = _dot(p.astype(BF16), vw, NN).astype(o_ref.dtype)

    return pl.pallas_call(
        body,
        grid=(H, nq),
        in_specs=[qs] + win + win + [bs],
        out_specs=qs,
        out_shape=jax.ShapeDtypeStruct((H, T, d), BF16),
        compiler_params=_cp("parallel", "parallel"),
        name=name,
    )(q, k, k, k, v, v, v, bias)


def _att_bwd(q, k, v, bias, do, name="att_bwd"):
    H, T, d = q.shape
    tq = ATT_TQ
    nq = T // tq
    qs, win, bs = _att_specs(tq, nq, True)
    kv_out = pl.BlockSpec((None, tq, d), lambda h, m: (h, jnp.maximum(m - 2, 0), 0))
    W3 = 3 * tq

    def body(q_ref, k0, k1, k2, v0, v1, v2, b_ref, do_ref, dq_ref, dk_ref, dv_ref, db_ref, dkc, dvc, dkw, dvw):
        m = pl.program_id(1)

        @pl.when(m == 0)
        def _():
            dkc[...] = jnp.zeros_like(dkc)
            dvc[...] = jnp.zeros_like(dvc)
            db_ref[...] = jnp.zeros_like(db_ref)

        @pl.when(m >= nq)
        def _():
            dkw[...] = jnp.zeros_like(dkw)
            dvw[...] = jnp.zeros_like(dvw)

        @pl.when(m < nq)
        def _():
            kw = jnp.concatenate([k0[...], k1[...], k2[...]], axis=0)
            vw = jnp.concatenate([v0[...], v1[...], v2[...]], axis=0)
            qv, dov = q_ref[...], do_ref[...]
            p = _att_probs(qv, kw, b_ref[...], (m - 2) * tq)
            dp = _dot(dov, vw, NT)
            ds = p * (dp - jnp.sum(dp * p, axis=-1, keepdims=True))
            db_ref[...] += ds
            dsb = ds.astype(BF16)
            dq_ref[...] = (_dot(dsb, kw, NN) * ATT_SCALE).astype(dq_ref.dtype)
            dkw[...] = _dot(dsb, qv, TN) * ATT_SCALE
            dvw[...] = _dot(p.astype(BF16), dov, TN)

        dk_ref[...] = (dkc[0:tq, :] + dkw[0:tq, :]).astype(dk_ref.dtype)
        dv_ref[...] = (dvc[0:tq, :] + dvw[0:tq, :]).astype(dv_ref.dtype)
        dkc[0:tq, :] = dkc[tq:2 * tq, :] + dkw[tq:2 * tq, :]
        dvc[0:tq, :] = dvc[tq:2 * tq, :] + dvw[tq:2 * tq, :]
        dkc[tq:2 * tq, :] = dkw[2 * tq:W3, :]
        dvc[tq:2 * tq, :] = dvw[2 * tq:W3, :]

    hd = jax.ShapeDtypeStruct((H, T, d), BF16)
    return pl.pallas_call(
        body,
        grid=(H, nq + 2),
        in_specs=[qs] + win + win + [bs, qs],
        out_specs=[qs, kv_out, kv_out, bs],
        out_shape=[hd, hd, hd, jax.ShapeDtypeStruct((H, ATT_TQ, ATT_WIN), F32)],
        scratch_shapes=[pltpu.VMEM((2 * tq, d), F32), pltpu.VMEM((2 * tq, d), F32),
                        pltpu.VMEM((W3, d), F32), pltpu.VMEM((W3, d), F32)],
        compiler_params=_cp("parallel", "arbitrary"),
        name=name,
    )(q, k, k, k, v, v, v, bias, do)


def _to_heads(z):
    T, w = z.shape
    n = w // (ATT_HEADS * ATT_HEAD_DIM)
    return z.reshape(T, n, ATT_HEADS, ATT_HEAD_DIM).transpose(1, 2, 0, 3)


def _from_heads(x):
    n, H, T, d = x.shape
    return x.transpose(2, 0, 1, 3).reshape(T, n * H * d)


def _local_step(x, target, wt, small):
    T = x.shape[0]
    Fh = FFN_HIDDEN
    tables = _ret_tables(T)
    gw, gs = {}, {}

    hn0 = _rms_fwd(x, small["attn_norm_g"][0], name="rms_fwd")
    z_a = _mm(hn0, wt["ab_in_t"][:RET_W], "nt", BF16, name="mm_ab_in_a")
    z_b = _mm(hn0, wt["ab_in_t"][RET_W:], "nt", BF16, name="mm_ab_in_b")
    y_a, states = _ret_fwd(z_a, tables)
    qkv = _to_heads(z_b)
    bias = _bias_tile(_bias_expand(small["rel_bias"]))
    o_b = _att_fwd(qkv[0], qkv[1], qkv[2], bias)
    y = jnp.concatenate([y_a, _from_heads(o_b[None])], axis=1)
    h1 = _mm(y, wt["ab_out"], "nn", F32, res=x, name="mm_ab_out")

    def ffn_fwd(h, layer):
        hf = _rms_fwd(h, small["ffn_norm_g"][layer], name="rms_fwd")
        zg = _mm(hf, wt["up_t"][layer][:Fh], "nt", BF16, name="mm_up")
        zu = _mm(hf, wt["up_t"][layer][Fh:], "nt", BF16, name="mm_up")
        a = _ffn_mid_fwd(zg, zu, small["conv_w"][layer], small["conv_b"][layer][None, :])
        h_out = _mm(a, wt["down"][layer], "nn", F32, res=h, name="mm_down")
        return h_out, (hf, zg, zu, a)

    def ffn_bwd(dh_out, h, layer, saved):
        hf, zg, zu, a = saved
        da = _mm(dh_out, wt["down"][layer], "nt", BF16, name="mm_d_a")
        d_down = _mm(a, dh_out, "tn", BF16, name="mm_dw_down")
        dzg, dzu, dcw, dcb = _ffn_mid_bwd(zg, zu, da, small["conv_w"][layer], small["conv_b"][layer][None, :])
        dhf = _mm(dzg, wt["up_t"][layer][:Fh], "nn", F32, name="mm_d_hf")
        dhf = _mm(dzu, wt["up_t"][layer][Fh:], "nn", F32, res=dhf, name="mm_d_hf_acc")
        d_up = jnp.concatenate([_mm(dzg, hf, "tn", BF16, name="mm_dw_up"), _mm(dzu, hf, "tn", BF16, name="mm_dw_up")], axis=0)
        dh, dg = _rms_bwd(dhf, h, small["ffn_norm_g"][layer], dh_out)
        return dh, dg, d_up, d_down, dcw, dcb

    h2, ffn0 = ffn_fwd(h1, 0)

    hn1 = _rms_fwd(h2, small["attn_norm_g"][1], name="rms_fwd")
    zz = _mm(hn1, wt["c_in_t"], "nt", BF16, name="mm_c_in")
    ys = _sgu_fwd(zz, small["ln_g"], small["ln_b"], small["w_s"], small["b_s"])
    h3 = _mm(ys, wt["c_out"], "nn", F32, res=h2, name="mm_c_out")
    h4, ffn1 = ffn_fwd(h3, 1)

    loss_vec, dh4, gs["final_g"] = _final_loss(h4, small["final_g"], target)

    dh3, dgf1, d_up1, d_down1, dcw1, dcb1 = ffn_bwd(dh4, h3, 1, ffn1)
    dys = _mm(dh3, wt["c_out"], "nt", BF16, name="mm_d_ys")
    gw["c_out"] = _mm(ys, dh3, "tn", BF16, name="mm_dw_c_out")
    dzz, gs["w_s"], dbs, dlg, dlb = _sgu_bwd(zz, dys, small["ln_g"], small["ln_b"], small["w_s"], small["b_s"])
    gs["b_s"], gs["ln_g"], gs["ln_b"] = dbs[:, :, 0], dlg[0], dlb[0]
    dhn1 = _mm(dzz, wt["c_in_t"], "nn", F32, name="mm_d_hn1")
    gw["c_in_t"] = _mm(dzz, hn1, "tn", BF16, name="mm_dw_c_in")
    dh2, dga1 = _rms_bwd(dhn1, h2, small["attn_norm_g"][1], dh3)

    dh1, dgf0, d_up0, d_down0, dcw0, dcb0 = ffn_bwd(dh2, h1, 0, ffn0)

    dy = _mm(dh1, wt["ab_out"], "nt", BF16, name="mm_d_y")
    gw["ab_out"] = _mm(y, dh1, "tn", BF16, name="mm_dw_ab_out")
    dz_a = _ret_bwd(z_a, dy, states, tables)
    do_b = _to_heads(dy[:, RET_HEADS * RET_V_DIM:])[0]
    dq, dk, dv, dbias = _att_bwd(qkv[0], qkv[1], qkv[2], bias, do_b)
    dz_b = _from_heads(jnp.stack([dq, dk, dv]))
    gs["rel_bias"] = _bias_reduce(_bias_untile(dbias))
    dhn0 = _mm(dz_a, wt["ab_in_t"][:RET_W], "nn", F32, name="mm_d_hn0")
    dhn0 = _mm(dz_b, wt["ab_in_t"][RET_W:], "nn", F32, res=dhn0, name="mm_d_hn0_acc")
    gw["ab_in_t"] = jnp.concatenate([_mm(dz_a, hn0, "tn", BF16, name="mm_dw_ab_in_a"),
                                     _mm(dz_b, hn0, "tn", BF16, name="mm_dw_ab_in_b")], axis=0)
    grad_x, dga0 = _rms_bwd(dhn0, x, small["attn_norm_g"][0], dh1)

    gw["up_t"] = [d_up0, d_up1]
    gw["down"] = [d_down0, d_down1]
    gs["attn_norm_g"] = jnp.concatenate([dga0, dga1], axis=0)
    gs["ffn_norm_g"] = jnp.concatenate([dgf0, dgf1], axis=0)
    gs["conv_w"] = jnp.stack([dcw0, dcw1])
    gs["conv_b"] = jnp.concatenate([dcb0, dcb1], axis=0)
    gs["final_g"] = gs["final_g"][0]
    return loss_vec[0, 0], grad_x, gw, gs


MESH_ID = pl.DeviceIdType.MESH
ANY = pl.BlockSpec(memory_space=pl.ANY)


def _my_place():
    return lax.axis_index("x"), lax.axis_index("y"), lax.axis_index("c")


def _all_gather(shards, name="all_gather"):
    n = len(shards)

    def body(*refs):
        x_refs, o_refs = refs[:n], refs[n:2 * n]
        send_sems, recv_sems, local_sems = refs[2 * n:]
        x, y, c = _my_place()
        me, sibling = (x, y, c), (x, y, 1 - c)
        chips = [(1 - x, y), (x, 1 - y), (1 - x, 1 - y)]

        def rows(a, place):
            m = x_refs[a].shape[0]
            px, py, pc = place
            return o_refs[a].at[pl.ds((4 * px + 2 * py + pc) * m, m), :]

        def copy(a, k, block, to, own=False):
            return pltpu.make_async_remote_copy(
                src_ref=x_refs[a] if own else rows(a, block), dst_ref=rows(a, block),
                send_sem=send_sems.at[a, k], recv_sem=recv_sems.at[a, k], device_id=to, device_id_type=MESH_ID)

        mine = [pltpu.make_async_copy(x_refs[a], rows(a, me), local_sems.at[a]) for a in range(n)]
        for cp in mine:
            cp.start()
        first = []
        for a in range(n):
            first.append(copy(a, 0, me, sibling, own=True))
            first += [copy(a, 1 + j, me, (*chip, c), own=True) for j, chip in enumerate(chips)]
        for cp in first:
            cp.start()
        passed = []
        for j, chip in enumerate(chips):
            for a in range(n):
                copy(a, 1 + j, (*chip, c), me).wait_recv()
                fw = copy(a, 4 + j, (*chip, c), sibling)
                fw.start()
                passed.append(fw)
        for a in range(n):
            copy(a, 0, sibling, me).wait_recv()
            for j, chip in enumerate(chips):
                copy(a, 4 + j, (*chip, 1 - c), me).wait_recv()
        for cp in first + passed:
            cp.wait_send()
        for cp in mine:
            cp.wait()

    return pl.pallas_call(
        body,
        out_shape=[jax.ShapeDtypeStruct((N_DEV * s.shape[0], s.shape[1]), s.dtype) for s in shards],
        in_specs=[ANY] * n,
        out_specs=[ANY] * n,
        scratch_shapes=[pltpu.SemaphoreType.DMA((n, 7)), pltpu.SemaphoreType.DMA((n, 7)), pltpu.SemaphoreType.DMA((n,))],
        name=name,
    )(*shards)


def _all_to_all(fulls, name="all_to_all"):
    n = len(fulls)

    def body(*refs):
        g_refs, o_refs = refs[:n], refs[n:2 * n]
        send_sems, recv_sems, local_sems = refs[2 * n:]
        x, y, c = _my_place()
        me = 4 * x + 2 * y + c
        peers = [(x ^ ((k >> 2) & 1), y ^ ((k >> 1) & 1), c ^ (k & 1)) for k in range(1, N_DEV)]

        def block(a, idx):
            m = g_refs[a].shape[0] // N_DEV
            return g_refs[a].at[pl.ds(idx * m, m), :]

        def copy(a, k, slot):
            px, py, pc = peers[k]
            return pltpu.make_async_remote_copy(
                src_ref=block(a, 4 * px + 2 * py + pc), dst_ref=o_refs[a].at[slot],
                send_sem=send_sems.at[a, k], recv_sem=recv_sems.at[a, k], device_id=peers[k], device_id_type=MESH_ID)

        mine = [pltpu.make_async_copy(block(a, me), o_refs[a].at[me], local_sems.at[a]) for a in range(n)]
        for cp in mine:
            cp.start()
        sends = [copy(a, k, me) for k in range(N_DEV - 1) for a in range(n)]
        for cp in sends:
            cp.start()
        for k in range(N_DEV - 1):
            px, py, pc = peers[k]
            for a in range(n):
                copy(a, k, 4 * px + 2 * py + pc).wait_recv()
        for cp in sends:
            cp.wait_send()
        for cp in mine:
            cp.wait()

    return pl.pallas_call(
        body,
        out_shape=[jax.ShapeDtypeStruct((N_DEV, g.shape[0] // N_DEV, g.shape[1]), g.dtype) for g in fulls],
        in_specs=[ANY] * n,
        out_specs=[ANY] * n,
        scratch_shapes=[pltpu.SemaphoreType.DMA((n, 7)), pltpu.SemaphoreType.DMA((n, 7)), pltpu.SemaphoreType.DMA((n,))],
        name=name,
    )(*fulls)


def _row_tile(r, target=256):
    best = None
    for t in range(8, min(r, target) + 1, 8):
        if r % t == 0:
            best = t
    return best if best is not None else r


def _sum8(parts, name="sum8"):
    _, M, N = parts.shape
    tr = _row_tile(M, 128)

    def body(p_ref, o_ref):
        acc = p_ref[0].astype(F32)
        for d in range(1, N_DEV):
            acc = acc + p_ref[d].astype(F32)
        o_ref[...] = acc

    return pl.pallas_call(
        body,
        grid=(M // tr,),
        in_specs=[pl.BlockSpec((N_DEV, tr, N), lambda i: (0, i, 0))],
        out_specs=pl.BlockSpec((tr, N), lambda i: (i, 0)),
        out_shape=jax.ShapeDtypeStruct((M, N), F32),
        compiler_params=_cp("parallel"),
        name=name,
    )(parts)


def _adamw(w, g, m, v, name="adamw"):
    shape = w.shape
    if w.ndim == 1:
        r2 = (1, shape[0])
    else:
        r2 = (int(np.prod(shape[:-1])), shape[-1])
    R, C = r2
    tr = _row_tile(R)
    bc1 = 1.0 - ADAM_B1 ** ADAM_STEP
    bc2 = 1.0 - ADAM_B2 ** ADAM_STEP

    def body(w_ref, g_ref, m_ref, v_ref, d_ref, nm_ref, nv_ref):
        gv = g_ref[...]
        nm = ADAM_B1 * m_ref[...] + (1.0 - ADAM_B1) * gv
        nv = ADAM_B2 * v_ref[...] + (1.0 - ADAM_B2) * (gv * gv)
        d_ref[...] = -ADAM_LR * ((nm / bc1) / (jnp.sqrt(nv / bc2) + ADAM_EPS) + ADAM_WD * w_ref[...])
        nm_ref[...] = nm
        nv_ref[...] = nv

    spec = pl.BlockSpec((tr, C), lambda i: (i, 0))
    out = pl.pallas_call(
        body,
        grid=(R // tr,),
        in_specs=[spec] * 4,
        out_specs=[spec] * 3,
        out_shape=[jax.ShapeDtypeStruct(r2, F32)] * 3,
        compiler_params=_cp("parallel"),
        name=name,
    )(w.reshape(r2), g.reshape(r2), m.reshape(r2), v.reshape(r2))
    return [o.reshape(shape) for o in out]


WEIGHTS = ['attn_norm_g', 'ffn_norm_g', 'ab_w_in', 'ab_w_out', 'ab_rel_bias', 'c_w_in', 'c_ln_g', 'c_ln_b', 'c_w_s', 'c_b_s',
           'c_w_out', 'ffn_w_up', 'ffn_conv_w', 'ffn_conv_b', 'ffn_w_down', 'final_norm_g']
SMALL_ORDER = ["attn_norm_g", "ffn_norm_g", "rel_bias", "ln_g", "ln_b", "w_s", "b_s", "conv_w", "conv_b", "final_g"]
PACK_ROW = 1024


def _pack(arrs):
    flat = jnp.concatenate([a.reshape(-1) for a in arrs])
    n = flat.shape[0]
    padded = -(-n // PACK_ROW) * PACK_ROW
    return jnp.pad(flat, (0, padded - n)).reshape(padded // 128, 128)


def _unpack(flat, shapes):
    out, off = [], 0
    for s in shapes:
        n = int(np.prod(s))
        out.append(flat[off:off + n].reshape(s))
        off += n
    return out


def _step(P):
    x, target = P["x"][0], P["loss_target"][0]
    me = 4 * lax.axis_index("x") + 2 * lax.axis_index("y") + lax.axis_index("c")
    n_up = P["ffn_w_up"].shape[0]
    Fc = P["ffn_conv_w"].shape[-1]
    Lc = P["c_ln_g"].shape[-1]

    shards = [P["ab_w_in"][0].T.astype(BF16), P["ab_w_out"][0].astype(BF16), P["c_w_in"][0].T.astype(BF16),
              P["c_w_out"][0].astype(BF16)]
    shards += [P["ffn_w_up"][l].T.astype(BF16) for l in range(n_up)]
    shards += [P["ffn_w_down"][l].astype(BF16) for l in range(n_up)]
    shards.append(_pack([P["ffn_conv_w"], P["c_ln_g"], P["c_ln_b"]]))
    full = _all_gather(shards, name="gather_weights")
    wt = {"ab_in_t": full[0], "ab_out": full[1], "c_in_t": full[2], "c_out": full[3],
          "up_t": full[4:4 + n_up], "down": full[4 + n_up:4 + 2 * n_up]}
    sm = full[-1].reshape(N_DEV, -1)
    conv_w = sm[:, :n_up * 3 * Fc].reshape(N_DEV, n_up, 3, Fc).transpose(1, 2, 0, 3).reshape(n_up, 3, N_DEV * Fc)
    off = n_up * 3 * Fc
    ln_g = sm[:, off:off + Lc].reshape(N_DEV * Lc)
    ln_b = sm[:, off + Lc:off + 2 * Lc].reshape(N_DEV * Lc)
    small = {"attn_norm_g": P["attn_norm_g"], "ffn_norm_g": P["ffn_norm_g"], "rel_bias": P["ab_rel_bias"][0],
             "ln_g": ln_g, "ln_b": ln_b, "w_s": P["c_w_s"][0], "b_s": P["c_b_s"][0], "conv_w": conv_w,
             "conv_b": P["ffn_conv_b"], "final_g": P["final_norm_g"]}

    loss_part, grad_x, gw, gs = _local_step(x, target, wt, small)
    loss = lax.psum(loss_part, ("x", "y", "c"))

    big = [gw["ab_in_t"], gw["ab_out"], gw["c_in_t"], gw["c_out"]] + list(gw["up_t"]) + list(gw["down"])
    recv = _all_to_all(big, name="exchange_grads")
    summed = [_sum8(r, name="sum8") for r in recv]
    g_big = {"ab_w_in": summed[0].T[None], "ab_w_out": summed[1][None], "c_w_in": summed[2].T[None], "c_w_out": summed[3][None],
             "ffn_w_up": jnp.stack([s.T for s in summed[4:4 + n_up]]), "ffn_w_down": jnp.stack(summed[4 + n_up:4 + 2 * n_up])}

    packed = _pack([gs[k] for k in SMALL_ORDER])
    gathered = _all_gather([packed], name="gather_small_grads")[0]
    tot = _sum8(gathered.reshape(N_DEV, packed.shape[0], 128), name="sum8_small").reshape(-1)
    gsm = dict(zip(SMALL_ORDER, _unpack(tot, [gs[k].shape for k in SMALL_ORDER])))
    grads = dict(g_big)
    grads["attn_norm_g"] = gsm["attn_norm_g"]
    grads["ffn_norm_g"] = gsm["ffn_norm_g"]
    grads["ab_rel_bias"] = gsm["rel_bias"][None]
    grads["c_ln_g"] = lax.dynamic_slice(gsm["ln_g"], (me * Lc,), (Lc,))[None]
    grads["c_ln_b"] = lax.dynamic_slice(gsm["ln_b"], (me * Lc,), (Lc,))[None]
    grads["c_w_s"] = gsm["w_s"][None]
    grads["c_b_s"] = gsm["b_s"][None]
    grads["ffn_conv_w"] = lax.dynamic_slice(gsm["conv_w"], (0, 0, me * Fc), (n_up, 3, Fc))
    grads["ffn_conv_b"] = gsm["conv_b"]
    grads["final_norm_g"] = gsm["final_g"]

    delta, new_m, new_v = {}, {}, {}
    for k in WEIGHTS:
        delta[k], new_m[k], new_v[k] = _adamw(P[k], grads[k], P["m_" + k], P["v_" + k], name="adamw")
    return (loss, grad_x[None], *[grads[k] for k in WEIGHTS], *[delta[k] for k in WEIGHTS],
            *[new_m[k] for k in WEIGHTS], *[new_v[k] for k in WEIGHTS])


def kernel(x, attn_norm_g, ffn_norm_g, ab_w_in, ab_w_out, ab_rel_bias, c_w_in, c_ln_g, c_ln_b, c_w_s, c_b_s, c_w_out, ffn_w_up, ffn_conv_w, ffn_conv_b, ffn_w_down, final_norm_g, loss_target, m_attn_norm_g, m_ffn_norm_g, m_ab_w_in, m_ab_w_out, m_ab_rel_bias, m_c_w_in, m_c_ln_g, m_c_ln_b, m_c_w_s, m_c_b_s, m_c_w_out, m_ffn_w_up, m_ffn_conv_w, m_ffn_conv_b, m_ffn_w_down, m_final_norm_g, v_attn_norm_g, v_ffn_norm_g, v_ab_w_in, v_ab_w_out, v_ab_rel_bias, v_c_w_in, v_c_ln_g, v_c_ln_b, v_c_w_s, v_c_b_s, v_c_w_out, v_ffn_w_up, v_ffn_conv_w, v_ffn_conv_b, v_ffn_w_down, v_final_norm_g):
    return _step(dict(locals()))
```

```python
import functools

import numpy as np
import jax
import jax.numpy as jnp
from jax import lax
from jax.experimental import pallas as pl
from jax.experimental.pallas import tpu as pltpu

F32 = jnp.float32
BF16 = jnp.bfloat16

D_MODEL = 1024
CHUNK = 64
EPS = 1e-6
NEG_INF = -1e30
RET_HEADS = 4
RET_QK_DIM = 128
RET_V_DIM = 256
ATT_HEADS = 8
ATT_HEAD_DIM = 64
ATT_PAST = 8
ATT_BAND = (ATT_PAST + 1) * CHUNK
MAX_REL = 128
N_REL = 2 * MAX_REL + 1
N_REL_PAD = 384
SGU_BLOCK = 128
SGU_GROUPS = 8
SGU_WIDTH = 2048
SGU_GW = SGU_WIDTH // SGU_GROUPS
FFN_HIDDEN = 2816
RET_W = 2 * RET_HEADS * RET_QK_DIM + 2 * RET_HEADS * RET_V_DIM
ATT_W = 3 * ATT_HEADS * ATT_HEAD_DIM
N_DEV = 8

ADAM_LR = 0.001
ADAM_B1 = 0.9
ADAM_B2 = 0.999
ADAM_EPS = 1e-08
ADAM_WD = 0.01
ADAM_STEP = 10

VMEM_LIMIT = 52 * 1024 * 1024


def _cp(*sem):
    return pltpu.CompilerParams(dimension_semantics=sem if sem else None, vmem_limit_bytes=VMEM_LIMIT)


def _tile(n, target):
    if n <= target:
        return n
    best = None
    for t in range(128, target + 1, 128):
        if n % t == 0:
            best = t
    assert best is not None, (n, target)
    return best


def _gelu(x):
    c = 0.7978845608028654
    return 0.5 * x * (1.0 + jnp.tanh(c * (x + 0.044715 * x * x * x)))


def _gelu_and_grad(x):
    c = 0.7978845608028654
    x2 = x * x
    t = jnp.tanh(c * (x + 0.044715 * x * x2))
    cdf = 0.5 * (1.0 + t)
    grad = cdf + x * (0.5 * c) * (1.0 - t * t) * (1.0 + 3.0 * 0.044715 * x2)
    return x * cdf, grad


def _dot(a, b, dims):
    return lax.dot_general(a, b, (dims, ((), ())), preferred_element_type=F32)


NN = ((1,), (0,))
NT = ((1,), (1,))
TN = ((0,), (0,))


def _mm(a, b, mode, out_dtype, res=None, name="mm", tm_t=None, tn_t=None, tk_t=None):
    if mode == "nt":
        (M, K), N = a.shape, b.shape[0]
        dm, dn, dk = 1024, 1408, K
    elif mode == "nn":
        (M, K), N = a.shape, b.shape[1]
        dm, dn, dk = (1024 if K <= 3072 else 512), 1024, K
    else:
        (K, M), N = a.shape, b.shape[1]
        dm, dn, dk = 1536, 1024, 1024
    tm, tn, tk = _tile(M, tm_t or dm), _tile(N, tn_t or dn), _tile(K, tk_t or dk)
    nk = K // tk
    dims = {"nt": NT, "nn": NN, "tn": TN}[mode]
    a_spec = pl.BlockSpec((tk, tm), lambda i, j, k: (k, i)) if mode == "tn" else pl.BlockSpec((tm, tk), lambda i, j, k: (i, k))
    b_spec = pl.BlockSpec((tn, tk), lambda i, j, k: (j, k)) if mode == "nt" else pl.BlockSpec((tk, tn), lambda i, j, k: (k, j))
    o_spec = pl.BlockSpec((tm, tn), lambda i, j, k: (i, j))
    has_res = res is not None

    def body(*refs):
        if has_res:
            a_ref, b_ref, r_ref, o_ref = refs[:4]
        else:
            a_ref, b_ref, o_ref = refs[:3]
        p = _dot(a_ref[...].astype(BF16), b_ref[...].astype(BF16), dims)
        if nk == 1:
            if has_res:
                p = p + r_ref[...]
            o_ref[...] = p.astype(out_dtype)
            return
        acc = refs[-1]
        k = pl.program_id(2)

        @pl.when(k == 0)
        def _():
            acc[...] = p

        @pl.when(k > 0)
        def _():
            acc[...] += p

        @pl.when(k == nk - 1)
        def _():
            t = acc[...]
            if has_res:
                t = t + r_ref[...]
            o_ref[...] = t.astype(out_dtype)

    in_specs = [a_spec, b_spec] + ([o_spec] if has_res else [])
    args = (a, b) + ((res,) if has_res else ())
    return pl.pallas_call(
        body,
        grid=(M // tm, N // tn, nk),
        in_specs=in_specs,
        out_specs=o_spec,
        out_shape=jax.ShapeDtypeStruct((M, N), out_dtype),
        scratch_shapes=[pltpu.VMEM((tm, tn), F32)] if nk > 1 else [],
        compiler_params=_cp("parallel", "parallel", "arbitrary"),
        name=name,
    )(*args)


def _rms_fwd(h, g, name="rms_fwd", tm=512):
    T, Dm = h.shape
    tm = min(tm, T)

    def body(h_ref, g_ref, o_ref):
        x = h_ref[...]
        r = lax.rsqrt(jnp.mean(x * x, axis=-1, keepdims=True) + EPS)
        o_ref[...] = (x * r * g_ref[...]).astype(o_ref.dtype)

    return pl.pallas_call(
        body,
        grid=(T // tm,),
        in_specs=[pl.BlockSpec((tm, Dm), lambda i: (i, 0)), pl.BlockSpec((1, Dm), lambda i: (0, 0))],
        out_specs=pl.BlockSpec((tm, Dm), lambda i: (i, 0)),
        out_shape=jax.ShapeDtypeStruct((T, Dm), BF16),
        compiler_params=_cp("parallel"),
        name=name,
    )(h, g.reshape(1, Dm))


def _rms_bwd(dhn, h, g, dres, name="rms_bwd", tm=512):
    T, Dm = h.shape
    tm = min(tm, T)

    def body(dhn_ref, h_ref, g_ref, dres_ref, dh_ref, dg_ref):
        i = pl.program_id(0)
        x = h_ref[...]
        r = lax.rsqrt(jnp.mean(x * x, axis=-1, keepdims=True) + EPS)
        xhat = x * r
        d = dhn_ref[...].astype(F32)

        @pl.when(i == 0)
        def _():
            dg_ref[...] = jnp.zeros_like(dg_ref)

        dg_ref[...] += jnp.sum(d * xhat, axis=0, keepdims=True)
        dx = d * g_ref[...]
        m = jnp.mean(dx * xhat, axis=-1, keepdims=True)
        dh_ref[...] = dres_ref[...] + r * (dx - xhat * m)

    row = pl.BlockSpec((tm, Dm), lambda i: (i, 0))
    vec = pl.BlockSpec((1, Dm), lambda i: (0, 0))
    return pl.pallas_call(
        body,
        grid=(T // tm,),
        in_specs=[row, row, vec, row],
        out_specs=[row, vec],
        out_shape=[jax.ShapeDtypeStruct((T, Dm), F32), jax.ShapeDtypeStruct((1, Dm), F32)],
        compiler_params=_cp("arbitrary"),
        name=name,
    )(dhn, h, g.reshape(1, Dm), dres)


def _final_loss(h, g, target, name="final_loss", tm=512):
    T, Dm = h.shape
    tm = min(tm, T)

    def body(h_ref, g_ref, t_ref, loss_ref, dh_ref, dg_ref):
        i = pl.program_id(0)
        x = h_ref[...]
        gv = g_ref[...]
        r = lax.rsqrt(jnp.mean(x * x, axis=-1, keepdims=True) + EPS)
        xhat = x * r
        e = xhat * gv - t_ref[...]

        @pl.when(i == 0)
        def _():
            loss_ref[...] = jnp.zeros_like(loss_ref)
            dg_ref[...] = jnp.zeros_like(dg_ref)

        loss_ref[...] += jnp.full((1, 128), 0.5 / Dm, F32) * jnp.sum(e * e)
        dy = e * (1.0 / Dm)
        dg_ref[...] += jnp.sum(dy * xhat, axis=0, keepdims=True)
        dx = dy * gv
        m = jnp.mean(dx * xhat, axis=-1, keepdims=True)
        dh_ref[...] = r * (dx - xhat * m)

    row = pl.BlockSpec((tm, Dm), lambda i: (i, 0))
    vec = pl.BlockSpec((1, Dm), lambda i: (0, 0))
    return pl.pallas_call(
        body,
        grid=(T // tm,),
        in_specs=[row, vec, row],
        out_specs=[pl.BlockSpec((1, 128), lambda i: (0, 0)), row, vec],
        out_shape=[jax.ShapeDtypeStruct((1, 128), F32), jax.ShapeDtypeStruct((T, Dm), F32), jax.ShapeDtypeStruct((1, Dm), F32)],
        compiler_params=_cp("arbitrary"),
        name=name,
    )(h, g.reshape(1, Dm), target)


HALO = 16


def _conv3(ext, w_ref, b_ref):
    return w_ref[0:1, :] * pltpu.roll(ext, 2, 0) + w_ref[1:2, :] * pltpu.roll(ext, 1, 0) + w_ref[2:3, :] * ext + b_ref[...]


def _ffn_mid_fwd(zg, zu, cw, cb, name="ffn_mid_fwd", tm=256, tc=1408):
    T, Fh = zg.shape
    tm = min(tm, T)
    nj = Fh // tc
    hb = tm // HALO

    def body(zg_ref, zu_ref, zgp_ref, zup_ref, wg_ref, wu_ref, bg_ref, bu_ref, a_ref):
        i = pl.program_id(1)
        first = i == 0

        def conv(z_ref, zp_ref, w_ref, b_ref):
            prev = jnp.where(first, 0.0, zp_ref[...].astype(F32))
            ext = jnp.concatenate([prev, z_ref[...].astype(F32)], axis=0)
            return _conv3(ext, w_ref, b_ref)[HALO:]

        cg = conv(zg_ref, zgp_ref, wg_ref, bg_ref)
        cu = conv(zu_ref, zup_ref, wu_ref, bu_ref)
        a_ref[...] = (_gelu(cg) * cu).astype(a_ref.dtype)

    cur = pl.BlockSpec((tm, tc), lambda j, i: (i, j))
    prev = pl.BlockSpec((HALO, tc), lambda j, i: (jnp.maximum(i * hb - 1, 0), j))
    wg = pl.BlockSpec((3, tc), lambda j, i: (0, j))
    wu = pl.BlockSpec((3, tc), lambda j, i: (0, j + nj))
    bg = pl.BlockSpec((1, tc), lambda j, i: (0, j))
    bu = pl.BlockSpec((1, tc), lambda j, i: (0, j + nj))
    return pl.pallas_call(
        body,
        grid=(nj, T // tm),
        in_specs=[cur, cur, prev, prev, wg, wu, bg, bu],
        out_specs=cur,
        out_shape=jax.ShapeDtypeStruct((T, Fh), BF16),
        compiler_params=_cp("parallel", "parallel"),
        name=name,
    )(zg, zu, zg, zu, cw, cw, cb, cb)


def _ffn_mid_bwd(zg, zu, da, cw, cb, comm=None, name="ffn_mid_bwd", tm=512, tc=256):
    T, Fh = zg.shape
    tm = min(tm, T)
    nj = Fh // tc
    hb = tm // HALO
    nhb = T // HALO

    nc = comm.n if comm else 0
    ni = T // tm

    def body(*refs):
        zg_ref, zu_ref, zgp_ref, zup_ref, zgn_ref, zun_ref, da_ref, dan_ref, wg_ref, wu_ref, bg_ref, bu_ref = refs[:12]
        cin = refs[12:12 + nc]
        dzg_ref, dzu_ref, dwg_ref, dwu_ref, dbg_ref, dbu_ref = refs[12 + nc:18 + nc]
        cout = refs[18 + nc:18 + 2 * nc]
        csem = refs[18 + 2 * nc:]
        i = pl.program_id(1)
        step = pl.program_id(0) * ni + i
        _host_comm(comm, "early", step, nj * ni, cin, cout, csem)
        first = i == 0
        last = i == ni - 1

        @pl.when(first)
        def _():
            dwg_ref[...] = jnp.zeros_like(dwg_ref)
            dwu_ref[...] = jnp.zeros_like(dwu_ref)
            dbg_ref[...] = jnp.zeros_like(dbg_ref)
            dbu_ref[...] = jnp.zeros_like(dbu_ref)

        def ext_of(p_ref, c_ref, n_ref):
            p = jnp.where(first, 0.0, p_ref[...].astype(F32))
            return jnp.concatenate([p, c_ref[...].astype(F32), n_ref[...].astype(F32)], axis=0)

        zge = ext_of(zgp_ref, zg_ref, zgn_ref)
        zue = ext_of(zup_ref, zu_ref, zun_ref)
        dan = jnp.where(last, 0.0, dan_ref[...].astype(F32))
        dae = jnp.concatenate([jnp.zeros((HALO, tc), F32), da_ref[...].astype(F32), dan], axis=0)
        cg = _conv3(zge, wg_ref, bg_ref)
        cu = _conv3(zue, wu_ref, bu_ref)
        gel, dgel = _gelu_and_grad(cg)
        dcg = dae * cu * dgel
        dcu = dae * gel
        lo, hi = HALO, HALO + tm

        def back(dc, ze, w_ref, dz_ref, dw_ref, db_ref):
            n = dc.shape[0]
            dz = w_ref[2:3, :] * dc + w_ref[1:2, :] * pltpu.roll(dc, n - 1, 0) + w_ref[0:1, :] * pltpu.roll(dc, n - 2, 0)
            dz_ref[...] = dz[lo:hi].astype(dz_ref.dtype)
            dcc = dc[lo:hi]
            db_ref[...] += jnp.sum(dcc, axis=0, keepdims=True)
            dw_ref[0:1, :] += jnp.sum(dcc * pltpu.roll(ze, 2, 0)[lo:hi], axis=0, keepdims=True)
            dw_ref[1:2, :] += jnp.sum(dcc * pltpu.roll(ze, 1, 0)[lo:hi], axis=0, keepdims=True)
            dw_ref[2:3, :] += jnp.sum(dcc * ze[lo:hi], axis=0, keepdims=True)

        back(dcg, zge, wg_ref, dzg_ref, dwg_ref, dbg_ref)
        back(dcu, zue, wu_ref, dzu_ref, dwu_ref, dbu_ref)
        _host_comm(comm, "late", step, nj * ni, cin, cout, csem)

    cur = pl.BlockSpec((tm, tc), lambda j, i: (i, j))
    prev = pl.BlockSpec((HALO, tc), lambda j, i: (jnp.maximum(i * hb - 1, 0), j))
    nxt = pl.BlockSpec((HALO, tc), lambda j, i: (jnp.minimum((i + 1) * hb, nhb - 1), j))
    wg = pl.BlockSpec((3, tc), lambda j, i: (0, j))
    wu = pl.BlockSpec((3, tc), lambda j, i: (0, j + nj))
    bg = pl.BlockSpec((1, tc), lambda j, i: (0, j))
    bu = pl.BlockSpec((1, tc), lambda j, i: (0, j + nj))
    dw = pl.BlockSpec((3, tc), lambda j, i: (0, j))
    db = pl.BlockSpec((1, tc), lambda j, i: (0, j))
    out = pl.pallas_call(
        body,
        grid=(nj, ni),
        in_specs=[cur, cur, prev, prev, nxt, nxt, cur, nxt, wg, wu, bg, bu] + [ANY] * nc,
        out_specs=[cur, cur, dw, dw, db, db] + [ANY] * nc,
        out_shape=[jax.ShapeDtypeStruct((T, Fh), BF16), jax.ShapeDtypeStruct((T, Fh), BF16),
                   jax.ShapeDtypeStruct((3, Fh), F32), jax.ShapeDtypeStruct((3, Fh), F32),
                   jax.ShapeDtypeStruct((1, Fh), F32), jax.ShapeDtypeStruct((1, Fh), F32)] + (comm.out_shape if comm else []),
        scratch_shapes=comm.scratch if comm else [],
        compiler_params=_cp("arbitrary", "arbitrary"),
        name=name,
    )(zg, zu, zg, zu, zg, zu, da, da, cw, cw, cb, cb, *(comm.arrays if comm else []))
    dzg, dzu, dwg, dwu, dbg, dbu = out[:6]
    return dzg, dzu, jnp.concatenate([dwg, dwu], axis=1), jnp.concatenate([dbg, dbu], axis=1), list(out[6:])


def _sgu_mask():
    r = lax.broadcasted_iota(jnp.int32, (SGU_BLOCK, SGU_BLOCK), 0)
    c = lax.broadcasted_iota(jnp.int32, (SGU_BLOCK, SGU_BLOCK), 1)
    return (c < CHUNK) | (r >= CHUNK)


def _sgu_fwd(zz, ln_g, ln_b, w_s, b_s, name="sgu_fwd", tm=256):
    T = zz.shape[0]
    tm = min(tm, T)
    W = SGU_WIDTH

    def body(zu_ref, zv_ref, g_ref, b_ref, ws_ref, bs_ref, y_ref):
        u = _gelu(zu_ref[...].astype(F32))
        v = _gelu(zv_ref[...].astype(F32))
        mu = jnp.mean(v, axis=-1, keepdims=True)
        xc = v - mu
        rstd = lax.rsqrt(jnp.mean(xc * xc, axis=-1, keepdims=True) + EPS)
        vn = (xc * rstd * g_ref[...] + b_ref[...]).astype(BF16)
        mask = _sgu_mask()
        for g in range(SGU_GROUPS):
            wm = jnp.where(mask, ws_ref[g], 0.0).astype(BF16)
            cs = slice(g * SGU_GW, (g + 1) * SGU_GW)
            for blk in range(tm // SGU_BLOCK):
                rs = slice(blk * SGU_BLOCK, (blk + 1) * SGU_BLOCK)
                mixed = _dot(wm, vn[rs, cs], NN) + bs_ref[g]
                y_ref[rs, cs] = (u[rs, cs] * mixed).astype(y_ref.dtype)

    return pl.pallas_call(
        body,
        grid=(T // tm,),
        in_specs=[pl.BlockSpec((tm, W), lambda i: (i, 0)), pl.BlockSpec((tm, W), lambda i: (i, 1)),
                  pl.BlockSpec((1, W), lambda i: (0, 0)), pl.BlockSpec((1, W), lambda i: (0, 0)),
                  pl.BlockSpec((SGU_GROUPS, SGU_BLOCK, SGU_BLOCK), lambda i: (0, 0, 0)),
                  pl.BlockSpec((SGU_GROUPS, SGU_BLOCK, 1), lambda i: (0, 0, 0))],
        out_specs=pl.BlockSpec((tm, W), lambda i: (i, 0)),
        out_shape=jax.ShapeDtypeStruct((T, W), BF16),
        compiler_params=_cp("parallel"),
        name=name,
    )(zz, zz, ln_g.reshape(1, W), ln_b.reshape(1, W), w_s, b_s.reshape(SGU_GROUPS, SGU_BLOCK, 1))


def _sgu_bwd(zz, dy, ln_g, ln_b, w_s, b_s, name="sgu_bwd", tm=256):
    T = zz.shape[0]
    tm = min(tm, T)
    W = SGU_WIDTH

    def body(zu_ref, zv_ref, dy_ref, g_ref, b_ref, ws_ref, bs_ref, dzz_ref, dws_ref, dbs_ref, dg_ref, db_ref, dvn_ref):
        i = pl.program_id(0)

        @pl.when(i == 0)
        def _():
            dws_ref[...] = jnp.zeros_like(dws_ref)
            dbs_ref[...] = jnp.zeros_like(dbs_ref)
            dg_ref[...] = jnp.zeros_like(dg_ref)
            db_ref[...] = jnp.zeros_like(db_ref)

        u, du_dz = _gelu_and_grad(zu_ref[...].astype(F32))
        v, dv_dz = _gelu_and_grad(zv_ref[...].astype(F32))
        mu = jnp.mean(v, axis=-1, keepdims=True)
        xc = v - mu
        rstd = lax.rsqrt(jnp.mean(xc * xc, axis=-1, keepdims=True) + EPS)
        xhat = xc * rstd
        gv = g_ref[...]
        vn = (xhat * gv + b_ref[...]).astype(BF16)
        dyv = dy_ref[...].astype(F32)
        mask = _sgu_mask()
        for g in range(SGU_GROUPS):
            wm = jnp.where(mask, ws_ref[g], 0.0).astype(BF16)
            cs = slice(g * SGU_GW, (g + 1) * SGU_GW)
            dw_acc = jnp.zeros((SGU_BLOCK, SGU_BLOCK), F32)
            db_acc = jnp.zeros((SGU_BLOCK, 1), F32)
            for blk in range(tm // SGU_BLOCK):
                rs = slice(blk * SGU_BLOCK, (blk + 1) * SGU_BLOCK)
                vn_bg = vn[rs, cs]
                mixed = _dot(wm, vn_bg, NN) + bs_ref[g]
                dy_bg = dyv[rs, cs]
                dmixed = dy_bg * u[rs, cs]
                dmb = dmixed.astype(BF16)
                dw_acc += _dot(dmb, vn_bg, NT)
                db_acc += jnp.sum(dmixed, axis=1, keepdims=True)
                dvn_ref[rs, cs] = _dot(wm, dmb, TN)
                dzz_ref[rs, cs] = (dy_bg * mixed * du_dz[rs, cs]).astype(dzz_ref.dtype)
            dws_ref[g] += jnp.where(mask, dw_acc, 0.0)
            dbs_ref[g] += db_acc
        dvn = dvn_ref[...]
        dg_ref[...] += jnp.sum(dvn * xhat, axis=0, keepdims=True)
        db_ref[...] += jnp.sum(dvn, axis=0, keepdims=True)
        dxh = dvn * gv
        m1 = jnp.mean(dxh, axis=-1, keepdims=True)
        m2 = jnp.mean(dxh * xhat, axis=-1, keepdims=True)
        dv = rstd * (dxh - m1 - xhat * m2)
        dzz_ref[:, W:] = (dv * dv_dz).astype(dzz_ref.dtype)

    vec = pl.BlockSpec((1, W), lambda i: (0, 0))
    ws_spec = pl.BlockSpec((SGU_GROUPS, SGU_BLOCK, SGU_BLOCK), lambda i: (0, 0, 0))
    bs_spec = pl.BlockSpec((SGU_GROUPS, SGU_BLOCK, 1), lambda i: (0, 0, 0))
    return pl.pallas_call(
        body,
        grid=(T // tm,),
        in_specs=[pl.BlockSpec((tm, W), lambda i: (i, 0)), pl.BlockSpec((tm, W), lambda i: (i, 1)),
                  pl.BlockSpec((tm, W), lambda i: (i, 0)), vec, vec, ws_spec, bs_spec],
        out_specs=[pl.BlockSpec((tm, 2 * W), lambda i: (i, 0)), ws_spec, bs_spec, vec, vec],
        out_shape=[jax.ShapeDtypeStruct((T, 2 * W), BF16),
                   jax.ShapeDtypeStruct((SGU_GROUPS, SGU_BLOCK, SGU_BLOCK), F32),
                   jax.ShapeDtypeStruct((SGU_GROUPS, SGU_BLOCK, 1), F32),
                   jax.ShapeDtypeStruct((1, W), F32), jax.ShapeDtypeStruct((1, W), F32)],
        scratch_shapes=[pltpu.VMEM((tm, W), F32)],
        compiler_params=_cp("arbitrary"),
        name=name,
    )(zz, zz, dy, ln_g.reshape(1, W), ln_b.reshape(1, W), w_s, b_s.reshape(SGU_GROUPS, SGU_BLOCK, 1))


RET_TR = 256
RET_BLK = 256
QK_SCALE = RET_QK_DIM ** -0.5


def _ret_tables(T):
    half = RET_QK_DIM // 2
    inv = 1.0 / (10000.0 ** jnp.linspace(0.0, 1.0, half, dtype=F32))
    ang = jnp.arange(T).astype(F32)[:, None] * inv[None, :]
    cos, sin = jnp.cos(ang), jnp.sin(ang)
    rot_c = jnp.concatenate([cos, cos], axis=1)
    rot_s = jnp.concatenate([-sin, sin], axis=1)
    log_g = jnp.log1p(-jnp.exp2(-5.0 - jnp.arange(RET_HEADS, dtype=F32)))
    idx = jnp.arange(RET_BLK, dtype=F32)
    dist = idx[:, None] - idx[None, :]
    cq, ck = jnp.arange(RET_BLK)[:, None] // CHUNK, jnp.arange(RET_BLK)[None, :] // CHUNK
    expo = jnp.where(ck == cq, jnp.abs(dist), dist)
    d_blk = jnp.where((ck <= cq)[None], jnp.exp(log_g[:, None, None] * expo[None]), 0.0)
    k_dec = jnp.exp(log_g[:, None] * (RET_BLK - 1 - idx)[None, :])[:, :, None]
    q_dec = jnp.exp(log_g[:, None] * (idx + 1.0)[None, :])[:, :, None]
    c_dec = jnp.exp(log_g * RET_BLK)[:, None, None]
    return rot_c, rot_s, d_blk, q_dec, k_dec, c_dec


def _rot(x, c, s):
    return x * c + pltpu.roll(x, RET_QK_DIM // 2, 1) * s


def _ret_specs(tr, rev, nb):
    ix = (lambda n: nb - 1 - n) if rev else (lambda n: n)
    tab = pl.BlockSpec((tr, RET_QK_DIM), lambda n: (ix(n), 0))
    dm = pl.BlockSpec((RET_HEADS, RET_BLK, RET_BLK), lambda n: (0, 0, 0))
    dv = pl.BlockSpec((RET_HEADS, RET_BLK, 1), lambda n: (0, 0, 0))
    dc = pl.BlockSpec((RET_HEADS, 1, 1), lambda n: (0, 0, 0))
    return ix, [tab, tab, dm, dv, dv, dc]


def _ret_fwd(z_a, tables, name="ret_fwd"):
    T = z_a.shape[0]
    tr = min(RET_TR, T)
    cpb = tr // RET_BLK
    nb = T // tr
    QW, VW = RET_HEADS * RET_QK_DIM, RET_HEADS * RET_V_DIM
    ix, tab_specs = _ret_specs(tr, False, nb)

    def body(z_ref, c_ref, s_ref, dm_ref, qd_ref, kd_ref, cd_ref, y_ref, st_ref, state):
        @pl.when(pl.program_id(0) == 0)
        def _():
            state[...] = jnp.zeros_like(state)

        for c in range(cpb):
            for h in range(RET_HEADS):
                rs = slice(c * RET_BLK, (c + 1) * RET_BLK)
                cc, ss = c_ref[rs, :], s_ref[rs, :]
                q = z_ref[rs, h * RET_QK_DIM:(h + 1) * RET_QK_DIM].astype(F32)
                k = z_ref[rs, QW + h * RET_QK_DIM:QW + (h + 1) * RET_QK_DIM].astype(F32)
                v = z_ref[rs, 2 * QW + h * RET_V_DIM:2 * QW + (h + 1) * RET_V_DIM]
                gt = z_ref[rs, 2 * QW + VW + h * RET_V_DIM:2 * QW + VW + (h + 1) * RET_V_DIM].astype(F32)
                qr = _rot(q, cc, ss)
                kr = _rot(k, cc, ss) * QK_SCALE
                s_old = state[h]
                sb = s_old.astype(BF16)
                st_ref[c, h] = sb
                s = _dot(qr.astype(BF16), kr.astype(BF16), NT) * dm_ref[h]
                o = _dot(s.astype(BF16), v, NN) + _dot((qr * qd_ref[h]).astype(BF16), sb, NN)
                state[h] = s_old * cd_ref[h] + _dot((kr * kd_ref[h]).astype(BF16), v, TN)
                mu = jnp.mean(o, axis=-1, keepdims=True)
                oc = o - mu
                rn = oc * lax.rsqrt(jnp.mean(oc * oc, axis=-1, keepdims=True) + EPS)
                silu = gt / (1.0 + jnp.exp(-gt))
                y_ref[rs, h * RET_V_DIM:(h + 1) * RET_V_DIM] = (silu * rn).astype(y_ref.dtype)

    return pl.pallas_call(
        body,
        grid=(nb,),
        in_specs=[pl.BlockSpec((tr, RET_W), lambda n: (n, 0))] + tab_specs,
        out_specs=[pl.BlockSpec((tr, VW), lambda n: (n, 0)),
                   pl.BlockSpec((cpb, RET_HEADS, RET_QK_DIM, RET_V_DIM), lambda n: (n, 0, 0, 0))],
        out_shape=[jax.ShapeDtypeStruct((T, Y_COLS), BF16),
                   jax.ShapeDtypeStruct((T // RET_BLK, RET_HEADS, RET_QK_DIM, RET_V_DIM), BF16)],
        scratch_shapes=[pltpu.VMEM((RET_HEADS, RET_QK_DIM, RET_V_DIM), F32)],
        compiler_params=_cp("arbitrary"),
        name=name,
    )(z_a, *tables)


def _ret_bwd(z_a, dy, states, tables, name="ret_bwd"):
    T = z_a.shape[0]
    tr = min(RET_TR, T)
    cpb = tr // RET_BLK
    nb = T // tr
    QW, VW = RET_HEADS * RET_QK_DIM, RET_HEADS * RET_V_DIM
    ix, tab_specs = _ret_specs(tr, True, nb)

    def body(z_ref, dy_ref, st_ref, c_ref, s_ref, dm_ref, qd_ref, kd_ref, cd_ref, dz_ref, dstate):
        @pl.when(pl.program_id(0) == 0)
        def _():
            dstate[...] = jnp.zeros_like(dstate)

        for c in reversed(range(cpb)):
            for h in range(RET_HEADS):
                rs = slice(c * RET_BLK, (c + 1) * RET_BLK)
                cc, ss = c_ref[rs, :], s_ref[rs, :]
                q = z_ref[rs, h * RET_QK_DIM:(h + 1) * RET_QK_DIM].astype(F32)
                k = z_ref[rs, QW + h * RET_QK_DIM:QW + (h + 1) * RET_QK_DIM].astype(F32)
                v = z_ref[rs, 2 * QW + h * RET_V_DIM:2 * QW + (h + 1) * RET_V_DIM]
                gt = z_ref[rs, 2 * QW + VW + h * RET_V_DIM:2 * QW + VW + (h + 1) * RET_V_DIM].astype(F32)
                dyv = dy_ref[rs, h * RET_V_DIM:(h + 1) * RET_V_DIM].astype(F32)
                dmat, qd, kd = dm_ref[h], qd_ref[h], kd_ref[h]
                qr = _rot(q, cc, ss)
                kr = _rot(k, cc, ss) * QK_SCALE
                qrb, krb = qr.astype(BF16), kr.astype(BF16)
                sb = st_ref[c, h]
                sd = (_dot(qrb, krb, NT) * dmat).astype(BF16)
                qdb = (qr * qd).astype(BF16)
                kdb = (kr * kd).astype(BF16)
                o = _dot(sd, v, NN) + _dot(qdb, sb, NN)
                mu = jnp.mean(o, axis=-1, keepdims=True)
                oc = o - mu
                rstd = lax.rsqrt(jnp.mean(oc * oc, axis=-1, keepdims=True) + EPS)
                rn = oc * rstd
                sg = 1.0 / (1.0 + jnp.exp(-gt))
                dgt = dyv * rn * (sg * (1.0 + gt * (1.0 - sg)))
                drn = dyv * (gt * sg)
                do = rstd * (drn - jnp.mean(drn, axis=-1, keepdims=True) - rn * jnp.mean(drn * rn, axis=-1, keepdims=True))
                dob = do.astype(BF16)
                dsn = dstate[h]
                dsnb = dsn.astype(BF16)
                ds_raw = (_dot(dob, v, NT) * dmat).astype(BF16)
                dv = _dot(sd, dob, TN) + _dot(kdb, dsnb, NN)
                dqr = _dot(ds_raw, krb, NN) + qd * _dot(dob, sb, NT)
                dkr = (_dot(ds_raw, qrb, TN) + kd * _dot(v, dsnb, NT)) * QK_SCALE
                dstate[h] = dsn * cd_ref[h] + _dot(qdb, dob, TN)
                dq = dqr * cc + pltpu.roll(dqr * ss, RET_QK_DIM // 2, 1)
                dk = dkr * cc + pltpu.roll(dkr * ss, RET_QK_DIM // 2, 1)
                dz_ref[rs, h * RET_QK_DIM:(h + 1) * RET_QK_DIM] = dq.astype(dz_ref.dtype)
                dz_ref[rs, QW + h * RET_QK_DIM:QW + (h + 1) * RET_QK_DIM] = dk.astype(dz_ref.dtype)
                dz_ref[rs, 2 * QW + h * RET_V_DIM:2 * QW + (h + 1) * RET_V_DIM] = dv.astype(dz_ref.dtype)
                dz_ref[rs, 2 * QW + VW + h * RET_V_DIM:2 * QW + VW + (h + 1) * RET_V_DIM] = dgt.astype(dz_ref.dtype)

    return pl.pallas_call(
        body,
        grid=(nb,),
        in_specs=[pl.BlockSpec((tr, RET_W), lambda n: (ix(n), 0)),
                  pl.BlockSpec((tr, VW), lambda n: (ix(n), 0)),
                  pl.BlockSpec((cpb, RET_HEADS, RET_QK_DIM, RET_V_DIM), lambda n: (ix(n), 0, 0, 0))] + tab_specs,
        out_specs=pl.BlockSpec((tr, RET_W), lambda n: (ix(n), 0)),
        out_shape=jax.ShapeDtypeStruct((T, RET_W), BF16),
        scratch_shapes=[pltpu.VMEM((RET_HEADS, RET_QK_DIM, RET_V_DIM), F32)],
        compiler_params=_cp("arbitrary"),
        name=name,
    )(z_a, dy, states, *tables)


ATT_TQ = 256
ATT_CPB = ATT_TQ // CHUNK
ATT_SCALE = ATT_HEAD_DIM ** -0.5


ATT_WIN = 3 * ATT_TQ
ATT_NB = CHUNK * ATT_BAND


def _rel_index():
    i = np.arange(CHUNK)[:, None]
    j = np.arange(ATT_BAND)[None, :]
    rel = np.clip(i + ATT_PAST * CHUNK - j, -MAX_REL, MAX_REL) + MAX_REL
    return jnp.asarray(rel.reshape(1, ATT_NB).astype(np.int32))


def _split3(x):
    hi = x.astype(BF16)
    r1 = x - hi.astype(F32)
    mid = r1.astype(BF16)
    lo = (r1 - mid.astype(F32)).astype(BF16)
    return hi, mid, lo


REL_TILE = 4608


def _bias_expand(rel_bias, name="bias_expand"):
    H = rel_bias.shape[0]
    n = ATT_NB
    padded = jnp.pad(rel_bias, ((0, 0), (0, N_REL_PAD - N_REL)))

    def body(rb_ref, idx_ref, o_ref):
        onehot = (lax.broadcasted_iota(jnp.int32, (N_REL_PAD, REL_TILE), 0) == idx_ref[...]).astype(BF16)
        hi, mid, lo = _split3(rb_ref[...])
        o_ref[...] = _dot(hi, onehot, NN) + _dot(mid, onehot, NN) + _dot(lo, onehot, NN)

    out = pl.pallas_call(
        body,
        grid=(n // REL_TILE,),
        in_specs=[pl.BlockSpec((H, N_REL_PAD), lambda t: (0, 0)), pl.BlockSpec((1, REL_TILE), lambda t: (0, t))],
        out_specs=pl.BlockSpec((H, REL_TILE), lambda t: (0, t)),
        out_shape=jax.ShapeDtypeStruct((H, n), F32),
        compiler_params=_cp("parallel"),
        name=name,
    )(padded, _rel_index())
    return out.reshape(H, CHUNK, ATT_BAND)


def _bias_tile(band, name="bias_tile"):
    H = band.shape[0]
    padded = jnp.pad(band, ((0, 0), (0, 0), (0, ATT_WIN - ATT_BAND)), constant_values=NEG_INF)

    def body(b_ref, o_ref):
        b = b_ref[...]
        for a in range(ATT_CPB):
            o_ref[a * CHUNK:(a + 1) * CHUNK, :] = pltpu.roll(b, a * CHUNK, 1) if a else b

    return pl.pallas_call(
        body,
        grid=(H,),
        in_specs=[pl.BlockSpec((None, CHUNK, ATT_WIN), lambda h: (h, 0, 0))],
        out_specs=pl.BlockSpec((None, ATT_TQ, ATT_WIN), lambda h: (h, 0, 0)),
        out_shape=jax.ShapeDtypeStruct((H, ATT_TQ, ATT_WIN), F32),
        compiler_params=_cp("parallel"),
        name=name,
    )(padded)


def _bias_untile(dtile, name="bias_untile"):
    H = dtile.shape[0]

    def body(d_ref, o_ref):
        acc = d_ref[0:CHUNK, :]
        for a in range(1, ATT_CPB):
            acc = acc + pltpu.roll(d_ref[a * CHUNK:(a + 1) * CHUNK, :], ATT_WIN - a * CHUNK, 1)
        o_ref[...] = acc

    out = pl.pallas_call(
        body,
        grid=(H,),
        in_specs=[pl.BlockSpec((None, ATT_TQ, ATT_WIN), lambda h: (h, 0, 0))],
        out_specs=pl.BlockSpec((None, CHUNK, ATT_WIN), lambda h: (h, 0, 0)),
        out_shape=jax.ShapeDtypeStruct((H, CHUNK, ATT_WIN), F32),
        compiler_params=_cp("parallel"),
        name=name,
    )(dtile)
    return out[:, :, :ATT_BAND]


def _bias_reduce(dbias, name="bias_reduce"):
    H = dbias.shape[0]
    n = ATT_NB

    def body(db_ref, idx_ref, o_ref):
        @pl.when(pl.program_id(0) == 0)
        def _():
            o_ref[...] = jnp.zeros_like(o_ref)

        onehot = (lax.broadcasted_iota(jnp.int32, (N_REL_PAD, REL_TILE), 0) == idx_ref[...]).astype(BF16)
        hi, mid, lo = _split3(db_ref[...])
        o_ref[...] += _dot(hi, onehot, NT) + _dot(mid, onehot, NT) + _dot(lo, onehot, NT)

    out = pl.pallas_call(
        body,
        grid=(n // REL_TILE,),
        in_specs=[pl.BlockSpec((H, REL_TILE), lambda t: (0, t)), pl.BlockSpec((1, REL_TILE), lambda t: (0, t))],
        out_specs=pl.BlockSpec((H, N_REL_PAD), lambda t: (0, 0)),
        out_shape=jax.ShapeDtypeStruct((H, N_REL_PAD), F32),
        compiler_params=_cp("arbitrary"),
        name=name,
    )(dbias.reshape(H, n), _rel_index())
    return out[:, :N_REL]


def _att_probs(q, kwin, bias, key_row0):
    s = _dot(q, kwin, NT) * ATT_SCALE + bias
    col = lax.broadcasted_iota(jnp.int32, (ATT_TQ, ATT_WIN), 1)
    s = jnp.where(col + key_row0 >= 0, s, NEG_INF)
    e = jnp.exp(s - jnp.max(s, axis=-1, keepdims=True))
    return e * (1.0 / jnp.sum(e, axis=-1, keepdims=True))


ATT_PAIR = 2 * ATT_HEAD_DIM
ATT_NP = ATT_HEADS // 2
ATT_QW = ATT_HEADS * ATT_HEAD_DIM
Y_COLS = RET_HEADS * RET_V_DIM + ATT_QW


def _att_specs(tq, nq, clip_q, q_col0):
    cb = ATT_QW // ATT_PAIR
    qi = (lambda p, m: (jnp.minimum(m, nq - 1), q_col0 + p)) if clip_q else (lambda p, m: (m, q_col0 + p))
    q = pl.BlockSpec((tq, ATT_PAIR), qi)

    def win(col0):
        return [pl.BlockSpec((tq, ATT_PAIR), functools.partial(lambda p, m, back: (jnp.clip(m - back, 0, nq - 1), col0 + p), back=b))
                for b in (2, 1, 0)]

    bias = pl.BlockSpec((2, ATT_TQ, ATT_WIN), lambda p, m: (p, 0, 0))
    return q, win(cb), win(2 * cb), bias


def _head_masks(rows):
    lane = lax.broadcasted_iota(jnp.int32, (rows, ATT_PAIR), 1)
    return lane < ATT_HEAD_DIM


def _att_fwd(z_b, bias, y, comm=None, name="att_fwd"):
    T = z_b.shape[0]
    tq = ATT_TQ
    nq = T // tq
    qs, kwin, vwin, bs = _att_specs(tq, nq, False, 0)
    nc = comm.n if comm else 0
    total = ATT_NP * nq

    def body(*refs):
        q_ref, k0, k1, k2, v0, v1, v2, b_ref = refs[:8]
        cin = refs[9:9 + nc]
        o_ref = refs[9 + nc]
        cout = refs[10 + nc:10 + 2 * nc]
        csem = refs[10 + 2 * nc:]
        m = pl.program_id(1)
        step = pl.program_id(0) * nq + m
        _host_comm(comm, "early", step, total, cin, cout, csem)
        kw = jnp.concatenate([k0[...], k1[...], k2[...]], axis=0)
        vw = jnp.concatenate([v0[...], v1[...], v2[...]], axis=0)
        q2 = q_ref[...]
        even = _head_masks(tq)
        outs = []
        for hh in range(2):
            qm = jnp.where(even if hh == 0 else ~even, q2, jnp.zeros_like(q2))
            p = _att_probs(qm, kw, b_ref[hh], (m - 2) * tq)
            outs.append(_dot(p.astype(BF16), vw, NN))
        o_ref[...] = jnp.where(even, outs[0], outs[1]).astype(o_ref.dtype)
        _host_comm(comm, "late", step, total, cin, cout, csem)

    y_cb = (Y_COLS - ATT_QW) // ATT_PAIR
    out = pl.pallas_call(
        body,
        grid=(ATT_NP, nq),
        in_specs=[qs] + kwin + vwin + [bs, ANY] + [ANY] * nc,
        out_specs=[pl.BlockSpec((tq, ATT_PAIR), lambda p, m: (m, y_cb + p))] + [ANY] * nc,
        out_shape=[jax.ShapeDtypeStruct((T, Y_COLS), BF16)] + (comm.out_shape if comm else []),
        scratch_shapes=comm.scratch if comm else [],
        input_output_aliases={8: 0},
        compiler_params=_cp("arbitrary", "arbitrary"),
        name=name,
    )(z_b, z_b, z_b, z_b, z_b, z_b, z_b, bias, y, *(comm.arrays if comm else []))
    return out[0], list(out[1:])


def _att_bwd(z_b, bias, dy, comm=None, name="att_bwd"):
    T = z_b.shape[0]
    tq = ATT_TQ
    nq = T // tq
    y_cb = (Y_COLS - ATT_QW) // ATT_PAIR
    qs, kwin, vwin, bs = _att_specs(tq, nq, True, 0)
    dos = _att_specs(tq, nq, True, y_cb)[0]
    kv_out = pl.BlockSpec((tq, ATT_PAIR), lambda p, m: (jnp.maximum(m - 2, 0), p))
    W3 = 3 * tq
    nc = comm.n if comm else 0
    total = ATT_NP * (nq + 2)

    def body(*refs):
        q_ref, k0, k1, k2, v0, v1, v2, b_ref, do_ref = refs[:9]
        cin = refs[9:9 + nc]
        dq_ref, dk_ref, dv_ref, db_ref = refs[9 + nc:13 + nc]
        cout = refs[13 + nc:13 + 2 * nc]
        dkc, dvc, dkw, dvw = refs[13 + 2 * nc:17 + 2 * nc]
        csem = refs[17 + 2 * nc:]
        m = pl.program_id(1)
        step = pl.program_id(0) * (nq + 2) + m
        _host_comm(comm, "early", step, total, cin, cout, csem)

        @pl.when(m == 0)
        def _():
            dkc[...] = jnp.zeros_like(dkc)
            dvc[...] = jnp.zeros_like(dvc)
            db_ref[...] = jnp.zeros_like(db_ref)

        @pl.when(m >= nq)
        def _():
            dkw[...] = jnp.zeros_like(dkw)
            dvw[...] = jnp.zeros_like(dvw)

        @pl.when(m < nq)
        def _():
            kw = jnp.concatenate([k0[...], k1[...], k2[...]], axis=0)
            vw = jnp.concatenate([v0[...], v1[...], v2[...]], axis=0)
            q2, do2 = q_ref[...], do_ref[...]
            even = _head_masks(tq)
            dqs, dks, dvs = [], [], []
            for hh in range(2):
                mine = even if hh == 0 else ~even
                p = _att_probs(jnp.where(mine, q2, jnp.zeros_like(q2)), kw, b_ref[hh], (m - 2) * tq)
                dp = _dot(jnp.where(mine, do2, jnp.zeros_like(do2)), vw, NT)
                ds = p * (dp - jnp.sum(dp * p, axis=-1, keepdims=True))
                db_ref[hh] += ds
                dsb = ds.astype(BF16)
                dqs.append(_dot(dsb, kw, NN))
                dks.append(_dot(dsb, q2, TN))
                dvs.append(_dot(p.astype(BF16), do2, TN))
            even_w = _head_masks(W3)
            dq_ref[...] = (jnp.where(even, dqs[0], dqs[1]) * ATT_SCALE).astype(dq_ref.dtype)
            dkw[...] = jnp.where(even_w, dks[0], dks[1]) * ATT_SCALE
            dvw[...] = jnp.where(even_w, dvs[0], dvs[1])

        dk_ref[...] = (dkc[0:tq, :] + dkw[0:tq, :]).astype(dk_ref.dtype)
        dv_ref[...] = (dvc[0:tq, :] + dvw[0:tq, :]).astype(dv_ref.dtype)
        dkc[0:tq, :] = dkc[tq:2 * tq, :] + dkw[tq:2 * tq, :]
        dvc[0:tq, :] = dvc[tq:2 * tq, :] + dvw[tq:2 * tq, :]
        dkc[tq:2 * tq, :] = dkw[2 * tq:W3, :]
        dvc[tq:2 * tq, :] = dvw[2 * tq:W3, :]
        _host_comm(comm, "late", step, total, cin, cout, csem)

    qo = pl.BlockSpec((tq, ATT_PAIR), lambda p, m: (jnp.minimum(m, nq - 1), p))
    hd = jax.ShapeDtypeStruct((T, ATT_QW), BF16)
    out = pl.pallas_call(
        body,
        grid=(ATT_NP, nq + 2),
        in_specs=[qs] + kwin + vwin + [bs, dos] + [ANY] * nc,
        out_specs=[qo, kv_out, kv_out, bs] + [ANY] * nc,
        out_shape=[hd, hd, hd, jax.ShapeDtypeStruct((ATT_HEADS, ATT_TQ, ATT_WIN), F32)] + (comm.out_shape if comm else []),
        scratch_shapes=[pltpu.VMEM((2 * tq, ATT_PAIR), F32), pltpu.VMEM((2 * tq, ATT_PAIR), F32),
                        pltpu.VMEM((W3, ATT_PAIR), F32), pltpu.VMEM((W3, ATT_PAIR), F32)] + (comm.scratch if comm else []),
        compiler_params=_cp("arbitrary", "arbitrary"),
        name=name,
    )(z_b, z_b, z_b, z_b, z_b, z_b, z_b, bias, dy, *(comm.arrays if comm else []))
    return out[0], out[1], out[2], out[3], list(out[4:])


REST = ["ab_out", "c_in_t", "c_out", "up_t0", "up_t1", "down0", "down1"]


def _local_step(x, target, wt, small, rest_shards=None, overlap=False):
    T = x.shape[0]
    Fh = FFN_HIDDEN
    tables = _ret_tables(T)
    gw, gs, recv = {}, {}, {}
    wt = dict(wt)

    hn0 = _rms_fwd(x, small["attn_norm_g"][0], name="rms_fwd")
    z_a = _mm(hn0, wt["ab_in_t"][:RET_W], "nt", BF16, name="mm_ab_in_a")
    z_b = _mm(hn0, wt["ab_in_t"][RET_W:], "nt", BF16, name="mm_ab_in_b")
    y, states = _ret_fwd(z_a, tables)
    bias = _bias_tile(_bias_expand(small["rel_bias"]))
    y, rest = _att_fwd(z_b, bias, y, comm=_Comm("gather", rest_shards) if rest_shards is not None else None)
    if rest_shards is not None:
        full = dict(zip(REST, rest))
        wt.update(ab_out=full["ab_out"], c_in_t=full["c_in_t"], c_out=full["c_out"],
                  up_t=[full["up_t0"], full["up_t1"]], down=[full["down0"], full["down1"]])
    h1 = _mm(y, wt["ab_out"], "nn", F32, res=x, name="mm_ab_out")

    def ffn_fwd(h, layer):
        hf = _rms_fwd(h, small["ffn_norm_g"][layer], name="rms_fwd")
        zg = _mm(hf, wt["up_t"][layer][:Fh], "nt", BF16, name="mm_up")
        zu = _mm(hf, wt["up_t"][layer][Fh:], "nt", BF16, name="mm_up")
        a = _ffn_mid_fwd(zg, zu, small["conv_w"][layer], small["conv_b"][layer][None, :])
        h_out = _mm(a, wt["down"][layer], "nn", F32, res=h, name="mm_down")
        return h_out, (hf, zg, zu, a)

    def ffn_bwd(dh_out, h, layer, saved, exchange=None):
        hf, zg, zu, a = saved
        da = _mm(dh_out, wt["down"][layer], "nt", BF16, name="mm_d_a")
        d_down = _mm(a, dh_out, "tn", BF16, name="mm_dw_down")
        comm = _Comm("exchange", [gw[k] for k in exchange]) if exchange else None
        dzg, dzu, dcw, dcb, got = _ffn_mid_bwd(zg, zu, da, small["conv_w"][layer], small["conv_b"][layer][None, :], comm=comm)
        recv.update(zip(exchange or [], got))
        dhf = _mm(dzg, wt["up_t"][layer][:Fh], "nn", F32, name="mm_d_hf")
        dhf = _mm(dzu, wt["up_t"][layer][Fh:], "nn", F32, res=dhf, name="mm_d_hf_acc")
        d_up = jnp.concatenate([_mm(dzg, hf, "tn", BF16, name="mm_dw_up"), _mm(dzu, hf, "tn", BF16, name="mm_dw_up")], axis=0)
        dh, dg = _rms_bwd(dhf, h, small["ffn_norm_g"][layer], dh_out)
        return dh, dg, d_up, d_down, dcw, dcb

    h2, ffn0 = ffn_fwd(h1, 0)

    hn1 = _rms_fwd(h2, small["attn_norm_g"][1], name="rms_fwd")
    zz = _mm(hn1, wt["c_in_t"], "nt", BF16, name="mm_c_in")
    ys = _sgu_fwd(zz, small["ln_g"], small["ln_b"], small["w_s"], small["b_s"])
    h3 = _mm(ys, wt["c_out"], "nn", F32, res=h2, name="mm_c_out")
    h4, ffn1 = ffn_fwd(h3, 1)

    loss_vec, dh4, gs["final_g"] = _final_loss(h4, small["final_g"], target)

    dh3, dgf1, gw["up_t1"], gw["down1"], dcw1, dcb1 = ffn_bwd(dh4, h3, 1, ffn1)
    dys = _mm(dh3, wt["c_out"], "nt", BF16, name="mm_d_ys")
    gw["c_out"] = _mm(ys, dh3, "tn", BF16, name="mm_dw_c_out")
    dzz, gs["w_s"], dbs, dlg, dlb = _sgu_bwd(zz, dys, small["ln_g"], small["ln_b"], small["w_s"], small["b_s"])
    gs["b_s"], gs["ln_g"], gs["ln_b"] = dbs[:, :, 0], dlg[0], dlb[0]
    dhn1 = _mm(dzz, wt["c_in_t"], "nn", F32, name="mm_d_hn1")
    gw["c_in_t"] = _mm(dzz, hn1, "tn", BF16, name="mm_dw_c_in")
    dh2, dga1 = _rms_bwd(dhn1, h2, small["attn_norm_g"][1], dh3)

    dh1, dgf0, gw["up_t0"], gw["down0"], dcw0, dcb0 = ffn_bwd(
        dh2, h1, 0, ffn0, exchange=["c_in_t", "c_out", "up_t1", "down1"] if overlap else None)

    dy = _mm(dh1, wt["ab_out"], "nt", BF16, name="mm_d_y")
    gw["ab_out"] = _mm(y, dh1, "tn", BF16, name="mm_dw_ab_out")
    dz_a = _ret_bwd(z_a, dy, states, tables)
    late = ["ab_out", "up_t0", "down0"] if overlap else []
    dq, dk, dv, dbias, got = _att_bwd(z_b, bias, dy, comm=_Comm("exchange", [gw[k] for k in late]) if late else None)
    recv.update(zip(late, got))
    dz_b = jnp.concatenate([dq, dk, dv], axis=1)
    gs["rel_bias"] = _bias_reduce(_bias_untile(dbias))
    dhn0 = _mm(dz_a, wt["ab_in_t"][:RET_W], "nn", F32, name="mm_d_hn0")
    dhn0 = _mm(dz_b, wt["ab_in_t"][RET_W:], "nn", F32, res=dhn0, name="mm_d_hn0_acc")
    gw["ab_in_t"] = jnp.concatenate([_mm(dz_a, hn0, "tn", BF16, name="mm_dw_ab_in_a"),
                                     _mm(dz_b, hn0, "tn", BF16, name="mm_dw_ab_in_b")], axis=0)
    grad_x, dga0 = _rms_bwd(dhn0, x, small["attn_norm_g"][0], dh1)

    gs["attn_norm_g"] = jnp.concatenate([dga0, dga1], axis=0)
    gs["ffn_norm_g"] = jnp.concatenate([dgf0, dgf1], axis=0)
    gs["conv_w"] = jnp.stack([dcw0, dcw1])
    gs["conv_b"] = jnp.concatenate([dcb0, dcb1], axis=0)
    gs["final_g"] = gs["final_g"][0]
    return loss_vec[0, 0], grad_x, gw, gs, recv


MESH_ID = pl.DeviceIdType.MESH
ANY = pl.BlockSpec(memory_space=pl.ANY)


def _my_place():
    return lax.axis_index("x"), lax.axis_index("y"), lax.axis_index("c")


class _Comm:
    def __init__(self, kind, arrays):
        self.kind, self.arrays, self.n = kind, list(arrays), len(arrays)
        if kind == "gather":
            self.out_shape = [jax.ShapeDtypeStruct((N_DEV * s.shape[0], s.shape[1]), s.dtype) for s in arrays]
        else:
            self.out_shape = [jax.ShapeDtypeStruct((N_DEV, g.shape[0] // N_DEV, g.shape[1]), g.dtype) for g in arrays]
        n = self.n
        self.scratch = [pltpu.SemaphoreType.DMA((n, 7)), pltpu.SemaphoreType.DMA((n, 7)), pltpu.SemaphoreType.DMA((n,))]

    def phase(self, ph, in_refs, out_refs, sems):
        (self._gather if self.kind == "gather" else self._exchange)(ph, in_refs, out_refs, sems)

    def _gather(self, ph, x_refs, o_refs, sems):
        n = self.n
        send_sems, recv_sems, local_sems = sems
        x, y, c = _my_place()
        me, sibling = (x, y, c), (x, y, 1 - c)
        chips = [(1 - x, y), (x, 1 - y), (1 - x, 1 - y)]

        def rows(a, place):
            m = x_refs[a].shape[0]
            px, py, pc = place
            return o_refs[a].at[pl.ds((4 * px + 2 * py + pc) * m, m), :]

        def copy(a, k, block, to, own=False):
            return pltpu.make_async_remote_copy(
                src_ref=x_refs[a] if own else rows(a, block), dst_ref=rows(a, block),
                send_sem=send_sems.at[a, k], recv_sem=recv_sems.at[a, k], device_id=to, device_id_type=MESH_ID)

        def mine():
            return [pltpu.make_async_copy(x_refs[a], rows(a, me), local_sems.at[a]) for a in range(n)]

        def first():
            out = []
            for a in range(n):
                out.append(copy(a, 0, me, sibling, own=True))
                out += [copy(a, 1 + j, me, (*chip, c), own=True) for j, chip in enumerate(chips)]
            return out

        def passed():
            return [copy(a, 4 + j, (*chip, c), sibling) for j, chip in enumerate(chips) for a in range(n)]

        if ph == 0:
            for cp in mine() + first():
                cp.start()
        elif ph == 1:
            fw = passed()
            for j, chip in enumerate(chips):
                for a in range(n):
                    copy(a, 1 + j, (*chip, c), me).wait_recv()
                    fw[j * n + a].start()
        else:
            for a in range(n):
                copy(a, 0, sibling, me).wait_recv()
                for j, chip in enumerate(chips):
                    copy(a, 4 + j, (*chip, 1 - c), me).wait_recv()
            for cp in first() + passed():
                cp.wait_send()
            for cp in mine():
                cp.wait()

    def _exchange(self, ph, g_refs, o_refs, sems):
        n = self.n
        send_sems, recv_sems, local_sems = sems
        x, y, c = _my_place()
        me = 4 * x + 2 * y + c
        peers = [(x ^ ((k >> 2) & 1), y ^ ((k >> 1) & 1), c ^ (k & 1)) for k in range(1, N_DEV)]

        def block(a, idx):
            m = g_refs[a].shape[0] // N_DEV
            return g_refs[a].at[pl.ds(idx * m, m), :]

        def copy(a, k, slot):
            px, py, pc = peers[k]
            return pltpu.make_async_remote_copy(
                src_ref=block(a, 4 * px + 2 * py + pc), dst_ref=o_refs[a].at[slot],
                send_sem=send_sems.at[a, k], recv_sem=recv_sems.at[a, k], device_id=peers[k], device_id_type=MESH_ID)

        if ph == 1:
            return
        mine = [pltpu.make_async_copy(block(a, me), o_refs[a].at[me], local_sems.at[a]) for a in range(n)]
        sends = [copy(a, k, me) for k in range(N_DEV - 1) for a in range(n)]
        if ph == 0:
            for cp in mine + sends:
                cp.start()
        else:
            for k in range(N_DEV - 1):
                px, py, pc = peers[k]
                for a in range(n):
                    copy(a, k, 4 * px + 2 * py + pc).wait_recv()
            for cp in sends:
                cp.wait_send()
            for cp in mine:
                cp.wait()


def _comm_call(comm, name):
    n = comm.n

    def body(*refs):
        for ph in range(3):
            comm.phase(ph, refs[:n], refs[n:2 * n], refs[2 * n:])

    return pl.pallas_call(
        body, out_shape=comm.out_shape, in_specs=[ANY] * n, out_specs=[ANY] * n, scratch_shapes=comm.scratch, name=name,
    )(*comm.arrays)


def _host_comm(comm, when, step, total, cin, cout, csem):
    if comm is None:
        return
    sched = {0: 0, 1: (3 * total) // 4, 2: total - 1}
    phases = (0, 1) if when == "early" else (2,)
    for ph in phases:
        if ph == 1 and comm.kind == "exchange":
            continue

        @pl.when(step == sched[ph])
        def _(ph=ph):
            comm.phase(ph, cin, cout, csem)


def _all_gather(shards, name="all_gather"):
    return _comm_call(_Comm("gather", shards), name)


def _all_to_all(fulls, name="all_to_all"):
    return _comm_call(_Comm("exchange", fulls), name)


def _row_tile(r, target=256):
    best = None
    for t in range(8, min(r, target) + 1, 8):
        if r % t == 0:
            best = t
    return best if best is not None else r


def _sum8(parts, name="sum8"):
    _, M, N = parts.shape
    tr = _row_tile(M, 128)

    def body(p_ref, o_ref):
        acc = p_ref[0].astype(F32)
        for d in range(1, N_DEV):
            acc = acc + p_ref[d].astype(F32)
        o_ref[...] = acc

    return pl.pallas_call(
        body,
        grid=(M // tr,),
        in_specs=[pl.BlockSpec((N_DEV, tr, N), lambda i: (0, i, 0))],
        out_specs=pl.BlockSpec((tr, N), lambda i: (i, 0)),
        out_shape=jax.ShapeDtypeStruct((M, N), F32),
        compiler_params=_cp("parallel"),
        name=name,
    )(parts)


def _adamw(w, g, m, v, name="adamw"):
    shape = w.shape
    if w.ndim == 1:
        r2 = (1, shape[0])
    else:
        r2 = (int(np.prod(shape[:-1])), shape[-1])
    R, C = r2
    tr = _row_tile(R)
    bc1 = 1.0 - ADAM_B1 ** ADAM_STEP
    bc2 = 1.0 - ADAM_B2 ** ADAM_STEP

    def body(w_ref, g_ref, m_ref, v_ref, d_ref, nm_ref, nv_ref):
        gv = g_ref[...]
        nm = ADAM_B1 * m_ref[...] + (1.0 - ADAM_B1) * gv
        nv = ADAM_B2 * v_ref[...] + (1.0 - ADAM_B2) * (gv * gv)
        d_ref[...] = -ADAM_LR * ((nm / bc1) / (jnp.sqrt(nv / bc2) + ADAM_EPS) + ADAM_WD * w_ref[...])
        nm_ref[...] = nm
        nv_ref[...] = nv

    spec = pl.BlockSpec((tr, C), lambda i: (i, 0))
    out = pl.pallas_call(
        body,
        grid=(R // tr,),
        in_specs=[spec] * 4,
        out_specs=[spec] * 3,
        out_shape=[jax.ShapeDtypeStruct(r2, F32)] * 3,
        compiler_params=_cp("parallel"),
        name=name,
    )(w.reshape(r2), g.reshape(r2), m.reshape(r2), v.reshape(r2))
    return [o.reshape(shape) for o in out]


WEIGHTS = ['attn_norm_g', 'ffn_norm_g', 'ab_w_in', 'ab_w_out', 'ab_rel_bias', 'c_w_in', 'c_ln_g', 'c_ln_b', 'c_w_s', 'c_b_s',
           'c_w_out', 'ffn_w_up', 'ffn_conv_w', 'ffn_conv_b', 'ffn_w_down', 'final_norm_g']
SMALL_ORDER = ["attn_norm_g", "ffn_norm_g", "rel_bias", "ln_g", "ln_b", "w_s", "b_s", "conv_w", "conv_b", "final_g"]
PACK_ROW = 1024


def _pack(arrs):
    flat = jnp.concatenate([a.reshape(-1) for a in arrs])
    n = flat.shape[0]
    padded = -(-n // PACK_ROW) * PACK_ROW
    return jnp.pad(flat, (0, padded - n)).reshape(padded // 128, 128)


def _unpack(flat, shapes):
    out, off = [], 0
    for s in shapes:
        n = int(np.prod(s))
        out.append(flat[off:off + n].reshape(s))
        off += n
    return out


def _step(P):
    x, target = P["x"][0], P["loss_target"][0]
    me = 4 * lax.axis_index("x") + 2 * lax.axis_index("y") + lax.axis_index("c")
    n_up = P["ffn_w_up"].shape[0]
    Fc = P["ffn_conv_w"].shape[-1]
    Lc = P["c_ln_g"].shape[-1]

    full = _all_gather([P["ab_w_in"][0].T.astype(BF16), _pack([P["ffn_conv_w"], P["c_ln_g"], P["c_ln_b"]])],
                       name="gather_first")
    wt = {"ab_in_t": full[0]}
    rest = {"ab_out": P["ab_w_out"][0], "c_in_t": P["c_w_in"][0].T, "c_out": P["c_w_out"][0],
            "up_t0": P["ffn_w_up"][0].T, "up_t1": P["ffn_w_up"][1].T, "down0": P["ffn_w_down"][0], "down1": P["ffn_w_down"][1]}
    rest_shards = [rest[k].astype(BF16) for k in REST]
    sm = full[-1].reshape(N_DEV, -1)
    conv_w = sm[:, :n_up * 3 * Fc].reshape(N_DEV, n_up, 3, Fc).transpose(1, 2, 0, 3).reshape(n_up, 3, N_DEV * Fc)
    off = n_up * 3 * Fc
    ln_g = sm[:, off:off + Lc].reshape(N_DEV * Lc)
    ln_b = sm[:, off + Lc:off + 2 * Lc].reshape(N_DEV * Lc)
    small = {"attn_norm_g": P["attn_norm_g"], "ffn_norm_g": P["ffn_norm_g"], "rel_bias": P["ab_rel_bias"][0],
             "ln_g": ln_g, "ln_b": ln_b, "w_s": P["c_w_s"][0], "b_s": P["c_b_s"][0], "conv_w": conv_w,
             "conv_b": P["ffn_conv_b"], "final_g": P["final_norm_g"]}

    loss_part, grad_x, gw, gs, recv = _local_step(x, target, wt, small, rest_shards=rest_shards, overlap=True)
    loss = lax.psum(loss_part, ("x", "y", "c"))

    recv["ab_in_t"] = _all_to_all([gw["ab_in_t"]], name="exchange_last")[0]
    s8 = {k: _sum8(recv[k], name="sum8") for k in ["ab_in_t"] + REST}
    g_big = {"ab_w_in": s8["ab_in_t"].T[None], "ab_w_out": s8["ab_out"][None], "c_w_in": s8["c_in_t"].T[None],
             "c_w_out": s8["c_out"][None], "ffn_w_up": jnp.stack([s8["up_t0"].T, s8["up_t1"].T]),
             "ffn_w_down": jnp.stack([s8["down0"], s8["down1"]])}

    packed = _pack([gs[k] for k in SMALL_ORDER])
    gathered = _all_gather([packed], name="gather_small_grads")[0]
    tot = _sum8(gathered.reshape(N_DEV, packed.shape[0], 128), name="sum8_small").reshape(-1)
    gsm = dict(zip(SMALL_ORDER, _unpack(tot, [gs[k].shape for k in SMALL_ORDER])))
    grads = dict(g_big)
    grads["attn_norm_g"] = gsm["attn_norm_g"]
    grads["ffn_norm_g"] = gsm["ffn_norm_g"]
    grads["ab_rel_bias"] = gsm["rel_bias"][None]
    grads["c_ln_g"] = lax.dynamic_slice(gsm["ln_g"], (me * Lc,), (Lc,))[None]
    grads["c_ln_b"] = lax.dynamic_slice(gsm["ln_b"], (me * Lc,), (Lc,))[None]
    grads["c_w_s"] = gsm["w_s"][None]
    grads["c_b_s"] = gsm["b_s"][None]
    grads["ffn_conv_w"] = lax.dynamic_slice(gsm["conv_w"], (0, 0, me * Fc), (n_up, 3, Fc))
    grads["ffn_conv_b"] = gsm["conv_b"]
    grads["final_norm_g"] = gsm["final_g"]

    delta, new_m, new_v = {}, {}, {}
    for k in WEIGHTS:
        delta[k], new_m[k], new_v[k] = _adamw(P[k], grads[k], P["m_" + k], P["v_" + k], name="adamw")
    return (loss, grad_x[None], *[grads[k] for k in WEIGHTS], *[delta[k] for k in WEIGHTS],
            *[new_m[k] for k in WEIGHTS], *[new_v[k] for k in WEIGHTS])


def kernel(x, attn_norm_g, ffn_norm_g, ab_w_in, ab_w_out, ab_rel_bias, c_w_in, c_ln_g, c_ln_b, c_w_s, c_b_s, c_w_out, ffn_w_up, ffn_conv_w, ffn_conv_b, ffn_w_down, final_norm_g, loss_target, m_attn_norm_g, m_ffn_norm_g, m_ab_w_in, m_ab_w_out, m_ab_rel_bias, m_c_w_in, m_c_ln_g, m_c_ln_b, m_c_w_s, m_c_b_s, m_c_w_out, m_ffn_w_up, m_ffn_conv_w, m_ffn_conv_b, m_ffn_w_down, m_final_norm_g, v_attn_norm_g, v_ffn_norm_g, v_ab_w_in, v_ab_w_out, v_ab_rel_bias, v_c_w_in, v_c_ln_g, v_c_ln_b, v_c_w_s, v_c_b_s, v_c_w_out, v_ffn_w_up, v_ffn_conv_w, v_ffn_conv_b, v_ffn_w_down, v_final_norm_g):
    return _step(dict(locals()))
```

```python
import functools

import numpy as np
import jax
import jax.numpy as jnp
from jax import lax
from jax.experimental import pallas as pl
from jax.experimental.pallas import tpu as pltpu

F32 = jnp.float32
BF16 = jnp.bfloat16

D_MODEL = 1024
CHUNK = 64
EPS = 1e-6
NEG_INF = -1e30
RET_HEADS = 4
RET_QK_DIM = 128
RET_V_DIM = 256
ATT_HEADS = 8
ATT_HEAD_DIM = 64
ATT_PAST = 8
ATT_BAND = (ATT_PAST + 1) * CHUNK
MAX_REL = 128
N_REL = 2 * MAX_REL + 1
N_REL_PAD = 384
SGU_BLOCK = 128
SGU_GROUPS = 8
SGU_WIDTH = 2048
SGU_GW = SGU_WIDTH // SGU_GROUPS
FFN_HIDDEN = 2816
RET_W = 2 * RET_HEADS * RET_QK_DIM + 2 * RET_HEADS * RET_V_DIM
ATT_W = 3 * ATT_HEADS * ATT_HEAD_DIM
N_DEV = 8

ADAM_LR = 0.001
ADAM_B1 = 0.9
ADAM_B2 = 0.999
ADAM_EPS = 1e-08
ADAM_WD = 0.01
ADAM_STEP = 10

VMEM_LIMIT = 52 * 1024 * 1024


def _cp(*sem):
    return pltpu.CompilerParams(dimension_semantics=sem if sem else None, vmem_limit_bytes=VMEM_LIMIT)


def _tile(n, target):
    if n <= target:
        return n
    best = None
    for t in range(128, target + 1, 128):
        if n % t == 0:
            best = t
    assert best is not None, (n, target)
    return best


def _gelu(x):
    c = 0.7978845608028654
    return 0.5 * x * (1.0 + jnp.tanh(c * (x + 0.044715 * x * x * x)))


def _gelu_and_grad(x):
    c = 0.7978845608028654
    x2 = x * x
    t = jnp.tanh(c * (x + 0.044715 * x * x2))
    cdf = 0.5 * (1.0 + t)
    grad = cdf + x * (0.5 * c) * (1.0 - t * t) * (1.0 + 3.0 * 0.044715 * x2)
    return x * cdf, grad


def _dot(a, b, dims):
    return lax.dot_general(a, b, (dims, ((), ())), preferred_element_type=F32)


NN = ((1,), (0,))
NT = ((1,), (1,))
TN = ((0,), (0,))


def _mm(a, b, mode, out_dtype, res=None, name="mm", tm_t=None, tn_t=None, tk_t=None):
    if mode == "nt":
        (M, K), N = a.shape, b.shape[0]
        dm, dn, dk = 512, 2816, K
    elif mode == "nn":
        (M, K), N = a.shape, b.shape[1]
        dm, dn, dk = (1024 if K <= 3072 else 512), 1024, K
    else:
        (K, M), N = a.shape, b.shape[1]
        dm, dn, dk = 1536, 1024, 1024
    tm, tn, tk = _tile(M, tm_t or dm), _tile(N, tn_t or dn), _tile(K, tk_t or dk)
    nk = K // tk
    dims = {"nt": NT, "nn": NN, "tn": TN}[mode]
    a_spec = pl.BlockSpec((tk, tm), lambda i, j, k: (k, i)) if mode == "tn" else pl.BlockSpec((tm, tk), lambda i, j, k: (i, k))
    b_spec = pl.BlockSpec((tn, tk), lambda i, j, k: (j, k)) if mode == "nt" else pl.BlockSpec((tk, tn), lambda i, j, k: (k, j))
    o_spec = pl.BlockSpec((tm, tn), lambda i, j, k: (i, j))
    has_res = res is not None

    def body(*refs):
        if has_res:
            a_ref, b_ref, r_ref, o_ref = refs[:4]
        else:
            a_ref, b_ref, o_ref = refs[:3]
        p = _dot(a_ref[...].astype(BF16), b_ref[...].astype(BF16), dims)
        if nk == 1:
            if has_res:
                p = p + r_ref[...]
            o_ref[...] = p.astype(out_dtype)
            return
        acc = refs[-1]
        k = pl.program_id(2)

        @pl.when(k == 0)
        def _():
            acc[...] = p

        @pl.when(k > 0)
        def _():
            acc[...] += p

        @pl.when(k == nk - 1)
        def _():
            t = acc[...]
            if has_res:
                t = t + r_ref[...]
            o_ref[...] = t.astype(out_dtype)

    in_specs = [a_spec, b_spec] + ([o_spec] if has_res else [])
    args = (a, b) + ((res,) if has_res else ())
    return pl.pallas_call(
        body,
        grid=(M // tm, N // tn, nk),
        in_specs=in_specs,
        out_specs=o_spec,
        out_shape=jax.ShapeDtypeStruct((M, N), out_dtype),
        scratch_shapes=[pltpu.VMEM((tm, tn), F32)] if nk > 1 else [],
        compiler_params=_cp("parallel", "parallel", "arbitrary"),
        name=name,
    )(*args)


def _mm_rows(a, b, res, name, norm_g=None, bwd=None, tm=512):
    M, K = a.shape
    Dm = b.shape[1]
    tm = min(tm, M)
    row = pl.BlockSpec((tm, Dm), lambda i: (i, 0))
    vec = pl.BlockSpec((1, Dm), lambda i: (0, 0))
    a_spec = pl.BlockSpec((tm, K), lambda i: (i, 0))
    b_spec = pl.BlockSpec((K, Dm), lambda i: (0, 0))

    if bwd is None:
        def body(a_ref, b_ref, r_ref, g_ref, o_ref, n_ref):
            t = _dot(a_ref[...].astype(BF16), b_ref[...].astype(BF16), NN) + r_ref[...]
            o_ref[...] = t
            r = lax.rsqrt(jnp.mean(t * t, axis=-1, keepdims=True) + EPS)
            n_ref[...] = (t * r * g_ref[...]).astype(n_ref.dtype)

        return pl.pallas_call(
            body, grid=(M // tm,), in_specs=[a_spec, b_spec, row, vec], out_specs=[row, row],
            out_shape=[jax.ShapeDtypeStruct((M, Dm), F32), jax.ShapeDtypeStruct((M, Dm), BF16)],
            compiler_params=_cp("parallel"), name=name,
        )(a, b, res, norm_g.reshape(1, Dm))

    h, g, dres = bwd
    has_res = res is not None

    def body(*refs):
        a_ref, b_ref = refs[:2]
        h_ref, g_ref, dres_ref, dh_ref, dg_ref = refs[-5:]

        @pl.when(pl.program_id(0) == 0)
        def _():
            dg_ref[...] = jnp.zeros_like(dg_ref)

        d = _dot(a_ref[...].astype(BF16), b_ref[...].astype(BF16), NN)
        if has_res:
            d = d + refs[2][...]
        x = h_ref[...]
        r = lax.rsqrt(jnp.mean(x * x, axis=-1, keepdims=True) + EPS)
        xhat = x * r
        dg_ref[...] += jnp.sum(d * xhat, axis=0, keepdims=True)
        dx = d * g_ref[...]
        m = jnp.mean(dx * xhat, axis=-1, keepdims=True)
        dh_ref[...] = dres_ref[...] + r * (dx - xhat * m)

    return pl.pallas_call(
        body, grid=(M // tm,), in_specs=[a_spec, b_spec] + ([row] if has_res else []) + [row, vec, row], out_specs=[row, vec],
        out_shape=[jax.ShapeDtypeStruct((M, Dm), F32), jax.ShapeDtypeStruct((1, Dm), F32)],
        compiler_params=_cp("arbitrary"), name=name,
    )(a, b, *((res,) if has_res else ()), h, g.reshape(1, Dm), dres)


def _rms_fwd(h, g, name="rms_fwd", tm=512):
    T, Dm = h.shape
    tm = min(tm, T)

    def body(h_ref, g_ref, o_ref):
        x = h_ref[...]
        r = lax.rsqrt(jnp.mean(x * x, axis=-1, keepdims=True) + EPS)
        o_ref[...] = (x * r * g_ref[...]).astype(o_ref.dtype)

    return pl.pallas_call(
        body,
        grid=(T // tm,),
        in_specs=[pl.BlockSpec((tm, Dm), lambda i: (i, 0)), pl.BlockSpec((1, Dm), lambda i: (0, 0))],
        out_specs=pl.BlockSpec((tm, Dm), lambda i: (i, 0)),
        out_shape=jax.ShapeDtypeStruct((T, Dm), BF16),
        compiler_params=_cp("parallel"),
        name=name,
    )(h, g.reshape(1, Dm))


def _final_loss(h, g, target, name="final_loss", tm=512):
    T, Dm = h.shape
    tm = min(tm, T)

    def body(h_ref, g_ref, t_ref, loss_ref, dh_ref, dg_ref):
        i = pl.program_id(0)
        x = h_ref[...]
        gv = g_ref[...]
        r = lax.rsqrt(jnp.mean(x * x, axis=-1, keepdims=True) + EPS)
        xhat = x * r
        e = xhat * gv - t_ref[...]

        @pl.when(i == 0)
        def _():
            loss_ref[...] = jnp.zeros_like(loss_ref)
            dg_ref[...] = jnp.zeros_like(dg_ref)

        loss_ref[...] += jnp.full((1, 128), 0.5 / Dm, F32) * jnp.sum(e * e)
        dy = e * (1.0 / Dm)
        dg_ref[...] += jnp.sum(dy * xhat, axis=0, keepdims=True)
        dx = dy * gv
        m = jnp.mean(dx * xhat, axis=-1, keepdims=True)
        dh_ref[...] = r * (dx - xhat * m)

    row = pl.BlockSpec((tm, Dm), lambda i: (i, 0))
    vec = pl.BlockSpec((1, Dm), lambda i: (0, 0))
    return pl.pallas_call(
        body,
        grid=(T // tm,),
        in_specs=[row, vec, row],
        out_specs=[pl.BlockSpec((1, 128), lambda i: (0, 0)), row, vec],
        out_shape=[jax.ShapeDtypeStruct((1, 128), F32), jax.ShapeDtypeStruct((T, Dm), F32), jax.ShapeDtypeStruct((1, Dm), F32)],
        compiler_params=_cp("arbitrary"),
        name=name,
    )(h, g.reshape(1, Dm), target)


HALO = 16


def _conv3(ext, w_ref, b_ref):
    return w_ref[0:1, :] * pltpu.roll(ext, 2, 0) + w_ref[1:2, :] * pltpu.roll(ext, 1, 0) + w_ref[2:3, :] * ext + b_ref[...]


def _ffn_mid_fwd(zg, zu, cw, cb, name="ffn_mid_fwd", tm=256, tc=1408):
    T, Fh = zg.shape
    tm = min(tm, T)
    nj = Fh // tc
    hb = tm // HALO

    def body(zg_ref, zu_ref, zgp_ref, zup_ref, wg_ref, wu_ref, bg_ref, bu_ref, a_ref):
        i = pl.program_id(1)
        first = i == 0

        def conv(z_ref, zp_ref, w_ref, b_ref):
            prev = jnp.where(first, 0.0, zp_ref[...].astype(F32))
            ext = jnp.concatenate([prev, z_ref[...].astype(F32)], axis=0)
            return _conv3(ext, w_ref, b_ref)[HALO:]

        cg = conv(zg_ref, zgp_ref, wg_ref, bg_ref)
        cu = conv(zu_ref, zup_ref, wu_ref, bu_ref)
        a_ref[...] = (_gelu(cg) * cu).astype(a_ref.dtype)

    cur = pl.BlockSpec((tm, tc), lambda j, i: (i, j))
    prev = pl.BlockSpec((HALO, tc), lambda j, i: (jnp.maximum(i * hb - 1, 0), j))
    wg = pl.BlockSpec((3, tc), lambda j, i: (0, j))
    wu = pl.BlockSpec((3, tc), lambda j, i: (0, j + nj))
    bg = pl.BlockSpec((1, tc), lambda j, i: (0, j))
    bu = pl.BlockSpec((1, tc), lambda j, i: (0, j + nj))
    return pl.pallas_call(
        body,
        grid=(nj, T // tm),
        in_specs=[cur, cur, prev, prev, wg, wu, bg, bu],
        out_specs=cur,
        out_shape=jax.ShapeDtypeStruct((T, Fh), BF16),
        compiler_params=_cp("parallel", "parallel"),
        name=name,
    )(zg, zu, zg, zu, cw, cw, cb, cb)


def _ffn_mid_bwd(zg, zu, da, cw, cb, comm=None, name="ffn_mid_bwd", tm=512, tc=256):
    T, Fh = zg.shape
    tm = min(tm, T)
    nj = Fh // tc
    hb = tm // HALO
    nhb = T // HALO

    nc = comm.n if comm else 0
    ni = T // tm

    def body(*refs):
        zg_ref, zu_ref, zgp_ref, zup_ref, zgn_ref, zun_ref, da_ref, dan_ref, wg_ref, wu_ref, bg_ref, bu_ref = refs[:12]
        cin = refs[12:12 + nc]
        dzg_ref, dzu_ref, dwg_ref, dwu_ref, dbg_ref, dbu_ref = refs[12 + nc:18 + nc]
        cout = refs[18 + nc:18 + 2 * nc]
        csem = refs[18 + 2 * nc:]
        i = pl.program_id(1)
        step = pl.program_id(0) * ni + i
        _host_comm(comm, "early", step, nj * ni, cin, cout, csem)
        first = i == 0
        last = i == ni - 1

        @pl.when(first)
        def _():
            dwg_ref[...] = jnp.zeros_like(dwg_ref)
            dwu_ref[...] = jnp.zeros_like(dwu_ref)
            dbg_ref[...] = jnp.zeros_like(dbg_ref)
            dbu_ref[...] = jnp.zeros_like(dbu_ref)

        def ext_of(p_ref, c_ref, n_ref):
            p = jnp.where(first, 0.0, p_ref[...].astype(F32))
            return jnp.concatenate([p, c_ref[...].astype(F32), n_ref[...].astype(F32)], axis=0)

        zge = ext_of(zgp_ref, zg_ref, zgn_ref)
        zue = ext_of(zup_ref, zu_ref, zun_ref)
        dan = jnp.where(last, 0.0, dan_ref[...].astype(F32))
        dae = jnp.concatenate([jnp.zeros((HALO, tc), F32), da_ref[...].astype(F32), dan], axis=0)
        zg1, zg2 = pltpu.roll(zge, 1, 0), pltpu.roll(zge, 2, 0)
        zu1, zu2 = pltpu.roll(zue, 1, 0), pltpu.roll(zue, 2, 0)
        cg = wg_ref[0:1, :] * zg2 + wg_ref[1:2, :] * zg1 + wg_ref[2:3, :] * zge + bg_ref[...]
        cu = wu_ref[0:1, :] * zu2 + wu_ref[1:2, :] * zu1 + wu_ref[2:3, :] * zue + bu_ref[...]
        gel, dgel = _gelu_and_grad(cg)
        dcg = dae * cu * dgel
        dcu = dae * gel
        lo, hi = HALO, HALO + tm

        def back(dc, taps, w_ref, dz_ref, dw_ref, db_ref):
            n = dc.shape[0]
            dz = w_ref[2:3, :] * dc + w_ref[1:2, :] * pltpu.roll(dc, n - 1, 0) + w_ref[0:1, :] * pltpu.roll(dc, n - 2, 0)
            dz_ref[...] = dz[lo:hi].astype(dz_ref.dtype)
            dcc = dc[lo:hi]
            db_ref[...] += jnp.sum(dcc, axis=0, keepdims=True)
            for k, tap in enumerate(taps):
                dw_ref[k:k + 1, :] += jnp.sum(dcc * tap[lo:hi], axis=0, keepdims=True)

        back(dcg, (zg2, zg1, zge), wg_ref, dzg_ref, dwg_ref, dbg_ref)
        back(dcu, (zu2, zu1, zue), wu_ref, dzu_ref, dwu_ref, dbu_ref)
        _host_comm(comm, "late", step, nj * ni, cin, cout, csem)

    cur = pl.BlockSpec((tm, tc), lambda j, i: (i, j))
    prev = pl.BlockSpec((HALO, tc), lambda j, i: (jnp.maximum(i * hb - 1, 0), j))
    nxt = pl.BlockSpec((HALO, tc), lambda j, i: (jnp.minimum((i + 1) * hb, nhb - 1), j))
    wg = pl.BlockSpec((3, tc), lambda j, i: (0, j))
    wu = pl.BlockSpec((3, tc), lambda j, i: (0, j + nj))
    bg = pl.BlockSpec((1, tc), lambda j, i: (0, j))
    bu = pl.BlockSpec((1, tc), lambda j, i: (0, j + nj))
    dw = pl.BlockSpec((3, tc), lambda j, i: (0, j))
    db = pl.BlockSpec((1, tc), lambda j, i: (0, j))
    out = pl.pallas_call(
        body,
        grid=(nj, ni),
        in_specs=[cur, cur, prev, prev, nxt, nxt, cur, nxt, wg, wu, bg, bu] + [ANY] * nc,
        out_specs=[cur, cur, dw, dw, db, db] + [ANY] * nc,
        out_shape=[jax.ShapeDtypeStruct((T, Fh), BF16), jax.ShapeDtypeStruct((T, Fh), BF16),
                   jax.ShapeDtypeStruct((3, Fh), F32), jax.ShapeDtypeStruct((3, Fh), F32),
                   jax.ShapeDtypeStruct((1, Fh), F32), jax.ShapeDtypeStruct((1, Fh), F32)] + (comm.out_shape if comm else []),
        scratch_shapes=comm.scratch if comm else [],
        compiler_params=_cp("arbitrary", "arbitrary"),
        name=name,
    )(zg, zu, zg, zu, zg, zu, da, da, cw, cw, cb, cb, *(comm.arrays if comm else []))
    dzg, dzu, dwg, dwu, dbg, dbu = out[:6]
    return dzg, dzu, jnp.concatenate([dwg, dwu], axis=1), jnp.concatenate([dbg, dbu], axis=1), list(out[6:])


def _sgu_mask():
    r = lax.broadcasted_iota(jnp.int32, (SGU_BLOCK, SGU_BLOCK), 0)
    c = lax.broadcasted_iota(jnp.int32, (SGU_BLOCK, SGU_BLOCK), 1)
    return (c < CHUNK) | (r >= CHUNK)


def _sgu_fwd(zz, ln_g, ln_b, w_s, b_s, name="sgu_fwd", tm=256):
    T = zz.shape[0]
    tm = min(tm, T)
    W = SGU_WIDTH

    def body(zu_ref, zv_ref, g_ref, b_ref, ws_ref, bs_ref, y_ref):
        u = _gelu(zu_ref[...].astype(F32))
        v = _gelu(zv_ref[...].astype(F32))
        mu = jnp.mean(v, axis=-1, keepdims=True)
        xc = v - mu
        rstd = lax.rsqrt(jnp.mean(xc * xc, axis=-1, keepdims=True) + EPS)
        vn = (xc * rstd * g_ref[...] + b_ref[...]).astype(BF16)
        mask = _sgu_mask()
        for g in range(SGU_GROUPS):
            wm = jnp.where(mask, ws_ref[g], 0.0).astype(BF16)
            cs = slice(g * SGU_GW, (g + 1) * SGU_GW)
            for blk in range(tm // SGU_BLOCK):
                rs = slice(blk * SGU_BLOCK, (blk + 1) * SGU_BLOCK)
                mixed = _dot(wm, vn[rs, cs], NN) + bs_ref[g]
                y_ref[rs, cs] = (u[rs, cs] * mixed).astype(y_ref.dtype)

    return pl.pallas_call(
        body,
        grid=(T // tm,),
        in_specs=[pl.BlockSpec((tm, W), lambda i: (i, 0)), pl.BlockSpec((tm, W), lambda i: (i, 1)),
                  pl.BlockSpec((1, W), lambda i: (0, 0)), pl.BlockSpec((1, W), lambda i: (0, 0)),
                  pl.BlockSpec((SGU_GROUPS, SGU_BLOCK, SGU_BLOCK), lambda i: (0, 0, 0)),
                  pl.BlockSpec((SGU_GROUPS, SGU_BLOCK, 1), lambda i: (0, 0, 0))],
        out_specs=pl.BlockSpec((tm, W), lambda i: (i, 0)),
        out_shape=jax.ShapeDtypeStruct((T, W), BF16),
        compiler_params=_cp("parallel"),
        name=name,
    )(zz, zz, ln_g.reshape(1, W), ln_b.reshape(1, W), w_s, b_s.reshape(SGU_GROUPS, SGU_BLOCK, 1))


def _sgu_bwd(zz, dy, ln_g, ln_b, w_s, b_s, name="sgu_bwd", tm=256):
    T = zz.shape[0]
    tm = min(tm, T)
    W = SGU_WIDTH

    def body(zu_ref, zv_ref, dy_ref, g_ref, b_ref, ws_ref, bs_ref, dzz_ref, dws_ref, dbs_ref, dg_ref, db_ref, dvn_ref):
        i = pl.program_id(0)

        @pl.when(i == 0)
        def _():
            dws_ref[...] = jnp.zeros_like(dws_ref)
            dbs_ref[...] = jnp.zeros_like(dbs_ref)
            dg_ref[...] = jnp.zeros_like(dg_ref)
            db_ref[...] = jnp.zeros_like(db_ref)

        u, du_dz = _gelu_and_grad(zu_ref[...].astype(F32))
        v, dv_dz = _gelu_and_grad(zv_ref[...].astype(F32))
        mu = jnp.mean(v, axis=-1, keepdims=True)
        xc = v - mu
        rstd = lax.rsqrt(jnp.mean(xc * xc, axis=-1, keepdims=True) + EPS)
        xhat = xc * rstd
        gv = g_ref[...]
        vn = (xhat * gv + b_ref[...]).astype(BF16)
        dyv = dy_ref[...].astype(F32)
        mask = _sgu_mask()
        for g in range(SGU_GROUPS):
            wm = jnp.where(mask, ws_ref[g], 0.0).astype(BF16)
            cs = slice(g * SGU_GW, (g + 1) * SGU_GW)
            dw_acc = jnp.zeros((SGU_BLOCK, SGU_BLOCK), F32)
            db_acc = jnp.zeros((SGU_BLOCK, 1), F32)
            for blk in range(tm // SGU_BLOCK):
                rs = slice(blk * SGU_BLOCK, (blk + 1) * SGU_BLOCK)
                vn_bg = vn[rs, cs]
                mixed = _dot(wm, vn_bg, NN) + bs_ref[g]
                dy_bg = dyv[rs, cs]
                dmixed = dy_bg * u[rs, cs]
                dmb = dmixed.astype(BF16)
                dw_acc += _dot(dmb, vn_bg, NT)
                db_acc += jnp.sum(dmixed, axis=1, keepdims=True)
                dvn_ref[rs, cs] = _dot(wm, dmb, TN)
                dzz_ref[rs, cs] = (dy_bg * mixed * du_dz[rs, cs]).astype(dzz_ref.dtype)
            dws_ref[g] += jnp.where(mask, dw_acc, 0.0)
            dbs_ref[g] += db_acc
        dvn = dvn_ref[...]
        dg_ref[...] += jnp.sum(dvn * xhat, axis=0, keepdims=True)
        db_ref[...] += jnp.sum(dvn, axis=0, keepdims=True)
        dxh = dvn * gv
        m1 = jnp.mean(dxh, axis=-1, keepdims=True)
        m2 = jnp.mean(dxh * xhat, axis=-1, keepdims=True)
        dv = rstd * (dxh - m1 - xhat * m2)
        dzz_ref[:, W:] = (dv * dv_dz).astype(dzz_ref.dtype)

    vec = pl.BlockSpec((1, W), lambda i: (0, 0))
    ws_spec = pl.BlockSpec((SGU_GROUPS, SGU_BLOCK, SGU_BLOCK), lambda i: (0, 0, 0))
    bs_spec = pl.BlockSpec((SGU_GROUPS, SGU_BLOCK, 1), lambda i: (0, 0, 0))
    return pl.pallas_call(
        body,
        grid=(T // tm,),
        in_specs=[pl.BlockSpec((tm, W), lambda i: (i, 0)), pl.BlockSpec((tm, W), lambda i: (i, 1)),
                  pl.BlockSpec((tm, W), lambda i: (i, 0)), vec, vec, ws_spec, bs_spec],
        out_specs=[pl.BlockSpec((tm, 2 * W), lambda i: (i, 0)), ws_spec, bs_spec, vec, vec],
        out_shape=[jax.ShapeDtypeStruct((T, 2 * W), BF16),
                   jax.ShapeDtypeStruct((SGU_GROUPS, SGU_BLOCK, SGU_BLOCK), F32),
                   jax.ShapeDtypeStruct((SGU_GROUPS, SGU_BLOCK, 1), F32),
                   jax.ShapeDtypeStruct((1, W), F32), jax.ShapeDtypeStruct((1, W), F32)],
        scratch_shapes=[pltpu.VMEM((tm, W), F32)],
        compiler_params=_cp("arbitrary"),
        name=name,
    )(zz, zz, dy, ln_g.reshape(1, W), ln_b.reshape(1, W), w_s, b_s.reshape(SGU_GROUPS, SGU_BLOCK, 1))


RET_TR = 256
RET_BLK = 256
QK_SCALE = RET_QK_DIM ** -0.5


def _ret_tables(T):
    half = RET_QK_DIM // 2
    inv = 1.0 / (10000.0 ** jnp.linspace(0.0, 1.0, half, dtype=F32))
    ang = jnp.arange(T).astype(F32)[:, None] * inv[None, :]
    cos, sin = jnp.cos(ang), jnp.sin(ang)
    rot_c = jnp.concatenate([cos, cos], axis=1)
    rot_s = jnp.concatenate([-sin, sin], axis=1)
    log_g = jnp.log1p(-jnp.exp2(-5.0 - jnp.arange(RET_HEADS, dtype=F32)))
    idx = jnp.arange(RET_BLK, dtype=F32)
    dist = idx[:, None] - idx[None, :]
    cq, ck = jnp.arange(RET_BLK)[:, None] // CHUNK, jnp.arange(RET_BLK)[None, :] // CHUNK
    expo = jnp.where(ck == cq, jnp.abs(dist), dist)
    d_blk = jnp.where((ck <= cq)[None], jnp.exp(log_g[:, None, None] * expo[None]), 0.0)
    k_dec = jnp.exp(log_g[:, None] * (RET_BLK - 1 - idx)[None, :])[:, :, None]
    q_dec = jnp.exp(log_g[:, None] * (idx + 1.0)[None, :])[:, :, None]
    c_dec = jnp.exp(log_g * RET_BLK)[:, None, None]
    return rot_c, rot_s, d_blk, q_dec, k_dec, c_dec


def _rot(x, c, s):
    return x * c + pltpu.roll(x, RET_QK_DIM // 2, 1) * s


def _ret_specs(tr, rev, nb):
    ix = (lambda n: nb - 1 - n) if rev else (lambda n: n)
    tab = pl.BlockSpec((tr, RET_QK_DIM), lambda n: (ix(n), 0))
    dm = pl.BlockSpec((RET_HEADS, RET_BLK, RET_BLK), lambda n: (0, 0, 0))
    dv = pl.BlockSpec((RET_HEADS, RET_BLK, 1), lambda n: (0, 0, 0))
    dc = pl.BlockSpec((RET_HEADS, 1, 1), lambda n: (0, 0, 0))
    return ix, [tab, tab, dm, dv, dv, dc]


def _ret_fwd(z_a, tables, name="ret_fwd"):
    T = z_a.shape[0]
    tr = min(RET_TR, T)
    cpb = tr // RET_BLK
    nb = T // tr
    QW, VW = RET_HEADS * RET_QK_DIM, RET_HEADS * RET_V_DIM
    ix, tab_specs = _ret_specs(tr, False, nb)

    def body(z_ref, c_ref, s_ref, dm_ref, qd_ref, kd_ref, cd_ref, y_ref, st_ref, state):
        @pl.when(pl.program_id(0) == 0)
        def _():
            state[...] = jnp.zeros_like(state)

        for c in range(cpb):
            for h in range(RET_HEADS):
                rs = slice(c * RET_BLK, (c + 1) * RET_BLK)
                cc, ss = c_ref[rs, :], s_ref[rs, :]
                q = z_ref[rs, h * RET_QK_DIM:(h + 1) * RET_QK_DIM].astype(F32)
                k = z_ref[rs, QW + h * RET_QK_DIM:QW + (h + 1) * RET_QK_DIM].astype(F32)
                v = z_ref[rs, 2 * QW + h * RET_V_DIM:2 * QW + (h + 1) * RET_V_DIM]
                gt = z_ref[rs, 2 * QW + VW + h * RET_V_DIM:2 * QW + VW + (h + 1) * RET_V_DIM].astype(F32)
                qr = _rot(q, cc, ss)
                kr = _rot(k, cc, ss) * QK_SCALE
                s_old = state[h]
                sb = s_old.astype(BF16)
                st_ref[c, h] = sb
                s = _dot(qr.astype(BF16), kr.astype(BF16), NT) * dm_ref[h]
                o = _dot(s.astype(BF16), v, NN) + _dot((qr * qd_ref[h]).astype(BF16), sb, NN)
                state[h] = s_old * cd_ref[h] + _dot((kr * kd_ref[h]).astype(BF16), v, TN)
                mu = jnp.mean(o, axis=-1, keepdims=True)
                oc = o - mu
                rn = oc * lax.rsqrt(jnp.mean(oc * oc, axis=-1, keepdims=True) + EPS)
                silu = gt / (1.0 + jnp.exp(-gt))
                y_ref[rs, h * RET_V_DIM:(h + 1) * RET_V_DIM] = (silu * rn).astype(y_ref.dtype)

    return pl.pallas_call(
        body,
        grid=(nb,),
        in_specs=[pl.BlockSpec((tr, RET_W), lambda n: (n, 0))] + tab_specs,
        out_specs=[pl.BlockSpec((tr, VW), lambda n: (n, 0)),
                   pl.BlockSpec((cpb, RET_HEADS, RET_QK_DIM, RET_V_DIM), lambda n: (n, 0, 0, 0))],
        out_shape=[jax.ShapeDtypeStruct((T, Y_COLS), BF16),
                   jax.ShapeDtypeStruct((T // RET_BLK, RET_HEADS, RET_QK_DIM, RET_V_DIM), BF16)],
        scratch_shapes=[pltpu.VMEM((RET_HEADS, RET_QK_DIM, RET_V_DIM), F32)],
        compiler_params=_cp("arbitrary"),
        name=name,
    )(z_a, *tables)


def _ret_bwd(z_a, dy, states, tables, name="ret_bwd"):
    T = z_a.shape[0]
    tr = min(RET_TR, T)
    cpb = tr // RET_BLK
    nb = T // tr
    QW, VW = RET_HEADS * RET_QK_DIM, RET_HEADS * RET_V_DIM
    ix, tab_specs = _ret_specs(tr, True, nb)

    def body(z_ref, dy_ref, st_ref, c_ref, s_ref, dm_ref, qd_ref, kd_ref, cd_ref, dz_ref, dstate):
        @pl.when(pl.program_id(0) == 0)
        def _():
            dstate[...] = jnp.zeros_like(dstate)

        for c in reversed(range(cpb)):
            for h in range(RET_HEADS):
                rs = slice(c * RET_BLK, (c + 1) * RET_BLK)
                cc, ss = c_ref[rs, :], s_ref[rs, :]
                q = z_ref[rs, h * RET_QK_DIM:(h + 1) * RET_QK_DIM].astype(F32)
                k = z_ref[rs, QW + h * RET_QK_DIM:QW + (h + 1) * RET_QK_DIM].astype(F32)
                v = z_ref[rs, 2 * QW + h * RET_V_DIM:2 * QW + (h + 1) * RET_V_DIM]
                gt = z_ref[rs, 2 * QW + VW + h * RET_V_DIM:2 * QW + VW + (h + 1) * RET_V_DIM].astype(F32)
                dyv = dy_ref[rs, h * RET_V_DIM:(h + 1) * RET_V_DIM].astype(F32)
                dmat, qd, kd = dm_ref[h], qd_ref[h], kd_ref[h]
                qr = _rot(q, cc, ss)
                kr = _rot(k, cc, ss) * QK_SCALE
                qrb, krb = qr.astype(BF16), kr.astype(BF16)
                sb = st_ref[c, h]
                sd = (_dot(qrb, krb, NT) * dmat).astype(BF16)
                qdb = (qr * qd).astype(BF16)
                kdb = (kr * kd).astype(BF16)
                o = _dot(sd, v, NN) + _dot(qdb, sb, NN)
                mu = jnp.mean(o, axis=-1, keepdims=True)
                oc = o - mu
                rstd = lax.rsqrt(jnp.mean(oc * oc, axis=-1, keepdims=True) + EPS)
                rn = oc * rstd
                sg = 1.0 / (1.0 + jnp.exp(-gt))
                dgt = dyv * rn * (sg * (1.0 + gt * (1.0 - sg)))
                drn = dyv * (gt * sg)
                do = rstd * (drn - jnp.mean(drn, axis=-1, keepdims=True) - rn * jnp.mean(drn * rn, axis=-1, keepdims=True))
                dob = do.astype(BF16)
                dsn = dstate[h]
                dsnb = dsn.astype(BF16)
                ds_raw = (_dot(dob, v, NT) * dmat).astype(BF16)
                dv = _dot(sd, dob, TN) + _dot(kdb, dsnb, NN)
                dqr = _dot(ds_raw, krb, NN) + qd * _dot(dob, sb, NT)
                dkr = (_dot(ds_raw, qrb, TN) + kd * _dot(v, dsnb, NT)) * QK_SCALE
                dstate[h] = dsn * cd_ref[h] + _dot(qdb, dob, TN)
                dq = dqr * cc + pltpu.roll(dqr * ss, RET_QK_DIM // 2, 1)
                dk = dkr * cc + pltpu.roll(dkr * ss, RET_QK_DIM // 2, 1)
                dz_ref[rs, h * RET_QK_DIM:(h + 1) * RET_QK_DIM] = dq.astype(dz_ref.dtype)
                dz_ref[rs, QW + h * RET_QK_DIM:QW + (h + 1) * RET_QK_DIM] = dk.astype(dz_ref.dtype)
                dz_ref[rs, 2 * QW + h * RET_V_DIM:2 * QW + (h + 1) * RET_V_DIM] = dv.astype(dz_ref.dtype)
                dz_ref[rs, 2 * QW + VW + h * RET_V_DIM:2 * QW + VW + (h + 1) * RET_V_DIM] = dgt.astype(dz_ref.dtype)

    return pl.pallas_call(
        body,
        grid=(nb,),
        in_specs=[pl.BlockSpec((tr, RET_W), lambda n: (ix(n), 0)),
                  pl.BlockSpec((tr, VW), lambda n: (ix(n), 0)),
                  pl.BlockSpec((cpb, RET_HEADS, RET_QK_DIM, RET_V_DIM), lambda n: (ix(n), 0, 0, 0))] + tab_specs,
        out_specs=pl.BlockSpec((tr, RET_W), lambda n: (ix(n), 0)),
        out_shape=jax.ShapeDtypeStruct((T, RET_W), BF16),
        scratch_shapes=[pltpu.VMEM((RET_HEADS, RET_QK_DIM, RET_V_DIM), F32)],
        compiler_params=_cp("arbitrary"),
        name=name,
    )(z_a, dy, states, *tables)


ATT_TQ = 256
ATT_CPB = ATT_TQ // CHUNK
ATT_SCALE = ATT_HEAD_DIM ** -0.5


ATT_WIN = 3 * ATT_TQ
ATT_NB = CHUNK * ATT_BAND


def _rel_index():
    i = np.arange(CHUNK)[:, None]
    j = np.arange(ATT_BAND)[None, :]
    rel = np.clip(i + ATT_PAST * CHUNK - j, -MAX_REL, MAX_REL) + MAX_REL
    return jnp.asarray(rel.reshape(1, ATT_NB).astype(np.int32))


def _split3(x):
    hi = x.astype(BF16)
    r1 = x - hi.astype(F32)
    mid = r1.astype(BF16)
    lo = (r1 - mid.astype(F32)).astype(BF16)
    return hi, mid, lo


REL_TILE = 4608


def _bias_expand(rel_bias, name="bias_expand"):
    H = rel_bias.shape[0]
    n = ATT_NB
    padded = jnp.pad(rel_bias, ((0, 0), (0, N_REL_PAD - N_REL)))

    def body(rb_ref, idx_ref, o_ref):
        onehot = (lax.broadcasted_iota(jnp.int32, (N_REL_PAD, REL_TILE), 0) == idx_ref[...]).astype(BF16)
        hi, mid, lo = _split3(rb_ref[...])
        o_ref[...] = _dot(hi, onehot, NN) + _dot(mid, onehot, NN) + _dot(lo, onehot, NN)

    out = pl.pallas_call(
        body,
        grid=(n // REL_TILE,),
        in_specs=[pl.BlockSpec((H, N_REL_PAD), lambda t: (0, 0)), pl.BlockSpec((1, REL_TILE), lambda t: (0, t))],
        out_specs=pl.BlockSpec((H, REL_TILE), lambda t: (0, t)),
        out_shape=jax.ShapeDtypeStruct((H, n), F32),
        compiler_params=_cp("parallel"),
        name=name,
    )(padded, _rel_index())
    return out.reshape(H, CHUNK, ATT_BAND)


def _bias_tile(band, name="bias_tile"):
    H = band.shape[0]
    padded = jnp.pad(band, ((0, 0), (0, 0), (0, ATT_WIN - ATT_BAND)), constant_values=NEG_INF)

    def body(b_ref, o_ref):
        b = b_ref[...]
        col = lax.broadcasted_iota(jnp.int32, (CHUNK, ATT_WIN), 1)
        keep = col >= (2 - pl.program_id(0)) * ATT_TQ
        for a in range(ATT_CPB):
            o_ref[a * CHUNK:(a + 1) * CHUNK, :] = jnp.where(keep, pltpu.roll(b, a * CHUNK, 1) if a else b, NEG_INF)

    return pl.pallas_call(
        body,
        grid=(3, H),
        in_specs=[pl.BlockSpec((None, CHUNK, ATT_WIN), lambda v, h: (h, 0, 0))],
        out_specs=pl.BlockSpec((None, None, ATT_TQ, ATT_WIN), lambda v, h: (v, h, 0, 0)),
        out_shape=jax.ShapeDtypeStruct((3, H, ATT_TQ, ATT_WIN), F32),
        compiler_params=_cp("parallel", "parallel"),
        name=name,
    )(padded)


def _bias_untile(dtile, name="bias_untile"):
    H = dtile.shape[0]

    def body(d_ref, o_ref):
        acc = d_ref[0:CHUNK, :]
        for a in range(1, ATT_CPB):
            acc = acc + pltpu.roll(d_ref[a * CHUNK:(a + 1) * CHUNK, :], ATT_WIN - a * CHUNK, 1)
        o_ref[...] = acc

    out = pl.pallas_call(
        body,
        grid=(H,),
        in_specs=[pl.BlockSpec((None, ATT_TQ, ATT_WIN), lambda h: (h, 0, 0))],
        out_specs=pl.BlockSpec((None, CHUNK, ATT_WIN), lambda h: (h, 0, 0)),
        out_shape=jax.ShapeDtypeStruct((H, CHUNK, ATT_WIN), F32),
        compiler_params=_cp("parallel"),
        name=name,
    )(dtile)
    return out[:, :, :ATT_BAND]


def _bias_reduce(dbias, name="bias_reduce"):
    H = dbias.shape[0]
    n = ATT_NB

    def body(db_ref, idx_ref, o_ref):
        @pl.when(pl.program_id(0) == 0)
        def _():
            o_ref[...] = jnp.zeros_like(o_ref)

        onehot = (lax.broadcasted_iota(jnp.int32, (N_REL_PAD, REL_TILE), 0) == idx_ref[...]).astype(BF16)
        hi, mid, lo = _split3(db_ref[...])
        o_ref[...] += _dot(hi, onehot, NT) + _dot(mid, onehot, NT) + _dot(lo, onehot, NT)

    out = pl.pallas_call(
        body,
        grid=(n // REL_TILE,),
        in_specs=[pl.BlockSpec((H, REL_TILE), lambda t: (0, t)), pl.BlockSpec((1, REL_TILE), lambda t: (0, t))],
        out_specs=pl.BlockSpec((H, N_REL_PAD), lambda t: (0, 0)),
        out_shape=jax.ShapeDtypeStruct((H, N_REL_PAD), F32),
        compiler_params=_cp("arbitrary"),
        name=name,
    )(dbias.reshape(H, n), _rel_index())
    return out[:, :N_REL]


def _att_probs(q, kwin, bias):
    s = _dot(q, kwin, NT) + bias
    e = jnp.exp(s - jnp.max(s, axis=-1, keepdims=True))
    return e * (1.0 / jnp.sum(e, axis=-1, keepdims=True))


ATT_PAIR = 2 * ATT_HEAD_DIM
ATT_NP = ATT_HEADS // 2
ATT_QW = ATT_HEADS * ATT_HEAD_DIM
Y_COLS = RET_HEADS * RET_V_DIM + ATT_QW


def _att_specs(tq, nq, clip_q, q_col0):
    cb = ATT_QW // ATT_PAIR
    qi = (lambda p, m: (jnp.minimum(m, nq - 1), q_col0 + p)) if clip_q else (lambda p, m: (m, q_col0 + p))
    q = pl.BlockSpec((tq, ATT_PAIR), qi)

    def win(col0):
        return [pl.BlockSpec((tq, ATT_PAIR), functools.partial(lambda p, m, back: (jnp.clip(m - back, 0, nq - 1), col0 + p), back=b))
                for b in (2, 1, 0)]

    bias = pl.BlockSpec((None, 2, ATT_TQ, ATT_WIN), lambda p, m: (jnp.minimum(m, 2), p, 0, 0))
    return q, win(cb), win(2 * cb), bias


def _head_masks(rows):
    lane = lax.broadcasted_iota(jnp.int32, (rows, ATT_PAIR), 1)
    return lane < ATT_HEAD_DIM


def _att_fwd(z_b, bias, y, comm=None, name="att_fwd"):
    T = z_b.shape[0]
    tq = ATT_TQ
    nq = T // tq
    qs, kwin, vwin, bs = _att_specs(tq, nq, False, 0)
    nc = comm.n if comm else 0
    total = ATT_NP * nq

    def body(*refs):
        q_ref, k0, k1, k2, v0, v1, v2, b_ref = refs[:8]
        cin = refs[9:9 + nc]
        o_ref = refs[9 + nc]
        cout = refs[10 + nc:10 + 2 * nc]
        csem = refs[10 + 2 * nc:]
        m = pl.program_id(1)
        step = pl.program_id(0) * nq + m
        _host_comm(comm, "early", step, total, cin, cout, csem)
        kw = jnp.concatenate([k0[...], k1[...], k2[...]], axis=0)
        vw = jnp.concatenate([v0[...], v1[...], v2[...]], axis=0)
        q2 = q_ref[...] * ATT_SCALE
        even = _head_masks(tq)
        outs = []
        for hh in range(2):
            qm = jnp.where(even if hh == 0 else ~even, q2, jnp.zeros_like(q2))
            p = _att_probs(qm, kw, b_ref[hh])
            outs.append(_dot(p.astype(BF16), vw, NN))
        o_ref[...] = jnp.where(even, outs[0], outs[1]).astype(o_ref.dtype)
        _host_comm(comm, "late", step, total, cin, cout, csem)

    y_cb = (Y_COLS - ATT_QW) // ATT_PAIR
    out = pl.pallas_call(
        body,
        grid=(ATT_NP, nq),
        in_specs=[qs] + kwin + vwin + [bs, ANY] + [ANY] * nc,
        out_specs=[pl.BlockSpec((tq, ATT_PAIR), lambda p, m: (m, y_cb + p))] + [ANY] * nc,
        out_shape=[jax.ShapeDtypeStruct((T, Y_COLS), BF16)] + (comm.out_shape if comm else []),
        scratch_shapes=comm.scratch if comm else [],
        input_output_aliases={8: 0},
        compiler_params=_cp("arbitrary", "arbitrary"),
        name=name,
    )(z_b, z_b, z_b, z_b, z_b, z_b, z_b, bias, y, *(comm.arrays if comm else []))
    return out[0], list(out[1:])


def _att_bwd(z_b, bias, dy, comm=None, name="att_bwd"):
    T = z_b.shape[0]
    tq = ATT_TQ
    nq = T // tq
    y_cb = (Y_COLS - ATT_QW) // ATT_PAIR
    qs, kwin, vwin, bs = _att_specs(tq, nq, True, 0)
    dos = _att_specs(tq, nq, True, y_cb)[0]
    kv_out = pl.BlockSpec((tq, ATT_PAIR), lambda p, m: (jnp.maximum(m - 2, 0), p))
    W3 = 3 * tq
    nc = comm.n if comm else 0
    total = ATT_NP * (nq + 2)

    def body(*refs):
        q_ref, k0, k1, k2, v0, v1, v2, b_ref, do_ref = refs[:9]
        cin = refs[9:9 + nc]
        dq_ref, dk_ref, dv_ref, db_ref = refs[9 + nc:13 + nc]
        cout = refs[13 + nc:13 + 2 * nc]
        dkc, dvc, dkw, dvw = refs[13 + 2 * nc:17 + 2 * nc]
        csem = refs[17 + 2 * nc:]
        m = pl.program_id(1)
        step = pl.program_id(0) * (nq + 2) + m
        _host_comm(comm, "early", step, total, cin, cout, csem)

        @pl.when(m == 0)
        def _():
            dkc[...] = jnp.zeros_like(dkc)
            dvc[...] = jnp.zeros_like(dvc)
            db_ref[...] = jnp.zeros_like(db_ref)

        @pl.when(m >= nq)
        def _():
            dkw[...] = jnp.zeros_like(dkw)
            dvw[...] = jnp.zeros_like(dvw)

        @pl.when(m < nq)
        def _():
            kw = jnp.concatenate([k0[...], k1[...], k2[...]], axis=0)
            vw = jnp.concatenate([v0[...], v1[...], v2[...]], axis=0)
            q2, do2 = q_ref[...] * ATT_SCALE, do_ref[...]
            even = _head_masks(tq)
            dqs, dks, dvs = [], [], []
            for hh in range(2):
                mine = even if hh == 0 else ~even
                p = _att_probs(jnp.where(mine, q2, jnp.zeros_like(q2)), kw, b_ref[hh])
                dp = _dot(jnp.where(mine, do2, jnp.zeros_like(do2)), vw, NT)
                ds = p * (dp - jnp.sum(dp * p, axis=-1, keepdims=True))
                db_ref[hh] += ds
                dsb = ds.astype(BF16)
                dqs.append(_dot(dsb, kw, NN))
                dks.append(_dot(dsb, q2, TN))
                dvs.append(_dot(p.astype(BF16), do2, TN))
            even_w = _head_masks(W3)
            dq_ref[...] = (jnp.where(even, dqs[0], dqs[1]) * ATT_SCALE).astype(dq_ref.dtype)
            dkw[...] = jnp.where(even_w, dks[0], dks[1])
            dvw[...] = jnp.where(even_w, dvs[0], dvs[1])

        dk_ref[...] = (dkc[0:tq, :] + dkw[0:tq, :]).astype(dk_ref.dtype)
        dv_ref[...] = (dvc[0:tq, :] + dvw[0:tq, :]).astype(dv_ref.dtype)
        dkc[0:tq, :] = dkc[tq:2 * tq, :] + dkw[tq:2 * tq, :]
        dvc[0:tq, :] = dvc[tq:2 * tq, :] + dvw[tq:2 * tq, :]
        dkc[tq:2 * tq, :] = dkw[2 * tq:W3, :]
        dvc[tq:2 * tq, :] = dvw[2 * tq:W3, :]
        _host_comm(comm, "late", step, total, cin, cout, csem)

    qo = pl.BlockSpec((tq, ATT_PAIR), lambda p, m: (jnp.minimum(m, nq - 1), p))
    dbs = pl.BlockSpec((2, ATT_TQ, ATT_WIN), lambda p, m: (p, 0, 0))
    hd = jax.ShapeDtypeStruct((T, ATT_QW), BF16)
    out = pl.pallas_call(
        body,
        grid=(ATT_NP, nq + 2),
        in_specs=[qs] + kwin + vwin + [bs, dos] + [ANY] * nc,
        out_specs=[qo, kv_out, kv_out, dbs] + [ANY] * nc,
        out_shape=[hd, hd, hd, jax.ShapeDtypeStruct((ATT_HEADS, ATT_TQ, ATT_WIN), F32)] + (comm.out_shape if comm else []),
        scratch_shapes=[pltpu.VMEM((2 * tq, ATT_PAIR), F32), pltpu.VMEM((2 * tq, ATT_PAIR), F32),
                        pltpu.VMEM((W3, ATT_PAIR), F32), pltpu.VMEM((W3, ATT_PAIR), F32)] + (comm.scratch if comm else []),
        compiler_params=_cp("arbitrary", "arbitrary"),
        name=name,
    )(z_b, z_b, z_b, z_b, z_b, z_b, z_b, bias, dy, *(comm.arrays if comm else []))
    return out[0], out[1], out[2], out[3], list(out[4:])


REST = ["ab_out", "c_in_t", "c_out", "up_t0", "up_t1", "down0", "down1"]


def _local_step(x, target, wt, small, rest_shards=None, overlap=False):
    T = x.shape[0]
    Fh = FFN_HIDDEN
    tables = _ret_tables(T)
    gw, gs, recv = {}, {}, {}
    wt = dict(wt)

    hn0 = _rms_fwd(x, small["attn_norm_g"][0], name="rms_fwd")
    z_a = _mm(hn0, wt["ab_in_t"][:RET_W], "nt", BF16, name="mm_ab_in_a")
    z_b = _mm(hn0, wt["ab_in_t"][RET_W:], "nt", BF16, name="mm_ab_in_b")
    y, states = _ret_fwd(z_a, tables)
    bias = _bias_tile(_bias_expand(small["rel_bias"]))
    y, rest = _att_fwd(z_b, bias, y, comm=_Comm("gather", rest_shards) if rest_shards is not None else None)
    if rest_shards is not None:
        full = dict(zip(REST, rest))
        wt.update(ab_out=full["ab_out"], c_in_t=full["c_in_t"], c_out=full["c_out"],
                  up_t=[full["up_t0"], full["up_t1"]], down=[full["down0"], full["down1"]])
    h1, hf0 = _mm_rows(y, wt["ab_out"], x, "mm_ab_out", norm_g=small["ffn_norm_g"][0])

    def ffn_fwd(h, hf, layer, next_g):
        zg = _mm(hf, wt["up_t"][layer][:Fh], "nt", BF16, name="mm_up")
        zu = _mm(hf, wt["up_t"][layer][Fh:], "nt", BF16, name="mm_up")
        a = _ffn_mid_fwd(zg, zu, small["conv_w"][layer], small["conv_b"][layer][None, :])
        if next_g is None:
            h_out, hn_next = _mm(a, wt["down"][layer], "nn", F32, res=h, name="mm_down_last"), None
        else:
            h_out, hn_next = _mm_rows(a, wt["down"][layer], h, "mm_down", norm_g=next_g)
        return h_out, hn_next, (hf, zg, zu, a)

    def ffn_bwd(dh_out, h, layer, saved, exchange=None):
        hf, zg, zu, a = saved
        da = _mm(dh_out, wt["down"][layer], "nt", BF16, name="mm_d_a")
        d_down = _mm(a, dh_out, "tn", BF16, name="mm_dw_down")
        comm = _Comm("exchange", [gw[k] for k in exchange]) if exchange else None
        dzg, dzu, dcw, dcb, got = _ffn_mid_bwd(zg, zu, da, small["conv_w"][layer], small["conv_b"][layer][None, :], comm=comm)
        recv.update(zip(exchange or [], got))
        dhf = _mm(dzg, wt["up_t"][layer][:Fh], "nn", F32, name="mm_d_hf")
        dh, dg = _mm_rows(dzu, wt["up_t"][layer][Fh:], dhf, "mm_d_hf_norm", bwd=(h, small["ffn_norm_g"][layer], dh_out))
        d_up = jnp.concatenate([_mm(dzg, hf, "tn", BF16, name="mm_dw_up"), _mm(dzu, hf, "tn", BF16, name="mm_dw_up")], axis=0)
        return dh, dg, d_up, d_down, dcw, dcb

    h2, hn1, ffn0 = ffn_fwd(h1, hf0, 0, small["attn_norm_g"][1])

    zz = _mm(hn1, wt["c_in_t"], "nt", BF16, name="mm_c_in")
    ys = _sgu_fwd(zz, small["ln_g"], small["ln_b"], small["w_s"], small["b_s"])
    h3, hf1 = _mm_rows(ys, wt["c_out"], h2, "mm_c_out", norm_g=small["ffn_norm_g"][1])
    h4, _, ffn1 = ffn_fwd(h3, hf1, 1, None)

    loss_vec, dh4, gs["final_g"] = _final_loss(h4, small["final_g"], target)

    dh3, dgf1, gw["up_t1"], gw["down1"], dcw1, dcb1 = ffn_bwd(dh4, h3, 1, ffn1)
    dys = _mm(dh3, wt["c_out"], "nt", BF16, name="mm_d_ys")
    gw["c_out"] = _mm(ys, dh3, "tn", BF16, name="mm_dw_c_out")
    dzz, gs["w_s"], dbs, dlg, dlb = _sgu_bwd(zz, dys, small["ln_g"], small["ln_b"], small["w_s"], small["b_s"])
    gs["b_s"], gs["ln_g"], gs["ln_b"] = dbs[:, :, 0], dlg[0], dlb[0]
    dh2, dga1 = _mm_rows(dzz, wt["c_in_t"], None, "mm_d_hn1_norm", bwd=(h2, small["attn_norm_g"][1], dh3))
    gw["c_in_t"] = _mm(dzz, hn1, "tn", BF16, name="mm_dw_c_in")

    dh1, dgf0, gw["up_t0"], gw["down0"], dcw0, dcb0 = ffn_bwd(
        dh2, h1, 0, ffn0, exchange=["c_in_t", "c_out", "up_t1", "down1"] if overlap else None)

    dy = _mm(dh1, wt["ab_out"], "nt", BF16, name="mm_d_y")
    gw["ab_out"] = _mm(y, dh1, "tn", BF16, name="mm_dw_ab_out")
    dz_a = _ret_bwd(z_a, dy, states, tables)
    late = ["ab_out", "up_t0", "down0"] if overlap else []
    dq, dk, dv, dbias, got = _att_bwd(z_b, bias, dy, comm=_Comm("exchange", [gw[k] for k in late]) if late else None)
    recv.update(zip(late, got))
    dz_b = jnp.concatenate([dq, dk, dv], axis=1)
    gs["rel_bias"] = _bias_reduce(_bias_untile(dbias))
    dhn0 = _mm(dz_a, wt["ab_in_t"][:RET_W], "nn", F32, name="mm_d_hn0")
    grad_x, dga0 = _mm_rows(dz_b, wt["ab_in_t"][RET_W:], dhn0, "mm_d_hn0_norm", bwd=(x, small["attn_norm_g"][0], dh1))
    gw["ab_in_t"] = jnp.concatenate([_mm(dz_a, hn0, "tn", BF16, name="mm_dw_ab_in_a"),
                                     _mm(dz_b, hn0, "tn", BF16, name="mm_dw_ab_in_b")], axis=0)

    gs["attn_norm_g"] = jnp.concatenate([dga0, dga1], axis=0)
    gs["ffn_norm_g"] = jnp.concatenate([dgf0, dgf1], axis=0)
    gs["conv_w"] = jnp.stack([dcw0, dcw1])
    gs["conv_b"] = jnp.concatenate([dcb0, dcb1], axis=0)
    gs["final_g"] = gs["final_g"][0]
    return loss_vec[0, 0], grad_x, gw, gs, recv


MESH_ID = pl.DeviceIdType.MESH
ANY = pl.BlockSpec(memory_space=pl.ANY)


def _my_place():
    return lax.axis_index("x"), lax.axis_index("y"), lax.axis_index("c")


class _Comm:
    def __init__(self, kind, arrays):
        self.kind, self.arrays, self.n = kind, list(arrays), len(arrays)
        if kind == "gather":
            self.out_shape = [jax.ShapeDtypeStruct((N_DEV * s.shape[0], s.shape[1]), s.dtype) for s in arrays]
        else:
            self.out_shape = [jax.ShapeDtypeStruct((N_DEV, g.shape[0] // N_DEV, g.shape[1]), g.dtype) for g in arrays]
        n = self.n
        self.scratch = [pltpu.SemaphoreType.DMA((n, 7)), pltpu.SemaphoreType.DMA((n, 7)), pltpu.SemaphoreType.DMA((n,))]

    def phase(self, ph, in_refs, out_refs, sems):
        (self._gather if self.kind == "gather" else self._exchange)(ph, in_refs, out_refs, sems)

    def _gather(self, ph, x_refs, o_refs, sems):
        n = self.n
        send_sems, recv_sems, local_sems = sems
        x, y, c = _my_place()
        me, sibling = (x, y, c), (x, y, 1 - c)
        chips = [(1 - x, y), (x, 1 - y), (1 - x, 1 - y)]

        def rows(a, place):
            m = x_refs[a].shape[0]
            px, py, pc = place
            return o_refs[a].at[pl.ds((4 * px + 2 * py + pc) * m, m), :]

        def copy(a, k, block, to, own=False):
            return pltpu.make_async_remote_copy(
                src_ref=x_refs[a] if own else rows(a, block), dst_ref=rows(a, block),
                send_sem=send_sems.at[a, k], recv_sem=recv_sems.at[a, k], device_id=to, device_id_type=MESH_ID)

        def mine():
            return [pltpu.make_async_copy(x_refs[a], rows(a, me), local_sems.at[a]) for a in range(n)]

        def first():
            out = []
            for a in range(n):
                out.append(copy(a, 0, me, sibling, own=True))
                out += [copy(a, 1 + j, me, (*chip, c), own=True) for j, chip in enumerate(chips)]
            return out

        def passed():
            return [copy(a, 4 + j, (*chip, c), sibling) for j, chip in enumerate(chips) for a in range(n)]

        if ph == 0:
            for cp in mine() + first():
                cp.start()
        elif ph == 1:
            fw = passed()
            for j, chip in enumerate(chips):
                for a in range(n):
                    copy(a, 1 + j, (*chip, c), me).wait_recv()
                    fw[j * n + a].start()
        else:
            for a in range(n):
                copy(a, 0, sibling, me).wait_recv()
                for j, chip in enumerate(chips):
                    copy(a, 4 + j, (*chip, 1 - c), me).wait_recv()
            for cp in first() + passed():
                cp.wait_send()
            for cp in mine():
                cp.wait()

    def _exchange(self, ph, g_refs, o_refs, sems):
        n = self.n
        send_sems, recv_sems, local_sems = sems
        x, y, c = _my_place()
        me = 4 * x + 2 * y + c
        peers = [(x ^ ((k >> 2) & 1), y ^ ((k >> 1) & 1), c ^ (k & 1)) for k in range(1, N_DEV)]

        def block(a, idx):
            m = g_refs[a].shape[0] // N_DEV
            return g_refs[a].at[pl.ds(idx * m, m), :]

        def copy(a, k, slot):
            px, py, pc = peers[k]
            return pltpu.make_async_remote_copy(
                src_ref=block(a, 4 * px + 2 * py + pc), dst_ref=o_refs[a].at[slot],
                send_sem=send_sems.at[a, k], recv_sem=recv_sems.at[a, k], device_id=peers[k], device_id_type=MESH_ID)

        if ph == 1:
            return
        mine = [pltpu.make_async_copy(block(a, me), o_refs[a].at[me], local_sems.at[a]) for a in range(n)]
        sends = [copy(a, k, me) for k in range(N_DEV - 1) for a in range(n)]
        if ph == 0:
            for cp in mine + sends:
                cp.start()
        else:
            for k in range(N_DEV - 1):
                px, py, pc = peers[k]
                for a in range(n):
                    copy(a, k, 4 * px + 2 * py + pc).wait_recv()
            for cp in sends:
                cp.wait_send()
            for cp in mine:
                cp.wait()


def _comm_call(comm, name):
    n = comm.n

    def body(*refs):
        for ph in range(3):
            comm.phase(ph, refs[:n], refs[n:2 * n], refs[2 * n:])

    return pl.pallas_call(
        body, out_shape=comm.out_shape, in_specs=[ANY] * n, out_specs=[ANY] * n, scratch_shapes=comm.scratch, name=name,
    )(*comm.arrays)


def _host_comm(comm, when, step, total, cin, cout, csem):
    if comm is None:
        return
    sched = {0: 0, 1: (3 * total) // 4, 2: total - 1}
    phases = (0, 1) if when == "early" else (2,)
    for ph in phases:
        if ph == 1 and comm.kind == "exchange":
            continue

        @pl.when(step == sched[ph])
        def _(ph=ph):
            comm.phase(ph, cin, cout, csem)


def _all_gather(shards, name="all_gather"):
    return _comm_call(_Comm("gather", shards), name)


def _all_to_all(fulls, name="all_to_all"):
    return _comm_call(_Comm("exchange", fulls), name)


def _row_tile(r, target=256):
    best = None
    for t in range(8, min(r, target) + 1, 8):
        if r % t == 0:
            best = t
    return best if best is not None else r


def _sum8(parts, name="sum8"):
    _, M, N = parts.shape
    tr = _row_tile(M, 128)

    def body(p_ref, o_ref):
        acc = p_ref[0].astype(F32)
        for d in range(1, N_DEV):
            acc = acc + p_ref[d].astype(F32)
        o_ref[...] = acc

    return pl.pallas_call(
        body,
        grid=(M // tr,),
        in_specs=[pl.BlockSpec((N_DEV, tr, N), lambda i: (0, i, 0))],
        out_specs=pl.BlockSpec((tr, N), lambda i: (i, 0)),
        out_shape=jax.ShapeDtypeStruct((M, N), F32),
        compiler_params=_cp("parallel"),
        name=name,
    )(parts)


def _adamw(w, g, m, v, name="adamw"):
    shape = w.shape
    if w.ndim == 1:
        r2 = (1, shape[0])
    else:
        r2 = (int(np.prod(shape[:-1])), shape[-1])
    R, C = r2
    tr = _row_tile(R)
    bc1 = 1.0 - ADAM_B1 ** ADAM_STEP
    bc2 = 1.0 - ADAM_B2 ** ADAM_STEP

    def body(w_ref, g_ref, m_ref, v_ref, d_ref, nm_ref, nv_ref):
        gv = g_ref[...]
        nm = ADAM_B1 * m_ref[...] + (1.0 - ADAM_B1) * gv
        nv = ADAM_B2 * v_ref[...] + (1.0 - ADAM_B2) * (gv * gv)
        d_ref[...] = -ADAM_LR * ((nm / bc1) / (jnp.sqrt(nv / bc2) + ADAM_EPS) + ADAM_WD * w_ref[...])
        nm_ref[...] = nm
        nv_ref[...] = nv

    spec = pl.BlockSpec((tr, C), lambda i: (i, 0))
    out = pl.pallas_call(
        body,
        grid=(R // tr,),
        in_specs=[spec] * 4,
        out_specs=[spec] * 3,
        out_shape=[jax.ShapeDtypeStruct(r2, F32)] * 3,
        compiler_params=_cp("parallel"),
        name=name,
    )(w.reshape(r2), g.reshape(r2), m.reshape(r2), v.reshape(r2))
    return [o.reshape(shape) for o in out]


WEIGHTS = ['attn_norm_g', 'ffn_norm_g', 'ab_w_in', 'ab_w_out', 'ab_rel_bias', 'c_w_in', 'c_ln_g', 'c_ln_b', 'c_w_s', 'c_b_s',
           'c_w_out', 'ffn_w_up', 'ffn_conv_w', 'ffn_conv_b', 'ffn_w_down', 'final_norm_g']
SMALL_ORDER = ["attn_norm_g", "ffn_norm_g", "rel_bias", "ln_g", "ln_b", "w_s", "b_s", "conv_w", "conv_b", "final_g"]
PACK_ROW = 1024


def _pack(arrs):
    flat = jnp.concatenate([a.reshape(-1) for a in arrs])
    n = flat.shape[0]
    padded = -(-n // PACK_ROW) * PACK_ROW
    return jnp.pad(flat, (0, padded - n)).reshape(padded // 128, 128)


def _unpack(flat, shapes):
    out, off = [], 0
    for s in shapes:
        n = int(np.prod(s))
        out.append(flat[off:off + n].reshape(s))
        off += n
    return out


def _step(P):
    x, target = P["x"][0], P["loss_target"][0]
    me = 4 * lax.axis_index("x") + 2 * lax.axis_index("y") + lax.axis_index("c")
    n_up = P["ffn_w_up"].shape[0]
    Fc = P["ffn_conv_w"].shape[-1]
    Lc = P["c_ln_g"].shape[-1]

    full = _all_gather([P["ab_w_in"][0].T.astype(BF16), _pack([P["ffn_conv_w"], P["c_ln_g"], P["c_ln_b"]])],
                       name="gather_first")
    wt = {"ab_in_t": full[0]}
    rest = {"ab_out": P["ab_w_out"][0], "c_in_t": P["c_w_in"][0].T, "c_out": P["c_w_out"][0],
            "up_t0": P["ffn_w_up"][0].T, "up_t1": P["ffn_w_up"][1].T, "down0": P["ffn_w_down"][0], "down1": P["ffn_w_down"][1]}
    rest_shards = [rest[k].astype(BF16) for k in REST]
    sm = full[-1].reshape(N_DEV, -1)
    conv_w = sm[:, :n_up * 3 * Fc].reshape(N_DEV, n_up, 3, Fc).transpose(1, 2, 0, 3).reshape(n_up, 3, N_DEV * Fc)
    off = n_up * 3 * Fc
    ln_g = sm[:, off:off + Lc].reshape(N_DEV * Lc)
    ln_b = sm[:, off + Lc:off + 2 * Lc].reshape(N_DEV * Lc)
    small = {"attn_norm_g": P["attn_norm_g"], "ffn_norm_g": P["ffn_norm_g"], "rel_bias": P["ab_rel_bias"][0],
             "ln_g": ln_g, "ln_b": ln_b, "w_s": P["c_w_s"][0], "b_s": P["c_b_s"][0], "conv_w": conv_w,
             "conv_b": P["ffn_conv_b"], "final_g": P["final_norm_g"]}

    loss_part, grad_x, gw, gs, recv = _local_step(x, target, wt, small, rest_shards=rest_shards, overlap=True)
    loss = lax.psum(loss_part, ("x", "y", "c"))

    recv["ab_in_t"] = _all_to_all([gw["ab_in_t"]], name="exchange_last")[0]
    s8 = {k: _sum8(recv[k], name="sum8") for k in ["ab_in_t"] + REST}
    g_big = {"ab_w_in": s8["ab_in_t"].T[None], "ab_w_out": s8["ab_out"][None], "c_w_in": s8["c_in_t"].T[None],
             "c_w_out": s8["c_out"][None], "ffn_w_up": jnp.stack([s8["up_t0"].T, s8["up_t1"].T]),
             "ffn_w_down": jnp.stack([s8["down0"], s8["down1"]])}

    packed = _pack([gs[k] for k in SMALL_ORDER])
    gathered = _all_gather([packed], name="gather_small_grads")[0]
    tot = _sum8(gathered.reshape(N_DEV, packed.shape[0], 128), name="sum8_small").reshape(-1)
    gsm = dict(zip(SMALL_ORDER, _unpack(tot, [gs[k].shape for k in SMALL_ORDER])))
    grads = dict(g_big)
    grads["attn_norm_g"] = gsm["attn_norm_g"]
    grads["ffn_norm_g"] = gsm["ffn_norm_g"]
    grads["ab_rel_bias"] = gsm["rel_bias"][None]
    grads["c_ln_g"] = lax.dynamic_slice(gsm["ln_g"], (me * Lc,), (Lc,))[None]
    grads["c_ln_b"] = lax.dynamic_slice(gsm["ln_b"], (me * Lc,), (Lc,))[None]
    grads["c_w_s"] = gsm["w_s"][None]
    grads["c_b_s"] = gsm["b_s"][None]
    grads["ffn_conv_w"] = lax.dynamic_slice(gsm["conv_w"], (0, 0, me * Fc), (n_up, 3, Fc))
    grads["ffn_conv_b"] = gsm["conv_b"]
    grads["final_norm_g"] = gsm["final_g"]

    delta, new_m, new_v = {}, {}, {}
    for k in WEIGHTS:
        delta[k], new_m[k], new_v[k] = _adamw(P[k], grads[k], P["m_" + k], P["v_" + k], name="adamw")
    return (loss, grad_x[None], *[grads[k] for k in WEIGHTS], *[delta[k] for k in WEIGHTS],
            *[new_m[k] for k in WEIGHTS], *[new_v[k] for k in WEIGHTS])


def kernel(x, attn_norm_g, ffn_norm_g, ab_w_in, ab_w_out, ab_rel_bias, c_w_in, c_ln_g, c_ln_b, c_w_s, c_b_s, c_w_out, ffn_w_up, ffn_conv_w, ffn_conv_b, ffn_w_down, final_norm_g, loss_target, m_attn_norm_g, m_ffn_norm_g, m_ab_w_in, m_ab_w_out, m_ab_rel_bias, m_c_w_in, m_c_ln_g, m_c_ln_b, m_c_w_s, m_c_b_s, m_c_w_out, m_ffn_w_up, m_ffn_conv_w, m_ffn_conv_b, m_ffn_w_down, m_final_norm_g, v_attn_norm_g, v_ffn_norm_g, v_ab_w_in, v_ab_w_out, v_ab_rel_bias, v_c_w_in, v_c_ln_g, v_c_ln_b, v_c_w_s, v_c_b_s, v_c_w_out, v_ffn_w_up, v_ffn_conv_w, v_ffn_conv_b, v_ffn_w_down, v_final_norm_g):
    return _step(dict(locals()))
```

```python
import functools

import numpy as np
import jax
import jax.numpy as jnp
from jax import lax
from jax.experimental import pallas as pl
from jax.experimental.pallas import tpu as pltpu

F32 = jnp.float32
BF16 = jnp.bfloat16

D_MODEL = 1024
CHUNK = 64
EPS = 1e-6
NEG_INF = -1e30
RET_HEADS = 4
RET_QK_DIM = 128
RET_V_DIM = 256
ATT_HEADS = 8
ATT_HEAD_DIM = 64
ATT_PAST = 8
ATT_BAND = (ATT_PAST + 1) * CHUNK
MAX_REL = 128
N_REL = 2 * MAX_REL + 1
N_REL_PAD = 384
SGU_BLOCK = 128
SGU_GROUPS = 8
SGU_WIDTH = 2048
SGU_GW = SGU_WIDTH // SGU_GROUPS
FFN_HIDDEN = 2816
RET_W = 2 * RET_HEADS * RET_QK_DIM + 2 * RET_HEADS * RET_V_DIM
ATT_W = 3 * ATT_HEADS * ATT_HEAD_DIM
N_DEV = 8

ADAM_LR = 0.001
ADAM_B1 = 0.9
ADAM_B2 = 0.999
ADAM_EPS = 1e-08
ADAM_WD = 0.01
ADAM_STEP = 10

VMEM_LIMIT = 52 * 1024 * 1024


def _cp(*sem):
    return pltpu.CompilerParams(dimension_semantics=sem if sem else None, vmem_limit_bytes=VMEM_LIMIT)


def _tile(n, target):
    if n <= target:
        return n
    best = None
    for t in range(128, target + 1, 128):
        if n % t == 0:
            best = t
    assert best is not None, (n, target)
    return best


def _gelu(x):
    c = 0.7978845608028654
    return 0.5 * x * (1.0 + jnp.tanh(c * (x + 0.044715 * x * x * x)))


def _gelu_and_grad(x):
    c = 0.7978845608028654
    x2 = x * x
    t = jnp.tanh(c * (x + 0.044715 * x * x2))
    cdf = 0.5 * (1.0 + t)
    grad = cdf + x * (0.5 * c) * (1.0 - t * t) * (1.0 + 3.0 * 0.044715 * x2)
    return x * cdf, grad


def _dot(a, b, dims):
    return lax.dot_general(a, b, (dims, ((), ())), preferred_element_type=F32)


NN = ((1,), (0,))
NT = ((1,), (1,))
TN = ((0,), (0,))


def _mm(a, b, mode, out_dtype, res=None, name="mm", tm_t=None, tn_t=None, tk_t=None, comm=None):
    if mode == "nt":
        (M, K), N = a.shape, b.shape[0]
        dm, dn, dk = (512, 2816, K) if N <= 2816 else (1024, 1024, K)
    elif mode == "nn":
        (M, K), N = a.shape, b.shape[1]
        dm, dn, dk = (1024 if K <= 3072 else 512), 1024, K
    else:
        (K, M), N = a.shape, b.shape[1]
        dm, dn, dk = 1536, 1024, 2048
    tm, tn, tk = _tile(M, tm_t or dm), _tile(N, tn_t or dn), _tile(K, tk_t or dk)
    nk = K // tk
    dims = {"nt": NT, "nn": NN, "tn": TN}[mode]
    a_spec = pl.BlockSpec((tk, tm), lambda i, j, k: (k, i)) if mode == "tn" else pl.BlockSpec((tm, tk), lambda i, j, k: (i, k))
    b_spec = pl.BlockSpec((tn, tk), lambda i, j, k: (j, k)) if mode == "nt" else pl.BlockSpec((tk, tn), lambda i, j, k: (k, j))
    o_spec = pl.BlockSpec((tm, tn), lambda i, j, k: (i, j))
    has_res = res is not None
    nc = comm.n if comm else 0
    n_in = 3 if has_res else 2
    gi, gj = M // tm, N // tn

    def body(*refs):
        a_ref, b_ref = refs[:2]
        r_ref = refs[2] if has_res else None
        cin = refs[n_in:n_in + nc]
        o_ref = refs[n_in + nc]
        cout = refs[n_in + nc + 1:n_in + 2 * nc + 1]
        scratch = refs[n_in + 2 * nc + 1:]
        csem = scratch[1:] if nk > 1 else scratch
        step = (pl.program_id(0) * gj + pl.program_id(1)) * nk + pl.program_id(2)
        _host_comm(comm, "early", step, gi * gj * nk, cin, cout, csem)
        p = _dot(a_ref[...].astype(BF16), b_ref[...].astype(BF16), dims)
        if nk == 1:
            if has_res:
                p = p + r_ref[...]
            o_ref[...] = p.astype(out_dtype)
        else:
            acc = scratch[0]
            k = pl.program_id(2)

            @pl.when(k == 0)
            def _():
                acc[...] = p

            @pl.when(k > 0)
            def _():
                acc[...] += p

            @pl.when(k == nk - 1)
            def _():
                t = acc[...]
                if has_res:
                    t = t + r_ref[...]
                o_ref[...] = t.astype(out_dtype)
        _host_comm(comm, "late", step, gi * gj * nk, cin, cout, csem)

    in_specs = [a_spec, b_spec] + ([o_spec] if has_res else []) + [ANY] * nc
    args = (a, b) + ((res,) if has_res else ()) + tuple(comm.arrays if comm else ())
    out = pl.pallas_call(
        body,
        grid=(gi, gj, nk),
        in_specs=in_specs,
        out_specs=[o_spec] + [ANY] * nc,
        out_shape=[jax.ShapeDtypeStruct((M, N), out_dtype)] + (comm.out_shape if comm else []),
        scratch_shapes=([pltpu.VMEM((tm, tn), F32)] if nk > 1 else []) + (comm.scratch if comm else []),
        compiler_params=_cp("arbitrary", "arbitrary", "arbitrary") if comm else _cp("parallel", "parallel", "arbitrary"),
        name=name,
    )(*args)
    return (out[0], list(out[1:])) if comm else out[0]


def _mm_rows(a, b, res, name, norm_g=None, bwd=None, tm=512):
    M, K = a.shape
    Dm = b.shape[1]
    tm = min(tm, M)
    row = pl.BlockSpec((tm, Dm), lambda i: (i, 0))
    vec = pl.BlockSpec((1, Dm), lambda i: (0, 0))
    a_spec = pl.BlockSpec((tm, K), lambda i: (i, 0))
    b_spec = pl.BlockSpec((K, Dm), lambda i: (0, 0))

    if bwd is None:
        def body(a_ref, b_ref, r_ref, g_ref, o_ref, n_ref):
            t = _dot(a_ref[...].astype(BF16), b_ref[...].astype(BF16), NN) + r_ref[...]
            o_ref[...] = t
            r = lax.rsqrt(jnp.mean(t * t, axis=-1, keepdims=True) + EPS)
            n_ref[...] = (t * r * g_ref[...]).astype(n_ref.dtype)

        return pl.pallas_call(
            body, grid=(M // tm,), in_specs=[a_spec, b_spec, row, vec], out_specs=[row, row],
            out_shape=[jax.ShapeDtypeStruct((M, Dm), F32), jax.ShapeDtypeStruct((M, Dm), BF16)],
            compiler_params=_cp("parallel"), name=name,
        )(a, b, res, norm_g.reshape(1, Dm))

    h, g, dres = bwd
    has_res = res is not None

    def body(*refs):
        a_ref, b_ref = refs[:2]
        h_ref, g_ref, dres_ref, dh_ref, dg_ref = refs[-5:]

        @pl.when(pl.program_id(0) == 0)
        def _():
            dg_ref[...] = jnp.zeros_like(dg_ref)

        d = _dot(a_ref[...].astype(BF16), b_ref[...].astype(BF16), NN)
        if has_res:
            d = d + refs[2][...]
        x = h_ref[...]
        r = lax.rsqrt(jnp.mean(x * x, axis=-1, keepdims=True) + EPS)
        xhat = x * r
        dg_ref[...] += jnp.sum(d * xhat, axis=0, keepdims=True)
        dx = d * g_ref[...]
        m = jnp.mean(dx * xhat, axis=-1, keepdims=True)
        dh_ref[...] = dres_ref[...] + r * (dx - xhat * m)

    return pl.pallas_call(
        body, grid=(M // tm,), in_specs=[a_spec, b_spec] + ([row] if has_res else []) + [row, vec, row], out_specs=[row, vec],
        out_shape=[jax.ShapeDtypeStruct((M, Dm), F32), jax.ShapeDtypeStruct((1, Dm), F32)],
        compiler_params=_cp("arbitrary"), name=name,
    )(a, b, *((res,) if has_res else ()), h, g.reshape(1, Dm), dres)


def _rms_fwd(h, g, name="rms_fwd", tm=512):
    T, Dm = h.shape
    tm = min(tm, T)

    def body(h_ref, g_ref, o_ref):
        x = h_ref[...]
        r = lax.rsqrt(jnp.mean(x * x, axis=-1, keepdims=True) + EPS)
        o_ref[...] = (x * r * g_ref[...]).astype(o_ref.dtype)

    return pl.pallas_call(
        body,
        grid=(T // tm,),
        in_specs=[pl.BlockSpec((tm, Dm), lambda i: (i, 0)), pl.BlockSpec((1, Dm), lambda i: (0, 0))],
        out_specs=pl.BlockSpec((tm, Dm), lambda i: (i, 0)),
        out_shape=jax.ShapeDtypeStruct((T, Dm), BF16),
        compiler_params=_cp("parallel"),
        name=name,
    )(h, g.reshape(1, Dm))


def _final_loss(h, g, target, name="final_loss", tm=512):
    T, Dm = h.shape
    tm = min(tm, T)

    def body(h_ref, g_ref, t_ref, loss_ref, dh_ref, dg_ref):
        i = pl.program_id(0)
        x = h_ref[...]
        gv = g_ref[...]
        r = lax.rsqrt(jnp.mean(x * x, axis=-1, keepdims=True) + EPS)
        xhat = x * r
        e = xhat * gv - t_ref[...]

        @pl.when(i == 0)
        def _():
            loss_ref[...] = jnp.zeros_like(loss_ref)
            dg_ref[...] = jnp.zeros_like(dg_ref)

        loss_ref[...] += jnp.full((1, 128), 0.5 / Dm, F32) * jnp.sum(e * e)
        dy = e * (1.0 / Dm)
        dg_ref[...] += jnp.sum(dy * xhat, axis=0, keepdims=True)
        dx = dy * gv
        m = jnp.mean(dx * xhat, axis=-1, keepdims=True)
        dh_ref[...] = r * (dx - xhat * m)

    row = pl.BlockSpec((tm, Dm), lambda i: (i, 0))
    vec = pl.BlockSpec((1, Dm), lambda i: (0, 0))
    return pl.pallas_call(
        body,
        grid=(T // tm,),
        in_specs=[row, vec, row],
        out_specs=[pl.BlockSpec((1, 128), lambda i: (0, 0)), row, vec],
        out_shape=[jax.ShapeDtypeStruct((1, 128), F32), jax.ShapeDtypeStruct((T, Dm), F32), jax.ShapeDtypeStruct((1, Dm), F32)],
        compiler_params=_cp("arbitrary"),
        name=name,
    )(h, g.reshape(1, Dm), target)


HALO = 16


def _conv3(ext, w_ref, b_ref):
    return w_ref[0:1, :] * pltpu.roll(ext, 2, 0) + w_ref[1:2, :] * pltpu.roll(ext, 1, 0) + w_ref[2:3, :] * ext + b_ref[...]


def _ffn_mid_fwd(zg, zu, cw, cb, name="ffn_mid_fwd", tm=256, tc=1408):
    T, Fh = zg.shape
    tm = min(tm, T)
    nj = Fh // tc
    hb = tm // HALO

    def body(zg_ref, zu_ref, zgp_ref, zup_ref, wg_ref, wu_ref, bg_ref, bu_ref, a_ref):
        i = pl.program_id(1)
        first = i == 0

        def conv(z_ref, zp_ref, w_ref, b_ref):
            prev = jnp.where(first, 0.0, zp_ref[...].astype(F32))
            ext = jnp.concatenate([prev, z_ref[...].astype(F32)], axis=0)
            return _conv3(ext, w_ref, b_ref)[HALO:]

        cg = conv(zg_ref, zgp_ref, wg_ref, bg_ref)
        cu = conv(zu_ref, zup_ref, wu_ref, bu_ref)
        a_ref[...] = (_gelu(cg) * cu).astype(a_ref.dtype)

    cur = pl.BlockSpec((tm, tc), lambda j, i: (i, j))
    prev = pl.BlockSpec((HALO, tc), lambda j, i: (jnp.maximum(i * hb - 1, 0), j))
    wg = pl.BlockSpec((3, tc), lambda j, i: (0, j))
    wu = pl.BlockSpec((3, tc), lambda j, i: (0, j + nj))
    bg = pl.BlockSpec((1, tc), lambda j, i: (0, j))
    bu = pl.BlockSpec((1, tc), lambda j, i: (0, j + nj))
    return pl.pallas_call(
        body,
        grid=(nj, T // tm),
        in_specs=[cur, cur, prev, prev, wg, wu, bg, bu],
        out_specs=cur,
        out_shape=jax.ShapeDtypeStruct((T, Fh), BF16),
        compiler_params=_cp("parallel", "parallel"),
        name=name,
    )(zg, zu, zg, zu, cw, cw, cb, cb)


def _ffn_mid_bwd(zg, zu, da, cw, cb, comm=None, name="ffn_mid_bwd", tm=512, tc=256):
    T, Fh = zg.shape
    tm = min(tm, T)
    nj = Fh // tc
    hb = tm // HALO
    nhb = T // HALO

    nc = comm.n if comm else 0
    ni = T // tm

    def body(*refs):
        zg_ref, zu_ref, zgp_ref, zup_ref, zgn_ref, zun_ref, da_ref, dan_ref, wg_ref, wu_ref, bg_ref, bu_ref = refs[:12]
        cin = refs[12:12 + nc]
        dzg_ref, dzu_ref, dwg_ref, dwu_ref, dbg_ref, dbu_ref = refs[12 + nc:18 + nc]
        cout = refs[18 + nc:18 + 2 * nc]
        csem = refs[18 + 2 * nc:]
        i = pl.program_id(1)
        step = pl.program_id(0) * ni + i
        _host_comm(comm, "early", step, nj * ni, cin, cout, csem)
        first = i == 0
        last = i == ni - 1

        @pl.when(first)
        def _():
            dwg_ref[...] = jnp.zeros_like(dwg_ref)
            dwu_ref[...] = jnp.zeros_like(dwu_ref)
            dbg_ref[...] = jnp.zeros_like(dbg_ref)
            dbu_ref[...] = jnp.zeros_like(dbu_ref)

        def ext_of(p_ref, c_ref, n_ref):
            p = jnp.where(first, 0.0, p_ref[...].astype(F32))
            return jnp.concatenate([p, c_ref[...].astype(F32), n_ref[...].astype(F32)], axis=0)

        zge = ext_of(zgp_ref, zg_ref, zgn_ref)
        zue = ext_of(zup_ref, zu_ref, zun_ref)
        dan = jnp.where(last, 0.0, dan_ref[...].astype(F32))
        dae = jnp.concatenate([jnp.zeros((HALO, tc), F32), da_ref[...].astype(F32), dan], axis=0)
        zg1, zg2 = pltpu.roll(zge, 1, 0), pltpu.roll(zge, 2, 0)
        zu1, zu2 = pltpu.roll(zue, 1, 0), pltpu.roll(zue, 2, 0)
        cg = wg_ref[0:1, :] * zg2 + wg_ref[1:2, :] * zg1 + wg_ref[2:3, :] * zge + bg_ref[...]
        cu = wu_ref[0:1, :] * zu2 + wu_ref[1:2, :] * zu1 + wu_ref[2:3, :] * zue + bu_ref[...]
        gel, dgel = _gelu_and_grad(cg)
        dcg = dae * cu * dgel
        dcu = dae * gel
        lo, hi = HALO, HALO + tm

        def back(dc, taps, w_ref, dz_ref, dw_ref, db_ref):
            n = dc.shape[0]
            dz = w_ref[2:3, :] * dc + w_ref[1:2, :] * pltpu.roll(dc, n - 1, 0) + w_ref[0:1, :] * pltpu.roll(dc, n - 2, 0)
            dz_ref[...] = dz[lo:hi].astype(dz_ref.dtype)
            dcc = dc[lo:hi]
            db_ref[...] += jnp.sum(dcc, axis=0, keepdims=True)
            for k, tap in enumerate(taps):
                dw_ref[k:k + 1, :] += jnp.sum(dcc * tap[lo:hi], axis=0, keepdims=True)

        back(dcg, (zg2, zg1, zge), wg_ref, dzg_ref, dwg_ref, dbg_ref)
        back(dcu, (zu2, zu1, zue), wu_ref, dzu_ref, dwu_ref, dbu_ref)
        _host_comm(comm, "late", step, nj * ni, cin, cout, csem)

    cur = pl.BlockSpec((tm, tc), lambda j, i: (i, j))
    prev = pl.BlockSpec((HALO, tc), lambda j, i: (jnp.maximum(i * hb - 1, 0), j))
    nxt = pl.BlockSpec((HALO, tc), lambda j, i: (jnp.minimum((i + 1) * hb, nhb - 1), j))
    wg = pl.BlockSpec((3, tc), lambda j, i: (0, j))
    wu = pl.BlockSpec((3, tc), lambda j, i: (0, j + nj))
    bg = pl.BlockSpec((1, tc), lambda j, i: (0, j))
    bu = pl.BlockSpec((1, tc), lambda j, i: (0, j + nj))
    dw = pl.BlockSpec((3, tc), lambda j, i: (0, j))
    db = pl.BlockSpec((1, tc), lambda j, i: (0, j))
    out = pl.pallas_call(
        body,
        grid=(nj, ni),
        in_specs=[cur, cur, prev, prev, nxt, nxt, cur, nxt, wg, wu, bg, bu] + [ANY] * nc,
        out_specs=[cur, cur, dw, dw, db, db] + [ANY] * nc,
        out_shape=[jax.ShapeDtypeStruct((T, Fh), BF16), jax.ShapeDtypeStruct((T, Fh), BF16),
                   jax.ShapeDtypeStruct((3, Fh), F32), jax.ShapeDtypeStruct((3, Fh), F32),
                   jax.ShapeDtypeStruct((1, Fh), F32), jax.ShapeDtypeStruct((1, Fh), F32)] + (comm.out_shape if comm else []),
        scratch_shapes=comm.scratch if comm else [],
        compiler_params=_cp("arbitrary", "arbitrary"),
        name=name,
    )(zg, zu, zg, zu, zg, zu, da, da, cw, cw, cb, cb, *(comm.arrays if comm else []))
    dzg, dzu, dwg, dwu, dbg, dbu = out[:6]
    return dzg, dzu, jnp.concatenate([dwg, dwu], axis=1), jnp.concatenate([dbg, dbu], axis=1), list(out[6:])


def _sgu_mask():
    r = lax.broadcasted_iota(jnp.int32, (SGU_BLOCK, SGU_BLOCK), 0)
    c = lax.broadcasted_iota(jnp.int32, (SGU_BLOCK, SGU_BLOCK), 1)
    return (c < CHUNK) | (r >= CHUNK)


def _sgu_fwd(zz, ln_g, ln_b, w_s, b_s, name="sgu_fwd", tm=256):
    T = zz.shape[0]
    tm = min(tm, T)
    W = SGU_WIDTH

    def body(zu_ref, zv_ref, g_ref, b_ref, ws_ref, bs_ref, y_ref):
        u = _gelu(zu_ref[...].astype(F32))
        v = _gelu(zv_ref[...].astype(F32))
        mu = jnp.mean(v, axis=-1, keepdims=True)
        xc = v - mu
        rstd = lax.rsqrt(jnp.mean(xc * xc, axis=-1, keepdims=True) + EPS)
        vn = (xc * rstd * g_ref[...] + b_ref[...]).astype(BF16)
        mask = _sgu_mask()
        for g in range(SGU_GROUPS):
            wm = jnp.where(mask, ws_ref[g], 0.0).astype(BF16)
            cs = slice(g * SGU_GW, (g + 1) * SGU_GW)
            for blk in range(tm // SGU_BLOCK):
                rs = slice(blk * SGU_BLOCK, (blk + 1) * SGU_BLOCK)
                mixed = _dot(wm, vn[rs, cs], NN) + bs_ref[g]
                y_ref[rs, cs] = (u[rs, cs] * mixed).astype(y_ref.dtype)

    return pl.pallas_call(
        body,
        grid=(T // tm,),
        in_specs=[pl.BlockSpec((tm, W), lambda i: (i, 0)), pl.BlockSpec((tm, W), lambda i: (i, 1)),
                  pl.BlockSpec((1, W), lambda i: (0, 0)), pl.BlockSpec((1, W), lambda i: (0, 0)),
                  pl.BlockSpec((SGU_GROUPS, SGU_BLOCK, SGU_BLOCK), lambda i: (0, 0, 0)),
                  pl.BlockSpec((SGU_GROUPS, SGU_BLOCK, 1), lambda i: (0, 0, 0))],
        out_specs=pl.BlockSpec((tm, W), lambda i: (i, 0)),
        out_shape=jax.ShapeDtypeStruct((T, W), BF16),
        compiler_params=_cp("parallel"),
        name=name,
    )(zz, zz, ln_g.reshape(1, W), ln_b.reshape(1, W), w_s, b_s.reshape(SGU_GROUPS, SGU_BLOCK, 1))


def _sgu_bwd(zz, dy, ln_g, ln_b, w_s, b_s, name="sgu_bwd", tm=256):
    T = zz.shape[0]
    tm = min(tm, T)
    W = SGU_WIDTH

    def body(zu_ref, zv_ref, dy_ref, g_ref, b_ref, ws_ref, bs_ref, dzz_ref, dws_ref, dbs_ref, dg_ref, db_ref, dvn_ref):
        i = pl.program_id(0)

        @pl.when(i == 0)
        def _():
            dws_ref[...] = jnp.zeros_like(dws_ref)
            dbs_ref[...] = jnp.zeros_like(dbs_ref)
            dg_ref[...] = jnp.zeros_like(dg_ref)
            db_ref[...] = jnp.zeros_like(db_ref)

        u, du_dz = _gelu_and_grad(zu_ref[...].astype(F32))
        v, dv_dz = _gelu_and_grad(zv_ref[...].astype(F32))
        mu = jnp.mean(v, axis=-1, keepdims=True)
        xc = v - mu
        rstd = lax.rsqrt(jnp.mean(xc * xc, axis=-1, keepdims=True) + EPS)
        xhat = xc * rstd
        gv = g_ref[...]
        vn = (xhat * gv + b_ref[...]).astype(BF16)
        dyv = dy_ref[...].astype(F32)
        mask = _sgu_mask()
        for g in range(SGU_GROUPS):
            wm = jnp.where(mask, ws_ref[g], 0.0).astype(BF16)
            cs = slice(g * SGU_GW, (g + 1) * SGU_GW)
            dw_acc = jnp.zeros((SGU_BLOCK, SGU_BLOCK), F32)
            db_acc = jnp.zeros((SGU_BLOCK, 1), F32)
            for blk in range(tm // SGU_BLOCK):
                rs = slice(blk * SGU_BLOCK, (blk + 1) * SGU_BLOCK)
                vn_bg = vn[rs, cs]
                mixed = _dot(wm, vn_bg, NN) + bs_ref[g]
                dy_bg = dyv[rs, cs]
                dmixed = dy_bg * u[rs, cs]
                dmb = dmixed.astype(BF16)
                dw_acc += _dot(dmb, vn_bg, NT)
                db_acc += jnp.sum(dmixed, axis=1, keepdims=True)
                dvn_ref[rs, cs] = _dot(wm, dmb, TN)
                dzz_ref[rs, cs] = (dy_bg * mixed * du_dz[rs, cs]).astype(dzz_ref.dtype)
            dws_ref[g] += jnp.where(mask, dw_acc, 0.0)
            dbs_ref[g] += db_acc
        dvn = dvn_ref[...]
        dg_ref[...] += jnp.sum(dvn * xhat, axis=0, keepdims=True)
        db_ref[...] += jnp.sum(dvn, axis=0, keepdims=True)
        dxh = dvn * gv
        m1 = jnp.mean(dxh, axis=-1, keepdims=True)
        m2 = jnp.mean(dxh * xhat, axis=-1, keepdims=True)
        dv = rstd * (dxh - m1 - xhat * m2)
        dzz_ref[:, W:] = (dv * dv_dz).astype(dzz_ref.dtype)

    vec = pl.BlockSpec((1, W), lambda i: (0, 0))
    ws_spec = pl.BlockSpec((SGU_GROUPS, SGU_BLOCK, SGU_BLOCK), lambda i: (0, 0, 0))
    bs_spec = pl.BlockSpec((SGU_GROUPS, SGU_BLOCK, 1), lambda i: (0, 0, 0))
    return pl.pallas_call(
        body,
        grid=(T // tm,),
        in_specs=[pl.BlockSpec((tm, W), lambda i: (i, 0)), pl.BlockSpec((tm, W), lambda i: (i, 1)),
                  pl.BlockSpec((tm, W), lambda i: (i, 0)), vec, vec, ws_spec, bs_spec],
        out_specs=[pl.BlockSpec((tm, 2 * W), lambda i: (i, 0)), ws_spec, bs_spec, vec, vec],
        out_shape=[jax.ShapeDtypeStruct((T, 2 * W), BF16),
                   jax.ShapeDtypeStruct((SGU_GROUPS, SGU_BLOCK, SGU_BLOCK), F32),
                   jax.ShapeDtypeStruct((SGU_GROUPS, SGU_BLOCK, 1), F32),
                   jax.ShapeDtypeStruct((1, W), F32), jax.ShapeDtypeStruct((1, W), F32)],
        scratch_shapes=[pltpu.VMEM((tm, W), F32)],
        compiler_params=_cp("arbitrary"),
        name=name,
    )(zz, zz, dy, ln_g.reshape(1, W), ln_b.reshape(1, W), w_s, b_s.reshape(SGU_GROUPS, SGU_BLOCK, 1))


RET_TR = 256
RET_BLK = 256
QK_SCALE = RET_QK_DIM ** -0.5


def _ret_tables(T):
    half = RET_QK_DIM // 2
    inv = 1.0 / (10000.0 ** jnp.linspace(0.0, 1.0, half, dtype=F32))
    ang = jnp.arange(T).astype(F32)[:, None] * inv[None, :]
    cos, sin = jnp.cos(ang), jnp.sin(ang)
    rot_c = jnp.concatenate([cos, cos], axis=1)
    rot_s = jnp.concatenate([-sin, sin], axis=1)
    log_g = jnp.log1p(-jnp.exp2(-5.0 - jnp.arange(RET_HEADS, dtype=F32)))
    idx = jnp.arange(RET_BLK, dtype=F32)
    dist = idx[:, None] - idx[None, :]
    cq, ck = jnp.arange(RET_BLK)[:, None] // CHUNK, jnp.arange(RET_BLK)[None, :] // CHUNK
    expo = jnp.where(ck == cq, jnp.abs(dist), dist)
    d_blk = jnp.where((ck <= cq)[None], jnp.exp(log_g[:, None, None] * expo[None]), 0.0)
    k_dec = jnp.exp(log_g[:, None] * (RET_BLK - 1 - idx)[None, :])[:, :, None]
    q_dec = jnp.exp(log_g[:, None] * (idx + 1.0)[None, :])[:, :, None]
    c_dec = jnp.exp(log_g * RET_BLK)[:, None, None]
    return rot_c, rot_s, d_blk, q_dec, k_dec, c_dec


def _rot(x, c, s):
    return x * c + pltpu.roll(x, RET_QK_DIM // 2, 1) * s


def _ret_specs(tr, rev, nb):
    ix = (lambda n: nb - 1 - n) if rev else (lambda n: n)
    tab = pl.BlockSpec((tr, RET_QK_DIM), lambda n: (ix(n), 0))
    dm = pl.BlockSpec((RET_HEADS, RET_BLK, RET_BLK), lambda n: (0, 0, 0))
    dv = pl.BlockSpec((RET_HEADS, RET_BLK, 1), lambda n: (0, 0, 0))
    dc = pl.BlockSpec((RET_HEADS, 1, 1), lambda n: (0, 0, 0))
    return ix, [tab, tab, dm, dv, dv, dc]


def _ret_fwd(z_a, tables, name="ret_fwd"):
    T = z_a.shape[0]
    tr = min(RET_TR, T)
    cpb = tr // RET_BLK
    nb = T // tr
    QW, VW = RET_HEADS * RET_QK_DIM, RET_HEADS * RET_V_DIM
    ix, tab_specs = _ret_specs(tr, False, nb)

    def body(z_ref, c_ref, s_ref, dm_ref, qd_ref, kd_ref, cd_ref, y_ref, st_ref, state):
        @pl.when(pl.program_id(0) == 0)
        def _():
            state[...] = jnp.zeros_like(state)

        for c in range(cpb):
            for h in range(RET_HEADS):
                rs = slice(c * RET_BLK, (c + 1) * RET_BLK)
                cc, ss = c_ref[rs, :], s_ref[rs, :]
                q = z_ref[rs, h * RET_QK_DIM:(h + 1) * RET_QK_DIM].astype(F32)
                k = z_ref[rs, QW + h * RET_QK_DIM:QW + (h + 1) * RET_QK_DIM].astype(F32)
                v = z_ref[rs, 2 * QW + h * RET_V_DIM:2 * QW + (h + 1) * RET_V_DIM]
                gt = z_ref[rs, 2 * QW + VW + h * RET_V_DIM:2 * QW + VW + (h + 1) * RET_V_DIM].astype(F32)
                qr = _rot(q, cc, ss)
                kr = _rot(k, cc, ss) * QK_SCALE
                s_old = state[h]
                sb = s_old.astype(BF16)
                st_ref[c, h] = sb
                s = _dot(qr.astype(BF16), kr.astype(BF16), NT) * dm_ref[h]
                o = _dot(s.astype(BF16), v, NN) + _dot((qr * qd_ref[h]).astype(BF16), sb, NN)
                state[h] = s_old * cd_ref[h] + _dot((kr * kd_ref[h]).astype(BF16), v, TN)
                mu = jnp.mean(o, axis=-1, keepdims=True)
                oc = o - mu
                rn = oc * lax.rsqrt(jnp.mean(oc * oc, axis=-1, keepdims=True) + EPS)
                silu = gt / (1.0 + jnp.exp(-gt))
                y_ref[rs, h * RET_V_DIM:(h + 1) * RET_V_DIM] = (silu * rn).astype(y_ref.dtype)

    return pl.pallas_call(
        body,
        grid=(nb,),
        in_specs=[pl.BlockSpec((tr, RET_W), lambda n: (n, 0))] + tab_specs,
        out_specs=[pl.BlockSpec((tr, VW), lambda n: (n, 0)),
                   pl.BlockSpec((cpb, RET_HEADS, RET_QK_DIM, RET_V_DIM), lambda n: (n, 0, 0, 0))],
        out_shape=[jax.ShapeDtypeStruct((T, Y_COLS), BF16),
                   jax.ShapeDtypeStruct((T // RET_BLK, RET_HEADS, RET_QK_DIM, RET_V_DIM), BF16)],
        scratch_shapes=[pltpu.VMEM((RET_HEADS, RET_QK_DIM, RET_V_DIM), F32)],
        compiler_params=_cp("arbitrary"),
        name=name,
    )(z_a, *tables)


def _ret_bwd(z_a, dy, states, tables, name="ret_bwd"):
    T = z_a.shape[0]
    tr = min(RET_TR, T)
    cpb = tr // RET_BLK
    nb = T // tr
    QW, VW = RET_HEADS * RET_QK_DIM, RET_HEADS * RET_V_DIM
    ix, tab_specs = _ret_specs(tr, True, nb)

    def body(z_ref, dy_ref, st_ref, c_ref, s_ref, dm_ref, qd_ref, kd_ref, cd_ref, dz_ref, dstate):
        @pl.when(pl.program_id(0) == 0)
        def _():
            dstate[...] = jnp.zeros_like(dstate)

        for c in reversed(range(cpb)):
            for h in range(RET_HEADS):
                rs = slice(c * RET_BLK, (c + 1) * RET_BLK)
                cc, ss = c_ref[rs, :], s_ref[rs, :]
                q = z_ref[rs, h * RET_QK_DIM:(h + 1) * RET_QK_DIM].astype(F32)
                k = z_ref[rs, QW + h * RET_QK_DIM:QW + (h + 1) * RET_QK_DIM].astype(F32)
                v = z_ref[rs, 2 * QW + h * RET_V_DIM:2 * QW + (h + 1) * RET_V_DIM]
                gt = z_ref[rs, 2 * QW + VW + h * RET_V_DIM:2 * QW + VW + (h + 1) * RET_V_DIM].astype(F32)
                dyv = dy_ref[rs, h * RET_V_DIM:(h + 1) * RET_V_DIM].astype(F32)
                dmat, qd, kd = dm_ref[h], qd_ref[h], kd_ref[h]
                qr = _rot(q, cc, ss)
                kr = _rot(k, cc, ss) * QK_SCALE
                qrb, krb = qr.astype(BF16), kr.astype(BF16)
                sb = st_ref[c, h]
                sd = (_dot(qrb, krb, NT) * dmat).astype(BF16)
                qdb = (qr * qd).astype(BF16)
                kdb = (kr * kd).astype(BF16)
                o = _dot(sd, v, NN) + _dot(qdb, sb, NN)
                mu = jnp.mean(o, axis=-1, keepdims=True)
                oc = o - mu
                rstd = lax.rsqrt(jnp.mean(oc * oc, axis=-1, keepdims=True) + EPS)
                rn = oc * rstd
                sg = 1.0 / (1.0 + jnp.exp(-gt))
                dgt = dyv * rn * (sg * (1.0 + gt * (1.0 - sg)))
                drn = dyv * (gt * sg)
                do = rstd * (drn - jnp.mean(drn, axis=-1, keepdims=True) - rn * jnp.mean(drn * rn, axis=-1, keepdims=True))
                dob = do.astype(BF16)
                dsn = dstate[h]
                dsnb = dsn.astype(BF16)
                ds_raw = (_dot(dob, v, NT) * dmat).astype(BF16)
                dv = _dot(sd, dob, TN) + _dot(kdb, dsnb, NN)
                dqr = _dot(ds_raw, krb, NN) + qd * _dot(dob, sb, NT)
                dkr = (_dot(ds_raw, qrb, TN) + kd * _dot(v, dsnb, NT)) * QK_SCALE
                dstate[h] = dsn * cd_ref[h] + _dot(qdb, dob, TN)
                dq = dqr * cc + pltpu.roll(dqr * ss, RET_QK_DIM // 2, 1)
                dk = dkr * cc + pltpu.roll(dkr * ss, RET_QK_DIM // 2, 1)
                dz_ref[rs, h * RET_QK_DIM:(h + 1) * RET_QK_DIM] = dq.astype(dz_ref.dtype)
                dz_ref[rs, QW + h * RET_QK_DIM:QW + (h + 1) * RET_QK_DIM] = dk.astype(dz_ref.dtype)
                dz_ref[rs, 2 * QW + h * RET_V_DIM:2 * QW + (h + 1) * RET_V_DIM] = dv.astype(dz_ref.dtype)
                dz_ref[rs, 2 * QW + VW + h * RET_V_DIM:2 * QW + VW + (h + 1) * RET_V_DIM] = dgt.astype(dz_ref.dtype)

    return pl.pallas_call(
        body,
        grid=(nb,),
        in_specs=[pl.BlockSpec((tr, RET_W), lambda n: (ix(n), 0)),
                  pl.BlockSpec((tr, VW), lambda n: (ix(n), 0)),
                  pl.BlockSpec((cpb, RET_HEADS, RET_QK_DIM, RET_V_DIM), lambda n: (ix(n), 0, 0, 0))] + tab_specs,
        out_specs=pl.BlockSpec((tr, RET_W), lambda n: (ix(n), 0)),
        out_shape=jax.ShapeDtypeStruct((T, RET_W), BF16),
        scratch_shapes=[pltpu.VMEM((RET_HEADS, RET_QK_DIM, RET_V_DIM), F32)],
        compiler_params=_cp("arbitrary"),
        name=name,
    )(z_a, dy, states, *tables)


ATT_TQ = 256
ATT_CPB = ATT_TQ // CHUNK
ATT_SCALE = ATT_HEAD_DIM ** -0.5


ATT_WIN = 3 * ATT_TQ
ATT_NB = CHUNK * ATT_BAND


def _rel_index():
    i = np.arange(CHUNK)[:, None]
    j = np.arange(ATT_BAND)[None, :]
    rel = np.clip(i + ATT_PAST * CHUNK - j, -MAX_REL, MAX_REL) + MAX_REL
    return jnp.asarray(rel.reshape(1, ATT_NB).astype(np.int32))


def _split3(x):
    hi = x.astype(BF16)
    r1 = x - hi.astype(F32)
    mid = r1.astype(BF16)
    lo = (r1 - mid.astype(F32)).astype(BF16)
    return hi, mid, lo


REL_TILE = 4608


def _bias_expand(rel_bias, name="bias_expand"):
    H = rel_bias.shape[0]
    n = ATT_NB
    padded = jnp.pad(rel_bias, ((0, 0), (0, N_REL_PAD - N_REL)))

    def body(rb_ref, idx_ref, o_ref):
        onehot = (lax.broadcasted_iota(jnp.int32, (N_REL_PAD, REL_TILE), 0) == idx_ref[...]).astype(BF16)
        hi, mid, lo = _split3(rb_ref[...])
        o_ref[...] = _dot(hi, onehot, NN) + _dot(mid, onehot, NN) + _dot(lo, onehot, NN)

    out = pl.pallas_call(
        body,
        grid=(n // REL_TILE,),
        in_specs=[pl.BlockSpec((H, N_REL_PAD), lambda t: (0, 0)), pl.BlockSpec((1, REL_TILE), lambda t: (0, t))],
        out_specs=pl.BlockSpec((H, REL_TILE), lambda t: (0, t)),
        out_shape=jax.ShapeDtypeStruct((H, n), F32),
        compiler_params=_cp("parallel"),
        name=name,
    )(padded, _rel_index())
    return out.reshape(H, CHUNK, ATT_BAND)


def _bias_tile(band, name="bias_tile"):
    H = band.shape[0]
    padded = jnp.pad(band, ((0, 0), (0, 0), (0, ATT_WIN - ATT_BAND)), constant_values=NEG_INF)

    def body(b_ref, o_ref):
        b = b_ref[...]
        col = lax.broadcasted_iota(jnp.int32, (CHUNK, ATT_WIN), 1)
        keep = col >= (2 - pl.program_id(0)) * ATT_TQ
        for a in range(ATT_CPB):
            o_ref[a * CHUNK:(a + 1) * CHUNK, :] = jnp.where(keep, pltpu.roll(b, a * CHUNK, 1) if a else b, NEG_INF)

    return pl.pallas_call(
        body,
        grid=(3, H),
        in_specs=[pl.BlockSpec((None, CHUNK, ATT_WIN), lambda v, h: (h, 0, 0))],
        out_specs=pl.BlockSpec((None, None, ATT_TQ, ATT_WIN), lambda v, h: (v, h, 0, 0)),
        out_shape=jax.ShapeDtypeStruct((3, H, ATT_TQ, ATT_WIN), F32),
        compiler_params=_cp("parallel", "parallel"),
        name=name,
    )(padded)


def _bias_untile(dtile, name="bias_untile"):
    H = dtile.shape[0]

    def body(d_ref, o_ref):
        acc = d_ref[0:CHUNK, :]
        for a in range(1, ATT_CPB):
            acc = acc + pltpu.roll(d_ref[a * CHUNK:(a + 1) * CHUNK, :], ATT_WIN - a * CHUNK, 1)
        o_ref[...] = acc

    out = pl.pallas_call(
        body,
        grid=(H,),
        in_specs=[pl.BlockSpec((None, ATT_TQ, ATT_WIN), lambda h: (h, 0, 0))],
        out_specs=pl.BlockSpec((None, CHUNK, ATT_WIN), lambda h: (h, 0, 0)),
        out_shape=jax.ShapeDtypeStruct((H, CHUNK, ATT_WIN), F32),
        compiler_params=_cp("parallel"),
        name=name,
    )(dtile)
    return out[:, :, :ATT_BAND]


def _bias_reduce(dbias, name="bias_reduce"):
    H = dbias.shape[0]
    n = ATT_NB

    def body(db_ref, idx_ref, o_ref):
        @pl.when(pl.program_id(0) == 0)
        def _():
            o_ref[...] = jnp.zeros_like(o_ref)

        onehot = (lax.broadcasted_iota(jnp.int32, (N_REL_PAD, REL_TILE), 0) == idx_ref[...]).astype(BF16)
        hi, mid, lo = _split3(db_ref[...])
        o_ref[...] += _dot(hi, onehot, NT) + _dot(mid, onehot, NT) + _dot(lo, onehot, NT)

    out = pl.pallas_call(
        body,
        grid=(n // REL_TILE,),
        in_specs=[pl.BlockSpec((H, REL_TILE), lambda t: (0, t)), pl.BlockSpec((1, REL_TILE), lambda t: (0, t))],
        out_specs=pl.BlockSpec((H, N_REL_PAD), lambda t: (0, 0)),
        out_shape=jax.ShapeDtypeStruct((H, N_REL_PAD), F32),
        compiler_params=_cp("arbitrary"),
        name=name,
    )(dbias.reshape(H, n), _rel_index())
    return out[:, :N_REL]


def _att_probs(q, kwin, bias):
    s = _dot(q, kwin, NT) + bias
    e = jnp.exp(s - jnp.max(s, axis=-1, keepdims=True))
    return e * (1.0 / jnp.sum(e, axis=-1, keepdims=True))


ATT_PAIR = 2 * ATT_HEAD_DIM
ATT_NP = ATT_HEADS // 2
ATT_QW = ATT_HEADS * ATT_HEAD_DIM
Y_COLS = RET_HEADS * RET_V_DIM + ATT_QW


def _att_specs(tq, nq, clip_q, q_col0):
    cb = ATT_QW // ATT_PAIR
    qi = (lambda p, m: (jnp.minimum(m, nq - 1), q_col0 + p)) if clip_q else (lambda p, m: (m, q_col0 + p))
    q = pl.BlockSpec((tq, ATT_PAIR), qi)

    def win(col0):
        return [pl.BlockSpec((tq, ATT_PAIR), functools.partial(lambda p, m, back: (jnp.clip(m - back, 0, nq - 1), col0 + p), back=b))
                for b in (2, 1, 0)]

    bias = pl.BlockSpec((None, 2, ATT_TQ, ATT_WIN), lambda p, m: (jnp.minimum(m, 2), p, 0, 0))
    return q, win(cb), win(2 * cb), bias


def _head_masks(rows):
    lane = lax.broadcasted_iota(jnp.int32, (rows, ATT_PAIR), 1)
    return lane < ATT_HEAD_DIM


def _att_fwd(z_b, bias, y, comm=None, name="att_fwd"):
    T = z_b.shape[0]
    tq = ATT_TQ
    nq = T // tq
    qs, kwin, vwin, bs = _att_specs(tq, nq, False, 0)
    nc = comm.n if comm else 0
    total = ATT_NP * nq

    def body(*refs):
        q_ref, k0, k1, k2, v0, v1, v2, b_ref = refs[:8]
        cin = refs[9:9 + nc]
        o_ref = refs[9 + nc]
        cout = refs[10 + nc:10 + 2 * nc]
        csem = refs[10 + 2 * nc:]
        m = pl.program_id(1)
        step = pl.program_id(0) * nq + m
        _host_comm(comm, "early", step, total, cin, cout, csem)
        kw = jnp.concatenate([k0[...], k1[...], k2[...]], axis=0)
        vw = jnp.concatenate([v0[...], v1[...], v2[...]], axis=0)
        q2 = q_ref[...] * ATT_SCALE
        even = _head_masks(tq)
        outs = []
        for hh in range(2):
            qm = jnp.where(even if hh == 0 else ~even, q2, jnp.zeros_like(q2))
            p = _att_probs(qm, kw, b_ref[hh])
            outs.append(_dot(p.astype(BF16), vw, NN))
        o_ref[...] = jnp.where(even, outs[0], outs[1]).astype(o_ref.dtype)
        _host_comm(comm, "late", step, total, cin, cout, csem)

    y_cb = (Y_COLS - ATT_QW) // ATT_PAIR
    out = pl.pallas_call(
        body,
        grid=(ATT_NP, nq),
        in_specs=[qs] + kwin + vwin + [bs, ANY] + [ANY] * nc,
        out_specs=[pl.BlockSpec((tq, ATT_PAIR), lambda p, m: (m, y_cb + p))] + [ANY] * nc,
        out_shape=[jax.ShapeDtypeStruct((T, Y_COLS), BF16)] + (comm.out_shape if comm else []),
        scratch_shapes=comm.scratch if comm else [],
        input_output_aliases={8: 0},
        compiler_params=_cp("arbitrary", "arbitrary"),
        name=name,
    )(z_b, z_b, z_b, z_b, z_b, z_b, z_b, bias, y, *(comm.arrays if comm else []))
    return out[0], list(out[1:])


def _att_bwd(z_b, bias, dy, comm=None, name="att_bwd"):
    T = z_b.shape[0]
    tq = ATT_TQ
    nq = T // tq
    y_cb = (Y_COLS - ATT_QW) // ATT_PAIR
    qs, kwin, vwin, bs = _att_specs(tq, nq, True, 0)
    dos = _att_specs(tq, nq, True, y_cb)[0]
    kv_out = pl.BlockSpec((tq, ATT_PAIR), lambda p, m: (jnp.maximum(m - 2, 0), p))
    W3 = 3 * tq
    nc = comm.n if comm else 0
    total = ATT_NP * (nq + 2)

    def body(*refs):
        q_ref, k0, k1, k2, v0, v1, v2, b_ref, do_ref = refs[:9]
        cin = refs[9:9 + nc]
        dq_ref, dk_ref, dv_ref, db_ref = refs[9 + nc:13 + nc]
        cout = refs[13 + nc:13 + 2 * nc]
        dkc, dvc, dkw, dvw = refs[13 + 2 * nc:17 + 2 * nc]
        csem = refs[17 + 2 * nc:]
        m = pl.program_id(1)
        step = pl.program_id(0) * (nq + 2) + m
        _host_comm(comm, "early", step, total, cin, cout, csem)

        @pl.when(m == 0)
        def _():
            dkc[...] = jnp.zeros_like(dkc)
            dvc[...] = jnp.zeros_like(dvc)
            db_ref[...] = jnp.zeros_like(db_ref)

        @pl.when(m >= nq)
        def _():
            dkw[...] = jnp.zeros_like(dkw)
            dvw[...] = jnp.zeros_like(dvw)

        @pl.when(m < nq)
        def _():
            kw = jnp.concatenate([k0[...], k1[...], k2[...]], axis=0)
            vw = jnp.concatenate([v0[...], v1[...], v2[...]], axis=0)
            q2, do2 = q_ref[...] * ATT_SCALE, do_ref[...]
            even = _head_masks(tq)
            dqs, dks, dvs = [], [], []
            for hh in range(2):
                mine = even if hh == 0 else ~even
                p = _att_probs(jnp.where(mine, q2, jnp.zeros_like(q2)), kw, b_ref[hh])
                dp = _dot(jnp.where(mine, do2, jnp.zeros_like(do2)), vw, NT)
                ds = p * (dp - jnp.sum(dp * p, axis=-1, keepdims=True))
                db_ref[hh] += ds
                dsb = ds.astype(BF16)
                dqs.append(_dot(dsb, kw, NN))
                dks.append(_dot(dsb, q2, TN))
                dvs.append(_dot(p.astype(BF16), do2, TN))
            even_w = _head_masks(W3)
            dq_ref[...] = (jnp.where(even, dqs[0], dqs[1]) * ATT_SCALE).astype(dq_ref.dtype)
            dkw[...] = jnp.where(even_w, dks[0], dks[1])
            dvw[...] = jnp.where(even_w, dvs[0], dvs[1])

        dk_ref[...] = (dkc[0:tq, :] + dkw[0:tq, :]).astype(dk_ref.dtype)
        dv_ref[...] = (dvc[0:tq, :] + dvw[0:tq, :]).astype(dv_ref.dtype)
        dkc[0:tq, :] = dkc[tq:2 * tq, :] + dkw[tq:2 * tq, :]
        dvc[0:tq, :] = dvc[tq:2 * tq, :] + dvw[tq:2 * tq, :]
        dkc[tq:2 * tq, :] = dkw[2 * tq:W3, :]
        dvc[tq:2 * tq, :] = dvw[2 * tq:W3, :]
        _host_comm(comm, "late", step, total, cin, cout, csem)

    qo = pl.BlockSpec((tq, ATT_PAIR), lambda p, m: (jnp.minimum(m, nq - 1), p))
    dbs = pl.BlockSpec((2, ATT_TQ, ATT_WIN), lambda p, m: (p, 0, 0))
    hd = jax.ShapeDtypeStruct((T, ATT_QW), BF16)
    out = pl.pallas_call(
        body,
        grid=(ATT_NP, nq + 2),
        in_specs=[qs] + kwin + vwin + [bs, dos] + [ANY] * nc,
        out_specs=[qo, kv_out, kv_out, dbs] + [ANY] * nc,
        out_shape=[hd, hd, hd, jax.ShapeDtypeStruct((ATT_HEADS, ATT_TQ, ATT_WIN), F32)] + (comm.out_shape if comm else []),
        scratch_shapes=[pltpu.VMEM((2 * tq, ATT_PAIR), F32), pltpu.VMEM((2 * tq, ATT_PAIR), F32),
                        pltpu.VMEM((W3, ATT_PAIR), F32), pltpu.VMEM((W3, ATT_PAIR), F32)] + (comm.scratch if comm else []),
        compiler_params=_cp("arbitrary", "arbitrary"),
        name=name,
    )(z_b, z_b, z_b, z_b, z_b, z_b, z_b, bias, dy, *(comm.arrays if comm else []))
    return out[0], out[1], out[2], out[3], list(out[4:])


REST = ["ab_out", "c_in_t", "c_out", "up_t0", "up_t1", "down0", "down1"]


def _local_step(x, target, wt, small, rest_shards=None, overlap=False):
    T = x.shape[0]
    Fh = FFN_HIDDEN
    tables = _ret_tables(T)
    gw, gs, recv = {}, {}, {}
    wt = dict(wt)

    hn0 = _rms_fwd(x, small["attn_norm_g"][0], name="rms_fwd")
    z_a = _mm(hn0, wt["ab_in_t"][:RET_W], "nt", BF16, name="mm_ab_in_a")
    z_b = _mm(hn0, wt["ab_in_t"][RET_W:], "nt", BF16, name="mm_ab_in_b")
    y, states = _ret_fwd(z_a, tables)
    bias = _bias_tile(_bias_expand(small["rel_bias"]))
    y, rest = _att_fwd(z_b, bias, y, comm=_Comm("gather", rest_shards) if rest_shards is not None else None)
    if rest_shards is not None:
        full = dict(zip(REST, rest))
        wt.update(ab_out=full["ab_out"], c_in_t=full["c_in_t"], c_out=full["c_out"],
                  up_t=[full["up_t0"], full["up_t1"]], down=[full["down0"], full["down1"]])
    h1, hf0 = _mm_rows(y, wt["ab_out"], x, "mm_ab_out", norm_g=small["ffn_norm_g"][0])

    def ffn_fwd(h, hf, layer, next_g):
        zg = _mm(hf, wt["up_t"][layer][:Fh], "nt", BF16, name="mm_up")
        zu = _mm(hf, wt["up_t"][layer][Fh:], "nt", BF16, name="mm_up")
        a = _ffn_mid_fwd(zg, zu, small["conv_w"][layer], small["conv_b"][layer][None, :])
        if next_g is None:
            h_out, hn_next = _mm(a, wt["down"][layer], "nn", F32, res=h, name="mm_down_last"), None
        else:
            h_out, hn_next = _mm_rows(a, wt["down"][layer], h, "mm_down", norm_g=next_g)
        return h_out, hn_next, (hf, zg, zu, a)

    def ffn_bwd(dh_out, h, layer, saved, exchange=None):
        hf, zg, zu, a = saved
        da = _mm(dh_out, wt["down"][layer], "nt", BF16, name="mm_d_a")
        d_down = _mm(a, dh_out, "tn", BF16, name="mm_dw_down")
        comm = _Comm("exchange", [gw[k] for k in exchange]) if exchange else None
        dzg, dzu, dcw, dcb, got = _ffn_mid_bwd(zg, zu, da, small["conv_w"][layer], small["conv_b"][layer][None, :], comm=comm)
        recv.update(zip(exchange or [], got))
        dhf = _mm(dzg, wt["up_t"][layer][:Fh], "nn", F32, name="mm_d_hf")
        dh, dg = _mm_rows(dzu, wt["up_t"][layer][Fh:], dhf, "mm_d_hf_norm", bwd=(h, small["ffn_norm_g"][layer], dh_out))
        d_up = jnp.concatenate([_mm(dzg, hf, "tn", BF16, name="mm_dw_up"), _mm(dzu, hf, "tn", BF16, name="mm_dw_up")], axis=0)
        return dh, dg, d_up, d_down, dcw, dcb

    h2, hn1, ffn0 = ffn_fwd(h1, hf0, 0, small["attn_norm_g"][1])

    zz = _mm(hn1, wt["c_in_t"], "nt", BF16, name="mm_c_in")
    ys = _sgu_fwd(zz, small["ln_g"], small["ln_b"], small["w_s"], small["b_s"])
    h3, hf1 = _mm_rows(ys, wt["c_out"], h2, "mm_c_out", norm_g=small["ffn_norm_g"][1])
    h4, _, ffn1 = ffn_fwd(h3, hf1, 1, None)

    loss_vec, dh4, gs["final_g"] = _final_loss(h4, small["final_g"], target)

    dh3, dgf1, gw["up_t1"], gw["down1"], dcw1, dcb1 = ffn_bwd(dh4, h3, 1, ffn1)
    dys = _mm(dh3, wt["c_out"], "nt", BF16, name="mm_d_ys")
    gw["c_out"] = _mm(ys, dh3, "tn", BF16, name="mm_dw_c_out")
    dzz, gs["w_s"], dbs, dlg, dlb = _sgu_bwd(zz, dys, small["ln_g"], small["ln_b"], small["w_s"], small["b_s"])
    gs["b_s"], gs["ln_g"], gs["ln_b"] = dbs[:, :, 0], dlg[0], dlb[0]
    dh2, dga1 = _mm_rows(dzz, wt["c_in_t"], None, "mm_d_hn1_norm", bwd=(h2, small["attn_norm_g"][1], dh3))
    gw["c_in_t"] = _mm(dzz, hn1, "tn", BF16, name="mm_dw_c_in")

    dh1, dgf0, gw["up_t0"], gw["down0"], dcw0, dcb0 = ffn_bwd(
        dh2, h1, 0, ffn0, exchange=["c_in_t", "c_out", "up_t1", "down1"] if overlap else None)

    dy = _mm(dh1, wt["ab_out"], "nt", BF16, name="mm_d_y")
    gw["ab_out"] = _mm(y, dh1, "tn", BF16, name="mm_dw_ab_out")
    dz_a = _ret_bwd(z_a, dy, states, tables)
    late = ["ab_out", "up_t0", "down0"] if overlap else []
    dq, dk, dv, dbias, got = _att_bwd(z_b, bias, dy, comm=_Comm("exchange", [gw[k] for k in late]) if late else None)
    recv.update(zip(late, got))
    dz_b = jnp.concatenate([dq, dk, dv], axis=1)
    gs["rel_bias"] = _bias_reduce(_bias_untile(dbias))
    gw["ab_in_t"] = jnp.concatenate([_mm(dz_a, hn0, "tn", BF16, name="mm_dw_ab_in_a"),
                                     _mm(dz_b, hn0, "tn", BF16, name="mm_dw_ab_in_b")], axis=0)
    if overlap:
        dhn0, got = _mm(dz_a, wt["ab_in_t"][:RET_W], "nn", F32, name="mm_d_hn0", comm=_Comm("exchange", [gw["ab_in_t"]]))
        recv["ab_in_t"] = got[0]
    else:
        dhn0 = _mm(dz_a, wt["ab_in_t"][:RET_W], "nn", F32, name="mm_d_hn0")
    grad_x, dga0 = _mm_rows(dz_b, wt["ab_in_t"][RET_W:], dhn0, "mm_d_hn0_norm", bwd=(x, small["attn_norm_g"][0], dh1))

    gs["attn_norm_g"] = jnp.concatenate([dga0, dga1], axis=0)
    gs["ffn_norm_g"] = jnp.concatenate([dgf0, dgf1], axis=0)
    gs["conv_w"] = jnp.stack([dcw0, dcw1])
    gs["conv_b"] = jnp.concatenate([dcb0, dcb1], axis=0)
    gs["final_g"] = gs["final_g"][0]
    return loss_vec[0, 0], grad_x, gw, gs, recv


MESH_ID = pl.DeviceIdType.MESH
ANY = pl.BlockSpec(memory_space=pl.ANY)


def _my_place():
    return lax.axis_index("x"), lax.axis_index("y"), lax.axis_index("c")


class _Comm:
    def __init__(self, kind, arrays):
        self.kind, self.arrays, self.n = kind, list(arrays), len(arrays)
        if kind == "gather":
            self.out_shape = [jax.ShapeDtypeStruct((N_DEV * s.shape[0], s.shape[1]), s.dtype) for s in arrays]
        else:
            self.out_shape = [jax.ShapeDtypeStruct((N_DEV, g.shape[0] // N_DEV, g.shape[1]), g.dtype) for g in arrays]
        n = self.n
        self.scratch = [pltpu.SemaphoreType.DMA((n, 7)), pltpu.SemaphoreType.DMA((n, 7)), pltpu.SemaphoreType.DMA((n,))]

    def phase(self, ph, in_refs, out_refs, sems):
        (self._gather if self.kind == "gather" else self._exchange)(ph, in_refs, out_refs, sems)

    def _gather(self, ph, x_refs, o_refs, sems):
        n = self.n
        send_sems, recv_sems, local_sems = sems
        x, y, c = _my_place()
        me, sibling = (x, y, c), (x, y, 1 - c)
        chips = [(1 - x, y), (x, 1 - y), (1 - x, 1 - y)]

        def rows(a, place):
            m = x_refs[a].shape[0]
            px, py, pc = place
            return o_refs[a].at[pl.ds((4 * px + 2 * py + pc) * m, m), :]

        def copy(a, k, block, to, own=False):
            return pltpu.make_async_remote_copy(
                src_ref=x_refs[a] if own else rows(a, block), dst_ref=rows(a, block),
                send_sem=send_sems.at[a, k], recv_sem=recv_sems.at[a, k], device_id=to, device_id_type=MESH_ID)

        def mine():
            return [pltpu.make_async_copy(x_refs[a], rows(a, me), local_sems.at[a]) for a in range(n)]

        def first():
            out = []
            for a in range(n):
                out.append(copy(a, 0, me, sibling, own=True))
                out += [copy(a, 1 + j, me, (*chip, c), own=True) for j, chip in enumerate(chips)]
            return out

        def passed():
            return [copy(a, 4 + j, (*chip, c), sibling) for j, chip in enumerate(chips) for a in range(n)]

        if ph == 0:
            for cp in mine() + first():
                cp.start()
        elif ph == 1:
            fw = passed()
            for j, chip in enumerate(chips):
                for a in range(n):
                    copy(a, 1 + j, (*chip, c), me).wait_recv()
                    fw[j * n + a].start()
        else:
            for a in range(n):
                copy(a, 0, sibling, me).wait_recv()
                for j, chip in enumerate(chips):
                    copy(a, 4 + j, (*chip, 1 - c), me).wait_recv()
            for cp in first() + passed():
                cp.wait_send()
            for cp in mine():
                cp.wait()

    def _exchange(self, ph, g_refs, o_refs, sems):
        n = self.n
        send_sems, recv_sems, local_sems = sems
        x, y, c = _my_place()
        me = 4 * x + 2 * y + c
        peers = [(x ^ ((k >> 2) & 1), y ^ ((k >> 1) & 1), c ^ (k & 1)) for k in range(1, N_DEV)]

        def block(a, idx):
            m = g_refs[a].shape[0] // N_DEV
            return g_refs[a].at[pl.ds(idx * m, m), :]

        def copy(a, k, slot):
            px, py, pc = peers[k]
            return pltpu.make_async_remote_copy(
                src_ref=block(a, 4 * px + 2 * py + pc), dst_ref=o_refs[a].at[slot],
                send_sem=send_sems.at[a, k], recv_sem=recv_sems.at[a, k], device_id=peers[k], device_id_type=MESH_ID)

        if ph == 1:
            return
        mine = [pltpu.make_async_copy(block(a, me), o_refs[a].at[me], local_sems.at[a]) for a in range(n)]
        sends = [copy(a, k, me) for k in range(N_DEV - 1) for a in range(n)]
        if ph == 0:
            for cp in mine + sends:
                cp.start()
        else:
            for k in range(N_DEV - 1):
                px, py, pc = peers[k]
                for a in range(n):
                    copy(a, k, 4 * px + 2 * py + pc).wait_recv()
            for cp in sends:
                cp.wait_send()
            for cp in mine:
                cp.wait()


def _comm_call(comm, name):
    n = comm.n

    def body(*refs):
        for ph in range(3):
            comm.phase(ph, refs[:n], refs[n:2 * n], refs[2 * n:])

    return pl.pallas_call(
        body, out_shape=comm.out_shape, in_specs=[ANY] * n, out_specs=[ANY] * n, scratch_shapes=comm.scratch, name=name,
    )(*comm.arrays)


def _host_comm(comm, when, step, total, cin, cout, csem):
    if comm is None:
        return
    sched = {0: 0, 1: (3 * total) // 4, 2: total - 1}
    phases = (0, 1) if when == "early" else (2,)
    for ph in phases:
        if ph == 1 and comm.kind == "exchange":
            continue

        @pl.when(step == sched[ph])
        def _(ph=ph):
            comm.phase(ph, cin, cout, csem)


def _all_gather(shards, name="all_gather"):
    return _comm_call(_Comm("gather", shards), name)


def _row_tile(r, target=256):
    best = None
    for t in range(8, min(r, target) + 1, 8):
        if r % t == 0:
            best = t
    return best if best is not None else r


def _sum8(parts, name="sum8"):
    _, M, N = parts.shape
    tr = _row_tile(M, 128)

    def body(p_ref, o_ref):
        acc = p_ref[0].astype(F32)
        for d in range(1, N_DEV):
            acc = acc + p_ref[d].astype(F32)
        o_ref[...] = acc

    return pl.pallas_call(
        body,
        grid=(M // tr,),
        in_specs=[pl.BlockSpec((N_DEV, tr, N), lambda i: (0, i, 0))],
        out_specs=pl.BlockSpec((tr, N), lambda i: (i, 0)),
        out_shape=jax.ShapeDtypeStruct((M, N), F32),
        compiler_params=_cp("parallel"),
        name=name,
    )(parts)


def _adamw(w, g, m, v, name="adamw"):
    shape = w.shape
    if w.ndim == 1:
        r2 = (1, shape[0])
    else:
        r2 = (int(np.prod(shape[:-1])), shape[-1])
    R, C = r2
    tr = _row_tile(R)
    bc1 = 1.0 - ADAM_B1 ** ADAM_STEP
    bc2 = 1.0 - ADAM_B2 ** ADAM_STEP

    def body(w_ref, g_ref, m_ref, v_ref, d_ref, nm_ref, nv_ref):
        gv = g_ref[...]
        nm = ADAM_B1 * m_ref[...] + (1.0 - ADAM_B1) * gv
        nv = ADAM_B2 * v_ref[...] + (1.0 - ADAM_B2) * (gv * gv)
        d_ref[...] = -ADAM_LR * ((nm / bc1) / (jnp.sqrt(nv / bc2) + ADAM_EPS) + ADAM_WD * w_ref[...])
        nm_ref[...] = nm
        nv_ref[...] = nv

    spec = pl.BlockSpec((tr, C), lambda i: (i, 0))
    out = pl.pallas_call(
        body,
        grid=(R // tr,),
        in_specs=[spec] * 4,
        out_specs=[spec] * 3,
        out_shape=[jax.ShapeDtypeStruct(r2, F32)] * 3,
        compiler_params=_cp("parallel"),
        name=name,
    )(w.reshape(r2), g.reshape(r2), m.reshape(r2), v.reshape(r2))
    return [o.reshape(shape) for o in out]


WEIGHTS = ['attn_norm_g', 'ffn_norm_g', 'ab_w_in', 'ab_w_out', 'ab_rel_bias', 'c_w_in', 'c_ln_g', 'c_ln_b', 'c_w_s', 'c_b_s',
           'c_w_out', 'ffn_w_up', 'ffn_conv_w', 'ffn_conv_b', 'ffn_w_down', 'final_norm_g']
SMALL_ORDER = ["attn_norm_g", "ffn_norm_g", "rel_bias", "ln_g", "ln_b", "w_s", "b_s", "conv_w", "conv_b", "final_g"]
PACK_ROW = 1024


def _pack(arrs):
    flat = jnp.concatenate([a.reshape(-1) for a in arrs])
    n = flat.shape[0]
    padded = -(-n // PACK_ROW) * PACK_ROW
    return jnp.pad(flat, (0, padded - n)).reshape(padded // 128, 128)


def _unpack(flat, shapes):
    out, off = [], 0
    for s in shapes:
        n = int(np.prod(s))
        out.append(flat[off:off + n].reshape(s))
        off += n
    return out


def _step(P):
    x, target = P["x"][0], P["loss_target"][0]
    me = 4 * lax.axis_index("x") + 2 * lax.axis_index("y") + lax.axis_index("c")
    n_up = P["ffn_w_up"].shape[0]
    Fc = P["ffn_conv_w"].shape[-1]
    Lc = P["c_ln_g"].shape[-1]

    full = _all_gather([P["ab_w_in"][0].T.astype(BF16), _pack([P["ffn_conv_w"], P["c_ln_g"], P["c_ln_b"]])],
                       name="gather_first")
    wt = {"ab_in_t": full[0]}
    rest = {"ab_out": P["ab_w_out"][0], "c_in_t": P["c_w_in"][0].T, "c_out": P["c_w_out"][0],
            "up_t0": P["ffn_w_up"][0].T, "up_t1": P["ffn_w_up"][1].T, "down0": P["ffn_w_down"][0], "down1": P["ffn_w_down"][1]}
    rest_shards = [rest[k].astype(BF16) for k in REST]
    sm = full[-1].reshape(N_DEV, -1)
    conv_w = sm[:, :n_up * 3 * Fc].reshape(N_DEV, n_up, 3, Fc).transpose(1, 2, 0, 3).reshape(n_up, 3, N_DEV * Fc)
    off = n_up * 3 * Fc
    ln_g = sm[:, off:off + Lc].reshape(N_DEV * Lc)
    ln_b = sm[:, off + Lc:off + 2 * Lc].reshape(N_DEV * Lc)
    small = {"attn_norm_g": P["attn_norm_g"], "ffn_norm_g": P["ffn_norm_g"], "rel_bias": P["ab_rel_bias"][0],
             "ln_g": ln_g, "ln_b": ln_b, "w_s": P["c_w_s"][0], "b_s": P["c_b_s"][0], "conv_w": conv_w,
             "conv_b": P["ffn_conv_b"], "final_g": P["final_norm_g"]}

    loss_part, grad_x, gw, gs, recv = _local_step(x, target, wt, small, rest_shards=rest_shards, overlap=True)
    loss = lax.psum(loss_part, ("x", "y", "c"))

    s8 ={k: _sum8(recv[k], name="sum8") for k in ["ab_in_t"] + REST}
    g_big = {"ab_w_in": s8["ab_in_t"].T[None], "ab_w_out": s8["ab_out"][None], "c_w_in": s8["c_in_t"].T[None],
             "c_w_out": s8["c_out"][None], "ffn_w_up": jnp.stack([s8["up_t0"].T, s8["up_t1"].T]),
             "ffn_w_down": jnp.stack([s8["down0"], s8["down1"]])}

    packed = _pack([gs[k] for k in SMALL_ORDER])
    gathered = _all_gather([packed], name="gather_small_grads")[0]
    tot = _sum8(gathered.reshape(N_DEV, packed.shape[0], 128), name="sum8_small").reshape(-1)
    gsm = dict(zip(SMALL_ORDER, _unpack(tot, [gs[k].shape for k in SMALL_ORDER])))
    grads = dict(g_big)
    grads["attn_norm_g"] = gsm["attn_norm_g"]
    grads["ffn_norm_g"] = gsm["ffn_norm_g"]
    grads["ab_rel_bias"] = gsm["rel_bias"][None]
    grads["c_ln_g"] = lax.dynamic_slice(gsm["ln_g"], (me * Lc,), (Lc,))[None]
    grads["c_ln_b"] = lax.dynamic_slice(gsm["ln_b"], (me * Lc,), (Lc,))[None]
    grads["c_w_s"] = gsm["w_s"][None]
    grads["c_b_s"] = gsm["b_s"][None]
    grads["ffn_conv_w"] = lax.dynamic_slice(gsm["conv_w"], (0, 0, me * Fc), (n_up, 3, Fc))
    grads["ffn_conv_b"] = gsm["conv_b"]
    grads["final_norm_g"] = gsm["final_g"]

    delta, new_m, new_v = {}, {}, {}
    for k in WEIGHTS:
        delta[k], new_m[k], new_v[k] = _adamw(P[k], grads[k], P["m_" + k], P["v_" + k], name="adamw")
    return (loss, grad_x[None], *[grads[k] for k in WEIGHTS], *[delta[k] for k in WEIGHTS],
            *[new_m[k] for k in WEIGHTS], *[new_v[k] for k in WEIGHTS])


def kernel(x, attn_norm_g, ffn_norm_g, ab_w_in, ab_w_out, ab_rel_bias, c_w_in, c_ln_g, c_ln_b, c_w_s, c_b_s, c_w_out, ffn_w_up, ffn_conv_w, ffn_conv_b, ffn_w_down, final_norm_g, loss_target, m_attn_norm_g, m_ffn_norm_g, m_ab_w_in, m_ab_w_out, m_ab_rel_bias, m_c_w_in, m_c_ln_g, m_c_ln_b, m_c_w_s, m_c_b_s, m_c_w_out, m_ffn_w_up, m_ffn_conv_w, m_ffn_conv_b, m_ffn_w_down, m_final_norm_g, v_attn_norm_g, v_ffn_norm_g, v_ab_w_in, v_ab_w_out, v_ab_rel_bias, v_c_w_in, v_c_ln_g, v_c_ln_b, v_c_w_s, v_c_b_s, v_c_w_out, v_ffn_w_up, v_ffn_conv_w, v_ffn_conv_b, v_ffn_w_down, v_final_norm_g):
    return _step(dict(locals()))
```

```python
import functools

import numpy as np
import jax
import jax.numpy as jnp
from jax import lax
from jax.experimental import pallas as pl
from jax.experimental.pallas import tpu as pltpu

F32 = jnp.float32
BF16 = jnp.bfloat16

D_MODEL = 1024
CHUNK = 64
EPS = 1e-6
NEG_INF = -1e30
RET_HEADS = 4
RET_QK_DIM = 128
RET_V_DIM = 256
ATT_HEADS = 8
ATT_HEAD_DIM = 64
ATT_PAST = 8
ATT_BAND = (ATT_PAST + 1) * CHUNK
MAX_REL = 128
N_REL = 2 * MAX_REL + 1
N_REL_PAD = 384
SGU_BLOCK = 128
SGU_GROUPS = 8
SGU_WIDTH = 2048
SGU_GW = SGU_WIDTH // SGU_GROUPS
FFN_HIDDEN = 2816
RET_W = 2 * RET_HEADS * RET_QK_DIM + 2 * RET_HEADS * RET_V_DIM
ATT_W = 3 * ATT_HEADS * ATT_HEAD_DIM
N_DEV = 8

ADAM_LR = 0.001
ADAM_B1 = 0.9
ADAM_B2 = 0.999
ADAM_EPS = 1e-08
ADAM_WD = 0.01
ADAM_STEP = 10

VMEM_LIMIT = 52 * 1024 * 1024


def _cp(*sem):
    return pltpu.CompilerParams(dimension_semantics=sem if sem else None, vmem_limit_bytes=VMEM_LIMIT)


def _tile(n, target):
    if n <= target:
        return n
    best = None
    for t in range(128, target + 1, 128):
        if n % t == 0:
            best = t
    assert best is not None, (n, target)
    return best


def _gelu(x):
    c = 0.7978845608028654
    return 0.5 * x * (1.0 + jnp.tanh(c * (x + 0.044715 * x * x * x)))


def _gelu_and_grad(x):
    c = 0.7978845608028654
    x2 = x * x
    t = jnp.tanh(c * (x + 0.044715 * x * x2))
    cdf = 0.5 * (1.0 + t)
    grad = cdf + x * (0.5 * c) * (1.0 - t * t) * (1.0 + 3.0 * 0.044715 * x2)
    return x * cdf, grad


def _dot(a, b, dims):
    return lax.dot_general(a, b, (dims, ((), ())), preferred_element_type=F32)


NN = ((1,), (0,))
NT = ((1,), (1,))
TN = ((0,), (0,))


def _mm(a, b, mode, out_dtype, res=None, name="mm", tm_t=None, tn_t=None, tk_t=None, comm=None):
    if mode == "nt":
        (M, K), N = a.shape, b.shape[0]
        dm, dn, dk = (512, 2816, K) if N <= 2816 else (1024, 1024, K)
    elif mode == "nn":
        (M, K), N = a.shape, b.shape[1]
        dm, dn, dk = (1024 if K <= 3072 else 512), 1024, K
    else:
        (K, M), N = a.shape, b.shape[1]
        dm, dn, dk = 1536, 1024, 2048
    tm, tn, tk = _tile(M, tm_t or dm), _tile(N, tn_t or dn), _tile(K, tk_t or dk)
    nk = K // tk
    dims = {"nt": NT, "nn": NN, "tn": TN}[mode]
    a_spec = pl.BlockSpec((tk, tm), lambda i, j, k: (k, i)) if mode == "tn" else pl.BlockSpec((tm, tk), lambda i, j, k: (i, k))
    b_spec = pl.BlockSpec((tn, tk), lambda i, j, k: (j, k)) if mode == "nt" else pl.BlockSpec((tk, tn), lambda i, j, k: (k, j))
    o_spec = pl.BlockSpec((tm, tn), lambda i, j, k: (i, j))
    has_res = res is not None
    nc = comm.n if comm else 0
    n_in = 3 if has_res else 2
    gi, gj = M // tm, N // tn

    def body(*refs):
        a_ref, b_ref = refs[:2]
        r_ref = refs[2] if has_res else None
        cin = refs[n_in:n_in + nc]
        o_ref = refs[n_in + nc]
        cout = refs[n_in + nc + 1:n_in + 2 * nc + 1]
        scratch = refs[n_in + 2 * nc + 1:]
        csem = scratch[1:] if nk > 1 else scratch
        step = (pl.program_id(0) * gj + pl.program_id(1)) * nk + pl.program_id(2)
        _host_comm(comm, "early", step, gi * gj * nk, cin, cout, csem)
        p = _dot(a_ref[...].astype(BF16), b_ref[...].astype(BF16), dims)
        if nk == 1:
            if has_res:
                p = p + r_ref[...]
            o_ref[...] = p.astype(out_dtype)
        else:
            acc = scratch[0]
            k = pl.program_id(2)

            @pl.when(k == 0)
            def _():
                acc[...] = p

            @pl.when(k > 0)
            def _():
                acc[...] += p

            @pl.when(k == nk - 1)
            def _():
                t = acc[...]
                if has_res:
                    t = t + r_ref[...]
                o_ref[...] = t.astype(out_dtype)
        _host_comm(comm, "late", step, gi * gj * nk, cin, cout, csem)

    in_specs = [a_spec, b_spec] + ([o_spec] if has_res else []) + [ANY] * nc
    args = (a, b) + ((res,) if has_res else ()) + tuple(comm.arrays if comm else ())
    out = pl.pallas_call(
        body,
        grid=(gi, gj, nk),
        in_specs=in_specs,
        out_specs=[o_spec] + [ANY] * nc,
        out_shape=[jax.ShapeDtypeStruct((M, N), out_dtype)] + (comm.out_shape if comm else []),
        scratch_shapes=([pltpu.VMEM((tm, tn), F32)] if nk > 1 else []) + (comm.scratch if comm else []),
        compiler_params=_cp("arbitrary", "arbitrary", "arbitrary") if comm else _cp("parallel", "parallel", "arbitrary"),
        name=name,
    )(*args)
    return (out[0], list(out[1:])) if comm else out[0]


def _mm_rows(a, b, res, name, norm_g=None, bwd=None, loss=None, tm=512):
    M, K = a.shape
    Dm = b.shape[1]
    tm = min(tm, M)
    row = pl.BlockSpec((tm, Dm), lambda i: (i, 0))
    vec = pl.BlockSpec((1, Dm), lambda i: (0, 0))
    a_spec = pl.BlockSpec((tm, K), lambda i: (i, 0))
    b_spec = pl.BlockSpec((K, Dm), lambda i: (0, 0))

    if loss is not None:
        target, g = loss

        def body(a_ref, b_ref, r_ref, g_ref, t_ref, loss_ref, dh_ref, dg_ref):
            @pl.when(pl.program_id(0) == 0)
            def _():
                loss_ref[...] = jnp.zeros_like(loss_ref)
                dg_ref[...] = jnp.zeros_like(dg_ref)

            x = _dot(a_ref[...].astype(BF16), b_ref[...].astype(BF16), NN) + r_ref[...]
            gv = g_ref[...]
            r = lax.rsqrt(jnp.mean(x * x, axis=-1, keepdims=True) + EPS)
            xhat = x * r
            e = xhat * gv - t_ref[...]
            loss_ref[...] += jnp.full((1, 128), 0.5 / Dm, F32) * jnp.sum(e * e)
            dy = e * (1.0 / Dm)
            dg_ref[...] += jnp.sum(dy * xhat, axis=0, keepdims=True)
            dx = dy * gv
            m = jnp.mean(dx * xhat, axis=-1, keepdims=True)
            dh_ref[...] = r * (dx - xhat * m)

        return pl.pallas_call(
            body, grid=(M // tm,), in_specs=[a_spec, b_spec, row, vec, row],
            out_specs=[pl.BlockSpec((1, 128), lambda i: (0, 0)), row, vec],
            out_shape=[jax.ShapeDtypeStruct((1, 128), F32), jax.ShapeDtypeStruct((M, Dm), F32), jax.ShapeDtypeStruct((1, Dm), F32)],
            compiler_params=_cp("arbitrary"), name=name,
        )(a, b, res, g.reshape(1, Dm), target)

    if bwd is None:
        def body(a_ref, b_ref, r_ref, g_ref, o_ref, n_ref):
            t = _dot(a_ref[...].astype(BF16), b_ref[...].astype(BF16), NN) + r_ref[...]
            o_ref[...] = t
            r = lax.rsqrt(jnp.mean(t * t, axis=-1, keepdims=True) + EPS)
            n_ref[...] = (t * r * g_ref[...]).astype(n_ref.dtype)

        return pl.pallas_call(
            body, grid=(M // tm,), in_specs=[a_spec, b_spec, row, vec], out_specs=[row, row],
            out_shape=[jax.ShapeDtypeStruct((M, Dm), F32), jax.ShapeDtypeStruct((M, Dm), BF16)],
            compiler_params=_cp("parallel"), name=name,
        )(a, b, res, norm_g.reshape(1, Dm))

    h, g, dres = bwd
    has_res = res is not None

    def body(*refs):
        a_ref, b_ref = refs[:2]
        h_ref, g_ref, dres_ref, dh_ref, dg_ref = refs[-5:]

        @pl.when(pl.program_id(0) == 0)
        def _():
            dg_ref[...] = jnp.zeros_like(dg_ref)

        d = _dot(a_ref[...].astype(BF16), b_ref[...].astype(BF16), NN)
        if has_res:
            d = d + refs[2][...]
        x = h_ref[...]
        r = lax.rsqrt(jnp.mean(x * x, axis=-1, keepdims=True) + EPS)
        xhat = x * r
        dg_ref[...] += jnp.sum(d * xhat, axis=0, keepdims=True)
        dx = d * g_ref[...]
        m = jnp.mean(dx * xhat, axis=-1, keepdims=True)
        dh_ref[...] = dres_ref[...] + r * (dx - xhat * m)

    return pl.pallas_call(
        body, grid=(M // tm,), in_specs=[a_spec, b_spec] + ([row] if has_res else []) + [row, vec, row], out_specs=[row, vec],
        out_shape=[jax.ShapeDtypeStruct((M, Dm), F32), jax.ShapeDtypeStruct((1, Dm), F32)],
        compiler_params=_cp("arbitrary"), name=name,
    )(a, b, *((res,) if has_res else ()), h, g.reshape(1, Dm), dres)


def _rms_fwd(h, g, name="rms_fwd", tm=512):
    T, Dm = h.shape
    tm = min(tm, T)

    def body(h_ref, g_ref, o_ref):
        x = h_ref[...]
        r = lax.rsqrt(jnp.mean(x * x, axis=-1, keepdims=True) + EPS)
        o_ref[...] = (x * r * g_ref[...]).astype(o_ref.dtype)

    return pl.pallas_call(
        body,
        grid=(T // tm,),
        in_specs=[pl.BlockSpec((tm, Dm), lambda i: (i, 0)), pl.BlockSpec((1, Dm), lambda i: (0, 0))],
        out_specs=pl.BlockSpec((tm, Dm), lambda i: (i, 0)),
        out_shape=jax.ShapeDtypeStruct((T, Dm), BF16),
        compiler_params=_cp("parallel"),
        name=name,
    )(h, g.reshape(1, Dm))


HALO = 16


def _conv3(ext, w_ref, b_ref):
    return w_ref[0:1, :] * pltpu.roll(ext, 2, 0) + w_ref[1:2, :] * pltpu.roll(ext, 1, 0) + w_ref[2:3, :] * ext + b_ref[...]


def _ffn_up_mid(hf, w_up_t, cw, cb, name="ffn_up_mid", tm=1024, tc=256):
    T, Dm = hf.shape
    Fh = w_up_t.shape[0] // 2
    tm = min(tm, T)
    nj, ni = Fh // tc, T // tm

    rc = min(256, tm)

    def body(h_ref, wg_ref, wu_ref, cwg_ref, cwu_ref, cbg_ref, cbu_ref, zg_ref, zu_ref, a_ref, eg_ref, eu_ref, ng_ref, nu_ref):
        @pl.when(pl.program_id(1) == 0)
        def _():
            eg_ref[...] = jnp.zeros_like(eg_ref)
            eu_ref[...] = jnp.zeros_like(eu_ref)

        for r in range(tm // rc):
            ext = slice(r * rc, r * rc + HALO + rc)
            cg = _conv3(eg_ref[ext, :].astype(F32), cwg_ref, cbg_ref)[HALO:]
            cu = _conv3(eu_ref[ext, :].astype(F32), cwu_ref, cbu_ref)[HALO:]
            a_ref[r * rc:(r + 1) * rc, :] = (_gelu(cg) * cu).astype(a_ref.dtype)
            h = h_ref[r * rc:(r + 1) * rc, :]
            for w_ref, z_ref, n_ref in ((wg_ref, zg_ref, ng_ref), (wu_ref, zu_ref, nu_ref)):
                z = _dot(h, w_ref[...], NT).astype(BF16)
                z_ref[r * rc:(r + 1) * rc, :] = z
                n_ref[r * rc:(r + 1) * rc, :] = z
        for e_ref, n_ref in ((eg_ref, ng_ref), (eu_ref, nu_ref)):
            e_ref[0:HALO, :] = e_ref[tm:tm + HALO, :]
            e_ref[HALO:HALO + tm, :] = n_ref[...]

    last = ni - 1
    z_spec = pl.BlockSpec((tm, tc), lambda j, i: (jnp.minimum(i, last), j))
    return pl.pallas_call(
        body,
        grid=(nj, ni + 1),
        in_specs=[pl.BlockSpec((tm, Dm), lambda j, i: (jnp.minimum(i, last), 0)),
                  pl.BlockSpec((tc, Dm), lambda j, i: (j, 0)), pl.BlockSpec((tc, Dm), lambda j, i: (nj + j, 0)),
                  pl.BlockSpec((3, tc), lambda j, i: (0, j)), pl.BlockSpec((3, tc), lambda j, i: (0, nj + j)),
                  pl.BlockSpec((1, tc), lambda j, i: (0, j)), pl.BlockSpec((1, tc), lambda j, i: (0, nj + j))],
        out_specs=[z_spec, z_spec, pl.BlockSpec((tm, tc), lambda j, i: (jnp.maximum(i - 1, 0), j))],
        out_shape=[jax.ShapeDtypeStruct((T, Fh), BF16)] * 3,
        scratch_shapes=[pltpu.VMEM((HALO + tm, tc), BF16), pltpu.VMEM((HALO + tm, tc), BF16),
                        pltpu.VMEM((tm, tc), BF16), pltpu.VMEM((tm, tc), BF16)],
        compiler_params=_cp("arbitrary", "arbitrary"),
        name=name,
    )(hf, w_up_t, w_up_t, cw, cw, cb, cb)


def _ffn_mid_bwd(zg, zu, da, cw, cb, comm=None, name="ffn_mid_bwd", tm=512, tc=256):
    T, Fh = zg.shape
    tm = min(tm, T)
    nj = Fh // tc
    hb = tm // HALO
    nhb = T // HALO

    nc = comm.n if comm else 0
    ni = T // tm

    def body(*refs):
        zg_ref, zu_ref, zgp_ref, zup_ref, zgn_ref, zun_ref, da_ref, dan_ref, wg_ref, wu_ref, bg_ref, bu_ref = refs[:12]
        cin = refs[12:12 + nc]
        dzg_ref, dzu_ref, dwg_ref, dwu_ref, dbg_ref, dbu_ref = refs[12 + nc:18 + nc]
        cout = refs[18 + nc:18 + 2 * nc]
        csem = refs[18 + 2 * nc:]
        i = pl.program_id(1)
        step = pl.program_id(0) * ni + i
        _host_comm(comm, "early", step, nj * ni, cin, cout, csem)
        first = i == 0
        last = i == ni - 1

        @pl.when(first)
        def _():
            dwg_ref[...] = jnp.zeros_like(dwg_ref)
            dwu_ref[...] = jnp.zeros_like(dwu_ref)
            dbg_ref[...] = jnp.zeros_like(dbg_ref)
            dbu_ref[...] = jnp.zeros_like(dbu_ref)

        def ext_of(p_ref, c_ref, n_ref):
            p = jnp.where(first, 0.0, p_ref[...].astype(F32))
            return jnp.concatenate([p, c_ref[...].astype(F32), n_ref[...].astype(F32)], axis=0)

        zge = ext_of(zgp_ref, zg_ref, zgn_ref)
        zue = ext_of(zup_ref, zu_ref, zun_ref)
        dan = jnp.where(last, 0.0, dan_ref[...].astype(F32))
        dae = jnp.concatenate([jnp.zeros((HALO, tc), F32), da_ref[...].astype(F32), dan], axis=0)
        zg1, zg2 = pltpu.roll(zge, 1, 0), pltpu.roll(zge, 2, 0)
        zu1, zu2 = pltpu.roll(zue, 1, 0), pltpu.roll(zue, 2, 0)
        cg = wg_ref[0:1, :] * zg2 + wg_ref[1:2, :] * zg1 + wg_ref[2:3, :] * zge + bg_ref[...]
        cu = wu_ref[0:1, :] * zu2 + wu_ref[1:2, :] * zu1 + wu_ref[2:3, :] * zue + bu_ref[...]
        gel, dgel = _gelu_and_grad(cg)
        dcg = dae * cu * dgel
        dcu = dae * gel
        lo, hi = HALO, HALO + tm

        def back(dc, taps, w_ref, dz_ref, dw_ref, db_ref):
            n = dc.shape[0]
            dz = w_ref[2:3, :] * dc + w_ref[1:2, :] * pltpu.roll(dc, n - 1, 0) + w_ref[0:1, :] * pltpu.roll(dc, n - 2, 0)
            dz_ref[...] = dz[lo:hi].astype(dz_ref.dtype)
            dcc = dc[lo:hi]
            db_ref[...] += jnp.sum(dcc, axis=0, keepdims=True)
            for k, tap in enumerate(taps):
                dw_ref[k:k + 1, :] += jnp.sum(dcc * tap[lo:hi], axis=0, keepdims=True)

        back(dcg, (zg2, zg1, zge), wg_ref, dzg_ref, dwg_ref, dbg_ref)
        back(dcu, (zu2, zu1, zue), wu_ref, dzu_ref, dwu_ref, dbu_ref)
        _host_comm(comm, "late", step, nj * ni, cin, cout, csem)

    cur = pl.BlockSpec((tm, tc), lambda j, i: (i, j))
    prev = pl.BlockSpec((HALO, tc), lambda j, i: (jnp.maximum(i * hb - 1, 0), j))
    nxt = pl.BlockSpec((HALO, tc), lambda j, i: (jnp.minimum((i + 1) * hb, nhb - 1), j))
    wg = pl.BlockSpec((3, tc), lambda j, i: (0, j))
    wu = pl.BlockSpec((3, tc), lambda j, i: (0, j + nj))
    bg = pl.BlockSpec((1, tc), lambda j, i: (0, j))
    bu = pl.BlockSpec((1, tc), lambda j, i: (0, j + nj))
    dw = pl.BlockSpec((3, tc), lambda j, i: (0, j))
    db = pl.BlockSpec((1, tc), lambda j, i: (0, j))
    out = pl.pallas_call(
        body,
        grid=(nj, ni),
        in_specs=[cur, cur, prev, prev, nxt, nxt, cur, nxt, wg, wu, bg, bu] + [ANY] * nc,
        out_specs=[cur, cur, dw, dw, db, db] + [ANY] * nc,
        out_shape=[jax.ShapeDtypeStruct((T, Fh), BF16), jax.ShapeDtypeStruct((T, Fh), BF16),
                   jax.ShapeDtypeStruct((3, Fh), F32), jax.ShapeDtypeStruct((3, Fh), F32),
                   jax.ShapeDtypeStruct((1, Fh), F32), jax.ShapeDtypeStruct((1, Fh), F32)] + (comm.out_shape if comm else []),
        scratch_shapes=comm.scratch if comm else [],
        compiler_params=_cp("arbitrary", "arbitrary"),
        name=name,
    )(zg, zu, zg, zu, zg, zu, da, da, cw, cw, cb, cb, *(comm.arrays if comm else []))
    dzg, dzu, dwg, dwu, dbg, dbu = out[:6]
    return dzg, dzu, jnp.concatenate([dwg, dwu], axis=1), jnp.concatenate([dbg, dbu], axis=1), list(out[6:])


def _sgu_mask():
    r = lax.broadcasted_iota(jnp.int32, (SGU_BLOCK, SGU_BLOCK), 0)
    c = lax.broadcasted_iota(jnp.int32, (SGU_BLOCK, SGU_BLOCK), 1)
    return (c < CHUNK) | (r >= CHUNK)


def _sgu_fwd(zz, ln_g, ln_b, w_s, b_s, name="sgu_fwd", tm=256):
    T = zz.shape[0]
    tm = min(tm, T)
    W = SGU_WIDTH

    def body(zu_ref, zv_ref, g_ref, b_ref, ws_ref, bs_ref, y_ref):
        u = _gelu(zu_ref[...].astype(F32))
        v = _gelu(zv_ref[...].astype(F32))
        mu = jnp.mean(v, axis=-1, keepdims=True)
        xc = v - mu
        rstd = lax.rsqrt(jnp.mean(xc * xc, axis=-1, keepdims=True) + EPS)
        vn = (xc * rstd * g_ref[...] + b_ref[...]).astype(BF16)
        mask = _sgu_mask()
        for g in range(SGU_GROUPS):
            wm = jnp.where(mask, ws_ref[g], 0.0).astype(BF16)
            cs = slice(g * SGU_GW, (g + 1) * SGU_GW)
            for blk in range(tm // SGU_BLOCK):
                rs = slice(blk * SGU_BLOCK, (blk + 1) * SGU_BLOCK)
                mixed = _dot(wm, vn[rs, cs], NN) + bs_ref[g]
                y_ref[rs, cs] = (u[rs, cs] * mixed).astype(y_ref.dtype)

    return pl.pallas_call(
        body,
        grid=(T // tm,),
        in_specs=[pl.BlockSpec((tm, W), lambda i: (i, 0)), pl.BlockSpec((tm, W), lambda i: (i, 1)),
                  pl.BlockSpec((1, W), lambda i: (0, 0)), pl.BlockSpec((1, W), lambda i: (0, 0)),
                  pl.BlockSpec((SGU_GROUPS, SGU_BLOCK, SGU_BLOCK), lambda i: (0, 0, 0)),
                  pl.BlockSpec((SGU_GROUPS, SGU_BLOCK, 1), lambda i: (0, 0, 0))],
        out_specs=pl.BlockSpec((tm, W), lambda i: (i, 0)),
        out_shape=jax.ShapeDtypeStruct((T, W), BF16),
        compiler_params=_cp("parallel"),
        name=name,
    )(zz, zz, ln_g.reshape(1, W), ln_b.reshape(1, W), w_s, b_s.reshape(SGU_GROUPS, SGU_BLOCK, 1))


def _sgu_bwd(zz, dy, ln_g, ln_b, w_s, b_s, name="sgu_bwd", tm=256):
    T = zz.shape[0]
    tm = min(tm, T)
    W = SGU_WIDTH

    def body(zu_ref, zv_ref, dy_ref, g_ref, b_ref, ws_ref, bs_ref, dzz_ref, dws_ref, dbs_ref, dg_ref, db_ref, dvn_ref):
        i = pl.program_id(0)

        @pl.when(i == 0)
        def _():
            dws_ref[...] = jnp.zeros_like(dws_ref)
            dbs_ref[...] = jnp.zeros_like(dbs_ref)
            dg_ref[...] = jnp.zeros_like(dg_ref)
            db_ref[...] = jnp.zeros_like(db_ref)

        u, du_dz = _gelu_and_grad(zu_ref[...].astype(F32))
        v, dv_dz = _gelu_and_grad(zv_ref[...].astype(F32))
        mu = jnp.mean(v, axis=-1, keepdims=True)
        xc = v - mu
        rstd = lax.rsqrt(jnp.mean(xc * xc, axis=-1, keepdims=True) + EPS)
        xhat = xc * rstd
        gv = g_ref[...]
        vn = (xhat * gv + b_ref[...]).astype(BF16)
        dyv = dy_ref[...].astype(F32)
        mask = _sgu_mask()
        for g in range(SGU_GROUPS):
            wm = jnp.where(mask, ws_ref[g], 0.0).astype(BF16)
            cs = slice(g * SGU_GW, (g + 1) * SGU_GW)
            dw_acc = jnp.zeros((SGU_BLOCK, SGU_BLOCK), F32)
            db_acc = jnp.zeros((SGU_BLOCK, 1), F32)
            for blk in range(tm // SGU_BLOCK):
                rs = slice(blk * SGU_BLOCK, (blk + 1) * SGU_BLOCK)
                vn_bg = vn[rs, cs]
                mixed = _dot(wm, vn_bg, NN) + bs_ref[g]
                dy_bg = dyv[rs, cs]
                dmixed = dy_bg * u[rs, cs]
                dmb = dmixed.astype(BF16)
                dw_acc += _dot(dmb, vn_bg, NT)
                db_acc += jnp.sum(dmixed, axis=1, keepdims=True)
                dvn_ref[rs, cs] = _dot(wm, dmb, TN)
                dzz_ref[rs, cs] = (dy_bg * mixed * du_dz[rs, cs]).astype(dzz_ref.dtype)
            dws_ref[g] += jnp.where(mask, dw_acc, 0.0)
            dbs_ref[g] += db_acc
        dvn = dvn_ref[...]
        dg_ref[...] += jnp.sum(dvn * xhat, axis=0, keepdims=True)
        db_ref[...] += jnp.sum(dvn, axis=0, keepdims=True)
        dxh = dvn * gv
        m1 = jnp.mean(dxh, axis=-1, keepdims=True)
        m2 = jnp.mean(dxh * xhat, axis=-1, keepdims=True)
        dv = rstd * (dxh - m1 - xhat * m2)
        dzz_ref[:, W:] = (dv * dv_dz).astype(dzz_ref.dtype)

    vec = pl.BlockSpec((1, W), lambda i: (0, 0))
    ws_spec = pl.BlockSpec((SGU_GROUPS, SGU_BLOCK, SGU_BLOCK), lambda i: (0, 0, 0))
    bs_spec = pl.BlockSpec((SGU_GROUPS, SGU_BLOCK, 1), lambda i: (0, 0, 0))
    return pl.pallas_call(
        body,
        grid=(T // tm,),
        in_specs=[pl.BlockSpec((tm, W), lambda i: (i, 0)), pl.BlockSpec((tm, W), lambda i: (i, 1)),
                  pl.BlockSpec((tm, W), lambda i: (i, 0)), vec, vec, ws_spec, bs_spec],
        out_specs=[pl.BlockSpec((tm, 2 * W), lambda i: (i, 0)), ws_spec, bs_spec, vec, vec],
        out_shape=[jax.ShapeDtypeStruct((T, 2 * W), BF16),
                   jax.ShapeDtypeStruct((SGU_GROUPS, SGU_BLOCK, SGU_BLOCK), F32),
                   jax.ShapeDtypeStruct((SGU_GROUPS, SGU_BLOCK, 1), F32),
                   jax.ShapeDtypeStruct((1, W), F32), jax.ShapeDtypeStruct((1, W), F32)],
        scratch_shapes=[pltpu.VMEM((tm, W), F32)],
        compiler_params=_cp("arbitrary"),
        name=name,
    )(zz, zz, dy, ln_g.reshape(1, W), ln_b.reshape(1, W), w_s, b_s.reshape(SGU_GROUPS, SGU_BLOCK, 1))


RET_TR = 256
RET_BLK = 256
QK_SCALE = RET_QK_DIM ** -0.5


def _ret_tables(T):
    half = RET_QK_DIM // 2
    inv = 1.0 / (10000.0 ** jnp.linspace(0.0, 1.0, half, dtype=F32))
    inv2 = jnp.concatenate([inv, inv])[None, :]
    sgn = jnp.concatenate([-jnp.ones((half,), F32), jnp.ones((half,), F32)])[None, :]
    tr = min(RET_TR, T)

    def trig(pos):
        ang = pos.astype(F32)[:, None] * inv2
        return jnp.stack([jnp.cos(ang), jnp.sin(ang), sgn * jnp.sin(ang)])

    tile_tab = jnp.pad(trig(jnp.arange(T // tr) * tr).transpose(1, 0, 2), ((0, 0), (0, 5), (0, 0)))
    row_tab = trig(jnp.arange(tr))
    log_g = jnp.log1p(-jnp.exp2(-5.0 - jnp.arange(RET_HEADS, dtype=F32)))
    idx = jnp.arange(RET_BLK, dtype=F32)
    dist = idx[:, None] - idx[None, :]
    cq, ck = jnp.arange(RET_BLK)[:, None] // CHUNK, jnp.arange(RET_BLK)[None, :] // CHUNK
    expo = jnp.where(ck == cq, jnp.abs(dist), dist)
    d_blk = jnp.where((ck <= cq)[None], jnp.exp(log_g[:, None, None] * expo[None]), 0.0)
    k_dec = jnp.exp(log_g[:, None] * (RET_BLK - 1 - idx)[None, :])[:, :, None]
    q_dec = jnp.exp(log_g[:, None] * (idx + 1.0)[None, :])[:, :, None]
    c_dec = jnp.exp(log_g * RET_BLK)[:, None, None]
    return tile_tab, row_tab, d_blk, q_dec, k_dec, c_dec


def _rot(x, c, s):
    return x * c + pltpu.roll(x, RET_QK_DIM // 2, 1) * s


def _rot_tables(tt_ref, rt_ref):
    ca, sa, ga = tt_ref[0:1, :], tt_ref[1:2, :], tt_ref[2:3, :]
    cb, sb, gb = rt_ref[0], rt_ref[1], rt_ref[2]
    return ca * cb - sa * sb, ga * cb + ca * gb


def _ret_specs(tr, rev, nb):
    ix = (lambda n: nb - 1 - n) if rev else (lambda n: n)
    tt = pl.BlockSpec((None, 8, RET_QK_DIM), lambda n: (ix(n), 0, 0))
    rt = pl.BlockSpec((3, tr, RET_QK_DIM), lambda n: (0, 0, 0))
    dm = pl.BlockSpec((RET_HEADS, RET_BLK, RET_BLK), lambda n: (0, 0, 0))
    dv = pl.BlockSpec((RET_HEADS, RET_BLK, 1), lambda n: (0, 0, 0))
    dc = pl.BlockSpec((RET_HEADS, 1, 1), lambda n: (0, 0, 0))
    return ix, [tt, rt, dm, dv, dv, dc]


def _ret_fwd(z_a, tables, name="ret_fwd"):
    T = z_a.shape[0]
    tr = min(RET_TR, T)
    cpb = tr // RET_BLK
    nb = T // tr
    QW, VW = RET_HEADS * RET_QK_DIM, RET_HEADS * RET_V_DIM
    ix, tab_specs = _ret_specs(tr, False, nb)

    def body(z_ref, tt_ref, rt_ref, dm_ref, qd_ref, kd_ref, cd_ref, y_ref, st_ref, state):
        @pl.when(pl.program_id(0) == 0)
        def _():
            state[...] = jnp.zeros_like(state)

        rot_c, rot_s = _rot_tables(tt_ref, rt_ref)
        for c in range(cpb):
            for h in range(RET_HEADS):
                rs = slice(c * RET_BLK, (c + 1) * RET_BLK)
                cc, ss = rot_c[rs, :], rot_s[rs, :]
                q = z_ref[rs, h * RET_QK_DIM:(h + 1) * RET_QK_DIM].astype(F32)
                k = z_ref[rs, QW + h * RET_QK_DIM:QW + (h + 1) * RET_QK_DIM].astype(F32)
                v = z_ref[rs, 2 * QW + h * RET_V_DIM:2 * QW + (h + 1) * RET_V_DIM]
                gt = z_ref[rs, 2 * QW + VW + h * RET_V_DIM:2 * QW + VW + (h + 1) * RET_V_DIM].astype(F32)
                qr = _rot(q, cc, ss)
                kr = _rot(k, cc, ss) * QK_SCALE
                s_old = state[h]
                sb = s_old.astype(BF16)
                st_ref[c, h] = sb
                s = _dot(qr.astype(BF16), kr.astype(BF16), NT) * dm_ref[h]
                o = _dot(s.astype(BF16), v, NN) + _dot((qr * qd_ref[h]).astype(BF16), sb, NN)
                state[h] = s_old * cd_ref[h] + _dot((kr * kd_ref[h]).astype(BF16), v, TN)
                mu = jnp.mean(o, axis=-1, keepdims=True)
                oc = o - mu
                rn = oc * lax.rsqrt(jnp.mean(oc * oc, axis=-1, keepdims=True) + EPS)
                silu = gt / (1.0 + jnp.exp(-gt))
                y_ref[rs, h * RET_V_DIM:(h + 1) * RET_V_DIM] = (silu * rn).astype(y_ref.dtype)

    return pl.pallas_call(
        body,
        grid=(nb,),
        in_specs=[pl.BlockSpec((tr, RET_W), lambda n: (n, 0))] + tab_specs,
        out_specs=[pl.BlockSpec((tr, VW), lambda n: (n, 0)),
                   pl.BlockSpec((cpb, RET_HEADS, RET_QK_DIM, RET_V_DIM), lambda n: (n, 0, 0, 0))],
        out_shape=[jax.ShapeDtypeStruct((T, Y_COLS), BF16),
                   jax.ShapeDtypeStruct((T // RET_BLK, RET_HEADS, RET_QK_DIM, RET_V_DIM), BF16)],
        scratch_shapes=[pltpu.VMEM((RET_HEADS, RET_QK_DIM, RET_V_DIM), F32)],
        compiler_params=_cp("arbitrary"),
        name=name,
    )(z_a, *tables)


def _ret_bwd(z_a, dy, states, tables, name="ret_bwd"):
    T = z_a.shape[0]
    tr = min(RET_TR, T)
    cpb = tr // RET_BLK
    nb = T // tr
    QW, VW = RET_HEADS * RET_QK_DIM, RET_HEADS * RET_V_DIM
    ix, tab_specs = _ret_specs(tr, True, nb)

    def body(z_ref, dy_ref, st_ref, tt_ref, rt_ref, dm_ref, qd_ref, kd_ref, cd_ref, dz_ref, dstate):
        @pl.when(pl.program_id(0) == 0)
        def _():
            dstate[...] = jnp.zeros_like(dstate)

        rot_c, rot_s = _rot_tables(tt_ref, rt_ref)
        for c in reversed(range(cpb)):
            for h in range(RET_HEADS):
                rs = slice(c * RET_BLK, (c + 1) * RET_BLK)
                cc, ss = rot_c[rs, :], rot_s[rs, :]
                q = z_ref[rs, h * RET_QK_DIM:(h + 1) * RET_QK_DIM].astype(F32)
                k = z_ref[rs, QW + h * RET_QK_DIM:QW + (h + 1) * RET_QK_DIM].astype(F32)
                v = z_ref[rs, 2 * QW + h * RET_V_DIM:2 * QW + (h + 1) * RET_V_DIM]
                gt = z_ref[rs, 2 * QW + VW + h * RET_V_DIM:2 * QW + VW + (h + 1) * RET_V_DIM].astype(F32)
                dyv = dy_ref[rs, h * RET_V_DIM:(h + 1) * RET_V_DIM].astype(F32)
                dmat, qd, kd = dm_ref[h], qd_ref[h], kd_ref[h]
                qr = _rot(q, cc, ss)
                kr = _rot(k, cc, ss) * QK_SCALE
                qrb, krb = qr.astype(BF16), kr.astype(BF16)
                sb = st_ref[c, h]
                sd = (_dot(qrb, krb, NT) * dmat).astype(BF16)
                qdb = (qr * qd).astype(BF16)
                kdb = (kr * kd).astype(BF16)
                o = _dot(sd, v, NN) + _dot(qdb, sb, NN)
                mu = jnp.mean(o, axis=-1, keepdims=True)
                oc = o - mu
                rstd = lax.rsqrt(jnp.mean(oc * oc, axis=-1, keepdims=True) + EPS)
                rn = oc * rstd
                sg = 1.0 / (1.0 + jnp.exp(-gt))
                dgt = dyv * rn * (sg * (1.0 + gt * (1.0 - sg)))
                drn = dyv * (gt * sg)
                do = rstd * (drn - jnp.mean(drn, axis=-1, keepdims=True) - rn * jnp.mean(drn * rn, axis=-1, keepdims=True))
                dob = do.astype(BF16)
                dsn = dstate[h]
                dsnb = dsn.astype(BF16)
                ds_raw = (_dot(dob, v, NT) * dmat).astype(BF16)
                dv = _dot(sd, dob, TN) + _dot(kdb, dsnb, NN)
                dqr = _dot(ds_raw, krb, NN) + qd * _dot(dob, sb, NT)
                dkr = (_dot(ds_raw, qrb, TN) + kd * _dot(v, dsnb, NT)) * QK_SCALE
                dstate[h] = dsn * cd_ref[h] + _dot(qdb, dob, TN)
                dq = dqr * cc + pltpu.roll(dqr * ss, RET_QK_DIM // 2, 1)
                dk = dkr * cc + pltpu.roll(dkr * ss, RET_QK_DIM // 2, 1)
                dz_ref[rs, h * RET_QK_DIM:(h + 1) * RET_QK_DIM] = dq.astype(dz_ref.dtype)
                dz_ref[rs, QW + h * RET_QK_DIM:QW + (h + 1) * RET_QK_DIM] = dk.astype(dz_ref.dtype)
                dz_ref[rs, 2 * QW + h * RET_V_DIM:2 * QW + (h + 1) * RET_V_DIM] = dv.astype(dz_ref.dtype)
                dz_ref[rs, 2 * QW + VW + h * RET_V_DIM:2 * QW + VW + (h + 1) * RET_V_DIM] = dgt.astype(dz_ref.dtype)

    return pl.pallas_call(
        body,
        grid=(nb,),
        in_specs=[pl.BlockSpec((tr, RET_W), lambda n: (ix(n), 0)),
                  pl.BlockSpec((tr, VW), lambda n: (ix(n), 0)),
                  pl.BlockSpec((cpb, RET_HEADS, RET_QK_DIM, RET_V_DIM), lambda n: (ix(n), 0, 0, 0))] + tab_specs,
        out_specs=pl.BlockSpec((tr, RET_W), lambda n: (ix(n), 0)),
        out_shape=jax.ShapeDtypeStruct((T, RET_W), BF16),
        scratch_shapes=[pltpu.VMEM((RET_HEADS, RET_QK_DIM, RET_V_DIM), F32)],
        compiler_params=_cp("arbitrary"),
        name=name,
    )(z_a, dy, states, *tables)


ATT_TQ = 256
ATT_CPB = ATT_TQ // CHUNK
ATT_SCALE = ATT_HEAD_DIM ** -0.5


ATT_WIN = 3 * ATT_TQ
ATT_NB = CHUNK * ATT_BAND


def _rel_index():
    i = np.arange(CHUNK)[:, None]
    j = np.arange(ATT_BAND)[None, :]
    rel = np.clip(i + ATT_PAST * CHUNK - j, -MAX_REL, MAX_REL) + MAX_REL
    return jnp.asarray(rel.reshape(1, ATT_NB).astype(np.int32))


def _split3(x):
    hi = x.astype(BF16)
    r1 = x - hi.astype(F32)
    mid = r1.astype(BF16)
    lo = (r1 - mid.astype(F32)).astype(BF16)
    return hi, mid, lo


REL_TILE = 4608


def _bias_expand(rel_bias, name="bias_expand"):
    H = rel_bias.shape[0]
    n = ATT_NB
    padded = jnp.pad(rel_bias, ((0, 0), (0, N_REL_PAD - N_REL)))

    def body(rb_ref, idx_ref, o_ref):
        onehot = (lax.broadcasted_iota(jnp.int32, (N_REL_PAD, REL_TILE), 0) == idx_ref[...]).astype(BF16)
        hi, mid, lo = _split3(rb_ref[...])
        o_ref[...] = _dot(hi, onehot, NN) + _dot(mid, onehot, NN) + _dot(lo, onehot, NN)

    out = pl.pallas_call(
        body,
        grid=(n // REL_TILE,),
        in_specs=[pl.BlockSpec((H, N_REL_PAD), lambda t: (0, 0)), pl.BlockSpec((1, REL_TILE), lambda t: (0, t))],
        out_specs=pl.BlockSpec((H, REL_TILE), lambda t: (0, t)),
        out_shape=jax.ShapeDtypeStruct((H, n), F32),
        compiler_params=_cp("parallel"),
        name=name,
    )(padded, _rel_index())
    return out.reshape(H, CHUNK, ATT_BAND)


def _bias_tile(band, name="bias_tile"):
    H = band.shape[0]
    padded = jnp.pad(band, ((0, 0), (0, 0), (0, ATT_WIN - ATT_BAND)), constant_values=NEG_INF)

    def body(b_ref, o_ref):
        b = b_ref[...]
        col = lax.broadcasted_iota(jnp.int32, (CHUNK, ATT_WIN), 1)
        keep = col >= (2 - pl.program_id(0)) * ATT_TQ
        for a in range(ATT_CPB):
            o_ref[a * CHUNK:(a + 1) * CHUNK, :] = jnp.where(keep, pltpu.roll(b, a * CHUNK, 1) if a else b, NEG_INF)

    return pl.pallas_call(
        body,
        grid=(3, H),
        in_specs=[pl.BlockSpec((None, CHUNK, ATT_WIN), lambda v, h: (h, 0, 0))],
        out_specs=pl.BlockSpec((None, None, ATT_TQ, ATT_WIN), lambda v, h: (v, h, 0, 0)),
        out_shape=jax.ShapeDtypeStruct((3, H, ATT_TQ, ATT_WIN), F32),
        compiler_params=_cp("parallel", "parallel"),
        name=name,
    )(padded)


def _bias_untile(dtile, name="bias_untile"):
    H = dtile.shape[0]

    def body(d_ref, o_ref):
        acc = d_ref[0:CHUNK, :]
        for a in range(1, ATT_CPB):
            acc = acc + pltpu.roll(d_ref[a * CHUNK:(a + 1) * CHUNK, :], ATT_WIN - a * CHUNK, 1)
        o_ref[...] = acc

    out = pl.pallas_call(
        body,
        grid=(H,),
        in_specs=[pl.BlockSpec((None, ATT_TQ, ATT_WIN), lambda h: (h, 0, 0))],
        out_specs=pl.BlockSpec((None, CHUNK, ATT_WIN), lambda h: (h, 0, 0)),
        out_shape=jax.ShapeDtypeStruct((H, CHUNK, ATT_WIN), F32),
        compiler_params=_cp("parallel"),
        name=name,
    )(dtile)
    return out[:, :, :ATT_BAND]


def _bias_reduce(dbias, name="bias_reduce"):
    H = dbias.shape[0]
    n = ATT_NB

    def body(db_ref, idx_ref, o_ref):
        @pl.when(pl.program_id(0) == 0)
        def _():
            o_ref[...] = jnp.zeros_like(o_ref)

        onehot = (lax.broadcasted_iota(jnp.int32, (N_REL_PAD, REL_TILE), 0) == idx_ref[...]).astype(BF16)
        hi, mid, lo = _split3(db_ref[...])
        o_ref[...] += _dot(hi, onehot, NT) + _dot(mid, onehot, NT) + _dot(lo, onehot, NT)

    out = pl.pallas_call(
        body,
        grid=(n // REL_TILE,),
        in_specs=[pl.BlockSpec((H, REL_TILE), lambda t: (0, t)), pl.BlockSpec((1, REL_TILE), lambda t: (0, t))],
        out_specs=pl.BlockSpec((H, N_REL_PAD), lambda t: (0, 0)),
        out_shape=jax.ShapeDtypeStruct((H, N_REL_PAD), F32),
        compiler_params=_cp("arbitrary"),
        name=name,
    )(dbias.reshape(H, n), _rel_index())
    return out[:, :N_REL]


def _att_probs(q, kwin, bias):
    s = _dot(q, kwin, NT) + bias
    e = jnp.exp(s - jnp.max(s, axis=-1, keepdims=True))
    return e * (1.0 / jnp.sum(e, axis=-1, keepdims=True))


ATT_PAIR = 2 * ATT_HEAD_DIM
ATT_NP = ATT_HEADS // 2
ATT_QW = ATT_HEADS * ATT_HEAD_DIM
Y_COLS = RET_HEADS * RET_V_DIM + ATT_QW


def _att_specs(tq, nq, clip_q, q_col0):
    cb = ATT_QW // ATT_PAIR
    qi = (lambda p, m: (jnp.minimum(m, nq - 1), q_col0 + p)) if clip_q else (lambda p, m: (m, q_col0 + p))
    q = pl.BlockSpec((tq, ATT_PAIR), qi)

    def win(col0):
        return [pl.BlockSpec((tq, ATT_PAIR), functools.partial(lambda p, m, back: (jnp.clip(m - back, 0, nq - 1), col0 + p), back=b))
                for b in (2, 1, 0)]

    bias = pl.BlockSpec((None, 2, ATT_TQ, ATT_WIN), lambda p, m: (jnp.minimum(m, 2), p, 0, 0))
    return q, win(cb), win(2 * cb), bias


def _head_masks(rows):
    lane = lax.broadcasted_iota(jnp.int32, (rows, ATT_PAIR), 1)
    return lane < ATT_HEAD_DIM


def _att_fwd(z_b, bias, y, comm=None, name="att_fwd"):
    T = z_b.shape[0]
    tq = ATT_TQ
    nq = T // tq
    qs, kwin, vwin, bs = _att_specs(tq, nq, False, 0)
    nc = comm.n if comm else 0
    total = ATT_NP * nq

    def body(*refs):
        q_ref, k0, k1, k2, v0, v1, v2, b_ref = refs[:8]
        cin = refs[9:9 + nc]
        o_ref = refs[9 + nc]
        cout = refs[10 + nc:10 + 2 * nc]
        csem = refs[10 + 2 * nc:]
        m = pl.program_id(1)
        step = pl.program_id(0) * nq + m
        _host_comm(comm, "early", step, total, cin, cout, csem)
        kw = jnp.concatenate([k0[...], k1[...], k2[...]], axis=0)
        vw = jnp.concatenate([v0[...], v1[...], v2[...]], axis=0)
        q2 = q_ref[...] * ATT_SCALE
        even = _head_masks(tq)
        outs = []
        for hh in range(2):
            qm = jnp.where(even if hh == 0 else ~even, q2, jnp.zeros_like(q2))
            p = _att_probs(qm, kw, b_ref[hh])
            outs.append(_dot(p.astype(BF16), vw, NN))
        o_ref[...] = jnp.where(even, outs[0], outs[1]).astype(o_ref.dtype)
        _host_comm(comm, "late", step, total, cin, cout, csem)

    y_cb = (Y_COLS - ATT_QW) // ATT_PAIR
    out = pl.pallas_call(
        body,
        grid=(ATT_NP, nq),
        in_specs=[qs] + kwin + vwin + [bs, ANY] + [ANY] * nc,
        out_specs=[pl.BlockSpec((tq, ATT_PAIR), lambda p, m: (m, y_cb + p))] + [ANY] * nc,
        out_shape=[jax.ShapeDtypeStruct((T, Y_COLS), BF16)] + (comm.out_shape if comm else []),
        scratch_shapes=comm.scratch if comm else [],
        input_output_aliases={8: 0},
        compiler_params=_cp("arbitrary", "arbitrary"),
        name=name,
    )(z_b, z_b, z_b, z_b, z_b, z_b, z_b, bias, y, *(comm.arrays if comm else []))
    return out[0], list(out[1:])


def _att_bwd(z_b, bias, dy, comm=None, name="att_bwd"):
    T = z_b.shape[0]
    tq = ATT_TQ
    nq = T // tq
    y_cb = (Y_COLS - ATT_QW) // ATT_PAIR
    qs, kwin, vwin, bs = _att_specs(tq, nq, True, 0)
    dos = _att_specs(tq, nq, True, y_cb)[0]
    kv_out = pl.BlockSpec((tq, ATT_PAIR), lambda p, m: (jnp.maximum(m - 2, 0), p))
    W3 = 3 * tq
    nc = comm.n if comm else 0
    total = ATT_NP * (nq + 2)

    def body(*refs):
        q_ref, k0, k1, k2, v0, v1, v2, b_ref, do_ref = refs[:9]
        cin = refs[9:9 + nc]
        dq_ref, dk_ref, dv_ref, db_ref = refs[9 + nc:13 + nc]
        cout = refs[13 + nc:13 + 2 * nc]
        dkc, dvc, dkw, dvw = refs[13 + 2 * nc:17 + 2 * nc]
        csem = refs[17 + 2 * nc:]
        m = pl.program_id(1)
        step = pl.program_id(0) * (nq + 2) + m
        _host_comm(comm, "early", step, total, cin, cout, csem)

        @pl.when(m == 0)
        def _():
            dkc[...] = jnp.zeros_like(dkc)
            dvc[...] = jnp.zeros_like(dvc)
            db_ref[...] = jnp.zeros_like(db_ref)

        @pl.when(m >= nq)
        def _():
            dkw[...] = jnp.zeros_like(dkw)
            dvw[...] = jnp.zeros_like(dvw)

        @pl.when(m < nq)
        def _():
            kw = jnp.concatenate([k0[...], k1[...], k2[...]], axis=0)
            vw = jnp.concatenate([v0[...], v1[...], v2[...]], axis=0)
            q2, do2 = q_ref[...] * ATT_SCALE, do_ref[...]
            even = _head_masks(tq)
            dqs, dks, dvs = [], [], []
            for hh in range(2):
                mine = even if hh == 0 else ~even
                p = _att_probs(jnp.where(mine, q2, jnp.zeros_like(q2)), kw, b_ref[hh])
                dp = _dot(jnp.where(mine, do2, jnp.zeros_like(do2)), vw, NT)
                ds = p * (dp - jnp.sum(dp * p, axis=-1, keepdims=True))
                db_ref[hh] += ds
                dsb = ds.astype(BF16)
                dqs.append(_dot(dsb, kw, NN))
                dks.append(_dot(dsb, q2, TN))
                dvs.append(_dot(p.astype(BF16), do2, TN))
            even_w = _head_masks(W3)
            dq_ref[...] = (jnp.where(even, dqs[0], dqs[1]) * ATT_SCALE).astype(dq_ref.dtype)
            dkw[...] = jnp.where(even_w, dks[0], dks[1])
            dvw[...] = jnp.where(even_w, dvs[0], dvs[1])

        dk_ref[...] = (dkc[0:tq, :] + dkw[0:tq, :]).astype(dk_ref.dtype)
        dv_ref[...] = (dvc[0:tq, :] + dvw[0:tq, :]).astype(dv_ref.dtype)
        dkc[0:tq, :] = dkc[tq:2 * tq, :] + dkw[tq:2 * tq, :]
        dvc[0:tq, :] = dvc[tq:2 * tq, :] + dvw[tq:2 * tq, :]
        dkc[tq:2 * tq, :] = dkw[2 * tq:W3, :]
        dvc[tq:2 * tq, :] = dvw[2 * tq:W3, :]
        _host_comm(comm, "late", step, total, cin, cout, csem)

    qo = pl.BlockSpec((tq, ATT_PAIR), lambda p, m: (jnp.minimum(m, nq - 1), p))
    dbs = pl.BlockSpec((2, ATT_TQ, ATT_WIN), lambda p, m: (p, 0, 0))
    hd = jax.ShapeDtypeStruct((T, ATT_QW), BF16)
    out = pl.pallas_call(
        body,
        grid=(ATT_NP, nq + 2),
        in_specs=[qs] + kwin + vwin + [bs, dos] + [ANY] * nc,
        out_specs=[qo, kv_out, kv_out, dbs] + [ANY] * nc,
        out_shape=[hd, hd, hd, jax.ShapeDtypeStruct((ATT_HEADS, ATT_TQ, ATT_WIN), F32)] + (comm.out_shape if comm else []),
        scratch_shapes=[pltpu.VMEM((2 * tq, ATT_PAIR), F32), pltpu.VMEM((2 * tq, ATT_PAIR), F32),
                        pltpu.VMEM((W3, ATT_PAIR), F32), pltpu.VMEM((W3, ATT_PAIR), F32)] + (comm.scratch if comm else []),
        compiler_params=_cp("arbitrary", "arbitrary"),
        name=name,
    )(z_b, z_b, z_b, z_b, z_b, z_b, z_b, bias, dy, *(comm.arrays if comm else []))
    return out[0], out[1], out[2], out[3], list(out[4:])


REST = ["ab_out", "c_in_t", "c_out", "up_t0", "up_t1", "down0", "down1"]


def _local_step(x, target, wt, small, rest_shards=None, overlap=False):
    T = x.shape[0]
    Fh = FFN_HIDDEN
    tables = _ret_tables(T)
    gw, gs, recv = {}, {}, {}
    wt = dict(wt)

    hn0 = _rms_fwd(x, small["attn_norm_g"][0], name="rms_fwd")
    z_a = _mm(hn0, wt["ab_in_t"][:RET_W], "nt", BF16, name="mm_ab_in_a")
    z_b = _mm(hn0, wt["ab_in_t"][RET_W:], "nt", BF16, name="mm_ab_in_b")
    y, states = _ret_fwd(z_a, tables)
    bias = _bias_tile(_bias_expand(small["rel_bias"]))
    y, rest = _att_fwd(z_b, bias, y, comm=_Comm("gather", rest_shards) if rest_shards is not None else None)
    if rest_shards is not None:
        full = dict(zip(REST, rest))
        wt.update(ab_out=full["ab_out"], c_in_t=full["c_in_t"], c_out=full["c_out"],
                  up_t=[full["up_t0"], full["up_t1"]], down=[full["down0"], full["down1"]])
    h1, hf0 = _mm_rows(y, wt["ab_out"], x, "mm_ab_out", norm_g=small["ffn_norm_g"][0])

    def ffn_fwd(h, hf, layer, next_g):
        zg, zu, a = _ffn_up_mid(hf, wt["up_t"][layer], small["conv_w"][layer], small["conv_b"][layer][None, :])
        if next_g is None:
            return _mm_rows(a, wt["down"][layer], h, "mm_down_loss", loss=(target, small["final_g"])), (hf, zg, zu, a)
        h_out, hn_next = _mm_rows(a, wt["down"][layer], h, "mm_down", norm_g=next_g)
        return h_out, hn_next, (hf, zg, zu, a)

    def ffn_bwd(dh_out, h, layer, saved, exchange=None):
        hf, zg, zu, a = saved
        da = _mm(dh_out, wt["down"][layer], "nt", BF16, name="mm_d_a")
        d_down = _mm(a, dh_out, "tn", BF16, name="mm_dw_down")
        comm = _Comm("exchange", [gw[k] for k in exchange]) if exchange else None
        dzg, dzu, dcw, dcb, got = _ffn_mid_bwd(zg, zu, da, small["conv_w"][layer], small["conv_b"][layer][None, :], comm=comm)
        recv.update(zip(exchange or [], got))
        dhf = _mm(dzg, wt["up_t"][layer][:Fh], "nn", F32, name="mm_d_hf")
        dh, dg = _mm_rows(dzu, wt["up_t"][layer][Fh:], dhf, "mm_d_hf_norm", bwd=(h, small["ffn_norm_g"][layer], dh_out))
        d_up = jnp.concatenate([_mm(dzg, hf, "tn", BF16, name="mm_dw_up"), _mm(dzu, hf, "tn", BF16, name="mm_dw_up")], axis=0)
        return dh, dg, d_up, d_down, dcw, dcb

    h2, hn1, ffn0 = ffn_fwd(h1, hf0, 0, small["attn_norm_g"][1])

    zz = _mm(hn1, wt["c_in_t"], "nt", BF16, name="mm_c_in")
    ys = _sgu_fwd(zz, small["ln_g"], small["ln_b"], small["w_s"], small["b_s"])
    h3, hf1 = _mm_rows(ys, wt["c_out"], h2, "mm_c_out", norm_g=small["ffn_norm_g"][1])
    (loss_vec, dh4, gs["final_g"]), ffn1 = ffn_fwd(h3, hf1, 1, None)

    dh3, dgf1, gw["up_t1"], gw["down1"], dcw1, dcb1 = ffn_bwd(dh4, h3, 1, ffn1)
    dys = _mm(dh3, wt["c_out"], "nt", BF16, name="mm_d_ys")
    gw["c_out"] = _mm(ys, dh3, "tn", BF16, name="mm_dw_c_out")
    dzz, gs["w_s"], dbs, dlg, dlb = _sgu_bwd(zz, dys, small["ln_g"], small["ln_b"], small["w_s"], small["b_s"])
    gs["b_s"], gs["ln_g"], gs["ln_b"] = dbs[:, :, 0], dlg[0], dlb[0]
    dh2, dga1 = _mm_rows(dzz, wt["c_in_t"], None, "mm_d_hn1_norm", bwd=(h2, small["attn_norm_g"][1], dh3))
    gw["c_in_t"] = _mm(dzz, hn1, "tn", BF16, name="mm_dw_c_in")

    dh1, dgf0, gw["up_t0"], gw["down0"], dcw0, dcb0 = ffn_bwd(
        dh2, h1, 0, ffn0, exchange=["c_in_t", "c_out", "up_t1", "down1"] if overlap else None)

    dy = _mm(dh1, wt["ab_out"], "nt", BF16, name="mm_d_y")
    gw["ab_out"] = _mm(y, dh1, "tn", BF16, name="mm_dw_ab_out")
    dz_a = _ret_bwd(z_a, dy, states, tables)
    late = ["ab_out", "up_t0", "down0"] if overlap else []
    dq, dk, dv, dbias, got = _att_bwd(z_b, bias, dy, comm=_Comm("exchange", [gw[k] for k in late]) if late else None)
    recv.update(zip(late, got))
    dz_b = jnp.concatenate([dq, dk, dv], axis=1)
    gs["rel_bias"] = _bias_reduce(_bias_untile(dbias))
    gw["ab_in_t"] = jnp.concatenate([_mm(dz_a, hn0, "tn", BF16, name="mm_dw_ab_in_a"),
                                     _mm(dz_b, hn0, "tn", BF16, name="mm_dw_ab_in_b")], axis=0)
    if overlap:
        dhn0, got = _mm(dz_a, wt["ab_in_t"][:RET_W], "nn", F32, name="mm_d_hn0", comm=_Comm("exchange", [gw["ab_in_t"]]))
        recv["ab_in_t"] = got[0]
    else:
        dhn0 = _mm(dz_a, wt["ab_in_t"][:RET_W], "nn", F32, name="mm_d_hn0")
    grad_x, dga0 = _mm_rows(dz_b, wt["ab_in_t"][RET_W:], dhn0, "mm_d_hn0_norm", bwd=(x, small["attn_norm_g"][0], dh1))

    gs["attn_norm_g"] = jnp.concatenate([dga0, dga1], axis=0)
    gs["ffn_norm_g"] = jnp.concatenate([dgf0, dgf1], axis=0)
    gs["conv_w"] = jnp.stack([dcw0, dcw1])
    gs["conv_b"] = jnp.concatenate([dcb0, dcb1], axis=0)
    gs["final_g"] = gs["final_g"][0]
    return loss_vec[0, 0], grad_x, gw, gs, recv


MESH_ID = pl.DeviceIdType.MESH
ANY = pl.BlockSpec(memory_space=pl.ANY)


def _my_place():
    return lax.axis_index("x"), lax.axis_index("y"), lax.axis_index("c")


class _Comm:
    def __init__(self, kind, arrays):
        self.kind, self.arrays, self.n = kind, list(arrays), len(arrays)
        if kind == "gather":
            self.out_shape = [jax.ShapeDtypeStruct((N_DEV * s.shape[0], s.shape[1]), s.dtype) for s in arrays]
        else:
            self.out_shape = [jax.ShapeDtypeStruct((N_DEV, g.shape[0] // N_DEV, g.shape[1]), g.dtype) for g in arrays]
        n = self.n
        self.scratch = [pltpu.SemaphoreType.DMA((n, 7)), pltpu.SemaphoreType.DMA((n, 7)), pltpu.SemaphoreType.DMA((n,))]

    def phase(self, ph, in_refs, out_refs, sems):
        (self._gather if self.kind == "gather" else self._exchange)(ph, in_refs, out_refs, sems)

    def _gather(self, ph, x_refs, o_refs, sems):
        n = self.n
        send_sems, recv_sems, local_sems = sems
        x, y, c = _my_place()
        me, sibling = (x, y, c), (x, y, 1 - c)
        chips = [(1 - x, y), (x, 1 - y), (1 - x, 1 - y)]

        def rows(a, place):
            m = x_refs[a].shape[0]
            px, py, pc = place
            return o_refs[a].at[pl.ds((4 * px + 2 * py + pc) * m, m), :]

        def copy(a, k, block, to, own=False):
            return pltpu.make_async_remote_copy(
                src_ref=x_refs[a] if own else rows(a, block), dst_ref=rows(a, block),
                send_sem=send_sems.at[a, k], recv_sem=recv_sems.at[a, k], device_id=to, device_id_type=MESH_ID)

        def mine():
            return [pltpu.make_async_copy(x_refs[a], rows(a, me), local_sems.at[a]) for a in range(n)]

        def first():
            out = []
            for a in range(n):
                out.append(copy(a, 0, me, sibling, own=True))
                out += [copy(a, 1 + j, me, (*chip, c), own=True) for j, chip in enumerate(chips)]
            return out

        def passed():
            return [copy(a, 4 + j, (*chip, c), sibling) for j, chip in enumerate(chips) for a in range(n)]

        if ph == 0:
            for cp in mine() + first():
                cp.start()
        elif ph == 1:
            fw = passed()
            for j, chip in enumerate(chips):
                for a in range(n):
                    copy(a, 1 + j, (*chip, c), me).wait_recv()
                    fw[j * n + a].start()
        else:
            for a in range(n):
                copy(a, 0, sibling, me).wait_recv()
                for j, chip in enumerate(chips):
                    copy(a, 4 + j, (*chip, 1 - c), me).wait_recv()
            for cp in first() + passed():
                cp.wait_send()
            for cp in mine():
                cp.wait()

    def _exchange(self, ph, g_refs, o_refs, sems):
        n = self.n
        send_sems, recv_sems, local_sems = sems
        x, y, c = _my_place()
        me = 4 * x + 2 * y + c
        peers = [(x ^ ((k >> 2) & 1), y ^ ((k >> 1) & 1), c ^ (k & 1)) for k in range(1, N_DEV)]

        def block(a, idx):
            m = g_refs[a].shape[0] // N_DEV
            return g_refs[a].at[pl.ds(idx * m, m), :]

        def copy(a, k, slot):
            px, py, pc = peers[k]
            return pltpu.make_async_remote_copy(
                src_ref=block(a, 4 * px + 2 * py + pc), dst_ref=o_refs[a].at[slot],
                send_sem=send_sems.at[a, k], recv_sem=recv_sems.at[a, k], device_id=peers[k], device_id_type=MESH_ID)

        if ph == 1:
            return
        mine = [pltpu.make_async_copy(block(a, me), o_refs[a].at[me], local_sems.at[a]) for a in range(n)]
        sends = [copy(a, k, me) for k in range(N_DEV - 1) for a in range(n)]
        if ph == 0:
            for cp in mine + sends:
                cp.start()
        else:
            for k in range(N_DEV - 1):
                px, py, pc = peers[k]
                for a in range(n):
                    copy(a, k, 4 * px + 2 * py + pc).wait_recv()
            for cp in sends:
                cp.wait_send()
            for cp in mine:
                cp.wait()


def _comm_call(comm, name):
    n = comm.n

    def body(*refs):
        for ph in range(3):
            comm.phase(ph, refs[:n], refs[n:2 * n], refs[2 * n:])

    return pl.pallas_call(
        body, out_shape=comm.out_shape, in_specs=[ANY] * n, out_specs=[ANY] * n, scratch_shapes=comm.scratch, name=name,
    )(*comm.arrays)


def _host_comm(comm, when, step, total, cin, cout, csem):
    if comm is None:
        return
    sched = {0: 0, 1: (3 * total) // 4, 2: total - 1}
    phases = (0, 1) if when == "early" else (2,)
    for ph in phases:
        if ph == 1 and comm.kind == "exchange":
            continue

        @pl.when(step == sched[ph])
        def _(ph=ph):
            comm.phase(ph, cin, cout, csem)


def _all_gather(shards, name="all_gather"):
    return _comm_call(_Comm("gather", shards), name)


def _row_tile(r, target=256):
    best = None
    for t in range(8, min(r, target) + 1, 8):
        if r % t == 0:
            best = t
    return best if best is not None else r


def _sum8(parts, name="sum8"):
    _, M, N = parts.shape
    tr = _row_tile(M, 128)

    def body(p_ref, o_ref):
        acc = p_ref[0].astype(F32)
        for d in range(1, N_DEV):
            acc = acc + p_ref[d].astype(F32)
        o_ref[...] = acc

    return pl.pallas_call(
        body,
        grid=(M // tr,),
        in_specs=[pl.BlockSpec((N_DEV, tr, N), lambda i: (0, i, 0))],
        out_specs=pl.BlockSpec((tr, N), lambda i: (i, 0)),
        out_shape=jax.ShapeDtypeStruct((M, N), F32),
        compiler_params=_cp("parallel"),
        name=name,
    )(parts)


def _adamw(w, g, m, v, name="adamw"):
    shape = w.shape
    if w.ndim == 1:
        r2 = (1, shape[0])
    else:
        r2 = (int(np.prod(shape[:-1])), shape[-1])
    R, C = r2
    tr = _row_tile(R)
    bc1 = 1.0 - ADAM_B1 ** ADAM_STEP
    bc2 = 1.0 - ADAM_B2 ** ADAM_STEP

    def body(w_ref, g_ref, m_ref, v_ref, d_ref, nm_ref, nv_ref):
        gv = g_ref[...]
        nm = ADAM_B1 * m_ref[...] + (1.0 - ADAM_B1) * gv
        nv = ADAM_B2 * v_ref[...] + (1.0 - ADAM_B2) * (gv * gv)
        d_ref[...] = -ADAM_LR * ((nm / bc1) / (jnp.sqrt(nv / bc2) + ADAM_EPS) + ADAM_WD * w_ref[...])
        nm_ref[...] = nm
        nv_ref[...] = nv

    spec = pl.BlockSpec((tr, C), lambda i: (i, 0))
    out = pl.pallas_call(
        body,
        grid=(R // tr,),
        in_specs=[spec] * 4,
        out_specs=[spec] * 3,
        out_shape=[jax.ShapeDtypeStruct(r2, F32)] * 3,
        compiler_params=_cp("parallel"),
        name=name,
    )(w.reshape(r2), g.reshape(r2), m.reshape(r2), v.reshape(r2))
    return [o.reshape(shape) for o in out]


WEIGHTS = ['attn_norm_g', 'ffn_norm_g', 'ab_w_in', 'ab_w_out', 'ab_rel_bias', 'c_w_in', 'c_ln_g', 'c_ln_b', 'c_w_s', 'c_b_s',
           'c_w_out', 'ffn_w_up', 'ffn_conv_w', 'ffn_conv_b', 'ffn_w_down', 'final_norm_g']
SMALL_ORDER = ["attn_norm_g", "ffn_norm_g", "rel_bias", "ln_g", "ln_b", "w_s", "b_s", "conv_w", "conv_b", "final_g"]
PACK_ROW = 1024


def _pack(arrs):
    flat = jnp.concatenate([a.reshape(-1) for a in arrs])
    n = flat.shape[0]
    padded = -(-n // PACK_ROW) * PACK_ROW
    return jnp.pad(flat, (0, padded - n)).reshape(padded // 128, 128)


def _unpack(flat, shapes):
    out, off = [], 0
    for s in shapes:
        n = int(np.prod(s))
        out.append(flat[off:off + n].reshape(s))
        off += n
    return out


def _step(P):
    x, target = P["x"][0], P["loss_target"][0]
    me = 4 * lax.axis_index("x") + 2 * lax.axis_index("y") + lax.axis_index("c")
    n_up = P["ffn_w_up"].shape[0]
    Fc = P["ffn_conv_w"].shape[-1]
    Lc = P["c_ln_g"].shape[-1]

    full = _all_gather([P["ab_w_in"][0].T.astype(BF16), _pack([P["ffn_conv_w"], P["c_ln_g"], P["c_ln_b"]])],
                       name="gather_first")
    wt = {"ab_in_t": full[0]}
    rest = {"ab_out": P["ab_w_out"][0], "c_in_t": P["c_w_in"][0].T, "c_out": P["c_w_out"][0],
            "up_t0": P["ffn_w_up"][0].T, "up_t1": P["ffn_w_up"][1].T, "down0": P["ffn_w_down"][0], "down1": P["ffn_w_down"][1]}
    rest_shards = [rest[k].astype(BF16) for k in REST]
    sm = full[-1].reshape(N_DEV, -1)
    conv_w = sm[:, :n_up * 3 * Fc].reshape(N_DEV, n_up, 3, Fc).transpose(1, 2, 0, 3).reshape(n_up, 3, N_DEV * Fc)
    off = n_up * 3 * Fc
    ln_g = sm[:, off:off + Lc].reshape(N_DEV * Lc)
    ln_b = sm[:, off + Lc:off + 2 * Lc].reshape(N_DEV * Lc)
    small = {"attn_norm_g": P["attn_norm_g"], "ffn_norm_g": P["ffn_norm_g"], "rel_bias": P["ab_rel_bias"][0],
             "ln_g": ln_g, "ln_b": ln_b, "w_s": P["c_w_s"][0], "b_s": P["c_b_s"][0], "conv_w": conv_w,
             "conv_b": P["ffn_conv_b"], "final_g": P["final_norm_g"]}

    loss_part, grad_x, gw, gs, recv = _local_step(x, target, wt, small, rest_shards=rest_shards, overlap=True)
    loss = lax.psum(loss_part, ("x", "y", "c"))

    s8 ={k: _sum8(recv[k], name="sum8") for k in ["ab_in_t"] + REST}
    g_big = {"ab_w_in": s8["ab_in_t"].T[None], "ab_w_out": s8["ab_out"][None], "c_w_in": s8["c_in_t"].T[None],
             "c_w_out": s8["c_out"][None], "ffn_w_up": jnp.stack([s8["up_t0"].T, s8["up_t1"].T]),
             "ffn_w_down": jnp.stack([s8["down0"], s8["down1"]])}

    packed = _pack([gs[k] for k in SMALL_ORDER])
    gathered = _all_gather([packed], name="gather_small_grads")[0]
    tot = _sum8(gathered.reshape(N_DEV, packed.shape[0], 128), name="sum8_small").reshape(-1)
    gsm = dict(zip(SMALL_ORDER, _unpack(tot, [gs[k].shape for k in SMALL_ORDER])))
    grads = dict(g_big)
    grads["attn_norm_g"] = gsm["attn_norm_g"]
    grads["ffn_norm_g"] = gsm["ffn_norm_g"]
    grads["ab_rel_bias"] = gsm["rel_bias"][None]
    grads["c_ln_g"] = lax.dynamic_slice(gsm["ln_g"], (me * Lc,), (Lc,))[None]
    grads["c_ln_b"] = lax.dynamic_slice(gsm["ln_b"], (me * Lc,), (Lc,))[None]
    grads["c_w_s"] = gsm["w_s"][None]
    grads["c_b_s"] = gsm["b_s"][None]
    grads["ffn_conv_w"] = lax.dynamic_slice(gsm["conv_w"], (0, 0, me * Fc), (n_up, 3, Fc))
    grads["ffn_conv_b"] = gsm["conv_b"]
    grads["final_norm_g"] = gsm["final_g"]

    delta, new_m, new_v = {}, {}, {}
    for k in WEIGHTS:
        delta[k], new_m[k], new_v[k] = _adamw(P[k], grads[k], P["m_" + k], P["v_" + k], name="adamw")
    return (loss, grad_x[None], *[grads[k] for k in WEIGHTS], *[delta[k] for k in WEIGHTS],
            *[new_m[k] for k in WEIGHTS], *[new_v[k] for k in WEIGHTS])


def kernel(x, attn_norm_g, ffn_norm_g, ab_w_in, ab_w_out, ab_rel_bias, c_w_in, c_ln_g, c_ln_b, c_w_s, c_b_s, c_w_out, ffn_w_up, ffn_conv_w, ffn_conv_b, ffn_w_down, final_norm_g, loss_target, m_attn_norm_g, m_ffn_norm_g, m_ab_w_in, m_ab_w_out, m_ab_rel_bias, m_c_w_in, m_c_ln_g, m_c_ln_b, m_c_w_s, m_c_b_s, m_c_w_out, m_ffn_w_up, m_ffn_conv_w, m_ffn_conv_b, m_ffn_w_down, m_final_norm_g, v_attn_norm_g, v_ffn_norm_g, v_ab_w_in, v_ab_w_out, v_ab_rel_bias, v_c_w_in, v_c_ln_g, v_c_ln_b, v_c_w_s, v_c_b_s, v_c_w_out, v_ffn_w_up, v_ffn_conv_w, v_ffn_conv_b, v_ffn_w_down, v_final_norm_g):
    return _step(dict(locals()))
```

```python
import functools

import numpy as np
import jax
import jax.numpy as jnp
from jax import lax
from jax.experimental import pallas as pl
from jax.experimental.pallas import tpu as pltpu

F32 = jnp.float32
BF16 = jnp.bfloat16

D_MODEL = 1024
CHUNK = 64
EPS = 1e-6
NEG_INF = -1e30
RET_HEADS = 4
RET_QK_DIM = 128
RET_V_DIM = 256
ATT_HEADS = 8
ATT_HEAD_DIM = 64
ATT_PAST = 8
ATT_BAND = (ATT_PAST + 1) * CHUNK
MAX_REL = 128
N_REL = 2 * MAX_REL + 1
N_REL_PAD = 384
SGU_BLOCK = 128
SGU_GROUPS = 8
SGU_WIDTH = 2048
SGU_GW = SGU_WIDTH // SGU_GROUPS
FFN_HIDDEN = 2816
RET_W = 2 * RET_HEADS * RET_QK_DIM + 2 * RET_HEADS * RET_V_DIM
ATT_W = 3 * ATT_HEADS * ATT_HEAD_DIM
N_DEV = 8

ADAM_LR = 0.001
ADAM_B1 = 0.9
ADAM_B2 = 0.999
ADAM_EPS = 1e-08
ADAM_WD = 0.01
ADAM_STEP = 10

VMEM_LIMIT = 52 * 1024 * 1024


def _cp(*sem):
    return pltpu.CompilerParams(dimension_semantics=sem if sem else None, vmem_limit_bytes=VMEM_LIMIT)


def _tile(n, target):
    if n <= target:
        return n
    best = None
    for t in range(128, target + 1, 128):
        if n % t == 0:
            best = t
    assert best is not None, (n, target)
    return best


def _gelu(x):
    c = 0.7978845608028654
    return 0.5 * x * (1.0 + jnp.tanh(c * (x + 0.044715 * x * x * x)))


def _gelu_and_grad(x):
    c = 0.7978845608028654
    x2 = x * x
    t = jnp.tanh(c * (x + 0.044715 * x * x2))
    cdf = 0.5 * (1.0 + t)
    grad = cdf + x * (0.5 * c) * (1.0 - t * t) * (1.0 + 3.0 * 0.044715 * x2)
    return x * cdf, grad


def _dot(a, b, dims):
    return lax.dot_general(a, b, (dims, ((), ())), preferred_element_type=F32)


NN = ((1,), (0,))
NT = ((1,), (1,))
TN = ((0,), (0,))


def _mm(a, b, mode, out_dtype, res=None, name="mm", tm_t=None, tn_t=None, tk_t=None, comm=None):
    if mode == "nt":
        (M, K), N = a.shape, b.shape[0]
        dm, dn, dk = (512, 2816, K) if N <= 2816 else (1024, 1024, K)
    elif mode == "nn":
        (M, K), N = a.shape, b.shape[1]
        dm, dn, dk = (1024 if K <= 3072 else 512), 1024, K
    else:
        (K, M), N = a.shape, b.shape[1]
        dm, dn, dk = 1536, 1024, 2048
    tm, tn, tk = _tile(M, tm_t or dm), _tile(N, tn_t or dn), _tile(K, tk_t or dk)
    nk = K // tk
    dims = {"nt": NT, "nn": NN, "tn": TN}[mode]
    a_spec = pl.BlockSpec((tk, tm), lambda i, j, k: (k, i)) if mode == "tn" else pl.BlockSpec((tm, tk), lambda i, j, k: (i, k))
    b_spec = pl.BlockSpec((tn, tk), lambda i, j, k: (j, k)) if mode == "nt" else pl.BlockSpec((tk, tn), lambda i, j, k: (k, j))
    o_spec = pl.BlockSpec((tm, tn), lambda i, j, k: (i, j))
    has_res = res is not None
    nc = comm.n if comm else 0
    n_in = 3 if has_res else 2
    gi, gj = M // tm, N // tn

    def body(*refs):
        a_ref, b_ref = refs[:2]
        r_ref = refs[2] if has_res else None
        cin = refs[n_in:n_in + nc]
        o_ref = refs[n_in + nc]
        cout = refs[n_in + nc + 1:n_in + 2 * nc + 1]
        scratch = refs[n_in + 2 * nc + 1:]
        csem = scratch[1:] if nk > 1 else scratch
        step = (pl.program_id(0) * gj + pl.program_id(1)) * nk + pl.program_id(2)
        _host_comm(comm, "early", step, gi * gj * nk, cin, cout, csem)
        p = _dot(a_ref[...].astype(BF16), b_ref[...].astype(BF16), dims)
        if nk == 1:
            if has_res:
                p = p + r_ref[...]
            o_ref[...] = p.astype(out_dtype)
        else:
            acc = scratch[0]
            k = pl.program_id(2)

            @pl.when(k == 0)
            def _():
                acc[...] = p

            @pl.when(k > 0)
            def _():
                acc[...] += p

            @pl.when(k == nk - 1)
            def _():
                t = acc[...]
                if has_res:
                    t = t + r_ref[...]
                o_ref[...] = t.astype(out_dtype)
        _host_comm(comm, "late", step, gi * gj * nk, cin, cout, csem)

    in_specs = [a_spec, b_spec] + ([o_spec] if has_res else []) + [ANY] * nc
    args = (a, b) + ((res,) if has_res else ()) + tuple(comm.arrays if comm else ())
    out = pl.pallas_call(
        body,
        grid=(gi, gj, nk),
        in_specs=in_specs,
        out_specs=[o_spec] + [ANY] * nc,
        out_shape=[jax.ShapeDtypeStruct((M, N), out_dtype)] + (comm.out_shape if comm else []),
        scratch_shapes=([pltpu.VMEM((tm, tn), F32)] if nk > 1 else []) + (comm.scratch if comm else []),
        compiler_params=_cp("arbitrary", "arbitrary", "arbitrary") if comm else _cp("parallel", "parallel", "arbitrary"),
        name=name,
    )(*args)
    return (out[0], list(out[1:])) if comm else out[0]


def _mm_rows(a, b, res, name, norm_g=None, bwd=None, loss=None, tm=512):
    M, K = a.shape
    Dm = b.shape[1]
    tm = min(tm, M)
    row = pl.BlockSpec((tm, Dm), lambda i: (i, 0))
    vec = pl.BlockSpec((1, Dm), lambda i: (0, 0))
    a_spec = pl.BlockSpec((tm, K), lambda i: (i, 0))
    b_spec = pl.BlockSpec((K, Dm), lambda i: (0, 0))

    if loss is not None:
        target, g = loss

        def body(a_ref, b_ref, r_ref, g_ref, t_ref, loss_ref, dh_ref, dg_ref):
            @pl.when(pl.program_id(0) == 0)
            def _():
                loss_ref[...] = jnp.zeros_like(loss_ref)
                dg_ref[...] = jnp.zeros_like(dg_ref)

            x = _dot(a_ref[...].astype(BF16), b_ref[...].astype(BF16), NN) + r_ref[...]
            gv = g_ref[...]
            r = lax.rsqrt(jnp.mean(x * x, axis=-1, keepdims=True) + EPS)
            xhat = x * r
            e = xhat * gv - t_ref[...]
            loss_ref[...] += jnp.full((1, 128), 0.5 / Dm, F32) * jnp.sum(e * e)
            dy = e * (1.0 / Dm)
            dg_ref[...] += jnp.sum(dy * xhat, axis=0, keepdims=True)
            dx = dy * gv
            m = jnp.mean(dx * xhat, axis=-1, keepdims=True)
            dh_ref[...] = r * (dx - xhat * m)

        return pl.pallas_call(
            body, grid=(M // tm,), in_specs=[a_spec, b_spec, row, vec, row],
            out_specs=[pl.BlockSpec((1, 128), lambda i: (0, 0)), row, vec],
            out_shape=[jax.ShapeDtypeStruct((1, 128), F32), jax.ShapeDtypeStruct((M, Dm), F32), jax.ShapeDtypeStruct((1, Dm), F32)],
            compiler_params=_cp("arbitrary"), name=name,
        )(a, b, res, g.reshape(1, Dm), target)

    if bwd is None:
        def body(a_ref, b_ref, r_ref, g_ref, o_ref, n_ref):
            t = _dot(a_ref[...].astype(BF16), b_ref[...].astype(BF16), NN) + r_ref[...]
            o_ref[...] = t
            r = lax.rsqrt(jnp.mean(t * t, axis=-1, keepdims=True) + EPS)
            n_ref[...] = (t * r * g_ref[...]).astype(n_ref.dtype)

        return pl.pallas_call(
            body, grid=(M // tm,), in_specs=[a_spec, b_spec, row, vec], out_specs=[row, row],
            out_shape=[jax.ShapeDtypeStruct((M, Dm), F32), jax.ShapeDtypeStruct((M, Dm), BF16)],
            compiler_params=_cp("parallel"), name=name,
        )(a, b, res, norm_g.reshape(1, Dm))

    h, g, dres = bwd
    has_res = res is not None

    def body(*refs):
        a_ref, b_ref = refs[:2]
        h_ref, g_ref, dres_ref, dh_ref, dg_ref = refs[-5:]

        @pl.when(pl.program_id(0) == 0)
        def _():
            dg_ref[...] = jnp.zeros_like(dg_ref)

        d = _dot(a_ref[...].astype(BF16), b_ref[...].astype(BF16), NN)
        if has_res:
            d = d + refs[2][...]
        x = h_ref[...]
        r = lax.rsqrt(jnp.mean(x * x, axis=-1, keepdims=True) + EPS)
        xhat = x * r
        dg_ref[...] += jnp.sum(d * xhat, axis=0, keepdims=True)
        dx = d * g_ref[...]
        m = jnp.mean(dx * xhat, axis=-1, keepdims=True)
        dh_ref[...] = dres_ref[...] + r * (dx - xhat * m)

    return pl.pallas_call(
        body, grid=(M // tm,), in_specs=[a_spec, b_spec] + ([row] if has_res else []) + [row, vec, row], out_specs=[row, vec],
        out_shape=[jax.ShapeDtypeStruct((M, Dm), F32), jax.ShapeDtypeStruct((1, Dm), F32)],
        compiler_params=_cp("arbitrary"), name=name,
    )(a, b, *((res,) if has_res else ()), h, g.reshape(1, Dm), dres)


def _rms_fwd(h, g, comm=None, name="rms_fwd", tm=512):
    T, Dm = h.shape
    tm = min(tm, T)
    nc = comm.n if comm else 0
    ni = T // tm

    def body(*refs):
        h_ref, g_ref = refs[:2]
        cin, o_ref, cout, csem = refs[2:2 + nc], refs[2 + nc], refs[3 + nc:3 + 2 * nc], refs[3 + 2 * nc:]
        step = pl.program_id(0)
        _host_comm(comm, "early", step, ni, cin, cout, csem)
        x = h_ref[...]
        r = lax.rsqrt(jnp.mean(x * x, axis=-1, keepdims=True) + EPS)
        o_ref[...] = (x * r * g_ref[...]).astype(o_ref.dtype)
        _host_comm(comm, "late", step, ni, cin, cout, csem)

    out = pl.pallas_call(
        body,
        grid=(ni,),
        in_specs=[pl.BlockSpec((tm, Dm), lambda i: (i, 0)), pl.BlockSpec((1, Dm), lambda i: (0, 0))] + [ANY] * nc,
        out_specs=[pl.BlockSpec((tm, Dm), lambda i: (i, 0))] + [ANY] * nc,
        out_shape=[jax.ShapeDtypeStruct((T, Dm), BF16)] + (comm.out_shape if comm else []),
        scratch_shapes=comm.scratch if comm else [],
        compiler_params=_cp("arbitrary"),
        name=name,
    )(h, g.reshape(1, Dm), *(comm.arrays if comm else []))
    return (out[0], list(out[1:])) if comm else out[0]


HALO = 16


def _conv3(ext, w_ref, b_ref):
    return w_ref[0:1, :] * pltpu.roll(ext, 2, 0) + w_ref[1:2, :] * pltpu.roll(ext, 1, 0) + w_ref[2:3, :] * ext + b_ref[...]


def _ffn_up_mid(hf, w_up_t, cw, cb, name="ffn_up_mid", tm=512, tc=1408):
    T, Dm = hf.shape
    Fh = w_up_t.shape[0] // 2
    tm = min(tm, T)
    nj, ni = Fh // tc, T // tm

    rc = min(256, tm)

    def body(h_ref, wg_ref, wu_ref, cwg_ref, cwu_ref, cbg_ref, cbu_ref, zg_ref, zu_ref, a_ref, eg_ref, eu_ref, ng_ref, nu_ref):
        @pl.when(pl.program_id(1) == 0)
        def _():
            eg_ref[...] = jnp.zeros_like(eg_ref)
            eu_ref[...] = jnp.zeros_like(eu_ref)

        for r in range(tm // rc):
            ext = slice(r * rc, r * rc + HALO + rc)
            cg = _conv3(eg_ref[ext, :].astype(F32), cwg_ref, cbg_ref)[HALO:]
            cu = _conv3(eu_ref[ext, :].astype(F32), cwu_ref, cbu_ref)[HALO:]
            a_ref[r * rc:(r + 1) * rc, :] = (_gelu(cg) * cu).astype(a_ref.dtype)
            h = h_ref[r * rc:(r + 1) * rc, :]
            for w_ref, z_ref, n_ref in ((wg_ref, zg_ref, ng_ref), (wu_ref, zu_ref, nu_ref)):
                z = _dot(h, w_ref[...], NT).astype(BF16)
                z_ref[r * rc:(r + 1) * rc, :] = z
                n_ref[r * rc:(r + 1) * rc, :] = z
        for e_ref, n_ref in ((eg_ref, ng_ref), (eu_ref, nu_ref)):
            e_ref[0:HALO, :] = e_ref[tm:tm + HALO, :]
            e_ref[HALO:HALO + tm, :] = n_ref[...]

    last = ni - 1
    z_spec = pl.BlockSpec((tm, tc), lambda j, i: (jnp.minimum(i, last), j))
    return pl.pallas_call(
        body,
        grid=(nj, ni + 1),
        in_specs=[pl.BlockSpec((tm, Dm), lambda j, i: (jnp.minimum(i, last), 0)),
                  pl.BlockSpec((tc, Dm), lambda j, i: (j, 0)), pl.BlockSpec((tc, Dm), lambda j, i: (nj + j, 0)),
                  pl.BlockSpec((3, tc), lambda j, i: (0, j)), pl.BlockSpec((3, tc), lambda j, i: (0, nj + j)),
                  pl.BlockSpec((1, tc), lambda j, i: (0, j)), pl.BlockSpec((1, tc), lambda j, i: (0, nj + j))],
        out_specs=[z_spec, z_spec, pl.BlockSpec((tm, tc), lambda j, i: (jnp.maximum(i - 1, 0), j))],
        out_shape=[jax.ShapeDtypeStruct((T, Fh), BF16)] * 3,
        scratch_shapes=[pltpu.VMEM((HALO + tm, tc), BF16), pltpu.VMEM((HALO + tm, tc), BF16),
                        pltpu.VMEM((tm, tc), BF16), pltpu.VMEM((tm, tc), BF16)],
        compiler_params=_cp("arbitrary", "arbitrary"),
        name=name,
    )(hf, w_up_t, w_up_t, cw, cw, cb, cb)


def _ffn_mid_bwd(zg, zu, da, cw, cb, comm=None, name="ffn_mid_bwd", tm=512, tc=256):
    T, Fh = zg.shape
    tm = min(tm, T)
    nj = Fh // tc
    hb = tm // HALO
    nhb = T // HALO

    nc = comm.n if comm else 0
    ni = T // tm

    def body(*refs):
        zg_ref, zu_ref, zgp_ref, zup_ref, zgn_ref, zun_ref, da_ref, dan_ref, wg_ref, wu_ref, bg_ref, bu_ref = refs[:12]
        cin = refs[12:12 + nc]
        dzg_ref, dzu_ref, dwg_ref, dwu_ref, dbg_ref, dbu_ref = refs[12 + nc:18 + nc]
        cout = refs[18 + nc:18 + 2 * nc]
        csem = refs[18 + 2 * nc:]
        i = pl.program_id(1)
        step = pl.program_id(0) * ni + i
        _host_comm(comm, "early", step, nj * ni, cin, cout, csem)
        first = i == 0
        last = i == ni - 1

        @pl.when(first)
        def _():
            dwg_ref[...] = jnp.zeros_like(dwg_ref)
            dwu_ref[...] = jnp.zeros_like(dwu_ref)
            dbg_ref[...] = jnp.zeros_like(dbg_ref)
            dbu_ref[...] = jnp.zeros_like(dbu_ref)

        def ext_of(p_ref, c_ref, n_ref):
            p = jnp.where(first, 0.0, p_ref[...].astype(F32))
            return jnp.concatenate([p, c_ref[...].astype(F32), n_ref[...].astype(F32)], axis=0)

        zge = ext_of(zgp_ref, zg_ref, zgn_ref)
        zue = ext_of(zup_ref, zu_ref, zun_ref)
        dan = jnp.where(last, 0.0, dan_ref[...].astype(F32))
        dae = jnp.concatenate([jnp.zeros((HALO, tc), F32), da_ref[...].astype(F32), dan], axis=0)
        zg1, zg2 = pltpu.roll(zge, 1, 0), pltpu.roll(zge, 2, 0)
        zu1, zu2 = pltpu.roll(zue, 1, 0), pltpu.roll(zue, 2, 0)
        cg = wg_ref[0:1, :] * zg2 + wg_ref[1:2, :] * zg1 + wg_ref[2:3, :] * zge + bg_ref[...]
        cu = wu_ref[0:1, :] * zu2 + wu_ref[1:2, :] * zu1 + wu_ref[2:3, :] * zue + bu_ref[...]
        gel, dgel = _gelu_and_grad(cg)
        dcg = dae * cu * dgel
        dcu = dae * gel
        lo, hi = HALO, HALO + tm

        def back(dc, taps, w_ref, dz_ref, dw_ref, db_ref):
            n = dc.shape[0]
            dz = w_ref[2:3, :] * dc + w_ref[1:2, :] * pltpu.roll(dc, n - 1, 0) + w_ref[0:1, :] * pltpu.roll(dc, n - 2, 0)
            dz_ref[...] = dz[lo:hi].astype(dz_ref.dtype)
            dcc = dc[lo:hi]
            db_ref[...] += jnp.sum(dcc, axis=0, keepdims=True)
            for k, tap in enumerate(taps):
                dw_ref[k:k + 1, :] += jnp.sum(dcc * tap[lo:hi], axis=0, keepdims=True)

        back(dcg, (zg2, zg1, zge), wg_ref, dzg_ref, dwg_ref, dbg_ref)
        back(dcu, (zu2, zu1, zue), wu_ref, dzu_ref, dwu_ref, dbu_ref)
        _host_comm(comm, "late", step, nj * ni, cin, cout, csem)

    cur = pl.BlockSpec((tm, tc), lambda j, i: (i, j))
    prev = pl.BlockSpec((HALO, tc), lambda j, i: (jnp.maximum(i * hb - 1, 0), j))
    nxt = pl.BlockSpec((HALO, tc), lambda j, i: (jnp.minimum((i + 1) * hb, nhb - 1), j))
    wg = pl.BlockSpec((3, tc), lambda j, i: (0, j))
    wu = pl.BlockSpec((3, tc), lambda j, i: (0, j + nj))
    bg = pl.BlockSpec((1, tc), lambda j, i: (0, j))
    bu = pl.BlockSpec((1, tc), lambda j, i: (0, j + nj))
    dw = pl.BlockSpec((3, tc), lambda j, i: (0, j))
    db = pl.BlockSpec((1, tc), lambda j, i: (0, j))
    out = pl.pallas_call(
        body,
        grid=(nj, ni),
        in_specs=[cur, cur, prev, prev, nxt, nxt, cur, nxt, wg, wu, bg, bu] + [ANY] * nc,
        out_specs=[cur, cur, dw, dw, db, db] + [ANY] * nc,
        out_shape=[jax.ShapeDtypeStruct((T, Fh), BF16), jax.ShapeDtypeStruct((T, Fh), BF16),
                   jax.ShapeDtypeStruct((3, Fh), F32), jax.ShapeDtypeStruct((3, Fh), F32),
                   jax.ShapeDtypeStruct((1, Fh), F32), jax.ShapeDtypeStruct((1, Fh), F32)] + (comm.out_shape if comm else []),
        scratch_shapes=comm.scratch if comm else [],
        compiler_params=_cp("arbitrary", "arbitrary"),
        name=name,
    )(zg, zu, zg, zu, zg, zu, da, da, cw, cw, cb, cb, *(comm.arrays if comm else []))
    dzg, dzu, dwg, dwu, dbg, dbu = out[:6]
    return dzg, dzu, jnp.concatenate([dwg, dwu], axis=1), jnp.concatenate([dbg, dbu], axis=1), list(out[6:])


def _sgu_mask():
    r = lax.broadcasted_iota(jnp.int32, (SGU_BLOCK, SGU_BLOCK), 0)
    c = lax.broadcasted_iota(jnp.int32, (SGU_BLOCK, SGU_BLOCK), 1)
    return (c < CHUNK) | (r >= CHUNK)


def _sgu_fwd(zz, ln_g, ln_b, w_s, b_s, name="sgu_fwd", tm=256):
    T = zz.shape[0]
    tm = min(tm, T)
    W = SGU_WIDTH

    def body(zu_ref, zv_ref, g_ref, b_ref, ws_ref, bs_ref, y_ref):
        u = _gelu(zu_ref[...].astype(F32))
        v = _gelu(zv_ref[...].astype(F32))
        mu = jnp.mean(v, axis=-1, keepdims=True)
        xc = v - mu
        rstd = lax.rsqrt(jnp.mean(xc * xc, axis=-1, keepdims=True) + EPS)
        vn = (xc * rstd * g_ref[...] + b_ref[...]).astype(BF16)
        mask = _sgu_mask()
        for g in range(SGU_GROUPS):
            wm = jnp.where(mask, ws_ref[g], 0.0).astype(BF16)
            cs = slice(g * SGU_GW, (g + 1) * SGU_GW)
            for blk in range(tm // SGU_BLOCK):
                rs = slice(blk * SGU_BLOCK, (blk + 1) * SGU_BLOCK)
                mixed = _dot(wm, vn[rs, cs], NN) + bs_ref[g]
                y_ref[rs, cs] = (u[rs, cs] * mixed).astype(y_ref.dtype)

    return pl.pallas_call(
        body,
        grid=(T // tm,),
        in_specs=[pl.BlockSpec((tm, W), lambda i: (i, 0)), pl.BlockSpec((tm, W), lambda i: (i, 1)),
                  pl.BlockSpec((1, W), lambda i: (0, 0)), pl.BlockSpec((1, W), lambda i: (0, 0)),
                  pl.BlockSpec((SGU_GROUPS, SGU_BLOCK, SGU_BLOCK), lambda i: (0, 0, 0)),
                  pl.BlockSpec((SGU_GROUPS, SGU_BLOCK, 1), lambda i: (0, 0, 0))],
        out_specs=pl.BlockSpec((tm, W), lambda i: (i, 0)),
        out_shape=jax.ShapeDtypeStruct((T, W), BF16),
        compiler_params=_cp("parallel"),
        name=name,
    )(zz, zz, ln_g.reshape(1, W), ln_b.reshape(1, W), w_s, b_s.reshape(SGU_GROUPS, SGU_BLOCK, 1))


def _sgu_bwd(zz, dy, ln_g, ln_b, w_s, b_s, name="sgu_bwd", tm=256):
    T = zz.shape[0]
    tm = min(tm, T)
    W = SGU_WIDTH

    def body(zu_ref, zv_ref, dy_ref, g_ref, b_ref, ws_ref, bs_ref, dzz_ref, dws_ref, dbs_ref, dg_ref, db_ref, dvn_ref):
        i = pl.program_id(0)

        @pl.when(i == 0)
        def _():
            dws_ref[...] = jnp.zeros_like(dws_ref)
            dbs_ref[...] = jnp.zeros_like(dbs_ref)
            dg_ref[...] = jnp.zeros_like(dg_ref)
            db_ref[...] = jnp.zeros_like(db_ref)

        u, du_dz = _gelu_and_grad(zu_ref[...].astype(F32))
        v, dv_dz = _gelu_and_grad(zv_ref[...].astype(F32))
        mu = jnp.mean(v, axis=-1, keepdims=True)
        xc = v - mu
        rstd = lax.rsqrt(jnp.mean(xc * xc, axis=-1, keepdims=True) + EPS)
        xhat = xc * rstd
        gv = g_ref[...]
        vn = (xhat * gv + b_ref[...]).astype(BF16)
        dyv = dy_ref[...].astype(F32)
        mask = _sgu_mask()
        for g in range(SGU_GROUPS):
            wm = jnp.where(mask, ws_ref[g], 0.0).astype(BF16)
            cs = slice(g * SGU_GW, (g + 1) * SGU_GW)
            dw_acc = jnp.zeros((SGU_BLOCK, SGU_BLOCK), F32)
            db_acc = jnp.zeros((SGU_BLOCK, 1), F32)
            for blk in range(tm // SGU_BLOCK):
                rs = slice(blk * SGU_BLOCK, (blk + 1) * SGU_BLOCK)
                vn_bg = vn[rs, cs]
                mixed = _dot(wm, vn_bg, NN) + bs_ref[g]
                dy_bg = dyv[rs, cs]
                dmixed = dy_bg * u[rs, cs]
                dmb = dmixed.astype(BF16)
                dw_acc += _dot(dmb, vn_bg, NT)
                db_acc += jnp.sum(dmixed, axis=1, keepdims=True)
                dvn_ref[rs, cs] = _dot(wm, dmb, TN)
                dzz_ref[rs, cs] = (dy_bg * mixed * du_dz[rs, cs]).astype(dzz_ref.dtype)
            dws_ref[g] += jnp.where(mask, dw_acc, 0.0)
            dbs_ref[g] += db_acc
        dvn = dvn_ref[...]
        dg_ref[...] += jnp.sum(dvn * xhat, axis=0, keepdims=True)
        db_ref[...] += jnp.sum(dvn, axis=0, keepdims=True)
        dxh = dvn * gv
        m1 = jnp.mean(dxh, axis=-1, keepdims=True)
        m2 = jnp.mean(dxh * xhat, axis=-1, keepdims=True)
        dv = rstd * (dxh - m1 - xhat * m2)
        dzz_ref[:, W:] = (dv * dv_dz).astype(dzz_ref.dtype)

    vec = pl.BlockSpec((1, W), lambda i: (0, 0))
    ws_spec = pl.BlockSpec((SGU_GROUPS, SGU_BLOCK, SGU_BLOCK), lambda i: (0, 0, 0))
    bs_spec = pl.BlockSpec((SGU_GROUPS, SGU_BLOCK, 1), lambda i: (0, 0, 0))
    return pl.pallas_call(
        body,
        grid=(T // tm,),
        in_specs=[pl.BlockSpec((tm, W), lambda i: (i, 0)), pl.BlockSpec((tm, W), lambda i: (i, 1)),
                  pl.BlockSpec((tm, W), lambda i: (i, 0)), vec, vec, ws_spec, bs_spec],
        out_specs=[pl.BlockSpec((tm, 2 * W), lambda i: (i, 0)), ws_spec, bs_spec, vec, vec],
        out_shape=[jax.ShapeDtypeStruct((T, 2 * W), BF16),
                   jax.ShapeDtypeStruct((SGU_GROUPS, SGU_BLOCK, SGU_BLOCK), F32),
                   jax.ShapeDtypeStruct((SGU_GROUPS, SGU_BLOCK, 1), F32),
                   jax.ShapeDtypeStruct((1, W), F32), jax.ShapeDtypeStruct((1, W), F32)],
        scratch_shapes=[pltpu.VMEM((tm, W), F32)],
        compiler_params=_cp("arbitrary"),
        name=name,
    )(zz, zz, dy, ln_g.reshape(1, W), ln_b.reshape(1, W), w_s, b_s.reshape(SGU_GROUPS, SGU_BLOCK, 1))


RET_TR = 256
RET_BLK = 256
QK_SCALE = RET_QK_DIM ** -0.5


def _ret_tables(T):
    half = RET_QK_DIM // 2
    inv = 1.0 / (10000.0 ** jnp.linspace(0.0, 1.0, half, dtype=F32))
    inv2 = jnp.concatenate([inv, inv])[None, :]
    sgn = jnp.concatenate([-jnp.ones((half,), F32), jnp.ones((half,), F32)])[None, :]
    tr = min(RET_TR, T)

    def trig(pos):
        ang = pos.astype(F32)[:, None] * inv2
        return jnp.stack([jnp.cos(ang), jnp.sin(ang), sgn * jnp.sin(ang)])

    tile_tab = jnp.pad(trig(jnp.arange(T // tr) * tr).transpose(1, 0, 2), ((0, 0), (0, 5), (0, 0)))
    row_tab = trig(jnp.arange(tr))
    log_g = jnp.log1p(-jnp.exp2(-5.0 - jnp.arange(RET_HEADS, dtype=F32)))
    idx = jnp.arange(RET_BLK, dtype=F32)
    dist = idx[:, None] - idx[None, :]
    cq, ck = jnp.arange(RET_BLK)[:, None] // CHUNK, jnp.arange(RET_BLK)[None, :] // CHUNK
    expo = jnp.where(ck == cq, jnp.abs(dist), dist)
    d_blk = jnp.where((ck <= cq)[None], jnp.exp(log_g[:, None, None] * expo[None]), 0.0)
    k_dec = jnp.exp(log_g[:, None] * (RET_BLK - 1 - idx)[None, :])[:, :, None]
    q_dec = jnp.exp(log_g[:, None] * (idx + 1.0)[None, :])[:, :, None]
    c_dec = jnp.exp(log_g * RET_BLK)[:, None, None]
    return tile_tab, row_tab, d_blk, q_dec, k_dec, c_dec


def _rot(x, c, s):
    return x * c + pltpu.roll(x, RET_QK_DIM // 2, 1) * s


def _rot_tables(tt_ref, rt_ref):
    ca, sa, ga = tt_ref[0:1, :], tt_ref[1:2, :], tt_ref[2:3, :]
    cb, sb, gb = rt_ref[0], rt_ref[1], rt_ref[2]
    return ca * cb - sa * sb, ga * cb + ca * gb


def _ret_specs(tr, rev, nb):
    ix = (lambda n: nb - 1 - n) if rev else (lambda n: n)
    tt = pl.BlockSpec((None, 8, RET_QK_DIM), lambda n: (ix(n), 0, 0))
    rt = pl.BlockSpec((3, tr, RET_QK_DIM), lambda n: (0, 0, 0))
    dm = pl.BlockSpec((RET_HEADS, RET_BLK, RET_BLK), lambda n: (0, 0, 0))
    dv = pl.BlockSpec((RET_HEADS, RET_BLK, 1), lambda n: (0, 0, 0))
    dc = pl.BlockSpec((RET_HEADS, 1, 1), lambda n: (0, 0, 0))
    return ix, [tt, rt, dm, dv, dv, dc]


def _ret_fwd(z_a, tables, name="ret_fwd"):
    T = z_a.shape[0]
    tr = min(RET_TR, T)
    cpb = tr // RET_BLK
    nb = T // tr
    QW, VW = RET_HEADS * RET_QK_DIM, RET_HEADS * RET_V_DIM
    ix, tab_specs = _ret_specs(tr, False, nb)

    def body(z_ref, tt_ref, rt_ref, dm_ref, qd_ref, kd_ref, cd_ref, y_ref, st_ref, state):
        @pl.when(pl.program_id(0) == 0)
        def _():
            state[...] = jnp.zeros_like(state)

        rot_c, rot_s = _rot_tables(tt_ref, rt_ref)
        for c in range(cpb):
            for h in range(RET_HEADS):
                rs = slice(c * RET_BLK, (c + 1) * RET_BLK)
                cc, ss = rot_c[rs, :], rot_s[rs, :]
                q = z_ref[rs, h * RET_QK_DIM:(h + 1) * RET_QK_DIM].astype(F32)
                k = z_ref[rs, QW + h * RET_QK_DIM:QW + (h + 1) * RET_QK_DIM].astype(F32)
                v = z_ref[rs, 2 * QW + h * RET_V_DIM:2 * QW + (h + 1) * RET_V_DIM]
                gt = z_ref[rs, 2 * QW + VW + h * RET_V_DIM:2 * QW + VW + (h + 1) * RET_V_DIM].astype(F32)
                qr = _rot(q, cc, ss)
                kr = _rot(k, cc, ss) * QK_SCALE
                s_old = state[h]
                sb = s_old.astype(BF16)
                st_ref[c, h] = sb
                s = _dot(qr.astype(BF16), kr.astype(BF16), NT) * dm_ref[h]
                o = _dot(s.astype(BF16), v, NN) + _dot((qr * qd_ref[h]).astype(BF16), sb, NN)
                state[h] = s_old * cd_ref[h] + _dot((kr * kd_ref[h]).astype(BF16), v, TN)
                mu = jnp.mean(o, axis=-1, keepdims=True)
                oc = o - mu
                rn = oc * lax.rsqrt(jnp.mean(oc * oc, axis=-1, keepdims=True) + EPS)
                silu = gt / (1.0 + jnp.exp(-gt))
                y_ref[rs, h * RET_V_DIM:(h + 1) * RET_V_DIM] = (silu * rn).astype(y_ref.dtype)

    return pl.pallas_call(
        body,
        grid=(nb,),
        in_specs=[pl.BlockSpec((tr, RET_W), lambda n: (n, 0))] + tab_specs,
        out_specs=[pl.BlockSpec((tr, VW), lambda n: (n, 0)),
                   pl.BlockSpec((cpb, RET_HEADS, RET_QK_DIM, RET_V_DIM), lambda n: (n, 0, 0, 0))],
        out_shape=[jax.ShapeDtypeStruct((T, Y_COLS), BF16),
                   jax.ShapeDtypeStruct((T // RET_BLK, RET_HEADS, RET_QK_DIM, RET_V_DIM), BF16)],
        scratch_shapes=[pltpu.VMEM((RET_HEADS, RET_QK_DIM, RET_V_DIM), F32)],
        compiler_params=_cp("arbitrary"),
        name=name,
    )(z_a, *tables)


def _ret_bwd(z_a, dy, states, tables, name="ret_bwd"):
    T = z_a.shape[0]
    tr = min(RET_TR, T)
    cpb = tr // RET_BLK
    nb = T // tr
    QW, VW = RET_HEADS * RET_QK_DIM, RET_HEADS * RET_V_DIM
    ix, tab_specs = _ret_specs(tr, True, nb)

    def body(z_ref, dy_ref, st_ref, tt_ref, rt_ref, dm_ref, qd_ref, kd_ref, cd_ref, dz_ref, dstate):
        @pl.when(pl.program_id(0) == 0)
        def _():
            dstate[...] = jnp.zeros_like(dstate)

        rot_c, rot_s = _rot_tables(tt_ref, rt_ref)
        for c in reversed(range(cpb)):
            for h in range(RET_HEADS):
                rs = slice(c * RET_BLK, (c + 1) * RET_BLK)
                cc, ss = rot_c[rs, :], rot_s[rs, :]
                q = z_ref[rs, h * RET_QK_DIM:(h + 1) * RET_QK_DIM].astype(F32)
                k = z_ref[rs, QW + h * RET_QK_DIM:QW + (h + 1) * RET_QK_DIM].astype(F32)
                v = z_ref[rs, 2 * QW + h * RET_V_DIM:2 * QW + (h + 1) * RET_V_DIM]
                gt = z_ref[rs, 2 * QW + VW + h * RET_V_DIM:2 * QW + VW + (h + 1) * RET_V_DIM].astype(F32)
                dyv = dy_ref[rs, h * RET_V_DIM:(h + 1) * RET_V_DIM].astype(F32)
                dmat, qd, kd = dm_ref[h], qd_ref[h], kd_ref[h]
                qr = _rot(q, cc, ss)
                kr = _rot(k, cc, ss) * QK_SCALE
                qrb, krb = qr.astype(BF16), kr.astype(BF16)
                sb = st_ref[c, h]
                sd = (_dot(qrb, krb, NT) * dmat).astype(BF16)
                qdb = (qr * qd).astype(BF16)
                kdb = (kr * kd).astype(BF16)
                o = _dot(sd, v, NN) + _dot(qdb, sb, NN)
                mu = jnp.mean(o, axis=-1, keepdims=True)
                oc = o - mu
                rstd = lax.rsqrt(jnp.mean(oc * oc, axis=-1, keepdims=True) + EPS)
                rn = oc * rstd
                sg = 1.0 / (1.0 + jnp.exp(-gt))
                dgt = dyv * rn * (sg * (1.0 + gt * (1.0 - sg)))
                drn = dyv * (gt * sg)
                do = rstd * (drn - jnp.mean(drn, axis=-1, keepdims=True) - rn * jnp.mean(drn * rn, axis=-1, keepdims=True))
                dob = do.astype(BF16)
                dsn = dstate[h]
                dsnb = dsn.astype(BF16)
                ds_raw = (_dot(dob, v, NT) * dmat).astype(BF16)
                dv = _dot(sd, dob, TN) + _dot(kdb, dsnb, NN)
                dqr = _dot(ds_raw, krb, NN) + qd * _dot(dob, sb, NT)
                dkr = (_dot(ds_raw, qrb, TN) + kd * _dot(v, dsnb, NT)) * QK_SCALE
                dstate[h] = dsn * cd_ref[h] + _dot(qdb, dob, TN)
                dq = dqr * cc + pltpu.roll(dqr * ss, RET_QK_DIM // 2, 1)
                dk = dkr * cc + pltpu.roll(dkr * ss, RET_QK_DIM // 2, 1)
                dz_ref[rs, h * RET_QK_DIM:(h + 1) * RET_QK_DIM] = dq.astype(dz_ref.dtype)
                dz_ref[rs, QW + h * RET_QK_DIM:QW + (h + 1) * RET_QK_DIM] = dk.astype(dz_ref.dtype)
                dz_ref[rs, 2 * QW + h * RET_V_DIM:2 * QW + (h + 1) * RET_V_DIM] = dv.astype(dz_ref.dtype)
                dz_ref[rs, 2 * QW + VW + h * RET_V_DIM:2 * QW + VW + (h + 1) * RET_V_DIM] = dgt.astype(dz_ref.dtype)

    return pl.pallas_call(
        body,
        grid=(nb,),
        in_specs=[pl.BlockSpec((tr, RET_W), lambda n: (ix(n), 0)),
                  pl.BlockSpec((tr, VW), lambda n: (ix(n), 0)),
                  pl.BlockSpec((cpb, RET_HEADS, RET_QK_DIM, RET_V_DIM), lambda n: (ix(n), 0, 0, 0))] + tab_specs,
        out_specs=pl.BlockSpec((tr, RET_W), lambda n: (ix(n), 0)),
        out_shape=jax.ShapeDtypeStruct((T, RET_W), BF16),
        scratch_shapes=[pltpu.VMEM((RET_HEADS, RET_QK_DIM, RET_V_DIM), F32)],
        compiler_params=_cp("arbitrary"),
        name=name,
    )(z_a, dy, states, *tables)


ATT_TQ = 256
ATT_CPB = ATT_TQ // CHUNK
ATT_SCALE = ATT_HEAD_DIM ** -0.5


ATT_WIN = 3 * ATT_TQ
ATT_NB = CHUNK * ATT_BAND


def _rel_index():
    i = np.arange(CHUNK)[:, None]
    j = np.arange(ATT_BAND)[None, :]
    rel = np.clip(i + ATT_PAST * CHUNK - j, -MAX_REL, MAX_REL) + MAX_REL
    return jnp.asarray(rel.reshape(1, ATT_NB).astype(np.int32))


def _split3(x):
    hi = x.astype(BF16)
    r1 = x - hi.astype(F32)
    mid = r1.astype(BF16)
    lo = (r1 - mid.astype(F32)).astype(BF16)
    return hi, mid, lo


REL_TILE = 4608


def _bias_expand(rel_bias, name="bias_expand"):
    H = rel_bias.shape[0]
    n = ATT_NB
    padded = jnp.pad(rel_bias, ((0, 0), (0, N_REL_PAD - N_REL)))

    def body(rb_ref, idx_ref, o_ref):
        onehot = (lax.broadcasted_iota(jnp.int32, (N_REL_PAD, REL_TILE), 0) == idx_ref[...]).astype(BF16)
        hi, mid, lo = _split3(rb_ref[...])
        o_ref[...] = _dot(hi, onehot, NN) + _dot(mid, onehot, NN) + _dot(lo, onehot, NN)

    out = pl.pallas_call(
        body,
        grid=(n // REL_TILE,),
        in_specs=[pl.BlockSpec((H, N_REL_PAD), lambda t: (0, 0)), pl.BlockSpec((1, REL_TILE), lambda t: (0, t))],
        out_specs=pl.BlockSpec((H, REL_TILE), lambda t: (0, t)),
        out_shape=jax.ShapeDtypeStruct((H, n), F32),
        compiler_params=_cp("parallel"),
        name=name,
    )(padded, _rel_index())
    return out.reshape(H, CHUNK, ATT_BAND)


def _bias_tile(band, name="bias_tile"):
    H = band.shape[0]
    padded = jnp.pad(band, ((0, 0), (0, 0), (0, ATT_WIN - ATT_BAND)), constant_values=NEG_INF)

    def body(b_ref, o_ref):
        b = b_ref[...]
        col = lax.broadcasted_iota(jnp.int32, (CHUNK, ATT_WIN), 1)
        keep = col >= (2 - pl.program_id(0)) * ATT_TQ
        for a in range(ATT_CPB):
            o_ref[a * CHUNK:(a + 1) * CHUNK, :] = jnp.where(keep, pltpu.roll(b, a * CHUNK, 1) if a else b, NEG_INF)

    return pl.pallas_call(
        body,
        grid=(3, H),
        in_specs=[pl.BlockSpec((None, CHUNK, ATT_WIN), lambda v, h: (h, 0, 0))],
        out_specs=pl.BlockSpec((None, None, ATT_TQ, ATT_WIN), lambda v, h: (v, h, 0, 0)),
        out_shape=jax.ShapeDtypeStruct((3, H, ATT_TQ, ATT_WIN), F32),
        compiler_params=_cp("parallel", "parallel"),
        name=name,
    )(padded)


def _bias_untile(dtile, name="bias_untile"):
    H = dtile.shape[0]

    def body(d_ref, o_ref):
        acc = d_ref[0:CHUNK, :]
        for a in range(1, ATT_CPB):
            acc = acc + pltpu.roll(d_ref[a * CHUNK:(a + 1) * CHUNK, :], ATT_WIN - a * CHUNK, 1)
        o_ref[...] = acc

    out = pl.pallas_call(
        body,
        grid=(H,),
        in_specs=[pl.BlockSpec((None, ATT_TQ, ATT_WIN), lambda h: (h, 0, 0))],
        out_specs=pl.BlockSpec((None, CHUNK, ATT_WIN), lambda h: (h, 0, 0)),
        out_shape=jax.ShapeDtypeStruct((H, CHUNK, ATT_WIN), F32),
        compiler_params=_cp("parallel"),
        name=name,
    )(dtile)
    return out[:, :, :ATT_BAND]


def _bias_reduce(dbias, name="bias_reduce"):
    H = dbias.shape[0]
    n = ATT_NB

    def body(db_ref, idx_ref, o_ref):
        @pl.when(pl.program_id(0) == 0)
        def _():
            o_ref[...] = jnp.zeros_like(o_ref)

        onehot = (lax.broadcasted_iota(jnp.int32, (N_REL_PAD, REL_TILE), 0) == idx_ref[...]).astype(BF16)
        hi, mid, lo = _split3(db_ref[...])
        o_ref[...] += _dot(hi, onehot, NT) + _dot(mid, onehot, NT) + _dot(lo, onehot, NT)

    out = pl.pallas_call(
        body,
        grid=(n // REL_TILE,),
        in_specs=[pl.BlockSpec((H, REL_TILE), lambda t: (0, t)), pl.BlockSpec((1, REL_TILE), lambda t: (0, t))],
        out_specs=pl.BlockSpec((H, N_REL_PAD), lambda t: (0, 0)),
        out_shape=jax.ShapeDtypeStruct((H, N_REL_PAD), F32),
        compiler_params=_cp("arbitrary"),
        name=name,
    )(dbias.reshape(H, n), _rel_index())
    return out[:, :N_REL]


def _att_probs(q, kwin, bias):
    s = _dot(q, kwin, NT) + bias
    e = jnp.exp(s - jnp.max(s, axis=-1, keepdims=True))
    return e * (1.0 / jnp.sum(e, axis=-1, keepdims=True))


ATT_PAIR = 2 * ATT_HEAD_DIM
ATT_NP = ATT_HEADS // 2
ATT_QW = ATT_HEADS * ATT_HEAD_DIM
Y_COLS = RET_HEADS * RET_V_DIM + ATT_QW


def _att_specs(tq, nq, clip_q, q_col0):
    cb = ATT_QW // ATT_PAIR
    qi = (lambda p, m: (jnp.minimum(m, nq - 1), q_col0 + p)) if clip_q else (lambda p, m: (m, q_col0 + p))
    q = pl.BlockSpec((tq, ATT_PAIR), qi)

    def win(col0):
        return [pl.BlockSpec((tq, ATT_PAIR), functools.partial(lambda p, m, back: (jnp.clip(m - back, 0, nq - 1), col0 + p), back=b))
                for b in (2, 1, 0)]

    bias = pl.BlockSpec((None, 2, ATT_TQ, ATT_WIN), lambda p, m: (jnp.minimum(m, 2), p, 0, 0))
    return q, win(cb), win(2 * cb), bias


def _head_masks(rows):
    lane = lax.broadcasted_iota(jnp.int32, (rows, ATT_PAIR), 1)
    return lane < ATT_HEAD_DIM


def _att_fwd(z_b, bias, y, comm=None, name="att_fwd"):
    T = z_b.shape[0]
    tq = ATT_TQ
    nq = T // tq
    qs, kwin, vwin, bs = _att_specs(tq, nq, False, 0)
    nc = comm.n if comm else 0
    total = ATT_NP * nq

    def body(*refs):
        q_ref, k0, k1, k2, v0, v1, v2, b_ref = refs[:8]
        cin = refs[9:9 + nc]
        o_ref = refs[9 + nc]
        cout = refs[10 + nc:10 + 2 * nc]
        csem = refs[10 + 2 * nc:]
        m = pl.program_id(1)
        step = pl.program_id(0) * nq + m
        _host_comm(comm, "early", step, total, cin, cout, csem)
        kw = jnp.concatenate([k0[...], k1[...], k2[...]], axis=0)
        vw = jnp.concatenate([v0[...], v1[...], v2[...]], axis=0)
        q2 = q_ref[...] * ATT_SCALE
        even = _head_masks(tq)
        outs = []
        for hh in range(2):
            qm = jnp.where(even if hh == 0 else ~even, q2, jnp.zeros_like(q2))
            p = _att_probs(qm, kw, b_ref[hh])
            outs.append(_dot(p.astype(BF16), vw, NN))
        o_ref[...] = jnp.where(even, outs[0], outs[1]).astype(o_ref.dtype)
        _host_comm(comm, "late", step, total, cin, cout, csem)

    y_cb = (Y_COLS - ATT_QW) // ATT_PAIR
    out = pl.pallas_call(
        body,
        grid=(ATT_NP, nq),
        in_specs=[qs] + kwin + vwin + [bs, ANY] + [ANY] * nc,
        out_specs=[pl.BlockSpec((tq, ATT_PAIR), lambda p, m: (m, y_cb + p))] + [ANY] * nc,
        out_shape=[jax.ShapeDtypeStruct((T, Y_COLS), BF16)] + (comm.out_shape if comm else []),
        scratch_shapes=comm.scratch if comm else [],
        input_output_aliases={8: 0},
        compiler_params=_cp("arbitrary", "arbitrary"),
        name=name,
    )(z_b, z_b, z_b, z_b, z_b, z_b, z_b, bias, y, *(comm.arrays if comm else []))
    return out[0], list(out[1:])


def _att_bwd(z_b, bias, dy, comm=None, name="att_bwd"):
    T = z_b.shape[0]
    tq = ATT_TQ
    nq = T // tq
    y_cb = (Y_COLS - ATT_QW) // ATT_PAIR
    qs, kwin, vwin, bs = _att_specs(tq, nq, True, 0)
    dos = _att_specs(tq, nq, True, y_cb)[0]
    kv_out = pl.BlockSpec((tq, ATT_PAIR), lambda p, m: (jnp.maximum(m - 2, 0), p))
    W3 = 3 * tq
    nc = comm.n if comm else 0
    total = ATT_NP * (nq + 2)

    def body(*refs):
        q_ref, k0, k1, k2, v0, v1, v2, b_ref, do_ref = refs[:9]
        cin = refs[9:9 + nc]
        dq_ref, dk_ref, dv_ref, db_ref = refs[9 + nc:13 + nc]
        cout = refs[13 + nc:13 + 2 * nc]
        dkc, dvc, dkw, dvw = refs[13 + 2 * nc:17 + 2 * nc]
        csem = refs[17 + 2 * nc:]
        m = pl.program_id(1)
        step = pl.program_id(0) * (nq + 2) + m
        _host_comm(comm, "early", step, total, cin, cout, csem)

        @pl.when(m == 0)
        def _():
            dkc[...] = jnp.zeros_like(dkc)
            dvc[...] = jnp.zeros_like(dvc)
            db_ref[...] = jnp.zeros_like(db_ref)

        @pl.when(m >= nq)
        def _():
            dkw[...] = jnp.zeros_like(dkw)
            dvw[...] = jnp.zeros_like(dvw)

        @pl.when(m < nq)
        def _():
            kw = jnp.concatenate([k0[...], k1[...], k2[...]], axis=0)
            vw = jnp.concatenate([v0[...], v1[...], v2[...]], axis=0)
            q2, do2 = q_ref[...] * ATT_SCALE, do_ref[...]
            even = _head_masks(tq)
            dqs, dks, dvs = [], [], []
            for hh in range(2):
                mine = even if hh == 0 else ~even
                p = _att_probs(jnp.where(mine, q2, jnp.zeros_like(q2)), kw, b_ref[hh])
                dp = _dot(jnp.where(mine, do2, jnp.zeros_like(do2)), vw, NT)
                ds = p * (dp - jnp.sum(dp * p, axis=-1, keepdims=True))
                db_ref[hh] += ds
                dsb = ds.astype(BF16)
                dqs.append(_dot(dsb, kw, NN))
                dks.append(_dot(dsb, q2, TN))
                dvs.append(_dot(p.astype(BF16), do2, TN))
            even_w = _head_masks(W3)
            dq_ref[...] = (jnp.where(even, dqs[0], dqs[1]) * ATT_SCALE).astype(dq_ref.dtype)
            dkw[...] = jnp.where(even_w, dks[0], dks[1])
            dvw[...] = jnp.where(even_w, dvs[0], dvs[1])

        dk_ref[...] = (dkc[0:tq, :] + dkw[0:tq, :]).astype(dk_ref.dtype)
        dv_ref[...] = (dvc[0:tq, :] + dvw[0:tq, :]).astype(dv_ref.dtype)
        dkc[0:tq, :] = dkc[tq:2 * tq, :] + dkw[tq:2 * tq, :]
        dvc[0:tq, :] = dvc[tq:2 * tq, :] + dvw[tq:2 * tq, :]
        dkc[tq:2 * tq, :] = dkw[2 * tq:W3, :]
        dvc[tq:2 * tq, :] = dvw[2 * tq:W3, :]
        _host_comm(comm, "late", step, total, cin, cout, csem)

    qo = pl.BlockSpec((tq, ATT_PAIR), lambda p, m: (jnp.minimum(m, nq - 1), p))
    dbs = pl.BlockSpec((2, ATT_TQ, ATT_WIN), lambda p, m: (p, 0, 0))
    hd = jax.ShapeDtypeStruct((T, ATT_QW), BF16)
    out = pl.pallas_call(
        body,
        grid=(ATT_NP, nq + 2),
        in_specs=[qs] + kwin + vwin + [bs, dos] + [ANY] * nc,
        out_specs=[qo, kv_out, kv_out, dbs] + [ANY] * nc,
        out_shape=[hd, hd, hd, jax.ShapeDtypeStruct((ATT_HEADS, ATT_TQ, ATT_WIN), F32)] + (comm.out_shape if comm else []),
        scratch_shapes=[pltpu.VMEM((2 * tq, ATT_PAIR), F32), pltpu.VMEM((2 * tq, ATT_PAIR), F32),
                        pltpu.VMEM((W3, ATT_PAIR), F32), pltpu.VMEM((W3, ATT_PAIR), F32)] + (comm.scratch if comm else []),
        compiler_params=_cp("arbitrary", "arbitrary"),
        name=name,
    )(z_b, z_b, z_b, z_b, z_b, z_b, z_b, bias, dy, *(comm.arrays if comm else []))
    return out[0], out[1], out[2], out[3], list(out[4:])


REST = ["ab_out", "c_in_t", "c_out", "up_t0", "up_t1", "down0", "down1"]


def _local_step(x, target, wt, small, rest_shards=None, overlap=False, hn0=None):
    T = x.shape[0]
    Fh = FFN_HIDDEN
    tables = _ret_tables(T)
    gw, gs, recv = {}, {}, {}
    wt = dict(wt)

    if hn0 is None:
        hn0 = _rms_fwd(x, small["attn_norm_g"][0], name="rms_fwd")
    z_a = _mm(hn0, wt["ab_in_t"][:RET_W], "nt", BF16, name="mm_ab_in_a")
    z_b = _mm(hn0, wt["ab_in_t"][RET_W:], "nt", BF16, name="mm_ab_in_b")
    y, states = _ret_fwd(z_a, tables)
    bias = _bias_tile(_bias_expand(small["rel_bias"]))
    y, rest = _att_fwd(z_b, bias, y, comm=_Comm("gather", rest_shards) if rest_shards is not None else None)
    if rest_shards is not None:
        full = dict(zip(REST, rest))
        wt.update(ab_out=full["ab_out"], c_in_t=full["c_in_t"], c_out=full["c_out"],
                  up_t=[full["up_t0"], full["up_t1"]], down=[full["down0"], full["down1"]])
    h1, hf0 = _mm_rows(y, wt["ab_out"], x, "mm_ab_out", norm_g=small["ffn_norm_g"][0])

    def ffn_fwd(h, hf, layer, next_g):
        zg, zu, a = _ffn_up_mid(hf, wt["up_t"][layer], small["conv_w"][layer], small["conv_b"][layer][None, :])
        if next_g is None:
            return _mm_rows(a, wt["down"][layer], h, "mm_down_loss", loss=(target, small["final_g"])), (hf, zg, zu, a)
        h_out, hn_next = _mm_rows(a, wt["down"][layer], h, "mm_down", norm_g=next_g)
        return h_out, hn_next, (hf, zg, zu, a)

    def ffn_bwd(dh_out, h, layer, saved, exchange=None):
        hf, zg, zu, a = saved
        da = _mm(dh_out, wt["down"][layer], "nt", BF16, name="mm_d_a")
        d_down = _mm(a, dh_out, "tn", BF16, name="mm_dw_down")
        comm = _Comm("exchange", [gw[k] for k in exchange]) if exchange else None
        dzg, dzu, dcw, dcb, got = _ffn_mid_bwd(zg, zu, da, small["conv_w"][layer], small["conv_b"][layer][None, :], comm=comm)
        recv.update(zip(exchange or [], got))
        dhf = _mm(dzg, wt["up_t"][layer][:Fh], "nn", F32, name="mm_d_hf")
        dh, dg = _mm_rows(dzu, wt["up_t"][layer][Fh:], dhf, "mm_d_hf_norm", bwd=(h, small["ffn_norm_g"][layer], dh_out))
        d_up = jnp.concatenate([_mm(dzg, hf, "tn", BF16, name="mm_dw_up"), _mm(dzu, hf, "tn", BF16, name="mm_dw_up")], axis=0)
        return dh, dg, d_up, d_down, dcw, dcb

    h2, hn1, ffn0 = ffn_fwd(h1, hf0, 0, small["attn_norm_g"][1])

    zz = _mm(hn1, wt["c_in_t"], "nt", BF16, name="mm_c_in")
    ys = _sgu_fwd(zz, small["ln_g"], small["ln_b"], small["w_s"], small["b_s"])
    h3, hf1 = _mm_rows(ys, wt["c_out"], h2, "mm_c_out", norm_g=small["ffn_norm_g"][1])
    (loss_vec, dh4, gs["final_g"]), ffn1 = ffn_fwd(h3, hf1, 1, None)

    dh3, dgf1, gw["up_t1"], gw["down1"], dcw1, dcb1 = ffn_bwd(dh4, h3, 1, ffn1)
    dys = _mm(dh3, wt["c_out"], "nt", BF16, name="mm_d_ys")
    gw["c_out"] = _mm(ys, dh3, "tn", BF16, name="mm_dw_c_out")
    dzz, gs["w_s"], dbs, dlg, dlb = _sgu_bwd(zz, dys, small["ln_g"], small["ln_b"], small["w_s"], small["b_s"])
    gs["b_s"], gs["ln_g"], gs["ln_b"] = dbs[:, :, 0], dlg[0], dlb[0]
    dh2, dga1 = _mm_rows(dzz, wt["c_in_t"], None, "mm_d_hn1_norm", bwd=(h2, small["attn_norm_g"][1], dh3))
    gw["c_in_t"] = _mm(dzz, hn1, "tn", BF16, name="mm_dw_c_in")

    dh1, dgf0, gw["up_t0"], gw["down0"], dcw0, dcb0 = ffn_bwd(
        dh2, h1, 0, ffn0, exchange=["c_in_t", "c_out", "up_t1", "down1"] if overlap else None)

    dy = _mm(dh1, wt["ab_out"], "nt", BF16, name="mm_d_y")
    gw["ab_out"] = _mm(y, dh1, "tn", BF16, name="mm_dw_ab_out")
    dz_a = _ret_bwd(z_a, dy, states, tables)
    late = ["ab_out", "up_t0", "down0"] if overlap else []
    dq, dk, dv, dbias, got = _att_bwd(z_b, bias, dy, comm=_Comm("exchange", [gw[k] for k in late]) if late else None)
    recv.update(zip(late, got))
    dz_b = jnp.concatenate([dq, dk, dv], axis=1)
    gs["rel_bias"] = _bias_reduce(_bias_untile(dbias))
    gw["ab_in_t"] = jnp.concatenate([_mm(dz_a, hn0, "tn", BF16, name="mm_dw_ab_in_a"),
                                     _mm(dz_b, hn0, "tn", BF16, name="mm_dw_ab_in_b")], axis=0)
    if overlap:
        dhn0, got = _mm(dz_a, wt["ab_in_t"][:RET_W], "nn", F32, name="mm_d_hn0", comm=_Comm("exchange", [gw["ab_in_t"]]))
        recv["ab_in_t"] = got[0]
    else:
        dhn0 = _mm(dz_a, wt["ab_in_t"][:RET_W], "nn", F32, name="mm_d_hn0")
    grad_x, dga0 = _mm_rows(dz_b, wt["ab_in_t"][RET_W:], dhn0, "mm_d_hn0_norm", bwd=(x, small["attn_norm_g"][0], dh1))

    gs["attn_norm_g"] = jnp.concatenate([dga0, dga1], axis=0)
    gs["ffn_norm_g"] = jnp.concatenate([dgf0, dgf1], axis=0)
    gs["conv_w"] = jnp.stack([dcw0, dcw1])
    gs["conv_b"] = jnp.concatenate([dcb0, dcb1], axis=0)
    gs["final_g"] = gs["final_g"][0]
    return loss_vec[0, 0], grad_x, gw, gs, recv


MESH_ID = pl.DeviceIdType.MESH
ANY = pl.BlockSpec(memory_space=pl.ANY)


def _my_place():
    return lax.axis_index("x"), lax.axis_index("y"), lax.axis_index("c")


class _Comm:
    def __init__(self, kind, arrays):
        self.kind, self.arrays, self.n = kind, list(arrays), len(arrays)
        if kind == "gather":
            self.out_shape = [jax.ShapeDtypeStruct((N_DEV * s.shape[0], s.shape[1]), s.dtype) for s in arrays]
        else:
            self.out_shape = [jax.ShapeDtypeStruct((N_DEV, g.shape[0] // N_DEV, g.shape[1]), g.dtype) for g in arrays]
        n = self.n
        self.scratch = [pltpu.SemaphoreType.DMA((n, 7)), pltpu.SemaphoreType.DMA((n, 7)), pltpu.SemaphoreType.DMA((n,))]

    def phase(self, ph, in_refs, out_refs, sems):
        (self._gather if self.kind == "gather" else self._exchange)(ph, in_refs, out_refs, sems)

    def _gather(self, ph, x_refs, o_refs, sems):
        n = self.n
        send_sems, recv_sems, local_sems = sems
        x, y, c = _my_place()
        me, sibling = (x, y, c), (x, y, 1 - c)
        chips = [(1 - x, y), (x, 1 - y), (1 - x, 1 - y)]

        def rows(a, place):
            m = x_refs[a].shape[0]
            px, py, pc = place
            return o_refs[a].at[pl.ds((4 * px + 2 * py + pc) * m, m), :]

        def copy(a, k, block, to, own=False):
            return pltpu.make_async_remote_copy(
                src_ref=x_refs[a] if own else rows(a, block), dst_ref=rows(a, block),
                send_sem=send_sems.at[a, k], recv_sem=recv_sems.at[a, k], device_id=to, device_id_type=MESH_ID)

        def mine():
            return [pltpu.make_async_copy(x_refs[a], rows(a, me), local_sems.at[a]) for a in range(n)]

        def first():
            out = []
            for a in range(n):
                out.append(copy(a, 0, me, sibling, own=True))
                out += [copy(a, 1 + j, me, (*chip, c), own=True) for j, chip in enumerate(chips)]
            return out

        def passed():
            return [copy(a, 4 + j, (*chip, c), sibling) for j, chip in enumerate(chips) for a in range(n)]

        if ph == 0:
            for cp in mine() + first():
                cp.start()
        elif ph == 1:
            fw = passed()
            for j, chip in enumerate(chips):
                for a in range(n):
                    copy(a, 1 + j, (*chip, c), me).wait_recv()
                    fw[j * n + a].start()
        else:
            for a in range(n):
                copy(a, 0, sibling, me).wait_recv()
                for j, chip in enumerate(chips):
                    copy(a, 4 + j, (*chip, 1 - c), me).wait_recv()
            for cp in first() + passed():
                cp.wait_send()
            for cp in mine():
                cp.wait()

    def _exchange(self, ph, g_refs, o_refs, sems):
        n = self.n
        send_sems, recv_sems, local_sems = sems
        x, y, c = _my_place()
        me = 4 * x + 2 * y + c
        peers = [(x ^ ((k >> 2) & 1), y ^ ((k >> 1) & 1), c ^ (k & 1)) for k in range(1, N_DEV)]

        def block(a, idx):
            m = g_refs[a].shape[0] // N_DEV
            return g_refs[a].at[pl.ds(idx * m, m), :]

        def copy(a, k, slot):
            px, py, pc = peers[k]
            return pltpu.make_async_remote_copy(
                src_ref=block(a, 4 * px + 2 * py + pc), dst_ref=o_refs[a].at[slot],
                send_sem=send_sems.at[a, k], recv_sem=recv_sems.at[a, k], device_id=peers[k], device_id_type=MESH_ID)

        if ph == 1:
            return
        mine = [pltpu.make_async_copy(block(a, me), o_refs[a].at[me], local_sems.at[a]) for a in range(n)]
        sends = [copy(a, k, me) for k in range(N_DEV - 1) for a in range(n)]
        if ph == 0:
            for cp in mine + sends:
                cp.start()
        else:
            for k in range(N_DEV - 1):
                px, py, pc = peers[k]
                for a in range(n):
                    copy(a, k, 4 * px + 2 * py + pc).wait_recv()
            for cp in sends:
                cp.wait_send()
            for cp in mine:
                cp.wait()


def _comm_call(comm, name):
    n = comm.n

    def body(*refs):
        for ph in range(3):
            comm.phase(ph, refs[:n], refs[n:2 * n], refs[2 * n:])

    return pl.pallas_call(
        body, out_shape=comm.out_shape, in_specs=[ANY] * n, out_specs=[ANY] * n, scratch_shapes=comm.scratch, name=name,
    )(*comm.arrays)


def _host_comm(comm, when, step, total, cin, cout, csem):
    if comm is None:
        return
    sched = {0: 0, 1: (3 * total) // 4, 2: total - 1}
    phases = (0, 1) if when == "early" else (2,)
    for ph in phases:
        if ph == 1 and comm.kind == "exchange":
            continue

        @pl.when(step == sched[ph])
        def _(ph=ph):
            comm.phase(ph, cin, cout, csem)


def _all_gather(shards, name="all_gather"):
    return _comm_call(_Comm("gather", shards), name)


def _row_tile(r, target=256):
    best = None
    for t in range(8, min(r, target) + 1, 8):
        if r % t == 0:
            best = t
    return best if best is not None else r


def _sum8(parts, name="sum8"):
    _, M, N = parts.shape
    tr = _row_tile(M, 128)

    def body(p_ref, o_ref):
        acc = p_ref[0].astype(F32)
        for d in range(1, N_DEV):
            acc = acc + p_ref[d].astype(F32)
        o_ref[...] = acc

    return pl.pallas_call(
        body,
        grid=(M // tr,),
        in_specs=[pl.BlockSpec((N_DEV, tr, N), lambda i: (0, i, 0))],
        out_specs=pl.BlockSpec((tr, N), lambda i: (i, 0)),
        out_shape=jax.ShapeDtypeStruct((M, N), F32),
        compiler_params=_cp("parallel"),
        name=name,
    )(parts)


def _adamw(w, g, m, v, name="adamw"):
    shape = w.shape
    if w.ndim == 1:
        r2 = (1, shape[0])
    else:
        r2 = (int(np.prod(shape[:-1])), shape[-1])
    R, C = r2
    tr = _row_tile(R)
    bc1 = 1.0 - ADAM_B1 ** ADAM_STEP
    bc2 = 1.0 - ADAM_B2 ** ADAM_STEP

    def body(w_ref, g_ref, m_ref, v_ref, d_ref, nm_ref, nv_ref):
        gv = g_ref[...]
        nm = ADAM_B1 * m_ref[...] + (1.0 - ADAM_B1) * gv
        nv = ADAM_B2 * v_ref[...] + (1.0 - ADAM_B2) * (gv * gv)
        d_ref[...] = -ADAM_LR * ((nm / bc1) / (jnp.sqrt(nv / bc2) + ADAM_EPS) + ADAM_WD * w_ref[...])
        nm_ref[...] = nm
        nv_ref[...] = nv

    spec = pl.BlockSpec((tr, C), lambda i: (i, 0))
    out = pl.pallas_call(
        body,
        grid=(R // tr,),
        in_specs=[spec] * 4,
        out_specs=[spec] * 3,
        out_shape=[jax.ShapeDtypeStruct(r2, F32)] * 3,
        compiler_params=_cp("parallel"),
        name=name,
    )(w.reshape(r2), g.reshape(r2), m.reshape(r2), v.reshape(r2))
    return [o.reshape(shape) for o in out]


WEIGHTS = ['attn_norm_g', 'ffn_norm_g', 'ab_w_in', 'ab_w_out', 'ab_rel_bias', 'c_w_in', 'c_ln_g', 'c_ln_b', 'c_w_s', 'c_b_s',
           'c_w_out', 'ffn_w_up', 'ffn_conv_w', 'ffn_conv_b', 'ffn_w_down', 'final_norm_g']
SMALL_ORDER = ["attn_norm_g", "ffn_norm_g", "rel_bias", "ln_g", "ln_b", "w_s", "b_s", "conv_w", "conv_b", "final_g"]
PACK_ROW = 1024


def _pack(arrs):
    flat = jnp.concatenate([a.reshape(-1) for a in arrs])
    n = flat.shape[0]
    padded = -(-n // PACK_ROW) * PACK_ROW
    return jnp.pad(flat, (0, padded - n)).reshape(padded // 128, 128)


def _unpack(flat, shapes):
    out, off = [], 0
    for s in shapes:
        n = int(np.prod(s))
        out.append(flat[off:off + n].reshape(s))
        off += n
    return out


def _step(P):
    x, target = P["x"][0], P["loss_target"][0]
    me = 4 * lax.axis_index("x") + 2 * lax.axis_index("y") + lax.axis_index("c")
    n_up = P["ffn_w_up"].shape[0]
    Fc = P["ffn_conv_w"].shape[-1]
    Lc = P["c_ln_g"].shape[-1]

    first = _Comm("gather", [P["ab_w_in"][0].T.astype(BF16), _pack([P["ffn_conv_w"], P["c_ln_g"], P["c_ln_b"]])])
    hn0, full = _rms_fwd(x, P["attn_norm_g"][0], comm=first, name="rms_fwd_gather")
    wt = {"ab_in_t": full[0]}
    rest = {"ab_out": P["ab_w_out"][0], "c_in_t": P["c_w_in"][0].T, "c_out": P["c_w_out"][0],
            "up_t0": P["ffn_w_up"][0].T, "up_t1": P["ffn_w_up"][1].T, "down0": P["ffn_w_down"][0], "down1": P["ffn_w_down"][1]}
    rest_shards = [rest[k].astype(BF16) for k in REST]
    sm = full[-1].reshape(N_DEV, -1)
    conv_w = sm[:, :n_up * 3 * Fc].reshape(N_DEV, n_up, 3, Fc).transpose(1, 2, 0, 3).reshape(n_up, 3, N_DEV * Fc)
    off = n_up * 3 * Fc
    ln_g = sm[:, off:off + Lc].reshape(N_DEV * Lc)
    ln_b = sm[:, off + Lc:off + 2 * Lc].reshape(N_DEV * Lc)
    small = {"attn_norm_g": P["attn_norm_g"], "ffn_norm_g": P["ffn_norm_g"], "rel_bias": P["ab_rel_bias"][0],
             "ln_g": ln_g, "ln_b": ln_b, "w_s": P["c_w_s"][0], "b_s": P["c_b_s"][0], "conv_w": conv_w,
             "conv_b": P["ffn_conv_b"], "final_g": P["final_norm_g"]}

    loss_part, grad_x, gw, gs, recv = _local_step(x, target, wt, small, rest_shards=rest_shards, overlap=True, hn0=hn0)
    loss = lax.psum(loss_part, ("x", "y", "c"))

    s8 ={k: _sum8(recv[k], name="sum8") for k in ["ab_in_t"] + REST}
    g_big = {"ab_w_in": s8["ab_in_t"].T[None], "ab_w_out": s8["ab_out"][None], "c_w_in": s8["c_in_t"].T[None],
             "c_w_out": s8["c_out"][None], "ffn_w_up": jnp.stack([s8["up_t0"].T, s8["up_t1"].T]),
             "ffn_w_down": jnp.stack([s8["down0"], s8["down1"]])}

    packed = _pack([gs[k] for k in SMALL_ORDER])
    gathered = _all_gather([packed], name="gather_small_grads")[0]
    tot = _sum8(gathered.reshape(N_DEV, packed.shape[0], 128), name="sum8_small").reshape(-1)
    gsm = dict(zip(SMALL_ORDER, _unpack(tot, [gs[k].shape for k in SMALL_ORDER])))
    grads = dict(g_big)
    grads["attn_norm_g"] = gsm["attn_norm_g"]
    grads["ffn_norm_g"] = gsm["ffn_norm_g"]
    grads["ab_rel_bias"] = gsm["rel_bias"][None]
    grads["c_ln_g"] = lax.dynamic_slice(gsm["ln_g"], (me * Lc,), (Lc,))[None]
    grads["c_ln_b"] = lax.dynamic_slice(gsm["ln_b"], (me * Lc,), (Lc,))[None]
    grads["c_w_s"] = gsm["w_s"][None]
    grads["c_b_s"] = gsm["b_s"][None]
    grads["ffn_conv_w"] = lax.dynamic_slice(gsm["conv_w"], (0, 0, me * Fc), (n_up, 3, Fc))
    grads["ffn_conv_b"] = gsm["conv_b"]
    grads["final_norm_g"] = gsm["final_g"]

    delta, new_m, new_v = {}, {}, {}
    for k in WEIGHTS:
        delta[k], new_m[k], new_v[k] = _adamw(P[k], grads[k], P["m_" + k], P["v_" + k], name="adamw")
    return (loss, grad_x[None], *[grads[k] for k in WEIGHTS], *[delta[k] for k in WEIGHTS],
            *[new_m[k] for k in WEIGHTS], *[new_v[k] for k in WEIGHTS])


def kernel(x, attn_norm_g, ffn_norm_g, ab_w_in, ab_w_out, ab_rel_bias, c_w_in, c_ln_g, c_ln_b, c_w_s, c_b_s, c_w_out, ffn_w_up, ffn_conv_w, ffn_conv_b, ffn_w_down, final_norm_g, loss_target, m_attn_norm_g, m_ffn_norm_g, m_ab_w_in, m_ab_w_out, m_ab_rel_bias, m_c_w_in, m_c_ln_g, m_c_ln_b, m_c_w_s, m_c_b_s, m_c_w_out, m_ffn_w_up, m_ffn_conv_w, m_ffn_conv_b, m_ffn_w_down, m_final_norm_g, v_attn_norm_g, v_ffn_norm_g, v_ab_w_in, v_ab_w_out, v_ab_rel_bias, v_c_w_in, v_c_ln_g, v_c_ln_b, v_c_w_s, v_c_b_s, v_c_w_out, v_ffn_w_up, v_ffn_conv_w, v_ffn_conv_b, v_ffn_w_down, v_final_norm_g):
    return _step(dict(locals()))
```

```python
import functools

import numpy as np
import jax
import jax.numpy as jnp
from jax import lax
from jax.experimental import pallas as pl
from jax.experimental.pallas import tpu as pltpu

F32 = jnp.float32
BF16 = jnp.bfloat16

D_MODEL = 1024
CHUNK = 64
EPS = 1e-6
NEG_INF = -1e30
RET_HEADS = 4
RET_QK_DIM = 128
RET_V_DIM = 256
ATT_HEADS = 8
ATT_HEAD_DIM = 64
ATT_PAST = 8
ATT_BAND = (ATT_PAST + 1) * CHUNK
MAX_REL = 128
N_REL = 2 * MAX_REL + 1
N_REL_PAD = 384
SGU_BLOCK = 128
SGU_GROUPS = 8
SGU_WIDTH = 2048
SGU_GW = SGU_WIDTH // SGU_GROUPS
FFN_HIDDEN = 2816
RET_W = 2 * RET_HEADS * RET_QK_DIM + 2 * RET_HEADS * RET_V_DIM
ATT_W = 3 * ATT_HEADS * ATT_HEAD_DIM
N_DEV = 8

ADAM_LR = 0.001
ADAM_B1 = 0.9
ADAM_B2 = 0.999
ADAM_EPS = 1e-08
ADAM_WD = 0.01
ADAM_STEP = 10

VMEM_LIMIT = 52 * 1024 * 1024


def _cp(*sem):
    return pltpu.CompilerParams(dimension_semantics=sem if sem else None, vmem_limit_bytes=VMEM_LIMIT)


def _tile(n, target):
    if n <= target:
        return n
    best = None
    for t in range(128, target + 1, 128):
        if n % t == 0:
            best = t
    assert best is not None, (n, target)
    return best


def _gelu(x):
    c = 0.7978845608028654
    return 0.5 * x * (1.0 + jnp.tanh(c * (x + 0.044715 * x * x * x)))


def _gelu_and_grad(x):
    c = 0.7978845608028654
    x2 = x * x
    t = jnp.tanh(c * (x + 0.044715 * x * x2))
    cdf = 0.5 * (1.0 + t)
    grad = cdf + x * (0.5 * c) * (1.0 - t * t) * (1.0 + 3.0 * 0.044715 * x2)
    return x * cdf, grad


def _dot(a, b, dims):
    return lax.dot_general(a, b, (dims, ((), ())), preferred_element_type=F32)


NN = ((1,), (0,))
NT = ((1,), (1,))
TN = ((0,), (0,))


def _mm(a, b, mode, out_dtype, res=None, name="mm", tm_t=None, tn_t=None, tk_t=None, comm=None):
    if mode == "nt":
        (M, K), N = a.shape, b.shape[0]
        dm, dn, dk = (512, 2816, K) if N <= 2816 else (1024, 1024, K)
    elif mode == "nn":
        (M, K), N = a.shape, b.shape[1]
        dm, dn, dk = (1024 if K <= 3072 else 512), 1024, K
    else:
        (K, M), N = a.shape, b.shape[1]
        dm, dn, dk = 1536, 1024, 2048
    tm, tn, tk = _tile(M, tm_t or dm), _tile(N, tn_t or dn), _tile(K, tk_t or dk)
    nk = K // tk
    dims = {"nt": NT, "nn": NN, "tn": TN}[mode]
    a_spec = pl.BlockSpec((tk, tm), lambda i, j, k: (k, i)) if mode == "tn" else pl.BlockSpec((tm, tk), lambda i, j, k: (i, k))
    b_spec = pl.BlockSpec((tn, tk), lambda i, j, k: (j, k)) if mode == "nt" else pl.BlockSpec((tk, tn), lambda i, j, k: (k, j))
    o_spec = pl.BlockSpec((tm, tn), lambda i, j, k: (i, j))
    has_res = res is not None
    nc = comm.n if comm else 0
    n_in = 3 if has_res else 2
    gi, gj = M // tm, N // tn

    def body(*refs):
        a_ref, b_ref = refs[:2]
        r_ref = refs[2] if has_res else None
        cin = refs[n_in:n_in + nc]
        o_ref = refs[n_in + nc]
        cout = refs[n_in + nc + 1:n_in + 2 * nc + 1]
        scratch = refs[n_in + 2 * nc + 1:]
        csem = scratch[1:] if nk > 1 else scratch
        step = (pl.program_id(0) * gj + pl.program_id(1)) * nk + pl.program_id(2)
        _host_comm(comm, "early", step, gi * gj * nk, cin, cout, csem)
        p = _dot(a_ref[...].astype(BF16), b_ref[...].astype(BF16), dims)
        if nk == 1:
            if has_res:
                p = p + r_ref[...]
            o_ref[...] = p.astype(out_dtype)
        else:
            acc = scratch[0]
            k = pl.program_id(2)

            @pl.when(k == 0)
            def _():
                acc[...] = p

            @pl.when(k > 0)
            def _():
                acc[...] += p

            @pl.when(k == nk - 1)
            def _():
                t = acc[...]
                if has_res:
                    t = t + r_ref[...]
                o_ref[...] = t.astype(out_dtype)
        _host_comm(comm, "late", step, gi * gj * nk, cin, cout, csem)

    in_specs = [a_spec, b_spec] + ([o_spec] if has_res else []) + [ANY] * nc
    args = (a, b) + ((res,) if has_res else ()) + tuple(comm.arrays if comm else ())
    out = pl.pallas_call(
        body,
        grid=(gi, gj, nk),
        in_specs=in_specs,
        out_specs=[o_spec] + [ANY] * nc,
        out_shape=[jax.ShapeDtypeStruct((M, N), out_dtype)] + (comm.out_shape if comm else []),
        scratch_shapes=([pltpu.VMEM((tm, tn), F32)] if nk > 1 else []) + (comm.scratch if comm else []),
        compiler_params=_cp("arbitrary", "arbitrary", "arbitrary") if comm else _cp("parallel", "parallel", "arbitrary"),
        name=name,
    )(*args)
    return (out[0], list(out[1:])) if comm else out[0]


def _mm_rows(a, b, res, name, norm_g=None, bwd=None, loss=None, tm=512):
    M, K = a.shape
    Dm = b.shape[1]
    tm = min(tm, M)
    row = pl.BlockSpec((tm, Dm), lambda i: (i, 0))
    vec = pl.BlockSpec((1, Dm), lambda i: (0, 0))
    a_spec = pl.BlockSpec((tm, K), lambda i: (i, 0))
    b_spec = pl.BlockSpec((K, Dm), lambda i: (0, 0))

    if loss is not None:
        target, g = loss

        def body(a_ref, b_ref, r_ref, g_ref, t_ref, loss_ref, dh_ref, dg_ref):
            @pl.when(pl.program_id(0) == 0)
            def _():
                loss_ref[...] = jnp.zeros_like(loss_ref)
                dg_ref[...] = jnp.zeros_like(dg_ref)

            x = _dot(a_ref[...].astype(BF16), b_ref[...].astype(BF16), NN) + r_ref[...]
            gv = g_ref[...]
            r = lax.rsqrt(jnp.mean(x * x, axis=-1, keepdims=True) + EPS)
            xhat = x * r
            e = xhat * gv - t_ref[...]
            loss_ref[...] += jnp.full((1, 128), 0.5 / Dm, F32) * jnp.sum(e * e)
            dy = e * (1.0 / Dm)
            dg_ref[...] += jnp.sum(dy * xhat, axis=0, keepdims=True)
            dx = dy * gv
            m = jnp.mean(dx * xhat, axis=-1, keepdims=True)
            dh_ref[...] = r * (dx - xhat * m)

        return pl.pallas_call(
            body, grid=(M // tm,), in_specs=[a_spec, b_spec, row, vec, row],
            out_specs=[pl.BlockSpec((1, 128), lambda i: (0, 0)), row, vec],
            out_shape=[jax.ShapeDtypeStruct((1, 128), F32), jax.ShapeDtypeStruct((M, Dm), F32), jax.ShapeDtypeStruct((1, Dm), F32)],
            compiler_params=_cp("arbitrary"), name=name,
        )(a, b, res, g.reshape(1, Dm), target)

    if bwd is None:
        def body(a_ref, b_ref, r_ref, g_ref, o_ref, n_ref):
            t = _dot(a_ref[...].astype(BF16), b_ref[...].astype(BF16), NN) + r_ref[...]
            o_ref[...] = t
            r = lax.rsqrt(jnp.mean(t * t, axis=-1, keepdims=True) + EPS)
            n_ref[...] = (t * r * g_ref[...]).astype(n_ref.dtype)

        return pl.pallas_call(
            body, grid=(M // tm,), in_specs=[a_spec, b_spec, row, vec], out_specs=[row, row],
            out_shape=[jax.ShapeDtypeStruct((M, Dm), F32), jax.ShapeDtypeStruct((M, Dm), BF16)],
            compiler_params=_cp("parallel"), name=name,
        )(a, b, res, norm_g.reshape(1, Dm))

    h, g, dres = bwd
    has_res = res is not None

    def body(*refs):
        a_ref, b_ref = refs[:2]
        h_ref, g_ref, dres_ref, dh_ref, dg_ref = refs[-5:]

        @pl.when(pl.program_id(0) == 0)
        def _():
            dg_ref[...] = jnp.zeros_like(dg_ref)

        d = _dot(a_ref[...].astype(BF16), b_ref[...].astype(BF16), NN)
        if has_res:
            d = d + refs[2][...]
        x = h_ref[...]
        r = lax.rsqrt(jnp.mean(x * x, axis=-1, keepdims=True) + EPS)
        xhat = x * r
        dg_ref[...] += jnp.sum(d * xhat, axis=0, keepdims=True)
        dx = d * g_ref[...]
        m = jnp.mean(dx * xhat, axis=-1, keepdims=True)
        dh_ref[...] = dres_ref[...] + r * (dx - xhat * m)

    return pl.pallas_call(
        body, grid=(M // tm,), in_specs=[a_spec, b_spec] + ([row] if has_res else []) + [row, vec, row], out_specs=[row, vec],
        out_shape=[jax.ShapeDtypeStruct((M, Dm), F32), jax.ShapeDtypeStruct((1, Dm), F32)],
        compiler_params=_cp("arbitrary"), name=name,
    )(a, b, *((res,) if has_res else ()), h, g.reshape(1, Dm), dres)


def _rms_fwd(h, g, comm=None, name="rms_fwd", tm=512):
    T, Dm = h.shape
    tm = min(tm, T)
    nc = comm.n if comm else 0
    ni = T // tm

    def body(*refs):
        h_ref, g_ref = refs[:2]
        cin, o_ref, cout, csem = refs[2:2 + nc], refs[2 + nc], refs[3 + nc:3 + 2 * nc], refs[3 + 2 * nc:]
        step = pl.program_id(0)
        _host_comm(comm, "early", step, ni, cin, cout, csem)
        x = h_ref[...]
        r = lax.rsqrt(jnp.mean(x * x, axis=-1, keepdims=True) + EPS)
        o_ref[...] = (x * r * g_ref[...]).astype(o_ref.dtype)
        _host_comm(comm, "late", step, ni, cin, cout, csem)

    out = pl.pallas_call(
        body,
        grid=(ni,),
        in_specs=[pl.BlockSpec((tm, Dm), lambda i: (i, 0)), pl.BlockSpec((1, Dm), lambda i: (0, 0))] + [ANY] * nc,
        out_specs=[pl.BlockSpec((tm, Dm), lambda i: (i, 0))] + [ANY] * nc,
        out_shape=[jax.ShapeDtypeStruct((T, Dm), BF16)] + (comm.out_shape if comm else []),
        scratch_shapes=comm.scratch if comm else [],
        compiler_params=_cp("arbitrary"),
        name=name,
    )(h, g.reshape(1, Dm), *(comm.arrays if comm else []))
    return (out[0], list(out[1:])) if comm else out[0]


HALO = 16


def _conv3(ext, w_ref, b_ref):
    return w_ref[0:1, :] * pltpu.roll(ext, 2, 0) + w_ref[1:2, :] * pltpu.roll(ext, 1, 0) + w_ref[2:3, :] * ext + b_ref[...]


def _ffn_up_mid(hf, w_up_t, cw, cb, name="ffn_up_mid", tm=512, tc=1408):
    T, Dm = hf.shape
    Fh = w_up_t.shape[0] // 2
    tm = min(tm, T)
    nj, ni = Fh // tc, T // tm

    rc = min(256, tm)

    def body(h_ref, wg_ref, wu_ref, cwg_ref, cwu_ref, cbg_ref, cbu_ref, zg_ref, zu_ref, a_ref, eg_ref, eu_ref, ng_ref, nu_ref):
        @pl.when(pl.program_id(1) == 0)
        def _():
            eg_ref[...] = jnp.zeros_like(eg_ref)
            eu_ref[...] = jnp.zeros_like(eu_ref)

        for r in range(tm // rc):
            ext = slice(r * rc, r * rc + HALO + rc)
            cg = _conv3(eg_ref[ext, :].astype(F32), cwg_ref, cbg_ref)[HALO:]
            cu = _conv3(eu_ref[ext, :].astype(F32), cwu_ref, cbu_ref)[HALO:]
            a_ref[r * rc:(r + 1) * rc, :] = _gelu(cg.astype(BF16)) * cu.astype(BF16)
            h = h_ref[r * rc:(r + 1) * rc, :]
            for w_ref, z_ref, n_ref in ((wg_ref, zg_ref, ng_ref), (wu_ref, zu_ref, nu_ref)):
                z = _dot(h, w_ref[...], NT).astype(BF16)
                z_ref[r * rc:(r + 1) * rc, :] = z
                n_ref[r * rc:(r + 1) * rc, :] = z
        for e_ref, n_ref in ((eg_ref, ng_ref), (eu_ref, nu_ref)):
            e_ref[0:HALO, :] = e_ref[tm:tm + HALO, :]
            e_ref[HALO:HALO + tm, :] = n_ref[...]

    last = ni - 1
    z_spec = pl.BlockSpec((tm, tc), lambda j, i: (jnp.minimum(i, last), j))
    return pl.pallas_call(
        body,
        grid=(nj, ni + 1),
        in_specs=[pl.BlockSpec((tm, Dm), lambda j, i: (jnp.minimum(i, last), 0)),
                  pl.BlockSpec((tc, Dm), lambda j, i: (j, 0)), pl.BlockSpec((tc, Dm), lambda j, i: (nj + j, 0)),
                  pl.BlockSpec((3, tc), lambda j, i: (0, j)), pl.BlockSpec((3, tc), lambda j, i: (0, nj + j)),
                  pl.BlockSpec((1, tc), lambda j, i: (0, j)), pl.BlockSpec((1, tc), lambda j, i: (0, nj + j))],
        out_specs=[z_spec, z_spec, pl.BlockSpec((tm, tc), lambda j, i: (jnp.maximum(i - 1, 0), j))],
        out_shape=[jax.ShapeDtypeStruct((T, Fh), BF16)] * 3,
        scratch_shapes=[pltpu.VMEM((HALO + tm, tc), BF16), pltpu.VMEM((HALO + tm, tc), BF16),
                        pltpu.VMEM((tm, tc), BF16), pltpu.VMEM((tm, tc), BF16)],
        compiler_params=_cp("arbitrary", "arbitrary"),
        name=name,
    )(hf, w_up_t, w_up_t, cw, cw, cb, cb)


def _ffn_mid_bwd(zg, zu, da, cw, cb, comm=None, name="ffn_mid_bwd", tm=512, tc=256):
    T, Fh = zg.shape
    tm = min(tm, T)
    nj = Fh // tc
    hb = tm // HALO
    nhb = T // HALO

    nc = comm.n if comm else 0
    ni = T // tm

    def body(*refs):
        zg_ref, zu_ref, zgp_ref, zup_ref, zgn_ref, zun_ref, da_ref, dan_ref, wg_ref, wu_ref, bg_ref, bu_ref = refs[:12]
        cin = refs[12:12 + nc]
        dzg_ref, dzu_ref, dwg_ref, dwu_ref, dbg_ref, dbu_ref = refs[12 + nc:18 + nc]
        cout = refs[18 + nc:18 + 2 * nc]
        csem = refs[18 + 2 * nc:]
        i = pl.program_id(1)
        step = pl.program_id(0) * ni + i
        _host_comm(comm, "early", step, nj * ni, cin, cout, csem)
        first = i == 0
        last = i == ni - 1

        @pl.when(first)
        def _():
            dwg_ref[...] = jnp.zeros_like(dwg_ref)
            dwu_ref[...] = jnp.zeros_like(dwu_ref)
            dbg_ref[...] = jnp.zeros_like(dbg_ref)
            dbu_ref[...] = jnp.zeros_like(dbu_ref)

        def ext_of(p_ref, c_ref, n_ref):
            p = jnp.where(first, 0.0, p_ref[...].astype(F32))
            return jnp.concatenate([p, c_ref[...].astype(F32), n_ref[...].astype(F32)], axis=0)

        zge = ext_of(zgp_ref, zg_ref, zgn_ref)
        zue = ext_of(zup_ref, zu_ref, zun_ref)
        dan = jnp.where(last, 0.0, dan_ref[...].astype(F32))
        dae = jnp.concatenate([jnp.zeros((HALO, tc), F32), da_ref[...].astype(F32), dan], axis=0)
        zg1, zg2 = pltpu.roll(zge, 1, 0), pltpu.roll(zge, 2, 0)
        zu1, zu2 = pltpu.roll(zue, 1, 0), pltpu.roll(zue, 2, 0)
        cg = wg_ref[0:1, :] * zg2 + wg_ref[1:2, :] * zg1 + wg_ref[2:3, :] * zge + bg_ref[...]
        cu = wu_ref[0:1, :] * zu2 + wu_ref[1:2, :] * zu1 + wu_ref[2:3, :] * zue + bu_ref[...]
        gel, dgel = (t.astype(F32) for t in _gelu_and_grad(cg.astype(BF16)))
        dcg = dae * cu * dgel
        dcu = dae * gel
        lo, hi = HALO, HALO + tm

        def back(dc, taps, w_ref, dz_ref, dw_ref, db_ref):
            n = dc.shape[0]
            dz = w_ref[2:3, :] * dc + w_ref[1:2, :] * pltpu.roll(dc, n - 1, 0) + w_ref[0:1, :] * pltpu.roll(dc, n - 2, 0)
            dz_ref[...] = dz[lo:hi].astype(dz_ref.dtype)
            dcc = dc[lo:hi]
            db_ref[...] += jnp.sum(dcc, axis=0, keepdims=True)
            for k, tap in enumerate(taps):
                dw_ref[k:k + 1, :] += jnp.sum(dcc * tap[lo:hi], axis=0, keepdims=True)

        back(dcg, (zg2, zg1, zge), wg_ref, dzg_ref, dwg_ref, dbg_ref)
        back(dcu, (zu2, zu1, zue), wu_ref, dzu_ref, dwu_ref, dbu_ref)
        _host_comm(comm, "late", step, nj * ni, cin, cout, csem)

    cur = pl.BlockSpec((tm, tc), lambda j, i: (i, j))
    prev = pl.BlockSpec((HALO, tc), lambda j, i: (jnp.maximum(i * hb - 1, 0), j))
    nxt = pl.BlockSpec((HALO, tc), lambda j, i: (jnp.minimum((i + 1) * hb, nhb - 1), j))
    wg = pl.BlockSpec((3, tc), lambda j, i: (0, j))
    wu = pl.BlockSpec((3, tc), lambda j, i: (0, j + nj))
    bg = pl.BlockSpec((1, tc), lambda j, i: (0, j))
    bu = pl.BlockSpec((1, tc), lambda j, i: (0, j + nj))
    dw = pl.BlockSpec((3, tc), lambda j, i: (0, j))
    db = pl.BlockSpec((1, tc), lambda j, i: (0, j))
    out = pl.pallas_call(
        body,
        grid=(nj, ni),
        in_specs=[cur, cur, prev, prev, nxt, nxt, cur, nxt, wg, wu, bg, bu] + [ANY] * nc,
        out_specs=[cur, cur, dw, dw, db, db] + [ANY] * nc,
        out_shape=[jax.ShapeDtypeStruct((T, Fh), BF16), jax.ShapeDtypeStruct((T, Fh), BF16),
                   jax.ShapeDtypeStruct((3, Fh), F32), jax.ShapeDtypeStruct((3, Fh), F32),
                   jax.ShapeDtypeStruct((1, Fh), F32), jax.ShapeDtypeStruct((1, Fh), F32)] + (comm.out_shape if comm else []),
        scratch_shapes=comm.scratch if comm else [],
        compiler_params=_cp("arbitrary", "arbitrary"),
        name=name,
    )(zg, zu, zg, zu, zg, zu, da, da, cw, cw, cb, cb, *(comm.arrays if comm else []))
    dzg, dzu, dwg, dwu, dbg, dbu = out[:6]
    return dzg, dzu, jnp.concatenate([dwg, dwu], axis=1), jnp.concatenate([dbg, dbu], axis=1), list(out[6:])


def _sgu_mask():
    r = lax.broadcasted_iota(jnp.int32, (SGU_BLOCK, SGU_BLOCK), 0)
    c = lax.broadcasted_iota(jnp.int32, (SGU_BLOCK, SGU_BLOCK), 1)
    return (c < CHUNK) | (r >= CHUNK)


def _sgu_fwd(zz, ln_g, ln_b, w_s, b_s, name="sgu_fwd", tm=256):
    T = zz.shape[0]
    tm = min(tm, T)
    W = SGU_WIDTH

    def body(zu_ref, zv_ref, g_ref, b_ref, ws_ref, bs_ref, y_ref):
        u = _gelu(zu_ref[...]).astype(F32)
        v = _gelu(zv_ref[...]).astype(F32)
        mu = jnp.mean(v, axis=-1, keepdims=True)
        xc = v - mu
        rstd = lax.rsqrt(jnp.mean(xc * xc, axis=-1, keepdims=True) + EPS)
        vn = (xc * rstd * g_ref[...] + b_ref[...]).astype(BF16)
        mask = _sgu_mask()
        for g in range(SGU_GROUPS):
            wm = jnp.where(mask, ws_ref[g], 0.0).astype(BF16)
            cs = slice(g * SGU_GW, (g + 1) * SGU_GW)
            for blk in range(tm // SGU_BLOCK):
                rs = slice(blk * SGU_BLOCK, (blk + 1) * SGU_BLOCK)
                mixed = _dot(wm, vn[rs, cs], NN) + bs_ref[g]
                y_ref[rs, cs] = (u[rs, cs] * mixed).astype(y_ref.dtype)

    return pl.pallas_call(
        body,
        grid=(T // tm,),
        in_specs=[pl.BlockSpec((tm, W), lambda i: (i, 0)), pl.BlockSpec((tm, W), lambda i: (i, 1)),
                  pl.BlockSpec((1, W), lambda i: (0, 0)), pl.BlockSpec((1, W), lambda i: (0, 0)),
                  pl.BlockSpec((SGU_GROUPS, SGU_BLOCK, SGU_BLOCK), lambda i: (0, 0, 0)),
                  pl.BlockSpec((SGU_GROUPS, SGU_BLOCK, 1), lambda i: (0, 0, 0))],
        out_specs=pl.BlockSpec((tm, W), lambda i: (i, 0)),
        out_shape=jax.ShapeDtypeStruct((T, W), BF16),
        compiler_params=_cp("parallel"),
        name=name,
    )(zz, zz, ln_g.reshape(1, W), ln_b.reshape(1, W), w_s, b_s.reshape(SGU_GROUPS, SGU_BLOCK, 1))


def _sgu_bwd(zz, dy, ln_g, ln_b, w_s, b_s, name="sgu_bwd", tm=256):
    T = zz.shape[0]
    tm = min(tm, T)
    W = SGU_WIDTH

    def body(zu_ref, zv_ref, dy_ref, g_ref, b_ref, ws_ref, bs_ref, dzz_ref, dws_ref, dbs_ref, dg_ref, db_ref, dvn_ref):
        i = pl.program_id(0)

        @pl.when(i == 0)
        def _():
            dws_ref[...] = jnp.zeros_like(dws_ref)
            dbs_ref[...] = jnp.zeros_like(dbs_ref)
            dg_ref[...] = jnp.zeros_like(dg_ref)
            db_ref[...] = jnp.zeros_like(db_ref)

        u, du_dz = (t.astype(F32) for t in _gelu_and_grad(zu_ref[...]))
        v, dv_dz = (t.astype(F32) for t in _gelu_and_grad(zv_ref[...]))
        mu = jnp.mean(v, axis=-1, keepdims=True)
        xc = v - mu
        rstd = lax.rsqrt(jnp.mean(xc * xc, axis=-1, keepdims=True) + EPS)
        xhat = xc * rstd
        gv = g_ref[...]
        vn = (xhat * gv + b_ref[...]).astype(BF16)
        dyv = dy_ref[...].astype(F32)
        mask = _sgu_mask()
        for g in range(SGU_GROUPS):
            wm = jnp.where(mask, ws_ref[g], 0.0).astype(BF16)
            cs = slice(g * SGU_GW, (g + 1) * SGU_GW)
            dw_acc = jnp.zeros((SGU_BLOCK, SGU_BLOCK), F32)
            db_acc = jnp.zeros((SGU_BLOCK, 1), F32)
            for blk in range(tm // SGU_BLOCK):
                rs = slice(blk * SGU_BLOCK, (blk + 1) * SGU_BLOCK)
                vn_bg = vn[rs, cs]
                mixed = _dot(wm, vn_bg, NN) + bs_ref[g]
                dy_bg = dyv[rs, cs]
                dmixed = dy_bg * u[rs, cs]
                dmb = dmixed.astype(BF16)
                dw_acc += _dot(dmb, vn_bg, NT)
                db_acc += jnp.sum(dmixed, axis=1, keepdims=True)
                dvn_ref[rs, cs] = _dot(wm, dmb, TN)
                dzz_ref[rs, cs] = (dy_bg * mixed * du_dz[rs, cs]).astype(dzz_ref.dtype)
            dws_ref[g] += jnp.where(mask, dw_acc, 0.0)
            dbs_ref[g] += db_acc
        dvn = dvn_ref[...]
        dg_ref[...] += jnp.sum(dvn * xhat, axis=0, keepdims=True)
        db_ref[...] += jnp.sum(dvn, axis=0, keepdims=True)
        dxh = dvn * gv
        m1 = jnp.mean(dxh, axis=-1, keepdims=True)
        m2 = jnp.mean(dxh * xhat, axis=-1, keepdims=True)
        dv = rstd * (dxh - m1 - xhat * m2)
        dzz_ref[:, W:] = (dv * dv_dz).astype(dzz_ref.dtype)

    vec = pl.BlockSpec((1, W), lambda i: (0, 0))
    ws_spec = pl.BlockSpec((SGU_GROUPS, SGU_BLOCK, SGU_BLOCK), lambda i: (0, 0, 0))
    bs_spec = pl.BlockSpec((SGU_GROUPS, SGU_BLOCK, 1), lambda i: (0, 0, 0))
    return pl.pallas_call(
        body,
        grid=(T // tm,),
        in_specs=[pl.BlockSpec((tm, W), lambda i: (i, 0)), pl.BlockSpec((tm, W), lambda i: (i, 1)),
                  pl.BlockSpec((tm, W), lambda i: (i, 0)), vec, vec, ws_spec, bs_spec],
        out_specs=[pl.BlockSpec((tm, 2 * W), lambda i: (i, 0)), ws_spec, bs_spec, vec, vec],
        out_shape=[jax.ShapeDtypeStruct((T, 2 * W), BF16),
                   jax.ShapeDtypeStruct((SGU_GROUPS, SGU_BLOCK, SGU_BLOCK), F32),
                   jax.ShapeDtypeStruct((SGU_GROUPS, SGU_BLOCK, 1), F32),
                   jax.ShapeDtypeStruct((1, W), F32), jax.ShapeDtypeStruct((1, W), F32)],
        scratch_shapes=[pltpu.VMEM((tm, W), F32)],
        compiler_params=_cp("arbitrary"),
        name=name,
    )(zz, zz, dy, ln_g.reshape(1, W), ln_b.reshape(1, W), w_s, b_s.reshape(SGU_GROUPS, SGU_BLOCK, 1))


RET_TR = 256
RET_BLK = 256
QK_SCALE = RET_QK_DIM ** -0.5


def _ret_tables(T):
    half = RET_QK_DIM // 2
    inv = 1.0 / (10000.0 ** jnp.linspace(0.0, 1.0, half, dtype=F32))
    inv2 = jnp.concatenate([inv, inv])[None, :]
    sgn = jnp.concatenate([-jnp.ones((half,), F32), jnp.ones((half,), F32)])[None, :]
    tr = min(RET_TR, T)

    def trig(pos):
        ang = pos.astype(F32)[:, None] * inv2
        return jnp.stack([jnp.cos(ang), jnp.sin(ang), sgn * jnp.sin(ang)])

    tile_tab = jnp.pad(trig(jnp.arange(T // tr) * tr).transpose(1, 0, 2), ((0, 0), (0, 5), (0, 0)))
    row_tab = trig(jnp.arange(tr))
    log_g = jnp.log1p(-jnp.exp2(-5.0 - jnp.arange(RET_HEADS, dtype=F32)))
    idx = jnp.arange(RET_BLK, dtype=F32)
    dist = idx[:, None] - idx[None, :]
    cq, ck = jnp.arange(RET_BLK)[:, None] // CHUNK, jnp.arange(RET_BLK)[None, :] // CHUNK
    expo = jnp.where(ck == cq, jnp.abs(dist), dist)
    d_blk = jnp.where((ck <= cq)[None], jnp.exp(log_g[:, None, None] * expo[None]), 0.0)
    k_dec = jnp.exp(log_g[:, None] * (RET_BLK - 1 - idx)[None, :])[:, :, None]
    q_dec = jnp.exp(log_g[:, None] * (idx + 1.0)[None, :])[:, :, None]
    c_dec = jnp.exp(log_g * RET_BLK)[:, None, None]
    return tile_tab, row_tab, d_blk, q_dec, k_dec, c_dec


def _rot(x, c, s):
    return x * c + pltpu.roll(x, RET_QK_DIM // 2, 1) * s


def _rot_tables(tt_ref, rt_ref):
    ca, sa, ga = tt_ref[0:1, :], tt_ref[1:2, :], tt_ref[2:3, :]
    cb, sb, gb = rt_ref[0], rt_ref[1], rt_ref[2]
    return ca * cb - sa * sb, ga * cb + ca * gb


def _ret_specs(tr, rev, nb):
    ix = (lambda n: nb - 1 - n) if rev else (lambda n: n)
    tt = pl.BlockSpec((None, 8, RET_QK_DIM), lambda n: (ix(n), 0, 0))
    rt = pl.BlockSpec((3, tr, RET_QK_DIM), lambda n: (0, 0, 0))
    dm = pl.BlockSpec((RET_HEADS, RET_BLK, RET_BLK), lambda n: (0, 0, 0))
    dv = pl.BlockSpec((RET_HEADS, RET_BLK, 1), lambda n: (0, 0, 0))
    dc = pl.BlockSpec((RET_HEADS, 1, 1), lambda n: (0, 0, 0))
    return ix, [tt, rt, dm, dv, dv, dc]


def _ret_fwd(z_a, tables, name="ret_fwd"):
    T = z_a.shape[0]
    tr = min(RET_TR, T)
    cpb = tr // RET_BLK
    nb = T // tr
    QW, VW = RET_HEADS * RET_QK_DIM, RET_HEADS * RET_V_DIM
    ix, tab_specs = _ret_specs(tr, False, nb)

    def body(z_ref, tt_ref, rt_ref, dm_ref, qd_ref, kd_ref, cd_ref, y_ref, st_ref, state):
        @pl.when(pl.program_id(0) == 0)
        def _():
            state[...] = jnp.zeros_like(state)

        rot_c, rot_s = _rot_tables(tt_ref, rt_ref)
        for c in range(cpb):
            for h in range(RET_HEADS):
                rs = slice(c * RET_BLK, (c + 1) * RET_BLK)
                cc, ss = rot_c[rs, :], rot_s[rs, :]
                q = z_ref[rs, h * RET_QK_DIM:(h + 1) * RET_QK_DIM].astype(F32)
                k = z_ref[rs, QW + h * RET_QK_DIM:QW + (h + 1) * RET_QK_DIM].astype(F32)
                v = z_ref[rs, 2 * QW + h * RET_V_DIM:2 * QW + (h + 1) * RET_V_DIM]
                gt = z_ref[rs, 2 * QW + VW + h * RET_V_DIM:2 * QW + VW + (h + 1) * RET_V_DIM].astype(F32)
                qr = _rot(q, cc, ss)
                kr = _rot(k, cc, ss) * QK_SCALE
                s_old = state[h]
                sb = s_old.astype(BF16)
                st_ref[c, h] = sb
                s = _dot(qr.astype(BF16), kr.astype(BF16), NT) * dm_ref[h]
                o = _dot(s.astype(BF16), v, NN) + _dot((qr * qd_ref[h]).astype(BF16), sb, NN)
                state[h] = s_old * cd_ref[h] + _dot((kr * kd_ref[h]).astype(BF16), v, TN)
                mu = jnp.mean(o, axis=-1, keepdims=True)
                oc = o - mu
                rn = oc * lax.rsqrt(jnp.mean(oc * oc, axis=-1, keepdims=True) + EPS)
                silu = gt / (1.0 + jnp.exp(-gt))
                y_ref[rs, h * RET_V_DIM:(h + 1) * RET_V_DIM] = (silu * rn).astype(y_ref.dtype)

    return pl.pallas_call(
        body,
        grid=(nb,),
        in_specs=[pl.BlockSpec((tr, RET_W), lambda n: (n, 0))] + tab_specs,
        out_specs=[pl.BlockSpec((tr, VW), lambda n: (n, 0)),
                   pl.BlockSpec((cpb, RET_HEADS, RET_QK_DIM, RET_V_DIM), lambda n: (n, 0, 0, 0))],
        out_shape=[jax.ShapeDtypeStruct((T, Y_COLS), BF16),
                   jax.ShapeDtypeStruct((T // RET_BLK, RET_HEADS, RET_QK_DIM, RET_V_DIM), BF16)],
        scratch_shapes=[pltpu.VMEM((RET_HEADS, RET_QK_DIM, RET_V_DIM), F32)],
        compiler_params=_cp("arbitrary"),
        name=name,
    )(z_a, *tables)


def _ret_bwd(z_a, dy, states, tables, name="ret_bwd"):
    T = z_a.shape[0]
    tr = min(RET_TR, T)
    cpb = tr // RET_BLK
    nb = T // tr
    QW, VW = RET_HEADS * RET_QK_DIM, RET_HEADS * RET_V_DIM
    ix, tab_specs = _ret_specs(tr, True, nb)

    def body(z_ref, dy_ref, st_ref, tt_ref, rt_ref, dm_ref, qd_ref, kd_ref, cd_ref, dz_ref, dstate):
        @pl.when(pl.program_id(0) == 0)
        def _():
            dstate[...] = jnp.zeros_like(dstate)

        rot_c, rot_s = _rot_tables(tt_ref, rt_ref)
        for c in reversed(range(cpb)):
            for h in range(RET_HEADS):
                rs = slice(c * RET_BLK, (c + 1) * RET_BLK)
                cc, ss = rot_c[rs, :], rot_s[rs, :]
                q = z_ref[rs, h * RET_QK_DIM:(h + 1) * RET_QK_DIM].astype(F32)
                k = z_ref[rs, QW + h * RET_QK_DIM:QW + (h + 1) * RET_QK_DIM].astype(F32)
                v = z_ref[rs, 2 * QW + h * RET_V_DIM:2 * QW + (h + 1) * RET_V_DIM]
                gt = z_ref[rs, 2 * QW + VW + h * RET_V_DIM:2 * QW + VW + (h + 1) * RET_V_DIM].astype(F32)
                dyv = dy_ref[rs, h * RET_V_DIM:(h + 1) * RET_V_DIM].astype(F32)
                dmat, qd, kd = dm_ref[h], qd_ref[h], kd_ref[h]
                qr = _rot(q, cc, ss)
                kr = _rot(k, cc, ss) * QK_SCALE
                qrb, krb = qr.astype(BF16), kr.astype(BF16)
                sb = st_ref[c, h]
                sd = (_dot(qrb, krb, NT) * dmat).astype(BF16)
                qdb = (qr * qd).astype(BF16)
                kdb = (kr * kd).astype(BF16)
                o = _dot(sd, v, NN) + _dot(qdb, sb, NN)
                mu = jnp.mean(o, axis=-1, keepdims=True)
                oc = o - mu
                rstd = lax.rsqrt(jnp.mean(oc * oc, axis=-1, keepdims=True) + EPS)
                rn = oc * rstd
                sg = 1.0 / (1.0 + jnp.exp(-gt))
                dgt = dyv * rn * (sg * (1.0 + gt * (1.0 - sg)))
                drn = dyv * (gt * sg)
                do = rstd * (drn - jnp.mean(drn, axis=-1, keepdims=True) - rn * jnp.mean(drn * rn, axis=-1, keepdims=True))
                dob = do.astype(BF16)
                dsn = dstate[h]
                dsnb = dsn.astype(BF16)
                ds_raw = (_dot(dob, v, NT) * dmat).astype(BF16)
                dv = _dot(sd, dob, TN) + _dot(kdb, dsnb, NN)
                dqr = _dot(ds_raw, krb, NN) + qd * _dot(dob, sb, NT)
                dkr = (_dot(ds_raw, qrb, TN) + kd * _dot(v, dsnb, NT)) * QK_SCALE
                dstate[h] = dsn * cd_ref[h] + _dot(qdb, dob, TN)
                dq = dqr * cc + pltpu.roll(dqr * ss, RET_QK_DIM // 2, 1)
                dk = dkr * cc + pltpu.roll(dkr * ss, RET_QK_DIM // 2, 1)
                dz_ref[rs, h * RET_QK_DIM:(h + 1) * RET_QK_DIM] = dq.astype(dz_ref.dtype)
                dz_ref[rs, QW + h * RET_QK_DIM:QW + (h + 1) * RET_QK_DIM] = dk.astype(dz_ref.dtype)
                dz_ref[rs, 2 * QW + h * RET_V_DIM:2 * QW + (h + 1) * RET_V_DIM] = dv.astype(dz_ref.dtype)
                dz_ref[rs, 2 * QW + VW + h * RET_V_DIM:2 * QW + VW + (h + 1) * RET_V_DIM] = dgt.astype(dz_ref.dtype)

    return pl.pallas_call(
        body,
        grid=(nb,),
        in_specs=[pl.BlockSpec((tr, RET_W), lambda n: (ix(n), 0)),
                  pl.BlockSpec((tr, VW), lambda n: (ix(n), 0)),
                  pl.BlockSpec((cpb, RET_HEADS, RET_QK_DIM, RET_V_DIM), lambda n: (ix(n), 0, 0, 0))] + tab_specs,
        out_specs=pl.BlockSpec((tr, RET_W), lambda n: (ix(n), 0)),
        out_shape=jax.ShapeDtypeStruct((T, RET_W), BF16),
        scratch_shapes=[pltpu.VMEM((RET_HEADS, RET_QK_DIM, RET_V_DIM), F32)],
        compiler_params=_cp("arbitrary"),
        name=name,
    )(z_a, dy, states, *tables)


ATT_TQ = 256
ATT_CPB = ATT_TQ // CHUNK
ATT_SCALE = ATT_HEAD_DIM ** -0.5


ATT_WIN = 3 * ATT_TQ
ATT_NB = CHUNK * ATT_BAND


def _rel_index():
    i = np.arange(CHUNK)[:, None]
    j = np.arange(ATT_BAND)[None, :]
    rel = np.clip(i + ATT_PAST * CHUNK - j, -MAX_REL, MAX_REL) + MAX_REL
    return jnp.asarray(rel.reshape(1, ATT_NB).astype(np.int32))


def _split3(x):
    hi = x.astype(BF16)
    r1 = x - hi.astype(F32)
    mid = r1.astype(BF16)
    lo = (r1 - mid.astype(F32)).astype(BF16)
    return hi, mid, lo


REL_TILE = 4608


def _bias_expand(rel_bias, name="bias_expand"):
    H = rel_bias.shape[0]
    n = ATT_NB
    padded = jnp.pad(rel_bias, ((0, 0), (0, N_REL_PAD - N_REL)))

    def body(rb_ref, idx_ref, o_ref):
        onehot = (lax.broadcasted_iota(jnp.int32, (N_REL_PAD, REL_TILE), 0) == idx_ref[...]).astype(BF16)
        hi, mid, lo = _split3(rb_ref[...])
        o_ref[...] = _dot(hi, onehot, NN) + _dot(mid, onehot, NN) + _dot(lo, onehot, NN)

    out = pl.pallas_call(
        body,
        grid=(n // REL_TILE,),
        in_specs=[pl.BlockSpec((H, N_REL_PAD), lambda t: (0, 0)), pl.BlockSpec((1, REL_TILE), lambda t: (0, t))],
        out_specs=pl.BlockSpec((H, REL_TILE), lambda t: (0, t)),
        out_shape=jax.ShapeDtypeStruct((H, n), F32),
        compiler_params=_cp("parallel"),
        name=name,
    )(padded, _rel_index())
    return out.reshape(H, CHUNK, ATT_BAND)


def _bias_tile(band, name="bias_tile"):
    H = band.shape[0]
    padded = jnp.pad(band, ((0, 0), (0, 0), (0, ATT_WIN - ATT_BAND)), constant_values=NEG_INF)

    def body(b_ref, o_ref):
        b = b_ref[...]
        col = lax.broadcasted_iota(jnp.int32, (CHUNK, ATT_WIN), 1)
        keep = col >= (2 - pl.program_id(0)) * ATT_TQ
        for a in range(ATT_CPB):
            o_ref[a * CHUNK:(a + 1) * CHUNK, :] = jnp.where(keep, pltpu.roll(b, a * CHUNK, 1) if a else b, NEG_INF)

    return pl.pallas_call(
        body,
        grid=(3, H),
        in_specs=[pl.BlockSpec((None, CHUNK, ATT_WIN), lambda v, h: (h, 0, 0))],
        out_specs=pl.BlockSpec((None, None, ATT_TQ, ATT_WIN), lambda v, h: (v, h, 0, 0)),
        out_shape=jax.ShapeDtypeStruct((3, H, ATT_TQ, ATT_WIN), F32),
        compiler_params=_cp("parallel", "parallel"),
        name=name,
    )(padded)


def _bias_untile(dtile, name="bias_untile"):
    H = dtile.shape[0]

    def body(d_ref, o_ref):
        acc = d_ref[0:CHUNK, :]
        for a in range(1, ATT_CPB):
            acc = acc + pltpu.roll(d_ref[a * CHUNK:(a + 1) * CHUNK, :], ATT_WIN - a * CHUNK, 1)
        o_ref[...] = acc

    out = pl.pallas_call(
        body,
        grid=(H,),
        in_specs=[pl.BlockSpec((None, ATT_TQ, ATT_WIN), lambda h: (h, 0, 0))],
        out_specs=pl.BlockSpec((None, CHUNK, ATT_WIN), lambda h: (h, 0, 0)),
        out_shape=jax.ShapeDtypeStruct((H, CHUNK, ATT_WIN), F32),
        compiler_params=_cp("parallel"),
        name=name,
    )(dtile)
    return out[:, :, :ATT_BAND]


def _bias_reduce(dbias, name="bias_reduce"):
    H = dbias.shape[0]
    n = ATT_NB

    def body(db_ref, idx_ref, o_ref):
        @pl.when(pl.program_id(0) == 0)
        def _():
            o_ref[...] = jnp.zeros_like(o_ref)

        onehot = (lax.broadcasted_iota(jnp.int32, (N_REL_PAD, REL_TILE), 0) == idx_ref[...]).astype(BF16)
        hi, mid, lo = _split3(db_ref[...])
        o_ref[...] += _dot(hi, onehot, NT) + _dot(mid, onehot, NT) + _dot(lo, onehot, NT)

    out = pl.pallas_call(
        body,
        grid=(n // REL_TILE,),
        in_specs=[pl.BlockSpec((H, REL_TILE), lambda t: (0, t)), pl.BlockSpec((1, REL_TILE), lambda t: (0, t))],
        out_specs=pl.BlockSpec((H, N_REL_PAD), lambda t: (0, 0)),
        out_shape=jax.ShapeDtypeStruct((H, N_REL_PAD), F32),
        compiler_params=_cp("arbitrary"),
        name=name,
    )(dbias.reshape(H, n), _rel_index())
    return out[:, :N_REL]


def _att_probs(q, kwin, bias):
    s = _dot(q, kwin, NT) + bias
    e = jnp.exp(s - jnp.max(s, axis=-1, keepdims=True))
    return e * (1.0 / jnp.sum(e, axis=-1, keepdims=True))


ATT_PAIR = 2 * ATT_HEAD_DIM
ATT_NP = ATT_HEADS // 2
ATT_QW = ATT_HEADS * ATT_HEAD_DIM
Y_COLS = RET_HEADS * RET_V_DIM + ATT_QW


def _att_specs(tq, nq, clip_q, q_col0):
    cb = ATT_QW // ATT_PAIR
    qi = (lambda p, m: (jnp.minimum(m, nq - 1), q_col0 + p)) if clip_q else (lambda p, m: (m, q_col0 + p))
    q = pl.BlockSpec((tq, ATT_PAIR), qi)

    def win(col0):
        return [pl.BlockSpec((tq, ATT_PAIR), functools.partial(lambda p, m, back: (jnp.clip(m - back, 0, nq - 1), col0 + p), back=b))
                for b in (2, 1, 0)]

    bias = pl.BlockSpec((None, 2, ATT_TQ, ATT_WIN), lambda p, m: (jnp.minimum(m, 2), p, 0, 0))
    return q, win(cb), win(2 * cb), bias


def _head_masks(rows):
    lane = lax.broadcasted_iota(jnp.int32, (rows, ATT_PAIR), 1)
    return lane < ATT_HEAD_DIM


def _att_fwd(z_b, bias, y, comm=None, name="att_fwd"):
    T = z_b.shape[0]
    tq = ATT_TQ
    nq = T // tq
    qs, kwin, vwin, bs = _att_specs(tq, nq, False, 0)
    nc = comm.n if comm else 0
    total = ATT_NP * nq

    def body(*refs):
        q_ref, k0, k1, k2, v0, v1, v2, b_ref = refs[:8]
        cin = refs[9:9 + nc]
        o_ref = refs[9 + nc]
        cout = refs[10 + nc:10 + 2 * nc]
        csem = refs[10 + 2 * nc:]
        m = pl.program_id(1)
        step = pl.program_id(0) * nq + m
        _host_comm(comm, "early", step, total, cin, cout, csem)
        kw = jnp.concatenate([k0[...], k1[...], k2[...]], axis=0)
        vw = jnp.concatenate([v0[...], v1[...], v2[...]], axis=0)
        q2 = q_ref[...] * ATT_SCALE
        even = _head_masks(tq)
        outs = []
        for hh in range(2):
            qm = jnp.where(even if hh == 0 else ~even, q2, jnp.zeros_like(q2))
            p = _att_probs(qm, kw, b_ref[hh])
            outs.append(_dot(p.astype(BF16), vw, NN))
        o_ref[...] = jnp.where(even, outs[0], outs[1]).astype(o_ref.dtype)
        _host_comm(comm, "late", step, total, cin, cout, csem)

    y_cb = (Y_COLS - ATT_QW) // ATT_PAIR
    out = pl.pallas_call(
        body,
        grid=(ATT_NP, nq),
        in_specs=[qs] + kwin + vwin + [bs, ANY] + [ANY] * nc,
        out_specs=[pl.BlockSpec((tq, ATT_PAIR), lambda p, m: (m, y_cb + p))] + [ANY] * nc,
        out_shape=[jax.ShapeDtypeStruct((T, Y_COLS), BF16)] + (comm.out_shape if comm else []),
        scratch_shapes=comm.scratch if comm else [],
        input_output_aliases={8: 0},
        compiler_params=_cp("arbitrary", "arbitrary"),
        name=name,
    )(z_b, z_b, z_b, z_b, z_b, z_b, z_b, bias, y, *(comm.arrays if comm else []))
    return out[0], list(out[1:])


def _att_bwd(z_b, bias, dy, comm=None, name="att_bwd"):
    T = z_b.shape[0]
    tq = ATT_TQ
    nq = T // tq
    y_cb = (Y_COLS - ATT_QW) // ATT_PAIR
    qs, kwin, vwin, bs = _att_specs(tq, nq, True, 0)
    dos = _att_specs(tq, nq, True, y_cb)[0]
    kv_out = pl.BlockSpec((tq, ATT_PAIR), lambda p, m: (jnp.maximum(m - 2, 0), p))
    W3 = 3 * tq
    nc = comm.n if comm else 0
    total = ATT_NP * (nq + 2)

    def body(*refs):
        q_ref, k0, k1, k2, v0, v1, v2, b_ref, do_ref = refs[:9]
        cin = refs[9:9 + nc]
        dq_ref, dk_ref, dv_ref, db_ref = refs[9 + nc:13 + nc]
        cout = refs[13 + nc:13 + 2 * nc]
        dkc, dvc, dkw, dvw = refs[13 + 2 * nc:17 + 2 * nc]
        csem = refs[17 + 2 * nc:]
        m = pl.program_id(1)
        step = pl.program_id(0) * (nq + 2) + m
        _host_comm(comm, "early", step, total, cin, cout, csem)

        @pl.when(m == 0)
        def _():
            dkc[...] = jnp.zeros_like(dkc)
            dvc[...] = jnp.zeros_like(dvc)
            db_ref[...] = jnp.zeros_like(db_ref)

        @pl.when(m >= nq)
        def _():
            dkw[...] = jnp.zeros_like(dkw)
            dvw[...] = jnp.zeros_like(dvw)

        @pl.when(m < nq)
        def _():
            kw = jnp.concatenate([k0[...], k1[...], k2[...]], axis=0)
            vw = jnp.concatenate([v0[...], v1[...], v2[...]], axis=0)
            q2, do2 = q_ref[...] * ATT_SCALE, do_ref[...]
            even = _head_masks(tq)
            dqs, dks, dvs = [], [], []
            for hh in range(2):
                mine = even if hh == 0 else ~even
                p = _att_probs(jnp.where(mine, q2, jnp.zeros_like(q2)), kw, b_ref[hh])
                dp = _dot(jnp.where(mine, do2, jnp.zeros_like(do2)), vw, NT)
                ds = p * (dp - jnp.sum(dp * p, axis=-1, keepdims=True))
                db_ref[hh] += ds
                dsb = ds.astype(BF16)
                dqs.append(_dot(dsb, kw, NN))
                dks.append(_dot(dsb, q2, TN))
                dvs.append(_dot(p.astype(BF16), do2, TN))
            even_w = _head_masks(W3)
            dq_ref[...] = (jnp.where(even, dqs[0], dqs[1]) * ATT_SCALE).astype(dq_ref.dtype)
            dkw[...] = jnp.where(even_w, dks[0], dks[1])
            dvw[...] = jnp.where(even_w, dvs[0], dvs[1])

        dk_ref[...] = (dkc[0:tq, :] + dkw[0:tq, :]).astype(dk_ref.dtype)
        dv_ref[...] = (dvc[0:tq, :] + dvw[0:tq, :]).astype(dv_ref.dtype)
        dkc[0:tq, :] = dkc[tq:2 * tq, :] + dkw[tq:2 * tq, :]
        dvc[0:tq, :] = dvc[tq:2 * tq, :] + dvw[tq:2 * tq, :]
        dkc[tq:2 * tq, :] = dkw[2 * tq:W3, :]
        dvc[tq:2 * tq, :] = dvw[2 * tq:W3, :]
        _host_comm(comm, "late", step, total, cin, cout, csem)

    qo = pl.BlockSpec((tq, ATT_PAIR), lambda p, m: (jnp.minimum(m, nq - 1), p))
    dbs = pl.BlockSpec((2, ATT_TQ, ATT_WIN), lambda p, m: (p, 0, 0))
    hd = jax.ShapeDtypeStruct((T, ATT_QW), BF16)
    out = pl.pallas_call(
        body,
        grid=(ATT_NP, nq + 2),
        in_specs=[qs] + kwin + vwin + [bs, dos] + [ANY] * nc,
        out_specs=[qo, kv_out, kv_out, dbs] + [ANY] * nc,
        out_shape=[hd, hd, hd, jax.ShapeDtypeStruct((ATT_HEADS, ATT_TQ, ATT_WIN), F32)] + (comm.out_shape if comm else []),
        scratch_shapes=[pltpu.VMEM((2 * tq, ATT_PAIR), F32), pltpu.VMEM((2 * tq, ATT_PAIR), F32),
                        pltpu.VMEM((W3, ATT_PAIR), F32), pltpu.VMEM((W3, ATT_PAIR), F32)] + (comm.scratch if comm else []),
        compiler_params=_cp("arbitrary", "arbitrary"),
        name=name,
    )(z_b, z_b, z_b, z_b, z_b, z_b, z_b, bias, dy, *(comm.arrays if comm else []))
    return out[0], out[1], out[2], out[3], list(out[4:])


REST = ["ab_out", "c_in_t", "c_out", "up_t0", "up_t1", "down0", "down1"]


def _local_step(x, target, wt, small, rest_shards=None, overlap=False, hn0=None):
    T = x.shape[0]
    Fh = FFN_HIDDEN
    tables = _ret_tables(T)
    gw, gs, recv = {}, {}, {}
    wt = dict(wt)

    if hn0 is None:
        hn0 = _rms_fwd(x, small["attn_norm_g"][0], name="rms_fwd")
    z_a = _mm(hn0, wt["ab_in_t"][:RET_W], "nt", BF16, name="mm_ab_in_a")
    z_b = _mm(hn0, wt["ab_in_t"][RET_W:], "nt", BF16, name="mm_ab_in_b")
    y, states = _ret_fwd(z_a, tables)
    bias = _bias_tile(_bias_expand(small["rel_bias"]))
    y, rest = _att_fwd(z_b, bias, y, comm=_Comm("gather", rest_shards) if rest_shards is not None else None)
    if rest_shards is not None:
        full = dict(zip(REST, rest))
        wt.update(ab_out=full["ab_out"], c_in_t=full["c_in_t"], c_out=full["c_out"],
                  up_t=[full["up_t0"], full["up_t1"]], down=[full["down0"], full["down1"]])
    h1, hf0 = _mm_rows(y, wt["ab_out"], x, "mm_ab_out", norm_g=small["ffn_norm_g"][0])

    def ffn_fwd(h, hf, layer, next_g):
        zg, zu, a = _ffn_up_mid(hf, wt["up_t"][layer], small["conv_w"][layer], small["conv_b"][layer][None, :])
        if next_g is None:
            return _mm_rows(a, wt["down"][layer], h, "mm_down_loss", loss=(target, small["final_g"])), (hf, zg, zu, a)
        h_out, hn_next = _mm_rows(a, wt["down"][layer], h, "mm_down", norm_g=next_g)
        return h_out, hn_next, (hf, zg, zu, a)

    def ffn_bwd(dh_out, h, layer, saved, exchange=None):
        hf, zg, zu, a = saved
        da = _mm(dh_out, wt["down"][layer], "nt", BF16, name="mm_d_a")
        d_down = _mm(a, dh_out, "tn", BF16, name="mm_dw_down")
        comm = _Comm("exchange", [gw[k] for k in exchange]) if exchange else None
        dzg, dzu, dcw, dcb, got = _ffn_mid_bwd(zg, zu, da, small["conv_w"][layer], small["conv_b"][layer][None, :], comm=comm)
        recv.update(zip(exchange or [], got))
        dhf = _mm(dzg, wt["up_t"][layer][:Fh], "nn", F32, name="mm_d_hf")
        dh, dg = _mm_rows(dzu, wt["up_t"][layer][Fh:], dhf, "mm_d_hf_norm", bwd=(h, small["ffn_norm_g"][layer], dh_out))
        d_up = jnp.concatenate([_mm(dzg, hf, "tn", BF16, name="mm_dw_up"), _mm(dzu, hf, "tn", BF16, name="mm_dw_up")], axis=0)
        return dh, dg, d_up, d_down, dcw, dcb

    h2, hn1, ffn0 = ffn_fwd(h1, hf0, 0, small["attn_norm_g"][1])

    zz = _mm(hn1, wt["c_in_t"], "nt", BF16, name="mm_c_in")
    ys = _sgu_fwd(zz, small["ln_g"], small["ln_b"], small["w_s"], small["b_s"])
    h3, hf1 = _mm_rows(ys, wt["c_out"], h2, "mm_c_out", norm_g=small["ffn_norm_g"][1])
    (loss_vec, dh4, gs["final_g"]), ffn1 = ffn_fwd(h3, hf1, 1, None)

    dh3, dgf1, gw["up_t1"], gw["down1"], dcw1, dcb1 = ffn_bwd(dh4, h3, 1, ffn1)
    dys = _mm(dh3, wt["c_out"], "nt", BF16, name="mm_d_ys")
    gw["c_out"] = _mm(ys, dh3, "tn", BF16, name="mm_dw_c_out")
    dzz, gs["w_s"], dbs, dlg, dlb = _sgu_bwd(zz, dys, small["ln_g"], small["ln_b"], small["w_s"], small["b_s"])
    gs["b_s"], gs["ln_g"], gs["ln_b"] = dbs[:, :, 0], dlg[0], dlb[0]
    dh2, dga1 = _mm_rows(dzz, wt["c_in_t"], None, "mm_d_hn1_norm", bwd=(h2, small["attn_norm_g"][1], dh3))
    gw["c_in_t"] = _mm(dzz, hn1, "tn", BF16, name="mm_dw_c_in")

    dh1, dgf0, gw["up_t0"], gw["down0"], dcw0, dcb0 = ffn_bwd(
        dh2, h1, 0, ffn0, exchange=["c_in_t", "c_out", "up_t1", "down1"] if overlap else None)

    dy = _mm(dh1, wt["ab_out"], "nt", BF16, name="mm_d_y")
    gw["ab_out"] = _mm(y, dh1, "tn", BF16, name="mm_dw_ab_out")
    dz_a = _ret_bwd(z_a, dy, states, tables)
    late = ["ab_out", "up_t0", "down0"] if overlap else []
    dq, dk, dv, dbias, got = _att_bwd(z_b, bias, dy, comm=_Comm("exchange", [gw[k] for k in late]) if late else None)
    recv.update(zip(late, got))
    dz_b = jnp.concatenate([dq, dk, dv], axis=1)
    gs["rel_bias"] = _bias_reduce(_bias_untile(dbias))
    gw["ab_in_t"] = jnp.concatenate([_mm(dz_a, hn0, "tn", BF16, name="mm_dw_ab_in_a"),
                                     _mm(dz_b, hn0, "tn", BF16, name="mm_dw_ab_in_b")], axis=0)
    if overlap:
        dhn0, got = _mm(dz_a, wt["ab_in_t"][:RET_W], "nn", F32, name="mm_d_hn0", comm=_Comm("exchange", [gw["ab_in_t"]]))
        recv["ab_in_t"] = got[0]
    else:
        dhn0 = _mm(dz_a, wt["ab_in_t"][:RET_W], "nn", F32, name="mm_d_hn0")
    grad_x, dga0 = _mm_rows(dz_b, wt["ab_in_t"][RET_W:], dhn0, "mm_d_hn0_norm", bwd=(x, small["attn_norm_g"][0], dh1))

    gs["attn_norm_g"] = jnp.concatenate([dga0, dga1], axis=0)
    gs["ffn_norm_g"] = jnp.concatenate([dgf0, dgf1], axis=0)
    gs["conv_w"] = jnp.stack([dcw0, dcw1])
    gs["conv_b"] = jnp.concatenate([dcb0, dcb1], axis=0)
    gs["final_g"] = gs["final_g"][0]
    return loss_vec[0, 0], grad_x, gw, gs, recv


MESH_ID = pl.DeviceIdType.MESH
ANY = pl.BlockSpec(memory_space=pl.ANY)


def _my_place():
    return lax.axis_index("x"), lax.axis_index("y"), lax.axis_index("c")


class _Comm:
    def __init__(self, kind, arrays):
        self.kind, self.arrays, self.n = kind, list(arrays), len(arrays)
        if kind == "gather":
            self.out_shape = [jax.ShapeDtypeStruct((N_DEV * s.shape[0], s.shape[1]), s.dtype) for s in arrays]
        else:
            self.out_shape = [jax.ShapeDtypeStruct((N_DEV, g.shape[0] // N_DEV, g.shape[1]), g.dtype) for g in arrays]
        n = self.n
        self.scratch = [pltpu.SemaphoreType.DMA((n, 7)), pltpu.SemaphoreType.DMA((n, 7)), pltpu.SemaphoreType.DMA((n,))]

    def phase(self, ph, in_refs, out_refs, sems):
        (self._gather if self.kind == "gather" else self._exchange)(ph, in_refs, out_refs, sems)

    def _gather(self, ph, x_refs, o_refs, sems):
        n = self.n
        send_sems, recv_sems, local_sems = sems
        x, y, c = _my_place()
        me, sibling = (x, y, c), (x, y, 1 - c)
        chips = [(1 - x, y), (x, 1 - y), (1 - x, 1 - y)]

        def rows(a, place):
            m = x_refs[a].shape[0]
            px, py, pc = place
            return o_refs[a].at[pl.ds((4 * px + 2 * py + pc) * m, m), :]

        def copy(a, k, block, to, own=False):
            return pltpu.make_async_remote_copy(
                src_ref=x_refs[a] if own else rows(a, block), dst_ref=rows(a, block),
                send_sem=send_sems.at[a, k], recv_sem=recv_sems.at[a, k], device_id=to, device_id_type=MESH_ID)

        def mine():
            return [pltpu.make_async_copy(x_refs[a], rows(a, me), local_sems.at[a]) for a in range(n)]

        def first():
            out = []
            for a in range(n):
                out.append(copy(a, 0, me, sibling, own=True))
                out += [copy(a, 1 + j, me, (*chip, c), own=True) for j, chip in enumerate(chips)]
            return out

        def passed():
            return [copy(a, 4 + j, (*chip, c), sibling) for j, chip in enumerate(chips) for a in range(n)]

        if ph == 0:
            for cp in mine() + first():
                cp.start()
        elif ph == 1:
            fw = passed()
            for j, chip in enumerate(chips):
                for a in range(n):
                    copy(a, 1 + j, (*chip, c), me).wait_recv()
                    fw[j * n + a].start()
        else:
            for a in range(n):
                copy(a, 0, sibling, me).wait_recv()
                for j, chip in enumerate(chips):
                    copy(a, 4 + j, (*chip, 1 - c), me).wait_recv()
            for cp in first() + passed():
                cp.wait_send()
            for cp in mine():
                cp.wait()

    def _exchange(self, ph, g_refs, o_refs, sems):
        n = self.n
        send_sems, recv_sems, local_sems = sems
        x, y, c = _my_place()
        me = 4 * x + 2 * y + c
        peers = [(x ^ ((k >> 2) & 1), y ^ ((k >> 1) & 1), c ^ (k & 1)) for k in range(1, N_DEV)]

        def block(a, idx):
            m = g_refs[a].shape[0] // N_DEV
            return g_refs[a].at[pl.ds(idx * m, m), :]

        def copy(a, k, slot):
            px, py, pc = peers[k]
            return pltpu.make_async_remote_copy(
                src_ref=block(a, 4 * px + 2 * py + pc), dst_ref=o_refs[a].at[slot],
                send_sem=send_sems.at[a, k], recv_sem=recv_sems.at[a, k], device_id=peers[k], device_id_type=MESH_ID)

        if ph == 1:
            return
        mine = [pltpu.make_async_copy(block(a, me), o_refs[a].at[me], local_sems.at[a]) for a in range(n)]
        sends = [copy(a, k, me) for k in range(N_DEV - 1) for a in range(n)]
        if ph == 0:
            for cp in mine + sends:
                cp.start()
        else:
            for k in range(N_DEV - 1):
                px, py, pc = peers[k]
                for a in range(n):
                    copy(a, k, 4 * px + 2 * py + pc).wait_recv()
            for cp in sends:
                cp.wait_send()
            for cp in mine:
                cp.wait()


def _comm_call(comm, name):
    n = comm.n

    def body(*refs):
        for ph in range(3):
            comm.phase(ph, refs[:n], refs[n:2 * n], refs[2 * n:])

    return pl.pallas_call(
        body, out_shape=comm.out_shape, in_specs=[ANY] * n, out_specs=[ANY] * n, scratch_shapes=comm.scratch, name=name,
    )(*comm.arrays)


def _host_comm(comm, when, step, total, cin, cout, csem):
    if comm is None:
        return
    sched = {0: 0, 1: (3 * total) // 4, 2: total - 1}
    phases = (0, 1) if when == "early" else (2,)
    for ph in phases:
        if ph == 1 and comm.kind == "exchange":
            continue

        @pl.when(step == sched[ph])
        def _(ph=ph):
            comm.phase(ph, cin, cout, csem)


def _all_gather(shards, name="all_gather"):
    return _comm_call(_Comm("gather", shards), name)


def _row_tile(r, target=256):
    best = None
    for t in range(8, min(r, target) + 1, 8):
        if r % t == 0:
            best = t
    return best if best is not None else r


def _sum8(parts, name="sum8"):
    _, M, N = parts.shape
    tr = _row_tile(M, 128)

    def body(p_ref, o_ref):
        acc = p_ref[0].astype(F32)
        for d in range(1, N_DEV):
            acc = acc + p_ref[d].astype(F32)
        o_ref[...] = acc

    return pl.pallas_call(
        body,
        grid=(M // tr,),
        in_specs=[pl.BlockSpec((N_DEV, tr, N), lambda i: (0, i, 0))],
        out_specs=pl.BlockSpec((tr, N), lambda i: (i, 0)),
        out_shape=jax.ShapeDtypeStruct((M, N), F32),
        compiler_params=_cp("parallel"),
        name=name,
    )(parts)


def _adamw(w, g, m, v, name="adamw"):
    shape = w.shape
    if w.ndim == 1:
        r2 = (1, shape[0])
    else:
        r2 = (int(np.prod(shape[:-1])), shape[-1])
    R, C = r2
    tr = _row_tile(R)
    bc1 = 1.0 - ADAM_B1 ** ADAM_STEP
    bc2 = 1.0 - ADAM_B2 ** ADAM_STEP

    def body(w_ref, g_ref, m_ref, v_ref, d_ref, nm_ref, nv_ref):
        gv = g_ref[...]
        nm = ADAM_B1 * m_ref[...] + (1.0 - ADAM_B1) * gv
        nv = ADAM_B2 * v_ref[...] + (1.0 - ADAM_B2) * (gv * gv)
        d_ref[...] = -ADAM_LR * ((nm / bc1) / (jnp.sqrt(nv / bc2) + ADAM_EPS) + ADAM_WD * w_ref[...])
        nm_ref[...] = nm
        nv_ref[...] = nv

    spec = pl.BlockSpec((tr, C), lambda i: (i, 0))
    out = pl.pallas_call(
        body,
        grid=(R // tr,),
        in_specs=[spec] * 4,
        out_specs=[spec] * 3,
        out_shape=[jax.ShapeDtypeStruct(r2, F32)] * 3,
        compiler_params=_cp("parallel"),
        name=name,
    )(w.reshape(r2), g.reshape(r2), m.reshape(r2), v.reshape(r2))
    return [o.reshape(shape) for o in out]


WEIGHTS = ['attn_norm_g', 'ffn_norm_g', 'ab_w_in', 'ab_w_out', 'ab_rel_bias', 'c_w_in', 'c_ln_g', 'c_ln_b', 'c_w_s', 'c_b_s',
           'c_w_out', 'ffn_w_up', 'ffn_conv_w', 'ffn_conv_b', 'ffn_w_down', 'final_norm_g']
SMALL_ORDER = ["attn_norm_g", "ffn_norm_g", "rel_bias", "ln_g", "ln_b", "w_s", "b_s", "conv_w", "conv_b", "final_g"]
PACK_ROW = 1024


def _pack(arrs):
    flat = jnp.concatenate([a.reshape(-1) for a in arrs])
    n = flat.shape[0]
    padded = -(-n // PACK_ROW) * PACK_ROW
    return jnp.pad(flat, (0, padded - n)).reshape(padded // 128, 128)


def _unpack(flat, shapes):
    out, off = [], 0
    for s in shapes:
        n = int(np.prod(s))
        out.append(flat[off:off + n].reshape(s))
        off += n
    return out


def _step(P):
    x, target = P["x"][0], P["loss_target"][0]
    me = 4 * lax.axis_index("x") + 2 * lax.axis_index("y") + lax.axis_index("c")
    n_up = P["ffn_w_up"].shape[0]
    Fc = P["ffn_conv_w"].shape[-1]
    Lc = P["c_ln_g"].shape[-1]

    first = _Comm("gather", [P["ab_w_in"][0].T.astype(BF16), _pack([P["ffn_conv_w"], P["c_ln_g"], P["c_ln_b"]])])
    hn0, full = _rms_fwd(x, P["attn_norm_g"][0], comm=first, name="rms_fwd_gather")
    wt = {"ab_in_t": full[0]}
    rest = {"ab_out": P["ab_w_out"][0], "c_in_t": P["c_w_in"][0].T, "c_out": P["c_w_out"][0],
            "up_t0": P["ffn_w_up"][0].T, "up_t1": P["ffn_w_up"][1].T, "down0": P["ffn_w_down"][0], "down1": P["ffn_w_down"][1]}
    rest_shards = [rest[k].astype(BF16) for k in REST]
    sm = full[-1].reshape(N_DEV, -1)
    conv_w = sm[:, :n_up * 3 * Fc].reshape(N_DEV, n_up, 3, Fc).transpose(1, 2, 0, 3).reshape(n_up, 3, N_DEV * Fc)
    off = n_up * 3 * Fc
    ln_g = sm[:, off:off + Lc].reshape(N_DEV * Lc)
    ln_b = sm[:, off + Lc:off + 2 * Lc].reshape(N_DEV * Lc)
    small = {"attn_norm_g": P["attn_norm_g"], "ffn_norm_g": P["ffn_norm_g"], "rel_bias": P["ab_rel_bias"][0],
             "ln_g": ln_g, "ln_b": ln_b, "w_s": P["c_w_s"][0], "b_s": P["c_b_s"][0], "conv_w": conv_w,
             "conv_b": P["ffn_conv_b"], "final_g": P["final_norm_g"]}

    loss_part, grad_x, gw, gs, recv = _local_step(x, target, wt, small, rest_shards=rest_shards, overlap=True, hn0=hn0)
    loss = lax.psum(loss_part, ("x", "y", "c"))

    s8 ={k: _sum8(recv[k], name="sum8") for k in ["ab_in_t"] + REST}
    g_big = {"ab_w_in": s8["ab_in_t"].T[None], "ab_w_out": s8["ab_out"][None], "c_w_in": s8["c_in_t"].T[None],
             "c_w_out": s8["c_out"][None], "ffn_w_up": jnp.stack([s8["up_t0"].T, s8["up_t1"].T]),
             "ffn_w_down": jnp.stack([s8["down0"], s8["down1"]])}

    packed = _pack([gs[k] for k in SMALL_ORDER])
    gathered = _all_gather([packed], name="gather_small_grads")[0]
    tot = _sum8(gathered.reshape(N_DEV, packed.shape[0], 128), name="sum8_small").reshape(-1)
    gsm = dict(zip(SMALL_ORDER, _unpack(tot, [gs[k].shape for k in SMALL_ORDER])))
    grads = dict(g_big)
    grads["attn_norm_g"] = gsm["attn_norm_g"]
    grads["ffn_norm_g"] = gsm["ffn_norm_g"]
    grads["ab_rel_bias"] = gsm["rel_bias"][None]
    grads["c_ln_g"] = lax.dynamic_slice(gsm["ln_g"], (me * Lc,), (Lc,))[None]
    grads["c_ln_b"] = lax.dynamic_slice(gsm["ln_b"], (me * Lc,), (Lc,))[None]
    grads["c_w_s"] = gsm["w_s"][None]
    grads["c_b_s"] = gsm["b_s"][None]
    grads["ffn_conv_w"] = lax.dynamic_slice(gsm["conv_w"], (0, 0, me * Fc), (n_up, 3, Fc))
    grads["ffn_conv_b"] = gsm["conv_b"]
    grads["final_norm_g"] = gsm["final_g"]

    delta, new_m, new_v = {}, {}, {}
    for k in WEIGHTS:
        delta[k], new_m[k], new_v[k] = _adamw(P[k], grads[k], P["m_" + k], P["v_" + k], name="adamw")
    return (loss, grad_x[None], *[grads[k] for k in WEIGHTS], *[delta[k] for k in WEIGHTS],
            *[new_m[k] for k in WEIGHTS], *[new_v[k] for k in WEIGHTS])


def kernel(x, attn_norm_g, ffn_norm_g, ab_w_in, ab_w_out, ab_rel_bias, c_w_in, c_ln_g, c_ln_b, c_w_s, c_b_s, c_w_out, ffn_w_up, ffn_conv_w, ffn_conv_b, ffn_w_down, final_norm_g, loss_target, m_attn_norm_g, m_ffn_norm_g, m_ab_w_in, m_ab_w_out, m_ab_rel_bias, m_c_w_in, m_c_ln_g, m_c_ln_b, m_c_w_s, m_c_b_s, m_c_w_out, m_ffn_w_up, m_ffn_conv_w, m_ffn_conv_b, m_ffn_w_down, m_final_norm_g, v_attn_norm_g, v_ffn_norm_g, v_ab_w_in, v_ab_w_out, v_ab_rel_bias, v_c_w_in, v_c_ln_g, v_c_ln_b, v_c_w_s, v_c_b_s, v_c_w_out, v_ffn_w_up, v_ffn_conv_w, v_ffn_conv_b, v_ffn_w_down, v_final_norm_g):
    return _step(dict(locals()))
```

```python
import functools

import numpy as np
import jax
import jax.numpy as jnp
from jax import lax
from jax.experimental import pallas as pl
from jax.experimental.pallas import tpu as pltpu

F32 = jnp.float32
BF16 = jnp.bfloat16

D_MODEL = 1024
CHUNK = 64
EPS = 1e-6
NEG_INF = -1e30
RET_HEADS = 4
RET_QK_DIM = 128
RET_V_DIM = 256
ATT_HEADS = 8
ATT_HEAD_DIM = 64
ATT_PAST = 8
ATT_BAND = (ATT_PAST + 1) * CHUNK
MAX_REL = 128
N_REL = 2 * MAX_REL + 1
N_REL_PAD = 384
SGU_BLOCK = 128
SGU_GROUPS = 8
SGU_WIDTH = 2048
SGU_GW = SGU_WIDTH // SGU_GROUPS
FFN_HIDDEN = 2816
RET_W = 2 * RET_HEADS * RET_QK_DIM + 2 * RET_HEADS * RET_V_DIM
ATT_W = 3 * ATT_HEADS * ATT_HEAD_DIM
N_DEV = 8

ADAM_LR = 0.001
ADAM_B1 = 0.9
ADAM_B2 = 0.999
ADAM_EPS = 1e-08
ADAM_WD = 0.01
ADAM_STEP = 10

VMEM_LIMIT = 52 * 1024 * 1024


def _cp(*sem):
    return pltpu.CompilerParams(dimension_semantics=sem if sem else None, vmem_limit_bytes=VMEM_LIMIT)


def _tile(n, target):
    if n <= target:
        return n
    best = None
    for t in range(128, target + 1, 128):
        if n % t == 0:
            best = t
    assert best is not None, (n, target)
    return best


def _gelu(x):
    c = 0.7978845608028654
    return 0.5 * x * (1.0 + jnp.tanh(c * (x + 0.044715 * x * x * x)))


def _gelu_and_grad(x):
    c = 0.7978845608028654
    x2 = x * x
    t = jnp.tanh(c * (x + 0.044715 * x * x2))
    cdf = 0.5 * (1.0 + t)
    grad = cdf + x * (0.5 * c) * (1.0 - t * t) * (1.0 + 3.0 * 0.044715 * x2)
    return x * cdf, grad


def _dot(a, b, dims):
    return lax.dot_general(a, b, (dims, ((), ())), preferred_element_type=F32)


NN = ((1,), (0,))
NT = ((1,), (1,))
TN = ((0,), (0,))


def _mm(a, b, mode, out_dtype, res=None, name="mm", tm_t=None, tn_t=None, tk_t=None, comm=None, rows=None):
    if mode == "nt":
        (M, K), N = a.shape, b.shape[0]
        dm, dn, dk = (512, 2816, K) if N <= 2816 else (1024, 1024, K)
    elif mode == "nn":
        (M, K), N = a.shape, b.shape[1]
        dm, dn, dk = (1024 if K <= 3072 else 512), 1024, K
    else:
        (K, M), N = a.shape, b.shape[1]
        dm, dn, dk = 1536, 1024, 2048
    tm, tn, tk = _tile(M, tm_t or dm), _tile(N, tn_t or dn), _tile(K, tk_t or dk)
    nk = K // tk
    dims = {"nt": NT, "nn": NN, "tn": TN}[mode]
    a_spec = pl.BlockSpec((tk, tm), lambda i, j, k: (k, i)) if mode == "tn" else pl.BlockSpec((tm, tk), lambda i, j, k: (i, k))
    b_spec = pl.BlockSpec((tn, tk), lambda i, j, k: (j, k)) if mode == "nt" else pl.BlockSpec((tk, tn), lambda i, j, k: (k, j))
    gi, gj = M // tm, N // tn
    out_rows, row0, into = rows if rows else (M, 0, None)
    assert row0 % tm == 0 and not (comm and into is not None)
    o_spec = pl.BlockSpec((tm, tn), lambda i, j, k: (i + row0 // tm, j))
    has_res = res is not None
    nc = 1 if into is not None else (comm.n if comm else 0)
    n_in = 3 if has_res else 2

    def body(*refs):
        a_ref, b_ref = refs[:2]
        r_ref = refs[2] if has_res else None
        nco = comm.n if comm else 0
        cin = refs[n_in:n_in + nc]
        o_ref = refs[n_in + nc]
        cout = refs[n_in + nc + 1:n_in + nc + 1 + nco]
        scratch = refs[n_in + nc + 1 + nco:]
        csem = scratch[1:] if nk > 1 else scratch
        step = (pl.program_id(0) * gj + pl.program_id(1)) * nk + pl.program_id(2)
        _host_comm(comm, "early", step, gi * gj * nk, cin, cout, csem)
        p = _dot(a_ref[...].astype(BF16), b_ref[...].astype(BF16), dims)
        if nk == 1:
            if has_res:
                p = p + r_ref[...]
            o_ref[...] = p.astype(out_dtype)
        else:
            acc = scratch[0]
            k = pl.program_id(2)

            @pl.when(k == 0)
            def _():
                acc[...] = p

            @pl.when(k > 0)
            def _():
                acc[...] += p

            @pl.when(k == nk - 1)
            def _():
                t = acc[...]
                if has_res:
                    t = t + r_ref[...]
                o_ref[...] = t.astype(out_dtype)
        _host_comm(comm, "late", step, gi * gj * nk, cin, cout, csem)

    in_specs = [a_spec, b_spec] + ([o_spec] if has_res else []) + [ANY] * nc
    args = (a, b) + ((res,) if has_res else ()) + ((into,) if into is not None else tuple(comm.arrays if comm else ()))
    out = pl.pallas_call(
        body,
        grid=(gi, gj, nk),
        in_specs=in_specs,
        out_specs=[o_spec] + ([ANY] * nc if comm else []),
        out_shape=[jax.ShapeDtypeStruct((out_rows, N), out_dtype)] + (comm.out_shape if comm else []),
        scratch_shapes=([pltpu.VMEM((tm, tn), F32)] if nk > 1 else []) + (comm.scratch if comm else []),
        input_output_aliases={n_in: 0} if into is not None else {},
        compiler_params=_cp("arbitrary", "arbitrary", "arbitrary") if comm else _cp("parallel", "parallel", "arbitrary"),
        name=name,
    )(*args)
    return (out[0], list(out[1:])) if comm else out[0]


def _mm_rows(a, b, res, name, norm_g=None, bwd=None, loss=None, second=None, tm=512):
    M, K = a.shape
    Dm = b.shape[1]
    tm = min(tm, M)
    row = pl.BlockSpec((tm, Dm), lambda i: (i, 0))
    vec = pl.BlockSpec((1, Dm), lambda i: (0, 0))
    a_spec = pl.BlockSpec((tm, K), lambda i: (i, 0))
    b_spec = pl.BlockSpec((K, Dm), lambda i: (0, 0))

    if loss is not None:
        target, g = loss

        def body(a_ref, b_ref, r_ref, g_ref, t_ref, loss_ref, dh_ref, dg_ref):
            @pl.when(pl.program_id(0) == 0)
            def _():
                loss_ref[...] = jnp.zeros_like(loss_ref)
                dg_ref[...] = jnp.zeros_like(dg_ref)

            x = _dot(a_ref[...].astype(BF16), b_ref[...].astype(BF16), NN) + r_ref[...]
            gv = g_ref[...]
            r = lax.rsqrt(jnp.mean(x * x, axis=-1, keepdims=True) + EPS)
            xhat = x * r
            e = xhat * gv - t_ref[...]
            loss_ref[...] += jnp.full((1, 128), 0.5 / Dm, F32) * jnp.sum(e * e)
            dy = e * (1.0 / Dm)
            dg_ref[...] += jnp.sum(dy * xhat, axis=0, keepdims=True)
            dx = dy * gv
            m = jnp.mean(dx * xhat, axis=-1, keepdims=True)
            dh_ref[...] = r * (dx - xhat * m)

        return pl.pallas_call(
            body, grid=(M // tm,), in_specs=[a_spec, b_spec, row, vec, row],
            out_specs=[pl.BlockSpec((1, 128), lambda i: (0, 0)), row, vec],
            out_shape=[jax.ShapeDtypeStruct((1, 128), F32), jax.ShapeDtypeStruct((M, Dm), F32), jax.ShapeDtypeStruct((1, Dm), F32)],
            compiler_params=_cp("arbitrary"), name=name,
        )(a, b, res, g.reshape(1, Dm), target)

    if bwd is None:
        def body(a_ref, b_ref, r_ref, g_ref, o_ref, n_ref):
            t = _dot(a_ref[...].astype(BF16), b_ref[...].astype(BF16), NN) + r_ref[...]
            o_ref[...] = t
            r = lax.rsqrt(jnp.mean(t * t, axis=-1, keepdims=True) + EPS)
            n_ref[...] = (t * r * g_ref[...]).astype(n_ref.dtype)

        return pl.pallas_call(
            body, grid=(M // tm,), in_specs=[a_spec, b_spec, row, vec], out_specs=[row, row],
            out_shape=[jax.ShapeDtypeStruct((M, Dm), F32), jax.ShapeDtypeStruct((M, Dm), BF16)],
            compiler_params=_cp("parallel"), name=name,
        )(a, b, res, norm_g.reshape(1, Dm))

    h, g, dres = bwd
    has_res = res is not None
    has2 = second is not None

    def body(*refs):
        a_ref, b_ref = refs[:2]
        h_ref, g_ref, dres_ref, dh_ref, dg_ref = refs[-5:]

        @pl.when(pl.program_id(0) == 0)
        def _():
            dg_ref[...] = jnp.zeros_like(dg_ref)

        d = _dot(a_ref[...].astype(BF16), b_ref[...].astype(BF16), NN)
        if has2:
            d = d + _dot(refs[2][...].astype(BF16), refs[3][...].astype(BF16), NN)
        if has_res:
            d = d + refs[4 if has2 else 2][...]
        x = h_ref[...]
        r = lax.rsqrt(jnp.mean(x * x, axis=-1, keepdims=True) + EPS)
        xhat = x * r
        dg_ref[...] += jnp.sum(d * xhat, axis=0, keepdims=True)
        dx = d * g_ref[...]
        m = jnp.mean(dx * xhat, axis=-1, keepdims=True)
        dh_ref[...] = dres_ref[...] + r * (dx - xhat * m)

    second_specs = [pl.BlockSpec((tm, second[0].shape[1]), lambda i: (i, 0)),
                    pl.BlockSpec(second[1].shape, lambda i: (0, 0))] if has2 else []
    return pl.pallas_call(
        body, grid=(M // tm,), in_specs=[a_spec, b_spec] + second_specs + ([row] if has_res else []) + [row, vec, row],
        out_specs=[row, vec],
        out_shape=[jax.ShapeDtypeStruct((M, Dm), F32), jax.ShapeDtypeStruct((1, Dm), F32)],
        compiler_params=_cp("arbitrary"), name=name,
    )(a, b, *(second if has2 else ()), *((res,) if has_res else ()), h, g.reshape(1, Dm), dres)


def _rms_fwd(h, g, comm=None, name="rms_fwd", tm=512):
    T, Dm = h.shape
    tm = min(tm, T)
    nc = comm.n if comm else 0
    ni = T // tm

    def body(*refs):
        h_ref, g_ref = refs[:2]
        cin, o_ref, cout, csem = refs[2:2 + nc], refs[2 + nc], refs[3 + nc:3 + 2 * nc], refs[3 + 2 * nc:]
        step = pl.program_id(0)
        _host_comm(comm, "early", step, ni, cin, cout, csem)
        x = h_ref[...]
        r = lax.rsqrt(jnp.mean(x * x, axis=-1, keepdims=True) + EPS)
        o_ref[...] = (x * r * g_ref[...]).astype(o_ref.dtype)
        _host_comm(comm, "late", step, ni, cin, cout, csem)

    out = pl.pallas_call(
        body,
        grid=(ni,),
        in_specs=[pl.BlockSpec((tm, Dm), lambda i: (i, 0)), pl.BlockSpec((1, Dm), lambda i: (0, 0))] + [ANY] * nc,
        out_specs=[pl.BlockSpec((tm, Dm), lambda i: (i, 0))] + [ANY] * nc,
        out_shape=[jax.ShapeDtypeStruct((T, Dm), BF16)] + (comm.out_shape if comm else []),
        scratch_shapes=comm.scratch if comm else [],
        compiler_params=_cp("arbitrary"),
        name=name,
    )(h, g.reshape(1, Dm), *(comm.arrays if comm else []))
    return (out[0], list(out[1:])) if comm else out[0]


HALO = 16


def _conv3(ext, w_ref, b_ref):
    return w_ref[0:1, :] * pltpu.roll(ext, 2, 0) + w_ref[1:2, :] * pltpu.roll(ext, 1, 0) + w_ref[2:3, :] * ext + b_ref[...]


def _ffn_up_mid(hf, w_up_t, cw, cb, name="ffn_up_mid", tm=512, tc=1408):
    T, Dm = hf.shape
    Fh = w_up_t.shape[0] // 2
    tm = min(tm, T)
    nj, ni = Fh // tc, T // tm

    rc = min(256, tm)

    def body(h_ref, wg_ref, wu_ref, cwg_ref, cwu_ref, cbg_ref, cbu_ref, zg_ref, zu_ref, a_ref, eg_ref, eu_ref, ng_ref, nu_ref):
        @pl.when(pl.program_id(1) == 0)
        def _():
            eg_ref[...] = jnp.zeros_like(eg_ref)
            eu_ref[...] = jnp.zeros_like(eu_ref)

        for r in range(tm // rc):
            ext = slice(r * rc, r * rc + HALO + rc)
            cg = _conv3(eg_ref[ext, :].astype(F32), cwg_ref, cbg_ref)[HALO:]
            cu = _conv3(eu_ref[ext, :].astype(F32), cwu_ref, cbu_ref)[HALO:]
            a_ref[r * rc:(r + 1) * rc, :] = _gelu(cg.astype(BF16)) * cu.astype(BF16)
            h = h_ref[r * rc:(r + 1) * rc, :]
            for w_ref, z_ref, n_ref in ((wg_ref, zg_ref, ng_ref), (wu_ref, zu_ref, nu_ref)):
                z = _dot(h, w_ref[...], NT).astype(BF16)
                z_ref[r * rc:(r + 1) * rc, :] = z
                n_ref[r * rc:(r + 1) * rc, :] = z
        for e_ref, n_ref in ((eg_ref, ng_ref), (eu_ref, nu_ref)):
            e_ref[0:HALO, :] = e_ref[tm:tm + HALO, :]
            e_ref[HALO:HALO + tm, :] = n_ref[...]

    last = ni - 1
    z_spec = pl.BlockSpec((tm, tc), lambda j, i: (jnp.minimum(i, last), j))
    return pl.pallas_call(
        body,
        grid=(nj, ni + 1),
        in_specs=[pl.BlockSpec((tm, Dm), lambda j, i: (jnp.minimum(i, last), 0)),
                  pl.BlockSpec((tc, Dm), lambda j, i: (j, 0)), pl.BlockSpec((tc, Dm), lambda j, i: (nj + j, 0)),
                  pl.BlockSpec((3, tc), lambda j, i: (0, j)), pl.BlockSpec((3, tc), lambda j, i: (0, nj + j)),
                  pl.BlockSpec((1, tc), lambda j, i: (0, j)), pl.BlockSpec((1, tc), lambda j, i: (0, nj + j))],
        out_specs=[z_spec, z_spec, pl.BlockSpec((tm, tc), lambda j, i: (jnp.maximum(i - 1, 0), j))],
        out_shape=[jax.ShapeDtypeStruct((T, Fh), BF16)] * 3,
        scratch_shapes=[pltpu.VMEM((HALO + tm, tc), BF16), pltpu.VMEM((HALO + tm, tc), BF16),
                        pltpu.VMEM((tm, tc), BF16), pltpu.VMEM((tm, tc), BF16)],
        compiler_params=_cp("arbitrary", "arbitrary"),
        name=name,
    )(hf, w_up_t, w_up_t, cw, cw, cb, cb)


def _ffn_mid_bwd(zg, zu, da, cw, cb, comm=None, name="ffn_mid_bwd", tm=512, tc=256):
    T, Fh = zg.shape
    tm = min(tm, T)
    nj = Fh // tc
    hb = tm // HALO
    nhb = T // HALO

    nc = comm.n if comm else 0
    ni = T // tm

    def body(*refs):
        zg_ref, zu_ref, zgp_ref, zup_ref, zgn_ref, zun_ref, da_ref, dan_ref, wg_ref, wu_ref, bg_ref, bu_ref = refs[:12]
        cin = refs[12:12 + nc]
        dzg_ref, dzu_ref, dwg_ref, dwu_ref, dbg_ref, dbu_ref = refs[12 + nc:18 + nc]
        cout = refs[18 + nc:18 + 2 * nc]
        csem = refs[18 + 2 * nc:]
        i = pl.program_id(1)
        step = pl.program_id(0) * ni + i
        _host_comm(comm, "early", step, nj * ni, cin, cout, csem)
        first = i == 0
        last = i == ni - 1

        @pl.when(first)
        def _():
            dwg_ref[...] = jnp.zeros_like(dwg_ref)
            dwu_ref[...] = jnp.zeros_like(dwu_ref)
            dbg_ref[...] = jnp.zeros_like(dbg_ref)
            dbu_ref[...] = jnp.zeros_like(dbu_ref)

        def ext_of(p_ref, c_ref, n_ref):
            p = jnp.where(first, 0.0, p_ref[...].astype(F32))
            return jnp.concatenate([p, c_ref[...].astype(F32), n_ref[...].astype(F32)], axis=0)

        zge = ext_of(zgp_ref, zg_ref, zgn_ref)
        zue = ext_of(zup_ref, zu_ref, zun_ref)
        dan = jnp.where(last, 0.0, dan_ref[...].astype(F32))
        dae = jnp.concatenate([jnp.zeros((HALO, tc), F32), da_ref[...].astype(F32), dan], axis=0)
        zg1, zg2 = pltpu.roll(zge, 1, 0), pltpu.roll(zge, 2, 0)
        zu1, zu2 = pltpu.roll(zue, 1, 0), pltpu.roll(zue, 2, 0)
        cg = wg_ref[0:1, :] * zg2 + wg_ref[1:2, :] * zg1 + wg_ref[2:3, :] * zge + bg_ref[...]
        cu = wu_ref[0:1, :] * zu2 + wu_ref[1:2, :] * zu1 + wu_ref[2:3, :] * zue + bu_ref[...]
        gel, dgel = (t.astype(F32) for t in _gelu_and_grad(cg.astype(BF16)))
        dcg = dae * cu * dgel
        dcu = dae * gel
        lo, hi = HALO, HALO + tm

        def back(dc, taps, w_ref, dz_ref, dw_ref, db_ref):
            n = dc.shape[0]
            dz = w_ref[2:3, :] * dc + w_ref[1:2, :] * pltpu.roll(dc, n - 1, 0) + w_ref[0:1, :] * pltpu.roll(dc, n - 2, 0)
            dz_ref[...] = dz[lo:hi].astype(dz_ref.dtype)
            dcc = dc[lo:hi]
            db_ref[...] += jnp.sum(dcc, axis=0, keepdims=True)
            for k, tap in enumerate(taps):
                dw_ref[k:k + 1, :] += jnp.sum(dcc * tap[lo:hi], axis=0, keepdims=True)

        back(dcg, (zg2, zg1, zge), wg_ref, dzg_ref, dwg_ref, dbg_ref)
        back(dcu, (zu2, zu1, zue), wu_ref, dzu_ref, dwu_ref, dbu_ref)
        _host_comm(comm, "late", step, nj * ni, cin, cout, csem)

    cur = pl.BlockSpec((tm, tc), lambda j, i: (i, j))
    prev = pl.BlockSpec((HALO, tc), lambda j, i: (jnp.maximum(i * hb - 1, 0), j))
    nxt = pl.BlockSpec((HALO, tc), lambda j, i: (jnp.minimum((i + 1) * hb, nhb - 1), j))
    wg = pl.BlockSpec((3, tc), lambda j, i: (0, j))
    wu = pl.BlockSpec((3, tc), lambda j, i: (0, j + nj))
    bg = pl.BlockSpec((1, tc), lambda j, i: (0, j))
    bu = pl.BlockSpec((1, tc), lambda j, i: (0, j + nj))
    dw = pl.BlockSpec((3, tc), lambda j, i: (0, j))
    db = pl.BlockSpec((1, tc), lambda j, i: (0, j))
    out = pl.pallas_call(
        body,
        grid=(nj, ni),
        in_specs=[cur, cur, prev, prev, nxt, nxt, cur, nxt, wg, wu, bg, bu] + [ANY] * nc,
        out_specs=[cur, cur, dw, dw, db, db] + [ANY] * nc,
        out_shape=[jax.ShapeDtypeStruct((T, Fh), BF16), jax.ShapeDtypeStruct((T, Fh), BF16),
                   jax.ShapeDtypeStruct((3, Fh), F32), jax.ShapeDtypeStruct((3, Fh), F32),
                   jax.ShapeDtypeStruct((1, Fh), F32), jax.ShapeDtypeStruct((1, Fh), F32)] + (comm.out_shape if comm else []),
        scratch_shapes=comm.scratch if comm else [],
        compiler_params=_cp("arbitrary", "arbitrary"),
        name=name,
    )(zg, zu, zg, zu, zg, zu, da, da, cw, cw, cb, cb, *(comm.arrays if comm else []))
    dzg, dzu, dwg, dwu, dbg, dbu = out[:6]
    return dzg, dzu, jnp.concatenate([dwg, dwu], axis=1), jnp.concatenate([dbg, dbu], axis=1), list(out[6:])


def _sgu_mask():
    r = lax.broadcasted_iota(jnp.int32, (SGU_BLOCK, SGU_BLOCK), 0)
    c = lax.broadcasted_iota(jnp.int32, (SGU_BLOCK, SGU_BLOCK), 1)
    return (c < CHUNK) | (r >= CHUNK)


def _sgu_fwd(zz, ln_g, ln_b, w_s, b_s, name="sgu_fwd", tm=256):
    T = zz.shape[0]
    tm = min(tm, T)
    W = SGU_WIDTH

    def body(zu_ref, zv_ref, g_ref, b_ref, ws_ref, bs_ref, y_ref):
        u = _gelu(zu_ref[...]).astype(F32)
        v = _gelu(zv_ref[...]).astype(F32)
        mu = jnp.mean(v, axis=-1, keepdims=True)
        xc = v - mu
        rstd = lax.rsqrt(jnp.mean(xc * xc, axis=-1, keepdims=True) + EPS)
        vn = (xc * rstd * g_ref[...] + b_ref[...]).astype(BF16)
        mask = _sgu_mask()
        for g in range(SGU_GROUPS):
            wm = jnp.where(mask, ws_ref[g], 0.0).astype(BF16)
            cs = slice(g * SGU_GW, (g + 1) * SGU_GW)
            for blk in range(tm // SGU_BLOCK):
                rs = slice(blk * SGU_BLOCK, (blk + 1) * SGU_BLOCK)
                mixed = _dot(wm, vn[rs, cs], NN) + bs_ref[g]
                y_ref[rs, cs] = (u[rs, cs] * mixed).astype(y_ref.dtype)

    return pl.pallas_call(
        body,
        grid=(T // tm,),
        in_specs=[pl.BlockSpec((tm, W), lambda i: (i, 0)), pl.BlockSpec((tm, W), lambda i: (i, 1)),
                  pl.BlockSpec((1, W), lambda i: (0, 0)), pl.BlockSpec((1, W), lambda i: (0, 0)),
                  pl.BlockSpec((SGU_GROUPS, SGU_BLOCK, SGU_BLOCK), lambda i: (0, 0, 0)),
                  pl.BlockSpec((SGU_GROUPS, SGU_BLOCK, 1), lambda i: (0, 0, 0))],
        out_specs=pl.BlockSpec((tm, W), lambda i: (i, 0)),
        out_shape=jax.ShapeDtypeStruct((T, W), BF16),
        compiler_params=_cp("parallel"),
        name=name,
    )(zz, zz, ln_g.reshape(1, W), ln_b.reshape(1, W), w_s, b_s.reshape(SGU_GROUPS, SGU_BLOCK, 1))


def _sgu_bwd(zz, dy, ln_g, ln_b, w_s, b_s, name="sgu_bwd", tm=256):
    T = zz.shape[0]
    tm = min(tm, T)
    W = SGU_WIDTH

    def body(zu_ref, zv_ref, dy_ref, g_ref, b_ref, ws_ref, bs_ref, dzz_ref, dws_ref, dbs_ref, dg_ref, db_ref, dvn_ref):
        i = pl.program_id(0)

        @pl.when(i == 0)
        def _():
            dws_ref[...] = jnp.zeros_like(dws_ref)
            dbs_ref[...] = jnp.zeros_like(dbs_ref)
            dg_ref[...] = jnp.zeros_like(dg_ref)
            db_ref[...] = jnp.zeros_like(db_ref)

        u, du_dz = (t.astype(F32) for t in _gelu_and_grad(zu_ref[...]))
        v, dv_dz = (t.astype(F32) for t in _gelu_and_grad(zv_ref[...]))
        mu = jnp.mean(v, axis=-1, keepdims=True)
        xc = v - mu
        rstd = lax.rsqrt(jnp.mean(xc * xc, axis=-1, keepdims=True) + EPS)
        xhat = xc * rstd
        gv = g_ref[...]
        vn = (xhat * gv + b_ref[...]).astype(BF16)
        dyv = dy_ref[...].astype(F32)
        mask = _sgu_mask()
        for g in range(SGU_GROUPS):
            wm = jnp.where(mask, ws_ref[g], 0.0).astype(BF16)
            cs = slice(g * SGU_GW, (g + 1) * SGU_GW)
            dw_acc = jnp.zeros((SGU_BLOCK, SGU_BLOCK), F32)
            db_acc = jnp.zeros((SGU_BLOCK, 1), F32)
            for blk in range(tm // SGU_BLOCK):
                rs = slice(blk * SGU_BLOCK, (blk + 1) * SGU_BLOCK)
                vn_bg = vn[rs, cs]
                mixed = _dot(wm, vn_bg, NN) + bs_ref[g]
                dy_bg = dyv[rs, cs]
                dmixed = dy_bg * u[rs, cs]
                dmb = dmixed.astype(BF16)
                dw_acc += _dot(dmb, vn_bg, NT)
                db_acc += jnp.sum(dmixed, axis=1, keepdims=True)
                dvn_ref[rs, cs] = _dot(wm, dmb, TN)
                dzz_ref[rs, cs] = (dy_bg * mixed * du_dz[rs, cs]).astype(dzz_ref.dtype)
            dws_ref[g] += jnp.where(mask, dw_acc, 0.0)
            dbs_ref[g] += db_acc
        dvn = dvn_ref[...]
        dg_ref[...] += jnp.sum(dvn * xhat, axis=0, keepdims=True)
        db_ref[...] += jnp.sum(dvn, axis=0, keepdims=True)
        dxh = dvn * gv
        m1 = jnp.mean(dxh, axis=-1, keepdims=True)
        m2 = jnp.mean(dxh * xhat, axis=-1, keepdims=True)
        dv = rstd * (dxh - m1 - xhat * m2)
        dzz_ref[:, W:] = (dv * dv_dz).astype(dzz_ref.dtype)

    vec = pl.BlockSpec((1, W), lambda i: (0, 0))
    ws_spec = pl.BlockSpec((SGU_GROUPS, SGU_BLOCK, SGU_BLOCK), lambda i: (0, 0, 0))
    bs_spec = pl.BlockSpec((SGU_GROUPS, SGU_BLOCK, 1), lambda i: (0, 0, 0))
    return pl.pallas_call(
        body,
        grid=(T // tm,),
        in_specs=[pl.BlockSpec((tm, W), lambda i: (i, 0)), pl.BlockSpec((tm, W), lambda i: (i, 1)),
                  pl.BlockSpec((tm, W), lambda i: (i, 0)), vec, vec, ws_spec, bs_spec],
        out_specs=[pl.BlockSpec((tm, 2 * W), lambda i: (i, 0)), ws_spec, bs_spec, vec, vec],
        out_shape=[jax.ShapeDtypeStruct((T, 2 * W), BF16),
                   jax.ShapeDtypeStruct((SGU_GROUPS, SGU_BLOCK, SGU_BLOCK), F32),
                   jax.ShapeDtypeStruct((SGU_GROUPS, SGU_BLOCK, 1), F32),
                   jax.ShapeDtypeStruct((1, W), F32), jax.ShapeDtypeStruct((1, W), F32)],
        scratch_shapes=[pltpu.VMEM((tm, W), F32)],
        compiler_params=_cp("arbitrary"),
        name=name,
    )(zz, zz, dy, ln_g.reshape(1, W), ln_b.reshape(1, W), w_s, b_s.reshape(SGU_GROUPS, SGU_BLOCK, 1))


RET_TR = 256
RET_BLK = 256
QK_SCALE = RET_QK_DIM ** -0.5


def _ret_tables(T):
    half = RET_QK_DIM // 2
    inv = 1.0 / (10000.0 ** jnp.linspace(0.0, 1.0, half, dtype=F32))
    inv2 = jnp.concatenate([inv, inv])[None, :]
    sgn = jnp.concatenate([-jnp.ones((half,), F32), jnp.ones((half,), F32)])[None, :]
    tr = min(RET_TR, T)

    def trig(pos):
        ang = pos.astype(F32)[:, None] * inv2
        return jnp.stack([jnp.cos(ang), jnp.sin(ang), sgn * jnp.sin(ang)])

    tile_tab = jnp.pad(trig(jnp.arange(T // tr) * tr).transpose(1, 0, 2), ((0, 0), (0, 5), (0, 0)))
    row_tab = trig(jnp.arange(tr))
    log_g = jnp.log1p(-jnp.exp2(-5.0 - jnp.arange(RET_HEADS, dtype=F32)))
    idx = jnp.arange(RET_BLK, dtype=F32)
    dist = idx[:, None] - idx[None, :]
    cq, ck = jnp.arange(RET_BLK)[:, None] // CHUNK, jnp.arange(RET_BLK)[None, :] // CHUNK
    expo = jnp.where(ck == cq, jnp.abs(dist), dist)
    d_blk = jnp.where((ck <= cq)[None], jnp.exp(log_g[:, None, None] * expo[None]), 0.0)
    k_dec = jnp.exp(log_g[:, None] * (RET_BLK - 1 - idx)[None, :])[:, :, None]
    q_dec = jnp.exp(log_g[:, None] * (idx + 1.0)[None, :])[:, :, None]
    c_dec = jnp.exp(log_g * RET_BLK)[:, None, None]
    return tile_tab, row_tab, d_blk, q_dec, k_dec, c_dec


def _rot(x, c, s):
    return x * c + pltpu.roll(x, RET_QK_DIM // 2, 1) * s


def _rot_tables(tt_ref, rt_ref):
    ca, sa, ga = tt_ref[0:1, :], tt_ref[1:2, :], tt_ref[2:3, :]
    cb, sb, gb = rt_ref[0], rt_ref[1], rt_ref[2]
    return ca * cb - sa * sb, ga * cb + ca * gb


def _ret_specs(tr, rev, nb):
    ix = (lambda n: nb - 1 - n) if rev else (lambda n: n)
    tt = pl.BlockSpec((None, 8, RET_QK_DIM), lambda n: (ix(n), 0, 0))
    rt = pl.BlockSpec((3, tr, RET_QK_DIM), lambda n: (0, 0, 0))
    dm = pl.BlockSpec((RET_HEADS, RET_BLK, RET_BLK), lambda n: (0, 0, 0))
    dv = pl.BlockSpec((RET_HEADS, RET_BLK, 1), lambda n: (0, 0, 0))
    dc = pl.BlockSpec((RET_HEADS, 1, 1), lambda n: (0, 0, 0))
    return ix, [tt, rt, dm, dv, dv, dc]


def _ret_fwd(z_a, tables, name="ret_fwd"):
    T = z_a.shape[0]
    tr = min(RET_TR, T)
    cpb = tr // RET_BLK
    nb = T // tr
    QW, VW = RET_HEADS * RET_QK_DIM, RET_HEADS * RET_V_DIM
    ix, tab_specs = _ret_specs(tr, False, nb)

    def body(z_ref, tt_ref, rt_ref, dm_ref, qd_ref, kd_ref, cd_ref, y_ref, st_ref, state):
        @pl.when(pl.program_id(0) == 0)
        def _():
            state[...] = jnp.zeros_like(state)

        rot_c, rot_s = _rot_tables(tt_ref, rt_ref)
        for c in range(cpb):
            for h in range(RET_HEADS):
                rs = slice(c * RET_BLK, (c + 1) * RET_BLK)
                cc, ss = rot_c[rs, :], rot_s[rs, :]
                q = z_ref[rs, h * RET_QK_DIM:(h + 1) * RET_QK_DIM].astype(F32)
                k = z_ref[rs, QW + h * RET_QK_DIM:QW + (h + 1) * RET_QK_DIM].astype(F32)
                v = z_ref[rs, 2 * QW + h * RET_V_DIM:2 * QW + (h + 1) * RET_V_DIM]
                gt = z_ref[rs, 2 * QW + VW + h * RET_V_DIM:2 * QW + VW + (h + 1) * RET_V_DIM].astype(F32)
                qr = _rot(q, cc, ss)
                kr = _rot(k, cc, ss) * QK_SCALE
                s_old = state[h]
                sb = s_old.astype(BF16)
                st_ref[c, h] = sb
                s = _dot(qr.astype(BF16), kr.astype(BF16), NT) * dm_ref[h]
                o = _dot(s.astype(BF16), v, NN) + _dot((qr * qd_ref[h]).astype(BF16), sb, NN)
                state[h] = s_old * cd_ref[h] + _dot((kr * kd_ref[h]).astype(BF16), v, TN)
                mu = jnp.mean(o, axis=-1, keepdims=True)
                oc = o - mu
                rn = oc * lax.rsqrt(jnp.mean(oc * oc, axis=-1, keepdims=True) + EPS)
                silu = gt / (1.0 + jnp.exp(-gt))
                y_ref[rs, h * RET_V_DIM:(h + 1) * RET_V_DIM] = (silu * rn).astype(y_ref.dtype)

    return pl.pallas_call(
        body,
        grid=(nb,),
        in_specs=[pl.BlockSpec((tr, RET_W), lambda n: (n, 0))] + tab_specs,
        out_specs=[pl.BlockSpec((tr, VW), lambda n: (n, 0)),
                   pl.BlockSpec((cpb, RET_HEADS, RET_QK_DIM, RET_V_DIM), lambda n: (n, 0, 0, 0))],
        out_shape=[jax.ShapeDtypeStruct((T, Y_COLS), BF16),
                   jax.ShapeDtypeStruct((T // RET_BLK, RET_HEADS, RET_QK_DIM, RET_V_DIM), BF16)],
        scratch_shapes=[pltpu.VMEM((RET_HEADS, RET_QK_DIM, RET_V_DIM), F32)],
        compiler_params=_cp("arbitrary"),
        name=name,
    )(z_a, *tables)


def _ret_bwd(z_a, dy, states, tables, name="ret_bwd"):
    T = z_a.shape[0]
    tr = min(RET_TR, T)
    cpb = tr // RET_BLK
    nb = T // tr
    QW, VW = RET_HEADS * RET_QK_DIM, RET_HEADS * RET_V_DIM
    ix, tab_specs = _ret_specs(tr, True, nb)

    def body(z_ref, dy_ref, st_ref, tt_ref, rt_ref, dm_ref, qd_ref, kd_ref, cd_ref, dz_ref, dstate):
        @pl.when(pl.program_id(0) == 0)
        def _():
            dstate[...] = jnp.zeros_like(dstate)

        rot_c, rot_s = _rot_tables(tt_ref, rt_ref)
        for c in reversed(range(cpb)):
            for h in range(RET_HEADS):
                rs = slice(c * RET_BLK, (c + 1) * RET_BLK)
                cc, ss = rot_c[rs, :], rot_s[rs, :]
                q = z_ref[rs, h * RET_QK_DIM:(h + 1) * RET_QK_DIM].astype(F32)
                k = z_ref[rs, QW + h * RET_QK_DIM:QW + (h + 1) * RET_QK_DIM].astype(F32)
                v = z_ref[rs, 2 * QW + h * RET_V_DIM:2 * QW + (h + 1) * RET_V_DIM]
                gt = z_ref[rs, 2 * QW + VW + h * RET_V_DIM:2 * QW + VW + (h + 1) * RET_V_DIM].astype(F32)
                dyv = dy_ref[rs, h * RET_V_DIM:(h + 1) * RET_V_DIM].astype(F32)
                dmat, qd, kd = dm_ref[h], qd_ref[h], kd_ref[h]
                qr = _rot(q, cc, ss)
                kr = _rot(k, cc, ss) * QK_SCALE
                qrb, krb = qr.astype(BF16), kr.astype(BF16)
                sb = st_ref[c, h]
                sd = (_dot(qrb, krb, NT) * dmat).astype(BF16)
                qdb = (qr * qd).astype(BF16)
                kdb = (kr * kd).astype(BF16)
                o = _dot(sd, v, NN) + _dot(qdb, sb, NN)
                mu = jnp.mean(o, axis=-1, keepdims=True)
                oc = o - mu
                rstd = lax.rsqrt(jnp.mean(oc * oc, axis=-1, keepdims=True) + EPS)
                rn = oc * rstd
                sg = 1.0 / (1.0 + jnp.exp(-gt))
                dgt = dyv * rn * (sg * (1.0 + gt * (1.0 - sg)))
                drn = dyv * (gt * sg)
                do = rstd * (drn - jnp.mean(drn, axis=-1, keepdims=True) - rn * jnp.mean(drn * rn, axis=-1, keepdims=True))
                dob = do.astype(BF16)
                dsn = dstate[h]
                dsnb = dsn.astype(BF16)
                ds_raw = (_dot(dob, v, NT) * dmat).astype(BF16)
                dv = _dot(sd, dob, TN) + _dot(kdb, dsnb, NN)
                dqr = _dot(ds_raw, krb, NN) + qd * _dot(dob, sb, NT)
                dkr = (_dot(ds_raw, qrb, TN) + kd * _dot(v, dsnb, NT)) * QK_SCALE
                dstate[h] = dsn * cd_ref[h] + _dot(qdb, dob, TN)
                dq = dqr * cc + pltpu.roll(dqr * ss, RET_QK_DIM // 2, 1)
                dk = dkr * cc + pltpu.roll(dkr * ss, RET_QK_DIM // 2, 1)
                dz_ref[rs, h * RET_QK_DIM:(h + 1) * RET_QK_DIM] = dq.astype(dz_ref.dtype)
                dz_ref[rs, QW + h * RET_QK_DIM:QW + (h + 1) * RET_QK_DIM] = dk.astype(dz_ref.dtype)
                dz_ref[rs, 2 * QW + h * RET_V_DIM:2 * QW + (h + 1) * RET_V_DIM] = dv.astype(dz_ref.dtype)
                dz_ref[rs, 2 * QW + VW + h * RET_V_DIM:2 * QW + VW + (h + 1) * RET_V_DIM] = dgt.astype(dz_ref.dtype)

    return pl.pallas_call(
        body,
        grid=(nb,),
        in_specs=[pl.BlockSpec((tr, RET_W), lambda n: (ix(n), 0)),
                  pl.BlockSpec((tr, VW), lambda n: (ix(n), 0)),
                  pl.BlockSpec((cpb, RET_HEADS, RET_QK_DIM, RET_V_DIM), lambda n: (ix(n), 0, 0, 0))] + tab_specs,
        out_specs=pl.BlockSpec((tr, RET_W), lambda n: (ix(n), 0)),
        out_shape=jax.ShapeDtypeStruct((T, RET_W), BF16),
        scratch_shapes=[pltpu.VMEM((RET_HEADS, RET_QK_DIM, RET_V_DIM), F32)],
        compiler_params=_cp("arbitrary"),
        name=name,
    )(z_a, dy, states, *tables)


ATT_TQ = 256
ATT_CPB = ATT_TQ // CHUNK
ATT_SCALE = ATT_HEAD_DIM ** -0.5


ATT_WIN = 3 * ATT_TQ
ATT_NB = CHUNK * ATT_BAND


def _rel_index():
    i = np.arange(CHUNK)[:, None]
    j = np.arange(ATT_BAND)[None, :]
    rel = np.clip(i + ATT_PAST * CHUNK - j, -MAX_REL, MAX_REL) + MAX_REL
    return jnp.asarray(rel.reshape(1, ATT_NB).astype(np.int32))


def _split3(x):
    hi = x.astype(BF16)
    r1 = x - hi.astype(F32)
    mid = r1.astype(BF16)
    lo = (r1 - mid.astype(F32)).astype(BF16)
    return hi, mid, lo


REL_TILE = 4608


def _bias_expand(rel_bias, name="bias_expand"):
    H = rel_bias.shape[0]
    n = ATT_NB
    padded = jnp.pad(rel_bias, ((0, 0), (0, N_REL_PAD - N_REL)))

    def body(rb_ref, idx_ref, o_ref):
        onehot = (lax.broadcasted_iota(jnp.int32, (N_REL_PAD, REL_TILE), 0) == idx_ref[...]).astype(BF16)
        hi, mid, lo = _split3(rb_ref[...])
        o_ref[...] = _dot(hi, onehot, NN) + _dot(mid, onehot, NN) + _dot(lo, onehot, NN)

    out = pl.pallas_call(
        body,
        grid=(n // REL_TILE,),
        in_specs=[pl.BlockSpec((H, N_REL_PAD), lambda t: (0, 0)), pl.BlockSpec((1, REL_TILE), lambda t: (0, t))],
        out_specs=pl.BlockSpec((H, REL_TILE), lambda t: (0, t)),
        out_shape=jax.ShapeDtypeStruct((H, n), F32),
        compiler_params=_cp("parallel"),
        name=name,
    )(padded, _rel_index())
    return out.reshape(H, CHUNK, ATT_BAND)


def _bias_tile(band, name="bias_tile"):
    H = band.shape[0]
    padded = jnp.pad(band, ((0, 0), (0, 0), (0, ATT_WIN - ATT_BAND)), constant_values=NEG_INF)

    def body(b_ref, o_ref):
        b = b_ref[...]
        col = lax.broadcasted_iota(jnp.int32, (CHUNK, ATT_WIN), 1)
        keep = col >= (2 - pl.program_id(0)) * ATT_TQ
        for a in range(ATT_CPB):
            o_ref[a * CHUNK:(a + 1) * CHUNK, :] = jnp.where(keep, pltpu.roll(b, a * CHUNK, 1) if a else b, NEG_INF)

    return pl.pallas_call(
        body,
        grid=(3, H),
        in_specs=[pl.BlockSpec((None, CHUNK, ATT_WIN), lambda v, h: (h, 0, 0))],
        out_specs=pl.BlockSpec((None, None, ATT_TQ, ATT_WIN), lambda v, h: (v, h, 0, 0)),
        out_shape=jax.ShapeDtypeStruct((3, H, ATT_TQ, ATT_WIN), F32),
        compiler_params=_cp("parallel", "parallel"),
        name=name,
    )(padded)


def _bias_untile(dtile, name="bias_untile"):
    H = dtile.shape[0]

    def body(d_ref, o_ref):
        acc = d_ref[0:CHUNK, :]
        for a in range(1, ATT_CPB):
            acc = acc + pltpu.roll(d_ref[a * CHUNK:(a + 1) * CHUNK, :], ATT_WIN - a * CHUNK, 1)
        o_ref[...] = acc

    out = pl.pallas_call(
        body,
        grid=(H,),
        in_specs=[pl.BlockSpec((None, ATT_TQ, ATT_WIN), lambda h: (h, 0, 0))],
        out_specs=pl.BlockSpec((None, CHUNK, ATT_WIN), lambda h: (h, 0, 0)),
        out_shape=jax.ShapeDtypeStruct((H, CHUNK, ATT_WIN), F32),
        compiler_params=_cp("parallel"),
        name=name,
    )(dtile)
    return out[:, :, :ATT_BAND]


def _bias_reduce(dbias, name="bias_reduce"):
    H = dbias.shape[0]
    n = ATT_NB

    def body(db_ref, idx_ref, o_ref):
        @pl.when(pl.program_id(0) == 0)
        def _():
            o_ref[...] = jnp.zeros_like(o_ref)

        onehot = (lax.broadcasted_iota(jnp.int32, (N_REL_PAD, REL_TILE), 0) == idx_ref[...]).astype(BF16)
        hi, mid, lo = _split3(db_ref[...])
        o_ref[...] += _dot(hi, onehot, NT) + _dot(mid, onehot, NT) + _dot(lo, onehot, NT)

    out = pl.pallas_call(
        body,
        grid=(n // REL_TILE,),
        in_specs=[pl.BlockSpec((H, REL_TILE), lambda t: (0, t)), pl.BlockSpec((1, REL_TILE), lambda t: (0, t))],
        out_specs=pl.BlockSpec((H, N_REL_PAD), lambda t: (0, 0)),
        out_shape=jax.ShapeDtypeStruct((H, N_REL_PAD), F32),
        compiler_params=_cp("arbitrary"),
        name=name,
    )(dbias.reshape(H, n), _rel_index())
    return out[:, :N_REL]


def _att_probs(q, kwin, bias):
    s = _dot(q, kwin, NT) + bias
    e = jnp.exp(s - jnp.max(s, axis=-1, keepdims=True))
    return e * (1.0 / jnp.sum(e, axis=-1, keepdims=True))


ATT_PAIR = 2 * ATT_HEAD_DIM
ATT_NP = ATT_HEADS // 2
ATT_QW = ATT_HEADS * ATT_HEAD_DIM
Y_COLS = RET_HEADS * RET_V_DIM + ATT_QW


def _att_specs(tq, nq, clip_q, q_col0):
    cb = ATT_QW // ATT_PAIR
    qi = (lambda p, m: (jnp.minimum(m, nq - 1), q_col0 + p)) if clip_q else (lambda p, m: (m, q_col0 + p))
    q = pl.BlockSpec((tq, ATT_PAIR), qi)

    def win(col0):
        return [pl.BlockSpec((tq, ATT_PAIR), functools.partial(lambda p, m, back: (jnp.clip(m - back, 0, nq - 1), col0 + p), back=b))
                for b in (2, 1, 0)]

    bias = pl.BlockSpec((None, 2, ATT_TQ, ATT_WIN), lambda p, m: (jnp.minimum(m, 2), p, 0, 0))
    return q, win(cb), win(2 * cb), bias


def _head_masks(rows):
    lane = lax.broadcasted_iota(jnp.int32, (rows, ATT_PAIR), 1)
    return lane < ATT_HEAD_DIM


def _att_fwd(z_b, bias, y, comm=None, name="att_fwd"):
    T = z_b.shape[0]
    tq = ATT_TQ
    nq = T // tq
    qs, kwin, vwin, bs = _att_specs(tq, nq, False, 0)
    nc = comm.n if comm else 0
    total = ATT_NP * nq

    def body(*refs):
        q_ref, k0, k1, k2, v0, v1, v2, b_ref = refs[:8]
        cin = refs[9:9 + nc]
        o_ref = refs[9 + nc]
        cout = refs[10 + nc:10 + 2 * nc]
        csem = refs[10 + 2 * nc:]
        m = pl.program_id(1)
        step = pl.program_id(0) * nq + m
        _host_comm(comm, "early", step, total, cin, cout, csem)
        kw = jnp.concatenate([k0[...], k1[...], k2[...]], axis=0)
        vw = jnp.concatenate([v0[...], v1[...], v2[...]], axis=0)
        q2 = q_ref[...] * ATT_SCALE
        even = _head_masks(tq)
        outs = []
        for hh in range(2):
            qm = jnp.where(even if hh == 0 else ~even, q2, jnp.zeros_like(q2))
            p = _att_probs(qm, kw, b_ref[hh])
            outs.append(_dot(p.astype(BF16), vw, NN))
        o_ref[...] = jnp.where(even, outs[0], outs[1]).astype(o_ref.dtype)
        _host_comm(comm, "late", step, total, cin, cout, csem)

    y_cb = (Y_COLS - ATT_QW) // ATT_PAIR
    out = pl.pallas_call(
        body,
        grid=(ATT_NP, nq),
        in_specs=[qs] + kwin + vwin + [bs, ANY] + [ANY] * nc,
        out_specs=[pl.BlockSpec((tq, ATT_PAIR), lambda p, m: (m, y_cb + p))] + [ANY] * nc,
        out_shape=[jax.ShapeDtypeStruct((T, Y_COLS), BF16)] + (comm.out_shape if comm else []),
        scratch_shapes=comm.scratch if comm else [],
        input_output_aliases={8: 0},
        compiler_params=_cp("arbitrary", "arbitrary"),
        name=name,
    )(z_b, z_b, z_b, z_b, z_b, z_b, z_b, bias, y, *(comm.arrays if comm else []))
    return out[0], list(out[1:])


def _att_bwd(z_b, bias, dy, comm=None, name="att_bwd"):
    T = z_b.shape[0]
    tq = ATT_TQ
    nq = T // tq
    y_cb = (Y_COLS - ATT_QW) // ATT_PAIR
    qs, kwin, vwin, bs = _att_specs(tq, nq, True, 0)
    dos = _att_specs(tq, nq, True, y_cb)[0]
    kv_out = pl.BlockSpec((tq, ATT_PAIR), lambda p, m: (jnp.maximum(m - 2, 0), p))
    W3 = 3 * tq
    nc = comm.n if comm else 0
    total = ATT_NP * (nq + 2)

    def body(*refs):
        q_ref, k0, k1, k2, v0, v1, v2, b_ref, do_ref = refs[:9]
        cin = refs[9:9 + nc]
        dq_ref, dk_ref, dv_ref, db_ref = refs[9 + nc:13 + nc]
        cout = refs[13 + nc:13 + 2 * nc]
        dkc, dvc, dkw, dvw = refs[13 + 2 * nc:17 + 2 * nc]
        csem = refs[17 + 2 * nc:]
        m = pl.program_id(1)
        step = pl.program_id(0) * (nq + 2) + m
        _host_comm(comm, "early", step, total, cin, cout, csem)

        @pl.when(m == 0)
        def _():
            dkc[...] = jnp.zeros_like(dkc)
            dvc[...] = jnp.zeros_like(dvc)
            db_ref[...] = jnp.zeros_like(db_ref)

        @pl.when(m >= nq)
        def _():
            dkw[...] = jnp.zeros_like(dkw)
            dvw[...] = jnp.zeros_like(dvw)

        @pl.when(m < nq)
        def _():
            kw = jnp.concatenate([k0[...], k1[...], k2[...]], axis=0)
            vw = jnp.concatenate([v0[...], v1[...], v2[...]], axis=0)
            q2, do2 = q_ref[...] * ATT_SCALE, do_ref[...]
            even = _head_masks(tq)
            dqs, dks, dvs = [], [], []
            for hh in range(2):
                mine = even if hh == 0 else ~even
                p = _att_probs(jnp.where(mine, q2, jnp.zeros_like(q2)), kw, b_ref[hh])
                dp = _dot(jnp.where(mine, do2, jnp.zeros_like(do2)), vw, NT)
                ds = p * (dp - jnp.sum(dp * p, axis=-1, keepdims=True))
                db_ref[hh] += ds
                dsb = ds.astype(BF16)
                dqs.append(_dot(dsb, kw, NN))
                dks.append(_dot(dsb, q2, TN))
                dvs.append(_dot(p.astype(BF16), do2, TN))
            even_w = _head_masks(W3)
            dq_ref[...] = (jnp.where(even, dqs[0], dqs[1]) * ATT_SCALE).astype(dq_ref.dtype)
            dkw[...] = jnp.where(even_w, dks[0], dks[1])
            dvw[...] = jnp.where(even_w, dvs[0], dvs[1])

        dk_ref[...] = (dkc[0:tq, :] + dkw[0:tq, :]).astype(dk_ref.dtype)
        dv_ref[...] = (dvc[0:tq, :] + dvw[0:tq, :]).astype(dv_ref.dtype)
        dkc[0:tq, :] = dkc[tq:2 * tq, :] + dkw[tq:2 * tq, :]
        dvc[0:tq, :] = dvc[tq:2 * tq, :] + dvw[tq:2 * tq, :]
        dkc[tq:2 * tq, :] = dkw[2 * tq:W3, :]
        dvc[tq:2 * tq, :] = dvw[2 * tq:W3, :]
        _host_comm(comm, "late", step, total, cin, cout, csem)

    qo = pl.BlockSpec((tq, ATT_PAIR), lambda p, m: (jnp.minimum(m, nq - 1), p))
    dbs = pl.BlockSpec((2, ATT_TQ, ATT_WIN), lambda p, m: (p, 0, 0))
    hd = jax.ShapeDtypeStruct((T, ATT_QW), BF16)
    out = pl.pallas_call(
        body,
        grid=(ATT_NP, nq + 2),
        in_specs=[qs] + kwin + vwin + [bs, dos] + [ANY] * nc,
        out_specs=[qo, kv_out, kv_out, dbs] + [ANY] * nc,
        out_shape=[hd, hd, hd, jax.ShapeDtypeStruct((ATT_HEADS, ATT_TQ, ATT_WIN), F32)] + (comm.out_shape if comm else []),
        scratch_shapes=[pltpu.VMEM((2 * tq, ATT_PAIR), F32), pltpu.VMEM((2 * tq, ATT_PAIR), F32),
                        pltpu.VMEM((W3, ATT_PAIR), F32), pltpu.VMEM((W3, ATT_PAIR), F32)] + (comm.scratch if comm else []),
        compiler_params=_cp("arbitrary", "arbitrary"),
        name=name,
    )(z_b, z_b, z_b, z_b, z_b, z_b, z_b, bias, dy, *(comm.arrays if comm else []))
    return out[0], out[1], out[2], out[3], list(out[4:])


REST = ["ab_out", "c_in_t", "c_out", "up_t0", "up_t1", "down0", "down1"]


def _local_step(x, target, wt, small, rest_shards=None, overlap=False, hn0=None):
    T = x.shape[0]
    Fh = FFN_HIDDEN
    tables = _ret_tables(T)
    gw, gs, recv = {}, {}, {}
    wt = dict(wt)

    if hn0 is None:
        hn0 = _rms_fwd(x, small["attn_norm_g"][0], name="rms_fwd")
    z_a = _mm(hn0, wt["ab_in_t"][:RET_W], "nt", BF16, name="mm_ab_in_a")
    z_b = _mm(hn0, wt["ab_in_t"][RET_W:], "nt", BF16, name="mm_ab_in_b")
    y, states = _ret_fwd(z_a, tables)
    bias = _bias_tile(_bias_expand(small["rel_bias"]))
    y, rest = _att_fwd(z_b, bias, y, comm=_Comm("gather", rest_shards) if rest_shards is not None else None)
    if rest_shards is not None:
        full = dict(zip(REST, rest))
        wt.update(ab_out=full["ab_out"], c_in_t=full["c_in_t"], c_out=full["c_out"],
                  up_t=[full["up_t0"], full["up_t1"]], down=[full["down0"], full["down1"]])
    h1, hf0 = _mm_rows(y, wt["ab_out"], x, "mm_ab_out", norm_g=small["ffn_norm_g"][0])

    def ffn_fwd(h, hf, layer, next_g):
        zg, zu, a = _ffn_up_mid(hf, wt["up_t"][layer], small["conv_w"][layer], small["conv_b"][layer][None, :])
        if next_g is None:
            return _mm_rows(a, wt["down"][layer], h, "mm_down_loss", loss=(target, small["final_g"])), (hf, zg, zu, a)
        h_out, hn_next = _mm_rows(a, wt["down"][layer], h, "mm_down", norm_g=next_g)
        return h_out, hn_next, (hf, zg, zu, a)

    def ffn_bwd(dh_out, h, layer, saved, exchange=None):
        hf, zg, zu, a = saved
        da = _mm(dh_out, wt["down"][layer], "nt", BF16, name="mm_d_a")
        d_down = _mm(a, dh_out, "tn", BF16, name="mm_dw_down")
        comm = _Comm("exchange", [gw[k] for k in exchange]) if exchange else None
        dzg, dzu, dcw, dcb, got = _ffn_mid_bwd(zg, zu, da, small["conv_w"][layer], small["conv_b"][layer][None, :], comm=comm)
        recv.update(zip(exchange or [], got))
        dh, dg = _mm_rows(dzg, wt["up_t"][layer][:Fh], None, "mm_d_hf_norm", second=(dzu, wt["up_t"][layer][Fh:]),
                          bwd=(h, small["ffn_norm_g"][layer], dh_out))
        d_up = _mm(dzg, hf, "tn", BF16, name="mm_dw_up", rows=(2 * Fh, 0, None))
        d_up = _mm(dzu, hf, "tn", BF16, name="mm_dw_up_2", rows=(2 * Fh, Fh, d_up))
        return dh, dg, d_up, d_down, dcw, dcb

    h2, hn1, ffn0 = ffn_fwd(h1, hf0, 0, small["attn_norm_g"][1])

    zz = _mm(hn1, wt["c_in_t"], "nt", BF16, name="mm_c_in")
    ys = _sgu_fwd(zz, small["ln_g"], small["ln_b"], small["w_s"], small["b_s"])
    h3, hf1 = _mm_rows(ys, wt["c_out"], h2, "mm_c_out", norm_g=small["ffn_norm_g"][1])
    (loss_vec, dh4, gs["final_g"]), ffn1 = ffn_fwd(h3, hf1, 1, None)

    dh3, dgf1, gw["up_t1"], gw["down1"], dcw1, dcb1 = ffn_bwd(dh4, h3, 1, ffn1)
    dys = _mm(dh3, wt["c_out"], "nt", BF16, name="mm_d_ys")
    gw["c_out"] = _mm(ys, dh3, "tn", BF16, name="mm_dw_c_out")
    dzz, gs["w_s"], dbs, dlg, dlb = _sgu_bwd(zz, dys, small["ln_g"], small["ln_b"], small["w_s"], small["b_s"])
    gs["b_s"], gs["ln_g"], gs["ln_b"] = dbs[:, :, 0], dlg[0], dlb[0]
    dh2, dga1 = _mm_rows(dzz, wt["c_in_t"], None, "mm_d_hn1_norm", bwd=(h2, small["attn_norm_g"][1], dh3))
    gw["c_in_t"] = _mm(dzz, hn1, "tn", BF16, name="mm_dw_c_in")

    dh1, dgf0, gw["up_t0"], gw["down0"], dcw0, dcb0 = ffn_bwd(
        dh2, h1, 0, ffn0, exchange=["c_in_t", "c_out", "up_t1", "down1"] if overlap else None)

    dy = _mm(dh1, wt["ab_out"], "nt", BF16, name="mm_d_y")
    gw["ab_out"] = _mm(y, dh1, "tn", BF16, name="mm_dw_ab_out")
    dz_a = _ret_bwd(z_a, dy, states, tables)
    late = ["ab_out", "up_t0", "down0"] if overlap else []
    dq, dk, dv, dbias, got = _att_bwd(z_b, bias, dy, comm=_Comm("exchange", [gw[k] for k in late]) if late else None)
    recv.update(zip(late, got))
    dz_b = jnp.concatenate([dq, dk, dv], axis=1)
    gs["rel_bias"] = _bias_reduce(_bias_untile(dbias))
    gw["ab_in_t"] = _mm(dz_a, hn0, "tn", BF16, name="mm_dw_ab_in_a", rows=(RET_W + ATT_W, 0, None))
    gw["ab_in_t"] = _mm(dz_b, hn0, "tn", BF16, name="mm_dw_ab_in_b", rows=(RET_W + ATT_W, RET_W, gw["ab_in_t"]))
    if overlap:
        dhn0, got = _mm(dz_a, wt["ab_in_t"][:RET_W], "nn", F32, name="mm_d_hn0", comm=_Comm("exchange", [gw["ab_in_t"]]))
        recv["ab_in_t"] = got[0]
    else:
        dhn0 = _mm(dz_a, wt["ab_in_t"][:RET_W], "nn", F32, name="mm_d_hn0")
    grad_x, dga0 = _mm_rows(dz_b, wt["ab_in_t"][RET_W:], dhn0, "mm_d_hn0_norm", bwd=(x, small["attn_norm_g"][0], dh1))

    gs["attn_norm_g"] = jnp.concatenate([dga0, dga1], axis=0)
    gs["ffn_norm_g"] = jnp.concatenate([dgf0, dgf1], axis=0)
    gs["conv_w"] = jnp.stack([dcw0, dcw1])
    gs["conv_b"] = jnp.concatenate([dcb0, dcb1], axis=0)
    gs["final_g"] = gs["final_g"][0]
    return loss_vec[0, 0], grad_x, gw, gs, recv


MESH_ID = pl.DeviceIdType.MESH
ANY = pl.BlockSpec(memory_space=pl.ANY)


def _my_place():
    return lax.axis_index("x"), lax.axis_index("y"), lax.axis_index("c")


class _Comm:
    def __init__(self, kind, arrays):
        self.kind, self.arrays, self.n = kind, list(arrays), len(arrays)
        if kind == "gather":
            self.out_shape = [jax.ShapeDtypeStruct((N_DEV * s.shape[0], s.shape[1]), s.dtype) for s in arrays]
        else:
            self.out_shape = [jax.ShapeDtypeStruct((N_DEV, g.shape[0] // N_DEV, g.shape[1]), g.dtype) for g in arrays]
        n = self.n
        self.scratch = [pltpu.SemaphoreType.DMA((n, 7)), pltpu.SemaphoreType.DMA((n, 7)), pltpu.SemaphoreType.DMA((n,))]

    def phase(self, ph, in_refs, out_refs, sems):
        (self._gather if self.kind == "gather" else self._exchange)(ph, in_refs, out_refs, sems)

    def _gather(self, ph, x_refs, o_refs, sems):
        n = self.n
        send_sems, recv_sems, local_sems = sems
        x, y, c = _my_place()
        me, sibling = (x, y, c), (x, y, 1 - c)
        chips = [(1 - x, y), (x, 1 - y), (1 - x, 1 - y)]

        def rows(a, place):
            m = x_refs[a].shape[0]
            px, py, pc = place
            return o_refs[a].at[pl.ds((4 * px + 2 * py + pc) * m, m), :]

        def copy(a, k, block, to, own=False):
            return pltpu.make_async_remote_copy(
                src_ref=x_refs[a] if own else rows(a, block), dst_ref=rows(a, block),
                send_sem=send_sems.at[a, k], recv_sem=recv_sems.at[a, k], device_id=to, device_id_type=MESH_ID)

        def mine():
            return [pltpu.make_async_copy(x_refs[a], rows(a, me), local_sems.at[a]) for a in range(n)]

        def first():
            out = []
            for a in range(n):
                out.append(copy(a, 0, me, sibling, own=True))
                out += [copy(a, 1 + j, me, (*chip, c), own=True) for j, chip in enumerate(chips)]
            return out

        def passed():
            return [copy(a, 4 + j, (*chip, c), sibling) for j, chip in enumerate(chips) for a in range(n)]

        if ph == 0:
            for cp in mine() + first():
                cp.start()
        elif ph == 1:
            fw = passed()
            for j, chip in enumerate(chips):
                for a in range(n):
                    copy(a, 1 + j, (*chip, c), me).wait_recv()
                    fw[j * n + a].start()
        else:
            for a in range(n):
                copy(a, 0, sibling, me).wait_recv()
                for j, chip in enumerate(chips):
                    copy(a, 4 + j, (*chip, 1 - c), me).wait_recv()
            for cp in first() + passed():
                cp.wait_send()
            for cp in mine():
                cp.wait()

    def _exchange(self, ph, g_refs, o_refs, sems):
        n = self.n
        send_sems, recv_sems, local_sems = sems
        x, y, c = _my_place()
        me = 4 * x + 2 * y + c
        peers = [(x ^ ((k >> 2) & 1), y ^ ((k >> 1) & 1), c ^ (k & 1)) for k in range(1, N_DEV)]

        def block(a, idx):
            m = g_refs[a].shape[0] // N_DEV
            return g_refs[a].at[pl.ds(idx * m, m), :]

        def copy(a, k, slot):
            px, py, pc = peers[k]
            return pltpu.make_async_remote_copy(
                src_ref=block(a, 4 * px + 2 * py + pc), dst_ref=o_refs[a].at[slot],
                send_sem=send_sems.at[a, k], recv_sem=recv_sems.at[a, k], device_id=peers[k], device_id_type=MESH_ID)

        if ph == 1:
            return
        mine = [pltpu.make_async_copy(block(a, me), o_refs[a].at[me], local_sems.at[a]) for a in range(n)]
        sends = [copy(a, k, me) for k in range(N_DEV - 1) for a in range(n)]
        if ph == 0:
            for cp in mine + sends:
                cp.start()
        else:
            for k in range(N_DEV - 1):
                px, py, pc = peers[k]
                for a in range(n):
                    copy(a, k, 4 * px + 2 * py + pc).wait_recv()
            for cp in sends:
                cp.wait_send()
            for cp in mine:
                cp.wait()


def _comm_call(comm, name):
    n = comm.n

    def body(*refs):
        for ph in range(3):
            comm.phase(ph, refs[:n], refs[n:2 * n], refs[2 * n:])

    return pl.pallas_call(
        body, out_shape=comm.out_shape, in_specs=[ANY] * n, out_specs=[ANY] * n, scratch_shapes=comm.scratch, name=name,
    )(*comm.arrays)


def _host_comm(comm, when, step, total, cin, cout, csem):
    if comm is None:
        return
    sched = {0: 0, 1: (3 * total) // 4, 2: total - 1}
    phases = (0, 1) if when == "early" else (2,)
    for ph in phases:
        if ph == 1 and comm.kind == "exchange":
            continue

        @pl.when(step == sched[ph])
        def _(ph=ph):
            comm.phase(ph, cin, cout, csem)


def _all_gather(shards, name="all_gather"):
    return _comm_call(_Comm("gather", shards), name)


def _row_tile(r, target=256):
    best = None
    for t in range(8, min(r, target) + 1, 8):
        if r % t == 0:
            best = t
    return best if best is not None else r


def _sum8(parts, name="sum8"):
    _, M, N = parts.shape
    tr = _row_tile(M, 128)

    def body(p_ref, o_ref):
        acc = p_ref[0].astype(F32)
        for d in range(1, N_DEV):
            acc = acc + p_ref[d].astype(F32)
        o_ref[...] = acc

    return pl.pallas_call(
        body,
        grid=(M // tr,),
        in_specs=[pl.BlockSpec((N_DEV, tr, N), lambda i: (0, i, 0))],
        out_specs=pl.BlockSpec((tr, N), lambda i: (i, 0)),
        out_shape=jax.ShapeDtypeStruct((M, N), F32),
        compiler_params=_cp("parallel"),
        name=name,
    )(parts)


def _adamw(w, g, m, v, name="adamw"):
    shape = w.shape
    if w.ndim == 1:
        r2 = (1, shape[0])
    else:
        r2 = (int(np.prod(shape[:-1])), shape[-1])
    R, C = r2
    tr = _row_tile(R)
    bc1 = 1.0 - ADAM_B1 ** ADAM_STEP
    bc2 = 1.0 - ADAM_B2 ** ADAM_STEP

    def body(w_ref, g_ref, m_ref, v_ref, d_ref, nm_ref, nv_ref):
        gv = g_ref[...]
        nm = ADAM_B1 * m_ref[...] + (1.0 - ADAM_B1) * gv
        nv = ADAM_B2 * v_ref[...] + (1.0 - ADAM_B2) * (gv * gv)
        d_ref[...] = -ADAM_LR * ((nm / bc1) / (jnp.sqrt(nv / bc2) + ADAM_EPS) + ADAM_WD * w_ref[...])
        nm_ref[...] = nm
        nv_ref[...] = nv

    spec = pl.BlockSpec((tr, C), lambda i: (i, 0))
    out = pl.pallas_call(
        body,
        grid=(R // tr,),
        in_specs=[spec] * 4,
        out_specs=[spec] * 3,
        out_shape=[jax.ShapeDtypeStruct(r2, F32)] * 3,
        compiler_params=_cp("parallel"),
        name=name,
    )(w.reshape(r2), g.reshape(r2), m.reshape(r2), v.reshape(r2))
    return [o.reshape(shape) for o in out]


WEIGHTS = ['attn_norm_g', 'ffn_norm_g', 'ab_w_in', 'ab_w_out', 'ab_rel_bias', 'c_w_in', 'c_ln_g', 'c_ln_b', 'c_w_s', 'c_b_s',
           'c_w_out', 'ffn_w_up', 'ffn_conv_w', 'ffn_conv_b', 'ffn_w_down', 'final_norm_g']
SMALL_ORDER = ["attn_norm_g", "ffn_norm_g", "rel_bias", "ln_g", "ln_b", "w_s", "b_s", "conv_w", "conv_b", "final_g"]
PACK_ROW = 1024


def _pack(arrs):
    flat = jnp.concatenate([a.reshape(-1) for a in arrs])
    n = flat.shape[0]
    padded = -(-n // PACK_ROW) * PACK_ROW
    return jnp.pad(flat, (0, padded - n)).reshape(padded // 128, 128)


def _unpack(flat, shapes):
    out, off = [], 0
    for s in shapes:
        n = int(np.prod(s))
        out.append(flat[off:off + n].reshape(s))
        off += n
    return out


def _step(P):
    x, target = P["x"][0], P["loss_target"][0]
    me = 4 * lax.axis_index("x") + 2 * lax.axis_index("y") + lax.axis_index("c")
    n_up = P["ffn_w_up"].shape[0]
    Fc = P["ffn_conv_w"].shape[-1]
    Lc = P["c_ln_g"].shape[-1]

    first = _Comm("gather", [P["ab_w_in"][0].T.astype(BF16), _pack([P["ffn_conv_w"], P["c_ln_g"], P["c_ln_b"]])])
    hn0, full = _rms_fwd(x, P["attn_norm_g"][0], comm=first, name="rms_fwd_gather")
    wt = {"ab_in_t": full[0]}
    rest = {"ab_out": P["ab_w_out"][0], "c_in_t": P["c_w_in"][0].T, "c_out": P["c_w_out"][0],
            "up_t0": P["ffn_w_up"][0].T, "up_t1": P["ffn_w_up"][1].T, "down0": P["ffn_w_down"][0], "down1": P["ffn_w_down"][1]}
    rest_shards = [rest[k].astype(BF16) for k in REST]
    sm = full[-1].reshape(N_DEV, -1)
    conv_w = sm[:, :n_up * 3 * Fc].reshape(N_DEV, n_up, 3, Fc).transpose(1, 2, 0, 3).reshape(n_up, 3, N_DEV * Fc)
    off = n_up * 3 * Fc
    ln_g = sm[:, off:off + Lc].reshape(N_DEV * Lc)
    ln_b = sm[:, off + Lc:off + 2 * Lc].reshape(N_DEV * Lc)
    small = {"attn_norm_g": P["attn_norm_g"], "ffn_norm_g": P["ffn_norm_g"], "rel_bias": P["ab_rel_bias"][0],
             "ln_g": ln_g, "ln_b": ln_b, "w_s": P["c_w_s"][0], "b_s": P["c_b_s"][0], "conv_w": conv_w,
             "conv_b": P["ffn_conv_b"], "final_g": P["final_norm_g"]}

    loss_part, grad_x, gw, gs, recv = _local_step(x, target, wt, small, rest_shards=rest_shards, overlap=True, hn0=hn0)
    loss = lax.psum(loss_part, ("x", "y", "c"))

    s8 ={k: _sum8(recv[k], name="sum8") for k in ["ab_in_t"] + REST}
    g_big = {"ab_w_in": s8["ab_in_t"].T[None], "ab_w_out": s8["ab_out"][None], "c_w_in": s8["c_in_t"].T[None],
             "c_w_out": s8["c_out"][None], "ffn_w_up": jnp.stack([s8["up_t0"].T, s8["up_t1"].T]),
             "ffn_w_down": jnp.stack([s8["down0"], s8["down1"]])}

    packed = _pack([gs[k] for k in SMALL_ORDER])
    gathered = _all_gather([packed], name="gather_small_grads")[0]
    tot = _sum8(gathered.reshape(N_DEV, packed.shape[0], 128), name="sum8_small").reshape(-1)
    gsm = dict(zip(SMALL_ORDER, _unpack(tot, [gs[k].shape for k in SMALL_ORDER])))
    grads = dict(g_big)
    grads["attn_norm_g"] = gsm["attn_norm_g"]
    grads["ffn_norm_g"] = gsm["ffn_norm_g"]
    grads["ab_rel_bias"] = gsm["rel_bias"][None]
    grads["c_ln_g"] = lax.dynamic_slice(gsm["ln_g"], (me * Lc,), (Lc,))[None]
    grads["c_ln_b"] = lax.dynamic_slice(gsm["ln_b"], (me * Lc,), (Lc,))[None]
    grads["c_w_s"] = gsm["w_s"][None]
    grads["c_b_s"] = gsm["b_s"][None]
    grads["ffn_conv_w"] = lax.dynamic_slice(gsm["conv_w"], (0, 0, me * Fc), (n_up, 3, Fc))
    grads["ffn_conv_b"] = gsm["conv_b"]
    grads["final_norm_g"] = gsm["final_g"]

    delta, new_m, new_v = {}, {}, {}
    for k in WEIGHTS:
        delta[k], new_m[k], new_v[k] = _adamw(P[k], grads[k], P["m_" + k], P["v_" + k], name="adamw")
    return (loss, grad_x[None], *[grads[k] for k in WEIGHTS], *[delta[k] for k in WEIGHTS],
            *[new_m[k] for k in WEIGHTS], *[new_v[k] for k in WEIGHTS])


def kernel(x, attn_norm_g, ffn_norm_g, ab_w_in, ab_w_out, ab_rel_bias, c_w_in, c_ln_g, c_ln_b, c_w_s, c_b_s, c_w_out, ffn_w_up, ffn_conv_w, ffn_conv_b, ffn_w_down, final_norm_g, loss_target, m_attn_norm_g, m_ffn_norm_g, m_ab_w_in, m_ab_w_out, m_ab_rel_bias, m_c_w_in, m_c_ln_g, m_c_ln_b, m_c_w_s, m_c_b_s, m_c_w_out, m_ffn_w_up, m_ffn_conv_w, m_ffn_conv_b, m_ffn_w_down, m_final_norm_g, v_attn_norm_g, v_ffn_norm_g, v_ab_w_in, v_ab_w_out, v_ab_rel_bias, v_c_w_in, v_c_ln_g, v_c_ln_b, v_c_w_s, v_c_b_s, v_c_w_out, v_ffn_w_up, v_ffn_conv_w, v_ffn_conv_b, v_ffn_w_down, v_final_norm_g):
    return _step(dict(locals()))
```

```python
import functools

import numpy as np
import jax
import jax.numpy as jnp
from jax import lax
from jax.experimental import pallas as pl
from jax.experimental.pallas import tpu as pltpu

F32 = jnp.float32
BF16 = jnp.bfloat16

D_MODEL = 1024
CHUNK = 64
EPS = 1e-6
NEG_INF = -1e30
RET_HEADS = 4
RET_QK_DIM = 128
RET_V_DIM = 256
ATT_HEADS = 8
ATT_HEAD_DIM = 64
ATT_PAST = 8
ATT_BAND = (ATT_PAST + 1) * CHUNK
MAX_REL = 128
N_REL = 2 * MAX_REL + 1
N_REL_PAD = 384
SGU_BLOCK = 128
SGU_GROUPS = 8
SGU_WIDTH = 2048
SGU_GW = SGU_WIDTH // SGU_GROUPS
FFN_HIDDEN = 2816
RET_W = 2 * RET_HEADS * RET_QK_DIM + 2 * RET_HEADS * RET_V_DIM
ATT_W = 3 * ATT_HEADS * ATT_HEAD_DIM
N_DEV = 8

ADAM_LR = 0.001
ADAM_B1 = 0.9
ADAM_B2 = 0.999
ADAM_EPS = 1e-08
ADAM_WD = 0.01
ADAM_STEP = 10

VMEM_LIMIT = 52 * 1024 * 1024


def _cp(*sem):
    return pltpu.CompilerParams(dimension_semantics=sem if sem else None, vmem_limit_bytes=VMEM_LIMIT)


def _tile(n, target):
    if n <= target:
        return n
    best = None
    for t in range(128, target + 1, 128):
        if n % t == 0:
            best = t
    assert best is not None, (n, target)
    return best


def _gelu(x):
    c = 0.7978845608028654
    return 0.5 * x * (1.0 + jnp.tanh(c * (x + 0.044715 * x * x * x)))


def _gelu_and_grad(x):
    c = 0.7978845608028654
    x2 = x * x
    t = jnp.tanh(c * (x + 0.044715 * x * x2))
    cdf = 0.5 * (1.0 + t)
    grad = cdf + x * (0.5 * c) * (1.0 - t * t) * (1.0 + 3.0 * 0.044715 * x2)
    return x * cdf, grad


def _dot(a, b, dims):
    return lax.dot_general(a, b, (dims, ((), ())), preferred_element_type=F32)


NN = ((1,), (0,))
NT = ((1,), (1,))
TN = ((0,), (0,))


def _mm(a, b, mode, out_dtype, res=None, name="mm", tm_t=None, tn_t=None, tk_t=None, comm=None, rows=None):
    if mode == "nt":
        (M, K), N = a.shape, b.shape[0]
        dm, dn, dk = (512, 2816, K) if N <= 2816 else (1024, 1024, K)
    elif mode == "nn":
        (M, K), N = a.shape, b.shape[1]
        dm, dn, dk = (1024 if K <= 3072 else 512), 1024, K
    else:
        (K, M), N = a.shape, b.shape[1]
        dm, dn, dk = 1536, 1024, 2048
    tm, tn, tk = _tile(M, tm_t or dm), _tile(N, tn_t or dn), _tile(K, tk_t or dk)
    nk = K // tk
    dims = {"nt": NT, "nn": NN, "tn": TN}[mode]
    a_spec = pl.BlockSpec((tk, tm), lambda i, j, k: (k, i)) if mode == "tn" else pl.BlockSpec((tm, tk), lambda i, j, k: (i, k))
    b_spec = pl.BlockSpec((tn, tk), lambda i, j, k: (j, k)) if mode == "nt" else pl.BlockSpec((tk, tn), lambda i, j, k: (k, j))
    gi, gj = M // tm, N // tn
    out_rows, row0, into = rows if rows else (M, 0, None)
    assert row0 % tm == 0 and not (comm and into is not None)
    o_spec = pl.BlockSpec((tm, tn), lambda i, j, k: (i + row0 // tm, j))
    has_res = res is not None
    nc = 1 if into is not None else (comm.n if comm else 0)
    n_in = 3 if has_res else 2

    def body(*refs):
        a_ref, b_ref = refs[:2]
        r_ref = refs[2] if has_res else None
        nco = comm.n if comm else 0
        cin = refs[n_in:n_in + nc]
        o_ref = refs[n_in + nc]
        cout = refs[n_in + nc + 1:n_in + nc + 1 + nco]
        scratch = refs[n_in + nc + 1 + nco:]
        csem = scratch[1:] if nk > 1 else scratch
        step = (pl.program_id(0) * gj + pl.program_id(1)) * nk + pl.program_id(2)
        _host_comm(comm, "early", step, gi * gj * nk, cin, cout, csem)
        p = _dot(a_ref[...].astype(BF16), b_ref[...].astype(BF16), dims)
        if nk == 1:
            if has_res:
                p = p + r_ref[...]
            o_ref[...] = p.astype(out_dtype)
        else:
            acc = scratch[0]
            k = pl.program_id(2)

            @pl.when(k == 0)
            def _():
                acc[...] = p

            @pl.when(k > 0)
            def _():
                acc[...] += p

            @pl.when(k == nk - 1)
            def _():
                t = acc[...]
                if has_res:
                    t = t + r_ref[...]
                o_ref[...] = t.astype(out_dtype)
        _host_comm(comm, "late", step, gi * gj * nk, cin, cout, csem)

    in_specs = [a_spec, b_spec] + ([o_spec] if has_res else []) + [ANY] * nc
    args = (a, b) + ((res,) if has_res else ()) + ((into,) if into is not None else tuple(comm.arrays if comm else ()))
    out = pl.pallas_call(
        body,
        grid=(gi, gj, nk),
        in_specs=in_specs,
        out_specs=[o_spec] + ([ANY] * nc if comm else []),
        out_shape=[jax.ShapeDtypeStruct((out_rows, N), out_dtype)] + (comm.out_shape if comm else []),
        scratch_shapes=([pltpu.VMEM((tm, tn), F32)] if nk > 1 else []) + (comm.scratch if comm else []),
        input_output_aliases={n_in: 0} if into is not None else {},
        compiler_params=_cp("arbitrary", "arbitrary", "arbitrary") if comm else _cp("parallel", "parallel", "arbitrary"),
        name=name,
    )(*args)
    return (out[0], list(out[1:])) if comm else out[0]


def _mm_rows(a, b, res, name, norm_g=None, bwd=None, loss=None, second=None, tm=512):
    M, K = a.shape
    Dm = b.shape[1]
    tm = min(tm, M)
    row = pl.BlockSpec((tm, Dm), lambda i: (i, 0))
    vec = pl.BlockSpec((1, Dm), lambda i: (0, 0))
    a_spec = pl.BlockSpec((tm, K), lambda i: (i, 0))
    b_spec = pl.BlockSpec((K, Dm), lambda i: (0, 0))

    if loss is not None:
        target, g = loss

        def body(a_ref, b_ref, r_ref, g_ref, t_ref, loss_ref, dh_ref, dg_ref):
            @pl.when(pl.program_id(0) == 0)
            def _():
                loss_ref[...] = jnp.zeros_like(loss_ref)
                dg_ref[...] = jnp.zeros_like(dg_ref)

            x = _dot(a_ref[...].astype(BF16), b_ref[...].astype(BF16), NN) + r_ref[...]
            gv = g_ref[...]
            r = lax.rsqrt(jnp.mean(x * x, axis=-1, keepdims=True) + EPS)
            xhat = x * r
            e = xhat * gv - t_ref[...]
            loss_ref[...] += jnp.full((1, 128), 0.5 / Dm, F32) * jnp.sum(e * e)
            dy = e * (1.0 / Dm)
            dg_ref[...] += jnp.sum(dy * xhat, axis=0, keepdims=True)
            dx = dy * gv
            m = jnp.mean(dx * xhat, axis=-1, keepdims=True)
            dh_ref[...] = r * (dx - xhat * m)

        return pl.pallas_call(
            body, grid=(M // tm,), in_specs=[a_spec, b_spec, row, vec, row],
            out_specs=[pl.BlockSpec((1, 128), lambda i: (0, 0)), row, vec],
            out_shape=[jax.ShapeDtypeStruct((1, 128), F32), jax.ShapeDtypeStruct((M, Dm), F32), jax.ShapeDtypeStruct((1, Dm), F32)],
            compiler_params=_cp("arbitrary"), name=name,
        )(a, b, res, g.reshape(1, Dm), target)

    if bwd is None:
        def body(a_ref, b_ref, r_ref, g_ref, o_ref, n_ref):
            t = _dot(a_ref[...].astype(BF16), b_ref[...].astype(BF16), NN) + r_ref[...]
            o_ref[...] = t
            r = lax.rsqrt(jnp.mean(t * t, axis=-1, keepdims=True) + EPS)
            n_ref[...] = (t * r * g_ref[...]).astype(n_ref.dtype)

        return pl.pallas_call(
            body, grid=(M // tm,), in_specs=[a_spec, b_spec, row, vec], out_specs=[row, row],
            out_shape=[jax.ShapeDtypeStruct((M, Dm), F32), jax.ShapeDtypeStruct((M, Dm), BF16)],
            compiler_params=_cp("parallel"), name=name,
        )(a, b, res, norm_g.reshape(1, Dm))

    h, g, dres = bwd
    has_res = res is not None
    has2 = second is not None

    def body(*refs):
        a_ref, b_ref = refs[:2]
        h_ref, g_ref, dres_ref, dh_ref, dg_ref = refs[-5:]

        @pl.when(pl.program_id(0) == 0)
        def _():
            dg_ref[...] = jnp.zeros_like(dg_ref)

        d = _dot(a_ref[...].astype(BF16), b_ref[...].astype(BF16), NN)
        if has2:
            d = d + _dot(refs[2][...].astype(BF16), refs[3][...].astype(BF16), NN)
        if has_res:
            d = d + refs[4 if has2 else 2][...]
        x = h_ref[...]
        r = lax.rsqrt(jnp.mean(x * x, axis=-1, keepdims=True) + EPS)
        xhat = x * r
        dg_ref[...] += jnp.sum(d * xhat, axis=0, keepdims=True)
        dx = d * g_ref[...]
        m = jnp.mean(dx * xhat, axis=-1, keepdims=True)
        dh_ref[...] = dres_ref[...] + r * (dx - xhat * m)

    second_specs = [pl.BlockSpec((tm, second[0].shape[1]), lambda i: (i, 0)),
                    pl.BlockSpec(second[1].shape, lambda i: (0, 0))] if has2 else []
    return pl.pallas_call(
        body, grid=(M // tm,), in_specs=[a_spec, b_spec] + second_specs + ([row] if has_res else []) + [row, vec, row],
        out_specs=[row, vec],
        out_shape=[jax.ShapeDtypeStruct((M, Dm), F32), jax.ShapeDtypeStruct((1, Dm), F32)],
        compiler_params=_cp("arbitrary"), name=name,
    )(a, b, *(second if has2 else ()), *((res,) if has_res else ()), h, g.reshape(1, Dm), dres)


def _rms_fwd(h, g, comm=None, name="rms_fwd", tm=512):
    T, Dm = h.shape
    tm = min(tm, T)
    nc = comm.n if comm else 0
    ni = T // tm

    def body(*refs):
        h_ref, g_ref = refs[:2]
        cin, o_ref, cout, csem = refs[2:2 + nc], refs[2 + nc], refs[3 + nc:3 + 2 * nc], refs[3 + 2 * nc:]
        step = pl.program_id(0)
        _host_comm(comm, "early", step, ni, cin, cout, csem)
        x = h_ref[...]
        r = lax.rsqrt(jnp.mean(x * x, axis=-1, keepdims=True) + EPS)
        o_ref[...] = (x * r * g_ref[...]).astype(o_ref.dtype)
        _host_comm(comm, "late", step, ni, cin, cout, csem)

    out = pl.pallas_call(
        body,
        grid=(ni,),
        in_specs=[pl.BlockSpec((tm, Dm), lambda i: (i, 0)), pl.BlockSpec((1, Dm), lambda i: (0, 0))] + [ANY] * nc,
        out_specs=[pl.BlockSpec((tm, Dm), lambda i: (i, 0))] + [ANY] * nc,
        out_shape=[jax.ShapeDtypeStruct((T, Dm), BF16)] + (comm.out_shape if comm else []),
        scratch_shapes=comm.scratch if comm else [],
        compiler_params=_cp("arbitrary"),
        name=name,
    )(h, g.reshape(1, Dm), *(comm.arrays if comm else []))
    return (out[0], list(out[1:])) if comm else out[0]


HALO = 16


def _conv3(ext, w_ref, b_ref):
    return w_ref[0:1, :] * pltpu.roll(ext, 2, 0) + w_ref[1:2, :] * pltpu.roll(ext, 1, 0) + w_ref[2:3, :] * ext + b_ref[...]


def _ffn_up_mid(hf, w_up_t, cw, cb, name="ffn_up_mid", tm=512, tc=1408):
    T, Dm = hf.shape
    Fh = w_up_t.shape[0] // 2
    tm = min(tm, T)
    nj, ni = Fh // tc, T // tm

    rc = min(256, tm)

    def body(h_ref, wg_ref, wu_ref, cwg_ref, cwu_ref, cbg_ref, cbu_ref, zg_ref, zu_ref, a_ref, eg_ref, eu_ref, ng_ref, nu_ref):
        @pl.when(pl.program_id(1) == 0)
        def _():
            eg_ref[...] = jnp.zeros_like(eg_ref)
            eu_ref[...] = jnp.zeros_like(eu_ref)

        for r in range(tm // rc):
            ext = slice(r * rc, r * rc + HALO + rc)
            cg = _conv3(eg_ref[ext, :].astype(F32), cwg_ref, cbg_ref)[HALO:]
            cu = _conv3(eu_ref[ext, :].astype(F32), cwu_ref, cbu_ref)[HALO:]
            a_ref[r * rc:(r + 1) * rc, :] = _gelu(cg.astype(BF16)) * cu.astype(BF16)
            h = h_ref[r * rc:(r + 1) * rc, :]
            for w_ref, z_ref, n_ref in ((wg_ref, zg_ref, ng_ref), (wu_ref, zu_ref, nu_ref)):
                z = _dot(h, w_ref[...], NT).astype(BF16)
                z_ref[r * rc:(r + 1) * rc, :] = z
                n_ref[r * rc:(r + 1) * rc, :] = z
        for e_ref, n_ref in ((eg_ref, ng_ref), (eu_ref, nu_ref)):
            e_ref[0:HALO, :] = e_ref[tm:tm + HALO, :]
            e_ref[HALO:HALO + tm, :] = n_ref[...]

    last = ni - 1
    z_spec = pl.BlockSpec((tm, tc), lambda j, i: (jnp.minimum(i, last), j))
    return pl.pallas_call(
        body,
        grid=(nj, ni + 1),
        in_specs=[pl.BlockSpec((tm, Dm), lambda j, i: (jnp.minimum(i, last), 0)),
                  pl.BlockSpec((tc, Dm), lambda j, i: (j, 0)), pl.BlockSpec((tc, Dm), lambda j, i: (nj + j, 0)),
                  pl.BlockSpec((3, tc), lambda j, i: (0, j)), pl.BlockSpec((3, tc), lambda j, i: (0, nj + j)),
                  pl.BlockSpec((1, tc), lambda j, i: (0, j)), pl.BlockSpec((1, tc), lambda j, i: (0, nj + j))],
        out_specs=[z_spec, z_spec, pl.BlockSpec((tm, tc), lambda j, i: (jnp.maximum(i - 1, 0), j))],
        out_shape=[jax.ShapeDtypeStruct((T, Fh), BF16)] * 3,
        scratch_shapes=[pltpu.VMEM((HALO + tm, tc), BF16), pltpu.VMEM((HALO + tm, tc), BF16),
                        pltpu.VMEM((tm, tc), BF16), pltpu.VMEM((tm, tc), BF16)],
        compiler_params=_cp("arbitrary", "arbitrary"),
        name=name,
    )(hf, w_up_t, w_up_t, cw, cw, cb, cb)


def _ffn_mid_bwd(zg, zu, da, cw, cb, comm=None, name="ffn_mid_bwd", tm=1024, tc=256):
    T, Fh = zg.shape
    tm = min(tm, T)
    nj = Fh // tc
    hb = tm // HALO
    nhb = T // HALO

    nc = comm.n if comm else 0
    ni = T // tm

    def body(*refs):
        zg_ref, zu_ref, zgp_ref, zup_ref, zgn_ref, zun_ref, da_ref, dan_ref, wg_ref, wu_ref, bg_ref, bu_ref = refs[:12]
        cin = refs[12:12 + nc]
        dzg_ref, dzu_ref, dwg_ref, dwu_ref, dbg_ref, dbu_ref = refs[12 + nc:18 + nc]
        cout = refs[18 + nc:18 + 2 * nc]
        csem = refs[18 + 2 * nc:]
        i = pl.program_id(1)
        step = pl.program_id(0) * ni + i
        _host_comm(comm, "early", step, nj * ni, cin, cout, csem)
        first = i == 0
        last = i == ni - 1

        @pl.when(first)
        def _():
            dwg_ref[...] = jnp.zeros_like(dwg_ref)
            dwu_ref[...] = jnp.zeros_like(dwu_ref)
            dbg_ref[...] = jnp.zeros_like(dbg_ref)
            dbu_ref[...] = jnp.zeros_like(dbu_ref)

        def ext_of(p_ref, c_ref, n_ref):
            p = jnp.where(first, 0.0, p_ref[...].astype(F32))
            return jnp.concatenate([p, c_ref[...].astype(F32), n_ref[...].astype(F32)], axis=0)

        zge = ext_of(zgp_ref, zg_ref, zgn_ref)
        zue = ext_of(zup_ref, zu_ref, zun_ref)
        dan = jnp.where(last, 0.0, dan_ref[...].astype(F32))
        dae = jnp.concatenate([jnp.zeros((HALO, tc), F32), da_ref[...].astype(F32), dan], axis=0)
        zg1, zg2 = pltpu.roll(zge, 1, 0), pltpu.roll(zge, 2, 0)
        zu1, zu2 = pltpu.roll(zue, 1, 0), pltpu.roll(zue, 2, 0)
        cg = wg_ref[0:1, :] * zg2 + wg_ref[1:2, :] * zg1 + wg_ref[2:3, :] * zge + bg_ref[...]
        cu = wu_ref[0:1, :] * zu2 + wu_ref[1:2, :] * zu1 + wu_ref[2:3, :] * zue + bu_ref[...]
        gel, dgel = (t.astype(F32) for t in _gelu_and_grad(cg.astype(BF16)))
        dcg = dae * cu * dgel
        dcu = dae * gel
        lo, hi = HALO, HALO + tm

        def back(dc, taps, w_ref, dz_ref, dw_ref, db_ref):
            n = dc.shape[0]
            dz = w_ref[2:3, :] * dc + w_ref[1:2, :] * pltpu.roll(dc, n - 1, 0) + w_ref[0:1, :] * pltpu.roll(dc, n - 2, 0)
            dz_ref[...] = dz[lo:hi].astype(dz_ref.dtype)
            dcc = dc[lo:hi]
            db_ref[...] += jnp.sum(dcc, axis=0, keepdims=True)
            for k, tap in enumerate(taps):
                dw_ref[k:k + 1, :] += jnp.sum(dcc * tap[lo:hi], axis=0, keepdims=True)

        back(dcg, (zg2, zg1, zge), wg_ref, dzg_ref, dwg_ref, dbg_ref)
        back(dcu, (zu2, zu1, zue), wu_ref, dzu_ref, dwu_ref, dbu_ref)
        _host_comm(comm, "late", step, nj * ni, cin, cout, csem)

    cur = pl.BlockSpec((tm, tc), lambda j, i: (i, j))
    prev = pl.BlockSpec((HALO, tc), lambda j, i: (jnp.maximum(i * hb - 1, 0), j))
    nxt = pl.BlockSpec((HALO, tc), lambda j, i: (jnp.minimum((i + 1) * hb, nhb - 1), j))
    wg = pl.BlockSpec((3, tc), lambda j, i: (0, j))
    wu = pl.BlockSpec((3, tc), lambda j, i: (0, j + nj))
    bg = pl.BlockSpec((1, tc), lambda j, i: (0, j))
    bu = pl.BlockSpec((1, tc), lambda j, i: (0, j + nj))
    dw = pl.BlockSpec((3, tc), lambda j, i: (0, j))
    db = pl.BlockSpec((1, tc), lambda j, i: (0, j))
    out = pl.pallas_call(
        body,
        grid=(nj, ni),
        in_specs=[cur, cur, prev, prev, nxt, nxt, cur, nxt, wg, wu, bg, bu] + [ANY] * nc,
        out_specs=[cur, cur, dw, dw, db, db] + [ANY] * nc,
        out_shape=[jax.ShapeDtypeStruct((T, Fh), BF16), jax.ShapeDtypeStruct((T, Fh), BF16),
                   jax.ShapeDtypeStruct((3, Fh), F32), jax.ShapeDtypeStruct((3, Fh), F32),
                   jax.ShapeDtypeStruct((1, Fh), F32), jax.ShapeDtypeStruct((1, Fh), F32)] + (comm.out_shape if comm else []),
        scratch_shapes=comm.scratch if comm else [],
        compiler_params=_cp("arbitrary", "arbitrary"),
        name=name,
    )(zg, zu, zg, zu, zg, zu, da, da, cw, cw, cb, cb, *(comm.arrays if comm else []))
    dzg, dzu, dwg, dwu, dbg, dbu = out[:6]
    return dzg, dzu, jnp.concatenate([dwg, dwu], axis=1), jnp.concatenate([dbg, dbu], axis=1), list(out[6:])


def _sgu_mask():
    r = lax.broadcasted_iota(jnp.int32, (SGU_BLOCK, SGU_BLOCK), 0)
    c = lax.broadcasted_iota(jnp.int32, (SGU_BLOCK, SGU_BLOCK), 1)
    return (c < CHUNK) | (r >= CHUNK)


def _sgu_fwd(zz, ln_g, ln_b, w_s, b_s, name="sgu_fwd", tm=256):
    T = zz.shape[0]
    tm = min(tm, T)
    W = SGU_WIDTH

    def body(zu_ref, zv_ref, g_ref, b_ref, ws_ref, bs_ref, y_ref):
        u = _gelu(zu_ref[...]).astype(F32)
        v = _gelu(zv_ref[...]).astype(F32)
        mu = jnp.mean(v, axis=-1, keepdims=True)
        xc = v - mu
        rstd = lax.rsqrt(jnp.mean(xc * xc, axis=-1, keepdims=True) + EPS)
        vn = (xc * rstd * g_ref[...] + b_ref[...]).astype(BF16)
        mask = _sgu_mask()
        for g in range(SGU_GROUPS):
            wm = jnp.where(mask, ws_ref[g], 0.0).astype(BF16)
            cs = slice(g * SGU_GW, (g + 1) * SGU_GW)
            for blk in range(tm // SGU_BLOCK):
                rs = slice(blk * SGU_BLOCK, (blk + 1) * SGU_BLOCK)
                mixed = _dot(wm, vn[rs, cs], NN) + bs_ref[g]
                y_ref[rs, cs] = (u[rs, cs] * mixed).astype(y_ref.dtype)

    return pl.pallas_call(
        body,
        grid=(T // tm,),
        in_specs=[pl.BlockSpec((tm, W), lambda i: (i, 0)), pl.BlockSpec((tm, W), lambda i: (i, 1)),
                  pl.BlockSpec((1, W), lambda i: (0, 0)), pl.BlockSpec((1, W), lambda i: (0, 0)),
                  pl.BlockSpec((SGU_GROUPS, SGU_BLOCK, SGU_BLOCK), lambda i: (0, 0, 0)),
                  pl.BlockSpec((SGU_GROUPS, SGU_BLOCK, 1), lambda i: (0, 0, 0))],
        out_specs=pl.BlockSpec((tm, W), lambda i: (i, 0)),
        out_shape=jax.ShapeDtypeStruct((T, W), BF16),
        compiler_params=_cp("parallel"),
        name=name,
    )(zz, zz, ln_g.reshape(1, W), ln_b.reshape(1, W), w_s, b_s.reshape(SGU_GROUPS, SGU_BLOCK, 1))


def _sgu_bwd(zz, dy, ln_g, ln_b, w_s, b_s, name="sgu_bwd", tm=256):
    T = zz.shape[0]
    tm = min(tm, T)
    W = SGU_WIDTH

    def body(zu_ref, zv_ref, dy_ref, g_ref, b_ref, ws_ref, bs_ref, dzz_ref, dws_ref, dbs_ref, dg_ref, db_ref, dvn_ref):
        i = pl.program_id(0)

        @pl.when(i == 0)
        def _():
            dws_ref[...] = jnp.zeros_like(dws_ref)
            dbs_ref[...] = jnp.zeros_like(dbs_ref)
            dg_ref[...] = jnp.zeros_like(dg_ref)
            db_ref[...] = jnp.zeros_like(db_ref)

        u, du_dz = (t.astype(F32) for t in _gelu_and_grad(zu_ref[...]))
        v, dv_dz = (t.astype(F32) for t in _gelu_and_grad(zv_ref[...]))
        mu = jnp.mean(v, axis=-1, keepdims=True)
        xc = v - mu
        rstd = lax.rsqrt(jnp.mean(xc * xc, axis=-1, keepdims=True) + EPS)
        xhat = xc * rstd
        gv = g_ref[...]
        vn = (xhat * gv + b_ref[...]).astype(BF16)
        dyv = dy_ref[...].astype(F32)
        mask = _sgu_mask()
        for g in range(SGU_GROUPS):
            wm = jnp.where(mask, ws_ref[g], 0.0).astype(BF16)
            cs = slice(g * SGU_GW, (g + 1) * SGU_GW)
            dw_acc = jnp.zeros((SGU_BLOCK, SGU_BLOCK), F32)
            db_acc = jnp.zeros((SGU_BLOCK, 1), F32)
            for blk in range(tm // SGU_BLOCK):
                rs = slice(blk * SGU_BLOCK, (blk + 1) * SGU_BLOCK)
                vn_bg = vn[rs, cs]
                mixed = _dot(wm, vn_bg, NN) + bs_ref[g]
                dy_bg = dyv[rs, cs]
                dmixed = dy_bg * u[rs, cs]
                dmb = dmixed.astype(BF16)
                dw_acc += _dot(dmb, vn_bg, NT)
                db_acc += jnp.sum(dmixed, axis=1, keepdims=True)
                dvn_ref[rs, cs] = _dot(wm, dmb, TN)
                dzz_ref[rs, cs] = (dy_bg * mixed * du_dz[rs, cs]).astype(dzz_ref.dtype)
            dws_ref[g] += jnp.where(mask, dw_acc, 0.0)
            dbs_ref[g] += db_acc
        dvn = dvn_ref[...]
        dg_ref[...] += jnp.sum(dvn * xhat, axis=0, keepdims=True)
        db_ref[...] += jnp.sum(dvn, axis=0, keepdims=True)
        dxh = dvn * gv
        m1 = jnp.mean(dxh, axis=-1, keepdims=True)
        m2 = jnp.mean(dxh * xhat, axis=-1, keepdims=True)
        dv = rstd * (dxh - m1 - xhat * m2)
        dzz_ref[:, W:] = (dv * dv_dz).astype(dzz_ref.dtype)

    vec = pl.BlockSpec((1, W), lambda i: (0, 0))
    ws_spec = pl.BlockSpec((SGU_GROUPS, SGU_BLOCK, SGU_BLOCK), lambda i: (0, 0, 0))
    bs_spec = pl.BlockSpec((SGU_GROUPS, SGU_BLOCK, 1), lambda i: (0, 0, 0))
    return pl.pallas_call(
        body,
        grid=(T // tm,),
        in_specs=[pl.BlockSpec((tm, W), lambda i: (i, 0)), pl.BlockSpec((tm, W), lambda i: (i, 1)),
                  pl.BlockSpec((tm, W), lambda i: (i, 0)), vec, vec, ws_spec, bs_spec],
        out_specs=[pl.BlockSpec((tm, 2 * W), lambda i: (i, 0)), ws_spec, bs_spec, vec, vec],
        out_shape=[jax.ShapeDtypeStruct((T, 2 * W), BF16),
                   jax.ShapeDtypeStruct((SGU_GROUPS, SGU_BLOCK, SGU_BLOCK), F32),
                   jax.ShapeDtypeStruct((SGU_GROUPS, SGU_BLOCK, 1), F32),
                   jax.ShapeDtypeStruct((1, W), F32), jax.ShapeDtypeStruct((1, W), F32)],
        scratch_shapes=[pltpu.VMEM((tm, W), F32)],
        compiler_params=_cp("arbitrary"),
        name=name,
    )(zz, zz, dy, ln_g.reshape(1, W), ln_b.reshape(1, W), w_s, b_s.reshape(SGU_GROUPS, SGU_BLOCK, 1))


RET_TR = 256
RET_BLK = 256
QK_SCALE = RET_QK_DIM ** -0.5


def _ret_tables(T):
    half = RET_QK_DIM // 2
    inv = 1.0 / (10000.0 ** jnp.linspace(0.0, 1.0, half, dtype=F32))
    inv2 = jnp.concatenate([inv, inv])[None, :]
    sgn = jnp.concatenate([-jnp.ones((half,), F32), jnp.ones((half,), F32)])[None, :]
    tr = min(RET_TR, T)

    def trig(pos):
        ang = pos.astype(F32)[:, None] * inv2
        return jnp.stack([jnp.cos(ang), jnp.sin(ang), sgn * jnp.sin(ang)])

    tile_tab = jnp.pad(trig(jnp.arange(T // tr) * tr).transpose(1, 0, 2), ((0, 0), (0, 5), (0, 0)))
    row_tab = trig(jnp.arange(tr))
    log_g = jnp.log1p(-jnp.exp2(-5.0 - jnp.arange(RET_HEADS, dtype=F32)))
    idx = jnp.arange(RET_BLK, dtype=F32)
    dist = idx[:, None] - idx[None, :]
    cq, ck = jnp.arange(RET_BLK)[:, None] // CHUNK, jnp.arange(RET_BLK)[None, :] // CHUNK
    expo = jnp.where(ck == cq, jnp.abs(dist), dist)
    d_blk = jnp.where((ck <= cq)[None], jnp.exp(log_g[:, None, None] * expo[None]), 0.0)
    k_dec = jnp.exp(log_g[:, None] * (RET_BLK - 1 - idx)[None, :])[:, :, None]
    q_dec = jnp.exp(log_g[:, None] * (idx + 1.0)[None, :])[:, :, None]
    c_dec = jnp.exp(log_g * RET_BLK)[:, None, None]
    return tile_tab, row_tab, d_blk, q_dec, k_dec, c_dec


def _rot(x, c, s):
    return x * c + pltpu.roll(x, RET_QK_DIM // 2, 1) * s


def _rot_tables(tt_ref, rt_ref):
    ca, sa, ga = tt_ref[0:1, :], tt_ref[1:2, :], tt_ref[2:3, :]
    cb, sb, gb = rt_ref[0], rt_ref[1], rt_ref[2]
    return ca * cb - sa * sb, ga * cb + ca * gb


def _ret_specs(tr, rev, nb):
    ix = (lambda n: nb - 1 - n) if rev else (lambda n: n)
    tt = pl.BlockSpec((None, 8, RET_QK_DIM), lambda n: (ix(n), 0, 0))
    rt = pl.BlockSpec((3, tr, RET_QK_DIM), lambda n: (0, 0, 0))
    dm = pl.BlockSpec((RET_HEADS, RET_BLK, RET_BLK), lambda n: (0, 0, 0))
    dv = pl.BlockSpec((RET_HEADS, RET_BLK, 1), lambda n: (0, 0, 0))
    dc = pl.BlockSpec((RET_HEADS, 1, 1), lambda n: (0, 0, 0))
    return ix, [tt, rt, dm, dv, dv, dc]


def _ret_fwd(z_a, tables, name="ret_fwd"):
    T = z_a.shape[0]
    tr = min(RET_TR, T)
    cpb = tr // RET_BLK
    nb = T // tr
    QW, VW = RET_HEADS * RET_QK_DIM, RET_HEADS * RET_V_DIM
    ix, tab_specs = _ret_specs(tr, False, nb)

    def body(z_ref, tt_ref, rt_ref, dm_ref, qd_ref, kd_ref, cd_ref, y_ref, st_ref, state):
        @pl.when(pl.program_id(0) == 0)
        def _():
            state[...] = jnp.zeros_like(state)

        rot_c, rot_s = _rot_tables(tt_ref, rt_ref)
        for c in range(cpb):
            for h in range(RET_HEADS):
                rs = slice(c * RET_BLK, (c + 1) * RET_BLK)
                cc, ss = rot_c[rs, :], rot_s[rs, :]
                q = z_ref[rs, h * RET_QK_DIM:(h + 1) * RET_QK_DIM].astype(F32)
                k = z_ref[rs, QW + h * RET_QK_DIM:QW + (h + 1) * RET_QK_DIM].astype(F32)
                v = z_ref[rs, 2 * QW + h * RET_V_DIM:2 * QW + (h + 1) * RET_V_DIM]
                gt = z_ref[rs, 2 * QW + VW + h * RET_V_DIM:2 * QW + VW + (h + 1) * RET_V_DIM].astype(F32)
                qr = _rot(q, cc, ss)
                kr = _rot(k, cc, ss) * QK_SCALE
                s_old = state[h]
                sb = s_old.astype(BF16)
                st_ref[c, h] = sb
                s = _dot(qr.astype(BF16), kr.astype(BF16), NT) * dm_ref[h]
                o = _dot(s.astype(BF16), v, NN) + _dot((qr * qd_ref[h]).astype(BF16), sb, NN)
                state[h] = s_old * cd_ref[h] + _dot((kr * kd_ref[h]).astype(BF16), v, TN)
                mu = jnp.mean(o, axis=-1, keepdims=True)
                oc = o - mu
                rn = oc * lax.rsqrt(jnp.mean(oc * oc, axis=-1, keepdims=True) + EPS)
                silu = gt / (1.0 + jnp.exp(-gt))
                y_ref[rs, h * RET_V_DIM:(h + 1) * RET_V_DIM] = (silu * rn).astype(y_ref.dtype)

    return pl.pallas_call(
        body,
        grid=(nb,),
        in_specs=[pl.BlockSpec((tr, RET_W), lambda n: (n, 0))] + tab_specs,
        out_specs=[pl.BlockSpec((tr, VW), lambda n: (n, 0)),
                   pl.BlockSpec((cpb, RET_HEADS, RET_QK_DIM, RET_V_DIM), lambda n: (n, 0, 0, 0))],
        out_shape=[jax.ShapeDtypeStruct((T, Y_COLS), BF16),
                   jax.ShapeDtypeStruct((T // RET_BLK, RET_HEADS, RET_QK_DIM, RET_V_DIM), BF16)],
        scratch_shapes=[pltpu.VMEM((RET_HEADS, RET_QK_DIM, RET_V_DIM), F32)],
        compiler_params=_cp("arbitrary"),
        name=name,
    )(z_a, *tables)


def _ret_bwd(z_a, dy, states, tables, name="ret_bwd"):
    T = z_a.shape[0]
    tr = min(RET_TR, T)
    cpb = tr // RET_BLK
    nb = T // tr
    QW, VW = RET_HEADS * RET_QK_DIM, RET_HEADS * RET_V_DIM
    ix, tab_specs = _ret_specs(tr, True, nb)

    def body(z_ref, dy_ref, st_ref, tt_ref, rt_ref, dm_ref, qd_ref, kd_ref, cd_ref, dz_ref, dstate):
        @pl.when(pl.program_id(0) == 0)
        def _():
            dstate[...] = jnp.zeros_like(dstate)

        rot_c, rot_s = _rot_tables(tt_ref, rt_ref)
        for c in reversed(range(cpb)):
            for h in range(RET_HEADS):
                rs = slice(c * RET_BLK, (c + 1) * RET_BLK)
                cc, ss = rot_c[rs, :], rot_s[rs, :]
                q = z_ref[rs, h * RET_QK_DIM:(h + 1) * RET_QK_DIM].astype(F32)
                k = z_ref[rs, QW + h * RET_QK_DIM:QW + (h + 1) * RET_QK_DIM].astype(F32)
                v = z_ref[rs, 2 * QW + h * RET_V_DIM:2 * QW + (h + 1) * RET_V_DIM]
                gt = z_ref[rs, 2 * QW + VW + h * RET_V_DIM:2 * QW + VW + (h + 1) * RET_V_DIM].astype(F32)
                dyv = dy_ref[rs, h * RET_V_DIM:(h + 1) * RET_V_DIM].astype(F32)
                dmat, qd, kd = dm_ref[h], qd_ref[h], kd_ref[h]
                qr = _rot(q, cc, ss)
                kr = _rot(k, cc, ss) * QK_SCALE
                qrb, krb = qr.astype(BF16), kr.astype(BF16)
                sb = st_ref[c, h]
                sd = (_dot(qrb, krb, NT) * dmat).astype(BF16)
                qdb = (qr * qd).astype(BF16)
                kdb = (kr * kd).astype(BF16)
                o = _dot(sd, v, NN) + _dot(qdb, sb, NN)
                mu = jnp.mean(o, axis=-1, keepdims=True)
                oc = o - mu
                rstd = lax.rsqrt(jnp.mean(oc * oc, axis=-1, keepdims=True) + EPS)
                rn = oc * rstd
                sg = 1.0 / (1.0 + jnp.exp(-gt))
                dgt = dyv * rn * (sg * (1.0 + gt * (1.0 - sg)))
                drn = dyv * (gt * sg)
                do = rstd * (drn - jnp.mean(drn, axis=-1, keepdims=True) - rn * jnp.mean(drn * rn, axis=-1, keepdims=True))
                dob = do.astype(BF16)
                dsn = dstate[h]
                dsnb = dsn.astype(BF16)
                ds_raw = (_dot(dob, v, NT) * dmat).astype(BF16)
                dv = _dot(sd, dob, TN) + _dot(kdb, dsnb, NN)
                dqr = _dot(ds_raw, krb, NN) + qd * _dot(dob, sb, NT)
                dkr = (_dot(ds_raw, qrb, TN) + kd * _dot(v, dsnb, NT)) * QK_SCALE
                dstate[h] = dsn * cd_ref[h] + _dot(qdb, dob, TN)
                dq = dqr * cc + pltpu.roll(dqr * ss, RET_QK_DIM // 2, 1)
                dk = dkr * cc + pltpu.roll(dkr * ss, RET_QK_DIM // 2, 1)
                dz_ref[rs, h * RET_QK_DIM:(h + 1) * RET_QK_DIM] = dq.astype(dz_ref.dtype)
                dz_ref[rs, QW + h * RET_QK_DIM:QW + (h + 1) * RET_QK_DIM] = dk.astype(dz_ref.dtype)
                dz_ref[rs, 2 * QW + h * RET_V_DIM:2 * QW + (h + 1) * RET_V_DIM] = dv.astype(dz_ref.dtype)
                dz_ref[rs, 2 * QW + VW + h * RET_V_DIM:2 * QW + VW + (h + 1) * RET_V_DIM] = dgt.astype(dz_ref.dtype)

    return pl.pallas_call(
        body,
        grid=(nb,),
        in_specs=[pl.BlockSpec((tr, RET_W), lambda n: (ix(n), 0)),
                  pl.BlockSpec((tr, VW), lambda n: (ix(n), 0)),
                  pl.BlockSpec((cpb, RET_HEADS, RET_QK_DIM, RET_V_DIM), lambda n: (ix(n), 0, 0, 0))] + tab_specs,
        out_specs=pl.BlockSpec((tr, RET_W), lambda n: (ix(n), 0)),
        out_shape=jax.ShapeDtypeStruct((T, RET_W), BF16),
        scratch_shapes=[pltpu.VMEM((RET_HEADS, RET_QK_DIM, RET_V_DIM), F32)],
        compiler_params=_cp("arbitrary"),
        name=name,
    )(z_a, dy, states, *tables)


ATT_TQ = 256
ATT_CPB = ATT_TQ // CHUNK
ATT_SCALE = ATT_HEAD_DIM ** -0.5


ATT_WIN = 3 * ATT_TQ
ATT_NB = CHUNK * ATT_BAND


def _rel_index():
    i = np.arange(CHUNK)[:, None]
    j = np.arange(ATT_BAND)[None, :]
    rel = np.clip(i + ATT_PAST * CHUNK - j, -MAX_REL, MAX_REL) + MAX_REL
    return jnp.asarray(rel.reshape(1, ATT_NB).astype(np.int32))


def _split3(x):
    hi = x.astype(BF16)
    r1 = x - hi.astype(F32)
    mid = r1.astype(BF16)
    lo = (r1 - mid.astype(F32)).astype(BF16)
    return hi, mid, lo


REL_TILE = 4608


def _bias_expand(rel_bias, name="bias_expand"):
    H = rel_bias.shape[0]
    n = ATT_NB
    padded = jnp.pad(rel_bias, ((0, 0), (0, N_REL_PAD - N_REL)))

    def body(rb_ref, idx_ref, o_ref):
        onehot = (lax.broadcasted_iota(jnp.int32, (N_REL_PAD, REL_TILE), 0) == idx_ref[...]).astype(BF16)
        hi, mid, lo = _split3(rb_ref[...])
        o_ref[...] = _dot(hi, onehot, NN) + _dot(mid, onehot, NN) + _dot(lo, onehot, NN)

    out = pl.pallas_call(
        body,
        grid=(n // REL_TILE,),
        in_specs=[pl.BlockSpec((H, N_REL_PAD), lambda t: (0, 0)), pl.BlockSpec((1, REL_TILE), lambda t: (0, t))],
        out_specs=pl.BlockSpec((H, REL_TILE), lambda t: (0, t)),
        out_shape=jax.ShapeDtypeStruct((H, n), F32),
        compiler_params=_cp("parallel"),
        name=name,
    )(padded, _rel_index())
    return out.reshape(H, CHUNK, ATT_BAND)


def _bias_tile(band, name="bias_tile"):
    H = band.shape[0]
    padded = jnp.pad(band, ((0, 0), (0, 0), (0, ATT_WIN - ATT_BAND)), constant_values=NEG_INF)

    def body(b_ref, o_ref):
        b = b_ref[...]
        col = lax.broadcasted_iota(jnp.int32, (CHUNK, ATT_WIN), 1)
        keep = col >= (2 - pl.program_id(0)) * ATT_TQ
        for a in range(ATT_CPB):
            o_ref[a * CHUNK:(a + 1) * CHUNK, :] = jnp.where(keep, pltpu.roll(b, a * CHUNK, 1) if a else b, NEG_INF)

    return pl.pallas_call(
        body,
        grid=(3, H),
        in_specs=[pl.BlockSpec((None, CHUNK, ATT_WIN), lambda v, h: (h, 0, 0))],
        out_specs=pl.BlockSpec((None, None, ATT_TQ, ATT_WIN), lambda v, h: (v, h, 0, 0)),
        out_shape=jax.ShapeDtypeStruct((3, H, ATT_TQ, ATT_WIN), F32),
        compiler_params=_cp("parallel", "parallel"),
        name=name,
    )(padded)


def _bias_untile(dtile, name="bias_untile"):
    H = dtile.shape[0]

    def body(d_ref, o_ref):
        acc = d_ref[0:CHUNK, :]
        for a in range(1, ATT_CPB):
            acc = acc + pltpu.roll(d_ref[a * CHUNK:(a + 1) * CHUNK, :], ATT_WIN - a * CHUNK, 1)
        o_ref[...] = acc

    out = pl.pallas_call(
        body,
        grid=(H,),
        in_specs=[pl.BlockSpec((None, ATT_TQ, ATT_WIN), lambda h: (h, 0, 0))],
        out_specs=pl.BlockSpec((None, CHUNK, ATT_WIN), lambda h: (h, 0, 0)),
        out_shape=jax.ShapeDtypeStruct((H, CHUNK, ATT_WIN), F32),
        compiler_params=_cp("parallel"),
        name=name,
    )(dtile)
    return out[:, :, :ATT_BAND]


def _bias_reduce(dbias, name="bias_reduce"):
    H = dbias.shape[0]
    n = ATT_NB

    def body(db_ref, idx_ref, o_ref):
        @pl.when(pl.program_id(0) == 0)
        def _():
            o_ref[...] = jnp.zeros_like(o_ref)

        onehot = (lax.broadcasted_iota(jnp.int32, (N_REL_PAD, REL_TILE), 0) == idx_ref[...]).astype(BF16)
        hi, mid, lo = _split3(db_ref[...])
        o_ref[...] += _dot(hi, onehot, NT) + _dot(mid, onehot, NT) + _dot(lo, onehot, NT)

    out = pl.pallas_call(
        body,
        grid=(n // REL_TILE,),
        in_specs=[pl.BlockSpec((H, REL_TILE), lambda t: (0, t)), pl.BlockSpec((1, REL_TILE), lambda t: (0, t))],
        out_specs=pl.BlockSpec((H, N_REL_PAD), lambda t: (0, 0)),
        out_shape=jax.ShapeDtypeStruct((H, N_REL_PAD), F32),
        compiler_params=_cp("arbitrary"),
        name=name,
    )(dbias.reshape(H, n), _rel_index())
    return out[:, :N_REL]


def _att_probs(q, kwin, bias):
    s = _dot(q, kwin, NT) + bias
    e = jnp.exp(s - jnp.max(s, axis=-1, keepdims=True))
    return e * (1.0 / jnp.sum(e, axis=-1, keepdims=True))


ATT_PAIR = 2 * ATT_HEAD_DIM
ATT_NP = ATT_HEADS // 2
ATT_QW = ATT_HEADS * ATT_HEAD_DIM
Y_COLS = RET_HEADS * RET_V_DIM + ATT_QW


def _att_specs(tq, nq, clip_q, q_col0):
    cb = ATT_QW // ATT_PAIR
    qi = (lambda p, m: (jnp.minimum(m, nq - 1), q_col0 + p)) if clip_q else (lambda p, m: (m, q_col0 + p))
    q = pl.BlockSpec((tq, ATT_PAIR), qi)

    def win(col0):
        return [pl.BlockSpec((tq, ATT_PAIR), functools.partial(lambda p, m, back: (jnp.clip(m - back, 0, nq - 1), col0 + p), back=b))
                for b in (2, 1, 0)]

    bias = pl.BlockSpec((None, 2, ATT_TQ, ATT_WIN), lambda p, m: (jnp.minimum(m, 2), p, 0, 0))
    return q, win(cb), win(2 * cb), bias


def _head_masks(rows):
    lane = lax.broadcasted_iota(jnp.int32, (rows, ATT_PAIR), 1)
    return lane < ATT_HEAD_DIM


def _att_fwd(z_b, bias, y, comm=None, name="att_fwd"):
    T = z_b.shape[0]
    tq = ATT_TQ
    nq = T // tq
    qs, kwin, vwin, bs = _att_specs(tq, nq, False, 0)
    nc = comm.n if comm else 0
    total = ATT_NP * nq

    def body(*refs):
        q_ref, k0, k1, k2, v0, v1, v2, b_ref = refs[:8]
        cin = refs[9:9 + nc]
        o_ref = refs[9 + nc]
        cout = refs[10 + nc:10 + 2 * nc]
        csem = refs[10 + 2 * nc:]
        m = pl.program_id(1)
        step = pl.program_id(0) * nq + m
        _host_comm(comm, "early", step, total, cin, cout, csem)
        kw = jnp.concatenate([k0[...], k1[...], k2[...]], axis=0)
        vw = jnp.concatenate([v0[...], v1[...], v2[...]], axis=0)
        q2 = q_ref[...] * ATT_SCALE
        even = _head_masks(tq)
        outs = []
        for hh in range(2):
            qm = jnp.where(even if hh == 0 else ~even, q2, jnp.zeros_like(q2))
            p = _att_probs(qm, kw, b_ref[hh])
            outs.append(_dot(p.astype(BF16), vw, NN))
        o_ref[...] = jnp.where(even, outs[0], outs[1]).astype(o_ref.dtype)
        _host_comm(comm, "late", step, total, cin, cout, csem)

    y_cb = (Y_COLS - ATT_QW) // ATT_PAIR
    out = pl.pallas_call(
        body,
        grid=(ATT_NP, nq),
        in_specs=[qs] + kwin + vwin + [bs, ANY] + [ANY] * nc,
        out_specs=[pl.BlockSpec((tq, ATT_PAIR), lambda p, m: (m, y_cb + p))] + [ANY] * nc,
        out_shape=[jax.ShapeDtypeStruct((T, Y_COLS), BF16)] + (comm.out_shape if comm else []),
        scratch_shapes=comm.scratch if comm else [],
        input_output_aliases={8: 0},
        compiler_params=_cp("arbitrary", "arbitrary"),
        name=name,
    )(z_b, z_b, z_b, z_b, z_b, z_b, z_b, bias, y, *(comm.arrays if comm else []))
    return out[0], list(out[1:])


def _att_bwd(z_b, bias, dy, comm=None, name="att_bwd"):
    T = z_b.shape[0]
    tq = ATT_TQ
    nq = T // tq
    y_cb = (Y_COLS - ATT_QW) // ATT_PAIR
    qs, kwin, vwin, bs = _att_specs(tq, nq, True, 0)
    dos = _att_specs(tq, nq, True, y_cb)[0]
    kv_out = pl.BlockSpec((tq, ATT_PAIR), lambda p, m: (jnp.maximum(m - 2, 0), p))
    W3 = 3 * tq
    nc = comm.n if comm else 0
    total = ATT_NP * (nq + 2)

    def body(*refs):
        q_ref, k0, k1, k2, v0, v1, v2, b_ref, do_ref = refs[:9]
        cin = refs[9:9 + nc]
        dq_ref, dk_ref, dv_ref, db_ref = refs[9 + nc:13 + nc]
        cout = refs[13 + nc:13 + 2 * nc]
        dkc, dvc, dkw, dvw = refs[13 + 2 * nc:17 + 2 * nc]
        csem = refs[17 + 2 * nc:]
        m = pl.program_id(1)
        step = pl.program_id(0) * (nq + 2) + m
        _host_comm(comm, "early", step, total, cin, cout, csem)

        @pl.when(m == 0)
        def _():
            dkc[...] = jnp.zeros_like(dkc)
            dvc[...] = jnp.zeros_like(dvc)
            db_ref[...] = jnp.zeros_like(db_ref)

        @pl.when(m >= nq)
        def _():
            dkw[...] = jnp.zeros_like(dkw)
            dvw[...] = jnp.zeros_like(dvw)

        @pl.when(m < nq)
        def _():
            kw = jnp.concatenate([k0[...], k1[...], k2[...]], axis=0)
            vw = jnp.concatenate([v0[...], v1[...], v2[...]], axis=0)
            q2, do2 = q_ref[...] * ATT_SCALE, do_ref[...]
            even = _head_masks(tq)
            dqs, dks, dvs = [], [], []
            for hh in range(2):
                mine = even if hh == 0 else ~even
                p = _att_probs(jnp.where(mine, q2, jnp.zeros_like(q2)), kw, b_ref[hh])
                dp = _dot(jnp.where(mine, do2, jnp.zeros_like(do2)), vw, NT)
                ds = p * (dp - jnp.sum(dp * p, axis=-1, keepdims=True))
                db_ref[hh] += ds
                dsb = ds.astype(BF16)
                dqs.append(_dot(dsb, kw, NN))
                dks.append(_dot(dsb, q2, TN))
                dvs.append(_dot(p.astype(BF16), do2, TN))
            even_w = _head_masks(W3)
            dq_ref[...] = (jnp.where(even, dqs[0], dqs[1]) * ATT_SCALE).astype(dq_ref.dtype)
            dkw[...] = jnp.where(even_w, dks[0], dks[1])
            dvw[...] = jnp.where(even_w, dvs[0], dvs[1])

        dk_ref[...] = (dkc[0:tq, :] + dkw[0:tq, :]).astype(dk_ref.dtype)
        dv_ref[...] = (dvc[0:tq, :] + dvw[0:tq, :]).astype(dv_ref.dtype)
        dkc[0:tq, :] = dkc[tq:2 * tq, :] + dkw[tq:2 * tq, :]
        dvc[0:tq, :] = dvc[tq:2 * tq, :] + dvw[tq:2 * tq, :]
        dkc[tq:2 * tq, :] = dkw[2 * tq:W3, :]
        dvc[tq:2 * tq, :] = dvw[2 * tq:W3, :]
        _host_comm(comm, "late", step, total, cin, cout, csem)

    qo = pl.BlockSpec((tq, ATT_PAIR), lambda p, m: (jnp.minimum(m, nq - 1), p))
    dbs = pl.BlockSpec((2, ATT_TQ, ATT_WIN), lambda p, m: (p, 0, 0))
    hd = jax.ShapeDtypeStruct((T, ATT_QW), BF16)
    out = pl.pallas_call(
        body,
        grid=(ATT_NP, nq + 2),
        in_specs=[qs] + kwin + vwin + [bs, dos] + [ANY] * nc,
        out_specs=[qo, kv_out, kv_out, dbs] + [ANY] * nc,
        out_shape=[hd, hd, hd, jax.ShapeDtypeStruct((ATT_HEADS, ATT_TQ, ATT_WIN), F32)] + (comm.out_shape if comm else []),
        scratch_shapes=[pltpu.VMEM((2 * tq, ATT_PAIR), F32), pltpu.VMEM((2 * tq, ATT_PAIR), F32),
                        pltpu.VMEM((W3, ATT_PAIR), F32), pltpu.VMEM((W3, ATT_PAIR), F32)] + (comm.scratch if comm else []),
        compiler_params=_cp("arbitrary", "arbitrary"),
        name=name,
    )(z_b, z_b, z_b, z_b, z_b, z_b, z_b, bias, dy, *(comm.arrays if comm else []))
    return out[0], out[1], out[2], out[3], list(out[4:])


REST = ["ab_out", "c_in_t", "c_out", "up_t0", "up_t1", "down0", "down1"]


def _local_step(x, target, wt, small, rest_shards=None, overlap=False, hn0=None):
    T = x.shape[0]
    Fh = FFN_HIDDEN
    tables = _ret_tables(T)
    gw, gs, recv = {}, {}, {}
    wt = dict(wt)

    if hn0 is None:
        hn0 = _rms_fwd(x, small["attn_norm_g"][0], name="rms_fwd")
    z_a = _mm(hn0, wt["ab_in_t"][:RET_W], "nt", BF16, name="mm_ab_in_a")
    z_b = _mm(hn0, wt["ab_in_t"][RET_W:], "nt", BF16, name="mm_ab_in_b")
    y, states = _ret_fwd(z_a, tables)
    bias = _bias_tile(_bias_expand(small["rel_bias"]))
    y, rest = _att_fwd(z_b, bias, y, comm=_Comm("gather", rest_shards) if rest_shards is not None else None)
    if rest_shards is not None:
        full = dict(zip(REST, rest))
        wt.update(ab_out=full["ab_out"], c_in_t=full["c_in_t"], c_out=full["c_out"],
                  up_t=[full["up_t0"], full["up_t1"]], down=[full["down0"], full["down1"]])
    h1, hf0 = _mm_rows(y, wt["ab_out"], x, "mm_ab_out", norm_g=small["ffn_norm_g"][0])

    def ffn_fwd(h, hf, layer, next_g):
        zg, zu, a = _ffn_up_mid(hf, wt["up_t"][layer], small["conv_w"][layer], small["conv_b"][layer][None, :])
        if next_g is None:
            return _mm_rows(a, wt["down"][layer], h, "mm_down_loss", loss=(target, small["final_g"])), (hf, zg, zu, a)
        h_out, hn_next = _mm_rows(a, wt["down"][layer], h, "mm_down", norm_g=next_g)
        return h_out, hn_next, (hf, zg, zu, a)

    def ffn_bwd(dh_out, h, layer, saved, exchange=None):
        hf, zg, zu, a = saved
        da = _mm(dh_out, wt["down"][layer], "nt", BF16, name="mm_d_a")
        d_down = _mm(a, dh_out, "tn", BF16, name="mm_dw_down")
        comm = _Comm("exchange", [gw[k] for k in exchange]) if exchange else None
        dzg, dzu, dcw, dcb, got = _ffn_mid_bwd(zg, zu, da, small["conv_w"][layer], small["conv_b"][layer][None, :], comm=comm)
        recv.update(zip(exchange or [], got))
        dh, dg = _mm_rows(dzg, wt["up_t"][layer][:Fh], None, "mm_d_hf_norm", second=(dzu, wt["up_t"][layer][Fh:]),
                          bwd=(h, small["ffn_norm_g"][layer], dh_out))
        d_up = _mm(dzg, hf, "tn", BF16, name="mm_dw_up", rows=(2 * Fh, 0, None))
        d_up = _mm(dzu, hf, "tn", BF16, name="mm_dw_up_2", rows=(2 * Fh, Fh, d_up))
        return dh, dg, d_up, d_down, dcw, dcb

    h2, hn1, ffn0 = ffn_fwd(h1, hf0, 0, small["attn_norm_g"][1])

    zz = _mm(hn1, wt["c_in_t"], "nt", BF16, name="mm_c_in")
    ys = _sgu_fwd(zz, small["ln_g"], small["ln_b"], small["w_s"], small["b_s"])
    h3, hf1 = _mm_rows(ys, wt["c_out"], h2, "mm_c_out", norm_g=small["ffn_norm_g"][1])
    (loss_vec, dh4, gs["final_g"]), ffn1 = ffn_fwd(h3, hf1, 1, None)

    dh3, dgf1, gw["up_t1"], gw["down1"], dcw1, dcb1 = ffn_bwd(dh4, h3, 1, ffn1)
    dys = _mm(dh3, wt["c_out"], "nt", BF16, name="mm_d_ys")
    gw["c_out"] = _mm(ys, dh3, "tn", BF16, name="mm_dw_c_out")
    dzz, gs["w_s"], dbs, dlg, dlb = _sgu_bwd(zz, dys, small["ln_g"], small["ln_b"], small["w_s"], small["b_s"])
    gs["b_s"], gs["ln_g"], gs["ln_b"] = dbs[:, :, 0], dlg[0], dlb[0]
    dh2, dga1 = _mm_rows(dzz, wt["c_in_t"], None, "mm_d_hn1_norm", bwd=(h2, small["attn_norm_g"][1], dh3))
    gw["c_in_t"] = _mm(dzz, hn1, "tn", BF16, name="mm_dw_c_in")

    dh1, dgf0, gw["up_t0"], gw["down0"], dcw0, dcb0 = ffn_bwd(
        dh2, h1, 0, ffn0, exchange=["c_in_t", "c_out", "up_t1", "down1"] if overlap else None)

    dy = _mm(dh1, wt["ab_out"], "nt", BF16, name="mm_d_y")
    gw["ab_out"] = _mm(y, dh1, "tn", BF16, name="mm_dw_ab_out")
    dz_a = _ret_bwd(z_a, dy, states, tables)
    late = ["ab_out", "up_t0", "down0"] if overlap else []
    dq, dk, dv, dbias, got = _att_bwd(z_b, bias, dy, comm=_Comm("exchange", [gw[k] for k in late]) if late else None)
    recv.update(zip(late, got))
    dz_b = jnp.concatenate([dq, dk, dv], axis=1)
    gs["rel_bias"] = _bias_reduce(_bias_untile(dbias))
    gw["ab_in_t"] = _mm(dz_a, hn0, "tn", BF16, name="mm_dw_ab_in_a", rows=(RET_W + ATT_W, 0, None))
    gw["ab_in_t"] = _mm(dz_b, hn0, "tn", BF16, name="mm_dw_ab_in_b", rows=(RET_W + ATT_W, RET_W, gw["ab_in_t"]))
    if overlap:
        dhn0, got = _mm(dz_a, wt["ab_in_t"][:RET_W], "nn", F32, name="mm_d_hn0", comm=_Comm("exchange", [gw["ab_in_t"]]))
        recv["ab_in_t"] = got[0]
    else:
        dhn0 = _mm(dz_a, wt["ab_in_t"][:RET_W], "nn", F32, name="mm_d_hn0")
    grad_x, dga0 = _mm_rows(dz_b, wt["ab_in_t"][RET_W:], dhn0, "mm_d_hn0_norm", bwd=(x, small["attn_norm_g"][0], dh1))

    gs["attn_norm_g"] = jnp.concatenate([dga0, dga1], axis=0)
    gs["ffn_norm_g"] = jnp.concatenate([dgf0, dgf1], axis=0)
    gs["conv_w"] = jnp.stack([dcw0, dcw1])
    gs["conv_b"] = jnp.concatenate([dcb0, dcb1], axis=0)
    gs["final_g"] = gs["final_g"][0]
    return loss_vec[0, 0], grad_x, gw, gs, recv


MESH_ID = pl.DeviceIdType.MESH
ANY = pl.BlockSpec(memory_space=pl.ANY)


def _my_place():
    return lax.axis_index("x"), lax.axis_index("y"), lax.axis_index("c")


class _Comm:
    def __init__(self, kind, arrays):
        self.kind, self.arrays, self.n = kind, list(arrays), len(arrays)
        if kind == "gather":
            self.out_shape = [jax.ShapeDtypeStruct((N_DEV * s.shape[0], s.shape[1]), s.dtype) for s in arrays]
        else:
            self.out_shape = [jax.ShapeDtypeStruct((N_DEV, g.shape[0] // N_DEV, g.shape[1]), g.dtype) for g in arrays]
        n = self.n
        self.scratch = [pltpu.SemaphoreType.DMA((n, 7)), pltpu.SemaphoreType.DMA((n, 7)), pltpu.SemaphoreType.DMA((n,))]

    def phase(self, ph, in_refs, out_refs, sems):
        (self._gather if self.kind == "gather" else self._exchange)(ph, in_refs, out_refs, sems)

    def _gather(self, ph, x_refs, o_refs, sems):
        n = self.n
        send_sems, recv_sems, local_sems = sems
        x, y, c = _my_place()
        me, sibling = (x, y, c), (x, y, 1 - c)
        chips = [(1 - x, y), (x, 1 - y), (1 - x, 1 - y)]

        def rows(a, place):
            m = x_refs[a].shape[0]
            px, py, pc = place
            return o_refs[a].at[pl.ds((4 * px + 2 * py + pc) * m, m), :]

        def copy(a, k, block, to, own=False):
            return pltpu.make_async_remote_copy(
                src_ref=x_refs[a] if own else rows(a, block), dst_ref=rows(a, block),
                send_sem=send_sems.at[a, k], recv_sem=recv_sems.at[a, k], device_id=to, device_id_type=MESH_ID)

        def mine():
            return [pltpu.make_async_copy(x_refs[a], rows(a, me), local_sems.at[a]) for a in range(n)]

        def first():
            out = []
            for a in range(n):
                out.append(copy(a, 0, me, sibling, own=True))
                out += [copy(a, 1 + j, me, (*chip, c), own=True) for j, chip in enumerate(chips)]
            return out

        def passed():
            return [copy(a, 4 + j, (*chip, c), sibling) for j, chip in enumerate(chips) for a in range(n)]

        if ph == 0:
            for cp in mine() + first():
                cp.start()
        elif ph == 1:
            fw = passed()
            for j, chip in enumerate(chips):
                for a in range(n):
                    copy(a, 1 + j, (*chip, c), me).wait_recv()
                    fw[j * n + a].start()
        else:
            for a in range(n):
                copy(a, 0, sibling, me).wait_recv()
                for j, chip in enumerate(chips):
                    copy(a, 4 + j, (*chip, 1 - c), me).wait_recv()
            for cp in first() + passed():
                cp.wait_send()
            for cp in mine():
                cp.wait()

    def _exchange(self, ph, g_refs, o_refs, sems):
        n = self.n
        send_sems, recv_sems, local_sems = sems
        x, y, c = _my_place()
        me = 4 * x + 2 * y + c
        peers = [(x ^ ((k >> 2) & 1), y ^ ((k >> 1) & 1), c ^ (k & 1)) for k in range(1, N_DEV)]

        def block(a, idx):
            m = g_refs[a].shape[0] // N_DEV
            return g_refs[a].at[pl.ds(idx * m, m), :]

        def copy(a, k, slot):
            px, py, pc = peers[k]
            return pltpu.make_async_remote_copy(
                src_ref=block(a, 4 * px + 2 * py + pc), dst_ref=o_refs[a].at[slot],
                send_sem=send_sems.at[a, k], recv_sem=recv_sems.at[a, k], device_id=peers[k], device_id_type=MESH_ID)

        if ph == 1:
            return
        mine = [pltpu.make_async_copy(block(a, me), o_refs[a].at[me], local_sems.at[a]) for a in range(n)]
        sends = [copy(a, k, me) for k in range(N_DEV - 1) for a in range(n)]
        if ph == 0:
            for cp in mine + sends:
                cp.start()
        else:
            for k in range(N_DEV - 1):
                px, py, pc = peers[k]
                for a in range(n):
                    copy(a, k, 4 * px + 2 * py + pc).wait_recv()
            for cp in sends:
                cp.wait_send()
            for cp in mine:
                cp.wait()


def _comm_call(comm, name):
    n = comm.n

    def body(*refs):
        for ph in range(3):
            comm.phase(ph, refs[:n], refs[n:2 * n], refs[2 * n:])

    return pl.pallas_call(
        body, out_shape=comm.out_shape, in_specs=[ANY] * n, out_specs=[ANY] * n, scratch_shapes=comm.scratch, name=name,
    )(*comm.arrays)


def _host_comm(comm, when, step, total, cin, cout, csem):
    if comm is None:
        return
    sched = {0: 0, 1: (3 * total) // 4, 2: total - 1}
    phases = (0, 1) if when == "early" else (2,)
    for ph in phases:
        if ph == 1 and comm.kind == "exchange":
            continue

        @pl.when(step == sched[ph])
        def _(ph=ph):
            comm.phase(ph, cin, cout, csem)


def _all_gather(shards, name="all_gather"):
    return _comm_call(_Comm("gather", shards), name)


def _row_tile(r, target=256):
    best = None
    for t in range(8, min(r, target) + 1, 8):
        if r % t == 0:
            best = t
    return best if best is not None else r


def _sum8(parts, name="sum8"):
    _, M, N = parts.shape
    tr = _row_tile(M, 128)

    def body(p_ref, o_ref):
        acc = p_ref[0].astype(F32)
        for d in range(1, N_DEV):
            acc = acc + p_ref[d].astype(F32)
        o_ref[...] = acc

    return pl.pallas_call(
        body,
        grid=(M // tr,),
        in_specs=[pl.BlockSpec((N_DEV, tr, N), lambda i: (0, i, 0))],
        out_specs=pl.BlockSpec((tr, N), lambda i: (i, 0)),
        out_shape=jax.ShapeDtypeStruct((M, N), F32),
        compiler_params=_cp("parallel"),
        name=name,
    )(parts)


def _adamw(w, g, m, v, name="adamw"):
    shape = w.shape
    if w.ndim == 1:
        r2 = (1, shape[0])
    else:
        r2 = (int(np.prod(shape[:-1])), shape[-1])
    R, C = r2
    tr = _row_tile(R)
    bc1 = 1.0 - ADAM_B1 ** ADAM_STEP
    bc2 = 1.0 - ADAM_B2 ** ADAM_STEP

    def body(w_ref, g_ref, m_ref, v_ref, d_ref, nm_ref, nv_ref):
        gv = g_ref[...]
        nm = ADAM_B1 * m_ref[...] + (1.0 - ADAM_B1) * gv
        nv = ADAM_B2 * v_ref[...] + (1.0 - ADAM_B2) * (gv * gv)
        d_ref[...] = -ADAM_LR * ((nm / bc1) / (jnp.sqrt(nv / bc2) + ADAM_EPS) + ADAM_WD * w_ref[...])
        nm_ref[...] = nm
        nv_ref[...] = nv

    spec = pl.BlockSpec((tr, C), lambda i: (i, 0))
    out = pl.pallas_call(
        body,
        grid=(R // tr,),
        in_specs=[spec] * 4,
        out_specs=[spec] * 3,
        out_shape=[jax.ShapeDtypeStruct(r2, F32)] * 3,
        compiler_params=_cp("parallel"),
        name=name,
    )(w.reshape(r2), g.reshape(r2), m.reshape(r2), v.reshape(r2))
    return [o.reshape(shape) for o in out]


WEIGHTS = ['attn_norm_g', 'ffn_norm_g', 'ab_w_in', 'ab_w_out', 'ab_rel_bias', 'c_w_in', 'c_ln_g', 'c_ln_b', 'c_w_s', 'c_b_s',
           'c_w_out', 'ffn_w_up', 'ffn_conv_w', 'ffn_conv_b', 'ffn_w_down', 'final_norm_g']
SMALL_ORDER = ["attn_norm_g", "ffn_norm_g", "rel_bias", "ln_g", "ln_b", "w_s", "b_s", "conv_w", "conv_b", "final_g"]
PACK_ROW = 1024


def _pack(arrs):
    flat = jnp.concatenate([a.reshape(-1) for a in arrs])
    n = flat.shape[0]
    padded = -(-n // PACK_ROW) * PACK_ROW
    return jnp.pad(flat, (0, padded - n)).reshape(padded // 128, 128)


def _unpack(flat, shapes):
    out, off = [], 0
    for s in shapes:
        n = int(np.prod(s))
        out.append(flat[off:off + n].reshape(s))
        off += n
    return out


def _step(P):
    x, target = P["x"][0], P["loss_target"][0]
    me = 4 * lax.axis_index("x") + 2 * lax.axis_index("y") + lax.axis_index("c")
    n_up = P["ffn_w_up"].shape[0]
    Fc = P["ffn_conv_w"].shape[-1]
    Lc = P["c_ln_g"].shape[-1]

    first = _Comm("gather", [P["ab_w_in"][0].T.astype(BF16), _pack([P["ffn_conv_w"], P["c_ln_g"], P["c_ln_b"]])])
    hn0, full = _rms_fwd(x, P["attn_norm_g"][0], comm=first, name="rms_fwd_gather")
    wt = {"ab_in_t": full[0]}
    rest = {"ab_out": P["ab_w_out"][0], "c_in_t": P["c_w_in"][0].T, "c_out": P["c_w_out"][0],
            "up_t0": P["ffn_w_up"][0].T, "up_t1": P["ffn_w_up"][1].T, "down0": P["ffn_w_down"][0], "down1": P["ffn_w_down"][1]}
    rest_shards = [rest[k].astype(BF16) for k in REST]
    sm = full[-1].reshape(N_DEV, -1)
    conv_w = sm[:, :n_up * 3 * Fc].reshape(N_DEV, n_up, 3, Fc).transpose(1, 2, 0, 3).reshape(n_up, 3, N_DEV * Fc)
    off = n_up * 3 * Fc
    ln_g = sm[:, off:off + Lc].reshape(N_DEV * Lc)
    ln_b = sm[:, off + Lc:off + 2 * Lc].reshape(N_DEV * Lc)
    small = {"attn_norm_g": P["attn_norm_g"], "ffn_norm_g": P["ffn_norm_g"], "rel_bias": P["ab_rel_bias"][0],
             "ln_g": ln_g, "ln_b": ln_b, "w_s": P["c_w_s"][0], "b_s": P["c_b_s"][0], "conv_w": conv_w,
             "conv_b": P["ffn_conv_b"], "final_g": P["final_norm_g"]}

    loss_part, grad_x, gw, gs, recv = _local_step(x, target, wt, small, rest_shards=rest_shards, overlap=True, hn0=hn0)
    loss = lax.psum(loss_part, ("x", "y", "c"))

    s8 ={k: _sum8(recv[k], name="sum8") for k in ["ab_in_t"] + REST}
    g_big = {"ab_w_in": s8["ab_in_t"].T[None], "ab_w_out": s8["ab_out"][None], "c_w_in": s8["c_in_t"].T[None],
             "c_w_out": s8["c_out"][None], "ffn_w_up": jnp.stack([s8["up_t0"].T, s8["up_t1"].T]),
             "ffn_w_down": jnp.stack([s8["down0"], s8["down1"]])}

    packed = _pack([gs[k] for k in SMALL_ORDER])
    gathered = _all_gather([packed], name="gather_small_grads")[0]
    tot = _sum8(gathered.reshape(N_DEV, packed.shape[0], 128), name="sum8_small").reshape(-1)
    gsm = dict(zip(SMALL_ORDER, _unpack(tot, [gs[k].shape for k in SMALL_ORDER])))
    grads = dict(g_big)
    grads["attn_norm_g"] = gsm["attn_norm_g"]
    grads["ffn_norm_g"] = gsm["ffn_norm_g"]
    grads["ab_rel_bias"] = gsm["rel_bias"][None]
    grads["c_ln_g"] = lax.dynamic_slice(gsm["ln_g"], (me * Lc,), (Lc,))[None]
    grads["c_ln_b"] = lax.dynamic_slice(gsm["ln_b"], (me * Lc,), (Lc,))[None]
    grads["c_w_s"] = gsm["w_s"][None]
    grads["c_b_s"] = gsm["b_s"][None]
    grads["ffn_conv_w"] = lax.dynamic_slice(gsm["conv_w"], (0, 0, me * Fc), (n_up, 3, Fc))
    grads["ffn_conv_b"] = gsm["conv_b"]
    grads["final_norm_g"] = gsm["final_g"]

    delta, new_m, new_v = {}, {}, {}
    for k in WEIGHTS:
        delta[k], new_m[k], new_v[k] = _adamw(P[k], grads[k], P["m_" + k], P["v_" + k], name="adamw")
    return (loss, grad_x[None], *[grads[k] for k in WEIGHTS], *[delta[k] for k in WEIGHTS],
            *[new_m[k] for k in WEIGHTS], *[new_v[k] for k in WEIGHTS])


def kernel(x, attn_norm_g, ffn_norm_g, ab_w_in, ab_w_out, ab_rel_bias, c_w_in, c_ln_g, c_ln_b, c_w_s, c_b_s, c_w_out, ffn_w_up, ffn_conv_w, ffn_conv_b, ffn_w_down, final_norm_g, loss_target, m_attn_norm_g, m_ffn_norm_g, m_ab_w_in, m_ab_w_out, m_ab_rel_bias, m_c_w_in, m_c_ln_g, m_c_ln_b, m_c_w_s, m_c_b_s, m_c_w_out, m_ffn_w_up, m_ffn_conv_w, m_ffn_conv_b, m_ffn_w_down, m_final_norm_g, v_attn_norm_g, v_ffn_norm_g, v_ab_w_in, v_ab_w_out, v_ab_rel_bias, v_c_w_in, v_c_ln_g, v_c_ln_b, v_c_w_s, v_c_b_s, v_c_w_out, v_ffn_w_up, v_ffn_conv_w, v_ffn_conv_b, v_ffn_w_down, v_final_norm_g):
    return _step(dict(locals()))
```

```python
import functools

import numpy as np
import jax
import jax.numpy as jnp
from jax import lax
from jax.experimental import pallas as pl
from jax.experimental.pallas import tpu as pltpu

F32 = jnp.float32
BF16 = jnp.bfloat16

D_MODEL = 1024
CHUNK = 64
EPS = 1e-6
NEG_INF = -1e30
RET_HEADS = 4
RET_QK_DIM = 128
RET_V_DIM = 256
ATT_HEADS = 8
ATT_HEAD_DIM = 64
ATT_PAST = 8
ATT_BAND = (ATT_PAST + 1) * CHUNK
MAX_REL = 128
N_REL = 2 * MAX_REL + 1
N_REL_PAD = 384
SGU_BLOCK = 128
SGU_GROUPS = 8
SGU_WIDTH = 2048
SGU_GW = SGU_WIDTH // SGU_GROUPS
FFN_HIDDEN = 2816
RET_W = 2 * RET_HEADS * RET_QK_DIM + 2 * RET_HEADS * RET_V_DIM
ATT_W = 3 * ATT_HEADS * ATT_HEAD_DIM
N_DEV = 8

ADAM_LR = 0.001
ADAM_B1 = 0.9
ADAM_B2 = 0.999
ADAM_EPS = 1e-08
ADAM_WD = 0.01
ADAM_STEP = 10

VMEM_LIMIT = 52 * 1024 * 1024


def _cp(*sem):
    return pltpu.CompilerParams(dimension_semantics=sem if sem else None, vmem_limit_bytes=VMEM_LIMIT)


def _tile(n, target):
    if n <= target:
        return n
    best = None
    for t in range(128, target + 1, 128):
        if n % t == 0:
            best = t
    assert best is not None, (n, target)
    return best


def _gelu(x):
    c = 0.7978845608028654
    return 0.5 * x * (1.0 + jnp.tanh(c * (x + 0.044715 * x * x * x)))


def _gelu_and_grad(x):
    c = 0.7978845608028654
    x2 = x * x
    t = jnp.tanh(c * (x + 0.044715 * x * x2))
    cdf = 0.5 * (1.0 + t)
    grad = cdf + x * (0.5 * c) * (1.0 - t * t) * (1.0 + 3.0 * 0.044715 * x2)
    return x * cdf, grad


def _dot(a, b, dims):
    return lax.dot_general(a, b, (dims, ((), ())), preferred_element_type=F32)


NN = ((1,), (0,))
NT = ((1,), (1,))
TN = ((0,), (0,))


def _mm(a, b, mode, out_dtype, res=None, name="mm", tm_t=None, tn_t=None, tk_t=None, comm=None, rows=None):
    if mode == "nt":
        (M, K), N = a.shape, b.shape[0]
        dm, dn, dk = (512, 2816, K) if N <= 2816 else (1024, 1024, K)
    elif mode == "nn":
        (M, K), N = a.shape, b.shape[1]
        dm, dn, dk = (1024 if K <= 3072 else 512), 1024, K
    else:
        (K, M), N = a.shape, b.shape[1]
        dm, dn, dk = 1536, 1024, 2048
    tm, tn, tk = _tile(M, tm_t or dm), _tile(N, tn_t or dn), _tile(K, tk_t or dk)
    nk = K // tk
    dims = {"nt": NT, "nn": NN, "tn": TN}[mode]
    a_spec = pl.BlockSpec((tk, tm), lambda i, j, k: (k, i)) if mode == "tn" else pl.BlockSpec((tm, tk), lambda i, j, k: (i, k))
    b_spec = pl.BlockSpec((tn, tk), lambda i, j, k: (j, k)) if mode == "nt" else pl.BlockSpec((tk, tn), lambda i, j, k: (k, j))
    gi, gj = M // tm, N // tn
    out_rows, row0, into = rows if rows else (M, 0, None)
    assert row0 % tm == 0 and not (comm and into is not None)
    o_spec = pl.BlockSpec((tm, tn), lambda i, j, k: (i + row0 // tm, j))
    has_res = res is not None
    nc = 1 if into is not None else (comm.n if comm else 0)
    n_in = 3 if has_res else 2

    def body(*refs):
        a_ref, b_ref = refs[:2]
        r_ref = refs[2] if has_res else None
        nco = comm.n if comm else 0
        cin = refs[n_in:n_in + nc]
        o_ref = refs[n_in + nc]
        cout = refs[n_in + nc + 1:n_in + nc + 1 + nco]
        scratch = refs[n_in + nc + 1 + nco:]
        csem = scratch[1:] if nk > 1 else scratch
        step = (pl.program_id(0) * gj + pl.program_id(1)) * nk + pl.program_id(2)
        _host_comm(comm, "early", step, gi * gj * nk, cin, cout, csem)
        p = _dot(a_ref[...].astype(BF16), b_ref[...].astype(BF16), dims)
        if nk == 1:
            if has_res:
                p = p + r_ref[...]
            o_ref[...] = p.astype(out_dtype)
        else:
            acc = scratch[0]
            k = pl.program_id(2)

            @pl.when(k == 0)
            def _():
                acc[...] = p

            @pl.when(k > 0)
            def _():
                acc[...] += p

            @pl.when(k == nk - 1)
            def _():
                t = acc[...]
                if has_res:
                    t = t + r_ref[...]
                o_ref[...] = t.astype(out_dtype)
        _host_comm(comm, "late", step, gi * gj * nk, cin, cout, csem)

    in_specs = [a_spec, b_spec] + ([o_spec] if has_res else []) + [ANY] * nc
    args = (a, b) + ((res,) if has_res else ()) + ((into,) if into is not None else tuple(comm.arrays if comm else ()))
    out = pl.pallas_call(
        body,
        grid=(gi, gj, nk),
        in_specs=in_specs,
        out_specs=[o_spec] + ([ANY] * nc if comm else []),
        out_shape=[jax.ShapeDtypeStruct((out_rows, N), out_dtype)] + (comm.out_shape if comm else []),
        scratch_shapes=([pltpu.VMEM((tm, tn), F32)] if nk > 1 else []) + (comm.scratch if comm else []),
        input_output_aliases={n_in: 0} if into is not None else {},
        compiler_params=_cp("arbitrary", "arbitrary", "arbitrary") if comm else _cp("parallel", "parallel", "arbitrary"),
        name=name,
    )(*args)
    return (out[0], list(out[1:])) if comm else out[0]


def _mm_rows(a, b, res, name, norm_g=None, bwd=None, loss=None, second=None, tm=512):
    M, K = a.shape
    Dm = b.shape[1]
    tm = min(tm, M)
    row = pl.BlockSpec((tm, Dm), lambda i: (i, 0))
    vec = pl.BlockSpec((1, Dm), lambda i: (0, 0))
    a_spec = pl.BlockSpec((tm, K), lambda i: (i, 0))
    b_spec = pl.BlockSpec((K, Dm), lambda i: (0, 0))

    if loss is not None:
        target, g = loss

        def body(a_ref, b_ref, r_ref, g_ref, t_ref, loss_ref, dh_ref, dg_ref):
            @pl.when(pl.program_id(0) == 0)
            def _():
                loss_ref[...] = jnp.zeros_like(loss_ref)
                dg_ref[...] = jnp.zeros_like(dg_ref)

            x = _dot(a_ref[...].astype(BF16), b_ref[...].astype(BF16), NN) + r_ref[...]
            gv = g_ref[...]
            r = lax.rsqrt(jnp.mean(x * x, axis=-1, keepdims=True) + EPS)
            xhat = x * r
            e = xhat * gv - t_ref[...]
            loss_ref[...] += jnp.full((1, 128), 0.5 / Dm, F32) * jnp.sum(e * e)
            dy = e * (1.0 / Dm)
            dg_ref[...] += jnp.sum(dy * xhat, axis=0, keepdims=True)
            dx = dy * gv
            m = jnp.mean(dx * xhat, axis=-1, keepdims=True)
            dh_ref[...] = r * (dx - xhat * m)

        return pl.pallas_call(
            body, grid=(M // tm,), in_specs=[a_spec, b_spec, row, vec, row],
            out_specs=[pl.BlockSpec((1, 128), lambda i: (0, 0)), row, vec],
            out_shape=[jax.ShapeDtypeStruct((1, 128), F32), jax.ShapeDtypeStruct((M, Dm), F32), jax.ShapeDtypeStruct((1, Dm), F32)],
            compiler_params=_cp("arbitrary"), name=name,
        )(a, b, res, g.reshape(1, Dm), target)

    if bwd is None:
        def body(a_ref, b_ref, r_ref, g_ref, o_ref, n_ref):
            t = _dot(a_ref[...].astype(BF16), b_ref[...].astype(BF16), NN) + r_ref[...]
            o_ref[...] = t
            r = lax.rsqrt(jnp.mean(t * t, axis=-1, keepdims=True) + EPS)
            n_ref[...] = (t * r * g_ref[...]).astype(n_ref.dtype)

        return pl.pallas_call(
            body, grid=(M // tm,), in_specs=[a_spec, b_spec, row, vec], out_specs=[row, row],
            out_shape=[jax.ShapeDtypeStruct((M, Dm), F32), jax.ShapeDtypeStruct((M, Dm), BF16)],
            compiler_params=_cp("parallel"), name=name,
        )(a, b, res, norm_g.reshape(1, Dm))

    h, g, dres = bwd
    has_res = res is not None
    has2 = second is not None

    def body(*refs):
        a_ref, b_ref = refs[:2]
        h_ref, g_ref, dres_ref, dh_ref, dg_ref = refs[-5:]

        @pl.when(pl.program_id(0) == 0)
        def _():
            dg_ref[...] = jnp.zeros_like(dg_ref)

        d = _dot(a_ref[...].astype(BF16), b_ref[...].astype(BF16), NN)
        if has2:
            d = d + _dot(refs[2][...].astype(BF16), refs[3][...].astype(BF16), NN)
        if has_res:
            d = d + refs[4 if has2 else 2][...]
        x = h_ref[...]
        r = lax.rsqrt(jnp.mean(x * x, axis=-1, keepdims=True) + EPS)
        xhat = x * r
        dg_ref[...] += jnp.sum(d * xhat, axis=0, keepdims=True)
        dx = d * g_ref[...]
        m = jnp.mean(dx * xhat, axis=-1, keepdims=True)
        dh_ref[...] = dres_ref[...] + r * (dx - xhat * m)

    second_specs = [pl.BlockSpec((tm, second[0].shape[1]), lambda i: (i, 0)),
                    pl.BlockSpec(second[1].shape, lambda i: (0, 0))] if has2 else []
    return pl.pallas_call(
        body, grid=(M // tm,), in_specs=[a_spec, b_spec] + second_specs + ([row] if has_res else []) + [row, vec, row],
        out_specs=[row, vec],
        out_shape=[jax.ShapeDtypeStruct((M, Dm), F32), jax.ShapeDtypeStruct((1, Dm), F32)],
        compiler_params=_cp("arbitrary"), name=name,
    )(a, b, *(second if has2 else ()), *((res,) if has_res else ()), h, g.reshape(1, Dm), dres)


def _rms_fwd(h, g, comm=None, name="rms_fwd", tm=512):
    T, Dm = h.shape
    tm = min(tm, T)
    nc = comm.n if comm else 0
    ni = T // tm

    def body(*refs):
        h_ref, g_ref = refs[:2]
        cin, o_ref, cout, csem = refs[2:2 + nc], refs[2 + nc], refs[3 + nc:3 + 2 * nc], refs[3 + 2 * nc:]
        step = pl.program_id(0)
        _host_comm(comm, "early", step, ni, cin, cout, csem)
        x = h_ref[...]
        r = lax.rsqrt(jnp.mean(x * x, axis=-1, keepdims=True) + EPS)
        o_ref[...] = (x * r * g_ref[...]).astype(o_ref.dtype)
        _host_comm(comm, "late", step, ni, cin, cout, csem)

    out = pl.pallas_call(
        body,
        grid=(ni,),
        in_specs=[pl.BlockSpec((tm, Dm), lambda i: (i, 0)), pl.BlockSpec((1, Dm), lambda i: (0, 0))] + [ANY] * nc,
        out_specs=[pl.BlockSpec((tm, Dm), lambda i: (i, 0))] + [ANY] * nc,
        out_shape=[jax.ShapeDtypeStruct((T, Dm), BF16)] + (comm.out_shape if comm else []),
        scratch_shapes=comm.scratch if comm else [],
        compiler_params=_cp("arbitrary"),
        name=name,
    )(h, g.reshape(1, Dm), *(comm.arrays if comm else []))
    return (out[0], list(out[1:])) if comm else out[0]


HALO = 16


def _conv3(ext, w_ref, b_ref):
    return w_ref[0:1, :] * pltpu.roll(ext, 2, 0) + w_ref[1:2, :] * pltpu.roll(ext, 1, 0) + w_ref[2:3, :] * ext + b_ref[...]


def _ffn_up_mid(hf, w_up_t, cw, cb, name="ffn_up_mid", tm=512, tc=1408):
    T, Dm = hf.shape
    Fh = w_up_t.shape[0] // 2
    tm = min(tm, T)
    nj, ni = Fh // tc, T // tm

    rc = min(512, tm)

    def body(h_ref, wg_ref, wu_ref, cwg_ref, cwu_ref, cbg_ref, cbu_ref, zg_ref, zu_ref, a_ref, eg_ref, eu_ref, ng_ref, nu_ref):
        @pl.when(pl.program_id(1) == 0)
        def _():
            eg_ref[...] = jnp.zeros_like(eg_ref)
            eu_ref[...] = jnp.zeros_like(eu_ref)

        for r in range(tm // rc):
            ext = slice(r * rc, r * rc + HALO + rc)
            cg = _conv3(eg_ref[ext, :].astype(F32), cwg_ref, cbg_ref)[HALO:]
            cu = _conv3(eu_ref[ext, :].astype(F32), cwu_ref, cbu_ref)[HALO:]
            a_ref[r * rc:(r + 1) * rc, :] = _gelu(cg.astype(BF16)) * cu.astype(BF16)
            h = h_ref[r * rc:(r + 1) * rc, :]
            for w_ref, z_ref, n_ref in ((wg_ref, zg_ref, ng_ref), (wu_ref, zu_ref, nu_ref)):
                z = _dot(h, w_ref[...], NT).astype(BF16)
                z_ref[r * rc:(r + 1) * rc, :] = z
                n_ref[r * rc:(r + 1) * rc, :] = z
        for e_ref, n_ref in ((eg_ref, ng_ref), (eu_ref, nu_ref)):
            e_ref[0:HALO, :] = e_ref[tm:tm + HALO, :]
            e_ref[HALO:HALO + tm, :] = n_ref[...]

    last = ni - 1
    z_spec = pl.BlockSpec((tm, tc), lambda j, i: (jnp.minimum(i, last), j))
    return pl.pallas_call(
        body,
        grid=(nj, ni + 1),
        in_specs=[pl.BlockSpec((tm, Dm), lambda j, i: (jnp.minimum(i, last), 0)),
                  pl.BlockSpec((tc, Dm), lambda j, i: (j, 0)), pl.BlockSpec((tc, Dm), lambda j, i: (nj + j, 0)),
                  pl.BlockSpec((3, tc), lambda j, i: (0, j)), pl.BlockSpec((3, tc), lambda j, i: (0, nj + j)),
                  pl.BlockSpec((1, tc), lambda j, i: (0, j)), pl.BlockSpec((1, tc), lambda j, i: (0, nj + j))],
        out_specs=[z_spec, z_spec, pl.BlockSpec((tm, tc), lambda j, i: (jnp.maximum(i - 1, 0), j))],
        out_shape=[jax.ShapeDtypeStruct((T, Fh), BF16)] * 3,
        scratch_shapes=[pltpu.VMEM((HALO + tm, tc), BF16), pltpu.VMEM((HALO + tm, tc), BF16),
                        pltpu.VMEM((tm, tc), BF16), pltpu.VMEM((tm, tc), BF16)],
        compiler_params=_cp("arbitrary", "arbitrary"),
        name=name,
    )(hf, w_up_t, w_up_t, cw, cw, cb, cb)


def _ffn_mid_bwd(zg, zu, da, cw, cb, comm=None, name="ffn_mid_bwd", tm=1024, tc=256):
    T, Fh = zg.shape
    tm = min(tm, T)
    nj = Fh // tc
    hb = tm // HALO
    nhb = T // HALO

    nc = comm.n if comm else 0
    ni = T // tm

    def body(*refs):
        zg_ref, zu_ref, zgp_ref, zup_ref, zgn_ref, zun_ref, da_ref, dan_ref, wg_ref, wu_ref, bg_ref, bu_ref = refs[:12]
        cin = refs[12:12 + nc]
        dzg_ref, dzu_ref, dwg_ref, dwu_ref, dbg_ref, dbu_ref = refs[12 + nc:18 + nc]
        cout = refs[18 + nc:18 + 2 * nc]
        csem = refs[18 + 2 * nc:]
        i = pl.program_id(1)
        step = pl.program_id(0) * ni + i
        _host_comm(comm, "early", step, nj * ni, cin, cout, csem)
        first = i == 0
        last = i == ni - 1

        @pl.when(first)
        def _():
            dwg_ref[...] = jnp.zeros_like(dwg_ref)
            dwu_ref[...] = jnp.zeros_like(dwu_ref)
            dbg_ref[...] = jnp.zeros_like(dbg_ref)
            dbu_ref[...] = jnp.zeros_like(dbu_ref)

        def ext_of(p_ref, c_ref, n_ref):
            p = jnp.where(first, 0.0, p_ref[...].astype(F32))
            return jnp.concatenate([p, c_ref[...].astype(F32), n_ref[...].astype(F32)], axis=0)

        zge = ext_of(zgp_ref, zg_ref, zgn_ref)
        zue = ext_of(zup_ref, zu_ref, zun_ref)
        dan = jnp.where(last, 0.0, dan_ref[...].astype(F32))
        dae = jnp.concatenate([jnp.zeros((HALO, tc), F32), da_ref[...].astype(F32), dan], axis=0)
        zg1, zg2 = pltpu.roll(zge, 1, 0), pltpu.roll(zge, 2, 0)
        zu1, zu2 = pltpu.roll(zue, 1, 0), pltpu.roll(zue, 2, 0)
        cg = wg_ref[0:1, :] * zg2 + wg_ref[1:2, :] * zg1 + wg_ref[2:3, :] * zge + bg_ref[...]
        cu = wu_ref[0:1, :] * zu2 + wu_ref[1:2, :] * zu1 + wu_ref[2:3, :] * zue + bu_ref[...]
        gel, dgel = (t.astype(F32) for t in _gelu_and_grad(cg.astype(BF16)))
        dcg = dae * cu * dgel
        dcu = dae * gel
        lo, hi = HALO, HALO + tm

        def back(dc, taps, w_ref, dz_ref, dw_ref, db_ref):
            n = dc.shape[0]
            dz = w_ref[2:3, :] * dc + w_ref[1:2, :] * pltpu.roll(dc, n - 1, 0) + w_ref[0:1, :] * pltpu.roll(dc, n - 2, 0)
            dz_ref[...] = dz[lo:hi].astype(dz_ref.dtype)
            dcc = dc[lo:hi]
            db_ref[...] += jnp.sum(dcc, axis=0, keepdims=True)
            for k, tap in enumerate(taps):
                dw_ref[k:k + 1, :] += jnp.sum(dcc * tap[lo:hi], axis=0, keepdims=True)

        back(dcg, (zg2, zg1, zge), wg_ref, dzg_ref, dwg_ref, dbg_ref)
        back(dcu, (zu2, zu1, zue), wu_ref, dzu_ref, dwu_ref, dbu_ref)
        _host_comm(comm, "late", step, nj * ni, cin, cout, csem)

    cur = pl.BlockSpec((tm, tc), lambda j, i: (i, j))
    prev = pl.BlockSpec((HALO, tc), lambda j, i: (jnp.maximum(i * hb - 1, 0), j))
    nxt = pl.BlockSpec((HALO, tc), lambda j, i: (jnp.minimum((i + 1) * hb, nhb - 1), j))
    wg = pl.BlockSpec((3, tc), lambda j, i: (0, j))
    wu = pl.BlockSpec((3, tc), lambda j, i: (0, j + nj))
    bg = pl.BlockSpec((1, tc), lambda j, i: (0, j))
    bu = pl.BlockSpec((1, tc), lambda j, i: (0, j + nj))
    dw = pl.BlockSpec((3, tc), lambda j, i: (0, j))
    db = pl.BlockSpec((1, tc), lambda j, i: (0, j))
    out = pl.pallas_call(
        body,
        grid=(nj, ni),
        in_specs=[cur, cur, prev, prev, nxt, nxt, cur, nxt, wg, wu, bg, bu] + [ANY] * nc,
        out_specs=[cur, cur, dw, dw, db, db] + [ANY] * nc,
        out_shape=[jax.ShapeDtypeStruct((T, Fh), BF16), jax.ShapeDtypeStruct((T, Fh), BF16),
                   jax.ShapeDtypeStruct((3, Fh), F32), jax.ShapeDtypeStruct((3, Fh), F32),
                   jax.ShapeDtypeStruct((1, Fh), F32), jax.ShapeDtypeStruct((1, Fh), F32)] + (comm.out_shape if comm else []),
        scratch_shapes=comm.scratch if comm else [],
        compiler_params=_cp("arbitrary", "arbitrary"),
        name=name,
    )(zg, zu, zg, zu, zg, zu, da, da, cw, cw, cb, cb, *(comm.arrays if comm else []))
    dzg, dzu, dwg, dwu, dbg, dbu = out[:6]
    return dzg, dzu, jnp.concatenate([dwg, dwu], axis=1), jnp.concatenate([dbg, dbu], axis=1), list(out[6:])


def _sgu_mask():
    r = lax.broadcasted_iota(jnp.int32, (SGU_BLOCK, SGU_BLOCK), 0)
    c = lax.broadcasted_iota(jnp.int32, (SGU_BLOCK, SGU_BLOCK), 1)
    return (c < CHUNK) | (r >= CHUNK)


def _sgu_fwd(zz, ln_g, ln_b, w_s, b_s, name="sgu_fwd", tm=256):
    T = zz.shape[0]
    tm = min(tm, T)
    W = SGU_WIDTH

    def body(zu_ref, zv_ref, g_ref, b_ref, ws_ref, bs_ref, y_ref):
        u = _gelu(zu_ref[...]).astype(F32)
        v = _gelu(zv_ref[...]).astype(F32)
        mu = jnp.mean(v, axis=-1, keepdims=True)
        xc = v - mu
        rstd = lax.rsqrt(jnp.mean(xc * xc, axis=-1, keepdims=True) + EPS)
        vn = (xc * rstd * g_ref[...] + b_ref[...]).astype(BF16)
        mask = _sgu_mask()
        for g in range(SGU_GROUPS):
            wm = jnp.where(mask, ws_ref[g], 0.0).astype(BF16)
            cs = slice(g * SGU_GW, (g + 1) * SGU_GW)
            for blk in range(tm // SGU_BLOCK):
                rs = slice(blk * SGU_BLOCK, (blk + 1) * SGU_BLOCK)
                mixed = _dot(wm, vn[rs, cs], NN) + bs_ref[g]
                y_ref[rs, cs] = (u[rs, cs] * mixed).astype(y_ref.dtype)

    return pl.pallas_call(
        body,
        grid=(T // tm,),
        in_specs=[pl.BlockSpec((tm, W), lambda i: (i, 0)), pl.BlockSpec((tm, W), lambda i: (i, 1)),
                  pl.BlockSpec((1, W), lambda i: (0, 0)), pl.BlockSpec((1, W), lambda i: (0, 0)),
                  pl.BlockSpec((SGU_GROUPS, SGU_BLOCK, SGU_BLOCK), lambda i: (0, 0, 0)),
                  pl.BlockSpec((SGU_GROUPS, SGU_BLOCK, 1), lambda i: (0, 0, 0))],
        out_specs=pl.BlockSpec((tm, W), lambda i: (i, 0)),
        out_shape=jax.ShapeDtypeStruct((T, W), BF16),
        compiler_params=_cp("parallel"),
        name=name,
    )(zz, zz, ln_g.reshape(1, W), ln_b.reshape(1, W), w_s, b_s.reshape(SGU_GROUPS, SGU_BLOCK, 1))


def _sgu_bwd(zz, dy, ln_g, ln_b, w_s, b_s, name="sgu_bwd", tm=256):
    T = zz.shape[0]
    tm = min(tm, T)
    W = SGU_WIDTH

    def body(zu_ref, zv_ref, dy_ref, g_ref, b_ref, ws_ref, bs_ref, dzz_ref, dws_ref, dbs_ref, dg_ref, db_ref, dvn_ref):
        i = pl.program_id(0)

        @pl.when(i == 0)
        def _():
            dws_ref[...] = jnp.zeros_like(dws_ref)
            dbs_ref[...] = jnp.zeros_like(dbs_ref)
            dg_ref[...] = jnp.zeros_like(dg_ref)
            db_ref[...] = jnp.zeros_like(db_ref)

        u, du_dz = (t.astype(F32) for t in _gelu_and_grad(zu_ref[...]))
        v, dv_dz = (t.astype(F32) for t in _gelu_and_grad(zv_ref[...]))
        mu = jnp.mean(v, axis=-1, keepdims=True)
        xc = v - mu
        rstd = lax.rsqrt(jnp.mean(xc * xc, axis=-1, keepdims=True) + EPS)
        xhat = xc * rstd
        gv = g_ref[...]
        vn = (xhat * gv + b_ref[...]).astype(BF16)
        dyv = dy_ref[...].astype(F32)
        mask = _sgu_mask()
        for g in range(SGU_GROUPS):
            wm = jnp.where(mask, ws_ref[g], 0.0).astype(BF16)
            cs = slice(g * SGU_GW, (g + 1) * SGU_GW)
            dw_acc = jnp.zeros((SGU_BLOCK, SGU_BLOCK), F32)
            db_acc = jnp.zeros((SGU_BLOCK, 1), F32)
            for blk in range(tm // SGU_BLOCK):
                rs = slice(blk * SGU_BLOCK, (blk + 1) * SGU_BLOCK)
                vn_bg = vn[rs, cs]
                mixed = _dot(wm, vn_bg, NN) + bs_ref[g]
                dy_bg = dyv[rs, cs]
                dmixed = dy_bg * u[rs, cs]
                dmb = dmixed.astype(BF16)
                dw_acc += _dot(dmb, vn_bg, NT)
                db_acc += jnp.sum(dmixed, axis=1, keepdims=True)
                dvn_ref[rs, cs] = _dot(wm, dmb, TN)
                dzz_ref[rs, cs] = (dy_bg * mixed * du_dz[rs, cs]).astype(dzz_ref.dtype)
            dws_ref[g] += jnp.where(mask, dw_acc, 0.0)
            dbs_ref[g] += db_acc
        dvn = dvn_ref[...]
        dg_ref[...] += jnp.sum(dvn * xhat, axis=0, keepdims=True)
        db_ref[...] += jnp.sum(dvn, axis=0, keepdims=True)
        dxh = dvn * gv
        m1 = jnp.mean(dxh, axis=-1, keepdims=True)
        m2 = jnp.mean(dxh * xhat, axis=-1, keepdims=True)
        dv = rstd * (dxh - m1 - xhat * m2)
        dzz_ref[:, W:] = (dv * dv_dz).astype(dzz_ref.dtype)

    vec = pl.BlockSpec((1, W), lambda i: (0, 0))
    ws_spec = pl.BlockSpec((SGU_GROUPS, SGU_BLOCK, SGU_BLOCK), lambda i: (0, 0, 0))
    bs_spec = pl.BlockSpec((SGU_GROUPS, SGU_BLOCK, 1), lambda i: (0, 0, 0))
    return pl.pallas_call(
        body,
        grid=(T // tm,),
        in_specs=[pl.BlockSpec((tm, W), lambda i: (i, 0)), pl.BlockSpec((tm, W), lambda i: (i, 1)),
                  pl.BlockSpec((tm, W), lambda i: (i, 0)), vec, vec, ws_spec, bs_spec],
        out_specs=[pl.BlockSpec((tm, 2 * W), lambda i: (i, 0)), ws_spec, bs_spec, vec, vec],
        out_shape=[jax.ShapeDtypeStruct((T, 2 * W), BF16),
                   jax.ShapeDtypeStruct((SGU_GROUPS, SGU_BLOCK, SGU_BLOCK), F32),
                   jax.ShapeDtypeStruct((SGU_GROUPS, SGU_BLOCK, 1), F32),
                   jax.ShapeDtypeStruct((1, W), F32), jax.ShapeDtypeStruct((1, W), F32)],
        scratch_shapes=[pltpu.VMEM((tm, W), F32)],
        compiler_params=_cp("arbitrary"),
        name=name,
    )(zz, zz, dy, ln_g.reshape(1, W), ln_b.reshape(1, W), w_s, b_s.reshape(SGU_GROUPS, SGU_BLOCK, 1))


RET_TR = 256
RET_BLK = 256
QK_SCALE = RET_QK_DIM ** -0.5


def _ret_tables(T):
    half = RET_QK_DIM // 2
    inv = 1.0 / (10000.0 ** jnp.linspace(0.0, 1.0, half, dtype=F32))
    inv2 = jnp.concatenate([inv, inv])[None, :]
    sgn = jnp.concatenate([-jnp.ones((half,), F32), jnp.ones((half,), F32)])[None, :]
    tr = min(RET_TR, T)

    def trig(pos):
        ang = pos.astype(F32)[:, None] * inv2
        return jnp.stack([jnp.cos(ang), jnp.sin(ang), sgn * jnp.sin(ang)])

    tile_tab = jnp.pad(trig(jnp.arange(T // tr) * tr).transpose(1, 0, 2), ((0, 0), (0, 5), (0, 0)))
    row_tab = trig(jnp.arange(tr))
    log_g = jnp.log1p(-jnp.exp2(-5.0 - jnp.arange(RET_HEADS, dtype=F32)))
    idx = jnp.arange(RET_BLK, dtype=F32)
    dist = idx[:, None] - idx[None, :]
    cq, ck = jnp.arange(RET_BLK)[:, None] // CHUNK, jnp.arange(RET_BLK)[None, :] // CHUNK
    expo = jnp.where(ck == cq, jnp.abs(dist), dist)
    d_blk = jnp.where((ck <= cq)[None], jnp.exp(log_g[:, None, None] * expo[None]), 0.0)
    k_dec = jnp.exp(log_g[:, None] * (RET_BLK - 1 - idx)[None, :])[:, :, None]
    q_dec = jnp.exp(log_g[:, None] * (idx + 1.0)[None, :])[:, :, None]
    c_dec = jnp.exp(log_g * RET_BLK)[:, None, None]
    return tile_tab, row_tab, d_blk, q_dec, k_dec, c_dec


def _rot(x, c, s):
    return x * c + pltpu.roll(x, RET_QK_DIM // 2, 1) * s


def _rot_tables(tt_ref, rt_ref):
    ca, sa, ga = tt_ref[0:1, :], tt_ref[1:2, :], tt_ref[2:3, :]
    cb, sb, gb = rt_ref[0], rt_ref[1], rt_ref[2]
    return ca * cb - sa * sb, ga * cb + ca * gb


def _ret_specs(tr, rev, nb):
    ix = (lambda n: nb - 1 - n) if rev else (lambda n: n)
    tt = pl.BlockSpec((None, 8, RET_QK_DIM), lambda n: (ix(n), 0, 0))
    rt = pl.BlockSpec((3, tr, RET_QK_DIM), lambda n: (0, 0, 0))
    dm = pl.BlockSpec((RET_HEADS, RET_BLK, RET_BLK), lambda n: (0, 0, 0))
    dv = pl.BlockSpec((RET_HEADS, RET_BLK, 1), lambda n: (0, 0, 0))
    dc = pl.BlockSpec((RET_HEADS, 1, 1), lambda n: (0, 0, 0))
    return ix, [tt, rt, dm, dv, dv, dc]


def _ret_fwd(z_a, tables, name="ret_fwd"):
    T = z_a.shape[0]
    tr = min(RET_TR, T)
    cpb = tr // RET_BLK
    nb = T // tr
    QW, VW = RET_HEADS * RET_QK_DIM, RET_HEADS * RET_V_DIM
    ix, tab_specs = _ret_specs(tr, False, nb)

    def body(z_ref, tt_ref, rt_ref, dm_ref, qd_ref, kd_ref, cd_ref, y_ref, st_ref, state):
        @pl.when(pl.program_id(0) == 0)
        def _():
            state[...] = jnp.zeros_like(state)

        rot_c, rot_s = _rot_tables(tt_ref, rt_ref)
        for c in range(cpb):
            for h in range(RET_HEADS):
                rs = slice(c * RET_BLK, (c + 1) * RET_BLK)
                cc, ss = rot_c[rs, :], rot_s[rs, :]
                q = z_ref[rs, h * RET_QK_DIM:(h + 1) * RET_QK_DIM].astype(F32)
                k = z_ref[rs, QW + h * RET_QK_DIM:QW + (h + 1) * RET_QK_DIM].astype(F32)
                v = z_ref[rs, 2 * QW + h * RET_V_DIM:2 * QW + (h + 1) * RET_V_DIM]
                gt = z_ref[rs, 2 * QW + VW + h * RET_V_DIM:2 * QW + VW + (h + 1) * RET_V_DIM].astype(F32)
                qr = _rot(q, cc, ss)
                kr = _rot(k, cc, ss) * QK_SCALE
                s_old = state[h]
                sb = s_old.astype(BF16)
                st_ref[c, h] = sb
                s = _dot(qr.astype(BF16), kr.astype(BF16), NT) * dm_ref[h]
                o = _dot(s.astype(BF16), v, NN) + _dot((qr * qd_ref[h]).astype(BF16), sb, NN)
                state[h] = s_old * cd_ref[h] + _dot((kr * kd_ref[h]).astype(BF16), v, TN)
                mu = jnp.mean(o, axis=-1, keepdims=True)
                oc = o - mu
                rn = oc * lax.rsqrt(jnp.mean(oc * oc, axis=-1, keepdims=True) + EPS)
                silu = gt / (1.0 + jnp.exp(-gt))
                y_ref[rs, h * RET_V_DIM:(h + 1) * RET_V_DIM] = (silu * rn).astype(y_ref.dtype)

    return pl.pallas_call(
        body,
        grid=(nb,),
        in_specs=[pl.BlockSpec((tr, RET_W), lambda n: (n, 0))] + tab_specs,
        out_specs=[pl.BlockSpec((tr, VW), lambda n: (n, 0)),
                   pl.BlockSpec((cpb, RET_HEADS, RET_QK_DIM, RET_V_DIM), lambda n: (n, 0, 0, 0))],
        out_shape=[jax.ShapeDtypeStruct((T, Y_COLS), BF16),
                   jax.ShapeDtypeStruct((T // RET_BLK, RET_HEADS, RET_QK_DIM, RET_V_DIM), BF16)],
        scratch_shapes=[pltpu.VMEM((RET_HEADS, RET_QK_DIM, RET_V_DIM), F32)],
        compiler_params=_cp("arbitrary"),
        name=name,
    )(z_a, *tables)


def _ret_bwd(z_a, dy, states, tables, name="ret_bwd"):
    T = z_a.shape[0]
    tr = min(RET_TR, T)
    cpb = tr // RET_BLK
    nb = T // tr
    QW, VW = RET_HEADS * RET_QK_DIM, RET_HEADS * RET_V_DIM
    ix, tab_specs = _ret_specs(tr, True, nb)

    def body(z_ref, dy_ref, st_ref, tt_ref, rt_ref, dm_ref, qd_ref, kd_ref, cd_ref, dz_ref, dstate):
        @pl.when(pl.program_id(0) == 0)
        def _():
            dstate[...] = jnp.zeros_like(dstate)

        rot_c, rot_s = _rot_tables(tt_ref, rt_ref)
        for c in reversed(range(cpb)):
            for h in range(RET_HEADS):
                rs = slice(c * RET_BLK, (c + 1) * RET_BLK)
                cc, ss = rot_c[rs, :], rot_s[rs, :]
                q = z_ref[rs, h * RET_QK_DIM:(h + 1) * RET_QK_DIM].astype(F32)
                k = z_ref[rs, QW + h * RET_QK_DIM:QW + (h + 1) * RET_QK_DIM].astype(F32)
                v = z_ref[rs, 2 * QW + h * RET_V_DIM:2 * QW + (h + 1) * RET_V_DIM]
                gt = z_ref[rs, 2 * QW + VW + h * RET_V_DIM:2 * QW + VW + (h + 1) * RET_V_DIM].astype(F32)
                dyv = dy_ref[rs, h * RET_V_DIM:(h + 1) * RET_V_DIM].astype(F32)
                dmat, qd, kd = dm_ref[h], qd_ref[h], kd_ref[h]
                qr = _rot(q, cc, ss)
                kr = _rot(k, cc, ss) * QK_SCALE
                qrb, krb = qr.astype(BF16), kr.astype(BF16)
                sb = st_ref[c, h]
                sd = (_dot(qrb, krb, NT) * dmat).astype(BF16)
                qdb = (qr * qd).astype(BF16)
                kdb = (kr * kd).astype(BF16)
                o = _dot(sd, v, NN) + _dot(qdb, sb, NN)
                mu = jnp.mean(o, axis=-1, keepdims=True)
                oc = o - mu
                rstd = lax.rsqrt(jnp.mean(oc * oc, axis=-1, keepdims=True) + EPS)
                rn = oc * rstd
                sg = 1.0 / (1.0 + jnp.exp(-gt))
                dgt = dyv * rn * (sg * (1.0 + gt * (1.0 - sg)))
                drn = dyv * (gt * sg)
                do = rstd * (drn - jnp.mean(drn, axis=-1, keepdims=True) - rn * jnp.mean(drn * rn, axis=-1, keepdims=True))
                dob = do.astype(BF16)
                dsn = dstate[h]
                dsnb = dsn.astype(BF16)
                ds_raw = (_dot(dob, v, NT) * dmat).astype(BF16)
                dv = _dot(sd, dob, TN) + _dot(kdb, dsnb, NN)
                dqr = _dot(ds_raw, krb, NN) + qd * _dot(dob, sb, NT)
                dkr = (_dot(ds_raw, qrb, TN) + kd * _dot(v, dsnb, NT)) * QK_SCALE
                dstate[h] = dsn * cd_ref[h] + _dot(qdb, dob, TN)
                dq = dqr * cc + pltpu.roll(dqr * ss, RET_QK_DIM // 2, 1)
                dk = dkr * cc + pltpu.roll(dkr * ss, RET_QK_DIM // 2, 1)
                dz_ref[rs, h * RET_QK_DIM:(h + 1) * RET_QK_DIM] = dq.astype(dz_ref.dtype)
                dz_ref[rs, QW + h * RET_QK_DIM:QW + (h + 1) * RET_QK_DIM] = dk.astype(dz_ref.dtype)
                dz_ref[rs, 2 * QW + h * RET_V_DIM:2 * QW + (h + 1) * RET_V_DIM] = dv.astype(dz_ref.dtype)
                dz_ref[rs, 2 * QW + VW + h * RET_V_DIM:2 * QW + VW + (h + 1) * RET_V_DIM] = dgt.astype(dz_ref.dtype)

    return pl.pallas_call(
        body,
        grid=(nb,),
        in_specs=[pl.BlockSpec((tr, RET_W), lambda n: (ix(n), 0)),
                  pl.BlockSpec((tr, VW), lambda n: (ix(n), 0)),
                  pl.BlockSpec((cpb, RET_HEADS, RET_QK_DIM, RET_V_DIM), lambda n: (ix(n), 0, 0, 0))] + tab_specs,
        out_specs=pl.BlockSpec((tr, RET_W), lambda n: (ix(n), 0)),
        out_shape=jax.ShapeDtypeStruct((T, RET_W), BF16),
        scratch_shapes=[pltpu.VMEM((RET_HEADS, RET_QK_DIM, RET_V_DIM), F32)],
        compiler_params=_cp("arbitrary"),
        name=name,
    )(z_a, dy, states, *tables)


ATT_TQ = 256
ATT_CPB = ATT_TQ // CHUNK
ATT_SCALE = ATT_HEAD_DIM ** -0.5


ATT_WIN = 3 * ATT_TQ
ATT_NB = CHUNK * ATT_BAND


def _rel_index():
    i = np.arange(CHUNK)[:, None]
    j = np.arange(ATT_BAND)[None, :]
    rel = np.clip(i + ATT_PAST * CHUNK - j, -MAX_REL, MAX_REL) + MAX_REL
    return jnp.asarray(rel.reshape(1, ATT_NB).astype(np.int32))


def _split3(x):
    hi = x.astype(BF16)
    r1 = x - hi.astype(F32)
    mid = r1.astype(BF16)
    lo = (r1 - mid.astype(F32)).astype(BF16)
    return hi, mid, lo


REL_TILE = 4608


def _bias_expand(rel_bias, name="bias_expand"):
    H = rel_bias.shape[0]
    n = ATT_NB
    padded = jnp.pad(rel_bias, ((0, 0), (0, N_REL_PAD - N_REL)))

    def body(rb_ref, idx_ref, o_ref):
        onehot = (lax.broadcasted_iota(jnp.int32, (N_REL_PAD, REL_TILE), 0) == idx_ref[...]).astype(BF16)
        hi, mid, lo = _split3(rb_ref[...])
        o_ref[...] = _dot(hi, onehot, NN) + _dot(mid, onehot, NN) + _dot(lo, onehot, NN)

    out = pl.pallas_call(
        body,
        grid=(n // REL_TILE,),
        in_specs=[pl.BlockSpec((H, N_REL_PAD), lambda t: (0, 0)), pl.BlockSpec((1, REL_TILE), lambda t: (0, t))],
        out_specs=pl.BlockSpec((H, REL_TILE), lambda t: (0, t)),
        out_shape=jax.ShapeDtypeStruct((H, n), F32),
        compiler_params=_cp("parallel"),
        name=name,
    )(padded, _rel_index())
    return out.reshape(H, CHUNK, ATT_BAND)


def _bias_tile(band, name="bias_tile"):
    H = band.shape[0]
    padded = jnp.pad(band, ((0, 0), (0, 0), (0, ATT_WIN - ATT_BAND)), constant_values=NEG_INF)

    def body(b_ref, o_ref):
        b = b_ref[...]
        col = lax.broadcasted_iota(jnp.int32, (CHUNK, ATT_WIN), 1)
        keep = col >= (2 - pl.program_id(0)) * ATT_TQ
        for a in range(ATT_CPB):
            o_ref[a * CHUNK:(a + 1) * CHUNK, :] = jnp.where(keep, pltpu.roll(b, a * CHUNK, 1) if a else b, NEG_INF)

    return pl.pallas_call(
        body,
        grid=(3, H),
        in_specs=[pl.BlockSpec((None, CHUNK, ATT_WIN), lambda v, h: (h, 0, 0))],
        out_specs=pl.BlockSpec((None, None, ATT_TQ, ATT_WIN), lambda v, h: (v, h, 0, 0)),
        out_shape=jax.ShapeDtypeStruct((3, H, ATT_TQ, ATT_WIN), F32),
        compiler_params=_cp("parallel", "parallel"),
        name=name,
    )(padded)


def _bias_untile(dtile, name="bias_untile"):
    H = dtile.shape[0]

    def body(d_ref, o_ref):
        acc = d_ref[0:CHUNK, :]
        for a in range(1, ATT_CPB):
            acc = acc + pltpu.roll(d_ref[a * CHUNK:(a + 1) * CHUNK, :], ATT_WIN - a * CHUNK, 1)
        o_ref[...] = acc

    out = pl.pallas_call(
        body,
        grid=(H,),
        in_specs=[pl.BlockSpec((None, ATT_TQ, ATT_WIN), lambda h: (h, 0, 0))],
        out_specs=pl.BlockSpec((None, CHUNK, ATT_WIN), lambda h: (h, 0, 0)),
        out_shape=jax.ShapeDtypeStruct((H, CHUNK, ATT_WIN), F32),
        compiler_params=_cp("parallel"),
        name=name,
    )(dtile)
    return out[:, :, :ATT_BAND]


def _bias_reduce(dbias, name="bias_reduce"):
    H = dbias.shape[0]
    n = ATT_NB

    def body(db_ref, idx_ref, o_ref):
        @pl.when(pl.program_id(0) == 0)
        def _():
            o_ref[...] = jnp.zeros_like(o_ref)

        onehot = (lax.broadcasted_iota(jnp.int32, (N_REL_PAD, REL_TILE), 0) == idx_ref[...]).astype(BF16)
        hi, mid, lo = _split3(db_ref[...])
        o_ref[...] += _dot(hi, onehot, NT) + _dot(mid, onehot, NT) + _dot(lo, onehot, NT)

    out = pl.pallas_call(
        body,
        grid=(n // REL_TILE,),
        in_specs=[pl.BlockSpec((H, REL_TILE), lambda t: (0, t)), pl.BlockSpec((1, REL_TILE), lambda t: (0, t))],
        out_specs=pl.BlockSpec((H, N_REL_PAD), lambda t: (0, 0)),
        out_shape=jax.ShapeDtypeStruct((H, N_REL_PAD), F32),
        compiler_params=_cp("arbitrary"),
        name=name,
    )(dbias.reshape(H, n), _rel_index())
    return out[:, :N_REL]


def _att_probs(q, kwin, bias):
    s = _dot(q, kwin, NT) + bias
    e = jnp.exp(s - jnp.max(s, axis=-1, keepdims=True))
    return e * (1.0 / jnp.sum(e, axis=-1, keepdims=True))


ATT_PAIR = 2 * ATT_HEAD_DIM
ATT_NP = ATT_HEADS // 2
ATT_QW = ATT_HEADS * ATT_HEAD_DIM
Y_COLS = RET_HEADS * RET_V_DIM + ATT_QW


def _att_specs(tq, nq, clip_q, q_col0):
    cb = ATT_QW // ATT_PAIR
    qi = (lambda p, m: (jnp.minimum(m, nq - 1), q_col0 + p)) if clip_q else (lambda p, m: (m, q_col0 + p))
    q = pl.BlockSpec((tq, ATT_PAIR), qi)

    def win(col0):
        return [pl.BlockSpec((tq, ATT_PAIR), functools.partial(lambda p, m, back: (jnp.clip(m - back, 0, nq - 1), col0 + p), back=b))
                for b in (2, 1, 0)]

    bias = pl.BlockSpec((None, 2, ATT_TQ, ATT_WIN), lambda p, m: (jnp.minimum(m, 2), p, 0, 0))
    return q, win(cb), win(2 * cb), bias


def _head_masks(rows):
    lane = lax.broadcasted_iota(jnp.int32, (rows, ATT_PAIR), 1)
    return lane < ATT_HEAD_DIM


def _att_fwd(z_b, bias, y, comm=None, name="att_fwd"):
    T = z_b.shape[0]
    tq = ATT_TQ
    nq = T // tq
    qs, kwin, vwin, bs = _att_specs(tq, nq, False, 0)
    nc = comm.n if comm else 0
    total = ATT_NP * nq

    def body(*refs):
        q_ref, k0, k1, k2, v0, v1, v2, b_ref = refs[:8]
        cin = refs[9:9 + nc]
        o_ref = refs[9 + nc]
        cout = refs[10 + nc:10 + 2 * nc]
        csem = refs[10 + 2 * nc:]
        m = pl.program_id(1)
        step = pl.program_id(0) * nq + m
        _host_comm(comm, "early", step, total, cin, cout, csem)
        kw = jnp.concatenate([k0[...], k1[...], k2[...]], axis=0)
        vw = jnp.concatenate([v0[...], v1[...], v2[...]], axis=0)
        q2 = q_ref[...] * ATT_SCALE
        even = _head_masks(tq)
        outs = []
        for hh in range(2):
            qm = jnp.where(even if hh == 0 else ~even, q2, jnp.zeros_like(q2))
            p = _att_probs(qm, kw, b_ref[hh])
            outs.append(_dot(p.astype(BF16), vw, NN))
        o_ref[...] = jnp.where(even, outs[0], outs[1]).astype(o_ref.dtype)
        _host_comm(comm, "late", step, total, cin, cout, csem)

    y_cb = (Y_COLS - ATT_QW) // ATT_PAIR
    out = pl.pallas_call(
        body,
        grid=(ATT_NP, nq),
        in_specs=[qs] + kwin + vwin + [bs, ANY] + [ANY] * nc,
        out_specs=[pl.BlockSpec((tq, ATT_PAIR), lambda p, m: (m, y_cb + p))] + [ANY] * nc,
        out_shape=[jax.ShapeDtypeStruct((T, Y_COLS), BF16)] + (comm.out_shape if comm else []),
        scratch_shapes=comm.scratch if comm else [],
        input_output_aliases={8: 0},
        compiler_params=_cp("arbitrary", "arbitrary"),
        name=name,
    )(z_b, z_b, z_b, z_b, z_b, z_b, z_b, bias, y, *(comm.arrays if comm else []))
    return out[0], list(out[1:])


def _att_bwd(z_b, bias, dy, comm=None, name="att_bwd"):
    T = z_b.shape[0]
    tq = ATT_TQ
    nq = T // tq
    y_cb = (Y_COLS - ATT_QW) // ATT_PAIR
    qs, kwin, vwin, bs = _att_specs(tq, nq, True, 0)
    dos = _att_specs(tq, nq, True, y_cb)[0]
    kv_out = pl.BlockSpec((tq, ATT_PAIR), lambda p, m: (jnp.maximum(m - 2, 0), p))
    W3 = 3 * tq
    nc = comm.n if comm else 0
    total = ATT_NP * (nq + 2)

    def body(*refs):
        q_ref, k0, k1, k2, v0, v1, v2, b_ref, do_ref = refs[:9]
        cin = refs[9:9 + nc]
        dq_ref, dk_ref, dv_ref, db_ref = refs[9 + nc:13 + nc]
        cout = refs[13 + nc:13 + 2 * nc]
        dkc, dvc, dkw, dvw = refs[13 + 2 * nc:17 + 2 * nc]
        csem = refs[17 + 2 * nc:]
        m = pl.program_id(1)
        step = pl.program_id(0) * (nq + 2) + m
        _host_comm(comm, "early", step, total, cin, cout, csem)

        @pl.when(m == 0)
        def _():
            dkc[...] = jnp.zeros_like(dkc)
            dvc[...] = jnp.zeros_like(dvc)
            db_ref[...] = jnp.zeros_like(db_ref)

        @pl.when(m >= nq)
        def _():
            dkw[...] = jnp.zeros_like(dkw)
            dvw[...] = jnp.zeros_like(dvw)

        @pl.when(m < nq)
        def _():
            kw = jnp.concatenate([k0[...], k1[...], k2[...]], axis=0)
            vw = jnp.concatenate([v0[...], v1[...], v2[...]], axis=0)
            q2, do2 = q_ref[...] * ATT_SCALE, do_ref[...]
            even = _head_masks(tq)
            dqs, dks, dvs = [], [], []
            for hh in range(2):
                mine = even if hh == 0 else ~even
                p = _att_probs(jnp.where(mine, q2, jnp.zeros_like(q2)), kw, b_ref[hh])
                dp = _dot(jnp.where(mine, do2, jnp.zeros_like(do2)), vw, NT)
                ds = p * (dp - jnp.sum(dp * p, axis=-1, keepdims=True))
                db_ref[hh] += ds
                dsb = ds.astype(BF16)
                dqs.append(_dot(dsb, kw, NN))
                dks.append(_dot(dsb, q2, TN))
                dvs.append(_dot(p.astype(BF16), do2, TN))
            even_w = _head_masks(W3)
            dq_ref[...] = (jnp.where(even, dqs[0], dqs[1]) * ATT_SCALE).astype(dq_ref.dtype)
            dkw[...] = jnp.where(even_w, dks[0], dks[1])
            dvw[...] = jnp.where(even_w, dvs[0], dvs[1])

        dk_ref[...] = (dkc[0:tq, :] + dkw[0:tq, :]).astype(dk_ref.dtype)
        dv_ref[...] = (dvc[0:tq, :] + dvw[0:tq, :]).astype(dv_ref.dtype)
        dkc[0:tq, :] = dkc[tq:2 * tq, :] + dkw[tq:2 * tq, :]
        dvc[0:tq, :] = dvc[tq:2 * tq, :] + dvw[tq:2 * tq, :]
        dkc[tq:2 * tq, :] = dkw[2 * tq:W3, :]
        dvc[tq:2 * tq, :] = dvw[2 * tq:W3, :]
        _host_comm(comm, "late", step, total, cin, cout, csem)

    qo = pl.BlockSpec((tq, ATT_PAIR), lambda p, m: (jnp.minimum(m, nq - 1), p))
    dbs = pl.BlockSpec((2, ATT_TQ, ATT_WIN), lambda p, m: (p, 0, 0))
    hd = jax.ShapeDtypeStruct((T, ATT_QW), BF16)
    out = pl.pallas_call(
        body,
        grid=(ATT_NP, nq + 2),
        in_specs=[qs] + kwin + vwin + [bs, dos] + [ANY] * nc,
        out_specs=[qo, kv_out, kv_out, dbs] + [ANY] * nc,
        out_shape=[hd, hd, hd, jax.ShapeDtypeStruct((ATT_HEADS, ATT_TQ, ATT_WIN), F32)] + (comm.out_shape if comm else []),
        scratch_shapes=[pltpu.VMEM((2 * tq, ATT_PAIR), F32), pltpu.VMEM((2 * tq, ATT_PAIR), F32),
                        pltpu.VMEM((W3, ATT_PAIR), F32), pltpu.VMEM((W3, ATT_PAIR), F32)] + (comm.scratch if comm else []),
        compiler_params=_cp("arbitrary", "arbitrary"),
        name=name,
    )(z_b, z_b, z_b, z_b, z_b, z_b, z_b, bias, dy, *(comm.arrays if comm else []))
    return out[0], out[1], out[2], out[3], list(out[4:])


REST = ["ab_out", "c_in_t", "c_out", "up_t0", "up_t1", "down0", "down1"]


def _local_step(x, target, wt, small, rest_shards=None, overlap=False, hn0=None):
    T = x.shape[0]
    Fh = FFN_HIDDEN
    tables = _ret_tables(T)
    gw, gs, recv = {}, {}, {}
    wt = dict(wt)

    if hn0 is None:
        hn0 = _rms_fwd(x, small["attn_norm_g"][0], name="rms_fwd")
    z_a = _mm(hn0, wt["ab_in_t"][:RET_W], "nt", BF16, name="mm_ab_in_a")
    z_b = _mm(hn0, wt["ab_in_t"][RET_W:], "nt", BF16, name="mm_ab_in_b")
    y, states = _ret_fwd(z_a, tables)
    bias = _bias_tile(_bias_expand(small["rel_bias"]))
    y, rest = _att_fwd(z_b, bias, y, comm=_Comm("gather", rest_shards) if rest_shards is not None else None)
    if rest_shards is not None:
        full = dict(zip(REST, rest))
        wt.update(ab_out=full["ab_out"], c_in_t=full["c_in_t"], c_out=full["c_out"],
                  up_t=[full["up_t0"], full["up_t1"]], down=[full["down0"], full["down1"]])
    h1, hf0 = _mm_rows(y, wt["ab_out"], x, "mm_ab_out", norm_g=small["ffn_norm_g"][0])

    def ffn_fwd(h, hf, layer, next_g):
        zg, zu, a = _ffn_up_mid(hf, wt["up_t"][layer], small["conv_w"][layer], small["conv_b"][layer][None, :])
        if next_g is None:
            return _mm_rows(a, wt["down"][layer], h, "mm_down_loss", loss=(target, small["final_g"])), (hf, zg, zu, a)
        h_out, hn_next = _mm_rows(a, wt["down"][layer], h, "mm_down", norm_g=next_g)
        return h_out, hn_next, (hf, zg, zu, a)

    def ffn_bwd(dh_out, h, layer, saved, exchange=None):
        hf, zg, zu, a = saved
        da = _mm(dh_out, wt["down"][layer], "nt", BF16, name="mm_d_a")
        d_down = _mm(a, dh_out, "tn", BF16, name="mm_dw_down")
        comm = _Comm("exchange", [gw[k] for k in exchange]) if exchange else None
        dzg, dzu, dcw, dcb, got = _ffn_mid_bwd(zg, zu, da, small["conv_w"][layer], small["conv_b"][layer][None, :], comm=comm)
        recv.update(zip(exchange or [], got))
        dh, dg = _mm_rows(dzg, wt["up_t"][layer][:Fh], None, "mm_d_hf_norm", second=(dzu, wt["up_t"][layer][Fh:]),
                          bwd=(h, small["ffn_norm_g"][layer], dh_out))
        d_up = _mm(dzg, hf, "tn", BF16, name="mm_dw_up", rows=(2 * Fh, 0, None))
        d_up = _mm(dzu, hf, "tn", BF16, name="mm_dw_up_2", rows=(2 * Fh, Fh, d_up))
        return dh, dg, d_up, d_down, dcw, dcb

    h2, hn1, ffn0 = ffn_fwd(h1, hf0, 0, small["attn_norm_g"][1])

    zz = _mm(hn1, wt["c_in_t"], "nt", BF16, name="mm_c_in")
    ys = _sgu_fwd(zz, small["ln_g"], small["ln_b"], small["w_s"], small["b_s"])
    h3, hf1 = _mm_rows(ys, wt["c_out"], h2, "mm_c_out", norm_g=small["ffn_norm_g"][1])
    (loss_vec, dh4, gs["final_g"]), ffn1 = ffn_fwd(h3, hf1, 1, None)

    dh3, dgf1, gw["up_t1"], gw["down1"], dcw1, dcb1 = ffn_bwd(dh4, h3, 1, ffn1)
    dys = _mm(dh3, wt["c_out"], "nt", BF16, name="mm_d_ys")
    gw["c_out"] = _mm(ys, dh3, "tn", BF16, name="mm_dw_c_out")
    dzz, gs["w_s"], dbs, dlg, dlb = _sgu_bwd(zz, dys, small["ln_g"], small["ln_b"], small["w_s"], small["b_s"])
    gs["b_s"], gs["ln_g"], gs["ln_b"] = dbs[:, :, 0], dlg[0], dlb[0]
    dh2, dga1 = _mm_rows(dzz, wt["c_in_t"], None, "mm_d_hn1_norm", bwd=(h2, small["attn_norm_g"][1], dh3))
    gw["c_in_t"] = _mm(dzz, hn1, "tn", BF16, name="mm_dw_c_in")

    dh1, dgf0, gw["up_t0"], gw["down0"], dcw0, dcb0 = ffn_bwd(
        dh2, h1, 0, ffn0, exchange=["c_in_t", "c_out", "up_t1", "down1"] if overlap else None)

    dy = _mm(dh1, wt["ab_out"], "nt", BF16, name="mm_d_y")
    gw["ab_out"] = _mm(y, dh1, "tn", BF16, name="mm_dw_ab_out")
    dz_a = _ret_bwd(z_a, dy, states, tables)
    late = ["ab_out", "up_t0", "down0"] if overlap else []
    dq, dk, dv, dbias, got = _att_bwd(z_b, bias, dy, comm=_Comm("exchange", [gw[k] for k in late]) if late else None)
    recv.update(zip(late, got))
    dz_b = jnp.concatenate([dq, dk, dv], axis=1)
    gs["rel_bias"] = _bias_reduce(_bias_untile(dbias))
    gw["ab_in_t"] = _mm(dz_a, hn0, "tn", BF16, name="mm_dw_ab_in_a", rows=(RET_W + ATT_W, 0, None))
    gw["ab_in_t"] = _mm(dz_b, hn0, "tn", BF16, name="mm_dw_ab_in_b", rows=(RET_W + ATT_W, RET_W, gw["ab_in_t"]))
    if overlap:
        dhn0, got = _mm(dz_a, wt["ab_in_t"][:RET_W], "nn", F32, name="mm_d_hn0", comm=_Comm("exchange", [gw["ab_in_t"]]))
        recv["ab_in_t"] = got[0]
    else:
        dhn0 = _mm(dz_a, wt["ab_in_t"][:RET_W], "nn", F32, name="mm_d_hn0")
    grad_x, dga0 = _mm_rows(dz_b, wt["ab_in_t"][RET_W:], dhn0, "mm_d_hn0_norm", bwd=(x, small["attn_norm_g"][0], dh1))

    gs["attn_norm_g"] = jnp.concatenate([dga0, dga1], axis=0)
    gs["ffn_norm_g"] = jnp.concatenate([dgf0, dgf1], axis=0)
    gs["conv_w"] = jnp.stack([dcw0, dcw1])
    gs["conv_b"] = jnp.concatenate([dcb0, dcb1], axis=0)
    gs["final_g"] = gs["final_g"][0]
    return loss_vec[0, 0], grad_x, gw, gs, recv


MESH_ID = pl.DeviceIdType.MESH
ANY = pl.BlockSpec(memory_space=pl.ANY)


def _my_place():
    return lax.axis_index("x"), lax.axis_index("y"), lax.axis_index("c")


class _Comm:
    def __init__(self, kind, arrays):
        self.kind, self.arrays, self.n = kind, list(arrays), len(arrays)
        if kind == "gather":
            self.out_shape = [jax.ShapeDtypeStruct((N_DEV * s.shape[0], s.shape[1]), s.dtype) for s in arrays]
        else:
            self.out_shape = [jax.ShapeDtypeStruct((N_DEV, g.shape[0] // N_DEV, g.shape[1]), g.dtype) for g in arrays]
        n = self.n
        self.scratch = [pltpu.SemaphoreType.DMA((n, 7)), pltpu.SemaphoreType.DMA((n, 7)), pltpu.SemaphoreType.DMA((n,))]

    def phase(self, ph, in_refs, out_refs, sems):
        (self._gather if self.kind == "gather" else self._exchange)(ph, in_refs, out_refs, sems)

    def _gather(self, ph, x_refs, o_refs, sems):
        n = self.n
        send_sems, recv_sems, local_sems = sems
        x, y, c = _my_place()
        me, sibling = (x, y, c), (x, y, 1 - c)
        chips = [(1 - x, y), (x, 1 - y), (1 - x, 1 - y)]

        def rows(a, place):
            m = x_refs[a].shape[0]
            px, py, pc = place
            return o_refs[a].at[pl.ds((4 * px + 2 * py + pc) * m, m), :]

        def copy(a, k, block, to, own=False):
            return pltpu.make_async_remote_copy(
                src_ref=x_refs[a] if own else rows(a, block), dst_ref=rows(a, block),
                send_sem=send_sems.at[a, k], recv_sem=recv_sems.at[a, k], device_id=to, device_id_type=MESH_ID)

        def mine():
            return [pltpu.make_async_copy(x_refs[a], rows(a, me), local_sems.at[a]) for a in range(n)]

        def first():
            out = []
            for a in range(n):
                out.append(copy(a, 0, me, sibling, own=True))
                out += [copy(a, 1 + j, me, (*chip, c), own=True) for j, chip in enumerate(chips)]
            return out

        def passed():
            return [copy(a, 4 + j, (*chip, c), sibling) for j, chip in enumerate(chips) for a in range(n)]

        if ph == 0:
            for cp in mine() + first():
                cp.start()
        elif ph == 1:
            fw = passed()
            for j, chip in enumerate(chips):
                for a in range(n):
                    copy(a, 1 + j, (*chip, c), me).wait_recv()
                    fw[j * n + a].start()
        else:
            for a in range(n):
                copy(a, 0, sibling, me).wait_recv()
                for j, chip in enumerate(chips):
                    copy(a, 4 + j, (*chip, 1 - c), me).wait_recv()
            for cp in first() + passed():
                cp.wait_send()
            for cp in mine():
                cp.wait()

    def _exchange(self, ph, g_refs, o_refs, sems):
        n = self.n
        send_sems, recv_sems, local_sems = sems
        x, y, c = _my_place()
        me = 4 * x + 2 * y + c
        peers = [(x ^ ((k >> 2) & 1), y ^ ((k >> 1) & 1), c ^ (k & 1)) for k in range(1, N_DEV)]

        def block(a, idx):
            m = g_refs[a].shape[0] // N_DEV
            return g_refs[a].at[pl.ds(idx * m, m), :]

        def copy(a, k, slot):
            px, py, pc = peers[k]
            return pltpu.make_async_remote_copy(
                src_ref=block(a, 4 * px + 2 * py + pc), dst_ref=o_refs[a].at[slot],
                send_sem=send_sems.at[a, k], recv_sem=recv_sems.at[a, k], device_id=peers[k], device_id_type=MESH_ID)

        if ph == 1:
            return
        mine = [pltpu.make_async_copy(block(a, me), o_refs[a].at[me], local_sems.at[a]) for a in range(n)]
        sends = [copy(a, k, me) for k in range(N_DEV - 1) for a in range(n)]
        if ph == 0:
            for cp in mine + sends:
                cp.start()
        else:
            for k in range(N_DEV - 1):
                px, py, pc = peers[k]
                for a in range(n):
                    copy(a, k, 4 * px + 2 * py + pc).wait_recv()
            for cp in sends:
                cp.wait_send()
            for cp in mine:
                cp.wait()


def _comm_call(comm, name):
    n = comm.n

    def body(*refs):
        for ph in range(3):
            comm.phase(ph, refs[:n], refs[n:2 * n], refs[2 * n:])

    return pl.pallas_call(
        body, out_shape=comm.out_shape, in_specs=[ANY] * n, out_specs=[ANY] * n, scratch_shapes=comm.scratch, name=name,
    )(*comm.arrays)


def _host_comm(comm, when, step, total, cin, cout, csem):
    if comm is None:
        return
    sched = {0: 0, 1: (3 * total) // 4, 2: total - 1}
    phases = (0, 1) if when == "early" else (2,)
    for ph in phases:
        if ph == 1 and comm.kind == "exchange":
            continue

        @pl.when(step == sched[ph])
        def _(ph=ph):
            comm.phase(ph, cin, cout, csem)


def _all_gather(shards, name="all_gather"):
    return _comm_call(_Comm("gather", shards), name)


def _row_tile(r, target=256):
    best = None
    for t in range(8, min(r, target) + 1, 8):
        if r % t == 0:
            best = t
    return best if best is not None else r


def _sum8(parts, name="sum8"):
    _, M, N = parts.shape
    tr = _row_tile(M, 128)

    def body(p_ref, o_ref):
        acc = p_ref[0].astype(F32)
        for d in range(1, N_DEV):
            acc = acc + p_ref[d].astype(F32)
        o_ref[...] = acc

    return pl.pallas_call(
        body,
        grid=(M // tr,),
        in_specs=[pl.BlockSpec((N_DEV, tr, N), lambda i: (0, i, 0))],
        out_specs=pl.BlockSpec((tr, N), lambda i: (i, 0)),
        out_shape=jax.ShapeDtypeStruct((M, N), F32),
        compiler_params=_cp("parallel"),
        name=name,
    )(parts)


def _adamw(w, g, m, v, name="adamw"):
    shape = w.shape
    if w.ndim == 1:
        r2 = (1, shape[0])
    else:
        r2 = (int(np.prod(shape[:-1])), shape[-1])
    R, C = r2
    tr = _row_tile(R)
    bc1 = 1.0 - ADAM_B1 ** ADAM_STEP
    bc2 = 1.0 - ADAM_B2 ** ADAM_STEP

    def body(w_ref, g_ref, m_ref, v_ref, d_ref, nm_ref, nv_ref):
        gv = g_ref[...]
        nm = ADAM_B1 * m_ref[...] + (1.0 - ADAM_B1) * gv
        nv = ADAM_B2 * v_ref[...] + (1.0 - ADAM_B2) * (gv * gv)
        d_ref[...] = -ADAM_LR * ((nm / bc1) / (jnp.sqrt(nv / bc2) + ADAM_EPS) + ADAM_WD * w_ref[...])
        nm_ref[...] = nm
        nv_ref[...] = nv

    spec = pl.BlockSpec((tr, C), lambda i: (i, 0))
    out = pl.pallas_call(
        body,
        grid=(R // tr,),
        in_specs=[spec] * 4,
        out_specs=[spec] * 3,
        out_shape=[jax.ShapeDtypeStruct(r2, F32)] * 3,
        compiler_params=_cp("parallel"),
        name=name,
    )(w.reshape(r2), g.reshape(r2), m.reshape(r2), v.reshape(r2))
    return [o.reshape(shape) for o in out]


WEIGHTS = ['attn_norm_g', 'ffn_norm_g', 'ab_w_in', 'ab_w_out', 'ab_rel_bias', 'c_w_in', 'c_ln_g', 'c_ln_b', 'c_w_s', 'c_b_s',
           'c_w_out', 'ffn_w_up', 'ffn_conv_w', 'ffn_conv_b', 'ffn_w_down', 'final_norm_g']
SMALL_ORDER = ["attn_norm_g", "ffn_norm_g", "rel_bias", "ln_g", "ln_b", "w_s", "b_s", "conv_w", "conv_b", "final_g"]
PACK_ROW = 1024


def _pack(arrs):
    flat = jnp.concatenate([a.reshape(-1) for a in arrs])
    n = flat.shape[0]
    padded = -(-n // PACK_ROW) * PACK_ROW
    return jnp.pad(flat, (0, padded - n)).reshape(padded // 128, 128)


def _unpack(flat, shapes):
    out, off = [], 0
    for s in shapes:
        n = int(np.prod(s))
        out.append(flat[off:off + n].reshape(s))
        off += n
    return out


def _step(P):
    x, target = P["x"][0], P["loss_target"][0]
    me = 4 * lax.axis_index("x") + 2 * lax.axis_index("y") + lax.axis_index("c")
    n_up = P["ffn_w_up"].shape[0]
    Fc = P["ffn_conv_w"].shape[-1]
    Lc = P["c_ln_g"].shape[-1]

    first = _Comm("gather", [P["ab_w_in"][0].T.astype(BF16), _pack([P["ffn_conv_w"], P["c_ln_g"], P["c_ln_b"]])])
    hn0, full = _rms_fwd(x, P["attn_norm_g"][0], comm=first, name="rms_fwd_gather")
    wt = {"ab_in_t": full[0]}
    rest = {"ab_out": P["ab_w_out"][0], "c_in_t": P["c_w_in"][0].T, "c_out": P["c_w_out"][0],
            "up_t0": P["ffn_w_up"][0].T, "up_t1": P["ffn_w_up"][1].T, "down0": P["ffn_w_down"][0], "down1": P["ffn_w_down"][1]}
    rest_shards = [rest[k].astype(BF16) for k in REST]
    sm = full[-1].reshape(N_DEV, -1)
    conv_w = sm[:, :n_up * 3 * Fc].reshape(N_DEV, n_up, 3, Fc).transpose(1, 2, 0, 3).reshape(n_up, 3, N_DEV * Fc)
    off = n_up * 3 * Fc
    ln_g = sm[:, off:off + Lc].reshape(N_DEV * Lc)
    ln_b = sm[:, off + Lc:off + 2 * Lc].reshape(N_DEV * Lc)
    small = {"attn_norm_g": P["attn_norm_g"], "ffn_norm_g": P["ffn_norm_g"], "rel_bias": P["ab_rel_bias"][0],
             "ln_g": ln_g, "ln_b": ln_b, "w_s": P["c_w_s"][0], "b_s": P["c_b_s"][0], "conv_w": conv_w,
             "conv_b": P["ffn_conv_b"], "final_g": P["final_norm_g"]}

    loss_part, grad_x, gw, gs, recv = _local_step(x, target, wt, small, rest_shards=rest_shards, overlap=True, hn0=hn0)
    loss = lax.psum(loss_part, ("x", "y", "c"))

    s8 ={k: _sum8(recv[k], name="sum8") for k in ["ab_in_t"] + REST}
    g_big = {"ab_w_in": s8["ab_in_t"].T[None], "ab_w_out": s8["ab_out"][None], "c_w_in": s8["c_in_t"].T[None],
             "c_w_out": s8["c_out"][None], "ffn_w_up": jnp.stack([s8["up_t0"].T, s8["up_t1"].T]),
             "ffn_w_down": jnp.stack([s8["down0"], s8["down1"]])}

    packed = _pack([gs[k] for k in SMALL_ORDER])
    gathered = _all_gather([packed], name="gather_small_grads")[0]
    tot = _sum8(gathered.reshape(N_DEV, packed.shape[0], 128), name="sum8_small").reshape(-1)
    gsm = dict(zip(SMALL_ORDER, _unpack(tot, [gs[k].shape for k in SMALL_ORDER])))
    grads = dict(g_big)
    grads["attn_norm_g"] = gsm["attn_norm_g"]
    grads["ffn_norm_g"] = gsm["ffn_norm_g"]
    grads["ab_rel_bias"] = gsm["rel_bias"][None]
    grads["c_ln_g"] = lax.dynamic_slice(gsm["ln_g"], (me * Lc,), (Lc,))[None]
    grads["c_ln_b"] = lax.dynamic_slice(gsm["ln_b"], (me * Lc,), (Lc,))[None]
    grads["c_w_s"] = gsm["w_s"][None]
    grads["c_b_s"] = gsm["b_s"][None]
    grads["ffn_conv_w"] = lax.dynamic_slice(gsm["conv_w"], (0, 0, me * Fc), (n_up, 3, Fc))
    grads["ffn_conv_b"] = gsm["conv_b"]
    grads["final_norm_g"] = gsm["final_g"]

    delta, new_m, new_v = {}, {}, {}
    for k in WEIGHTS:
        delta[k], new_m[k], new_v[k] = _adamw(P[k], grads[k], P["m_" + k], P["v_" + k], name="adamw")
    return (loss, grad_x[None], *[grads[k] for k in WEIGHTS], *[delta[k] for k in WEIGHTS],
            *[new_m[k] for k in WEIGHTS], *[new_v[k] for k in WEIGHTS])


def kernel(x, attn_norm_g, ffn_norm_g, ab_w_in, ab_w_out, ab_rel_bias, c_w_in, c_ln_g, c_ln_b, c_w_s, c_b_s, c_w_out, ffn_w_up, ffn_conv_w, ffn_conv_b, ffn_w_down, final_norm_g, loss_target, m_attn_norm_g, m_ffn_norm_g, m_ab_w_in, m_ab_w_out, m_ab_rel_bias, m_c_w_in, m_c_ln_g, m_c_ln_b, m_c_w_s, m_c_b_s, m_c_w_out, m_ffn_w_up, m_ffn_conv_w, m_ffn_conv_b, m_ffn_w_down, m_final_norm_g, v_attn_norm_g, v_ffn_norm_g, v_ab_w_in, v_ab_w_out, v_ab_rel_bias, v_c_w_in, v_c_ln_g, v_c_ln_b, v_c_w_s, v_c_b_s, v_c_w_out, v_ffn_w_up, v_ffn_conv_w, v_ffn_conv_b, v_ffn_w_down, v_final_norm_g):
    return _step(dict(locals()))
```

```python
import functools

import numpy as np
import jax
import jax.numpy as jnp
from jax import lax
from jax.experimental import pallas as pl
from jax.experimental.pallas import tpu as pltpu

F32 = jnp.float32
BF16 = jnp.bfloat16

D_MODEL = 1024
CHUNK = 64
EPS = 1e-6
NEG_INF = -1e30
RET_HEADS = 4
RET_QK_DIM = 128
RET_V_DIM = 256
ATT_HEADS = 8
ATT_HEAD_DIM = 64
ATT_PAST = 8
ATT_BAND = (ATT_PAST + 1) * CHUNK
MAX_REL = 128
N_REL = 2 * MAX_REL + 1
N_REL_PAD = 384
SGU_BLOCK = 128
SGU_GROUPS = 8
SGU_WIDTH = 2048
SGU_GW = SGU_WIDTH // SGU_GROUPS
FFN_HIDDEN = 2816
RET_W = 2 * RET_HEADS * RET_QK_DIM + 2 * RET_HEADS * RET_V_DIM
ATT_W = 3 * ATT_HEADS * ATT_HEAD_DIM
N_DEV = 8

ADAM_LR = 0.001
ADAM_B1 = 0.9
ADAM_B2 = 0.999
ADAM_EPS = 1e-08
ADAM_WD = 0.01
ADAM_STEP = 10

VMEM_LIMIT = 52 * 1024 * 1024


def _cp(*sem):
    return pltpu.CompilerParams(dimension_semantics=sem if sem else None, vmem_limit_bytes=VMEM_LIMIT)


def _tile(n, target):
    if n <= target:
        return n
    best = None
    for t in range(128, target + 1, 128):
        if n % t == 0:
            best = t
    assert best is not None, (n, target)
    return best


def _gelu(x):
    c = 0.7978845608028654
    return 0.5 * x * (1.0 + jnp.tanh(c * (x + 0.044715 * x * x * x)))


def _gelu_and_grad(x):
    c = 0.7978845608028654
    x2 = x * x
    t = jnp.tanh(c * (x + 0.044715 * x * x2))
    cdf = 0.5 * (1.0 + t)
    grad = cdf + x * (0.5 * c) * (1.0 - t * t) * (1.0 + 3.0 * 0.044715 * x2)
    return x * cdf, grad


def _dot(a, b, dims):
    return lax.dot_general(a, b, (dims, ((), ())), preferred_element_type=F32)


NN = ((1,), (0,))
NT = ((1,), (1,))
TN = ((0,), (0,))


def _mm(a, b, mode, out_dtype, res=None, name="mm", tm_t=None, tn_t=None, tk_t=None, comm=None, rows=None):
    if mode == "nt":
        (M, K), N = a.shape, b.shape[0]
        dm, dn, dk = (1024, 2816, K) if N <= 2816 else (1024, 1024, K)
    elif mode == "nn":
        (M, K), N = a.shape, b.shape[1]
        dm, dn, dk = (1024 if K <= 3072 else 512), 1024, K
    else:
        (K, M), N = a.shape, b.shape[1]
        dm, dn, dk = 1536, 1024, 2048
    tm, tn, tk = _tile(M, tm_t or dm), _tile(N, tn_t or dn), _tile(K, tk_t or dk)
    nk = K // tk
    dims = {"nt": NT, "nn": NN, "tn": TN}[mode]
    a_spec = pl.BlockSpec((tk, tm), lambda i, j, k: (k, i)) if mode == "tn" else pl.BlockSpec((tm, tk), lambda i, j, k: (i, k))
    b_spec = pl.BlockSpec((tn, tk), lambda i, j, k: (j, k)) if mode == "nt" else pl.BlockSpec((tk, tn), lambda i, j, k: (k, j))
    gi, gj = M // tm, N // tn
    out_rows, row0, into = rows if rows else (M, 0, None)
    assert row0 % tm == 0 and not (comm and into is not None)
    o_spec = pl.BlockSpec((tm, tn), lambda i, j, k: (i + row0 // tm, j))
    has_res = res is not None
    nc = 1 if into is not None else (comm.n if comm else 0)
    n_in = 3 if has_res else 2

    def body(*refs):
        a_ref, b_ref = refs[:2]
        r_ref = refs[2] if has_res else None
        nco = comm.n if comm else 0
        cin = refs[n_in:n_in + nc]
        o_ref = refs[n_in + nc]
        cout = refs[n_in + nc + 1:n_in + nc + 1 + nco]
        scratch = refs[n_in + nc + 1 + nco:]
        csem = scratch[1:] if nk > 1 else scratch
        step = (pl.program_id(0) * gj + pl.program_id(1)) * nk + pl.program_id(2)
        _host_comm(comm, "early", step, gi * gj * nk, cin, cout, csem)
        p = _dot(a_ref[...].astype(BF16), b_ref[...].astype(BF16), dims)
        if nk == 1:
            if has_res:
                p = p + r_ref[...]
            o_ref[...] = p.astype(out_dtype)
        else:
            acc = scratch[0]
            k = pl.program_id(2)

            @pl.when(k == 0)
            def _():
                acc[...] = p

            @pl.when(k > 0)
            def _():
                acc[...] += p

            @pl.when(k == nk - 1)
            def _():
                t = acc[...]
                if has_res:
                    t = t + r_ref[...]
                o_ref[...] = t.astype(out_dtype)
        _host_comm(comm, "late", step, gi * gj * nk, cin, cout, csem)

    in_specs = [a_spec, b_spec] + ([o_spec] if has_res else []) + [ANY] * nc
    args = (a, b) + ((res,) if has_res else ()) + ((into,) if into is not None else tuple(comm.arrays if comm else ()))
    out = pl.pallas_call(
        body,
        grid=(gi, gj, nk),
        in_specs=in_specs,
        out_specs=[o_spec] + ([ANY] * nc if comm else []),
        out_shape=[jax.ShapeDtypeStruct((out_rows, N), out_dtype)] + (comm.out_shape if comm else []),
        scratch_shapes=([pltpu.VMEM((tm, tn), F32)] if nk > 1 else []) + (comm.scratch if comm else []),
        input_output_aliases={n_in: 0} if into is not None else {},
        compiler_params=_cp("arbitrary", "arbitrary", "arbitrary") if comm else _cp("parallel", "parallel", "arbitrary"),
        name=name,
    )(*args)
    return (out[0], list(out[1:])) if comm else out[0]


def _mm_rows(a, b, res, name, norm_g=None, bwd=None, loss=None, second=None, tm=512):
    M, K = a.shape
    Dm = b.shape[1]
    tm = min(tm, M)
    row = pl.BlockSpec((tm, Dm), lambda i: (i, 0))
    vec = pl.BlockSpec((1, Dm), lambda i: (0, 0))
    a_spec = pl.BlockSpec((tm, K), lambda i: (i, 0))
    b_spec = pl.BlockSpec((K, Dm), lambda i: (0, 0))

    if loss is not None:
        target, g = loss

        def body(a_ref, b_ref, r_ref, g_ref, t_ref, loss_ref, dh_ref, dg_ref):
            @pl.when(pl.program_id(0) == 0)
            def _():
                loss_ref[...] = jnp.zeros_like(loss_ref)
                dg_ref[...] = jnp.zeros_like(dg_ref)

            x = _dot(a_ref[...].astype(BF16), b_ref[...].astype(BF16), NN) + r_ref[...]
            gv = g_ref[...]
            r = lax.rsqrt(jnp.mean(x * x, axis=-1, keepdims=True) + EPS)
            xhat = x * r
            e = xhat * gv - t_ref[...]
            loss_ref[...] += jnp.full((1, 128), 0.5 / Dm, F32) * jnp.sum(e * e)
            dy = e * (1.0 / Dm)
            dg_ref[...] += jnp.sum(dy * xhat, axis=0, keepdims=True)
            dx = dy * gv
            m = jnp.mean(dx * xhat, axis=-1, keepdims=True)
            dh_ref[...] = r * (dx - xhat * m)

        return pl.pallas_call(
            body, grid=(M // tm,), in_specs=[a_spec, b_spec, row, vec, row],
            out_specs=[pl.BlockSpec((1, 128), lambda i: (0, 0)), row, vec],
            out_shape=[jax.ShapeDtypeStruct((1, 128), F32), jax.ShapeDtypeStruct((M, Dm), F32), jax.ShapeDtypeStruct((1, Dm), F32)],
            compiler_params=_cp("arbitrary"), name=name,
        )(a, b, res, g.reshape(1, Dm), target)

    if bwd is None:
        def body(a_ref, b_ref, r_ref, g_ref, o_ref, n_ref):
            t = _dot(a_ref[...].astype(BF16), b_ref[...].astype(BF16), NN) + r_ref[...]
            o_ref[...] = t
            r = lax.rsqrt(jnp.mean(t * t, axis=-1, keepdims=True) + EPS)
            n_ref[...] = (t * r * g_ref[...]).astype(n_ref.dtype)

        return pl.pallas_call(
            body, grid=(M // tm,), in_specs=[a_spec, b_spec, row, vec], out_specs=[row, row],
            out_shape=[jax.ShapeDtypeStruct((M, Dm), F32), jax.ShapeDtypeStruct((M, Dm), BF16)],
            compiler_params=_cp("parallel"), name=name,
        )(a, b, res, norm_g.reshape(1, Dm))

    h, g, dres = bwd
    has_res = res is not None
    has2 = second is not None

    def body(*refs):
        a_ref, b_ref = refs[:2]
        h_ref, g_ref, dres_ref, dh_ref, dg_ref = refs[-5:]

        @pl.when(pl.program_id(0) == 0)
        def _():
            dg_ref[...] = jnp.zeros_like(dg_ref)

        d = _dot(a_ref[...].astype(BF16), b_ref[...].astype(BF16), NN)
        if has2:
            d = d + _dot(refs[2][...].astype(BF16), refs[3][...].astype(BF16), NN)
        if has_res:
            d = d + refs[4 if has2 else 2][...]
        x = h_ref[...]
        r = lax.rsqrt(jnp.mean(x * x, axis=-1, keepdims=True) + EPS)
        xhat = x * r
        dg_ref[...] += jnp.sum(d * xhat, axis=0, keepdims=True)
        dx = d * g_ref[...]
        m = jnp.mean(dx * xhat, axis=-1, keepdims=True)
        dh_ref[...] = dres_ref[...] + r * (dx - xhat * m)

    second_specs = [pl.BlockSpec((tm, second[0].shape[1]), lambda i: (i, 0)),
                    pl.BlockSpec(second[1].shape, lambda i: (0, 0))] if has2 else []
    return pl.pallas_call(
        body, grid=(M // tm,), in_specs=[a_spec, b_spec] + second_specs + ([row] if has_res else []) + [row, vec, row],
        out_specs=[row, vec],
        out_shape=[jax.ShapeDtypeStruct((M, Dm), F32), jax.ShapeDtypeStruct((1, Dm), F32)],
        compiler_params=_cp("arbitrary"), name=name,
    )(a, b, *(second if has2 else ()), *((res,) if has_res else ()), h, g.reshape(1, Dm), dres)


def _rms_fwd(h, g, comm=None, name="rms_fwd", tm=512):
    T, Dm = h.shape
    tm = min(tm, T)
    nc = comm.n if comm else 0
    ni = T // tm

    def body(*refs):
        h_ref, g_ref = refs[:2]
        cin, o_ref, cout, csem = refs[2:2 + nc], refs[2 + nc], refs[3 + nc:3 + 2 * nc], refs[3 + 2 * nc:]
        step = pl.program_id(0)
        _host_comm(comm, "early", step, ni, cin, cout, csem)
        x = h_ref[...]
        r = lax.rsqrt(jnp.mean(x * x, axis=-1, keepdims=True) + EPS)
        o_ref[...] = (x * r * g_ref[...]).astype(o_ref.dtype)
        _host_comm(comm, "late", step, ni, cin, cout, csem)

    out = pl.pallas_call(
        body,
        grid=(ni,),
        in_specs=[pl.BlockSpec((tm, Dm), lambda i: (i, 0)), pl.BlockSpec((1, Dm), lambda i: (0, 0))] + [ANY] * nc,
        out_specs=[pl.BlockSpec((tm, Dm), lambda i: (i, 0))] + [ANY] * nc,
        out_shape=[jax.ShapeDtypeStruct((T, Dm), BF16)] + (comm.out_shape if comm else []),
        scratch_shapes=comm.scratch if comm else [],
        compiler_params=_cp("arbitrary"),
        name=name,
    )(h, g.reshape(1, Dm), *(comm.arrays if comm else []))
    return (out[0], list(out[1:])) if comm else out[0]


HALO = 16


def _conv3(ext, w_ref, b_ref):
    return w_ref[0:1, :] * pltpu.roll(ext, 2, 0) + w_ref[1:2, :] * pltpu.roll(ext, 1, 0) + w_ref[2:3, :] * ext + b_ref[...]


def _ffn_up_mid(hf, w_up_t, cw, cb, name="ffn_up_mid", tm=512, tc=1408):
    T, Dm = hf.shape
    Fh = w_up_t.shape[0] // 2
    tm = min(tm, T)
    nj, ni = Fh // tc, T // tm

    rc = min(512, tm)

    def body(h_ref, wg_ref, wu_ref, cwg_ref, cwu_ref, cbg_ref, cbu_ref, zg_ref, zu_ref, a_ref, eg_ref, eu_ref, ng_ref, nu_ref):
        @pl.when(pl.program_id(1) == 0)
        def _():
            eg_ref[...] = jnp.zeros_like(eg_ref)
            eu_ref[...] = jnp.zeros_like(eu_ref)

        for r in range(tm // rc):
            ext = slice(r * rc, r * rc + HALO + rc)
            cg = _conv3(eg_ref[ext, :].astype(F32), cwg_ref, cbg_ref)[HALO:]
            cu = _conv3(eu_ref[ext, :].astype(F32), cwu_ref, cbu_ref)[HALO:]
            a_ref[r * rc:(r + 1) * rc, :] = _gelu(cg.astype(BF16)) * cu.astype(BF16)
            h = h_ref[r * rc:(r + 1) * rc, :]
            for w_ref, z_ref, n_ref in ((wg_ref, zg_ref, ng_ref), (wu_ref, zu_ref, nu_ref)):
                z = _dot(h, w_ref[...], NT).astype(BF16)
                z_ref[r * rc:(r + 1) * rc, :] = z
                n_ref[r * rc:(r + 1) * rc, :] = z
        for e_ref, n_ref in ((eg_ref, ng_ref), (eu_ref, nu_ref)):
            e_ref[0:HALO, :] = e_ref[tm:tm + HALO, :]
            e_ref[HALO:HALO + tm, :] = n_ref[...]

    last = ni - 1
    z_spec = pl.BlockSpec((tm, tc), lambda j, i: (jnp.minimum(i, last), j))
    return pl.pallas_call(
        body,
        grid=(nj, ni + 1),
        in_specs=[pl.BlockSpec((tm, Dm), lambda j, i: (jnp.minimum(i, last), 0)),
                  pl.BlockSpec((tc, Dm), lambda j, i: (j, 0)), pl.BlockSpec((tc, Dm), lambda j, i: (nj + j, 0)),
                  pl.BlockSpec((3, tc), lambda j, i: (0, j)), pl.BlockSpec((3, tc), lambda j, i: (0, nj + j)),
                  pl.BlockSpec((1, tc), lambda j, i: (0, j)), pl.BlockSpec((1, tc), lambda j, i: (0, nj + j))],
        out_specs=[z_spec, z_spec, pl.BlockSpec((tm, tc), lambda j, i: (jnp.maximum(i - 1, 0), j))],
        out_shape=[jax.ShapeDtypeStruct((T, Fh), BF16)] * 3,
        scratch_shapes=[pltpu.VMEM((HALO + tm, tc), BF16), pltpu.VMEM((HALO + tm, tc), BF16),
                        pltpu.VMEM((tm, tc), BF16), pltpu.VMEM((tm, tc), BF16)],
        compiler_params=_cp("arbitrary", "arbitrary"),
        name=name,
    )(hf, w_up_t, w_up_t, cw, cw, cb, cb)


def _ffn_mid_bwd(zg, zu, da, cw, cb, comm=None, name="ffn_mid_bwd", tm=1024, tc=256):
    T, Fh = zg.shape
    tm = min(tm, T)
    nj = Fh // tc
    hb = tm // HALO
    nhb = T // HALO

    nc = comm.n if comm else 0
    ni = T // tm

    def body(*refs):
        zg_ref, zu_ref, zgp_ref, zup_ref, zgn_ref, zun_ref, da_ref, dan_ref, wg_ref, wu_ref, bg_ref, bu_ref = refs[:12]
        cin = refs[12:12 + nc]
        dzg_ref, dzu_ref, dwg_ref, dwu_ref, dbg_ref, dbu_ref = refs[12 + nc:18 + nc]
        cout = refs[18 + nc:18 + 2 * nc]
        csem = refs[18 + 2 * nc:]
        i = pl.program_id(1)
        step = pl.program_id(0) * ni + i
        _host_comm(comm, "early", step, nj * ni, cin, cout, csem)
        first = i == 0
        last = i == ni - 1

        @pl.when(first)
        def _():
            dwg_ref[...] = jnp.zeros_like(dwg_ref)
            dwu_ref[...] = jnp.zeros_like(dwu_ref)
            dbg_ref[...] = jnp.zeros_like(dbg_ref)
            dbu_ref[...] = jnp.zeros_like(dbu_ref)

        def ext_of(p_ref, c_ref, n_ref):
            p = jnp.where(first, 0.0, p_ref[...].astype(F32))
            return jnp.concatenate([p, c_ref[...].astype(F32), n_ref[...].astype(F32)], axis=0)

        zge = ext_of(zgp_ref, zg_ref, zgn_ref)
        zue = ext_of(zup_ref, zu_ref, zun_ref)
        dan = jnp.where(last, 0.0, dan_ref[...].astype(F32))
        dae = jnp.concatenate([jnp.zeros((HALO, tc), F32), da_ref[...].astype(F32), dan], axis=0)
        zg1, zg2 = pltpu.roll(zge, 1, 0), pltpu.roll(zge, 2, 0)
        zu1, zu2 = pltpu.roll(zue, 1, 0), pltpu.roll(zue, 2, 0)
        cg = wg_ref[0:1, :] * zg2 + wg_ref[1:2, :] * zg1 + wg_ref[2:3, :] * zge + bg_ref[...]
        cu = wu_ref[0:1, :] * zu2 + wu_ref[1:2, :] * zu1 + wu_ref[2:3, :] * zue + bu_ref[...]
        gel, dgel = (t.astype(F32) for t in _gelu_and_grad(cg.astype(BF16)))
        dcg = dae * cu * dgel
        dcu = dae * gel
        lo, hi = HALO, HALO + tm

        def back(dc, taps, w_ref, dz_ref, dw_ref, db_ref):
            n = dc.shape[0]
            dz = w_ref[2:3, :] * dc + w_ref[1:2, :] * pltpu.roll(dc, n - 1, 0) + w_ref[0:1, :] * pltpu.roll(dc, n - 2, 0)
            dz_ref[...] = dz[lo:hi].astype(dz_ref.dtype)
            dcc = dc[lo:hi]
            db_ref[...] += jnp.sum(dcc, axis=0, keepdims=True)
            for k, tap in enumerate(taps):
                dw_ref[k:k + 1, :] += jnp.sum(dcc * tap[lo:hi], axis=0, keepdims=True)

        back(dcg, (zg2, zg1, zge), wg_ref, dzg_ref, dwg_ref, dbg_ref)
        back(dcu, (zu2, zu1, zue), wu_ref, dzu_ref, dwu_ref, dbu_ref)
        _host_comm(comm, "late", step, nj * ni, cin, cout, csem)

    cur = pl.BlockSpec((tm, tc), lambda j, i: (i, j))
    prev = pl.BlockSpec((HALO, tc), lambda j, i: (jnp.maximum(i * hb - 1, 0), j))
    nxt = pl.BlockSpec((HALO, tc), lambda j, i: (jnp.minimum((i + 1) * hb, nhb - 1), j))
    wg = pl.BlockSpec((3, tc), lambda j, i: (0, j))
    wu = pl.BlockSpec((3, tc), lambda j, i: (0, j + nj))
    bg = pl.BlockSpec((1, tc), lambda j, i: (0, j))
    bu = pl.BlockSpec((1, tc), lambda j, i: (0, j + nj))
    dw = pl.BlockSpec((3, tc), lambda j, i: (0, j))
    db = pl.BlockSpec((1, tc), lambda j, i: (0, j))
    out = pl.pallas_call(
        body,
        grid=(nj, ni),
        in_specs=[cur, cur, prev, prev, nxt, nxt, cur, nxt, wg, wu, bg, bu] + [ANY] * nc,
        out_specs=[cur, cur, dw, dw, db, db] + [ANY] * nc,
        out_shape=[jax.ShapeDtypeStruct((T, Fh), BF16), jax.ShapeDtypeStruct((T, Fh), BF16),
                   jax.ShapeDtypeStruct((3, Fh), F32), jax.ShapeDtypeStruct((3, Fh), F32),
                   jax.ShapeDtypeStruct((1, Fh), F32), jax.ShapeDtypeStruct((1, Fh), F32)] + (comm.out_shape if comm else []),
        scratch_shapes=comm.scratch if comm else [],
        compiler_params=_cp("arbitrary", "arbitrary"),
        name=name,
    )(zg, zu, zg, zu, zg, zu, da, da, cw, cw, cb, cb, *(comm.arrays if comm else []))
    dzg, dzu, dwg, dwu, dbg, dbu = out[:6]
    return dzg, dzu, jnp.concatenate([dwg, dwu], axis=1), jnp.concatenate([dbg, dbu], axis=1), list(out[6:])


def _sgu_mask():
    r = lax.broadcasted_iota(jnp.int32, (SGU_BLOCK, SGU_BLOCK), 0)
    c = lax.broadcasted_iota(jnp.int32, (SGU_BLOCK, SGU_BLOCK), 1)
    return (c < CHUNK) | (r >= CHUNK)


def _sgu_fwd(zz, ln_g, ln_b, w_s, b_s, name="sgu_fwd", tm=256):
    T = zz.shape[0]
    tm = min(tm, T)
    W = SGU_WIDTH

    def body(zu_ref, zv_ref, g_ref, b_ref, ws_ref, bs_ref, y_ref):
        u = _gelu(zu_ref[...]).astype(F32)
        v = _gelu(zv_ref[...]).astype(F32)
        mu = jnp.mean(v, axis=-1, keepdims=True)
        xc = v - mu
        rstd = lax.rsqrt(jnp.mean(xc * xc, axis=-1, keepdims=True) + EPS)
        vn = (xc * rstd * g_ref[...] + b_ref[...]).astype(BF16)
        mask = _sgu_mask()
        for g in range(SGU_GROUPS):
            wm = jnp.where(mask, ws_ref[g], 0.0).astype(BF16)
            cs = slice(g * SGU_GW, (g + 1) * SGU_GW)
            for blk in range(tm // SGU_BLOCK):
                rs = slice(blk * SGU_BLOCK, (blk + 1) * SGU_BLOCK)
                mixed = _dot(wm, vn[rs, cs], NN) + bs_ref[g]
                y_ref[rs, cs] = (u[rs, cs] * mixed).astype(y_ref.dtype)

    return pl.pallas_call(
        body,
        grid=(T // tm,),
        in_specs=[pl.BlockSpec((tm, W), lambda i: (i, 0)), pl.BlockSpec((tm, W), lambda i: (i, 1)),
                  pl.BlockSpec((1, W), lambda i: (0, 0)), pl.BlockSpec((1, W), lambda i: (0, 0)),
                  pl.BlockSpec((SGU_GROUPS, SGU_BLOCK, SGU_BLOCK), lambda i: (0, 0, 0)),
                  pl.BlockSpec((SGU_GROUPS, SGU_BLOCK, 1), lambda i: (0, 0, 0))],
        out_specs=pl.BlockSpec((tm, W), lambda i: (i, 0)),
        out_shape=jax.ShapeDtypeStruct((T, W), BF16),
        compiler_params=_cp("parallel"),
        name=name,
    )(zz, zz, ln_g.reshape(1, W), ln_b.reshape(1, W), w_s, b_s.reshape(SGU_GROUPS, SGU_BLOCK, 1))


def _sgu_bwd(zz, dy, ln_g, ln_b, w_s, b_s, name="sgu_bwd", tm=256):
    T = zz.shape[0]
    tm = min(tm, T)
    W = SGU_WIDTH

    def body(zu_ref, zv_ref, dy_ref, g_ref, b_ref, ws_ref, bs_ref, dzz_ref, dws_ref, dbs_ref, dg_ref, db_ref, dvn_ref):
        i = pl.program_id(0)

        @pl.when(i == 0)
        def _():
            dws_ref[...] = jnp.zeros_like(dws_ref)
            dbs_ref[...] = jnp.zeros_like(dbs_ref)
            dg_ref[...] = jnp.zeros_like(dg_ref)
            db_ref[...] = jnp.zeros_like(db_ref)

        u, du_dz = (t.astype(F32) for t in _gelu_and_grad(zu_ref[...]))
        v, dv_dz = (t.astype(F32) for t in _gelu_and_grad(zv_ref[...]))
        mu = jnp.mean(v, axis=-1, keepdims=True)
        xc = v - mu
        rstd = lax.rsqrt(jnp.mean(xc * xc, axis=-1, keepdims=True) + EPS)
        xhat = xc * rstd
        gv = g_ref[...]
        vn = (xhat * gv + b_ref[...]).astype(BF16)
        dyv = dy_ref[...].astype(F32)
        mask = _sgu_mask()
        for g in range(SGU_GROUPS):
            wm = jnp.where(mask, ws_ref[g], 0.0).astype(BF16)
            cs = slice(g * SGU_GW, (g + 1) * SGU_GW)
            dw_acc = jnp.zeros((SGU_BLOCK, SGU_BLOCK), F32)
            db_acc = jnp.zeros((SGU_BLOCK, 1), F32)
            for blk in range(tm // SGU_BLOCK):
                rs = slice(blk * SGU_BLOCK, (blk + 1) * SGU_BLOCK)
                vn_bg = vn[rs, cs]
                mixed = _dot(wm, vn_bg, NN) + bs_ref[g]
                dy_bg = dyv[rs, cs]
                dmixed = dy_bg * u[rs, cs]
                dmb = dmixed.astype(BF16)
                dw_acc += _dot(dmb, vn_bg, NT)
                db_acc += jnp.sum(dmixed, axis=1, keepdims=True)
                dvn_ref[rs, cs] = _dot(wm, dmb, TN)
                dzz_ref[rs, cs] = (dy_bg * mixed * du_dz[rs, cs]).astype(dzz_ref.dtype)
            dws_ref[g] += jnp.where(mask, dw_acc, 0.0)
            dbs_ref[g] += db_acc
        dvn = dvn_ref[...]
        dg_ref[...] += jnp.sum(dvn * xhat, axis=0, keepdims=True)
        db_ref[...] += jnp.sum(dvn, axis=0, keepdims=True)
        dxh = dvn * gv
        m1 = jnp.mean(dxh, axis=-1, keepdims=True)
        m2 = jnp.mean(dxh * xhat, axis=-1, keepdims=True)
        dv = rstd * (dxh - m1 - xhat * m2)
        dzz_ref[:, W:] = (dv * dv_dz).astype(dzz_ref.dtype)

    vec = pl.BlockSpec((1, W), lambda i: (0, 0))
    ws_spec = pl.BlockSpec((SGU_GROUPS, SGU_BLOCK, SGU_BLOCK), lambda i: (0, 0, 0))
    bs_spec = pl.BlockSpec((SGU_GROUPS, SGU_BLOCK, 1), lambda i: (0, 0, 0))
    return pl.pallas_call(
        body,
        grid=(T // tm,),
        in_specs=[pl.BlockSpec((tm, W), lambda i: (i, 0)), pl.BlockSpec((tm, W), lambda i: (i, 1)),
                  pl.BlockSpec((tm, W), lambda i: (i, 0)), vec, vec, ws_spec, bs_spec],
        out_specs=[pl.BlockSpec((tm, 2 * W), lambda i: (i, 0)), ws_spec, bs_spec, vec, vec],
        out_shape=[jax.ShapeDtypeStruct((T, 2 * W), BF16),
                   jax.ShapeDtypeStruct((SGU_GROUPS, SGU_BLOCK, SGU_BLOCK), F32),
                   jax.ShapeDtypeStruct((SGU_GROUPS, SGU_BLOCK, 1), F32),
                   jax.ShapeDtypeStruct((1, W), F32), jax.ShapeDtypeStruct((1, W), F32)],
        scratch_shapes=[pltpu.VMEM((tm, W), F32)],
        compiler_params=_cp("arbitrary"),
        name=name,
    )(zz, zz, dy, ln_g.reshape(1, W), ln_b.reshape(1, W), w_s, b_s.reshape(SGU_GROUPS, SGU_BLOCK, 1))


RET_TR = 256
RET_BLK = 256
QK_SCALE = RET_QK_DIM ** -0.5


def _ret_tables(T):
    half = RET_QK_DIM // 2
    inv = 1.0 / (10000.0 ** jnp.linspace(0.0, 1.0, half, dtype=F32))
    inv2 = jnp.concatenate([inv, inv])[None, :]
    sgn = jnp.concatenate([-jnp.ones((half,), F32), jnp.ones((half,), F32)])[None, :]
    tr = min(RET_TR, T)

    def trig(pos):
        ang = pos.astype(F32)[:, None] * inv2
        return jnp.stack([jnp.cos(ang), jnp.sin(ang), sgn * jnp.sin(ang)])

    tile_tab = jnp.pad(trig(jnp.arange(T // tr) * tr).transpose(1, 0, 2), ((0, 0), (0, 5), (0, 0)))
    row_tab = trig(jnp.arange(tr))
    log_g = jnp.log1p(-jnp.exp2(-5.0 - jnp.arange(RET_HEADS, dtype=F32)))
    idx = jnp.arange(RET_BLK, dtype=F32)
    dist = idx[:, None] - idx[None, :]
    cq, ck = jnp.arange(RET_BLK)[:, None] // CHUNK, jnp.arange(RET_BLK)[None, :] // CHUNK
    expo = jnp.where(ck == cq, jnp.abs(dist), dist)
    d_blk = jnp.where((ck <= cq)[None], jnp.exp(log_g[:, None, None] * expo[None]), 0.0)
    k_dec = jnp.exp(log_g[:, None] * (RET_BLK - 1 - idx)[None, :])[:, :, None]
    q_dec = jnp.exp(log_g[:, None] * (idx + 1.0)[None, :])[:, :, None]
    c_dec = jnp.exp(log_g * RET_BLK)[:, None, None]
    return tile_tab, row_tab, d_blk, q_dec, k_dec, c_dec


def _rot(x, c, s):
    return x * c + pltpu.roll(x, RET_QK_DIM // 2, 1) * s


def _rot_tables(tt_ref, rt_ref):
    ca, sa, ga = tt_ref[0:1, :], tt_ref[1:2, :], tt_ref[2:3, :]
    cb, sb, gb = rt_ref[0], rt_ref[1], rt_ref[2]
    return ca * cb - sa * sb, ga * cb + ca * gb


def _ret_specs(tr, rev, nb):
    ix = (lambda n: nb - 1 - n) if rev else (lambda n: n)
    tt = pl.BlockSpec((None, 8, RET_QK_DIM), lambda n: (ix(n), 0, 0))
    rt = pl.BlockSpec((3, tr, RET_QK_DIM), lambda n: (0, 0, 0))
    dm = pl.BlockSpec((RET_HEADS, RET_BLK, RET_BLK), lambda n: (0, 0, 0))
    dv = pl.BlockSpec((RET_HEADS, RET_BLK, 1), lambda n: (0, 0, 0))
    dc = pl.BlockSpec((RET_HEADS, 1, 1), lambda n: (0, 0, 0))
    return ix, [tt, rt, dm, dv, dv, dc]


def _ret_fwd(z_a, tables, name="ret_fwd"):
    T = z_a.shape[0]
    tr = min(RET_TR, T)
    cpb = tr // RET_BLK
    nb = T // tr
    QW, VW = RET_HEADS * RET_QK_DIM, RET_HEADS * RET_V_DIM
    ix, tab_specs = _ret_specs(tr, False, nb)

    def body(z_ref, tt_ref, rt_ref, dm_ref, qd_ref, kd_ref, cd_ref, y_ref, st_ref, state):
        @pl.when(pl.program_id(0) == 0)
        def _():
            state[...] = jnp.zeros_like(state)

        rot_c, rot_s = _rot_tables(tt_ref, rt_ref)
        for c in range(cpb):
            for h in range(RET_HEADS):
                rs = slice(c * RET_BLK, (c + 1) * RET_BLK)
                cc, ss = rot_c[rs, :], rot_s[rs, :]
                q = z_ref[rs, h * RET_QK_DIM:(h + 1) * RET_QK_DIM].astype(F32)
                k = z_ref[rs, QW + h * RET_QK_DIM:QW + (h + 1) * RET_QK_DIM].astype(F32)
                v = z_ref[rs, 2 * QW + h * RET_V_DIM:2 * QW + (h + 1) * RET_V_DIM]
                gt = z_ref[rs, 2 * QW + VW + h * RET_V_DIM:2 * QW + VW + (h + 1) * RET_V_DIM].astype(F32)
                qr = _rot(q, cc, ss)
                kr = _rot(k, cc, ss) * QK_SCALE
                s_old = state[h]
                sb = s_old.astype(BF16)
                st_ref[c, h] = sb
                s = _dot(qr.astype(BF16), kr.astype(BF16), NT) * dm_ref[h]
                o = _dot(s.astype(BF16), v, NN) + _dot((qr * qd_ref[h]).astype(BF16), sb, NN)
                state[h] = s_old * cd_ref[h] + _dot((kr * kd_ref[h]).astype(BF16), v, TN)
                mu = jnp.mean(o, axis=-1, keepdims=True)
                oc = o - mu
                rn = oc * lax.rsqrt(jnp.mean(oc * oc, axis=-1, keepdims=True) + EPS)
                silu = gt / (1.0 + jnp.exp(-gt))
                y_ref[rs, h * RET_V_DIM:(h + 1) * RET_V_DIM] = (silu * rn).astype(y_ref.dtype)

    return pl.pallas_call(
        body,
        grid=(nb,),
        in_specs=[pl.BlockSpec((tr, RET_W), lambda n: (n, 0))] + tab_specs,
        out_specs=[pl.BlockSpec((tr, VW), lambda n: (n, 0)),
                   pl.BlockSpec((cpb, RET_HEADS, RET_QK_DIM, RET_V_DIM), lambda n: (n, 0, 0, 0))],
        out_shape=[jax.ShapeDtypeStruct((T, Y_COLS), BF16),
                   jax.ShapeDtypeStruct((T // RET_BLK, RET_HEADS, RET_QK_DIM, RET_V_DIM), BF16)],
        scratch_shapes=[pltpu.VMEM((RET_HEADS, RET_QK_DIM, RET_V_DIM), F32)],
        compiler_params=_cp("arbitrary"),
        name=name,
    )(z_a, *tables)


def _ret_bwd(z_a, dy, states, tables, name="ret_bwd"):
    T = z_a.shape[0]
    tr = min(RET_TR, T)
    cpb = tr // RET_BLK
    nb = T // tr
    QW, VW = RET_HEADS * RET_QK_DIM, RET_HEADS * RET_V_DIM
    ix, tab_specs = _ret_specs(tr, True, nb)

    def body(z_ref, dy_ref, st_ref, tt_ref, rt_ref, dm_ref, qd_ref, kd_ref, cd_ref, dz_ref, dstate):
        @pl.when(pl.program_id(0) == 0)
        def _():
            dstate[...] = jnp.zeros_like(dstate)

        rot_c, rot_s = _rot_tables(tt_ref, rt_ref)
        for c in reversed(range(cpb)):
            for h in range(RET_HEADS):
                rs = slice(c * RET_BLK, (c + 1) * RET_BLK)
                cc, ss = rot_c[rs, :], rot_s[rs, :]
                q = z_ref[rs, h * RET_QK_DIM:(h + 1) * RET_QK_DIM].astype(F32)
                k = z_ref[rs, QW + h * RET_QK_DIM:QW + (h + 1) * RET_QK_DIM].astype(F32)
                v = z_ref[rs, 2 * QW + h * RET_V_DIM:2 * QW + (h + 1) * RET_V_DIM]
                gt = z_ref[rs, 2 * QW + VW + h * RET_V_DIM:2 * QW + VW + (h + 1) * RET_V_DIM].astype(F32)
                dyv = dy_ref[rs, h * RET_V_DIM:(h + 1) * RET_V_DIM].astype(F32)
                dmat, qd, kd = dm_ref[h], qd_ref[h], kd_ref[h]
                qr = _rot(q, cc, ss)
                kr = _rot(k, cc, ss) * QK_SCALE
                qrb, krb = qr.astype(BF16), kr.astype(BF16)
                sb = st_ref[c, h]
                sd = (_dot(qrb, krb, NT) * dmat).astype(BF16)
                qdb = (qr * qd).astype(BF16)
                kdb = (kr * kd).astype(BF16)
                o = _dot(sd, v, NN) + _dot(qdb, sb, NN)
                mu = jnp.mean(o, axis=-1, keepdims=True)
                oc = o - mu
                rstd = lax.rsqrt(jnp.mean(oc * oc, axis=-1, keepdims=True) + EPS)
                rn = oc * rstd
                sg = 1.0 / (1.0 + jnp.exp(-gt))
                dgt = dyv * rn * (sg * (1.0 + gt * (1.0 - sg)))
                drn = dyv * (gt * sg)
                do = rstd * (drn - jnp.mean(drn, axis=-1, keepdims=True) - rn * jnp.mean(drn * rn, axis=-1, keepdims=True))
                dob = do.astype(BF16)
                dsn = dstate[h]
                dsnb = dsn.astype(BF16)
                ds_raw = (_dot(dob, v, NT) * dmat).astype(BF16)
                dv = _dot(sd, dob, TN) + _dot(kdb, dsnb, NN)
                dqr = _dot(ds_raw, krb, NN) + qd * _dot(dob, sb, NT)
                dkr = (_dot(ds_raw, qrb, TN) + kd * _dot(v, dsnb, NT)) * QK_SCALE
                dstate[h] = dsn * cd_ref[h] + _dot(qdb, dob, TN)
                dq = dqr * cc + pltpu.roll(dqr * ss, RET_QK_DIM // 2, 1)
                dk = dkr * cc + pltpu.roll(dkr * ss, RET_QK_DIM // 2, 1)
                dz_ref[rs, h * RET_QK_DIM:(h + 1) * RET_QK_DIM] = dq.astype(dz_ref.dtype)
                dz_ref[rs, QW + h * RET_QK_DIM:QW + (h + 1) * RET_QK_DIM] = dk.astype(dz_ref.dtype)
                dz_ref[rs, 2 * QW + h * RET_V_DIM:2 * QW + (h + 1) * RET_V_DIM] = dv.astype(dz_ref.dtype)
                dz_ref[rs, 2 * QW + VW + h * RET_V_DIM:2 * QW + VW + (h + 1) * RET_V_DIM] = dgt.astype(dz_ref.dtype)

    return pl.pallas_call(
        body,
        grid=(nb,),
        in_specs=[pl.BlockSpec((tr, RET_W), lambda n: (ix(n), 0)),
                  pl.BlockSpec((tr, VW), lambda n: (ix(n), 0)),
                  pl.BlockSpec((cpb, RET_HEADS, RET_QK_DIM, RET_V_DIM), lambda n: (ix(n), 0, 0, 0))] + tab_specs,
        out_specs=pl.BlockSpec((tr, RET_W), lambda n: (ix(n), 0)),
        out_shape=jax.ShapeDtypeStruct((T, RET_W), BF16),
        scratch_shapes=[pltpu.VMEM((RET_HEADS, RET_QK_DIM, RET_V_DIM), F32)],
        compiler_params=_cp("arbitrary"),
        name=name,
    )(z_a, dy, states, *tables)


ATT_TQ = 256
ATT_CPB = ATT_TQ // CHUNK
ATT_SCALE = ATT_HEAD_DIM ** -0.5


ATT_WIN = 3 * ATT_TQ
ATT_NB = CHUNK * ATT_BAND


def _rel_index():
    i = np.arange(CHUNK)[:, None]
    j = np.arange(ATT_BAND)[None, :]
    rel = np.clip(i + ATT_PAST * CHUNK - j, -MAX_REL, MAX_REL) + MAX_REL
    return jnp.asarray(rel.reshape(1, ATT_NB).astype(np.int32))


def _split3(x):
    hi = x.astype(BF16)
    r1 = x - hi.astype(F32)
    mid = r1.astype(BF16)
    lo = (r1 - mid.astype(F32)).astype(BF16)
    return hi, mid, lo


REL_TILE = 4608


def _bias_expand(rel_bias, name="bias_expand"):
    H = rel_bias.shape[0]
    n = ATT_NB
    padded = jnp.pad(rel_bias, ((0, 0), (0, N_REL_PAD - N_REL)))

    def body(rb_ref, idx_ref, o_ref):
        onehot = (lax.broadcasted_iota(jnp.int32, (N_REL_PAD, REL_TILE), 0) == idx_ref[...]).astype(BF16)
        hi, mid, lo = _split3(rb_ref[...])
        o_ref[...] = _dot(hi, onehot, NN) + _dot(mid, onehot, NN) + _dot(lo, onehot, NN)

    out = pl.pallas_call(
        body,
        grid=(n // REL_TILE,),
        in_specs=[pl.BlockSpec((H, N_REL_PAD), lambda t: (0, 0)), pl.BlockSpec((1, REL_TILE), lambda t: (0, t))],
        out_specs=pl.BlockSpec((H, REL_TILE), lambda t: (0, t)),
        out_shape=jax.ShapeDtypeStruct((H, n), F32),
        compiler_params=_cp("parallel"),
        name=name,
    )(padded, _rel_index())
    return out.reshape(H, CHUNK, ATT_BAND)


def _bias_tile(band, name="bias_tile"):
    H = band.shape[0]
    padded = jnp.pad(band, ((0, 0), (0, 0), (0, ATT_WIN - ATT_BAND)), constant_values=NEG_INF)

    def body(b_ref, o_ref):
        b = b_ref[...]
        col = lax.broadcasted_iota(jnp.int32, (CHUNK, ATT_WIN), 1)
        keep = col >= (2 - pl.program_id(0)) * ATT_TQ
        for a in range(ATT_CPB):
            o_ref[a * CHUNK:(a + 1) * CHUNK, :] = jnp.where(keep, pltpu.roll(b, a * CHUNK, 1) if a else b, NEG_INF)

    return pl.pallas_call(
        body,
        grid=(3, H),
        in_specs=[pl.BlockSpec((None, CHUNK, ATT_WIN), lambda v, h: (h, 0, 0))],
        out_specs=pl.BlockSpec((None, None, ATT_TQ, ATT_WIN), lambda v, h: (v, h, 0, 0)),
        out_shape=jax.ShapeDtypeStruct((3, H, ATT_TQ, ATT_WIN), F32),
        compiler_params=_cp("parallel", "parallel"),
        name=name,
    )(padded)


def _bias_untile(dtile, name="bias_untile"):
    H = dtile.shape[0]

    def body(d_ref, o_ref):
        acc = d_ref[0:CHUNK, :]
        for a in range(1, ATT_CPB):
            acc = acc + pltpu.roll(d_ref[a * CHUNK:(a + 1) * CHUNK, :], ATT_WIN - a * CHUNK, 1)
        o_ref[...] = acc

    out = pl.pallas_call(
        body,
        grid=(H,),
        in_specs=[pl.BlockSpec((None, ATT_TQ, ATT_WIN), lambda h: (h, 0, 0))],
        out_specs=pl.BlockSpec((None, CHUNK, ATT_WIN), lambda h: (h, 0, 0)),
        out_shape=jax.ShapeDtypeStruct((H, CHUNK, ATT_WIN), F32),
        compiler_params=_cp("parallel"),
        name=name,
    )(dtile)
    return out[:, :, :ATT_BAND]


def _bias_reduce(dbias, name="bias_reduce"):
    H = dbias.shape[0]
    n = ATT_NB

    def body(db_ref, idx_ref, o_ref):
        @pl.when(pl.program_id(0) == 0)
        def _():
            o_ref[...] = jnp.zeros_like(o_ref)

        onehot = (lax.broadcasted_iota(jnp.int32, (N_REL_PAD, REL_TILE), 0) == idx_ref[...]).astype(BF16)
        hi, mid, lo = _split3(db_ref[...])
        o_ref[...] += _dot(hi, onehot, NT) + _dot(mid, onehot, NT) + _dot(lo, onehot, NT)

    out = pl.pallas_call(
        body,
        grid=(n // REL_TILE,),
        in_specs=[pl.BlockSpec((H, REL_TILE), lambda t: (0, t)), pl.BlockSpec((1, REL_TILE), lambda t: (0, t))],
        out_specs=pl.BlockSpec((H, N_REL_PAD), lambda t: (0, 0)),
        out_shape=jax.ShapeDtypeStruct((H, N_REL_PAD), F32),
        compiler_params=_cp("arbitrary"),
        name=name,
    )(dbias.reshape(H, n), _rel_index())
    return out[:, :N_REL]


def _att_probs(q, kwin, bias):
    s = _dot(q, kwin, NT) + bias
    e = jnp.exp(s - jnp.max(s, axis=-1, keepdims=True))
    return e * (1.0 / jnp.sum(e, axis=-1, keepdims=True))


ATT_PAIR = 2 * ATT_HEAD_DIM
ATT_NP = ATT_HEADS // 2
ATT_QW = ATT_HEADS * ATT_HEAD_DIM
Y_COLS = RET_HEADS * RET_V_DIM + ATT_QW


def _att_specs(tq, nq, clip_q, q_col0):
    cb = ATT_QW // ATT_PAIR
    qi = (lambda p, m: (jnp.minimum(m, nq - 1), q_col0 + p)) if clip_q else (lambda p, m: (m, q_col0 + p))
    q = pl.BlockSpec((tq, ATT_PAIR), qi)

    def win(col0):
        return [pl.BlockSpec((tq, ATT_PAIR), functools.partial(lambda p, m, back: (jnp.clip(m - back, 0, nq - 1), col0 + p), back=b))
                for b in (2, 1, 0)]

    bias = pl.BlockSpec((None, 2, ATT_TQ, ATT_WIN), lambda p, m: (jnp.minimum(m, 2), p, 0, 0))
    return q, win(cb), win(2 * cb), bias


def _head_masks(rows):
    lane = lax.broadcasted_iota(jnp.int32, (rows, ATT_PAIR), 1)
    return lane < ATT_HEAD_DIM


def _att_fwd(z_b, bias, y, comm=None, name="att_fwd"):
    T = z_b.shape[0]
    tq = ATT_TQ
    nq = T // tq
    qs, kwin, vwin, bs = _att_specs(tq, nq, False, 0)
    nc = comm.n if comm else 0
    total = ATT_NP * nq

    def body(*refs):
        q_ref, k0, k1, k2, v0, v1, v2, b_ref = refs[:8]
        cin = refs[9:9 + nc]
        o_ref = refs[9 + nc]
        cout = refs[10 + nc:10 + 2 * nc]
        csem = refs[10 + 2 * nc:]
        m = pl.program_id(1)
        step = pl.program_id(0) * nq + m
        _host_comm(comm, "early", step, total, cin, cout, csem)
        kw = jnp.concatenate([k0[...], k1[...], k2[...]], axis=0)
        vw = jnp.concatenate([v0[...], v1[...], v2[...]], axis=0)
        q2 = q_ref[...] * ATT_SCALE
        even = _head_masks(tq)
        outs = []
        for hh in range(2):
            qm = jnp.where(even if hh == 0 else ~even, q2, jnp.zeros_like(q2))
            p = _att_probs(qm, kw, b_ref[hh])
            outs.append(_dot(p.astype(BF16), vw, NN))
        o_ref[...] = jnp.where(even, outs[0], outs[1]).astype(o_ref.dtype)
        _host_comm(comm, "late", step, total, cin, cout, csem)

    y_cb = (Y_COLS - ATT_QW) // ATT_PAIR
    out = pl.pallas_call(
        body,
        grid=(ATT_NP, nq),
        in_specs=[qs] + kwin + vwin + [bs, ANY] + [ANY] * nc,
        out_specs=[pl.BlockSpec((tq, ATT_PAIR), lambda p, m: (m, y_cb + p))] + [ANY] * nc,
        out_shape=[jax.ShapeDtypeStruct((T, Y_COLS), BF16)] + (comm.out_shape if comm else []),
        scratch_shapes=comm.scratch if comm else [],
        input_output_aliases={8: 0},
        compiler_params=_cp("arbitrary", "arbitrary"),
        name=name,
    )(z_b, z_b, z_b, z_b, z_b, z_b, z_b, bias, y, *(comm.arrays if comm else []))
    return out[0], list(out[1:])


def _att_bwd(z_b, bias, dy, comm=None, name="att_bwd"):
    T = z_b.shape[0]
    tq = ATT_TQ
    nq = T // tq
    y_cb = (Y_COLS - ATT_QW) // ATT_PAIR
    qs, kwin, vwin, bs = _att_specs(tq, nq, True, 0)
    dos = _att_specs(tq, nq, True, y_cb)[0]
    kv_out = pl.BlockSpec((tq, ATT_PAIR), lambda p, m: (jnp.maximum(m - 2, 0), p))
    W3 = 3 * tq
    nc = comm.n if comm else 0
    total = ATT_NP * (nq + 2)

    def body(*refs):
        q_ref, k0, k1, k2, v0, v1, v2, b_ref, do_ref = refs[:9]
        cin = refs[9:9 + nc]
        dq_ref, dk_ref, dv_ref, db_ref = refs[9 + nc:13 + nc]
        cout = refs[13 + nc:13 + 2 * nc]
        dkc, dvc, dkw, dvw = refs[13 + 2 * nc:17 + 2 * nc]
        csem = refs[17 + 2 * nc:]
        m = pl.program_id(1)
        step = pl.program_id(0) * (nq + 2) + m
        _host_comm(comm, "early", step, total, cin, cout, csem)

        @pl.when(m == 0)
        def _():
            dkc[...] = jnp.zeros_like(dkc)
            dvc[...] = jnp.zeros_like(dvc)
            db_ref[...] = jnp.zeros_like(db_ref)

        @pl.when(m >= nq)
        def _():
            dkw[...] = jnp.zeros_like(dkw)
            dvw[...] = jnp.zeros_like(dvw)

        @pl.when(m < nq)
        def _():
            kw = jnp.concatenate([k0[...], k1[...], k2[...]], axis=0)
            vw = jnp.concatenate([v0[...], v1[...], v2[...]], axis=0)
            q2, do2 = q_ref[...] * ATT_SCALE, do_ref[...]
            even = _head_masks(tq)
            dqs, dks, dvs = [], [], []
            for hh in range(2):
                mine = even if hh == 0 else ~even
                p = _att_probs(jnp.where(mine, q2, jnp.zeros_like(q2)), kw, b_ref[hh])
                dp = _dot(jnp.where(mine, do2, jnp.zeros_like(do2)), vw, NT)
                ds = p * (dp - jnp.sum(dp * p, axis=-1, keepdims=True))
                db_ref[hh] += ds
                dsb = ds.astype(BF16)
                dqs.append(_dot(dsb, kw, NN))
                dks.append(_dot(dsb, q2, TN))
                dvs.append(_dot(p.astype(BF16), do2, TN))
            even_w = _head_masks(W3)
            dq_ref[...] = (jnp.where(even, dqs[0], dqs[1]) * ATT_SCALE).astype(dq_ref.dtype)
            dkw[...] = jnp.where(even_w, dks[0], dks[1])
            dvw[...] = jnp.where(even_w, dvs[0], dvs[1])

        dk_ref[...] = (dkc[0:tq, :] + dkw[0:tq, :]).astype(dk_ref.dtype)
        dv_ref[...] = (dvc[0:tq, :] + dvw[0:tq, :]).astype(dv_ref.dtype)
        dkc[0:tq, :] = dkc[tq:2 * tq, :] + dkw[tq:2 * tq, :]
        dvc[0:tq, :] = dvc[tq:2 * tq, :] + dvw[tq:2 * tq, :]
        dkc[tq:2 * tq, :] = dkw[2 * tq:W3, :]
        dvc[tq:2 * tq, :] = dvw[2 * tq:W3, :]
        _host_comm(comm, "late", step, total, cin, cout, csem)

    qo = pl.BlockSpec((tq, ATT_PAIR), lambda p, m: (jnp.minimum(m, nq - 1), p))
    dbs = pl.BlockSpec((2, ATT_TQ, ATT_WIN), lambda p, m: (p, 0, 0))
    hd = jax.ShapeDtypeStruct((T, ATT_QW), BF16)
    out = pl.pallas_call(
        body,
        grid=(ATT_NP, nq + 2),
        in_specs=[qs] + kwin + vwin + [bs, dos] + [ANY] * nc,
        out_specs=[qo, kv_out, kv_out, dbs] + [ANY] * nc,
        out_shape=[hd, hd, hd, jax.ShapeDtypeStruct((ATT_HEADS, ATT_TQ, ATT_WIN), F32)] + (comm.out_shape if comm else []),
        scratch_shapes=[pltpu.VMEM((2 * tq, ATT_PAIR), F32), pltpu.VMEM((2 * tq, ATT_PAIR), F32),
                        pltpu.VMEM((W3, ATT_PAIR), F32), pltpu.VMEM((W3, ATT_PAIR), F32)] + (comm.scratch if comm else []),
        compiler_params=_cp("arbitrary", "arbitrary"),
        name=name,
    )(z_b, z_b, z_b, z_b, z_b, z_b, z_b, bias, dy, *(comm.arrays if comm else []))
    return out[0], out[1], out[2], out[3], list(out[4:])


REST = ["ab_out", "c_in_t", "c_out", "up_t0", "up_t1", "down0", "down1"]


def _local_step(x, target, wt, small, rest_shards=None, overlap=False, hn0=None):
    T = x.shape[0]
    Fh = FFN_HIDDEN
    tables = _ret_tables(T)
    gw, gs, recv = {}, {}, {}
    wt = dict(wt)

    if hn0 is None:
        hn0 = _rms_fwd(x, small["attn_norm_g"][0], name="rms_fwd")
    z_a = _mm(hn0, wt["ab_in_t"][:RET_W], "nt", BF16, name="mm_ab_in_a")
    z_b = _mm(hn0, wt["ab_in_t"][RET_W:], "nt", BF16, name="mm_ab_in_b")
    y, states = _ret_fwd(z_a, tables)
    bias = _bias_tile(_bias_expand(small["rel_bias"]))
    y, rest = _att_fwd(z_b, bias, y, comm=_Comm("gather", rest_shards) if rest_shards is not None else None)
    if rest_shards is not None:
        full = dict(zip(REST, rest))
        wt.update(ab_out=full["ab_out"], c_in_t=full["c_in_t"], c_out=full["c_out"],
                  up_t=[full["up_t0"], full["up_t1"]], down=[full["down0"], full["down1"]])
    h1, hf0 = _mm_rows(y, wt["ab_out"], x, "mm_ab_out", norm_g=small["ffn_norm_g"][0])

    def ffn_fwd(h, hf, layer, next_g):
        zg, zu, a = _ffn_up_mid(hf, wt["up_t"][layer], small["conv_w"][layer], small["conv_b"][layer][None, :])
        if next_g is None:
            return _mm_rows(a, wt["down"][layer], h, "mm_down_loss", loss=(target, small["final_g"])), (hf, zg, zu, a)
        h_out, hn_next = _mm_rows(a, wt["down"][layer], h, "mm_down", norm_g=next_g)
        return h_out, hn_next, (hf, zg, zu, a)

    def ffn_bwd(dh_out, h, layer, saved, exchange=None):
        hf, zg, zu, a = saved
        da = _mm(dh_out, wt["down"][layer], "nt", BF16, name="mm_d_a")
        d_down = _mm(a, dh_out, "tn", BF16, name="mm_dw_down")
        comm = _Comm("exchange", [gw[k] for k in exchange]) if exchange else None
        dzg, dzu, dcw, dcb, got = _ffn_mid_bwd(zg, zu, da, small["conv_w"][layer], small["conv_b"][layer][None, :], comm=comm)
        recv.update(zip(exchange or [], got))
        dh, dg = _mm_rows(dzg, wt["up_t"][layer][:Fh], None, "mm_d_hf_norm", second=(dzu, wt["up_t"][layer][Fh:]),
                          bwd=(h, small["ffn_norm_g"][layer], dh_out))
        d_up = _mm(dzg, hf, "tn", BF16, name="mm_dw_up", rows=(2 * Fh, 0, None))
        d_up = _mm(dzu, hf, "tn", BF16, name="mm_dw_up_2", rows=(2 * Fh, Fh, d_up))
        return dh, dg, d_up, d_down, dcw, dcb

    h2, hn1, ffn0 = ffn_fwd(h1, hf0, 0, small["attn_norm_g"][1])

    zz = _mm(hn1, wt["c_in_t"], "nt", BF16, name="mm_c_in")
    ys = _sgu_fwd(zz, small["ln_g"], small["ln_b"], small["w_s"], small["b_s"])
    h3, hf1 = _mm_rows(ys, wt["c_out"], h2, "mm_c_out", norm_g=small["ffn_norm_g"][1])
    (loss_vec, dh4, gs["final_g"]), ffn1 = ffn_fwd(h3, hf1, 1, None)

    dh3, dgf1, gw["up_t1"], gw["down1"], dcw1, dcb1 = ffn_bwd(dh4, h3, 1, ffn1)
    dys = _mm(dh3, wt["c_out"], "nt", BF16, name="mm_d_ys")
    gw["c_out"] = _mm(ys, dh3, "tn", BF16, name="mm_dw_c_out")
    dzz, gs["w_s"], dbs, dlg, dlb = _sgu_bwd(zz, dys, small["ln_g"], small["ln_b"], small["w_s"], small["b_s"])
    gs["b_s"], gs["ln_g"], gs["ln_b"] = dbs[:, :, 0], dlg[0], dlb[0]
    dh2, dga1 = _mm_rows(dzz, wt["c_in_t"], None, "mm_d_hn1_norm", bwd=(h2, small["attn_norm_g"][1], dh3))
    gw["c_in_t"] = _mm(dzz, hn1, "tn", BF16, name="mm_dw_c_in")

    dh1, dgf0, gw["up_t0"], gw["down0"], dcw0, dcb0 = ffn_bwd(
        dh2, h1, 0, ffn0, exchange=["c_in_t", "c_out", "up_t1", "down1"] if overlap else None)

    dy = _mm(dh1, wt["ab_out"], "nt", BF16, name="mm_d_y")
    gw["ab_out"] = _mm(y, dh1, "tn", BF16, name="mm_dw_ab_out")
    dz_a = _ret_bwd(z_a, dy, states, tables)
    late = ["ab_out", "up_t0", "down0"] if overlap else []
    dq, dk, dv, dbias, got = _att_bwd(z_b, bias, dy, comm=_Comm("exchange", [gw[k] for k in late]) if late else None)
    recv.update(zip(late, got))
    dz_b = jnp.concatenate([dq, dk, dv], axis=1)
    gs["rel_bias"] = _bias_reduce(_bias_untile(dbias))
    gw["ab_in_t"] = _mm(dz_a, hn0, "tn", BF16, name="mm_dw_ab_in_a", rows=(RET_W + ATT_W, 0, None))
    gw["ab_in_t"] = _mm(dz_b, hn0, "tn", BF16, name="mm_dw_ab_in_b", rows=(RET_W + ATT_W, RET_W, gw["ab_in_t"]))
    if overlap:
        dhn0, got = _mm(dz_a, wt["ab_in_t"][:RET_W], "nn", F32, name="mm_d_hn0", comm=_Comm("exchange", [gw["ab_in_t"]]))
        recv["ab_in_t"] = got[0]
    else:
        dhn0 = _mm(dz_a, wt["ab_in_t"][:RET_W], "nn", F32, name="mm_d_hn0")
    grad_x, dga0 = _mm_rows(dz_b, wt["ab_in_t"][RET_W:], dhn0, "mm_d_hn0_norm", bwd=(x, small["attn_norm_g"][0], dh1))

    gs["attn_norm_g"] = jnp.concatenate([dga0, dga1], axis=0)
    gs["ffn_norm_g"] = jnp.concatenate([dgf0, dgf1], axis=0)
    gs["conv_w"] = jnp.stack([dcw0, dcw1])
    gs["conv_b"] = jnp.concatenate([dcb0, dcb1], axis=0)
    gs["final_g"] = gs["final_g"][0]
    return loss_vec[0, 0], grad_x, gw, gs, recv


MESH_ID = pl.DeviceIdType.MESH
ANY = pl.BlockSpec(memory_space=pl.ANY)


def _my_place():
    return lax.axis_index("x"), lax.axis_index("y"), lax.axis_index("c")


class _Comm:
    def __init__(self, kind, arrays):
        self.kind, self.arrays, self.n = kind, list(arrays), len(arrays)
        if kind == "gather":
            self.out_shape = [jax.ShapeDtypeStruct((N_DEV * s.shape[0], s.shape[1]), s.dtype) for s in arrays]
        else:
            self.out_shape = [jax.ShapeDtypeStruct((N_DEV, g.shape[0] // N_DEV, g.shape[1]), g.dtype) for g in arrays]
        n = self.n
        self.scratch = [pltpu.SemaphoreType.DMA((n, 7)), pltpu.SemaphoreType.DMA((n, 7)), pltpu.SemaphoreType.DMA((n,))]

    def phase(self, ph, in_refs, out_refs, sems):
        (self._gather if self.kind == "gather" else self._exchange)(ph, in_refs, out_refs, sems)

    def _gather(self, ph, x_refs, o_refs, sems):
        n = self.n
        send_sems, recv_sems, local_sems = sems
        x, y, c = _my_place()
        me, sibling = (x, y, c), (x, y, 1 - c)
        chips = [(1 - x, y), (x, 1 - y), (1 - x, 1 - y)]

        def rows(a, place):
            m = x_refs[a].shape[0]
            px, py, pc = place
            return o_refs[a].at[pl.ds((4 * px + 2 * py + pc) * m, m), :]

        def copy(a, k, block, to, own=False):
            return pltpu.make_async_remote_copy(
                src_ref=x_refs[a] if own else rows(a, block), dst_ref=rows(a, block),
                send_sem=send_sems.at[a, k], recv_sem=recv_sems.at[a, k], device_id=to, device_id_type=MESH_ID)

        def mine():
            return [pltpu.make_async_copy(x_refs[a], rows(a, me), local_sems.at[a]) for a in range(n)]

        def first():
            out = []
            for a in range(n):
                out.append(copy(a, 0, me, sibling, own=True))
                out += [copy(a, 1 + j, me, (*chip, c), own=True) for j, chip in enumerate(chips)]
            return out

        def passed():
            return [copy(a, 4 + j, (*chip, c), sibling) for j, chip in enumerate(chips) for a in range(n)]

        if ph == 0:
            for cp in mine() + first():
                cp.start()
        elif ph == 1:
            fw = passed()
            for j, chip in enumerate(chips):
                for a in range(n):
                    copy(a, 1 + j, (*chip, c), me).wait_recv()
                    fw[j * n + a].start()
        else:
            for a in range(n):
                copy(a, 0, sibling, me).wait_recv()
                for j, chip in enumerate(chips):
                    copy(a, 4 + j, (*chip, 1 - c), me).wait_recv()
            for cp in first() + passed():
                cp.wait_send()
            for cp in mine():
                cp.wait()

    def _exchange(self, ph, g_refs, o_refs, sems):
        n = self.n
        send_sems, recv_sems, local_sems = sems
        x, y, c = _my_place()
        me = 4 * x + 2 * y + c
        peers = [(x ^ ((k >> 2) & 1), y ^ ((k >> 1) & 1), c ^ (k & 1)) for k in range(1, N_DEV)]

        def block(a, idx):
            m = g_refs[a].shape[0] // N_DEV
            return g_refs[a].at[pl.ds(idx * m, m), :]

        def copy(a, k, slot):
            px, py, pc = peers[k]
            return pltpu.make_async_remote_copy(
                src_ref=block(a, 4 * px + 2 * py + pc), dst_ref=o_refs[a].at[slot],
                send_sem=send_sems.at[a, k], recv_sem=recv_sems.at[a, k], device_id=peers[k], device_id_type=MESH_ID)

        if ph == 1:
            return
        mine = [pltpu.make_async_copy(block(a, me), o_refs[a].at[me], local_sems.at[a]) for a in range(n)]
        sends = [copy(a, k, me) for k in range(N_DEV - 1) for a in range(n)]
        if ph == 0:
            for cp in mine + sends:
                cp.start()
        else:
            for k in range(N_DEV - 1):
                px, py, pc = peers[k]
                for a in range(n):
                    copy(a, k, 4 * px + 2 * py + pc).wait_recv()
            for cp in sends:
                cp.wait_send()
            for cp in mine:
                cp.wait()


def _comm_call(comm, name):
    n = comm.n

    def body(*refs):
        for ph in range(3):
            comm.phase(ph, refs[:n], refs[n:2 * n], refs[2 * n:])

    return pl.pallas_call(
        body, out_shape=comm.out_shape, in_specs=[ANY] * n, out_specs=[ANY] * n, scratch_shapes=comm.scratch, name=name,
    )(*comm.arrays)


def _host_comm(comm, when, step, total, cin, cout, csem):
    if comm is None:
        return
    sched = {0: 0, 1: (3 * total) // 4, 2: total - 1}
    phases = (0, 1) if when == "early" else (2,)
    for ph in phases:
        if ph == 1 and comm.kind == "exchange":
            continue

        @pl.when(step == sched[ph])
        def _(ph=ph):
            comm.phase(ph, cin, cout, csem)


def _all_gather(shards, name="all_gather"):
    return _comm_call(_Comm("gather", shards), name)


def _row_tile(r, target=256):
    best = None
    for t in range(8, min(r, target) + 1, 8):
        if r % t == 0:
            best = t
    return best if best is not None else r


def _sum8(parts, name="sum8"):
    _, M, N = parts.shape
    tr = _row_tile(M, 128)

    def body(p_ref, o_ref):
        acc = p_ref[0].astype(F32)
        for d in range(1, N_DEV):
            acc = acc + p_ref[d].astype(F32)
        o_ref[...] = acc

    return pl.pallas_call(
        body,
        grid=(M // tr,),
        in_specs=[pl.BlockSpec((N_DEV, tr, N), lambda i: (0, i, 0))],
        out_specs=pl.BlockSpec((tr, N), lambda i: (i, 0)),
        out_shape=jax.ShapeDtypeStruct((M, N), F32),
        compiler_params=_cp("parallel"),
        name=name,
    )(parts)


def _adamw(w, g, m, v, name="adamw"):
    shape = w.shape
    if w.ndim == 1:
        r2 = (1, shape[0])
    else:
        r2 = (int(np.prod(shape[:-1])), shape[-1])
    R, C = r2
    tr = _row_tile(R)
    bc1 = 1.0 - ADAM_B1 ** ADAM_STEP
    bc2 = 1.0 - ADAM_B2 ** ADAM_STEP

    def body(w_ref, g_ref, m_ref, v_ref, d_ref, nm_ref, nv_ref):
        gv = g_ref[...]
        nm = ADAM_B1 * m_ref[...] + (1.0 - ADAM_B1) * gv
        nv = ADAM_B2 * v_ref[...] + (1.0 - ADAM_B2) * (gv * gv)
        d_ref[...] = -ADAM_LR * ((nm / bc1) / (jnp.sqrt(nv / bc2) + ADAM_EPS) + ADAM_WD * w_ref[...])
        nm_ref[...] = nm
        nv_ref[...] = nv

    spec = pl.BlockSpec((tr, C), lambda i: (i, 0))
    out = pl.pallas_call(
        body,
        grid=(R // tr,),
        in_specs=[spec] * 4,
        out_specs=[spec] * 3,
        out_shape=[jax.ShapeDtypeStruct(r2, F32)] * 3,
        compiler_params=_cp("parallel"),
        name=name,
    )(w.reshape(r2), g.reshape(r2), m.reshape(r2), v.reshape(r2))
    return [o.reshape(shape) for o in out]


WEIGHTS = ['attn_norm_g', 'ffn_norm_g', 'ab_w_in', 'ab_w_out', 'ab_rel_bias', 'c_w_in', 'c_ln_g', 'c_ln_b', 'c_w_s', 'c_b_s',
           'c_w_out', 'ffn_w_up', 'ffn_conv_w', 'ffn_conv_b', 'ffn_w_down', 'final_norm_g']
SMALL_ORDER = ["attn_norm_g", "ffn_norm_g", "rel_bias", "ln_g", "ln_b", "w_s", "b_s", "conv_w", "conv_b", "final_g"]
PACK_ROW = 1024


def _pack(arrs):
    flat = jnp.concatenate([a.reshape(-1) for a in arrs])
    n = flat.shape[0]
    padded = -(-n // PACK_ROW) * PACK_ROW
    return jnp.pad(flat, (0, padded - n)).reshape(padded // 128, 128)


def _unpack(flat, shapes):
    out, off = [], 0
    for s in shapes:
        n = int(np.prod(s))
        out.append(flat[off:off + n].reshape(s))
        off += n
    return out


def _step(P):
    x, target = P["x"][0], P["loss_target"][0]
    me = 4 * lax.axis_index("x") + 2 * lax.axis_index("y") + lax.axis_index("c")
    n_up = P["ffn_w_up"].shape[0]
    Fc = P["ffn_conv_w"].shape[-1]
    Lc = P["c_ln_g"].shape[-1]

    first = _Comm("gather", [P["ab_w_in"][0].T.astype(BF16), _pack([P["ffn_conv_w"], P["c_ln_g"], P["c_ln_b"]])])
    hn0, full = _rms_fwd(x, P["attn_norm_g"][0], comm=first, name="rms_fwd_gather")
    wt = {"ab_in_t": full[0]}
    rest = {"ab_out": P["ab_w_out"][0], "c_in_t": P["c_w_in"][0].T, "c_out": P["c_w_out"][0],
            "up_t0": P["ffn_w_up"][0].T, "up_t1": P["ffn_w_up"][1].T, "down0": P["ffn_w_down"][0], "down1": P["ffn_w_down"][1]}
    rest_shards = [rest[k].astype(BF16) for k in REST]
    sm = full[-1].reshape(N_DEV, -1)
    conv_w = sm[:, :n_up * 3 * Fc].reshape(N_DEV, n_up, 3, Fc).transpose(1, 2, 0, 3).reshape(n_up, 3, N_DEV * Fc)
    off = n_up * 3 * Fc
    ln_g = sm[:, off:off + Lc].reshape(N_DEV * Lc)
    ln_b = sm[:, off + Lc:off + 2 * Lc].reshape(N_DEV * Lc)
    small = {"attn_norm_g": P["attn_norm_g"], "ffn_norm_g": P["ffn_norm_g"], "rel_bias": P["ab_rel_bias"][0],
             "ln_g": ln_g, "ln_b": ln_b, "w_s": P["c_w_s"][0], "b_s": P["c_b_s"][0], "conv_w": conv_w,
             "conv_b": P["ffn_conv_b"], "final_g": P["final_norm_g"]}

    loss_part, grad_x, gw, gs, recv = _local_step(x, target, wt, small, rest_shards=rest_shards, overlap=True, hn0=hn0)
    loss = lax.psum(loss_part, ("x", "y", "c"))

    s8 ={k: _sum8(recv[k], name="sum8") for k in ["ab_in_t"] + REST}
    g_big = {"ab_w_in": s8["ab_in_t"].T[None], "ab_w_out": s8["ab_out"][None], "c_w_in": s8["c_in_t"].T[None],
             "c_w_out": s8["c_out"][None], "ffn_w_up": jnp.stack([s8["up_t0"].T, s8["up_t1"].T]),
             "ffn_w_down": jnp.stack([s8["down0"], s8["down1"]])}

    packed = _pack([gs[k] for k in SMALL_ORDER])
    gathered = _all_gather([packed], name="gather_small_grads")[0]
    tot = _sum8(gathered.reshape(N_DEV, packed.shape[0], 128), name="sum8_small").reshape(-1)
    gsm = dict(zip(SMALL_ORDER, _unpack(tot, [gs[k].shape for k in SMALL_ORDER])))
    grads = dict(g_big)
    grads["attn_norm_g"] = gsm["attn_norm_g"]
    grads["ffn_norm_g"] = gsm["ffn_norm_g"]
    grads["ab_rel_bias"] = gsm["rel_bias"][None]
    grads["c_ln_g"] = lax.dynamic_slice(gsm["ln_g"], (me * Lc,), (Lc,))[None]
    grads["c_ln_b"] = lax.dynamic_slice(gsm["ln_b"], (me * Lc,), (Lc,))[None]
    grads["c_w_s"] = gsm["w_s"][None]
    grads["c_b_s"] = gsm["b_s"][None]
    grads["ffn_conv_w"] = lax.dynamic_slice(gsm["conv_w"], (0, 0, me * Fc), (n_up, 3, Fc))
    grads["ffn_conv_b"] = gsm["conv_b"]
    grads["final_norm_g"] = gsm["final_g"]

    delta, new_m, new_v = {}, {}, {}
    for k in WEIGHTS:
        delta[k], new_m[k], new_v[k] = _adamw(P[k], grads[k], P["m_" + k], P["v_" + k], name="adamw")
    return (loss, grad_x[None], *[grads[k] for k in WEIGHTS], *[delta[k] for k in WEIGHTS],
            *[new_m[k] for k in WEIGHTS], *[new_v[k] for k in WEIGHTS])


def kernel(x, attn_norm_g, ffn_norm_g, ab_w_in, ab_w_out, ab_rel_bias, c_w_in, c_ln_g, c_ln_b, c_w_s, c_b_s, c_w_out, ffn_w_up, ffn_conv_w, ffn_conv_b, ffn_w_down, final_norm_g, loss_target, m_attn_norm_g, m_ffn_norm_g, m_ab_w_in, m_ab_w_out, m_ab_rel_bias, m_c_w_in, m_c_ln_g, m_c_ln_b, m_c_w_s, m_c_b_s, m_c_w_out, m_ffn_w_up, m_ffn_conv_w, m_ffn_conv_b, m_ffn_w_down, m_final_norm_g, v_attn_norm_g, v_ffn_norm_g, v_ab_w_in, v_ab_w_out, v_ab_rel_bias, v_c_w_in, v_c_ln_g, v_c_ln_b, v_c_w_s, v_c_b_s, v_c_w_out, v_ffn_w_up, v_ffn_conv_w, v_ffn_conv_b, v_ffn_w_down, v_final_norm_g):
    return _step(dict(locals()))
```

```python
import functools

import numpy as np
import jax
import jax.numpy as jnp
from jax import lax
from jax.experimental import pallas as pl
from jax.experimental.pallas import tpu as pltpu

F32 = jnp.float32
BF16 = jnp.bfloat16

D_MODEL = 1024
CHUNK = 64
EPS = 1e-6
NEG_INF = -1e30
RET_HEADS = 4
RET_QK_DIM = 128
RET_V_DIM = 256
ATT_HEADS = 8
ATT_HEAD_DIM = 64
ATT_PAST = 8
ATT_BAND = (ATT_PAST + 1) * CHUNK
MAX_REL = 128
N_REL = 2 * MAX_REL + 1
N_REL_PAD = 384
SGU_BLOCK = 128
SGU_GROUPS = 8
SGU_WIDTH = 2048
SGU_GW = SGU_WIDTH // SGU_GROUPS
FFN_HIDDEN = 2816
RET_W = 2 * RET_HEADS * RET_QK_DIM + 2 * RET_HEADS * RET_V_DIM
ATT_W = 3 * ATT_HEADS * ATT_HEAD_DIM
N_DEV = 8

ADAM_LR = 0.001
ADAM_B1 = 0.9
ADAM_B2 = 0.999
ADAM_EPS = 1e-08
ADAM_WD = 0.01
ADAM_STEP = 10

VMEM_LIMIT = 52 * 1024 * 1024


def _cp(*sem):
    return pltpu.CompilerParams(dimension_semantics=sem if sem else None, vmem_limit_bytes=VMEM_LIMIT)


def _tile(n, target):
    if n <= target:
        return n
    best = None
    for t in range(128, target + 1, 128):
        if n % t == 0:
            best = t
    assert best is not None, (n, target)
    return best


def _gelu(x):
    c = 0.7978845608028654
    return 0.5 * x * (1.0 + jnp.tanh(c * (x + 0.044715 * x * x * x)))


def _gelu_and_grad(x):
    c = 0.7978845608028654
    x2 = x * x
    t = jnp.tanh(c * (x + 0.044715 * x * x2))
    cdf = 0.5 * (1.0 + t)
    grad = cdf + x * (0.5 * c) * (1.0 - t * t) * (1.0 + 3.0 * 0.044715 * x2)
    return x * cdf, grad


def _dot(a, b, dims):
    return lax.dot_general(a, b, (dims, ((), ())), preferred_element_type=F32)


NN = ((1,), (0,))
NT = ((1,), (1,))
TN = ((0,), (0,))


def _mm(a, b, mode, out_dtype, res=None, name="mm", tm_t=None, tn_t=None, tk_t=None, comm=None, rows=None):
    if mode == "nt":
        (M, K), N = a.shape, b.shape[0]
        dm, dn, dk = (1024, 2816, K) if N <= 2816 else (1024, 1024, K)
    elif mode == "nn":
        (M, K), N = a.shape, b.shape[1]
        dm, dn, dk = (1024 if K <= 3072 else 512), 1024, K
    else:
        (K, M), N = a.shape, b.shape[1]
        dm, dn, dk = 1536, 1024, 2048
    tm, tn, tk = _tile(M, tm_t or dm), _tile(N, tn_t or dn), _tile(K, tk_t or dk)
    nk = K // tk
    dims = {"nt": NT, "nn": NN, "tn": TN}[mode]
    a_spec = pl.BlockSpec((tk, tm), lambda i, j, k: (k, i)) if mode == "tn" else pl.BlockSpec((tm, tk), lambda i, j, k: (i, k))
    b_spec = pl.BlockSpec((tn, tk), lambda i, j, k: (j, k)) if mode == "nt" else pl.BlockSpec((tk, tn), lambda i, j, k: (k, j))
    gi, gj = M // tm, N // tn
    out_rows, row0, into = rows if rows else (M, 0, None)
    assert row0 % tm == 0 and not (comm and into is not None)
    o_spec = pl.BlockSpec((tm, tn), lambda i, j, k: (i + row0 // tm, j))
    has_res = res is not None
    nc = 1 if into is not None else (comm.n if comm else 0)
    n_in = 3 if has_res else 2

    def body(*refs):
        a_ref, b_ref = refs[:2]
        r_ref = refs[2] if has_res else None
        nco = comm.n if comm else 0
        cin = refs[n_in:n_in + nc]
        o_ref = refs[n_in + nc]
        cout = refs[n_in + nc + 1:n_in + nc + 1 + nco]
        scratch = refs[n_in + nc + 1 + nco:]
        csem = scratch[1:] if nk > 1 else scratch
        step = (pl.program_id(0) * gj + pl.program_id(1)) * nk + pl.program_id(2)
        _host_comm(comm, "early", step, gi * gj * nk, cin, cout, csem)
        p = _dot(a_ref[...].astype(BF16), b_ref[...].astype(BF16), dims)
        if nk == 1:
            if has_res:
                p = p + r_ref[...]
            o_ref[...] = p.astype(out_dtype)
        else:
            acc = scratch[0]
            k = pl.program_id(2)

            @pl.when(k == 0)
            def _():
                acc[...] = p

            @pl.when(k > 0)
            def _():
                acc[...] += p

            @pl.when(k == nk - 1)
            def _():
                t = acc[...]
                if has_res:
                    t = t + r_ref[...]
                o_ref[...] = t.astype(out_dtype)
        _host_comm(comm, "late", step, gi * gj * nk, cin, cout, csem)

    in_specs = [a_spec, b_spec] + ([o_spec] if has_res else []) + [ANY] * nc
    args = (a, b) + ((res,) if has_res else ()) + ((into,) if into is not None else tuple(comm.arrays if comm else ()))
    out = pl.pallas_call(
        body,
        grid=(gi, gj, nk),
        in_specs=in_specs,
        out_specs=[o_spec] + ([ANY] * nc if comm else []),
        out_shape=[jax.ShapeDtypeStruct((out_rows, N), out_dtype)] + (comm.out_shape if comm else []),
        scratch_shapes=([pltpu.VMEM((tm, tn), F32)] if nk > 1 else []) + (comm.scratch if comm else []),
        input_output_aliases={n_in: 0} if into is not None else {},
        compiler_params=_cp("arbitrary", "arbitrary", "arbitrary") if comm else _cp("parallel", "parallel", "arbitrary"),
        name=name,
    )(*args)
    return (out[0], list(out[1:])) if comm else out[0]


def _mm_rows(a, b, res, name, norm_g=None, bwd=None, loss=None, second=None, tm=512):
    M, K = a.shape
    Dm = b.shape[1]
    if norm_g is not None and K <= 2048:
        tm = 2 * tm
    tm = min(tm, M)
    row = pl.BlockSpec((tm, Dm), lambda i: (i, 0))
    vec = pl.BlockSpec((1, Dm), lambda i: (0, 0))
    a_spec = pl.BlockSpec((tm, K), lambda i: (i, 0))
    b_spec = pl.BlockSpec((K, Dm), lambda i: (0, 0))

    if loss is not None:
        target, g = loss

        def body(a_ref, b_ref, r_ref, g_ref, t_ref, loss_ref, dh_ref, dg_ref):
            @pl.when(pl.program_id(0) == 0)
            def _():
                loss_ref[...] = jnp.zeros_like(loss_ref)
                dg_ref[...] = jnp.zeros_like(dg_ref)

            x = _dot(a_ref[...].astype(BF16), b_ref[...].astype(BF16), NN) + r_ref[...]
            gv = g_ref[...]
            r = lax.rsqrt(jnp.mean(x * x, axis=-1, keepdims=True) + EPS)
            xhat = x * r
            e = xhat * gv - t_ref[...]
            loss_ref[...] += jnp.full((1, 128), 0.5 / Dm, F32) * jnp.sum(e * e)
            dy = e * (1.0 / Dm)
            dg_ref[...] += jnp.sum(dy * xhat, axis=0, keepdims=True)
            dx = dy * gv
            m = jnp.mean(dx * xhat, axis=-1, keepdims=True)
            dh_ref[...] = r * (dx - xhat * m)

        return pl.pallas_call(
            body, grid=(M // tm,), in_specs=[a_spec, b_spec, row, vec, row],
            out_specs=[pl.BlockSpec((1, 128), lambda i: (0, 0)), row, vec],
            out_shape=[jax.ShapeDtypeStruct((1, 128), F32), jax.ShapeDtypeStruct((M, Dm), F32), jax.ShapeDtypeStruct((1, Dm), F32)],
            compiler_params=_cp("arbitrary"), name=name,
        )(a, b, res, g.reshape(1, Dm), target)

    if bwd is None:
        def body(a_ref, b_ref, r_ref, g_ref, o_ref, n_ref):
            t = _dot(a_ref[...].astype(BF16), b_ref[...].astype(BF16), NN) + r_ref[...]
            o_ref[...] = t
            r = lax.rsqrt(jnp.mean(t * t, axis=-1, keepdims=True) + EPS)
            n_ref[...] = (t * r * g_ref[...]).astype(n_ref.dtype)

        return pl.pallas_call(
            body, grid=(M // tm,), in_specs=[a_spec, b_spec, row, vec], out_specs=[row, row],
            out_shape=[jax.ShapeDtypeStruct((M, Dm), F32), jax.ShapeDtypeStruct((M, Dm), BF16)],
            compiler_params=_cp("parallel"), name=name,
        )(a, b, res, norm_g.reshape(1, Dm))

    h, g, dres = bwd
    has_res = res is not None
    has2 = second is not None

    def body(*refs):
        a_ref, b_ref = refs[:2]
        h_ref, g_ref, dres_ref, dh_ref, dg_ref = refs[-5:]

        @pl.when(pl.program_id(0) == 0)
        def _():
            dg_ref[...] = jnp.zeros_like(dg_ref)

        d = _dot(a_ref[...].astype(BF16), b_ref[...].astype(BF16), NN)
        if has2:
            d = d + _dot(refs[2][...].astype(BF16), refs[3][...].astype(BF16), NN)
        if has_res:
            d = d + refs[4 if has2 else 2][...]
        x = h_ref[...]
        r = lax.rsqrt(jnp.mean(x * x, axis=-1, keepdims=True) + EPS)
        xhat = x * r
        dg_ref[...] += jnp.sum(d * xhat, axis=0, keepdims=True)
        dx = d * g_ref[...]
        m = jnp.mean(dx * xhat, axis=-1, keepdims=True)
        dh_ref[...] = dres_ref[...] + r * (dx - xhat * m)

    second_specs = [pl.BlockSpec((tm, second[0].shape[1]), lambda i: (i, 0)),
                    pl.BlockSpec(second[1].shape, lambda i: (0, 0))] if has2 else []
    return pl.pallas_call(
        body, grid=(M // tm,), in_specs=[a_spec, b_spec] + second_specs + ([row] if has_res else []) + [row, vec, row],
        out_specs=[row, vec],
        out_shape=[jax.ShapeDtypeStruct((M, Dm), F32), jax.ShapeDtypeStruct((1, Dm), F32)],
        compiler_params=_cp("arbitrary"), name=name,
    )(a, b, *(second if has2 else ()), *((res,) if has_res else ()), h, g.reshape(1, Dm), dres)


def _rms_fwd(h, g, comm=None, name="rms_fwd", tm=512):
    T, Dm = h.shape
    tm = min(tm, T)
    nc = comm.n if comm else 0
    ni = T // tm

    def body(*refs):
        h_ref, g_ref = refs[:2]
        cin, o_ref, cout, csem = refs[2:2 + nc], refs[2 + nc], refs[3 + nc:3 + 2 * nc], refs[3 + 2 * nc:]
        step = pl.program_id(0)
        _host_comm(comm, "early", step, ni, cin, cout, csem)
        x = h_ref[...]
        r = lax.rsqrt(jnp.mean(x * x, axis=-1, keepdims=True) + EPS)
        o_ref[...] = (x * r * g_ref[...]).astype(o_ref.dtype)
        _host_comm(comm, "late", step, ni, cin, cout, csem)

    out = pl.pallas_call(
        body,
        grid=(ni,),
        in_specs=[pl.BlockSpec((tm, Dm), lambda i: (i, 0)), pl.BlockSpec((1, Dm), lambda i: (0, 0))] + [ANY] * nc,
        out_specs=[pl.BlockSpec((tm, Dm), lambda i: (i, 0))] + [ANY] * nc,
        out_shape=[jax.ShapeDtypeStruct((T, Dm), BF16)] + (comm.out_shape if comm else []),
        scratch_shapes=comm.scratch if comm else [],
        compiler_params=_cp("arbitrary"),
        name=name,
    )(h, g.reshape(1, Dm), *(comm.arrays if comm else []))
    return (out[0], list(out[1:])) if comm else out[0]


HALO = 16


def _conv3(ext, w_ref, b_ref):
    return w_ref[0:1, :] * pltpu.roll(ext, 2, 0) + w_ref[1:2, :] * pltpu.roll(ext, 1, 0) + w_ref[2:3, :] * ext + b_ref[...]


def _ffn_up_mid(hf, w_up_t, cw, cb, name="ffn_up_mid", tm=512, tc=1408):
    T, Dm = hf.shape
    Fh = w_up_t.shape[0] // 2
    tm = min(tm, T)
    nj, ni = Fh // tc, T // tm

    rc = min(512, tm)

    def body(h_ref, wg_ref, wu_ref, cwg_ref, cwu_ref, cbg_ref, cbu_ref, zg_ref, zu_ref, a_ref, eg_ref, eu_ref, ng_ref, nu_ref):
        @pl.when(pl.program_id(1) == 0)
        def _():
            eg_ref[...] = jnp.zeros_like(eg_ref)
            eu_ref[...] = jnp.zeros_like(eu_ref)

        for r in range(tm // rc):
            ext = slice(r * rc, r * rc + HALO + rc)
            cg = _conv3(eg_ref[ext, :].astype(F32), cwg_ref, cbg_ref)[HALO:]
            cu = _conv3(eu_ref[ext, :].astype(F32), cwu_ref, cbu_ref)[HALO:]
            a_ref[r * rc:(r + 1) * rc, :] = _gelu(cg.astype(BF16)) * cu.astype(BF16)
            h = h_ref[r * rc:(r + 1) * rc, :]
            for w_ref, z_ref, n_ref in ((wg_ref, zg_ref, ng_ref), (wu_ref, zu_ref, nu_ref)):
                z = _dot(h, w_ref[...], NT).astype(BF16)
                z_ref[r * rc:(r + 1) * rc, :] = z
                n_ref[r * rc:(r + 1) * rc, :] = z
        for e_ref, n_ref in ((eg_ref, ng_ref), (eu_ref, nu_ref)):
            e_ref[0:HALO, :] = e_ref[tm:tm + HALO, :]
            e_ref[HALO:HALO + tm, :] = n_ref[...]

    last = ni - 1
    z_spec = pl.BlockSpec((tm, tc), lambda j, i: (jnp.minimum(i, last), j))
    return pl.pallas_call(
        body,
        grid=(nj, ni + 1),
        in_specs=[pl.BlockSpec((tm, Dm), lambda j, i: (jnp.minimum(i, last), 0)),
                  pl.BlockSpec((tc, Dm), lambda j, i: (j, 0)), pl.BlockSpec((tc, Dm), lambda j, i: (nj + j, 0)),
                  pl.BlockSpec((3, tc), lambda j, i: (0, j)), pl.BlockSpec((3, tc), lambda j, i: (0, nj + j)),
                  pl.BlockSpec((1, tc), lambda j, i: (0, j)), pl.BlockSpec((1, tc), lambda j, i: (0, nj + j))],
        out_specs=[z_spec, z_spec, pl.BlockSpec((tm, tc), lambda j, i: (jnp.maximum(i - 1, 0), j))],
        out_shape=[jax.ShapeDtypeStruct((T, Fh), BF16)] * 3,
        scratch_shapes=[pltpu.VMEM((HALO + tm, tc), BF16), pltpu.VMEM((HALO + tm, tc), BF16),
                        pltpu.VMEM((tm, tc), BF16), pltpu.VMEM((tm, tc), BF16)],
        compiler_params=_cp("arbitrary", "arbitrary"),
        name=name,
    )(hf, w_up_t, w_up_t, cw, cw, cb, cb)


def _ffn_mid_bwd(zg, zu, da, cw, cb, comm=None, name="ffn_mid_bwd", tm=1024, tc=256):
    T, Fh = zg.shape
    tm = min(tm, T)
    nj = Fh // tc
    hb = tm // HALO
    nhb = T // HALO

    nc = comm.n if comm else 0
    ni = T // tm

    def body(*refs):
        zg_ref, zu_ref, zgp_ref, zup_ref, zgn_ref, zun_ref, da_ref, dan_ref, wg_ref, wu_ref, bg_ref, bu_ref = refs[:12]
        cin = refs[12:12 + nc]
        dzg_ref, dzu_ref, dwg_ref, dwu_ref, dbg_ref, dbu_ref = refs[12 + nc:18 + nc]
        cout = refs[18 + nc:18 + 2 * nc]
        csem = refs[18 + 2 * nc:]
        i = pl.program_id(1)
        step = pl.program_id(0) * ni + i
        _host_comm(comm, "early", step, nj * ni, cin, cout, csem)
        first = i == 0
        last = i == ni - 1

        @pl.when(first)
        def _():
            dwg_ref[...] = jnp.zeros_like(dwg_ref)
            dwu_ref[...] = jnp.zeros_like(dwu_ref)
            dbg_ref[...] = jnp.zeros_like(dbg_ref)
            dbu_ref[...] = jnp.zeros_like(dbu_ref)

        def ext_of(p_ref, c_ref, n_ref):
            p = jnp.where(first, 0.0, p_ref[...].astype(F32))
            return jnp.concatenate([p, c_ref[...].astype(F32), n_ref[...].astype(F32)], axis=0)

        zge = ext_of(zgp_ref, zg_ref, zgn_ref)
        zue = ext_of(zup_ref, zu_ref, zun_ref)
        dan = jnp.where(last, 0.0, dan_ref[...].astype(F32))
        dae = jnp.concatenate([jnp.zeros((HALO, tc), F32), da_ref[...].astype(F32), dan], axis=0)
        zg1, zg2 = pltpu.roll(zge, 1, 0), pltpu.roll(zge, 2, 0)
        zu1, zu2 = pltpu.roll(zue, 1, 0), pltpu.roll(zue, 2, 0)
        cg = wg_ref[0:1, :] * zg2 + wg_ref[1:2, :] * zg1 + wg_ref[2:3, :] * zge + bg_ref[...]
        cu = wu_ref[0:1, :] * zu2 + wu_ref[1:2, :] * zu1 + wu_ref[2:3, :] * zue + bu_ref[...]
        gel, dgel = (t.astype(F32) for t in _gelu_and_grad(cg.astype(BF16)))
        dcg = dae * cu * dgel
        dcu = dae * gel
        lo, hi = HALO, HALO + tm

        def back(dc, taps, w_ref, dz_ref, dw_ref, db_ref):
            n = dc.shape[0]
            dz = w_ref[2:3, :] * dc + w_ref[1:2, :] * pltpu.roll(dc, n - 1, 0) + w_ref[0:1, :] * pltpu.roll(dc, n - 2, 0)
            dz_ref[...] = dz[lo:hi].astype(dz_ref.dtype)
            dcc = dc[lo:hi]
            db_ref[...] += jnp.sum(dcc, axis=0, keepdims=True)
            for k, tap in enumerate(taps):
                dw_ref[k:k + 1, :] += jnp.sum(dcc * tap[lo:hi], axis=0, keepdims=True)

        back(dcg, (zg2, zg1, zge), wg_ref, dzg_ref, dwg_ref, dbg_ref)
        back(dcu, (zu2, zu1, zue), wu_ref, dzu_ref, dwu_ref, dbu_ref)
        _host_comm(comm, "late", step, nj * ni, cin, cout, csem)

    cur = pl.BlockSpec((tm, tc), lambda j, i: (i, j))
    prev = pl.BlockSpec((HALO, tc), lambda j, i: (jnp.maximum(i * hb - 1, 0), j))
    nxt = pl.BlockSpec((HALO, tc), lambda j, i: (jnp.minimum((i + 1) * hb, nhb - 1), j))
    wg = pl.BlockSpec((3, tc), lambda j, i: (0, j))
    wu = pl.BlockSpec((3, tc), lambda j, i: (0, j + nj))
    bg = pl.BlockSpec((1, tc), lambda j, i: (0, j))
    bu = pl.BlockSpec((1, tc), lambda j, i: (0, j + nj))
    dw = pl.BlockSpec((3, tc), lambda j, i: (0, j))
    db = pl.BlockSpec((1, tc), lambda j, i: (0, j))
    out = pl.pallas_call(
        body,
        grid=(nj, ni),
        in_specs=[cur, cur, prev, prev, nxt, nxt, cur, nxt, wg, wu, bg, bu] + [ANY] * nc,
        out_specs=[cur, cur, dw, dw, db, db] + [ANY] * nc,
        out_shape=[jax.ShapeDtypeStruct((T, Fh), BF16), jax.ShapeDtypeStruct((T, Fh), BF16),
                   jax.ShapeDtypeStruct((3, Fh), F32), jax.ShapeDtypeStruct((3, Fh), F32),
                   jax.ShapeDtypeStruct((1, Fh), F32), jax.ShapeDtypeStruct((1, Fh), F32)] + (comm.out_shape if comm else []),
        scratch_shapes=comm.scratch if comm else [],
        compiler_params=_cp("arbitrary", "arbitrary"),
        name=name,
    )(zg, zu, zg, zu, zg, zu, da, da, cw, cw, cb, cb, *(comm.arrays if comm else []))
    dzg, dzu, dwg, dwu, dbg, dbu = out[:6]
    return dzg, dzu, jnp.concatenate([dwg, dwu], axis=1), jnp.concatenate([dbg, dbu], axis=1), list(out[6:])


def _sgu_mask():
    r = lax.broadcasted_iota(jnp.int32, (SGU_BLOCK, SGU_BLOCK), 0)
    c = lax.broadcasted_iota(jnp.int32, (SGU_BLOCK, SGU_BLOCK), 1)
    return (c < CHUNK) | (r >= CHUNK)


def _sgu_fwd(zz, ln_g, ln_b, w_s, b_s, name="sgu_fwd", tm=256):
    T = zz.shape[0]
    tm = min(tm, T)
    W = SGU_WIDTH

    def body(zu_ref, zv_ref, g_ref, b_ref, ws_ref, bs_ref, y_ref):
        u = _gelu(zu_ref[...]).astype(F32)
        v = _gelu(zv_ref[...]).astype(F32)
        mu = jnp.mean(v, axis=-1, keepdims=True)
        xc = v - mu
        rstd = lax.rsqrt(jnp.mean(xc * xc, axis=-1, keepdims=True) + EPS)
        vn = (xc * rstd * g_ref[...] + b_ref[...]).astype(BF16)
        mask = _sgu_mask()
        for g in range(SGU_GROUPS):
            wm = jnp.where(mask, ws_ref[g], 0.0).astype(BF16)
            cs = slice(g * SGU_GW, (g + 1) * SGU_GW)
            for blk in range(tm // SGU_BLOCK):
                rs = slice(blk * SGU_BLOCK, (blk + 1) * SGU_BLOCK)
                mixed = _dot(wm, vn[rs, cs], NN) + bs_ref[g]
                y_ref[rs, cs] = (u[rs, cs] * mixed).astype(y_ref.dtype)

    return pl.pallas_call(
        body,
        grid=(T // tm,),
        in_specs=[pl.BlockSpec((tm, W), lambda i: (i, 0)), pl.BlockSpec((tm, W), lambda i: (i, 1)),
                  pl.BlockSpec((1, W), lambda i: (0, 0)), pl.BlockSpec((1, W), lambda i: (0, 0)),
                  pl.BlockSpec((SGU_GROUPS, SGU_BLOCK, SGU_BLOCK), lambda i: (0, 0, 0)),
                  pl.BlockSpec((SGU_GROUPS, SGU_BLOCK, 1), lambda i: (0, 0, 0))],
        out_specs=pl.BlockSpec((tm, W), lambda i: (i, 0)),
        out_shape=jax.ShapeDtypeStruct((T, W), BF16),
        compiler_params=_cp("parallel"),
        name=name,
    )(zz, zz, ln_g.reshape(1, W), ln_b.reshape(1, W), w_s, b_s.reshape(SGU_GROUPS, SGU_BLOCK, 1))


def _sgu_bwd(zz, dy, ln_g, ln_b, w_s, b_s, name="sgu_bwd", tm=256):
    T = zz.shape[0]
    tm = min(tm, T)
    W = SGU_WIDTH

    def body(zu_ref, zv_ref, dy_ref, g_ref, b_ref, ws_ref, bs_ref, dzz_ref, dws_ref, dbs_ref, dg_ref, db_ref, dvn_ref):
        i = pl.program_id(0)

        @pl.when(i == 0)
        def _():
            dws_ref[...] = jnp.zeros_like(dws_ref)
            dbs_ref[...] = jnp.zeros_like(dbs_ref)
            dg_ref[...] = jnp.zeros_like(dg_ref)
            db_ref[...] = jnp.zeros_like(db_ref)

        u, du_dz = (t.astype(F32) for t in _gelu_and_grad(zu_ref[...]))
        v, dv_dz = (t.astype(F32) for t in _gelu_and_grad(zv_ref[...]))
        mu = jnp.mean(v, axis=-1, keepdims=True)
        xc = v - mu
        rstd = lax.rsqrt(jnp.mean(xc * xc, axis=-1, keepdims=True) + EPS)
        xhat = xc * rstd
        gv = g_ref[...]
        vn = (xhat * gv + b_ref[...]).astype(BF16)
        dyv = dy_ref[...].astype(F32)
        mask = _sgu_mask()
        for g in range(SGU_GROUPS):
            wm = jnp.where(mask, ws_ref[g], 0.0).astype(BF16)
            cs = slice(g * SGU_GW, (g + 1) * SGU_GW)
            dw_acc = jnp.zeros((SGU_BLOCK, SGU_BLOCK), F32)
            db_acc = jnp.zeros((SGU_BLOCK, 1), F32)
            for blk in range(tm // SGU_BLOCK):
                rs = slice(blk * SGU_BLOCK, (blk + 1) * SGU_BLOCK)
                vn_bg = vn[rs, cs]
                mixed = _dot(wm, vn_bg, NN) + bs_ref[g]
                dy_bg = dyv[rs, cs]
                dmixed = dy_bg * u[rs, cs]
                dmb = dmixed.astype(BF16)
                dw_acc += _dot(dmb, vn_bg, NT)
                db_acc += jnp.sum(dmixed, axis=1, keepdims=True)
                dvn_ref[rs, cs] = _dot(wm, dmb, TN)
                dzz_ref[rs, cs] = (dy_bg * mixed * du_dz[rs, cs]).astype(dzz_ref.dtype)
            dws_ref[g] += jnp.where(mask, dw_acc, 0.0)
            dbs_ref[g] += db_acc
        dvn = dvn_ref[...]
        dg_ref[...] += jnp.sum(dvn * xhat, axis=0, keepdims=True)
        db_ref[...] += jnp.sum(dvn, axis=0, keepdims=True)
        dxh = dvn * gv
        m1 = jnp.mean(dxh, axis=-1, keepdims=True)
        m2 = jnp.mean(dxh * xhat, axis=-1, keepdims=True)
        dv = rstd * (dxh - m1 - xhat * m2)
        dzz_ref[:, W:] = (dv * dv_dz).astype(dzz_ref.dtype)

    vec = pl.BlockSpec((1, W), lambda i: (0, 0))
    ws_spec = pl.BlockSpec((SGU_GROUPS, SGU_BLOCK, SGU_BLOCK), lambda i: (0, 0, 0))
    bs_spec = pl.BlockSpec((SGU_GROUPS, SGU_BLOCK, 1), lambda i: (0, 0, 0))
    return pl.pallas_call(
        body,
        grid=(T // tm,),
        in_specs=[pl.BlockSpec((tm, W), lambda i: (i, 0)), pl.BlockSpec((tm, W), lambda i: (i, 1)),
                  pl.BlockSpec((tm, W), lambda i: (i, 0)), vec, vec, ws_spec, bs_spec],
        out_specs=[pl.BlockSpec((tm, 2 * W), lambda i: (i, 0)), ws_spec, bs_spec, vec, vec],
        out_shape=[jax.ShapeDtypeStruct((T, 2 * W), BF16),
                   jax.ShapeDtypeStruct((SGU_GROUPS, SGU_BLOCK, SGU_BLOCK), F32),
                   jax.ShapeDtypeStruct((SGU_GROUPS, SGU_BLOCK, 1), F32),
                   jax.ShapeDtypeStruct((1, W), F32), jax.ShapeDtypeStruct((1, W), F32)],
        scratch_shapes=[pltpu.VMEM((tm, W), F32)],
        compiler_params=_cp("arbitrary"),
        name=name,
    )(zz, zz, dy, ln_g.reshape(1, W), ln_b.reshape(1, W), w_s, b_s.reshape(SGU_GROUPS, SGU_BLOCK, 1))


RET_TR = 256
RET_BLK = 256
QK_SCALE = RET_QK_DIM ** -0.5


def _ret_tables(T):
    half = RET_QK_DIM // 2
    inv = 1.0 / (10000.0 ** jnp.linspace(0.0, 1.0, half, dtype=F32))
    inv2 = jnp.concatenate([inv, inv])[None, :]
    sgn = jnp.concatenate([-jnp.ones((half,), F32), jnp.ones((half,), F32)])[None, :]
    tr = min(RET_TR, T)

    def trig(pos):
        ang = pos.astype(F32)[:, None] * inv2
        return jnp.stack([jnp.cos(ang), jnp.sin(ang), sgn * jnp.sin(ang)])

    tile_tab = jnp.pad(trig(jnp.arange(T // tr) * tr).transpose(1, 0, 2), ((0, 0), (0, 5), (0, 0)))
    row_tab = trig(jnp.arange(tr))
    log_g = jnp.log1p(-jnp.exp2(-5.0 - jnp.arange(RET_HEADS, dtype=F32)))
    idx = jnp.arange(RET_BLK, dtype=F32)
    dist = idx[:, None] - idx[None, :]
    cq, ck = jnp.arange(RET_BLK)[:, None] // CHUNK, jnp.arange(RET_BLK)[None, :] // CHUNK
    expo = jnp.where(ck == cq, jnp.abs(dist), dist)
    d_blk = jnp.where((ck <= cq)[None], jnp.exp(log_g[:, None, None] * expo[None]), 0.0)
    k_dec = jnp.exp(log_g[:, None] * (RET_BLK - 1 - idx)[None, :])[:, :, None]
    q_dec = jnp.exp(log_g[:, None] * (idx + 1.0)[None, :])[:, :, None]
    c_dec = jnp.exp(log_g * RET_BLK)[:, None, None]
    return tile_tab, row_tab, d_blk, q_dec, k_dec, c_dec


def _rot(x, c, s):
    return x * c + pltpu.roll(x, RET_QK_DIM // 2, 1) * s


def _rot_tables(tt_ref, rt_ref):
    ca, sa, ga = tt_ref[0:1, :], tt_ref[1:2, :], tt_ref[2:3, :]
    cb, sb, gb = rt_ref[0], rt_ref[1], rt_ref[2]
    return ca * cb - sa * sb, ga * cb + ca * gb


def _ret_specs(tr, rev, nb):
    ix = (lambda n: nb - 1 - n) if rev else (lambda n: n)
    tt = pl.BlockSpec((None, 8, RET_QK_DIM), lambda n: (ix(n), 0, 0))
    rt = pl.BlockSpec((3, tr, RET_QK_DIM), lambda n: (0, 0, 0))
    dm = pl.BlockSpec((RET_HEADS, RET_BLK, RET_BLK), lambda n: (0, 0, 0))
    dv = pl.BlockSpec((RET_HEADS, RET_BLK, 1), lambda n: (0, 0, 0))
    dc = pl.BlockSpec((RET_HEADS, 1, 1), lambda n: (0, 0, 0))
    return ix, [tt, rt, dm, dv, dv, dc]


def _ret_fwd(z_a, tables, name="ret_fwd"):
    T = z_a.shape[0]
    tr = min(RET_TR, T)
    cpb = tr // RET_BLK
    nb = T // tr
    QW, VW = RET_HEADS * RET_QK_DIM, RET_HEADS * RET_V_DIM
    ix, tab_specs = _ret_specs(tr, False, nb)

    def body(z_ref, tt_ref, rt_ref, dm_ref, qd_ref, kd_ref, cd_ref, y_ref, st_ref, state):
        @pl.when(pl.program_id(0) == 0)
        def _():
            state[...] = jnp.zeros_like(state)

        rot_c, rot_s = _rot_tables(tt_ref, rt_ref)
        for c in range(cpb):
            for h in range(RET_HEADS):
                rs = slice(c * RET_BLK, (c + 1) * RET_BLK)
                cc, ss = rot_c[rs, :], rot_s[rs, :]
                q = z_ref[rs, h * RET_QK_DIM:(h + 1) * RET_QK_DIM].astype(F32)
                k = z_ref[rs, QW + h * RET_QK_DIM:QW + (h + 1) * RET_QK_DIM].astype(F32)
                v = z_ref[rs, 2 * QW + h * RET_V_DIM:2 * QW + (h + 1) * RET_V_DIM]
                gt = z_ref[rs, 2 * QW + VW + h * RET_V_DIM:2 * QW + VW + (h + 1) * RET_V_DIM].astype(F32)
                qr = _rot(q, cc, ss)
                kr = _rot(k, cc, ss) * QK_SCALE
                s_old = state[h]
                sb = s_old.astype(BF16)
                st_ref[c, h] = sb
                s = _dot(qr.astype(BF16), kr.astype(BF16), NT) * dm_ref[h]
                o = _dot(s.astype(BF16), v, NN) + _dot((qr * qd_ref[h]).astype(BF16), sb, NN)
                state[h] = s_old * cd_ref[h] + _dot((kr * kd_ref[h]).astype(BF16), v, TN)
                mu = jnp.mean(o, axis=-1, keepdims=True)
                oc = o - mu
                rn = oc * lax.rsqrt(jnp.mean(oc * oc, axis=-1, keepdims=True) + EPS)
                silu = gt / (1.0 + jnp.exp(-gt))
                y_ref[rs, h * RET_V_DIM:(h + 1) * RET_V_DIM] = (silu * rn).astype(y_ref.dtype)

    return pl.pallas_call(
        body,
        grid=(nb,),
        in_specs=[pl.BlockSpec((tr, RET_W), lambda n: (n, 0))] + tab_specs,
        out_specs=[pl.BlockSpec((tr, VW), lambda n: (n, 0)),
                   pl.BlockSpec((cpb, RET_HEADS, RET_QK_DIM, RET_V_DIM), lambda n: (n, 0, 0, 0))],
        out_shape=[jax.ShapeDtypeStruct((T, Y_COLS), BF16),
                   jax.ShapeDtypeStruct((T // RET_BLK, RET_HEADS, RET_QK_DIM, RET_V_DIM), BF16)],
        scratch_shapes=[pltpu.VMEM((RET_HEADS, RET_QK_DIM, RET_V_DIM), F32)],
        compiler_params=_cp("arbitrary"),
        name=name,
    )(z_a, *tables)


def _ret_bwd(z_a, dy, states, tables, name="ret_bwd"):
    T = z_a.shape[0]
    tr = min(RET_TR, T)
    cpb = tr // RET_BLK
    nb = T // tr
    QW, VW = RET_HEADS * RET_QK_DIM, RET_HEADS * RET_V_DIM
    ix, tab_specs = _ret_specs(tr, True, nb)

    def body(z_ref, dy_ref, st_ref, tt_ref, rt_ref, dm_ref, qd_ref, kd_ref, cd_ref, dz_ref, dstate):
        @pl.when(pl.program_id(0) == 0)
        def _():
            dstate[...] = jnp.zeros_like(dstate)

        rot_c, rot_s = _rot_tables(tt_ref, rt_ref)
        for c in reversed(range(cpb)):
            for h in range(RET_HEADS):
                rs = slice(c * RET_BLK, (c + 1) * RET_BLK)
                cc, ss = rot_c[rs, :], rot_s[rs, :]
                q = z_ref[rs, h * RET_QK_DIM:(h + 1) * RET_QK_DIM].astype(F32)
                k = z_ref[rs, QW + h * RET_QK_DIM:QW + (h + 1) * RET_QK_DIM].astype(F32)
                v = z_ref[rs, 2 * QW + h * RET_V_DIM:2 * QW + (h + 1) * RET_V_DIM]
                gt = z_ref[rs, 2 * QW + VW + h * RET_V_DIM:2 * QW + VW + (h + 1) * RET_V_DIM].astype(F32)
                dyv = dy_ref[rs, h * RET_V_DIM:(h + 1) * RET_V_DIM].astype(F32)
                dmat, qd, kd = dm_ref[h], qd_ref[h], kd_ref[h]
                qr = _rot(q, cc, ss)
                kr = _rot(k, cc, ss) * QK_SCALE
                qrb, krb = qr.astype(BF16), kr.astype(BF16)
                sb = st_ref[c, h]
                sd = (_dot(qrb, krb, NT) * dmat).astype(BF16)
                qdb = (qr * qd).astype(BF16)
                kdb = (kr * kd).astype(BF16)
                o = _dot(sd, v, NN) + _dot(qdb, sb, NN)
                mu = jnp.mean(o, axis=-1, keepdims=True)
                oc = o - mu
                rstd = lax.rsqrt(jnp.mean(oc * oc, axis=-1, keepdims=True) + EPS)
                rn = oc * rstd
                sg = 1.0 / (1.0 + jnp.exp(-gt))
                dgt = dyv * rn * (sg * (1.0 + gt * (1.0 - sg)))
                drn = dyv * (gt * sg)
                do = rstd * (drn - jnp.mean(drn, axis=-1, keepdims=True) - rn * jnp.mean(drn * rn, axis=-1, keepdims=True))
                dob = do.astype(BF16)
                dsn = dstate[h]
                dsnb = dsn.astype(BF16)
                ds_raw = (_dot(dob, v, NT) * dmat).astype(BF16)
                dv = _dot(sd, dob, TN) + _dot(kdb, dsnb, NN)
                dqr = _dot(ds_raw, krb, NN) + qd * _dot(dob, sb, NT)
                dkr = (_dot(ds_raw, qrb, TN) + kd * _dot(v, dsnb, NT)) * QK_SCALE
                dstate[h] = dsn * cd_ref[h] + _dot(qdb, dob, TN)
                dq = dqr * cc + pltpu.roll(dqr * ss, RET_QK_DIM // 2, 1)
                dk = dkr * cc + pltpu.roll(dkr * ss, RET_QK_DIM // 2, 1)
                dz_ref[rs, h * RET_QK_DIM:(h + 1) * RET_QK_DIM] = dq.astype(dz_ref.dtype)
                dz_ref[rs, QW + h * RET_QK_DIM:QW + (h + 1) * RET_QK_DIM] = dk.astype(dz_ref.dtype)
                dz_ref[rs, 2 * QW + h * RET_V_DIM:2 * QW + (h + 1) * RET_V_DIM] = dv.astype(dz_ref.dtype)
                dz_ref[rs, 2 * QW + VW + h * RET_V_DIM:2 * QW + VW + (h + 1) * RET_V_DIM] = dgt.astype(dz_ref.dtype)

    return pl.pallas_call(
        body,
        grid=(nb,),
        in_specs=[pl.BlockSpec((tr, RET_W), lambda n: (ix(n), 0)),
                  pl.BlockSpec((tr, VW), lambda n: (ix(n), 0)),
                  pl.BlockSpec((cpb, RET_HEADS, RET_QK_DIM, RET_V_DIM), lambda n: (ix(n), 0, 0, 0))] + tab_specs,
        out_specs=pl.BlockSpec((tr, RET_W), lambda n: (ix(n), 0)),
        out_shape=jax.ShapeDtypeStruct((T, RET_W), BF16),
        scratch_shapes=[pltpu.VMEM((RET_HEADS, RET_QK_DIM, RET_V_DIM), F32)],
        compiler_params=_cp("arbitrary"),
        name=name,
    )(z_a, dy, states, *tables)


ATT_TQ = 256
ATT_CPB = ATT_TQ // CHUNK
ATT_SCALE = ATT_HEAD_DIM ** -0.5


ATT_WIN = 3 * ATT_TQ
ATT_NB = CHUNK * ATT_BAND


def _rel_index():
    i = np.arange(CHUNK)[:, None]
    j = np.arange(ATT_BAND)[None, :]
    rel = np.clip(i + ATT_PAST * CHUNK - j, -MAX_REL, MAX_REL) + MAX_REL
    return jnp.asarray(rel.reshape(1, ATT_NB).astype(np.int32))


def _split3(x):
    hi = x.astype(BF16)
    r1 = x - hi.astype(F32)
    mid = r1.astype(BF16)
    lo = (r1 - mid.astype(F32)).astype(BF16)
    return hi, mid, lo


REL_TILE = 4608


def _bias_expand(rel_bias, name="bias_expand"):
    H = rel_bias.shape[0]
    n = ATT_NB
    padded = jnp.pad(rel_bias, ((0, 0), (0, N_REL_PAD - N_REL)))

    def body(rb_ref, idx_ref, o_ref):
        onehot = (lax.broadcasted_iota(jnp.int32, (N_REL_PAD, REL_TILE), 0) == idx_ref[...]).astype(BF16)
        hi, mid, lo = _split3(rb_ref[...])
        o_ref[...] = _dot(hi, onehot, NN) + _dot(mid, onehot, NN) + _dot(lo, onehot, NN)

    out = pl.pallas_call(
        body,
        grid=(n // REL_TILE,),
        in_specs=[pl.BlockSpec((H, N_REL_PAD), lambda t: (0, 0)), pl.BlockSpec((1, REL_TILE), lambda t: (0, t))],
        out_specs=pl.BlockSpec((H, REL_TILE), lambda t: (0, t)),
        out_shape=jax.ShapeDtypeStruct((H, n), F32),
        compiler_params=_cp("parallel"),
        name=name,
    )(padded, _rel_index())
    return out.reshape(H, CHUNK, ATT_BAND)


def _bias_tile(band, name="bias_tile"):
    H = band.shape[0]
    padded = jnp.pad(band, ((0, 0), (0, 0), (0, ATT_WIN - ATT_BAND)), constant_values=NEG_INF)

    def body(b_ref, o_ref):
        b = b_ref[...]
        col = lax.broadcasted_iota(jnp.int32, (CHUNK, ATT_WIN), 1)
        keep = col >= (2 - pl.program_id(0)) * ATT_TQ
        for a in range(ATT_CPB):
            o_ref[a * CHUNK:(a + 1) * CHUNK, :] = jnp.where(keep, pltpu.roll(b, a * CHUNK, 1) if a else b, NEG_INF)

    return pl.pallas_call(
        body,
        grid=(3, H),
        in_specs=[pl.BlockSpec((None, CHUNK, ATT_WIN), lambda v, h: (h, 0, 0))],
        out_specs=pl.BlockSpec((None, None, ATT_TQ, ATT_WIN), lambda v, h: (v, h, 0, 0)),
        out_shape=jax.ShapeDtypeStruct((3, H, ATT_TQ, ATT_WIN), F32),
        compiler_params=_cp("parallel", "parallel"),
        name=name,
    )(padded)


def _bias_untile(dtile, name="bias_untile"):
    H = dtile.shape[0]

    def body(d_ref, o_ref):
        acc = d_ref[0:CHUNK, :]
        for a in range(1, ATT_CPB):
            acc = acc + pltpu.roll(d_ref[a * CHUNK:(a + 1) * CHUNK, :], ATT_WIN - a * CHUNK, 1)
        o_ref[...] = acc

    out = pl.pallas_call(
        body,
        grid=(H,),
        in_specs=[pl.BlockSpec((None, ATT_TQ, ATT_WIN), lambda h: (h, 0, 0))],
        out_specs=pl.BlockSpec((None, CHUNK, ATT_WIN), lambda h: (h, 0, 0)),
        out_shape=jax.ShapeDtypeStruct((H, CHUNK, ATT_WIN), F32),
        compiler_params=_cp("parallel"),
        name=name,
    )(dtile)
    return out[:, :, :ATT_BAND]


def _bias_reduce(dbias, name="bias_reduce"):
    H = dbias.shape[0]
    n = ATT_NB

    def body(db_ref, idx_ref, o_ref):
        @pl.when(pl.program_id(0) == 0)
        def _():
            o_ref[...] = jnp.zeros_like(o_ref)

        onehot = (lax.broadcasted_iota(jnp.int32, (N_REL_PAD, REL_TILE), 0) == idx_ref[...]).astype(BF16)
        hi, mid, lo = _split3(db_ref[...])
        o_ref[...] += _dot(hi, onehot, NT) + _dot(mid, onehot, NT) + _dot(lo, onehot, NT)

    out = pl.pallas_call(
        body,
        grid=(n // REL_TILE,),
        in_specs=[pl.BlockSpec((H, REL_TILE), lambda t: (0, t)), pl.BlockSpec((1, REL_TILE), lambda t: (0, t))],
        out_specs=pl.BlockSpec((H, N_REL_PAD), lambda t: (0, 0)),
        out_shape=jax.ShapeDtypeStruct((H, N_REL_PAD), F32),
        compiler_params=_cp("arbitrary"),
        name=name,
    )(dbias.reshape(H, n), _rel_index())
    return out[:, :N_REL]


def _att_probs(q, kwin, bias):
    s = _dot(q, kwin, NT) + bias
    e = jnp.exp(s - jnp.max(s, axis=-1, keepdims=True))
    return e * (1.0 / jnp.sum(e, axis=-1, keepdims=True))


ATT_PAIR = 2 * ATT_HEAD_DIM
ATT_NP = ATT_HEADS // 2
ATT_QW = ATT_HEADS * ATT_HEAD_DIM
Y_COLS = RET_HEADS * RET_V_DIM + ATT_QW


def _att_specs(tq, nq, clip_q, q_col0):
    cb = ATT_QW // ATT_PAIR
    qi = (lambda p, m: (jnp.minimum(m, nq - 1), q_col0 + p)) if clip_q else (lambda p, m: (m, q_col0 + p))
    q = pl.BlockSpec((tq, ATT_PAIR), qi)

    def win(col0):
        return [pl.BlockSpec((tq, ATT_PAIR), functools.partial(lambda p, m, back: (jnp.clip(m - back, 0, nq - 1), col0 + p), back=b))
                for b in (2, 1, 0)]

    bias = pl.BlockSpec((None, 2, ATT_TQ, ATT_WIN), lambda p, m: (jnp.minimum(m, 2), p, 0, 0))
    return q, win(cb), win(2 * cb), bias


def _head_masks(rows):
    lane = lax.broadcasted_iota(jnp.int32, (rows, ATT_PAIR), 1)
    return lane < ATT_HEAD_DIM


def _att_fwd(z_b, bias, y, comm=None, name="att_fwd"):
    T = z_b.shape[0]
    tq = ATT_TQ
    nq = T // tq
    qs, kwin, vwin, bs = _att_specs(tq, nq, False, 0)
    nc = comm.n if comm else 0
    total = ATT_NP * nq

    def body(*refs):
        q_ref, k0, k1, k2, v0, v1, v2, b_ref = refs[:8]
        cin = refs[9:9 + nc]
        o_ref = refs[9 + nc]
        cout = refs[10 + nc:10 + 2 * nc]
        csem = refs[10 + 2 * nc:]
        m = pl.program_id(1)
        step = pl.program_id(0) * nq + m
        _host_comm(comm, "early", step, total, cin, cout, csem)
        kw = jnp.concatenate([k0[...], k1[...], k2[...]], axis=0)
        vw = jnp.concatenate([v0[...], v1[...], v2[...]], axis=0)
        q2 = q_ref[...] * ATT_SCALE
        even = _head_masks(tq)
        outs = []
        for hh in range(2):
            qm = jnp.where(even if hh == 0 else ~even, q2, jnp.zeros_like(q2))
            p = _att_probs(qm, kw, b_ref[hh])
            outs.append(_dot(p.astype(BF16), vw, NN))
        o_ref[...] = jnp.where(even, outs[0], outs[1]).astype(o_ref.dtype)
        _host_comm(comm, "late", step, total, cin, cout, csem)

    y_cb = (Y_COLS - ATT_QW) // ATT_PAIR
    out = pl.pallas_call(
        body,
        grid=(ATT_NP, nq),
        in_specs=[qs] + kwin + vwin + [bs, ANY] + [ANY] * nc,
        out_specs=[pl.BlockSpec((tq, ATT_PAIR), lambda p, m: (m, y_cb + p))] + [ANY] * nc,
        out_shape=[jax.ShapeDtypeStruct((T, Y_COLS), BF16)] + (comm.out_shape if comm else []),
        scratch_shapes=comm.scratch if comm else [],
        input_output_aliases={8: 0},
        compiler_params=_cp("arbitrary", "arbitrary"),
        name=name,
    )(z_b, z_b, z_b, z_b, z_b, z_b, z_b, bias, y, *(comm.arrays if comm else []))
    return out[0], list(out[1:])


def _att_bwd(z_b, bias, dy, comm=None, name="att_bwd"):
    T = z_b.shape[0]
    tq = ATT_TQ
    nq = T // tq
    y_cb = (Y_COLS - ATT_QW) // ATT_PAIR
    qs, kwin, vwin, bs = _att_specs(tq, nq, True, 0)
    dos = _att_specs(tq, nq, True, y_cb)[0]
    kv_out = pl.BlockSpec((tq, ATT_PAIR), lambda p, m: (jnp.maximum(m - 2, 0), p))
    W3 = 3 * tq
    nc = comm.n if comm else 0
    total = ATT_NP * (nq + 2)

    def body(*refs):
        q_ref, k0, k1, k2, v0, v1, v2, b_ref, do_ref = refs[:9]
        cin = refs[9:9 + nc]
        dq_ref, dk_ref, dv_ref, db_ref = refs[9 + nc:13 + nc]
        cout = refs[13 + nc:13 + 2 * nc]
        dkc, dvc, dkw, dvw = refs[13 + 2 * nc:17 + 2 * nc]
        csem = refs[17 + 2 * nc:]
        m = pl.program_id(1)
        step = pl.program_id(0) * (nq + 2) + m
        _host_comm(comm, "early", step, total, cin, cout, csem)

        @pl.when(m == 0)
        def _():
            dkc[...] = jnp.zeros_like(dkc)
            dvc[...] = jnp.zeros_like(dvc)
            db_ref[...] = jnp.zeros_like(db_ref)

        @pl.when(m >= nq)
        def _():
            dkw[...] = jnp.zeros_like(dkw)
            dvw[...] = jnp.zeros_like(dvw)

        @pl.when(m < nq)
        def _():
            kw = jnp.concatenate([k0[...], k1[...], k2[...]], axis=0)
            vw = jnp.concatenate([v0[...], v1[...], v2[...]], axis=0)
            q2, do2 = q_ref[...] * ATT_SCALE, do_ref[...]
            even = _head_masks(tq)
            dqs, dks, dvs = [], [], []
            for hh in range(2):
                mine = even if hh == 0 else ~even
                p = _att_probs(jnp.where(mine, q2, jnp.zeros_like(q2)), kw, b_ref[hh])
                dp = _dot(jnp.where(mine, do2, jnp.zeros_like(do2)), vw, NT)
                ds = p * (dp - jnp.sum(dp * p, axis=-1, keepdims=True))
                db_ref[hh] += ds
                dsb = ds.astype(BF16)
                dqs.append(_dot(dsb, kw, NN))
                dks.append(_dot(dsb, q2, TN))
                dvs.append(_dot(p.astype(BF16), do2, TN))
            even_w = _head_masks(W3)
            dq_ref[...] = (jnp.where(even, dqs[0], dqs[1]) * ATT_SCALE).astype(dq_ref.dtype)
            dkw[...] = jnp.where(even_w, dks[0], dks[1])
            dvw[...] = jnp.where(even_w, dvs[0], dvs[1])

        dk_ref[...] = (dkc[0:tq, :] + dkw[0:tq, :]).astype(dk_ref.dtype)
        dv_ref[...] = (dvc[0:tq, :] + dvw[0:tq, :]).astype(dv_ref.dtype)
        dkc[0:tq, :] = dkc[tq:2 * tq, :] + dkw[tq:2 * tq, :]
        dvc[0:tq, :] = dvc[tq:2 * tq, :] + dvw[tq:2 * tq, :]
        dkc[tq:2 * tq, :] = dkw[2 * tq:W3, :]
        dvc[tq:2 * tq, :] = dvw[2 * tq:W3, :]
        _host_comm(comm, "late", step, total, cin, cout, csem)

    qo = pl.BlockSpec((tq, ATT_PAIR), lambda p, m: (jnp.minimum(m, nq - 1), p))
    dbs = pl.BlockSpec((2, ATT_TQ, ATT_WIN), lambda p, m: (p, 0, 0))
    hd = jax.ShapeDtypeStruct((T, ATT_QW), BF16)
    out = pl.pallas_call(
        body,
        grid=(ATT_NP, nq + 2),
        in_specs=[qs] + kwin + vwin + [bs, dos] + [ANY] * nc,
        out_specs=[qo, kv_out, kv_out, dbs] + [ANY] * nc,
        out_shape=[hd, hd, hd, jax.ShapeDtypeStruct((ATT_HEADS, ATT_TQ, ATT_WIN), F32)] + (comm.out_shape if comm else []),
        scratch_shapes=[pltpu.VMEM((2 * tq, ATT_PAIR), F32), pltpu.VMEM((2 * tq, ATT_PAIR), F32),
                        pltpu.VMEM((W3, ATT_PAIR), F32), pltpu.VMEM((W3, ATT_PAIR), F32)] + (comm.scratch if comm else []),
        compiler_params=_cp("arbitrary", "arbitrary"),
        name=name,
    )(z_b, z_b, z_b, z_b, z_b, z_b, z_b, bias, dy, *(comm.arrays if comm else []))
    return out[0], out[1], out[2], out[3], list(out[4:])


REST = ["ab_out", "c_in_t", "c_out", "up_t0", "up_t1", "down0", "down1"]


def _local_step(x, target, wt, small, rest_shards=None, overlap=False, hn0=None):
    T = x.shape[0]
    Fh = FFN_HIDDEN
    tables = _ret_tables(T)
    gw, gs, recv = {}, {}, {}
    wt = dict(wt)

    if hn0 is None:
        hn0 = _rms_fwd(x, small["attn_norm_g"][0], name="rms_fwd")
    z_a = _mm(hn0, wt["ab_in_t"][:RET_W], "nt", BF16, name="mm_ab_in_a")
    z_b = _mm(hn0, wt["ab_in_t"][RET_W:], "nt", BF16, name="mm_ab_in_b")
    y, states = _ret_fwd(z_a, tables)
    bias = _bias_tile(_bias_expand(small["rel_bias"]))
    y, rest = _att_fwd(z_b, bias, y, comm=_Comm("gather", rest_shards) if rest_shards is not None else None)
    if rest_shards is not None:
        full = dict(zip(REST, rest))
        wt.update(ab_out=full["ab_out"], c_in_t=full["c_in_t"], c_out=full["c_out"],
                  up_t=[full["up_t0"], full["up_t1"]], down=[full["down0"], full["down1"]])
    h1, hf0 = _mm_rows(y, wt["ab_out"], x, "mm_ab_out", norm_g=small["ffn_norm_g"][0])

    def ffn_fwd(h, hf, layer, next_g):
        zg, zu, a = _ffn_up_mid(hf, wt["up_t"][layer], small["conv_w"][layer], small["conv_b"][layer][None, :])
        if next_g is None:
            return _mm_rows(a, wt["down"][layer], h, "mm_down_loss", loss=(target, small["final_g"])), (hf, zg, zu, a)
        h_out, hn_next = _mm_rows(a, wt["down"][layer], h, "mm_down", norm_g=next_g)
        return h_out, hn_next, (hf, zg, zu, a)

    def ffn_bwd(dh_out, h, layer, saved, exchange=None):
        hf, zg, zu, a = saved
        da = _mm(dh_out, wt["down"][layer], "nt", BF16, name="mm_d_a")
        d_down = _mm(a, dh_out, "tn", BF16, name="mm_dw_down")
        comm = _Comm("exchange", [gw[k] for k in exchange]) if exchange else None
        dzg, dzu, dcw, dcb, got = _ffn_mid_bwd(zg, zu, da, small["conv_w"][layer], small["conv_b"][layer][None, :], comm=comm)
        recv.update(zip(exchange or [], got))
        dh, dg = _mm_rows(dzg, wt["up_t"][layer][:Fh], None, "mm_d_hf_norm", second=(dzu, wt["up_t"][layer][Fh:]),
                          bwd=(h, small["ffn_norm_g"][layer], dh_out))
        d_up = _mm(dzg, hf, "tn", BF16, name="mm_dw_up", rows=(2 * Fh, 0, None))
        d_up = _mm(dzu, hf, "tn", BF16, name="mm_dw_up_2", rows=(2 * Fh, Fh, d_up))
        return dh, dg, d_up, d_down, dcw, dcb

    h2, hn1, ffn0 = ffn_fwd(h1, hf0, 0, small["attn_norm_g"][1])

    zz = _mm(hn1, wt["c_in_t"], "nt", BF16, name="mm_c_in")
    ys = _sgu_fwd(zz, small["ln_g"], small["ln_b"], small["w_s"], small["b_s"])
    h3, hf1 = _mm_rows(ys, wt["c_out"], h2, "mm_c_out", norm_g=small["ffn_norm_g"][1])
    (loss_vec, dh4, gs["final_g"]), ffn1 = ffn_fwd(h3, hf1, 1, None)

    dh3, dgf1, gw["up_t1"], gw["down1"], dcw1, dcb1 = ffn_bwd(dh4, h3, 1, ffn1)
    dys = _mm(dh3, wt["c_out"], "nt", BF16, name="mm_d_ys")
    gw["c_out"] = _mm(ys, dh3, "tn", BF16, name="mm_dw_c_out")
    dzz, gs["w_s"], dbs, dlg, dlb = _sgu_bwd(zz, dys, small["ln_g"], small["ln_b"], small["w_s"], small["b_s"])
    gs["b_s"], gs["ln_g"], gs["ln_b"] = dbs[:, :, 0], dlg[0], dlb[0]
    dh2, dga1 = _mm_rows(dzz, wt["c_in_t"], None, "mm_d_hn1_norm", bwd=(h2, small["attn_norm_g"][1], dh3))
    gw["c_in_t"] = _mm(dzz, hn1, "tn", BF16, name="mm_dw_c_in")

    dh1, dgf0, gw["up_t0"], gw["down0"], dcw0, dcb0 = ffn_bwd(
        dh2, h1, 0, ffn0, exchange=["c_in_t", "c_out", "up_t1", "down1"] if overlap else None)

    dy = _mm(dh1, wt["ab_out"], "nt", BF16, name="mm_d_y")
    gw["ab_out"] = _mm(y, dh1, "tn", BF16, name="mm_dw_ab_out")
    dz_a = _ret_bwd(z_a, dy, states, tables)
    late = ["ab_out", "up_t0", "down0"] if overlap else []
    dq, dk, dv, dbias, got = _att_bwd(z_b, bias, dy, comm=_Comm("exchange", [gw[k] for k in late]) if late else None)
    recv.update(zip(late, got))
    dz_b = jnp.concatenate([dq, dk, dv], axis=1)
    gs["rel_bias"] = _bias_reduce(_bias_untile(dbias))
    gw["ab_in_t"] = _mm(dz_a, hn0, "tn", BF16, name="mm_dw_ab_in_a", rows=(RET_W + ATT_W, 0, None))
    gw["ab_in_t"] = _mm(dz_b, hn0, "tn", BF16, name="mm_dw_ab_in_b", rows=(RET_W + ATT_W, RET_W, gw["ab_in_t"]))
    if overlap:
        dhn0, got = _mm(dz_a, wt["ab_in_t"][:RET_W], "nn", F32, name="mm_d_hn0", comm=_Comm("exchange", [gw["ab_in_t"]]))
        recv["ab_in_t"] = got[0]
    else:
        dhn0 = _mm(dz_a, wt["ab_in_t"][:RET_W], "nn", F32, name="mm_d_hn0")
    grad_x, dga0 = _mm_rows(dz_b, wt["ab_in_t"][RET_W:], dhn0, "mm_d_hn0_norm", bwd=(x, small["attn_norm_g"][0], dh1))

    gs["attn_norm_g"] = jnp.concatenate([dga0, dga1], axis=0)
    gs["ffn_norm_g"] = jnp.concatenate([dgf0, dgf1], axis=0)
    gs["conv_w"] = jnp.stack([dcw0, dcw1])
    gs["conv_b"] = jnp.concatenate([dcb0, dcb1], axis=0)
    gs["final_g"] = gs["final_g"][0]
    return loss_vec[0, 0], grad_x, gw, gs, recv


MESH_ID = pl.DeviceIdType.MESH
ANY = pl.BlockSpec(memory_space=pl.ANY)


def _my_place():
    return lax.axis_index("x"), lax.axis_index("y"), lax.axis_index("c")


class _Comm:
    def __init__(self, kind, arrays):
        self.kind, self.arrays, self.n = kind, list(arrays), len(arrays)
        if kind == "gather":
            self.out_shape = [jax.ShapeDtypeStruct((N_DEV * s.shape[0], s.shape[1]), s.dtype) for s in arrays]
        else:
            self.out_shape = [jax.ShapeDtypeStruct((N_DEV, g.shape[0] // N_DEV, g.shape[1]), g.dtype) for g in arrays]
        n = self.n
        self.scratch = [pltpu.SemaphoreType.DMA((n, 7)), pltpu.SemaphoreType.DMA((n, 7)), pltpu.SemaphoreType.DMA((n,))]

    def phase(self, ph, in_refs, out_refs, sems):
        (self._gather if self.kind == "gather" else self._exchange)(ph, in_refs, out_refs, sems)

    def _gather(self, ph, x_refs, o_refs, sems):
        n = self.n
        send_sems, recv_sems, local_sems = sems
        x, y, c = _my_place()
        me, sibling = (x, y, c), (x, y, 1 - c)
        chips = [(1 - x, y), (x, 1 - y), (1 - x, 1 - y)]

        def rows(a, place):
            m = x_refs[a].shape[0]
            px, py, pc = place
            return o_refs[a].at[pl.ds((4 * px + 2 * py + pc) * m, m), :]

        def copy(a, k, block, to, own=False):
            return pltpu.make_async_remote_copy(
                src_ref=x_refs[a] if own else rows(a, block), dst_ref=rows(a, block),
                send_sem=send_sems.at[a, k], recv_sem=recv_sems.at[a, k], device_id=to, device_id_type=MESH_ID)

        def mine():
            return [pltpu.make_async_copy(x_refs[a], rows(a, me), local_sems.at[a]) for a in range(n)]

        def first():
            out = []
            for a in range(n):
                out.append(copy(a, 0, me, sibling, own=True))
                out += [copy(a, 1 + j, me, (*chip, c), own=True) for j, chip in enumerate(chips)]
            return out

        def passed():
            return [copy(a, 4 + j, (*chip, c), sibling) for j, chip in enumerate(chips) for a in range(n)]

        if ph == 0:
            for cp in mine() + first():
                cp.start()
        elif ph == 1:
            fw = passed()
            for j, chip in enumerate(chips):
                for a in range(n):
                    copy(a, 1 + j, (*chip, c), me).wait_recv()
                    fw[j * n + a].start()
        else:
            for a in range(n):
                copy(a, 0, sibling, me).wait_recv()
                for j, chip in enumerate(chips):
                    copy(a, 4 + j, (*chip, 1 - c), me).wait_recv()
            for cp in first() + passed():
                cp.wait_send()
            for cp in mine():
                cp.wait()

    def _exchange(self, ph, g_refs, o_refs, sems):
        n = self.n
        send_sems, recv_sems, local_sems = sems
        x, y, c = _my_place()
        me = 4 * x + 2 * y + c
        peers = [(x ^ ((k >> 2) & 1), y ^ ((k >> 1) & 1), c ^ (k & 1)) for k in range(1, N_DEV)]

        def block(a, idx):
            m = g_refs[a].shape[0] // N_DEV
            return g_refs[a].at[pl.ds(idx * m, m), :]

        def copy(a, k, slot):
            px, py, pc = peers[k]
            return pltpu.make_async_remote_copy(
                src_ref=block(a, 4 * px + 2 * py + pc), dst_ref=o_refs[a].at[slot],
                send_sem=send_sems.at[a, k], recv_sem=recv_sems.at[a, k], device_id=peers[k], device_id_type=MESH_ID)

        if ph == 1:
            return
        mine = [pltpu.make_async_copy(block(a, me), o_refs[a].at[me], local_sems.at[a]) for a in range(n)]
        sends = [copy(a, k, me) for k in range(N_DEV - 1) for a in range(n)]
        if ph == 0:
            for cp in mine + sends:
                cp.start()
        else:
            for k in range(N_DEV - 1):
                px, py, pc = peers[k]
                for a in range(n):
                    copy(a, k, 4 * px + 2 * py + pc).wait_recv()
            for cp in sends:
                cp.wait_send()
            for cp in mine:
                cp.wait()


def _comm_call(comm, name):
    n = comm.n

    def body(*refs):
        for ph in range(3):
            comm.phase(ph, refs[:n], refs[n:2 * n], refs[2 * n:])

    return pl.pallas_call(
        body, out_shape=comm.out_shape, in_specs=[ANY] * n, out_specs=[ANY] * n, scratch_shapes=comm.scratch, name=name,
    )(*comm.arrays)


def _host_comm(comm, when, step, total, cin, cout, csem):
    if comm is None:
        return
    sched = {0: 0, 1: (3 * total) // 4, 2: total - 1}
    phases = (0, 1) if when == "early" else (2,)
    for ph in phases:
        if ph == 1 and comm.kind == "exchange":
            continue

        @pl.when(step == sched[ph])
        def _(ph=ph):
            comm.phase(ph, cin, cout, csem)


def _all_gather(shards, name="all_gather"):
    return _comm_call(_Comm("gather", shards), name)


def _row_tile(r, target=256):
    best = None
    for t in range(8, min(r, target) + 1, 8):
        if r % t == 0:
            best = t
    return best if best is not None else r


def _sum8(parts, name="sum8"):
    _, M, N = parts.shape
    tr = _row_tile(M, 128)

    def body(p_ref, o_ref):
        acc = p_ref[0].astype(F32)
        for d in range(1, N_DEV):
            acc = acc + p_ref[d].astype(F32)
        o_ref[...] = acc

    return pl.pallas_call(
        body,
        grid=(M // tr,),
        in_specs=[pl.BlockSpec((N_DEV, tr, N), lambda i: (0, i, 0))],
        out_specs=pl.BlockSpec((tr, N), lambda i: (i, 0)),
        out_shape=jax.ShapeDtypeStruct((M, N), F32),
        compiler_params=_cp("parallel"),
        name=name,
    )(parts)


def _adamw(w, g, m, v, name="adamw"):
    shape = w.shape
    if w.ndim == 1:
        r2 = (1, shape[0])
    else:
        r2 = (int(np.prod(shape[:-1])), shape[-1])
    R, C = r2
    tr = _row_tile(R)
    bc1 = 1.0 - ADAM_B1 ** ADAM_STEP
    bc2 = 1.0 - ADAM_B2 ** ADAM_STEP

    def body(w_ref, g_ref, m_ref, v_ref, d_ref, nm_ref, nv_ref):
        gv = g_ref[...]
        nm = ADAM_B1 * m_ref[...] + (1.0 - ADAM_B1) * gv
        nv = ADAM_B2 * v_ref[...] + (1.0 - ADAM_B2) * (gv * gv)
        d_ref[...] = -ADAM_LR * ((nm / bc1) / (jnp.sqrt(nv / bc2) + ADAM_EPS) + ADAM_WD * w_ref[...])
        nm_ref[...] = nm
        nv_ref[...] = nv

    spec = pl.BlockSpec((tr, C), lambda i: (i, 0))
    out = pl.pallas_call(
        body,
        grid=(R // tr,),
        in_specs=[spec] * 4,
        out_specs=[spec] * 3,
        out_shape=[jax.ShapeDtypeStruct(r2, F32)] * 3,
        compiler_params=_cp("parallel"),
        name=name,
    )(w.reshape(r2), g.reshape(r2), m.reshape(r2), v.reshape(r2))
    return [o.reshape(shape) for o in out]


WEIGHTS = ['attn_norm_g', 'ffn_norm_g', 'ab_w_in', 'ab_w_out', 'ab_rel_bias', 'c_w_in', 'c_ln_g', 'c_ln_b', 'c_w_s', 'c_b_s',
           'c_w_out', 'ffn_w_up', 'ffn_conv_w', 'ffn_conv_b', 'ffn_w_down', 'final_norm_g']
SMALL_ORDER = ["attn_norm_g", "ffn_norm_g", "rel_bias", "ln_g", "ln_b", "w_s", "b_s", "conv_w", "conv_b", "final_g"]
PACK_ROW = 1024


def _pack(arrs):
    flat = jnp.concatenate([a.reshape(-1) for a in arrs])
    n = flat.shape[0]
    padded = -(-n // PACK_ROW) * PACK_ROW
    return jnp.pad(flat, (0, padded - n)).reshape(padded // 128, 128)


def _unpack(flat, shapes):
    out, off = [], 0
    for s in shapes:
        n = int(np.prod(s))
        out.append(flat[off:off + n].reshape(s))
        off += n
    return out


def _step(P):
    x, target = P["x"][0], P["loss_target"][0]
    me = 4 * lax.axis_index("x") + 2 * lax.axis_index("y") + lax.axis_index("c")
    n_up = P["ffn_w_up"].shape[0]
    Fc = P["ffn_conv_w"].shape[-1]
    Lc = P["c_ln_g"].shape[-1]

    first = _Comm("gather", [P["ab_w_in"][0].T.astype(BF16), _pack([P["ffn_conv_w"], P["c_ln_g"], P["c_ln_b"]])])
    hn0, full = _rms_fwd(x, P["attn_norm_g"][0], comm=first, name="rms_fwd_gather")
    wt = {"ab_in_t": full[0]}
    rest = {"ab_out": P["ab_w_out"][0], "c_in_t": P["c_w_in"][0].T, "c_out": P["c_w_out"][0],
            "up_t0": P["ffn_w_up"][0].T, "up_t1": P["ffn_w_up"][1].T, "down0": P["ffn_w_down"][0], "down1": P["ffn_w_down"][1]}
    rest_shards = [rest[k].astype(BF16) for k in REST]
    sm = full[-1].reshape(N_DEV, -1)
    conv_w = sm[:, :n_up * 3 * Fc].reshape(N_DEV, n_up, 3, Fc).transpose(1, 2, 0, 3).reshape(n_up, 3, N_DEV * Fc)
    off = n_up * 3 * Fc
    ln_g = sm[:, off:off + Lc].reshape(N_DEV * Lc)
    ln_b = sm[:, off + Lc:off + 2 * Lc].reshape(N_DEV * Lc)
    small = {"attn_norm_g": P["attn_norm_g"], "ffn_norm_g": P["ffn_norm_g"], "rel_bias": P["ab_rel_bias"][0],
             "ln_g": ln_g, "ln_b": ln_b, "w_s": P["c_w_s"][0], "b_s": P["c_b_s"][0], "conv_w": conv_w,
             "conv_b": P["ffn_conv_b"], "final_g": P["final_norm_g"]}

    loss_part, grad_x, gw, gs, recv = _local_step(x, target, wt, small, rest_shards=rest_shards, overlap=True, hn0=hn0)
    loss = lax.psum(loss_part, ("x", "y", "c"))

    s8 ={k: _sum8(recv[k], name="sum8") for k in ["ab_in_t"] + REST}
    g_big = {"ab_w_in": s8["ab_in_t"].T[None], "ab_w_out": s8["ab_out"][None], "c_w_in": s8["c_in_t"].T[None],
             "c_w_out": s8["c_out"][None], "ffn_w_up": jnp.stack([s8["up_t0"].T, s8["up_t1"].T]),
             "ffn_w_down": jnp.stack([s8["down0"], s8["down1"]])}

    packed = _pack([gs[k] for k in SMALL_ORDER])
    gathered = _all_gather([packed], name="gather_small_grads")[0]
    tot = _sum8(gathered.reshape(N_DEV, packed.shape[0], 128), name="sum8_small").reshape(-1)
    gsm = dict(zip(SMALL_ORDER, _unpack(tot, [gs[k].shape for k in SMALL_ORDER])))
    grads = dict(g_big)
    grads["attn_norm_g"] = gsm["attn_norm_g"]
    grads["ffn_norm_g"] = gsm["ffn_norm_g"]
    grads["ab_rel_bias"] = gsm["rel_bias"][None]
    grads["c_ln_g"] = lax.dynamic_slice(gsm["ln_g"], (me * Lc,), (Lc,))[None]
    grads["c_ln_b"] = lax.dynamic_slice(gsm["ln_b"], (me * Lc,), (Lc,))[None]
    grads["c_w_s"] = gsm["w_s"][None]
    grads["c_b_s"] = gsm["b_s"][None]
    grads["ffn_conv_w"] = lax.dynamic_slice(gsm["conv_w"], (0, 0, me * Fc), (n_up, 3, Fc))
    grads["ffn_conv_b"] = gsm["conv_b"]
    grads["final_norm_g"] = gsm["final_g"]

    delta, new_m, new_v = {}, {}, {}
    for k in WEIGHTS:
        delta[k], new_m[k], new_v[k] = _adamw(P[k], grads[k], P["m_" + k], P["v_" + k], name="adamw")
    return (loss, grad_x[None], *[grads[k] for k in WEIGHTS], *[delta[k] for k in WEIGHTS],
            *[new_m[k] for k in WEIGHTS], *[new_v[k] for k in WEIGHTS])


def kernel(x, attn_norm_g, ffn_norm_g, ab_w_in, ab_w_out, ab_rel_bias, c_w_in, c_ln_g, c_ln_b, c_w_s, c_b_s, c_w_out, ffn_w_up, ffn_conv_w, ffn_conv_b, ffn_w_down, final_norm_g, loss_target, m_attn_norm_g, m_ffn_norm_g, m_ab_w_in, m_ab_w_out, m_ab_rel_bias, m_c_w_in, m_c_ln_g, m_c_ln_b, m_c_w_s, m_c_b_s, m_c_w_out, m_ffn_w_up, m_ffn_conv_w, m_ffn_conv_b, m_ffn_w_down, m_final_norm_g, v_attn_norm_g, v_ffn_norm_g, v_ab_w_in, v_ab_w_out, v_ab_rel_bias, v_c_w_in, v_c_ln_g, v_c_ln_b, v_c_w_s, v_c_b_s, v_c_w_out, v_ffn_w_up, v_ffn_conv_w, v_ffn_conv_b, v_ffn_w_down, v_final_norm_g):
    return _step(dict(locals()))
```

```python
import functools

import numpy as np
import jax
import jax.numpy as jnp
from jax import lax
from jax.experimental import pallas as pl
from jax.experimental.pallas import tpu as pltpu

F32 = jnp.float32
BF16 = jnp.bfloat16

D_MODEL = 1024
CHUNK = 64
EPS = 1e-6
NEG_INF = -1e30
RET_HEADS = 4
RET_QK_DIM = 128
RET_V_DIM = 256
ATT_HEADS = 8
ATT_HEAD_DIM = 64
ATT_PAST = 8
ATT_BAND = (ATT_PAST + 1) * CHUNK
MAX_REL = 128
N_REL = 2 * MAX_REL + 1
N_REL_PAD = 384
SGU_BLOCK = 128
SGU_GROUPS = 8
SGU_WIDTH = 2048
SGU_GW = SGU_WIDTH // SGU_GROUPS
FFN_HIDDEN = 2816
RET_W = 2 * RET_HEADS * RET_QK_DIM + 2 * RET_HEADS * RET_V_DIM
ATT_W = 3 * ATT_HEADS * ATT_HEAD_DIM
N_DEV = 8

ADAM_LR = 0.001
ADAM_B1 = 0.9
ADAM_B2 = 0.999
ADAM_EPS = 1e-08
ADAM_WD = 0.01
ADAM_STEP = 10

VMEM_LIMIT = 52 * 1024 * 1024


def _cp(*sem):
    return pltpu.CompilerParams(dimension_semantics=sem if sem else None, vmem_limit_bytes=VMEM_LIMIT)


def _tile(n, target):
    if n <= target:
        return n
    best = None
    for t in range(128, target + 1, 128):
        if n % t == 0:
            best = t
    assert best is not None, (n, target)
    return best


def _gelu(x):
    c = 0.7978845608028654
    return 0.5 * x * (1.0 + jnp.tanh(c * (x + 0.044715 * x * x * x)))


def _gelu_and_grad(x):
    c = 0.7978845608028654
    x2 = x * x
    t = jnp.tanh(c * (x + 0.044715 * x * x2))
    cdf = 0.5 * (1.0 + t)
    grad = cdf + x * (0.5 * c) * (1.0 - t * t) * (1.0 + 3.0 * 0.044715 * x2)
    return x * cdf, grad


def _dot(a, b, dims):
    return lax.dot_general(a, b, (dims, ((), ())), preferred_element_type=F32)


NN = ((1,), (0,))
NT = ((1,), (1,))
TN = ((0,), (0,))


def _mm(a, b, mode, out_dtype, res=None, name="mm", tm_t=None, tn_t=None, tk_t=None, comm=None, rows=None):
    if mode == "nt":
        (M, K), N = a.shape, b.shape[0]
        dm, dn, dk = (1024, 2816, K) if N <= 2816 else (1024, 2048, K)
    elif mode == "nn":
        (M, K), N = a.shape, b.shape[1]
        dm, dn, dk = (1024 if K <= 3072 else 512), 1024, K
    else:
        (K, M), N = a.shape, b.shape[1]
        dm, dn, dk = 1536, 1024, 2048
    tm, tn, tk = _tile(M, tm_t or dm), _tile(N, tn_t or dn), _tile(K, tk_t or dk)
    nk = K // tk
    dims = {"nt": NT, "nn": NN, "tn": TN}[mode]
    a_spec = pl.BlockSpec((tk, tm), lambda i, j, k: (k, i)) if mode == "tn" else pl.BlockSpec((tm, tk), lambda i, j, k: (i, k))
    b_spec = pl.BlockSpec((tn, tk), lambda i, j, k: (j, k)) if mode == "nt" else pl.BlockSpec((tk, tn), lambda i, j, k: (k, j))
    gi, gj = M // tm, N // tn
    out_rows, row0, into = rows if rows else (M, 0, None)
    assert row0 % tm == 0 and not (comm and into is not None)
    o_spec = pl.BlockSpec((tm, tn), lambda i, j, k: (i + row0 // tm, j))
    has_res = res is not None
    nc = 1 if into is not None else (comm.n if comm else 0)
    n_in = 3 if has_res else 2

    def body(*refs):
        a_ref, b_ref = refs[:2]
        r_ref = refs[2] if has_res else None
        nco = comm.n if comm else 0
        cin = refs[n_in:n_in + nc]
        o_ref = refs[n_in + nc]
        cout = refs[n_in + nc + 1:n_in + nc + 1 + nco]
        scratch = refs[n_in + nc + 1 + nco:]
        csem = scratch[1:] if nk > 1 else scratch
        step = (pl.program_id(0) * gj + pl.program_id(1)) * nk + pl.program_id(2)
        _host_comm(comm, "early", step, gi * gj * nk, cin, cout, csem)
        p = _dot(a_ref[...].astype(BF16), b_ref[...].astype(BF16), dims)
        if nk == 1:
            if has_res:
                p = p + r_ref[...]
            o_ref[...] = p.astype(out_dtype)
        else:
            acc = scratch[0]
            k = pl.program_id(2)

            @pl.when(k == 0)
            def _():
                acc[...] = p

            @pl.when(k > 0)
            def _():
                acc[...] += p

            @pl.when(k == nk - 1)
            def _():
                t = acc[...]
                if has_res:
                    t = t + r_ref[...]
                o_ref[...] = t.astype(out_dtype)
        _host_comm(comm, "late", step, gi * gj * nk, cin, cout, csem)

    in_specs = [a_spec, b_spec] + ([o_spec] if has_res else []) + [ANY] * nc
    args = (a, b) + ((res,) if has_res else ()) + ((into,) if into is not None else tuple(comm.arrays if comm else ()))
    out = pl.pallas_call(
        body,
        grid=(gi, gj, nk),
        in_specs=in_specs,
        out_specs=[o_spec] + ([ANY] * nc if comm else []),
        out_shape=[jax.ShapeDtypeStruct((out_rows, N), out_dtype)] + (comm.out_shape if comm else []),
        scratch_shapes=([pltpu.VMEM((tm, tn), F32)] if nk > 1 else []) + (comm.scratch if comm else []),
        input_output_aliases={n_in: 0} if into is not None else {},
        compiler_params=_cp("arbitrary", "arbitrary", "arbitrary") if comm else _cp("parallel", "parallel", "arbitrary"),
        name=name,
    )(*args)
    return (out[0], list(out[1:])) if comm else out[0]


def _mm_rows(a, b, res, name, norm_g=None, bwd=None, loss=None, second=None, tm=512):
    M, K = a.shape
    Dm = b.shape[1]
    if norm_g is not None and K <= 2048:
        tm = 2 * tm
    tm = min(tm, M)
    row = pl.BlockSpec((tm, Dm), lambda i: (i, 0))
    vec = pl.BlockSpec((1, Dm), lambda i: (0, 0))
    a_spec = pl.BlockSpec((tm, K), lambda i: (i, 0))
    b_spec = pl.BlockSpec((K, Dm), lambda i: (0, 0))

    if loss is not None:
        target, g = loss

        def body(a_ref, b_ref, r_ref, g_ref, t_ref, loss_ref, dh_ref, dg_ref):
            @pl.when(pl.program_id(0) == 0)
            def _():
                loss_ref[...] = jnp.zeros_like(loss_ref)
                dg_ref[...] = jnp.zeros_like(dg_ref)

            x = _dot(a_ref[...].astype(BF16), b_ref[...].astype(BF16), NN) + r_ref[...]
            gv = g_ref[...]
            r = lax.rsqrt(jnp.mean(x * x, axis=-1, keepdims=True) + EPS)
            xhat = x * r
            e = xhat * gv - t_ref[...]
            loss_ref[...] += jnp.full((1, 128), 0.5 / Dm, F32) * jnp.sum(e * e)
            dy = e * (1.0 / Dm)
            dg_ref[...] += jnp.sum(dy * xhat, axis=0, keepdims=True)
            dx = dy * gv
            m = jnp.mean(dx * xhat, axis=-1, keepdims=True)
            dh_ref[...] = r * (dx - xhat * m)

        return pl.pallas_call(
            body, grid=(M // tm,), in_specs=[a_spec, b_spec, row, vec, row],
            out_specs=[pl.BlockSpec((1, 128), lambda i: (0, 0)), row, vec],
            out_shape=[jax.ShapeDtypeStruct((1, 128), F32), jax.ShapeDtypeStruct((M, Dm), F32), jax.ShapeDtypeStruct((1, Dm), F32)],
            compiler_params=_cp("arbitrary"), name=name,
        )(a, b, res, g.reshape(1, Dm), target)

    if bwd is None:
        def body(a_ref, b_ref, r_ref, g_ref, o_ref, n_ref):
            t = _dot(a_ref[...].astype(BF16), b_ref[...].astype(BF16), NN) + r_ref[...]
            o_ref[...] = t
            r = lax.rsqrt(jnp.mean(t * t, axis=-1, keepdims=True) + EPS)
            n_ref[...] = (t * r * g_ref[...]).astype(n_ref.dtype)

        return pl.pallas_call(
            body, grid=(M // tm,), in_specs=[a_spec, b_spec, row, vec], out_specs=[row, row],
            out_shape=[jax.ShapeDtypeStruct((M, Dm), F32), jax.ShapeDtypeStruct((M, Dm), BF16)],
            compiler_params=_cp("parallel"), name=name,
        )(a, b, res, norm_g.reshape(1, Dm))

    h, g, dres = bwd
    has_res = res is not None
    has2 = second is not None

    def body(*refs):
        a_ref, b_ref = refs[:2]
        h_ref, g_ref, dres_ref, dh_ref, dg_ref = refs[-5:]

        @pl.when(pl.program_id(0) == 0)
        def _():
            dg_ref[...] = jnp.zeros_like(dg_ref)

        d = _dot(a_ref[...].astype(BF16), b_ref[...].astype(BF16), NN)
        if has2:
            d = d + _dot(refs[2][...].astype(BF16), refs[3][...].astype(BF16), NN)
        if has_res:
            d = d + refs[4 if has2 else 2][...]
        x = h_ref[...]
        r = lax.rsqrt(jnp.mean(x * x, axis=-1, keepdims=True) + EPS)
        xhat = x * r
        dg_ref[...] += jnp.sum(d * xhat, axis=0, keepdims=True)
        dx = d * g_ref[...]
        m = jnp.mean(dx * xhat, axis=-1, keepdims=True)
        dh_ref[...] = dres_ref[...] + r * (dx - xhat * m)

    second_specs = [pl.BlockSpec((tm, second[0].shape[1]), lambda i: (i, 0)),
                    pl.BlockSpec(second[1].shape, lambda i: (0, 0))] if has2 else []
    return pl.pallas_call(
        body, grid=(M // tm,), in_specs=[a_spec, b_spec] + second_specs + ([row] if has_res else []) + [row, vec, row],
        out_specs=[row, vec],
        out_shape=[jax.ShapeDtypeStruct((M, Dm), F32), jax.ShapeDtypeStruct((1, Dm), F32)],
        compiler_params=_cp("arbitrary"), name=name,
    )(a, b, *(second if has2 else ()), *((res,) if has_res else ()), h, g.reshape(1, Dm), dres)


def _rms_fwd(h, g, comm=None, name="rms_fwd", tm=512):
    T, Dm = h.shape
    tm = min(tm, T)
    nc = comm.n if comm else 0
    ni = T // tm

    def body(*refs):
        h_ref, g_ref = refs[:2]
        cin, o_ref, cout, csem = refs[2:2 + nc], refs[2 + nc], refs[3 + nc:3 + 2 * nc], refs[3 + 2 * nc:]
        step = pl.program_id(0)
        _host_comm(comm, "early", step, ni, cin, cout, csem)
        x = h_ref[...]
        r = lax.rsqrt(jnp.mean(x * x, axis=-1, keepdims=True) + EPS)
        o_ref[...] = (x * r * g_ref[...]).astype(o_ref.dtype)
        _host_comm(comm, "late", step, ni, cin, cout, csem)

    out = pl.pallas_call(
        body,
        grid=(ni,),
        in_specs=[pl.BlockSpec((tm, Dm), lambda i: (i, 0)), pl.BlockSpec((1, Dm), lambda i: (0, 0))] + [ANY] * nc,
        out_specs=[pl.BlockSpec((tm, Dm), lambda i: (i, 0))] + [ANY] * nc,
        out_shape=[jax.ShapeDtypeStruct((T, Dm), BF16)] + (comm.out_shape if comm else []),
        scratch_shapes=comm.scratch if comm else [],
        compiler_params=_cp("arbitrary"),
        name=name,
    )(h, g.reshape(1, Dm), *(comm.arrays if comm else []))
    return (out[0], list(out[1:])) if comm else out[0]


HALO = 16


def _conv3(ext, w_ref, b_ref):
    return w_ref[0:1, :] * pltpu.roll(ext, 2, 0) + w_ref[1:2, :] * pltpu.roll(ext, 1, 0) + w_ref[2:3, :] * ext + b_ref[...]


def _ffn_up_mid(hf, w_up_t, cw, cb, name="ffn_up_mid", tm=512, tc=1408):
    T, Dm = hf.shape
    Fh = w_up_t.shape[0] // 2
    tm = min(tm, T)
    nj, ni = Fh // tc, T // tm

    rc = min(512, tm)

    def body(h_ref, wg_ref, wu_ref, cwg_ref, cwu_ref, cbg_ref, cbu_ref, zg_ref, zu_ref, a_ref, eg_ref, eu_ref, ng_ref, nu_ref):
        @pl.when(pl.program_id(1) == 0)
        def _():
            eg_ref[...] = jnp.zeros_like(eg_ref)
            eu_ref[...] = jnp.zeros_like(eu_ref)

        for r in range(tm // rc):
            ext = slice(r * rc, r * rc + HALO + rc)
            cg = _conv3(eg_ref[ext, :].astype(F32), cwg_ref, cbg_ref)[HALO:]
            cu = _conv3(eu_ref[ext, :].astype(F32), cwu_ref, cbu_ref)[HALO:]
            a_ref[r * rc:(r + 1) * rc, :] = _gelu(cg.astype(BF16)) * cu.astype(BF16)
            h = h_ref[r * rc:(r + 1) * rc, :]
            for w_ref, z_ref, n_ref in ((wg_ref, zg_ref, ng_ref), (wu_ref, zu_ref, nu_ref)):
                z = _dot(h, w_ref[...], NT).astype(BF16)
                z_ref[r * rc:(r + 1) * rc, :] = z
                n_ref[r * rc:(r + 1) * rc, :] = z
        for e_ref, n_ref in ((eg_ref, ng_ref), (eu_ref, nu_ref)):
            e_ref[0:HALO, :] = e_ref[tm:tm + HALO, :]
            e_ref[HALO:HALO + tm, :] = n_ref[...]

    last = ni - 1
    z_spec = pl.BlockSpec((tm, tc), lambda j, i: (jnp.minimum(i, last), j))
    return pl.pallas_call(
        body,
        grid=(nj, ni + 1),
        in_specs=[pl.BlockSpec((tm, Dm), lambda j, i: (jnp.minimum(i, last), 0)),
                  pl.BlockSpec((tc, Dm), lambda j, i: (j, 0)), pl.BlockSpec((tc, Dm), lambda j, i: (nj + j, 0)),
                  pl.BlockSpec((3, tc), lambda j, i: (0, j)), pl.BlockSpec((3, tc), lambda j, i: (0, nj + j)),
                  pl.BlockSpec((1, tc), lambda j, i: (0, j)), pl.BlockSpec((1, tc), lambda j, i: (0, nj + j))],
        out_specs=[z_spec, z_spec, pl.BlockSpec((tm, tc), lambda j, i: (jnp.maximum(i - 1, 0), j))],
        out_shape=[jax.ShapeDtypeStruct((T, Fh), BF16)] * 3,
        scratch_shapes=[pltpu.VMEM((HALO + tm, tc), BF16), pltpu.VMEM((HALO + tm, tc), BF16),
                        pltpu.VMEM((tm, tc), BF16), pltpu.VMEM((tm, tc), BF16)],
        compiler_params=_cp("arbitrary", "arbitrary"),
        name=name,
    )(hf, w_up_t, w_up_t, cw, cw, cb, cb)


def _ffn_mid_bwd(zg, zu, da, cw, cb, comm=None, name="ffn_mid_bwd", tm=1024, tc=256):
    T, Fh = zg.shape
    tm = min(tm, T)
    nj = Fh // tc
    hb = tm // HALO
    nhb = T // HALO

    nc = comm.n if comm else 0
    ni = T // tm

    def body(*refs):
        zg_ref, zu_ref, zgp_ref, zup_ref, zgn_ref, zun_ref, da_ref, dan_ref, wg_ref, wu_ref, bg_ref, bu_ref = refs[:12]
        cin = refs[12:12 + nc]
        dzg_ref, dzu_ref, dwg_ref, dwu_ref, dbg_ref, dbu_ref = refs[12 + nc:18 + nc]
        cout = refs[18 + nc:18 + 2 * nc]
        csem = refs[18 + 2 * nc:]
        i = pl.program_id(1)
        step = pl.program_id(0) * ni + i
        _host_comm(comm, "early", step, nj * ni, cin, cout, csem)
        first = i == 0
        last = i == ni - 1

        @pl.when(first)
        def _():
            dwg_ref[...] = jnp.zeros_like(dwg_ref)
            dwu_ref[...] = jnp.zeros_like(dwu_ref)
            dbg_ref[...] = jnp.zeros_like(dbg_ref)
            dbu_ref[...] = jnp.zeros_like(dbu_ref)

        def ext_of(p_ref, c_ref, n_ref):
            p = jnp.where(first, 0.0, p_ref[...].astype(F32))
            return jnp.concatenate([p, c_ref[...].astype(F32), n_ref[...].astype(F32)], axis=0)

        zge = ext_of(zgp_ref, zg_ref, zgn_ref)
        zue = ext_of(zup_ref, zu_ref, zun_ref)
        dan = jnp.where(last, 0.0, dan_ref[...].astype(F32))
        dae = jnp.concatenate([jnp.zeros((HALO, tc), F32), da_ref[...].astype(F32), dan], axis=0)
        zg1, zg2 = pltpu.roll(zge, 1, 0), pltpu.roll(zge, 2, 0)
        zu1, zu2 = pltpu.roll(zue, 1, 0), pltpu.roll(zue, 2, 0)
        cg = wg_ref[0:1, :] * zg2 + wg_ref[1:2, :] * zg1 + wg_ref[2:3, :] * zge + bg_ref[...]
        cu = wu_ref[0:1, :] * zu2 + wu_ref[1:2, :] * zu1 + wu_ref[2:3, :] * zue + bu_ref[...]
        gel, dgel = (t.astype(F32) for t in _gelu_and_grad(cg.astype(BF16)))
        dcg = dae * cu * dgel
        dcu = dae * gel
        lo, hi = HALO, HALO + tm

        def back(dc, taps, w_ref, dz_ref, dw_ref, db_ref):
            n = dc.shape[0]
            dz = w_ref[2:3, :] * dc + w_ref[1:2, :] * pltpu.roll(dc, n - 1, 0) + w_ref[0:1, :] * pltpu.roll(dc, n - 2, 0)
            dz_ref[...] = dz[lo:hi].astype(dz_ref.dtype)
            dcc = dc[lo:hi]
            db_ref[...] += jnp.sum(dcc, axis=0, keepdims=True)
            for k, tap in enumerate(taps):
                dw_ref[k:k + 1, :] += jnp.sum(dcc * tap[lo:hi], axis=0, keepdims=True)

        back(dcg, (zg2, zg1, zge), wg_ref, dzg_ref, dwg_ref, dbg_ref)
        back(dcu, (zu2, zu1, zue), wu_ref, dzu_ref, dwu_ref, dbu_ref)
        _host_comm(comm, "late", step, nj * ni, cin, cout, csem)

    cur = pl.BlockSpec((tm, tc), lambda j, i: (i, j))
    prev = pl.BlockSpec((HALO, tc), lambda j, i: (jnp.maximum(i * hb - 1, 0), j))
    nxt = pl.BlockSpec((HALO, tc), lambda j, i: (jnp.minimum((i + 1) * hb, nhb - 1), j))
    wg = pl.BlockSpec((3, tc), lambda j, i: (0, j))
    wu = pl.BlockSpec((3, tc), lambda j, i: (0, j + nj))
    bg = pl.BlockSpec((1, tc), lambda j, i: (0, j))
    bu = pl.BlockSpec((1, tc), lambda j, i: (0, j + nj))
    dw = pl.BlockSpec((3, tc), lambda j, i: (0, j))
    db = pl.BlockSpec((1, tc), lambda j, i: (0, j))
    out = pl.pallas_call(
        body,
        grid=(nj, ni),
        in_specs=[cur, cur, prev, prev, nxt, nxt, cur, nxt, wg, wu, bg, bu] + [ANY] * nc,
        out_specs=[cur, cur, dw, dw, db, db] + [ANY] * nc,
        out_shape=[jax.ShapeDtypeStruct((T, Fh), BF16), jax.ShapeDtypeStruct((T, Fh), BF16),
                   jax.ShapeDtypeStruct((3, Fh), F32), jax.ShapeDtypeStruct((3, Fh), F32),
                   jax.ShapeDtypeStruct((1, Fh), F32), jax.ShapeDtypeStruct((1, Fh), F32)] + (comm.out_shape if comm else []),
        scratch_shapes=comm.scratch if comm else [],
        compiler_params=_cp("arbitrary", "arbitrary"),
        name=name,
    )(zg, zu, zg, zu, zg, zu, da, da, cw, cw, cb, cb, *(comm.arrays if comm else []))
    dzg, dzu, dwg, dwu, dbg, dbu = out[:6]
    return dzg, dzu, jnp.concatenate([dwg, dwu], axis=1), jnp.concatenate([dbg, dbu], axis=1), list(out[6:])


def _sgu_mask():
    r = lax.broadcasted_iota(jnp.int32, (SGU_BLOCK, SGU_BLOCK), 0)
    c = lax.broadcasted_iota(jnp.int32, (SGU_BLOCK, SGU_BLOCK), 1)
    return (c < CHUNK) | (r >= CHUNK)


def _sgu_fwd(zz, ln_g, ln_b, w_s, b_s, name="sgu_fwd", tm=256):
    T = zz.shape[0]
    tm = min(tm, T)
    W = SGU_WIDTH

    def body(zu_ref, zv_ref, g_ref, b_ref, ws_ref, bs_ref, y_ref):
        u = _gelu(zu_ref[...]).astype(F32)
        v = _gelu(zv_ref[...]).astype(F32)
        mu = jnp.mean(v, axis=-1, keepdims=True)
        xc = v - mu
        rstd = lax.rsqrt(jnp.mean(xc * xc, axis=-1, keepdims=True) + EPS)
        vn = (xc * rstd * g_ref[...] + b_ref[...]).astype(BF16)
        mask = _sgu_mask()
        for g in range(SGU_GROUPS):
            wm = jnp.where(mask, ws_ref[g], 0.0).astype(BF16)
            cs = slice(g * SGU_GW, (g + 1) * SGU_GW)
            for blk in range(tm // SGU_BLOCK):
                rs = slice(blk * SGU_BLOCK, (blk + 1) * SGU_BLOCK)
                mixed = _dot(wm, vn[rs, cs], NN) + bs_ref[g]
                y_ref[rs, cs] = (u[rs, cs] * mixed).astype(y_ref.dtype)

    return pl.pallas_call(
        body,
        grid=(T // tm,),
        in_specs=[pl.BlockSpec((tm, W), lambda i: (i, 0)), pl.BlockSpec((tm, W), lambda i: (i, 1)),
                  pl.BlockSpec((1, W), lambda i: (0, 0)), pl.BlockSpec((1, W), lambda i: (0, 0)),
                  pl.BlockSpec((SGU_GROUPS, SGU_BLOCK, SGU_BLOCK), lambda i: (0, 0, 0)),
                  pl.BlockSpec((SGU_GROUPS, SGU_BLOCK, 1), lambda i: (0, 0, 0))],
        out_specs=pl.BlockSpec((tm, W), lambda i: (i, 0)),
        out_shape=jax.ShapeDtypeStruct((T, W), BF16),
        compiler_params=_cp("parallel"),
        name=name,
    )(zz, zz, ln_g.reshape(1, W), ln_b.reshape(1, W), w_s, b_s.reshape(SGU_GROUPS, SGU_BLOCK, 1))


def _sgu_bwd(zz, dy, ln_g, ln_b, w_s, b_s, name="sgu_bwd", tm=256):
    T = zz.shape[0]
    tm = min(tm, T)
    W = SGU_WIDTH

    def body(zu_ref, zv_ref, dy_ref, g_ref, b_ref, ws_ref, bs_ref, dzz_ref, dws_ref, dbs_ref, dg_ref, db_ref, dvn_ref):
        i = pl.program_id(0)

        @pl.when(i == 0)
        def _():
            dws_ref[...] = jnp.zeros_like(dws_ref)
            dbs_ref[...] = jnp.zeros_like(dbs_ref)
            dg_ref[...] = jnp.zeros_like(dg_ref)
            db_ref[...] = jnp.zeros_like(db_ref)

        u, du_dz = (t.astype(F32) for t in _gelu_and_grad(zu_ref[...]))
        v, dv_dz = (t.astype(F32) for t in _gelu_and_grad(zv_ref[...]))
        mu = jnp.mean(v, axis=-1, keepdims=True)
        xc = v - mu
        rstd = lax.rsqrt(jnp.mean(xc * xc, axis=-1, keepdims=True) + EPS)
        xhat = xc * rstd
        gv = g_ref[...]
        vn = (xhat * gv + b_ref[...]).astype(BF16)
        dyv = dy_ref[...].astype(F32)
        mask = _sgu_mask()
        for g in range(SGU_GROUPS):
            wm = jnp.where(mask, ws_ref[g], 0.0).astype(BF16)
            cs = slice(g * SGU_GW, (g + 1) * SGU_GW)
            dw_acc = jnp.zeros((SGU_BLOCK, SGU_BLOCK), F32)
            db_acc = jnp.zeros((SGU_BLOCK, 1), F32)
            for blk in range(tm // SGU_BLOCK):
                rs = slice(blk * SGU_BLOCK, (blk + 1) * SGU_BLOCK)
                vn_bg = vn[rs, cs]
                mixed = _dot(wm, vn_bg, NN) + bs_ref[g]
                dy_bg = dyv[rs, cs]
                dmixed = dy_bg * u[rs, cs]
                dmb = dmixed.astype(BF16)
                dw_acc += _dot(dmb, vn_bg, NT)
                db_acc += jnp.sum(dmixed, axis=1, keepdims=True)
                dvn_ref[rs, cs] = _dot(wm, dmb, TN)
                dzz_ref[rs, cs] = (dy_bg * mixed * du_dz[rs, cs]).astype(dzz_ref.dtype)
            dws_ref[g] += jnp.where(mask, dw_acc, 0.0)
            dbs_ref[g] += db_acc
        dvn = dvn_ref[...]
        dg_ref[...] += jnp.sum(dvn * xhat, axis=0, keepdims=True)
        db_ref[...] += jnp.sum(dvn, axis=0, keepdims=True)
        dxh = dvn * gv
        m1 = jnp.mean(dxh, axis=-1, keepdims=True)
        m2 = jnp.mean(dxh * xhat, axis=-1, keepdims=True)
        dv = rstd * (dxh - m1 - xhat * m2)
        dzz_ref[:, W:] = (dv * dv_dz).astype(dzz_ref.dtype)

    vec = pl.BlockSpec((1, W), lambda i: (0, 0))
    ws_spec = pl.BlockSpec((SGU_GROUPS, SGU_BLOCK, SGU_BLOCK), lambda i: (0, 0, 0))
    bs_spec = pl.BlockSpec((SGU_GROUPS, SGU_BLOCK, 1), lambda i: (0, 0, 0))
    return pl.pallas_call(
        body,
        grid=(T // tm,),
        in_specs=[pl.BlockSpec((tm, W), lambda i: (i, 0)), pl.BlockSpec((tm, W), lambda i: (i, 1)),
                  pl.BlockSpec((tm, W), lambda i: (i, 0)), vec, vec, ws_spec, bs_spec],
        out_specs=[pl.BlockSpec((tm, 2 * W), lambda i: (i, 0)), ws_spec, bs_spec, vec, vec],
        out_shape=[jax.ShapeDtypeStruct((T, 2 * W), BF16),
                   jax.ShapeDtypeStruct((SGU_GROUPS, SGU_BLOCK, SGU_BLOCK), F32),
                   jax.ShapeDtypeStruct((SGU_GROUPS, SGU_BLOCK, 1), F32),
                   jax.ShapeDtypeStruct((1, W), F32), jax.ShapeDtypeStruct((1, W), F32)],
        scratch_shapes=[pltpu.VMEM((tm, W), F32)],
        compiler_params=_cp("arbitrary"),
        name=name,
    )(zz, zz, dy, ln_g.reshape(1, W), ln_b.reshape(1, W), w_s, b_s.reshape(SGU_GROUPS, SGU_BLOCK, 1))


RET_TR = 256
RET_BLK = 256
QK_SCALE = RET_QK_DIM ** -0.5


def _ret_tables(T):
    half = RET_QK_DIM // 2
    inv = 1.0 / (10000.0 ** jnp.linspace(0.0, 1.0, half, dtype=F32))
    inv2 = jnp.concatenate([inv, inv])[None, :]
    sgn = jnp.concatenate([-jnp.ones((half,), F32), jnp.ones((half,), F32)])[None, :]
    tr = min(RET_TR, T)

    def trig(pos):
        ang = pos.astype(F32)[:, None] * inv2
        return jnp.stack([jnp.cos(ang), jnp.sin(ang), sgn * jnp.sin(ang)])

    tile_tab = jnp.pad(trig(jnp.arange(T // tr) * tr).transpose(1, 0, 2), ((0, 0), (0, 5), (0, 0)))
    row_tab = trig(jnp.arange(tr))
    log_g = jnp.log1p(-jnp.exp2(-5.0 - jnp.arange(RET_HEADS, dtype=F32)))
    idx = jnp.arange(RET_BLK, dtype=F32)
    dist = idx[:, None] - idx[None, :]
    cq, ck = jnp.arange(RET_BLK)[:, None] // CHUNK, jnp.arange(RET_BLK)[None, :] // CHUNK
    expo = jnp.where(ck == cq, jnp.abs(dist), dist)
    d_blk = jnp.where((ck <= cq)[None], jnp.exp(log_g[:, None, None] * expo[None]), 0.0)
    k_dec = jnp.exp(log_g[:, None] * (RET_BLK - 1 - idx)[None, :])[:, :, None]
    q_dec = jnp.exp(log_g[:, None] * (idx + 1.0)[None, :])[:, :, None]
    c_dec = jnp.exp(log_g * RET_BLK)[:, None, None]
    return tile_tab, row_tab, d_blk, q_dec, k_dec, c_dec


def _rot(x, c, s):
    return x * c + pltpu.roll(x, RET_QK_DIM // 2, 1) * s


def _rot_tables(tt_ref, rt_ref):
    ca, sa, ga = tt_ref[0:1, :], tt_ref[1:2, :], tt_ref[2:3, :]
    cb, sb, gb = rt_ref[0], rt_ref[1], rt_ref[2]
    return ca * cb - sa * sb, ga * cb + ca * gb


def _ret_specs(tr, rev, nb):
    ix = (lambda n: nb - 1 - n) if rev else (lambda n: n)
    tt = pl.BlockSpec((None, 8, RET_QK_DIM), lambda n: (ix(n), 0, 0))
    rt = pl.BlockSpec((3, tr, RET_QK_DIM), lambda n: (0, 0, 0))
    dm = pl.BlockSpec((RET_HEADS, RET_BLK, RET_BLK), lambda n: (0, 0, 0))
    dv = pl.BlockSpec((RET_HEADS, RET_BLK, 1), lambda n: (0, 0, 0))
    dc = pl.BlockSpec((RET_HEADS, 1, 1), lambda n: (0, 0, 0))
    return ix, [tt, rt, dm, dv, dv, dc]


def _ret_fwd(z_a, tables, name="ret_fwd"):
    T = z_a.shape[0]
    tr = min(RET_TR, T)
    cpb = tr // RET_BLK
    nb = T // tr
    QW, VW = RET_HEADS * RET_QK_DIM, RET_HEADS * RET_V_DIM
    ix, tab_specs = _ret_specs(tr, False, nb)

    def body(z_ref, tt_ref, rt_ref, dm_ref, qd_ref, kd_ref, cd_ref, y_ref, st_ref, state):
        @pl.when(pl.program_id(0) == 0)
        def _():
            state[...] = jnp.zeros_like(state)

        rot_c, rot_s = _rot_tables(tt_ref, rt_ref)
        for c in range(cpb):
            for h in range(RET_HEADS):
                rs = slice(c * RET_BLK, (c + 1) * RET_BLK)
                cc, ss = rot_c[rs, :], rot_s[rs, :]
                q = z_ref[rs, h * RET_QK_DIM:(h + 1) * RET_QK_DIM].astype(F32)
                k = z_ref[rs, QW + h * RET_QK_DIM:QW + (h + 1) * RET_QK_DIM].astype(F32)
                v = z_ref[rs, 2 * QW + h * RET_V_DIM:2 * QW + (h + 1) * RET_V_DIM]
                gt = z_ref[rs, 2 * QW + VW + h * RET_V_DIM:2 * QW + VW + (h + 1) * RET_V_DIM].astype(F32)
                qr = _rot(q, cc, ss)
                kr = _rot(k, cc, ss) * QK_SCALE
                s_old = state[h]
                sb = s_old.astype(BF16)
                st_ref[c, h] = sb
                s = _dot(qr.astype(BF16), kr.astype(BF16), NT) * dm_ref[h]
                o = _dot(s.astype(BF16), v, NN) + _dot((qr * qd_ref[h]).astype(BF16), sb, NN)
                state[h] = s_old * cd_ref[h] + _dot((kr * kd_ref[h]).astype(BF16), v, TN)
                mu = jnp.mean(o, axis=-1, keepdims=True)
                oc = o - mu
                rn = oc * lax.rsqrt(jnp.mean(oc * oc, axis=-1, keepdims=True) + EPS)
                silu = gt / (1.0 + jnp.exp(-gt))
                y_ref[rs, h * RET_V_DIM:(h + 1) * RET_V_DIM] = (silu * rn).astype(y_ref.dtype)

    return pl.pallas_call(
        body,
        grid=(nb,),
        in_specs=[pl.BlockSpec((tr, RET_W), lambda n: (n, 0))] + tab_specs,
        out_specs=[pl.BlockSpec((tr, VW), lambda n: (n, 0)),
                   pl.BlockSpec((cpb, RET_HEADS, RET_QK_DIM, RET_V_DIM), lambda n: (n, 0, 0, 0))],
        out_shape=[jax.ShapeDtypeStruct((T, Y_COLS), BF16),
                   jax.ShapeDtypeStruct((T // RET_BLK, RET_HEADS, RET_QK_DIM, RET_V_DIM), BF16)],
        scratch_shapes=[pltpu.VMEM((RET_HEADS, RET_QK_DIM, RET_V_DIM), F32)],
        compiler_params=_cp("arbitrary"),
        name=name,
    )(z_a, *tables)


def _ret_bwd(z_a, dy, states, tables, name="ret_bwd"):
    T = z_a.shape[0]
    tr = min(RET_TR, T)
    cpb = tr // RET_BLK
    nb = T // tr
    QW, VW = RET_HEADS * RET_QK_DIM, RET_HEADS * RET_V_DIM
    ix, tab_specs = _ret_specs(tr, True, nb)

    def body(z_ref, dy_ref, st_ref, tt_ref, rt_ref, dm_ref, qd_ref, kd_ref, cd_ref, dz_ref, dstate):
        @pl.when(pl.program_id(0) == 0)
        def _():
            dstate[...] = jnp.zeros_like(dstate)

        rot_c, rot_s = _rot_tables(tt_ref, rt_ref)
        for c in reversed(range(cpb)):
            for h in range(RET_HEADS):
                rs = slice(c * RET_BLK, (c + 1) * RET_BLK)
                cc, ss = rot_c[rs, :], rot_s[rs, :]
                q = z_ref[rs, h * RET_QK_DIM:(h + 1) * RET_QK_DIM].astype(F32)
                k = z_ref[rs, QW + h * RET_QK_DIM:QW + (h + 1) * RET_QK_DIM].astype(F32)
                v = z_ref[rs, 2 * QW + h * RET_V_DIM:2 * QW + (h + 1) * RET_V_DIM]
                gt = z_ref[rs, 2 * QW + VW + h * RET_V_DIM:2 * QW + VW + (h + 1) * RET_V_DIM].astype(F32)
                dyv = dy_ref[rs, h * RET_V_DIM:(h + 1) * RET_V_DIM].astype(F32)
                dmat, qd, kd = dm_ref[h], qd_ref[h], kd_ref[h]
                qr = _rot(q, cc, ss)
                kr = _rot(k, cc, ss) * QK_SCALE
                qrb, krb = qr.astype(BF16), kr.astype(BF16)
                sb = st_ref[c, h]
                sd = (_dot(qrb, krb, NT) * dmat).astype(BF16)
                qdb = (qr * qd).astype(BF16)
                kdb = (kr * kd).astype(BF16)
                o = _dot(sd, v, NN) + _dot(qdb, sb, NN)
                mu = jnp.mean(o, axis=-1, keepdims=True)
                oc = o - mu
                rstd = lax.rsqrt(jnp.mean(oc * oc, axis=-1, keepdims=True) + EPS)
                rn = oc * rstd
                sg = 1.0 / (1.0 + jnp.exp(-gt))
                dgt = dyv * rn * (sg * (1.0 + gt * (1.0 - sg)))
                drn = dyv * (gt * sg)
                do = rstd * (drn - jnp.mean(drn, axis=-1, keepdims=True) - rn * jnp.mean(drn * rn, axis=-1, keepdims=True))
                dob = do.astype(BF16)
                dsn = dstate[h]
                dsnb = dsn.astype(BF16)
                ds_raw = (_dot(dob, v, NT) * dmat).astype(BF16)
                dv = _dot(sd, dob, TN) + _dot(kdb, dsnb, NN)
                dqr = _dot(ds_raw, krb, NN) + qd * _dot(dob, sb, NT)
                dkr = (_dot(ds_raw, qrb, TN) + kd * _dot(v, dsnb, NT)) * QK_SCALE
                dstate[h] = dsn * cd_ref[h] + _dot(qdb, dob, TN)
                dq = dqr * cc + pltpu.roll(dqr * ss, RET_QK_DIM // 2, 1)
                dk = dkr * cc + pltpu.roll(dkr * ss, RET_QK_DIM // 2, 1)
                dz_ref[rs, h * RET_QK_DIM:(h + 1) * RET_QK_DIM] = dq.astype(dz_ref.dtype)
                dz_ref[rs, QW + h * RET_QK_DIM:QW + (h + 1) * RET_QK_DIM] = dk.astype(dz_ref.dtype)
                dz_ref[rs, 2 * QW + h * RET_V_DIM:2 * QW + (h + 1) * RET_V_DIM] = dv.astype(dz_ref.dtype)
                dz_ref[rs, 2 * QW + VW + h * RET_V_DIM:2 * QW + VW + (h + 1) * RET_V_DIM] = dgt.astype(dz_ref.dtype)

    return pl.pallas_call(
        body,
        grid=(nb,),
        in_specs=[pl.BlockSpec((tr, RET_W), lambda n: (ix(n), 0)),
                  pl.BlockSpec((tr, VW), lambda n: (ix(n), 0)),
                  pl.BlockSpec((cpb, RET_HEADS, RET_QK_DIM, RET_V_DIM), lambda n: (ix(n), 0, 0, 0))] + tab_specs,
        out_specs=pl.BlockSpec((tr, RET_W), lambda n: (ix(n), 0)),
        out_shape=jax.ShapeDtypeStruct((T, RET_W), BF16),
        scratch_shapes=[pltpu.VMEM((RET_HEADS, RET_QK_DIM, RET_V_DIM), F32)],
        compiler_params=_cp("arbitrary"),
        name=name,
    )(z_a, dy, states, *tables)


ATT_TQ = 256
ATT_CPB = ATT_TQ // CHUNK
ATT_SCALE = ATT_HEAD_DIM ** -0.5


ATT_WIN = 3 * ATT_TQ
ATT_NB = CHUNK * ATT_BAND


def _rel_index():
    i = np.arange(CHUNK)[:, None]
    j = np.arange(ATT_BAND)[None, :]
    rel = np.clip(i + ATT_PAST * CHUNK - j, -MAX_REL, MAX_REL) + MAX_REL
    return jnp.asarray(rel.reshape(1, ATT_NB).astype(np.int32))


def _split3(x):
    hi = x.astype(BF16)
    r1 = x - hi.astype(F32)
    mid = r1.astype(BF16)
    lo = (r1 - mid.astype(F32)).astype(BF16)
    return hi, mid, lo


REL_TILE = 4608


def _bias_expand(rel_bias, name="bias_expand"):
    H = rel_bias.shape[0]
    n = ATT_NB
    padded = jnp.pad(rel_bias, ((0, 0), (0, N_REL_PAD - N_REL)))

    def body(rb_ref, idx_ref, o_ref):
        onehot = (lax.broadcasted_iota(jnp.int32, (N_REL_PAD, REL_TILE), 0) == idx_ref[...]).astype(BF16)
        hi, mid, lo = _split3(rb_ref[...])
        o_ref[...] = _dot(hi, onehot, NN) + _dot(mid, onehot, NN) + _dot(lo, onehot, NN)

    out = pl.pallas_call(
        body,
        grid=(n // REL_TILE,),
        in_specs=[pl.BlockSpec((H, N_REL_PAD), lambda t: (0, 0)), pl.BlockSpec((1, REL_TILE), lambda t: (0, t))],
        out_specs=pl.BlockSpec((H, REL_TILE), lambda t: (0, t)),
        out_shape=jax.ShapeDtypeStruct((H, n), F32),
        compiler_params=_cp("parallel"),
        name=name,
    )(padded, _rel_index())
    return out.reshape(H, CHUNK, ATT_BAND)


def _bias_tile(band, name="bias_tile"):
    H = band.shape[0]
    padded = jnp.pad(band, ((0, 0), (0, 0), (0, ATT_WIN - ATT_BAND)), constant_values=NEG_INF)

    def body(b_ref, o_ref):
        b = b_ref[...]
        col = lax.broadcasted_iota(jnp.int32, (CHUNK, ATT_WIN), 1)
        keep = col >= (2 - pl.program_id(0)) * ATT_TQ
        for a in range(ATT_CPB):
            o_ref[a * CHUNK:(a + 1) * CHUNK, :] = jnp.where(keep, pltpu.roll(b, a * CHUNK, 1) if a else b, NEG_INF)

    return pl.pallas_call(
        body,
        grid=(3, H),
        in_specs=[pl.BlockSpec((None, CHUNK, ATT_WIN), lambda v, h: (h, 0, 0))],
        out_specs=pl.BlockSpec((None, None, ATT_TQ, ATT_WIN), lambda v, h: (v, h, 0, 0)),
        out_shape=jax.ShapeDtypeStruct((3, H, ATT_TQ, ATT_WIN), F32),
        compiler_params=_cp("parallel", "parallel"),
        name=name,
    )(padded)


def _bias_untile(dtile, name="bias_untile"):
    H = dtile.shape[0]

    def body(d_ref, o_ref):
        acc = d_ref[0:CHUNK, :]
        for a in range(1, ATT_CPB):
            acc = acc + pltpu.roll(d_ref[a * CHUNK:(a + 1) * CHUNK, :], ATT_WIN - a * CHUNK, 1)
        o_ref[...] = acc

    out = pl.pallas_call(
        body,
        grid=(H,),
        in_specs=[pl.BlockSpec((None, ATT_TQ, ATT_WIN), lambda h: (h, 0, 0))],
        out_specs=pl.BlockSpec((None, CHUNK, ATT_WIN), lambda h: (h, 0, 0)),
        out_shape=jax.ShapeDtypeStruct((H, CHUNK, ATT_WIN), F32),
        compiler_params=_cp("parallel"),
        name=name,
    )(dtile)
    return out[:, :, :ATT_BAND]


def _bias_reduce(dbias, name="bias_reduce"):
    H = dbias.shape[0]
    n = ATT_NB

    def body(db_ref, idx_ref, o_ref):
        @pl.when(pl.program_id(0) == 0)
        def _():
            o_ref[...] = jnp.zeros_like(o_ref)

        onehot = (lax.broadcasted_iota(jnp.int32, (N_REL_PAD, REL_TILE), 0) == idx_ref[...]).astype(BF16)
        hi, mid, lo = _split3(db_ref[...])
        o_ref[...] += _dot(hi, onehot, NT) + _dot(mid, onehot, NT) + _dot(lo, onehot, NT)

    out = pl.pallas_call(
        body,
        grid=(n // REL_TILE,),
        in_specs=[pl.BlockSpec((H, REL_TILE), lambda t: (0, t)), pl.BlockSpec((1, REL_TILE), lambda t: (0, t))],
        out_specs=pl.BlockSpec((H, N_REL_PAD), lambda t: (0, 0)),
        out_shape=jax.ShapeDtypeStruct((H, N_REL_PAD), F32),
        compiler_params=_cp("arbitrary"),
        name=name,
    )(dbias.reshape(H, n), _rel_index())
    return out[:, :N_REL]


def _att_probs(q, kwin, bias):
    s = _dot(q, kwin, NT) + bias
    e = jnp.exp(s - jnp.max(s, axis=-1, keepdims=True))
    return e * (1.0 / jnp.sum(e, axis=-1, keepdims=True))


ATT_PAIR = 2 * ATT_HEAD_DIM
ATT_NP = ATT_HEADS // 2
ATT_QW = ATT_HEADS * ATT_HEAD_DIM
Y_COLS = RET_HEADS * RET_V_DIM + ATT_QW


def _att_specs(tq, nq, clip_q, q_col0):
    cb = ATT_QW // ATT_PAIR
    qi = (lambda p, m: (jnp.minimum(m, nq - 1), q_col0 + p)) if clip_q else (lambda p, m: (m, q_col0 + p))
    q = pl.BlockSpec((tq, ATT_PAIR), qi)

    def win(col0):
        return [pl.BlockSpec((tq, ATT_PAIR), functools.partial(lambda p, m, back: (jnp.clip(m - back, 0, nq - 1), col0 + p), back=b))
                for b in (2, 1, 0)]

    bias = pl.BlockSpec((None, 2, ATT_TQ, ATT_WIN), lambda p, m: (jnp.minimum(m, 2), p, 0, 0))
    return q, win(cb), win(2 * cb), bias


def _head_masks(rows):
    lane = lax.broadcasted_iota(jnp.int32, (rows, ATT_PAIR), 1)
    return lane < ATT_HEAD_DIM


def _att_fwd(z_b, bias, y, comm=None, name="att_fwd"):
    T = z_b.shape[0]
    tq = ATT_TQ
    nq = T // tq
    qs, kwin, vwin, bs = _att_specs(tq, nq, False, 0)
    nc = comm.n if comm else 0
    total = ATT_NP * nq

    def body(*refs):
        q_ref, k0, k1, k2, v0, v1, v2, b_ref = refs[:8]
        cin = refs[9:9 + nc]
        o_ref = refs[9 + nc]
        cout = refs[10 + nc:10 + 2 * nc]
        csem = refs[10 + 2 * nc:]
        m = pl.program_id(1)
        step = pl.program_id(0) * nq + m
        _host_comm(comm, "early", step, total, cin, cout, csem)
        kw = jnp.concatenate([k0[...], k1[...], k2[...]], axis=0)
        vw = jnp.concatenate([v0[...], v1[...], v2[...]], axis=0)
        q2 = q_ref[...] * ATT_SCALE
        even = _head_masks(tq)
        outs = []
        for hh in range(2):
            qm = jnp.where(even if hh == 0 else ~even, q2, jnp.zeros_like(q2))
            p = _att_probs(qm, kw, b_ref[hh])
            outs.append(_dot(p.astype(BF16), vw, NN))
        o_ref[...] = jnp.where(even, outs[0], outs[1]).astype(o_ref.dtype)
        _host_comm(comm, "late", step, total, cin, cout, csem)

    y_cb = (Y_COLS - ATT_QW) // ATT_PAIR
    out = pl.pallas_call(
        body,
        grid=(ATT_NP, nq),
        in_specs=[qs] + kwin + vwin + [bs, ANY] + [ANY] * nc,
        out_specs=[pl.BlockSpec((tq, ATT_PAIR), lambda p, m: (m, y_cb + p))] + [ANY] * nc,
        out_shape=[jax.ShapeDtypeStruct((T, Y_COLS), BF16)] + (comm.out_shape if comm else []),
        scratch_shapes=comm.scratch if comm else [],
        input_output_aliases={8: 0},
        compiler_params=_cp("arbitrary", "arbitrary"),
        name=name,
    )(z_b, z_b, z_b, z_b, z_b, z_b, z_b, bias, y, *(comm.arrays if comm else []))
    return out[0], list(out[1:])


def _att_bwd(z_b, bias, dy, comm=None, name="att_bwd"):
    T = z_b.shape[0]
    tq = ATT_TQ
    nq = T // tq
    y_cb = (Y_COLS - ATT_QW) // ATT_PAIR
    qs, kwin, vwin, bs = _att_specs(tq, nq, True, 0)
    dos = _att_specs(tq, nq, True, y_cb)[0]
    kv_out = pl.BlockSpec((tq, ATT_PAIR), lambda p, m: (jnp.maximum(m - 2, 0), p))
    W3 = 3 * tq
    nc = comm.n if comm else 0
    total = ATT_NP * (nq + 2)

    def body(*refs):
        q_ref, k0, k1, k2, v0, v1, v2, b_ref, do_ref = refs[:9]
        cin = refs[9:9 + nc]
        dq_ref, dk_ref, dv_ref, db_ref = refs[9 + nc:13 + nc]
        cout = refs[13 + nc:13 + 2 * nc]
        dkc, dvc, dkw, dvw = refs[13 + 2 * nc:17 + 2 * nc]
        csem = refs[17 + 2 * nc:]
        m = pl.program_id(1)
        step = pl.program_id(0) * (nq + 2) + m
        _host_comm(comm, "early", step, total, cin, cout, csem)

        @pl.when(m == 0)
        def _():
            dkc[...] = jnp.zeros_like(dkc)
            dvc[...] = jnp.zeros_like(dvc)
            db_ref[...] = jnp.zeros_like(db_ref)

        @pl.when(m >= nq)
        def _():
            dkw[...] = jnp.zeros_like(dkw)
            dvw[...] = jnp.zeros_like(dvw)

        @pl.when(m < nq)
        def _():
            kw = jnp.concatenate([k0[...], k1[...], k2[...]], axis=0)
            vw = jnp.concatenate([v0[...], v1[...], v2[...]], axis=0)
            q2, do2 = q_ref[...] * ATT_SCALE, do_ref[...]
            even = _head_masks(tq)
            dqs, dks, dvs = [], [], []
            for hh in range(2):
                mine = even if hh == 0 else ~even
                p = _att_probs(jnp.where(mine, q2, jnp.zeros_like(q2)), kw, b_ref[hh])
                dp = _dot(jnp.where(mine, do2, jnp.zeros_like(do2)), vw, NT)
                ds = p * (dp - jnp.sum(dp * p, axis=-1, keepdims=True))
                db_ref[hh] += ds
                dsb = ds.astype(BF16)
                dqs.append(_dot(dsb, kw, NN))
                dks.append(_dot(dsb, q2, TN))
                dvs.append(_dot(p.astype(BF16), do2, TN))
            even_w = _head_masks(W3)
            dq_ref[...] = (jnp.where(even, dqs[0], dqs[1]) * ATT_SCALE).astype(dq_ref.dtype)
            dkw[...] = jnp.where(even_w, dks[0], dks[1])
            dvw[...] = jnp.where(even_w, dvs[0], dvs[1])

        dk_ref[...] = (dkc[0:tq, :] + dkw[0:tq, :]).astype(dk_ref.dtype)
        dv_ref[...] = (dvc[0:tq, :] + dvw[0:tq, :]).astype(dv_ref.dtype)
        dkc[0:tq, :] = dkc[tq:2 * tq, :] + dkw[tq:2 * tq, :]
        dvc[0:tq, :] = dvc[tq:2 * tq, :] + dvw[tq:2 * tq, :]
        dkc[tq:2 * tq, :] = dkw[2 * tq:W3, :]
        dvc[tq:2 * tq, :] = dvw[2 * tq:W3, :]
        _host_comm(comm, "late", step, total, cin, cout, csem)

    qo = pl.BlockSpec((tq, ATT_PAIR), lambda p, m: (jnp.minimum(m, nq - 1), p))
    dbs = pl.BlockSpec((2, ATT_TQ, ATT_WIN), lambda p, m: (p, 0, 0))
    hd = jax.ShapeDtypeStruct((T, ATT_QW), BF16)
    out = pl.pallas_call(
        body,
        grid=(ATT_NP, nq + 2),
        in_specs=[qs] + kwin + vwin + [bs, dos] + [ANY] * nc,
        out_specs=[qo, kv_out, kv_out, dbs] + [ANY] * nc,
        out_shape=[hd, hd, hd, jax.ShapeDtypeStruct((ATT_HEADS, ATT_TQ, ATT_WIN), F32)] + (comm.out_shape if comm else []),
        scratch_shapes=[pltpu.VMEM((2 * tq, ATT_PAIR), F32), pltpu.VMEM((2 * tq, ATT_PAIR), F32),
                        pltpu.VMEM((W3, ATT_PAIR), F32), pltpu.VMEM((W3, ATT_PAIR), F32)] + (comm.scratch if comm else []),
        compiler_params=_cp("arbitrary", "arbitrary"),
        name=name,
    )(z_b, z_b, z_b, z_b, z_b, z_b, z_b, bias, dy, *(comm.arrays if comm else []))
    return out[0], out[1], out[2], out[3], list(out[4:])


REST = ["ab_out", "c_in_t", "c_out", "up_t0", "up_t1", "down0", "down1"]


def _local_step(x, target, wt, small, rest_shards=None, overlap=False, hn0=None):
    T = x.shape[0]
    Fh = FFN_HIDDEN
    tables = _ret_tables(T)
    gw, gs, recv = {}, {}, {}
    wt = dict(wt)

    if hn0 is None:
        hn0 = _rms_fwd(x, small["attn_norm_g"][0], name="rms_fwd")
    z_a = _mm(hn0, wt["ab_in_t"][:RET_W], "nt", BF16, name="mm_ab_in_a")
    z_b = _mm(hn0, wt["ab_in_t"][RET_W:], "nt", BF16, name="mm_ab_in_b")
    y, states = _ret_fwd(z_a, tables)
    bias = _bias_tile(_bias_expand(small["rel_bias"]))
    y, rest = _att_fwd(z_b, bias, y, comm=_Comm("gather", rest_shards) if rest_shards is not None else None)
    if rest_shards is not None:
        full = dict(zip(REST, rest))
        wt.update(ab_out=full["ab_out"], c_in_t=full["c_in_t"], c_out=full["c_out"],
                  up_t=[full["up_t0"], full["up_t1"]], down=[full["down0"], full["down1"]])
    h1, hf0 = _mm_rows(y, wt["ab_out"], x, "mm_ab_out", norm_g=small["ffn_norm_g"][0])

    def ffn_fwd(h, hf, layer, next_g):
        zg, zu, a = _ffn_up_mid(hf, wt["up_t"][layer], small["conv_w"][layer], small["conv_b"][layer][None, :])
        if next_g is None:
            return _mm_rows(a, wt["down"][layer], h, "mm_down_loss", loss=(target, small["final_g"])), (hf, zg, zu, a)
        h_out, hn_next = _mm_rows(a, wt["down"][layer], h, "mm_down", norm_g=next_g)
        return h_out, hn_next, (hf, zg, zu, a)

    def ffn_bwd(dh_out, h, layer, saved, exchange=None):
        hf, zg, zu, a = saved
        da = _mm(dh_out, wt["down"][layer], "nt", BF16, name="mm_d_a")
        d_down = _mm(a, dh_out, "tn", BF16, name="mm_dw_down")
        comm = _Comm("exchange", [gw[k] for k in exchange]) if exchange else None
        dzg, dzu, dcw, dcb, got = _ffn_mid_bwd(zg, zu, da, small["conv_w"][layer], small["conv_b"][layer][None, :], comm=comm)
        recv.update(zip(exchange or [], got))
        dh, dg = _mm_rows(dzg, wt["up_t"][layer][:Fh], None, "mm_d_hf_norm", second=(dzu, wt["up_t"][layer][Fh:]),
                          bwd=(h, small["ffn_norm_g"][layer], dh_out))
        d_up = _mm(dzg, hf, "tn", BF16, name="mm_dw_up", rows=(2 * Fh, 0, None))
        d_up = _mm(dzu, hf, "tn", BF16, name="mm_dw_up_2", rows=(2 * Fh, Fh, d_up))
        return dh, dg, d_up, d_down, dcw, dcb

    h2, hn1, ffn0 = ffn_fwd(h1, hf0, 0, small["attn_norm_g"][1])

    zz = _mm(hn1, wt["c_in_t"], "nt", BF16, name="mm_c_in")
    ys = _sgu_fwd(zz, small["ln_g"], small["ln_b"], small["w_s"], small["b_s"])
    h3, hf1 = _mm_rows(ys, wt["c_out"], h2, "mm_c_out", norm_g=small["ffn_norm_g"][1])
    (loss_vec, dh4, gs["final_g"]), ffn1 = ffn_fwd(h3, hf1, 1, None)

    dh3, dgf1, gw["up_t1"], gw["down1"], dcw1, dcb1 = ffn_bwd(dh4, h3, 1, ffn1)
    dys = _mm(dh3, wt["c_out"], "nt", BF16, name="mm_d_ys")
    gw["c_out"] = _mm(ys, dh3, "tn", BF16, name="mm_dw_c_out")
    dzz, gs["w_s"], dbs, dlg, dlb = _sgu_bwd(zz, dys, small["ln_g"], small["ln_b"], small["w_s"], small["b_s"])
    gs["b_s"], gs["ln_g"], gs["ln_b"] = dbs[:, :, 0], dlg[0], dlb[0]
    dh2, dga1 = _mm_rows(dzz, wt["c_in_t"], None, "mm_d_hn1_norm", bwd=(h2, small["attn_norm_g"][1], dh3))
    gw["c_in_t"] = _mm(dzz, hn1, "tn", BF16, name="mm_dw_c_in")

    dh1, dgf0, gw["up_t0"], gw["down0"], dcw0, dcb0 = ffn_bwd(
        dh2, h1, 0, ffn0, exchange=["c_in_t", "c_out", "up_t1", "down1"] if overlap else None)

    dy = _mm(dh1, wt["ab_out"], "nt", BF16, name="mm_d_y")
    gw["ab_out"] = _mm(y, dh1, "tn", BF16, name="mm_dw_ab_out")
    dz_a = _ret_bwd(z_a, dy, states, tables)
    late = ["ab_out", "up_t0", "down0"] if overlap else []
    dq, dk, dv, dbias, got = _att_bwd(z_b, bias, dy, comm=_Comm("exchange", [gw[k] for k in late]) if late else None)
    recv.update(zip(late, got))
    dz_b = jnp.concatenate([dq, dk, dv], axis=1)
    gs["rel_bias"] = _bias_reduce(_bias_untile(dbias))
    gw["ab_in_t"] = _mm(dz_a, hn0, "tn", BF16, name="mm_dw_ab_in_a", rows=(RET_W + ATT_W, 0, None))
    gw["ab_in_t"] = _mm(dz_b, hn0, "tn", BF16, name="mm_dw_ab_in_b", rows=(RET_W + ATT_W, RET_W, gw["ab_in_t"]))
    if overlap:
        dhn0, got = _mm(dz_a, wt["ab_in_t"][:RET_W], "nn", F32, name="mm_d_hn0", comm=_Comm("exchange", [gw["ab_in_t"]]))
        recv["ab_in_t"] = got[0]
    else:
        dhn0 = _mm(dz_a, wt["ab_in_t"][:RET_W], "nn", F32, name="mm_d_hn0")
    grad_x, dga0 = _mm_rows(dz_b, wt["ab_in_t"][RET_W:], dhn0, "mm_d_hn0_norm", bwd=(x, small["attn_norm_g"][0], dh1))

    gs["attn_norm_g"] = jnp.concatenate([dga0, dga1], axis=0)
    gs["ffn_norm_g"] = jnp.concatenate([dgf0, dgf1], axis=0)
    gs["conv_w"] = jnp.stack([dcw0, dcw1])
    gs["conv_b"] = jnp.concatenate([dcb0, dcb1], axis=0)
    gs["final_g"] = gs["final_g"][0]
    return loss_vec[0, 0], grad_x, gw, gs, recv


MESH_ID = pl.DeviceIdType.MESH
ANY = pl.BlockSpec(memory_space=pl.ANY)


def _my_place():
    return lax.axis_index("x"), lax.axis_index("y"), lax.axis_index("c")


class _Comm:
    def __init__(self, kind, arrays):
        self.kind, self.arrays, self.n = kind, list(arrays), len(arrays)
        if kind == "gather":
            self.out_shape = [jax.ShapeDtypeStruct((N_DEV * s.shape[0], s.shape[1]), s.dtype) for s in arrays]
        else:
            self.out_shape = [jax.ShapeDtypeStruct((N_DEV, g.shape[0] // N_DEV, g.shape[1]), g.dtype) for g in arrays]
        n = self.n
        self.scratch = [pltpu.SemaphoreType.DMA((n, 7)), pltpu.SemaphoreType.DMA((n, 7)), pltpu.SemaphoreType.DMA((n,))]

    def phase(self, ph, in_refs, out_refs, sems):
        (self._gather if self.kind == "gather" else self._exchange)(ph, in_refs, out_refs, sems)

    def _gather(self, ph, x_refs, o_refs, sems):
        n = self.n
        send_sems, recv_sems, local_sems = sems
        x, y, c = _my_place()
        me, sibling = (x, y, c), (x, y, 1 - c)
        chips = [(1 - x, y), (x, 1 - y), (1 - x, 1 - y)]

        def rows(a, place):
            m = x_refs[a].shape[0]
            px, py, pc = place
            return o_refs[a].at[pl.ds((4 * px + 2 * py + pc) * m, m), :]

        def copy(a, k, block, to, own=False):
            return pltpu.make_async_remote_copy(
                src_ref=x_refs[a] if own else rows(a, block), dst_ref=rows(a, block),
                send_sem=send_sems.at[a, k], recv_sem=recv_sems.at[a, k], device_id=to, device_id_type=MESH_ID)

        def mine():
            return [pltpu.make_async_copy(x_refs[a], rows(a, me), local_sems.at[a]) for a in range(n)]

        def first():
            out = []
            for a in range(n):
                out.append(copy(a, 0, me, sibling, own=True))
                out += [copy(a, 1 + j, me, (*chip, c), own=True) for j, chip in enumerate(chips)]
            return out

        def passed():
            return [copy(a, 4 + j, (*chip, c), sibling) for j, chip in enumerate(chips) for a in range(n)]

        if ph == 0:
            for cp in mine() + first():
                cp.start()
        elif ph == 1:
            fw = passed()
            for j, chip in enumerate(chips):
                for a in range(n):
                    copy(a, 1 + j, (*chip, c), me).wait_recv()
                    fw[j * n + a].start()
        else:
            for a in range(n):
                copy(a, 0, sibling, me).wait_recv()
                for j, chip in enumerate(chips):
                    copy(a, 4 + j, (*chip, 1 - c), me).wait_recv()
            for cp in first() + passed():
                cp.wait_send()
            for cp in mine():
                cp.wait()

    def _exchange(self, ph, g_refs, o_refs, sems):
        n = self.n
        send_sems, recv_sems, local_sems = sems
        x, y, c = _my_place()
        me = 4 * x + 2 * y + c
        peers = [(x ^ ((k >> 2) & 1), y ^ ((k >> 1) & 1), c ^ (k & 1)) for k in range(1, N_DEV)]

        def block(a, idx):
            m = g_refs[a].shape[0] // N_DEV
            return g_refs[a].at[pl.ds(idx * m, m), :]

        def copy(a, k, slot):
            px, py, pc = peers[k]
            return pltpu.make_async_remote_copy(
                src_ref=block(a, 4 * px + 2 * py + pc), dst_ref=o_refs[a].at[slot],
                send_sem=send_sems.at[a, k], recv_sem=recv_sems.at[a, k], device_id=peers[k], device_id_type=MESH_ID)

        if ph == 1:
            return
        mine = [pltpu.make_async_copy(block(a, me), o_refs[a].at[me], local_sems.at[a]) for a in range(n)]
        sends = [copy(a, k, me) for k in range(N_DEV - 1) for a in range(n)]
        if ph == 0:
            for cp in mine + sends:
                cp.start()
        else:
            for k in range(N_DEV - 1):
                px, py, pc = peers[k]
                for a in range(n):
                    copy(a, k, 4 * px + 2 * py + pc).wait_recv()
            for cp in sends:
                cp.wait_send()
            for cp in mine:
                cp.wait()


def _comm_call(comm, name):
    n = comm.n

    def body(*refs):
        for ph in range(3):
            comm.phase(ph, refs[:n], refs[n:2 * n], refs[2 * n:])

    return pl.pallas_call(
        body, out_shape=comm.out_shape, in_specs=[ANY] * n, out_specs=[ANY] * n, scratch_shapes=comm.scratch, name=name,
    )(*comm.arrays)


def _host_comm(comm, when, step, total, cin, cout, csem):
    if comm is None:
        return
    sched = {0: 0, 1: (3 * total) // 4, 2: total - 1}
    phases = (0, 1) if when == "early" else (2,)
    for ph in phases:
        if ph == 1 and comm.kind == "exchange":
            continue

        @pl.when(step == sched[ph])
        def _(ph=ph):
            comm.phase(ph, cin, cout, csem)


def _all_gather(shards, name="all_gather"):
    return _comm_call(_Comm("gather", shards), name)


def _row_tile(r, target=256):
    best = None
    for t in range(8, min(r, target) + 1, 8):
        if r % t == 0:
            best = t
    return best if best is not None else r


def _sum8(parts, name="sum8"):
    _, M, N = parts.shape
    tr = _row_tile(M, 128)

    def body(p_ref, o_ref):
        acc = p_ref[0].astype(F32)
        for d in range(1, N_DEV):
            acc = acc + p_ref[d].astype(F32)
        o_ref[...] = acc

    return pl.pallas_call(
        body,
        grid=(M // tr,),
        in_specs=[pl.BlockSpec((N_DEV, tr, N), lambda i: (0, i, 0))],
        out_specs=pl.BlockSpec((tr, N), lambda i: (i, 0)),
        out_shape=jax.ShapeDtypeStruct((M, N), F32),
        compiler_params=_cp("parallel"),
        name=name,
    )(parts)


def _adamw(w, g, m, v, name="adamw"):
    shape = w.shape
    if w.ndim == 1:
        r2 = (1, shape[0])
    else:
        r2 = (int(np.prod(shape[:-1])), shape[-1])
    R, C = r2
    tr = _row_tile(R)
    bc1 = 1.0 - ADAM_B1 ** ADAM_STEP
    bc2 = 1.0 - ADAM_B2 ** ADAM_STEP

    def body(w_ref, g_ref, m_ref, v_ref, d_ref, nm_ref, nv_ref):
        gv = g_ref[...]
        nm = ADAM_B1 * m_ref[...] + (1.0 - ADAM_B1) * gv
        nv = ADAM_B2 * v_ref[...] + (1.0 - ADAM_B2) * (gv * gv)
        d_ref[...] = -ADAM_LR * ((nm / bc1) / (jnp.sqrt(nv / bc2) + ADAM_EPS) + ADAM_WD * w_ref[...])
        nm_ref[...] = nm
        nv_ref[...] = nv

    spec = pl.BlockSpec((tr, C), lambda i: (i, 0))
    out = pl.pallas_call(
        body,
        grid=(R // tr,),
        in_specs=[spec] * 4,
        out_specs=[spec] * 3,
        out_shape=[jax.ShapeDtypeStruct(r2, F32)] * 3,
        compiler_params=_cp("parallel"),
        name=name,
    )(w.reshape(r2), g.reshape(r2), m.reshape(r2), v.reshape(r2))
    return [o.reshape(shape) for o in out]


WEIGHTS = ['attn_norm_g', 'ffn_norm_g', 'ab_w_in', 'ab_w_out', 'ab_rel_bias', 'c_w_in', 'c_ln_g', 'c_ln_b', 'c_w_s', 'c_b_s',
           'c_w_out', 'ffn_w_up', 'ffn_conv_w', 'ffn_conv_b', 'ffn_w_down', 'final_norm_g']
SMALL_ORDER = ["attn_norm_g", "ffn_norm_g", "rel_bias", "ln_g", "ln_b", "w_s", "b_s", "conv_w", "conv_b", "final_g"]
PACK_ROW = 1024


def _pack(arrs):
    flat = jnp.concatenate([a.reshape(-1) for a in arrs])
    n = flat.shape[0]
    padded = -(-n // PACK_ROW) * PACK_ROW
    return jnp.pad(flat, (0, padded - n)).reshape(padded // 128, 128)


def _unpack(flat, shapes):
    out, off = [], 0
    for s in shapes:
        n = int(np.prod(s))
        out.append(flat[off:off + n].reshape(s))
        off += n
    return out


def _step(P):
    x, target = P["x"][0], P["loss_target"][0]
    me = 4 * lax.axis_index("x") + 2 * lax.axis_index("y") + lax.axis_index("c")
    n_up = P["ffn_w_up"].shape[0]
    Fc = P["ffn_conv_w"].shape[-1]
    Lc = P["c_ln_g"].shape[-1]

    first = _Comm("gather", [P["ab_w_in"][0].T.astype(BF16), _pack([P["ffn_conv_w"], P["c_ln_g"], P["c_ln_b"]])])
    hn0, full = _rms_fwd(x, P["attn_norm_g"][0], comm=first, name="rms_fwd_gather")
    wt = {"ab_in_t": full[0]}
    rest = {"ab_out": P["ab_w_out"][0], "c_in_t": P["c_w_in"][0].T, "c_out": P["c_w_out"][0],
            "up_t0": P["ffn_w_up"][0].T, "up_t1": P["ffn_w_up"][1].T, "down0": P["ffn_w_down"][0], "down1": P["ffn_w_down"][1]}
    rest_shards = [rest[k].astype(BF16) for k in REST]
    sm = full[-1].reshape(N_DEV, -1)
    conv_w = sm[:, :n_up * 3 * Fc].reshape(N_DEV, n_up, 3, Fc).transpose(1, 2, 0, 3).reshape(n_up, 3, N_DEV * Fc)
    off = n_up * 3 * Fc
    ln_g = sm[:, off:off + Lc].reshape(N_DEV * Lc)
    ln_b = sm[:, off + Lc:off + 2 * Lc].reshape(N_DEV * Lc)
    small = {"attn_norm_g": P["attn_norm_g"], "ffn_norm_g": P["ffn_norm_g"], "rel_bias": P["ab_rel_bias"][0],
             "ln_g": ln_g, "ln_b": ln_b, "w_s": P["c_w_s"][0], "b_s": P["c_b_s"][0], "conv_w": conv_w,
             "conv_b": P["ffn_conv_b"], "final_g": P["final_norm_g"]}

    loss_part, grad_x, gw, gs, recv = _local_step(x, target, wt, small, rest_shards=rest_shards, overlap=True, hn0=hn0)
    loss = lax.psum(loss_part, ("x", "y", "c"))

    s8 ={k: _sum8(recv[k], name="sum8") for k in ["ab_in_t"] + REST}
    g_big = {"ab_w_in": s8["ab_in_t"].T[None], "ab_w_out": s8["ab_out"][None], "c_w_in": s8["c_in_t"].T[None],
             "c_w_out": s8["c_out"][None], "ffn_w_up": jnp.stack([s8["up_t0"].T, s8["up_t1"].T]),
             "ffn_w_down": jnp.stack([s8["down0"], s8["down1"]])}

    packed = _pack([gs[k] for k in SMALL_ORDER])
    gathered = _all_gather([packed], name="gather_small_grads")[0]
    tot = _sum8(gathered.reshape(N_DEV, packed.shape[0], 128), name="sum8_small").reshape(-1)
    gsm = dict(zip(SMALL_ORDER, _unpack(tot, [gs[k].shape for k in SMALL_ORDER])))
    grads = dict(g_big)
    grads["attn_norm_g"] = gsm["attn_norm_g"]
    grads["ffn_norm_g"] = gsm["ffn_norm_g"]
    grads["ab_rel_bias"] = gsm["rel_bias"][None]
    grads["c_ln_g"] = lax.dynamic_slice(gsm["ln_g"], (me * Lc,), (Lc,))[None]
    grads["c_ln_b"] = lax.dynamic_slice(gsm["ln_b"], (me * Lc,), (Lc,))[None]
    grads["c_w_s"] = gsm["w_s"][None]
    grads["c_b_s"] = gsm["b_s"][None]
    grads["ffn_conv_w"] = lax.dynamic_slice(gsm["conv_w"], (0, 0, me * Fc), (n_up, 3, Fc))
    grads["ffn_conv_b"] = gsm["conv_b"]
    grads["final_norm_g"] = gsm["final_g"]

    delta, new_m, new_v = {}, {}, {}
    for k in WEIGHTS:
        delta[k], new_m[k], new_v[k] = _adamw(P[k], grads[k], P["m_" + k], P["v_" + k], name="adamw")
    return (loss, grad_x[None], *[grads[k] for k in WEIGHTS], *[delta[k] for k in WEIGHTS],
            *[new_m[k] for k in WEIGHTS], *[new_v[k] for k in WEIGHTS])


def kernel(x, attn_norm_g, ffn_norm_g, ab_w_in, ab_w_out, ab_rel_bias, c_w_in, c_ln_g, c_ln_b, c_w_s, c_b_s, c_w_out, ffn_w_up, ffn_conv_w, ffn_conv_b, ffn_w_down, final_norm_g, loss_target, m_attn_norm_g, m_ffn_norm_g, m_ab_w_in, m_ab_w_out, m_ab_rel_bias, m_c_w_in, m_c_ln_g, m_c_ln_b, m_c_w_s, m_c_b_s, m_c_w_out, m_ffn_w_up, m_ffn_conv_w, m_ffn_conv_b, m_ffn_w_down, m_final_norm_g, v_attn_norm_g, v_ffn_norm_g, v_ab_w_in, v_ab_w_out, v_ab_rel_bias, v_c_w_in, v_c_ln_g, v_c_ln_b, v_c_w_s, v_c_b_s, v_c_w_out, v_ffn_w_up, v_ffn_conv_w, v_ffn_conv_b, v_ffn_w_down, v_final_norm_g):
    return _step(dict(locals()))
```

```python
import functools

import numpy as np
import jax
import jax.numpy as jnp
from jax import lax
from jax.experimental import pallas as pl
from jax.experimental.pallas import tpu as pltpu

F32 = jnp.float32
BF16 = jnp.bfloat16

D_MODEL = 1024
CHUNK = 64
EPS = 1e-6
NEG_INF = -1e30
RET_HEADS = 4
RET_QK_DIM = 128
RET_V_DIM = 256
ATT_HEADS = 8
ATT_HEAD_DIM = 64
ATT_PAST = 8
ATT_BAND = (ATT_PAST + 1) * CHUNK
MAX_REL = 128
N_REL = 2 * MAX_REL + 1
N_REL_PAD = 384
SGU_BLOCK = 128
SGU_GROUPS = 8
SGU_WIDTH = 2048
SGU_GW = SGU_WIDTH // SGU_GROUPS
FFN_HIDDEN = 2816
RET_W = 2 * RET_HEADS * RET_QK_DIM + 2 * RET_HEADS * RET_V_DIM
ATT_W = 3 * ATT_HEADS * ATT_HEAD_DIM
N_DEV = 8

ADAM_LR = 0.001
ADAM_B1 = 0.9
ADAM_B2 = 0.999
ADAM_EPS = 1e-08
ADAM_WD = 0.01
ADAM_STEP = 10

VMEM_LIMIT = 52 * 1024 * 1024


def _cp(*sem):
    return pltpu.CompilerParams(dimension_semantics=sem if sem else None, vmem_limit_bytes=VMEM_LIMIT)


def _tile(n, target):
    if n <= target:
        return n
    best = None
    for t in range(128, target + 1, 128):
        if n % t == 0:
            best = t
    assert best is not None, (n, target)
    return best


def _gelu(x):
    c = 0.7978845608028654
    return 0.5 * x * (1.0 + jnp.tanh(c * (x + 0.044715 * x * x * x)))


def _gelu_and_grad(x):
    c = 0.7978845608028654
    x2 = x * x
    t = jnp.tanh(c * (x + 0.044715 * x * x2))
    cdf = 0.5 * (1.0 + t)
    grad = cdf + x * (0.5 * c) * (1.0 - t * t) * (1.0 + 3.0 * 0.044715 * x2)
    return x * cdf, grad


def _dot(a, b, dims):
    return lax.dot_general(a, b, (dims, ((), ())), preferred_element_type=F32)


NN = ((1,), (0,))
NT = ((1,), (1,))
TN = ((0,), (0,))


def _mm(a, b, mode, out_dtype, res=None, name="mm", tm_t=None, tn_t=None, tk_t=None, comm=None, rows=None):
    if mode == "nt":
        (M, K), N = a.shape, b.shape[0]
        dm, dn, dk = (1024, 2816, K) if N <= 2816 else (1024, 2048, K)
    elif mode == "nn":
        (M, K), N = a.shape, b.shape[1]
        dm, dn, dk = (1024 if K <= 3072 else 512), 1024, K
    else:
        (K, M), N = a.shape, b.shape[1]
        dm, dn, dk = 1536, 1024, 2048
    tm, tn, tk = _tile(M, tm_t or dm), _tile(N, tn_t or dn), _tile(K, tk_t or dk)
    nk = K // tk
    dims = {"nt": NT, "nn": NN, "tn": TN}[mode]
    a_spec = pl.BlockSpec((tk, tm), lambda i, j, k: (k, i)) if mode == "tn" else pl.BlockSpec((tm, tk), lambda i, j, k: (i, k))
    b_spec = pl.BlockSpec((tn, tk), lambda i, j, k: (j, k)) if mode == "nt" else pl.BlockSpec((tk, tn), lambda i, j, k: (k, j))
    gi, gj = M // tm, N // tn
    out_rows, row0, into = rows if rows else (M, 0, None)
    assert row0 % tm == 0 and not (comm and into is not None)
    o_spec = pl.BlockSpec((tm, tn), lambda i, j, k: (i + row0 // tm, j))
    has_res = res is not None
    nc = 1 if into is not None else (comm.n if comm else 0)
    n_in = 3 if has_res else 2

    def body(*refs):
        a_ref, b_ref = refs[:2]
        r_ref = refs[2] if has_res else None
        nco = comm.n if comm else 0
        cin = refs[n_in:n_in + nc]
        o_ref = refs[n_in + nc]
        cout = refs[n_in + nc + 1:n_in + nc + 1 + nco]
        scratch = refs[n_in + nc + 1 + nco:]
        csem = scratch[1:] if nk > 1 else scratch
        step = (pl.program_id(0) * gj + pl.program_id(1)) * nk + pl.program_id(2)
        _host_comm(comm, "early", step, gi * gj * nk, cin, cout, csem)
        p = _dot(a_ref[...].astype(BF16), b_ref[...].astype(BF16), dims)
        if nk == 1:
            if has_res:
                p = p + r_ref[...]
            o_ref[...] = p.astype(out_dtype)
        else:
            acc = scratch[0]
            k = pl.program_id(2)

            @pl.when(k == 0)
            def _():
                acc[...] = p

            @pl.when(k > 0)
            def _():
                acc[...] += p

            @pl.when(k == nk - 1)
            def _():
                t = acc[...]
                if has_res:
                    t = t + r_ref[...]
                o_ref[...] = t.astype(out_dtype)
        _host_comm(comm, "late", step, gi * gj * nk, cin, cout, csem)

    in_specs = [a_spec, b_spec] + ([o_spec] if has_res else []) + [ANY] * nc
    args = (a, b) + ((res,) if has_res else ()) + ((into,) if into is not None else tuple(comm.arrays if comm else ()))
    out = pl.pallas_call(
        body,
        grid=(gi, gj, nk),
        in_specs=in_specs,
        out_specs=[o_spec] + ([ANY] * nc if comm else []),
        out_shape=[jax.ShapeDtypeStruct((out_rows, N), out_dtype)] + (comm.out_shape if comm else []),
        scratch_shapes=([pltpu.VMEM((tm, tn), F32)] if nk > 1 else []) + (comm.scratch if comm else []),
        input_output_aliases={n_in: 0} if into is not None else {},
        compiler_params=_cp("arbitrary", "arbitrary", "arbitrary") if comm else _cp("parallel", "parallel", "arbitrary"),
        name=name,
    )(*args)
    return (out[0], list(out[1:])) if comm else out[0]


def _mm_rows(a, b, res, name, norm_g=None, bwd=None, loss=None, second=None, tm=512):
    M, K = a.shape
    Dm = b.shape[1]
    if norm_g is not None and K <= 2048:
        tm = 2 * tm
    tm = min(tm, M)
    row = pl.BlockSpec((tm, Dm), lambda i: (i, 0))
    vec = pl.BlockSpec((1, Dm), lambda i: (0, 0))
    a_spec = pl.BlockSpec((tm, K), lambda i: (i, 0))
    b_spec = pl.BlockSpec((K, Dm), lambda i: (0, 0))

    if loss is not None:
        target, g = loss

        def body(a_ref, b_ref, r_ref, g_ref, t_ref, loss_ref, dh_ref, dg_ref):
            @pl.when(pl.program_id(0) == 0)
            def _():
                loss_ref[...] = jnp.zeros_like(loss_ref)
                dg_ref[...] = jnp.zeros_like(dg_ref)

            x = _dot(a_ref[...].astype(BF16), b_ref[...].astype(BF16), NN) + r_ref[...]
            gv = g_ref[...]
            r = lax.rsqrt(jnp.mean(x * x, axis=-1, keepdims=True) + EPS)
            xhat = x * r
            e = xhat * gv - t_ref[...]
            loss_ref[...] += jnp.full((1, 128), 0.5 / Dm, F32) * jnp.sum(e * e)
            dy = e * (1.0 / Dm)
            dg_ref[...] += jnp.sum(dy * xhat, axis=0, keepdims=True)
            dx = dy * gv
            m = jnp.mean(dx * xhat, axis=-1, keepdims=True)
            dh_ref[...] = r * (dx - xhat * m)

        return pl.pallas_call(
            body, grid=(M // tm,), in_specs=[a_spec, b_spec, row, vec, row],
            out_specs=[pl.BlockSpec((1, 128), lambda i: (0, 0)), row, vec],
            out_shape=[jax.ShapeDtypeStruct((1, 128), F32), jax.ShapeDtypeStruct((M, Dm), F32), jax.ShapeDtypeStruct((1, Dm), F32)],
            compiler_params=_cp("arbitrary"), name=name,
        )(a, b, res, g.reshape(1, Dm), target)

    if bwd is None:
        def body(a_ref, b_ref, r_ref, g_ref, o_ref, n_ref):
            t = _dot(a_ref[...].astype(BF16), b_ref[...].astype(BF16), NN) + r_ref[...]
            o_ref[...] = t
            r = lax.rsqrt(jnp.mean(t * t, axis=-1, keepdims=True) + EPS)
            n_ref[...] = (t * r * g_ref[...]).astype(n_ref.dtype)

        return pl.pallas_call(
            body, grid=(M // tm,), in_specs=[a_spec, b_spec, row, vec], out_specs=[row, row],
            out_shape=[jax.ShapeDtypeStruct((M, Dm), F32), jax.ShapeDtypeStruct((M, Dm), BF16)],
            compiler_params=_cp("parallel"), name=name,
        )(a, b, res, norm_g.reshape(1, Dm))

    h, g, dres = bwd
    has_res = res is not None
    has2 = second is not None

    def body(*refs):
        a_ref, b_ref = refs[:2]
        h_ref, g_ref, dres_ref, dh_ref, dg_ref = refs[-5:]

        @pl.when(pl.program_id(0) == 0)
        def _():
            dg_ref[...] = jnp.zeros_like(dg_ref)

        d = _dot(a_ref[...].astype(BF16), b_ref[...].astype(BF16), NN)
        if has2:
            d = d + _dot(refs[2][...].astype(BF16), refs[3][...].astype(BF16), NN)
        if has_res:
            d = d + refs[4 if has2 else 2][...]
        x = h_ref[...]
        r = lax.rsqrt(jnp.mean(x * x, axis=-1, keepdims=True) + EPS)
        xhat = x * r
        dg_ref[...] += jnp.sum(d * xhat, axis=0, keepdims=True)
        dx = d * g_ref[...]
        m = jnp.mean(dx * xhat, axis=-1, keepdims=True)
        dh_ref[...] = dres_ref[...] + r * (dx - xhat * m)

    second_specs = [pl.BlockSpec((tm, second[0].shape[1]), lambda i: (i, 0)),
                    pl.BlockSpec(second[1].shape, lambda i: (0, 0))] if has2 else []
    return pl.pallas_call(
        body, grid=(M // tm,), in_specs=[a_spec, b_spec] + second_specs + ([row] if has_res else []) + [row, vec, row],
        out_specs=[row, vec],
        out_shape=[jax.ShapeDtypeStruct((M, Dm), F32), jax.ShapeDtypeStruct((1, Dm), F32)],
        compiler_params=_cp("arbitrary"), name=name,
    )(a, b, *(second if has2 else ()), *((res,) if has_res else ()), h, g.reshape(1, Dm), dres)


def _rms_fwd(h, g, comm=None, name="rms_fwd", tm=512):
    T, Dm = h.shape
    tm = min(tm, T)
    nc = comm.n if comm else 0
    ni = T // tm

    def body(*refs):
        h_ref, g_ref = refs[:2]
        cin, o_ref, cout, csem = refs[2:2 + nc], refs[2 + nc], refs[3 + nc:3 + 2 * nc], refs[3 + 2 * nc:]
        step = pl.program_id(0)
        _host_comm(comm, "early", step, ni, cin, cout, csem)
        x = h_ref[...]
        r = lax.rsqrt(jnp.mean(x * x, axis=-1, keepdims=True) + EPS)
        o_ref[...] = (x * r * g_ref[...]).astype(o_ref.dtype)
        _host_comm(comm, "late", step, ni, cin, cout, csem)

    out = pl.pallas_call(
        body,
        grid=(ni,),
        in_specs=[pl.BlockSpec((tm, Dm), lambda i: (i, 0)), pl.BlockSpec((1, Dm), lambda i: (0, 0))] + [ANY] * nc,
        out_specs=[pl.BlockSpec((tm, Dm), lambda i: (i, 0))] + [ANY] * nc,
        out_shape=[jax.ShapeDtypeStruct((T, Dm), BF16)] + (comm.out_shape if comm else []),
        scratch_shapes=comm.scratch if comm else [],
        compiler_params=_cp("arbitrary"),
        name=name,
    )(h, g.reshape(1, Dm), *(comm.arrays if comm else []))
    return (out[0], list(out[1:])) if comm else out[0]


HALO = 16


def _conv3(ext, w_ref, b_ref):
    return w_ref[0:1, :] * pltpu.roll(ext, 2, 0) + w_ref[1:2, :] * pltpu.roll(ext, 1, 0) + w_ref[2:3, :] * ext + b_ref[...]


def _ffn_up_mid(hf, w_up_t, cw, cb, name="ffn_up_mid", tm=512, tc=1408):
    T, Dm = hf.shape
    Fh = w_up_t.shape[0] // 2
    tm = min(tm, T)
    nj, ni = Fh // tc, T // tm

    rc = min(512, tm)

    def body(h_ref, wg_ref, wu_ref, cwg_ref, cwu_ref, cbg_ref, cbu_ref, zg_ref, zu_ref, a_ref, eg_ref, eu_ref, ng_ref, nu_ref):
        @pl.when(pl.program_id(1) == 0)
        def _():
            eg_ref[...] = jnp.zeros_like(eg_ref)
            eu_ref[...] = jnp.zeros_like(eu_ref)

        for r in range(tm // rc):
            ext = slice(r * rc, r * rc + HALO + rc)
            cg = _conv3(eg_ref[ext, :].astype(F32), cwg_ref, cbg_ref)[HALO:]
            cu = _conv3(eu_ref[ext, :].astype(F32), cwu_ref, cbu_ref)[HALO:]
            a_ref[r * rc:(r + 1) * rc, :] = _gelu(cg.astype(BF16)) * cu.astype(BF16)
            h = h_ref[r * rc:(r + 1) * rc, :]
            for w_ref, z_ref, n_ref in ((wg_ref, zg_ref, ng_ref), (wu_ref, zu_ref, nu_ref)):
                z = _dot(h, w_ref[...], NT).astype(BF16)
                z_ref[r * rc:(r + 1) * rc, :] = z
                n_ref[r * rc:(r + 1) * rc, :] = z
        for e_ref, n_ref in ((eg_ref, ng_ref), (eu_ref, nu_ref)):
            e_ref[0:HALO, :] = e_ref[tm:tm + HALO, :]
            e_ref[HALO:HALO + tm, :] = n_ref[...]

    last = ni - 1
    z_spec = pl.BlockSpec((tm, tc), lambda j, i: (jnp.minimum(i, last), j))
    return pl.pallas_call(
        body,
        grid=(nj, ni + 1),
        in_specs=[pl.BlockSpec((tm, Dm), lambda j, i: (jnp.minimum(i, last), 0)),
                  pl.BlockSpec((tc, Dm), lambda j, i: (j, 0)), pl.BlockSpec((tc, Dm), lambda j, i: (nj + j, 0)),
                  pl.BlockSpec((3, tc), lambda j, i: (0, j)), pl.BlockSpec((3, tc), lambda j, i: (0, nj + j)),
                  pl.BlockSpec((1, tc), lambda j, i: (0, j)), pl.BlockSpec((1, tc), lambda j, i: (0, nj + j))],
        out_specs=[z_spec, z_spec, pl.BlockSpec((tm, tc), lambda j, i: (jnp.maximum(i - 1, 0), j))],
        out_shape=[jax.ShapeDtypeStruct((T, Fh), BF16)] * 3,
        scratch_shapes=[pltpu.VMEM((HALO + tm, tc), BF16), pltpu.VMEM((HALO + tm, tc), BF16),
                        pltpu.VMEM((tm, tc), BF16), pltpu.VMEM((tm, tc), BF16)],
        compiler_params=_cp("arbitrary", "arbitrary"),
        name=name,
    )(hf, w_up_t, w_up_t, cw, cw, cb, cb)


def _ffn_mid_bwd(zg, zu, da, cw, cb, comm=None, name="ffn_mid_bwd", tm=2048, tc=256):
    T, Fh = zg.shape
    tm = min(tm, T)
    nj = Fh // tc
    hb = tm // HALO
    nhb = T // HALO

    nc = comm.n if comm else 0
    ni = T // tm

    def body(*refs):
        zg_ref, zu_ref, zgp_ref, zup_ref, zgn_ref, zun_ref, da_ref, dan_ref, wg_ref, wu_ref, bg_ref, bu_ref = refs[:12]
        cin = refs[12:12 + nc]
        dzg_ref, dzu_ref, dwg_ref, dwu_ref, dbg_ref, dbu_ref = refs[12 + nc:18 + nc]
        cout = refs[18 + nc:18 + 2 * nc]
        csem = refs[18 + 2 * nc:]
        i = pl.program_id(1)
        step = pl.program_id(0) * ni + i
        _host_comm(comm, "early", step, nj * ni, cin, cout, csem)
        first = i == 0
        last = i == ni - 1

        @pl.when(first)
        def _():
            dwg_ref[...] = jnp.zeros_like(dwg_ref)
            dwu_ref[...] = jnp.zeros_like(dwu_ref)
            dbg_ref[...] = jnp.zeros_like(dbg_ref)
            dbu_ref[...] = jnp.zeros_like(dbu_ref)

        def ext_of(p_ref, c_ref, n_ref):
            p = jnp.where(first, 0.0, p_ref[...].astype(F32))
            return jnp.concatenate([p, c_ref[...].astype(F32), n_ref[...].astype(F32)], axis=0)

        zge = ext_of(zgp_ref, zg_ref, zgn_ref)
        zue = ext_of(zup_ref, zu_ref, zun_ref)
        dan = jnp.where(last, 0.0, dan_ref[...].astype(F32))
        dae = jnp.concatenate([jnp.zeros((HALO, tc), F32), da_ref[...].astype(F32), dan], axis=0)
        zg1, zg2 = pltpu.roll(zge, 1, 0), pltpu.roll(zge, 2, 0)
        zu1, zu2 = pltpu.roll(zue, 1, 0), pltpu.roll(zue, 2, 0)
        cg = wg_ref[0:1, :] * zg2 + wg_ref[1:2, :] * zg1 + wg_ref[2:3, :] * zge + bg_ref[...]
        cu = wu_ref[0:1, :] * zu2 + wu_ref[1:2, :] * zu1 + wu_ref[2:3, :] * zue + bu_ref[...]
        gel, dgel = (t.astype(F32) for t in _gelu_and_grad(cg.astype(BF16)))
        dcg = dae * cu * dgel
        dcu = dae * gel
        lo, hi = HALO, HALO + tm

        def back(dc, taps, w_ref, dz_ref, dw_ref, db_ref):
            n = dc.shape[0]
            dz = w_ref[2:3, :] * dc + w_ref[1:2, :] * pltpu.roll(dc, n - 1, 0) + w_ref[0:1, :] * pltpu.roll(dc, n - 2, 0)
            dz_ref[...] = dz[lo:hi].astype(dz_ref.dtype)
            dcc = dc[lo:hi]
            db_ref[...] += jnp.sum(dcc, axis=0, keepdims=True)
            for k, tap in enumerate(taps):
                dw_ref[k:k + 1, :] += jnp.sum(dcc * tap[lo:hi], axis=0, keepdims=True)

        back(dcg, (zg2, zg1, zge), wg_ref, dzg_ref, dwg_ref, dbg_ref)
        back(dcu, (zu2, zu1, zue), wu_ref, dzu_ref, dwu_ref, dbu_ref)
        _host_comm(comm, "late", step, nj * ni, cin, cout, csem)

    cur = pl.BlockSpec((tm, tc), lambda j, i: (i, j))
    prev = pl.BlockSpec((HALO, tc), lambda j, i: (jnp.maximum(i * hb - 1, 0), j))
    nxt = pl.BlockSpec((HALO, tc), lambda j, i: (jnp.minimum((i + 1) * hb, nhb - 1), j))
    wg = pl.BlockSpec((3, tc), lambda j, i: (0, j))
    wu = pl.BlockSpec((3, tc), lambda j, i: (0, j + nj))
    bg = pl.BlockSpec((1, tc), lambda j, i: (0, j))
    bu = pl.BlockSpec((1, tc), lambda j, i: (0, j + nj))
    dw = pl.BlockSpec((3, tc), lambda j, i: (0, j))
    db = pl.BlockSpec((1, tc), lambda j, i: (0, j))
    out = pl.pallas_call(
        body,
        grid=(nj, ni),
        in_specs=[cur, cur, prev, prev, nxt, nxt, cur, nxt, wg, wu, bg, bu] + [ANY] * nc,
        out_specs=[cur, cur, dw, dw, db, db] + [ANY] * nc,
        out_shape=[jax.ShapeDtypeStruct((T, Fh), BF16), jax.ShapeDtypeStruct((T, Fh), BF16),
                   jax.ShapeDtypeStruct((3, Fh), F32), jax.ShapeDtypeStruct((3, Fh), F32),
                   jax.ShapeDtypeStruct((1, Fh), F32), jax.ShapeDtypeStruct((1, Fh), F32)] + (comm.out_shape if comm else []),
        scratch_shapes=comm.scratch if comm else [],
        compiler_params=_cp("arbitrary", "arbitrary"),
        name=name,
    )(zg, zu, zg, zu, zg, zu, da, da, cw, cw, cb, cb, *(comm.arrays if comm else []))
    dzg, dzu, dwg, dwu, dbg, dbu = out[:6]
    return dzg, dzu, jnp.concatenate([dwg, dwu], axis=1), jnp.concatenate([dbg, dbu], axis=1), list(out[6:])


def _sgu_mask():
    r = lax.broadcasted_iota(jnp.int32, (SGU_BLOCK, SGU_BLOCK), 0)
    c = lax.broadcasted_iota(jnp.int32, (SGU_BLOCK, SGU_BLOCK), 1)
    return (c < CHUNK) | (r >= CHUNK)


def _sgu_fwd(zz, ln_g, ln_b, w_s, b_s, name="sgu_fwd", tm=256):
    T = zz.shape[0]
    tm = min(tm, T)
    W = SGU_WIDTH

    def body(zu_ref, zv_ref, g_ref, b_ref, ws_ref, bs_ref, y_ref):
        u = _gelu(zu_ref[...]).astype(F32)
        v = _gelu(zv_ref[...]).astype(F32)
        mu = jnp.mean(v, axis=-1, keepdims=True)
        xc = v - mu
        rstd = lax.rsqrt(jnp.mean(xc * xc, axis=-1, keepdims=True) + EPS)
        vn = (xc * rstd * g_ref[...] + b_ref[...]).astype(BF16)
        mask = _sgu_mask()
        for g in range(SGU_GROUPS):
            wm = jnp.where(mask, ws_ref[g], 0.0).astype(BF16)
            cs = slice(g * SGU_GW, (g + 1) * SGU_GW)
            for blk in range(tm // SGU_BLOCK):
                rs = slice(blk * SGU_BLOCK, (blk + 1) * SGU_BLOCK)
                mixed = _dot(wm, vn[rs, cs], NN) + bs_ref[g]
                y_ref[rs, cs] = (u[rs, cs] * mixed).astype(y_ref.dtype)

    return pl.pallas_call(
        body,
        grid=(T // tm,),
        in_specs=[pl.BlockSpec((tm, W), lambda i: (i, 0)), pl.BlockSpec((tm, W), lambda i: (i, 1)),
                  pl.BlockSpec((1, W), lambda i: (0, 0)), pl.BlockSpec((1, W), lambda i: (0, 0)),
                  pl.BlockSpec((SGU_GROUPS, SGU_BLOCK, SGU_BLOCK), lambda i: (0, 0, 0)),
                  pl.BlockSpec((SGU_GROUPS, SGU_BLOCK, 1), lambda i: (0, 0, 0))],
        out_specs=pl.BlockSpec((tm, W), lambda i: (i, 0)),
        out_shape=jax.ShapeDtypeStruct((T, W), BF16),
        compiler_params=_cp("parallel"),
        name=name,
    )(zz, zz, ln_g.reshape(1, W), ln_b.reshape(1, W), w_s, b_s.reshape(SGU_GROUPS, SGU_BLOCK, 1))


def _sgu_bwd(zz, dy, ln_g, ln_b, w_s, b_s, name="sgu_bwd", tm=256):
    T = zz.shape[0]
    tm = min(tm, T)
    W = SGU_WIDTH

    def body(zu_ref, zv_ref, dy_ref, g_ref, b_ref, ws_ref, bs_ref, dzz_ref, dws_ref, dbs_ref, dg_ref, db_ref, dvn_ref):
        i = pl.program_id(0)

        @pl.when(i == 0)
        def _():
            dws_ref[...] = jnp.zeros_like(dws_ref)
            dbs_ref[...] = jnp.zeros_like(dbs_ref)
            dg_ref[...] = jnp.zeros_like(dg_ref)
            db_ref[...] = jnp.zeros_like(db_ref)

        u, du_dz = (t.astype(F32) for t in _gelu_and_grad(zu_ref[...]))
        v, dv_dz = (t.astype(F32) for t in _gelu_and_grad(zv_ref[...]))
        mu = jnp.mean(v, axis=-1, keepdims=True)
        xc = v - mu
        rstd = lax.rsqrt(jnp.mean(xc * xc, axis=-1, keepdims=True) + EPS)
        xhat = xc * rstd
        gv = g_ref[...]
        vn = (xhat * gv + b_ref[...]).astype(BF16)
        dyv = dy_ref[...].astype(F32)
        mask = _sgu_mask()
        for g in range(SGU_GROUPS):
            wm = jnp.where(mask, ws_ref[g], 0.0).astype(BF16)
            cs = slice(g * SGU_GW, (g + 1) * SGU_GW)
            dw_acc = jnp.zeros((SGU_BLOCK, SGU_BLOCK), F32)
            db_acc = jnp.zeros((SGU_BLOCK, 1), F32)
            for blk in range(tm // SGU_BLOCK):
                rs = slice(blk * SGU_BLOCK, (blk + 1) * SGU_BLOCK)
                vn_bg = vn[rs, cs]
                mixed = _dot(wm, vn_bg, NN) + bs_ref[g]
                dy_bg = dyv[rs, cs]
                dmixed = dy_bg * u[rs, cs]
                dmb = dmixed.astype(BF16)
                dw_acc += _dot(dmb, vn_bg, NT)
                db_acc += jnp.sum(dmixed, axis=1, keepdims=True)
                dvn_ref[rs, cs] = _dot(wm, dmb, TN)
                dzz_ref[rs, cs] = (dy_bg * mixed * du_dz[rs, cs]).astype(dzz_ref.dtype)
            dws_ref[g] += jnp.where(mask, dw_acc, 0.0)
            dbs_ref[g] += db_acc
        dvn = dvn_ref[...]
        dg_ref[...] += jnp.sum(dvn * xhat, axis=0, keepdims=True)
        db_ref[...] += jnp.sum(dvn, axis=0, keepdims=True)
        dxh = dvn * gv
        m1 = jnp.mean(dxh, axis=-1, keepdims=True)
        m2 = jnp.mean(dxh * xhat, axis=-1, keepdims=True)
        dv = rstd * (dxh - m1 - xhat * m2)
        dzz_ref[:, W:] = (dv * dv_dz).astype(dzz_ref.dtype)

    vec = pl.BlockSpec((1, W), lambda i: (0, 0))
    ws_spec = pl.BlockSpec((SGU_GROUPS, SGU_BLOCK, SGU_BLOCK), lambda i: (0, 0, 0))
    bs_spec = pl.BlockSpec((SGU_GROUPS, SGU_BLOCK, 1), lambda i: (0, 0, 0))
    return pl.pallas_call(
        body,
        grid=(T // tm,),
        in_specs=[pl.BlockSpec((tm, W), lambda i: (i, 0)), pl.BlockSpec((tm, W), lambda i: (i, 1)),
                  pl.BlockSpec((tm, W), lambda i: (i, 0)), vec, vec, ws_spec, bs_spec],
        out_specs=[pl.BlockSpec((tm, 2 * W), lambda i: (i, 0)), ws_spec, bs_spec, vec, vec],
        out_shape=[jax.ShapeDtypeStruct((T, 2 * W), BF16),
                   jax.ShapeDtypeStruct((SGU_GROUPS, SGU_BLOCK, SGU_BLOCK), F32),
                   jax.ShapeDtypeStruct((SGU_GROUPS, SGU_BLOCK, 1), F32),
                   jax.ShapeDtypeStruct((1, W), F32), jax.ShapeDtypeStruct((1, W), F32)],
        scratch_shapes=[pltpu.VMEM((tm, W), F32)],
        compiler_params=_cp("arbitrary"),
        name=name,
    )(zz, zz, dy, ln_g.reshape(1, W), ln_b.reshape(1, W), w_s, b_s.reshape(SGU_GROUPS, SGU_BLOCK, 1))


RET_TR = 256
RET_BLK = 256
QK_SCALE = RET_QK_DIM ** -0.5


def _ret_tables(T):
    half = RET_QK_DIM // 2
    inv = 1.0 / (10000.0 ** jnp.linspace(0.0, 1.0, half, dtype=F32))
    inv2 = jnp.concatenate([inv, inv])[None, :]
    sgn = jnp.concatenate([-jnp.ones((half,), F32), jnp.ones((half,), F32)])[None, :]
    tr = min(RET_TR, T)

    def trig(pos):
        ang = pos.astype(F32)[:, None] * inv2
        return jnp.stack([jnp.cos(ang), jnp.sin(ang), sgn * jnp.sin(ang)])

    tile_tab = jnp.pad(trig(jnp.arange(T // tr) * tr).transpose(1, 0, 2), ((0, 0), (0, 5), (0, 0)))
    row_tab = trig(jnp.arange(tr))
    log_g = jnp.log1p(-jnp.exp2(-5.0 - jnp.arange(RET_HEADS, dtype=F32)))
    idx = jnp.arange(RET_BLK, dtype=F32)
    dist = idx[:, None] - idx[None, :]
    cq, ck = jnp.arange(RET_BLK)[:, None] // CHUNK, jnp.arange(RET_BLK)[None, :] // CHUNK
    expo = jnp.where(ck == cq, jnp.abs(dist), dist)
    d_blk = jnp.where((ck <= cq)[None], jnp.exp(log_g[:, None, None] * expo[None]), 0.0)
    k_dec = jnp.exp(log_g[:, None] * (RET_BLK - 1 - idx)[None, :])[:, :, None]
    q_dec = jnp.exp(log_g[:, None] * (idx + 1.0)[None, :])[:, :, None]
    c_dec = jnp.exp(log_g * RET_BLK)[:, None, None]
    return tile_tab, row_tab, d_blk, q_dec, k_dec, c_dec


def _rot(x, c, s):
    return x * c + pltpu.roll(x, RET_QK_DIM // 2, 1) * s


def _rot_tables(tt_ref, rt_ref):
    ca, sa, ga = tt_ref[0:1, :], tt_ref[1:2, :], tt_ref[2:3, :]
    cb, sb, gb = rt_ref[0], rt_ref[1], rt_ref[2]
    return ca * cb - sa * sb, ga * cb + ca * gb


def _ret_specs(tr, rev, nb):
    ix = (lambda n: nb - 1 - n) if rev else (lambda n: n)
    tt = pl.BlockSpec((None, 8, RET_QK_DIM), lambda n: (ix(n), 0, 0))
    rt = pl.BlockSpec((3, tr, RET_QK_DIM), lambda n: (0, 0, 0))
    dm = pl.BlockSpec((RET_HEADS, RET_BLK, RET_BLK), lambda n: (0, 0, 0))
    dv = pl.BlockSpec((RET_HEADS, RET_BLK, 1), lambda n: (0, 0, 0))
    dc = pl.BlockSpec((RET_HEADS, 1, 1), lambda n: (0, 0, 0))
    return ix, [tt, rt, dm, dv, dv, dc]


def _ret_fwd(z_a, tables, name="ret_fwd"):
    T = z_a.shape[0]
    tr = min(RET_TR, T)
    cpb = tr // RET_BLK
    nb = T // tr
    QW, VW = RET_HEADS * RET_QK_DIM, RET_HEADS * RET_V_DIM
    ix, tab_specs = _ret_specs(tr, False, nb)

    def body(z_ref, tt_ref, rt_ref, dm_ref, qd_ref, kd_ref, cd_ref, y_ref, st_ref, state):
        @pl.when(pl.program_id(0) == 0)
        def _():
            state[...] = jnp.zeros_like(state)

        rot_c, rot_s = _rot_tables(tt_ref, rt_ref)
        for c in range(cpb):
            for h in range(RET_HEADS):
                rs = slice(c * RET_BLK, (c + 1) * RET_BLK)
                cc, ss = rot_c[rs, :], rot_s[rs, :]
                q = z_ref[rs, h * RET_QK_DIM:(h + 1) * RET_QK_DIM].astype(F32)
                k = z_ref[rs, QW + h * RET_QK_DIM:QW + (h + 1) * RET_QK_DIM].astype(F32)
                v = z_ref[rs, 2 * QW + h * RET_V_DIM:2 * QW + (h + 1) * RET_V_DIM]
                gt = z_ref[rs, 2 * QW + VW + h * RET_V_DIM:2 * QW + VW + (h + 1) * RET_V_DIM].astype(F32)
                qr = _rot(q, cc, ss)
                kr = _rot(k, cc, ss) * QK_SCALE
                s_old = state[h]
                sb = s_old.astype(BF16)
                st_ref[c, h] = sb
                s = _dot(qr.astype(BF16), kr.astype(BF16), NT) * dm_ref[h]
                o = _dot(s.astype(BF16), v, NN) + _dot((qr * qd_ref[h]).astype(BF16), sb, NN)
                state[h] = s_old * cd_ref[h] + _dot((kr * kd_ref[h]).astype(BF16), v, TN)
                mu = jnp.mean(o, axis=-1, keepdims=True)
                oc = o - mu
                rn = oc * lax.rsqrt(jnp.mean(oc * oc, axis=-1, keepdims=True) + EPS)
                silu = gt / (1.0 + jnp.exp(-gt))
                y_ref[rs, h * RET_V_DIM:(h + 1) * RET_V_DIM] = (silu * rn).astype(y_ref.dtype)

    return pl.pallas_call(
        body,
        grid=(nb,),
        in_specs=[pl.BlockSpec((tr, RET_W), lambda n: (n, 0))] + tab_specs,
        out_specs=[pl.BlockSpec((tr, VW), lambda n: (n, 0)),
                   pl.BlockSpec((cpb, RET_HEADS, RET_QK_DIM, RET_V_DIM), lambda n: (n, 0, 0, 0))],
        out_shape=[jax.ShapeDtypeStruct((T, Y_COLS), BF16),
                   jax.ShapeDtypeStruct((T // RET_BLK, RET_HEADS, RET_QK_DIM, RET_V_DIM), BF16)],
        scratch_shapes=[pltpu.VMEM((RET_HEADS, RET_QK_DIM, RET_V_DIM), F32)],
        compiler_params=_cp("arbitrary"),
        name=name,
    )(z_a, *tables)


def _ret_bwd(z_a, dy, states, tables, name="ret_bwd"):
    T = z_a.shape[0]
    tr = min(RET_TR, T)
    cpb = tr // RET_BLK
    nb = T // tr
    QW, VW = RET_HEADS * RET_QK_DIM, RET_HEADS * RET_V_DIM
    ix, tab_specs = _ret_specs(tr, True, nb)

    def body(z_ref, dy_ref, st_ref, tt_ref, rt_ref, dm_ref, qd_ref, kd_ref, cd_ref, dz_ref, dstate):
        @pl.when(pl.program_id(0) == 0)
        def _():
            dstate[...] = jnp.zeros_like(dstate)

        rot_c, rot_s = _rot_tables(tt_ref, rt_ref)
        for c in reversed(range(cpb)):
            for h in range(RET_HEADS):
                rs = slice(c * RET_BLK, (c + 1) * RET_BLK)
                cc, ss = rot_c[rs, :], rot_s[rs, :]
                q = z_ref[rs, h * RET_QK_DIM:(h + 1) * RET_QK_DIM].astype(F32)
                k = z_ref[rs, QW + h * RET_QK_DIM:QW + (h + 1) * RET_QK_DIM].astype(F32)
                v = z_ref[rs, 2 * QW + h * RET_V_DIM:2 * QW + (h + 1) * RET_V_DIM]
                gt = z_ref[rs, 2 * QW + VW + h * RET_V_DIM:2 * QW + VW + (h + 1) * RET_V_DIM].astype(F32)
                dyv = dy_ref[rs, h * RET_V_DIM:(h + 1) * RET_V_DIM].astype(F32)
                dmat, qd, kd = dm_ref[h], qd_ref[h], kd_ref[h]
                qr = _rot(q, cc, ss)
                kr = _rot(k, cc, ss) * QK_SCALE
                qrb, krb = qr.astype(BF16), kr.astype(BF16)
                sb = st_ref[c, h]
                sd = (_dot(qrb, krb, NT) * dmat).astype(BF16)
                qdb = (qr * qd).astype(BF16)
                kdb = (kr * kd).astype(BF16)
                o = _dot(sd, v, NN) + _dot(qdb, sb, NN)
                mu = jnp.mean(o, axis=-1, keepdims=True)
                oc = o - mu
                rstd = lax.rsqrt(jnp.mean(oc * oc, axis=-1, keepdims=True) + EPS)
                rn = oc * rstd
                sg = 1.0 / (1.0 + jnp.exp(-gt))
                dgt = dyv * rn * (sg * (1.0 + gt * (1.0 - sg)))
                drn = dyv * (gt * sg)
                do = rstd * (drn - jnp.mean(drn, axis=-1, keepdims=True) - rn * jnp.mean(drn * rn, axis=-1, keepdims=True))
                dob = do.astype(BF16)
                dsn = dstate[h]
                dsnb = dsn.astype(BF16)
                ds_raw = (_dot(dob, v, NT) * dmat).astype(BF16)
                dv = _dot(sd, dob, TN) + _dot(kdb, dsnb, NN)
                dqr = _dot(ds_raw, krb, NN) + qd * _dot(dob, sb, NT)
                dkr = (_dot(ds_raw, qrb, TN) + kd * _dot(v, dsnb, NT)) * QK_SCALE
                dstate[h] = dsn * cd_ref[h] + _dot(qdb, dob, TN)
                dq = dqr * cc + pltpu.roll(dqr * ss, RET_QK_DIM // 2, 1)
                dk = dkr * cc + pltpu.roll(dkr * ss, RET_QK_DIM // 2, 1)
                dz_ref[rs, h * RET_QK_DIM:(h + 1) * RET_QK_DIM] = dq.astype(dz_ref.dtype)
                dz_ref[rs, QW + h * RET_QK_DIM:QW + (h + 1) * RET_QK_DIM] = dk.astype(dz_ref.dtype)
                dz_ref[rs, 2 * QW + h * RET_V_DIM:2 * QW + (h + 1) * RET_V_DIM] = dv.astype(dz_ref.dtype)
                dz_ref[rs, 2 * QW + VW + h * RET_V_DIM:2 * QW + VW + (h + 1) * RET_V_DIM] = dgt.astype(dz_ref.dtype)

    return pl.pallas_call(
        body,
        grid=(nb,),
        in_specs=[pl.BlockSpec((tr, RET_W), lambda n: (ix(n), 0)),
                  pl.BlockSpec((tr, VW), lambda n: (ix(n), 0)),
                  pl.BlockSpec((cpb, RET_HEADS, RET_QK_DIM, RET_V_DIM), lambda n: (ix(n), 0, 0, 0))] + tab_specs,
        out_specs=pl.BlockSpec((tr, RET_W), lambda n: (ix(n), 0)),
        out_shape=jax.ShapeDtypeStruct((T, RET_W), BF16),
        scratch_shapes=[pltpu.VMEM((RET_HEADS, RET_QK_DIM, RET_V_DIM), F32)],
        compiler_params=_cp("arbitrary"),
        name=name,
    )(z_a, dy, states, *tables)


ATT_TQ = 256
ATT_CPB = ATT_TQ // CHUNK
ATT_SCALE = ATT_HEAD_DIM ** -0.5


ATT_WIN = 3 * ATT_TQ
ATT_NB = CHUNK * ATT_BAND


def _rel_index():
    i = np.arange(CHUNK)[:, None]
    j = np.arange(ATT_BAND)[None, :]
    rel = np.clip(i + ATT_PAST * CHUNK - j, -MAX_REL, MAX_REL) + MAX_REL
    return jnp.asarray(rel.reshape(1, ATT_NB).astype(np.int32))


def _split3(x):
    hi = x.astype(BF16)
    r1 = x - hi.astype(F32)
    mid = r1.astype(BF16)
    lo = (r1 - mid.astype(F32)).astype(BF16)
    return hi, mid, lo


REL_TILE = 4608


def _bias_expand(rel_bias, name="bias_expand"):
    H = rel_bias.shape[0]
    n = ATT_NB
    padded = jnp.pad(rel_bias, ((0, 0), (0, N_REL_PAD - N_REL)))

    def body(rb_ref, idx_ref, o_ref):
        onehot = (lax.broadcasted_iota(jnp.int32, (N_REL_PAD, REL_TILE), 0) == idx_ref[...]).astype(BF16)
        hi, mid, lo = _split3(rb_ref[...])
        o_ref[...] = _dot(hi, onehot, NN) + _dot(mid, onehot, NN) + _dot(lo, onehot, NN)

    out = pl.pallas_call(
        body,
        grid=(n // REL_TILE,),
        in_specs=[pl.BlockSpec((H, N_REL_PAD), lambda t: (0, 0)), pl.BlockSpec((1, REL_TILE), lambda t: (0, t))],
        out_specs=pl.BlockSpec((H, REL_TILE), lambda t: (0, t)),
        out_shape=jax.ShapeDtypeStruct((H, n), F32),
        compiler_params=_cp("parallel"),
        name=name,
    )(padded, _rel_index())
    return out.reshape(H, CHUNK, ATT_BAND)


def _bias_tile(band, name="bias_tile"):
    H = band.shape[0]
    padded = jnp.pad(band, ((0, 0), (0, 0), (0, ATT_WIN - ATT_BAND)), constant_values=NEG_INF)

    def body(b_ref, o_ref):
        b = b_ref[...]
        col = lax.broadcasted_iota(jnp.int32, (CHUNK, ATT_WIN), 1)
        keep = col >= (2 - pl.program_id(0)) * ATT_TQ
        for a in range(ATT_CPB):
            o_ref[a * CHUNK:(a + 1) * CHUNK, :] = jnp.where(keep, pltpu.roll(b, a * CHUNK, 1) if a else b, NEG_INF)

    return pl.pallas_call(
        body,
        grid=(3, H),
        in_specs=[pl.BlockSpec((None, CHUNK, ATT_WIN), lambda v, h: (h, 0, 0))],
        out_specs=pl.BlockSpec((None, None, ATT_TQ, ATT_WIN), lambda v, h: (v, h, 0, 0)),
        out_shape=jax.ShapeDtypeStruct((3, H, ATT_TQ, ATT_WIN), F32),
        compiler_params=_cp("parallel", "parallel"),
        name=name,
    )(padded)


def _bias_untile(dtile, name="bias_untile"):
    H = dtile.shape[0]

    def body(d_ref, o_ref):
        acc = d_ref[0:CHUNK, :]
        for a in range(1, ATT_CPB):
            acc = acc + pltpu.roll(d_ref[a * CHUNK:(a + 1) * CHUNK, :], ATT_WIN - a * CHUNK, 1)
        o_ref[...] = acc

    out = pl.pallas_call(
        body,
        grid=(H,),
        in_specs=[pl.BlockSpec((None, ATT_TQ, ATT_WIN), lambda h: (h, 0, 0))],
        out_specs=pl.BlockSpec((None, CHUNK, ATT_WIN), lambda h: (h, 0, 0)),
        out_shape=jax.ShapeDtypeStruct((H, CHUNK, ATT_WIN), F32),
        compiler_params=_cp("parallel"),
        name=name,
    )(dtile)
    return out[:, :, :ATT_BAND]


def _bias_reduce(dbias, name="bias_reduce"):
    H = dbias.shape[0]
    n = ATT_NB

    def body(db_ref, idx_ref, o_ref):
        @pl.when(pl.program_id(0) == 0)
        def _():
            o_ref[...] = jnp.zeros_like(o_ref)

        onehot = (lax.broadcasted_iota(jnp.int32, (N_REL_PAD, REL_TILE), 0) == idx_ref[...]).astype(BF16)
        hi, mid, lo = _split3(db_ref[...])
        o_ref[...] += _dot(hi, onehot, NT) + _dot(mid, onehot, NT) + _dot(lo, onehot, NT)

    out = pl.pallas_call(
        body,
        grid=(n // REL_TILE,),
        in_specs=[pl.BlockSpec((H, REL_TILE), lambda t: (0, t)), pl.BlockSpec((1, REL_TILE), lambda t: (0, t))],
        out_specs=pl.BlockSpec((H, N_REL_PAD), lambda t: (0, 0)),
        out_shape=jax.ShapeDtypeStruct((H, N_REL_PAD), F32),
        compiler_params=_cp("arbitrary"),
        name=name,
    )(dbias.reshape(H, n), _rel_index())
    return out[:, :N_REL]


def _att_probs(q, kwin, bias):
    s = _dot(q, kwin, NT) + bias
    e = jnp.exp(s - jnp.max(s, axis=-1, keepdims=True))
    return e * (1.0 / jnp.sum(e, axis=-1, keepdims=True))


ATT_PAIR = 2 * ATT_HEAD_DIM
ATT_NP = ATT_HEADS // 2
ATT_QW = ATT_HEADS * ATT_HEAD_DIM
Y_COLS = RET_HEADS * RET_V_DIM + ATT_QW


def _att_specs(tq, nq, clip_q, q_col0):
    cb = ATT_QW // ATT_PAIR
    qi = (lambda p, m: (jnp.minimum(m, nq - 1), q_col0 + p)) if clip_q else (lambda p, m: (m, q_col0 + p))
    q = pl.BlockSpec((tq, ATT_PAIR), qi)

    def win(col0):
        return [pl.BlockSpec((tq, ATT_PAIR), functools.partial(lambda p, m, back: (jnp.clip(m - back, 0, nq - 1), col0 + p), back=b))
                for b in (2, 1, 0)]

    bias = pl.BlockSpec((None, 2, ATT_TQ, ATT_WIN), lambda p, m: (jnp.minimum(m, 2), p, 0, 0))
    return q, win(cb), win(2 * cb), bias


def _head_masks(rows):
    lane = lax.broadcasted_iota(jnp.int32, (rows, ATT_PAIR), 1)
    return lane < ATT_HEAD_DIM


def _att_fwd(z_b, bias, y, comm=None, name="att_fwd"):
    T = z_b.shape[0]
    tq = ATT_TQ
    nq = T // tq
    qs, kwin, vwin, bs = _att_specs(tq, nq, False, 0)
    nc = comm.n if comm else 0
    total = ATT_NP * nq

    def body(*refs):
        q_ref, k0, k1, k2, v0, v1, v2, b_ref = refs[:8]
        cin = refs[9:9 + nc]
        o_ref = refs[9 + nc]
        cout = refs[10 + nc:10 + 2 * nc]
        csem = refs[10 + 2 * nc:]
        m = pl.program_id(1)
        step = pl.program_id(0) * nq + m
        _host_comm(comm, "early", step, total, cin, cout, csem)
        kw = jnp.concatenate([k0[...], k1[...], k2[...]], axis=0)
        vw = jnp.concatenate([v0[...], v1[...], v2[...]], axis=0)
        q2 = q_ref[...] * ATT_SCALE
        even = _head_masks(tq)
        outs = []
        for hh in range(2):
            qm = jnp.where(even if hh == 0 else ~even, q2, jnp.zeros_like(q2))
            p = _att_probs(qm, kw, b_ref[hh])
            outs.append(_dot(p.astype(BF16), vw, NN))
        o_ref[...] = jnp.where(even, outs[0], outs[1]).astype(o_ref.dtype)
        _host_comm(comm, "late", step, total, cin, cout, csem)

    y_cb = (Y_COLS - ATT_QW) // ATT_PAIR
    out = pl.pallas_call(
        body,
        grid=(ATT_NP, nq),
        in_specs=[qs] + kwin + vwin + [bs, ANY] + [ANY] * nc,
        out_specs=[pl.BlockSpec((tq, ATT_PAIR), lambda p, m: (m, y_cb + p))] + [ANY] * nc,
        out_shape=[jax.ShapeDtypeStruct((T, Y_COLS), BF16)] + (comm.out_shape if comm else []),
        scratch_shapes=comm.scratch if comm else [],
        input_output_aliases={8: 0},
        compiler_params=_cp("arbitrary", "arbitrary"),
        name=name,
    )(z_b, z_b, z_b, z_b, z_b, z_b, z_b, bias, y, *(comm.arrays if comm else []))
    return out[0], list(out[1:])


def _att_bwd(z_b, bias, dy, comm=None, name="att_bwd"):
    T = z_b.shape[0]
    tq = ATT_TQ
    nq = T // tq
    y_cb = (Y_COLS - ATT_QW) // ATT_PAIR
    qs, kwin, vwin, bs = _att_specs(tq, nq, True, 0)
    dos = _att_specs(tq, nq, True, y_cb)[0]
    kv_out = pl.BlockSpec((tq, ATT_PAIR), lambda p, m: (jnp.maximum(m - 2, 0), p))
    W3 = 3 * tq
    nc = comm.n if comm else 0
    total = ATT_NP * (nq + 2)

    def body(*refs):
        q_ref, k0, k1, k2, v0, v1, v2, b_ref, do_ref = refs[:9]
        cin = refs[9:9 + nc]
        dq_ref, dk_ref, dv_ref, db_ref = refs[9 + nc:13 + nc]
        cout = refs[13 + nc:13 + 2 * nc]
        dkc, dvc, dkw, dvw = refs[13 + 2 * nc:17 + 2 * nc]
        csem = refs[17 + 2 * nc:]
        m = pl.program_id(1)
        step = pl.program_id(0) * (nq + 2) + m
        _host_comm(comm, "early", step, total, cin, cout, csem)

        @pl.when(m == 0)
        def _():
            dkc[...] = jnp.zeros_like(dkc)
            dvc[...] = jnp.zeros_like(dvc)
            db_ref[...] = jnp.zeros_like(db_ref)

        @pl.when(m >= nq)
        def _():
            dkw[...] = jnp.zeros_like(dkw)
            dvw[...] = jnp.zeros_like(dvw)

        @pl.when(m < nq)
        def _():
            kw = jnp.concatenate([k0[...], k1[...], k2[...]], axis=0)
            vw = jnp.concatenate([v0[...], v1[...], v2[...]], axis=0)
            q2, do2 = q_ref[...] * ATT_SCALE, do_ref[...]
            even = _head_masks(tq)
            dqs, dks, dvs = [], [], []
            for hh in range(2):
                mine = even if hh == 0 else ~even
                p = _att_probs(jnp.where(mine, q2, jnp.zeros_like(q2)), kw, b_ref[hh])
                dp = _dot(jnp.where(mine, do2, jnp.zeros_like(do2)), vw, NT)
                ds = p * (dp - jnp.sum(dp * p, axis=-1, keepdims=True))
                db_ref[hh] += ds
                dsb = ds.astype(BF16)
                dqs.append(_dot(dsb, kw, NN))
                dks.append(_dot(dsb, q2, TN))
                dvs.append(_dot(p.astype(BF16), do2, TN))
            even_w = _head_masks(W3)
            dq_ref[...] = (jnp.where(even, dqs[0], dqs[1]) * ATT_SCALE).astype(dq_ref.dtype)
            dkw[...] = jnp.where(even_w, dks[0], dks[1])
            dvw[...] = jnp.where(even_w, dvs[0], dvs[1])

        dk_ref[...] = (dkc[0:tq, :] + dkw[0:tq, :]).astype(dk_ref.dtype)
        dv_ref[...] = (dvc[0:tq, :] + dvw[0:tq, :]).astype(dv_ref.dtype)
        dkc[0:tq, :] = dkc[tq:2 * tq, :] + dkw[tq:2 * tq, :]
        dvc[0:tq, :] = dvc[tq:2 * tq, :] + dvw[tq:2 * tq, :]
        dkc[tq:2 * tq, :] = dkw[2 * tq:W3, :]
        dvc[tq:2 * tq, :] = dvw[2 * tq:W3, :]
        _host_comm(comm, "late", step, total, cin, cout, csem)

    qo = pl.BlockSpec((tq, ATT_PAIR), lambda p, m: (jnp.minimum(m, nq - 1), p))
    dbs = pl.BlockSpec((2, ATT_TQ, ATT_WIN), lambda p, m: (p, 0, 0))
    hd = jax.ShapeDtypeStruct((T, ATT_QW), BF16)
    out = pl.pallas_call(
        body,
        grid=(ATT_NP, nq + 2),
        in_specs=[qs] + kwin + vwin + [bs, dos] + [ANY] * nc,
        out_specs=[qo, kv_out, kv_out, dbs] + [ANY] * nc,
        out_shape=[hd, hd, hd, jax.ShapeDtypeStruct((ATT_HEADS, ATT_TQ, ATT_WIN), F32)] + (comm.out_shape if comm else []),
        scratch_shapes=[pltpu.VMEM((2 * tq, ATT_PAIR), F32), pltpu.VMEM((2 * tq, ATT_PAIR), F32),
                        pltpu.VMEM((W3, ATT_PAIR), F32), pltpu.VMEM((W3, ATT_PAIR), F32)] + (comm.scratch if comm else []),
        compiler_params=_cp("arbitrary", "arbitrary"),
        name=name,
    )(z_b, z_b, z_b, z_b, z_b, z_b, z_b, bias, dy, *(comm.arrays if comm else []))
    return out[0], out[1], out[2], out[3], list(out[4:])


REST = ["ab_out", "c_in_t", "c_out", "up_t0", "up_t1", "down0", "down1"]


def _local_step(x, target, wt, small, rest_shards=None, overlap=False, hn0=None):
    T = x.shape[0]
    Fh = FFN_HIDDEN
    tables = _ret_tables(T)
    gw, gs, recv = {}, {}, {}
    wt = dict(wt)

    if hn0 is None:
        hn0 = _rms_fwd(x, small["attn_norm_g"][0], name="rms_fwd")
    z_a = _mm(hn0, wt["ab_in_t"][:RET_W], "nt", BF16, name="mm_ab_in_a")
    z_b = _mm(hn0, wt["ab_in_t"][RET_W:], "nt", BF16, name="mm_ab_in_b")
    y, states = _ret_fwd(z_a, tables)
    bias = _bias_tile(_bias_expand(small["rel_bias"]))
    y, rest = _att_fwd(z_b, bias, y, comm=_Comm("gather", rest_shards) if rest_shards is not None else None)
    if rest_shards is not None:
        full = dict(zip(REST, rest))
        wt.update(ab_out=full["ab_out"], c_in_t=full["c_in_t"], c_out=full["c_out"],
                  up_t=[full["up_t0"], full["up_t1"]], down=[full["down0"], full["down1"]])
    h1, hf0 = _mm_rows(y, wt["ab_out"], x, "mm_ab_out", norm_g=small["ffn_norm_g"][0])

    def ffn_fwd(h, hf, layer, next_g):
        zg, zu, a = _ffn_up_mid(hf, wt["up_t"][layer], small["conv_w"][layer], small["conv_b"][layer][None, :])
        if next_g is None:
            return _mm_rows(a, wt["down"][layer], h, "mm_down_loss", loss=(target, small["final_g"])), (hf, zg, zu, a)
        h_out, hn_next = _mm_rows(a, wt["down"][layer], h, "mm_down", norm_g=next_g)
        return h_out, hn_next, (hf, zg, zu, a)

    def ffn_bwd(dh_out, h, layer, saved, exchange=None):
        hf, zg, zu, a = saved
        da = _mm(dh_out, wt["down"][layer], "nt", BF16, name="mm_d_a")
        d_down = _mm(a, dh_out, "tn", BF16, name="mm_dw_down")
        comm = _Comm("exchange", [gw[k] for k in exchange]) if exchange else None
        dzg, dzu, dcw, dcb, got = _ffn_mid_bwd(zg, zu, da, small["conv_w"][layer], small["conv_b"][layer][None, :], comm=comm)
        recv.update(zip(exchange or [], got))
        dh, dg = _mm_rows(dzg, wt["up_t"][layer][:Fh], None, "mm_d_hf_norm", second=(dzu, wt["up_t"][layer][Fh:]),
                          bwd=(h, small["ffn_norm_g"][layer], dh_out))
        d_up = _mm(dzg, hf, "tn", BF16, name="mm_dw_up", rows=(2 * Fh, 0, None))
        d_up = _mm(dzu, hf, "tn", BF16, name="mm_dw_up_2", rows=(2 * Fh, Fh, d_up))
        return dh, dg, d_up, d_down, dcw, dcb

    h2, hn1, ffn0 = ffn_fwd(h1, hf0, 0, small["attn_norm_g"][1])

    zz = _mm(hn1, wt["c_in_t"], "nt", BF16, name="mm_c_in")
    ys = _sgu_fwd(zz, small["ln_g"], small["ln_b"], small["w_s"], small["b_s"])
    h3, hf1 = _mm_rows(ys, wt["c_out"], h2, "mm_c_out", norm_g=small["ffn_norm_g"][1])
    (loss_vec, dh4, gs["final_g"]), ffn1 = ffn_fwd(h3, hf1, 1, None)

    dh3, dgf1, gw["up_t1"], gw["down1"], dcw1, dcb1 = ffn_bwd(dh4, h3, 1, ffn1)
    dys = _mm(dh3, wt["c_out"], "nt", BF16, name="mm_d_ys")
    gw["c_out"] = _mm(ys, dh3, "tn", BF16, name="mm_dw_c_out")
    dzz, gs["w_s"], dbs, dlg, dlb = _sgu_bwd(zz, dys, small["ln_g"], small["ln_b"], small["w_s"], small["b_s"])
    gs["b_s"], gs["ln_g"], gs["ln_b"] = dbs[:, :, 0], dlg[0], dlb[0]
    dh2, dga1 = _mm_rows(dzz, wt["c_in_t"], None, "mm_d_hn1_norm", bwd=(h2, small["attn_norm_g"][1], dh3))
    gw["c_in_t"] = _mm(dzz, hn1, "tn", BF16, name="mm_dw_c_in")

    dh1, dgf0, gw["up_t0"], gw["down0"], dcw0, dcb0 = ffn_bwd(
        dh2, h1, 0, ffn0, exchange=["c_in_t", "c_out", "up_t1", "down1"] if overlap else None)

    dy = _mm(dh1, wt["ab_out"], "nt", BF16, name="mm_d_y")
    gw["ab_out"] = _mm(y, dh1, "tn", BF16, name="mm_dw_ab_out")
    dz_a = _ret_bwd(z_a, dy, states, tables)
    late = ["ab_out", "up_t0", "down0"] if overlap else []
    dq, dk, dv, dbias, got = _att_bwd(z_b, bias, dy, comm=_Comm("exchange", [gw[k] for k in late]) if late else None)
    recv.update(zip(late, got))
    dz_b = jnp.concatenate([dq, dk, dv], axis=1)
    gs["rel_bias"] = _bias_reduce(_bias_untile(dbias))
    gw["ab_in_t"] = _mm(dz_a, hn0, "tn", BF16, name="mm_dw_ab_in_a", rows=(RET_W + ATT_W, 0, None))
    gw["ab_in_t"] = _mm(dz_b, hn0, "tn", BF16, name="mm_dw_ab_in_b", rows=(RET_W + ATT_W, RET_W, gw["ab_in_t"]))
    if overlap:
        dhn0, got = _mm(dz_a, wt["ab_in_t"][:RET_W], "nn", F32, name="mm_d_hn0", comm=_Comm("exchange", [gw["ab_in_t"]]))
        recv["ab_in_t"] = got[0]
    else:
        dhn0 = _mm(dz_a, wt["ab_in_t"][:RET_W], "nn", F32, name="mm_d_hn0")
    grad_x, dga0 = _mm_rows(dz_b, wt["ab_in_t"][RET_W:], dhn0, "mm_d_hn0_norm", bwd=(x, small["attn_norm_g"][0], dh1))

    gs["attn_norm_g"] = jnp.concatenate([dga0, dga1], axis=0)
    gs["ffn_norm_g"] = jnp.concatenate([dgf0, dgf1], axis=0)
    gs["conv_w"] = jnp.stack([dcw0, dcw1])
    gs["conv_b"] = jnp.concatenate([dcb0, dcb1], axis=0)
    gs["final_g"] = gs["final_g"][0]
    return loss_vec[0, 0], grad_x, gw, gs, recv


MESH_ID = pl.DeviceIdType.MESH
ANY = pl.BlockSpec(memory_space=pl.ANY)


def _my_place():
    return lax.axis_index("x"), lax.axis_index("y"), lax.axis_index("c")


class _Comm:
    def __init__(self, kind, arrays):
        self.kind, self.arrays, self.n = kind, list(arrays), len(arrays)
        if kind == "gather":
            self.out_shape = [jax.ShapeDtypeStruct((N_DEV * s.shape[0], s.shape[1]), s.dtype) for s in arrays]
        else:
            self.out_shape = [jax.ShapeDtypeStruct((N_DEV, g.shape[0] // N_DEV, g.shape[1]), g.dtype) for g in arrays]
        n = self.n
        self.scratch = [pltpu.SemaphoreType.DMA((n, 7)), pltpu.SemaphoreType.DMA((n, 7)), pltpu.SemaphoreType.DMA((n,))]

    def phase(self, ph, in_refs, out_refs, sems):
        (self._gather if self.kind == "gather" else self._exchange)(ph, in_refs, out_refs, sems)

    def _gather(self, ph, x_refs, o_refs, sems):
        n = self.n
        send_sems, recv_sems, local_sems = sems
        x, y, c = _my_place()
        me, sibling = (x, y, c), (x, y, 1 - c)
        chips = [(1 - x, y), (x, 1 - y), (1 - x, 1 - y)]

        def rows(a, place):
            m = x_refs[a].shape[0]
            px, py, pc = place
            return o_refs[a].at[pl.ds((4 * px + 2 * py + pc) * m, m), :]

        def copy(a, k, block, to, own=False):
            return pltpu.make_async_remote_copy(
                src_ref=x_refs[a] if own else rows(a, block), dst_ref=rows(a, block),
                send_sem=send_sems.at[a, k], recv_sem=recv_sems.at[a, k], device_id=to, device_id_type=MESH_ID)

        def mine():
            return [pltpu.make_async_copy(x_refs[a], rows(a, me), local_sems.at[a]) for a in range(n)]

        def first():
            out = []
            for a in range(n):
                out.append(copy(a, 0, me, sibling, own=True))
                out += [copy(a, 1 + j, me, (*chip, c), own=True) for j, chip in enumerate(chips)]
            return out

        def passed():
            return [copy(a, 4 + j, (*chip, c), sibling) for j, chip in enumerate(chips) for a in range(n)]

        if ph == 0:
            for cp in mine() + first():
                cp.start()
        elif ph == 1:
            fw = passed()
            for j, chip in enumerate(chips):
                for a in range(n):
                    copy(a, 1 + j, (*chip, c), me).wait_recv()
                    fw[j * n + a].start()
        else:
            for a in range(n):
                copy(a, 0, sibling, me).wait_recv()
                for j, chip in enumerate(chips):
                    copy(a, 4 + j, (*chip, 1 - c), me).wait_recv()
            for cp in first() + passed():
                cp.wait_send()
            for cp in mine():
                cp.wait()

    def _exchange(self, ph, g_refs, o_refs, sems):
        n = self.n
        send_sems, recv_sems, local_sems = sems
        x, y, c = _my_place()
        me = 4 * x + 2 * y + c
        peers = [(x ^ ((k >> 2) & 1), y ^ ((k >> 1) & 1), c ^ (k & 1)) for k in range(1, N_DEV)]

        def block(a, idx):
            m = g_refs[a].shape[0] // N_DEV
            return g_refs[a].at[pl.ds(idx * m, m), :]

        def copy(a, k, slot):
            px, py, pc = peers[k]
            return pltpu.make_async_remote_copy(
                src_ref=block(a, 4 * px + 2 * py + pc), dst_ref=o_refs[a].at[slot],
                send_sem=send_sems.at[a, k], recv_sem=recv_sems.at[a, k], device_id=peers[k], device_id_type=MESH_ID)

        if ph == 1:
            return
        mine = [pltpu.make_async_copy(block(a, me), o_refs[a].at[me], local_sems.at[a]) for a in range(n)]
        sends = [copy(a, k, me) for k in range(N_DEV - 1) for a in range(n)]
        if ph == 0:
            for cp in mine + sends:
                cp.start()
        else:
            for k in range(N_DEV - 1):
                px, py, pc = peers[k]
                for a in range(n):
                    copy(a, k, 4 * px + 2 * py + pc).wait_recv()
            for cp in sends:
                cp.wait_send()
            for cp in mine:
                cp.wait()


def _comm_call(comm, name):
    n = comm.n

    def body(*refs):
        for ph in range(3):
            comm.phase(ph, refs[:n], refs[n:2 * n], refs[2 * n:])

    return pl.pallas_call(
        body, out_shape=comm.out_shape, in_specs=[ANY] * n, out_specs=[ANY] * n, scratch_shapes=comm.scratch, name=name,
    )(*comm.arrays)


def _host_comm(comm, when, step, total, cin, cout, csem):
    if comm is None:
        return
    sched = {0: 0, 1: (3 * total) // 4, 2: total - 1}
    phases = (0, 1) if when == "early" else (2,)
    for ph in phases:
        if ph == 1 and comm.kind == "exchange":
            continue

        @pl.when(step == sched[ph])
        def _(ph=ph):
            comm.phase(ph, cin, cout, csem)


def _all_gather(shards, name="all_gather"):
    return _comm_call(_Comm("gather", shards), name)


def _row_tile(r, target=256):
    best = None
    for t in range(8, min(r, target) + 1, 8):
        if r % t == 0:
            best = t
    return best if best is not None else r


def _sum8(parts, name="sum8"):
    _, M, N = parts.shape
    tr = _row_tile(M, 128)

    def body(p_ref, o_ref):
        acc = p_ref[0].astype(F32)
        for d in range(1, N_DEV):
            acc = acc + p_ref[d].astype(F32)
        o_ref[...] = acc

    return pl.pallas_call(
        body,
        grid=(M // tr,),
        in_specs=[pl.BlockSpec((N_DEV, tr, N), lambda i: (0, i, 0))],
        out_specs=pl.BlockSpec((tr, N), lambda i: (i, 0)),
        out_shape=jax.ShapeDtypeStruct((M, N), F32),
        compiler_params=_cp("parallel"),
        name=name,
    )(parts)


def _adamw(w, g, m, v, name="adamw"):
    shape = w.shape
    if w.ndim == 1:
        r2 = (1, shape[0])
    else:
        r2 = (int(np.prod(shape[:-1])), shape[-1])
    R, C = r2
    tr = _row_tile(R)
    bc1 = 1.0 - ADAM_B1 ** ADAM_STEP
    bc2 = 1.0 - ADAM_B2 ** ADAM_STEP

    def body(w_ref, g_ref, m_ref, v_ref, d_ref, nm_ref, nv_ref):
        gv = g_ref[...]
        nm = ADAM_B1 * m_ref[...] + (1.0 - ADAM_B1) * gv
        nv = ADAM_B2 * v_ref[...] + (1.0 - ADAM_B2) * (gv * gv)
        d_ref[...] = -ADAM_LR * ((nm / bc1) / (jnp.sqrt(nv / bc2) + ADAM_EPS) + ADAM_WD * w_ref[...])
        nm_ref[...] = nm
        nv_ref[...] = nv

    spec = pl.BlockSpec((tr, C), lambda i: (i, 0))
    out = pl.pallas_call(
        body,
        grid=(R // tr,),
        in_specs=[spec] * 4,
        out_specs=[spec] * 3,
        out_shape=[jax.ShapeDtypeStruct(r2, F32)] * 3,
        compiler_params=_cp("parallel"),
        name=name,
    )(w.reshape(r2), g.reshape(r2), m.reshape(r2), v.reshape(r2))
    return [o.reshape(shape) for o in out]


WEIGHTS = ['attn_norm_g', 'ffn_norm_g', 'ab_w_in', 'ab_w_out', 'ab_rel_bias', 'c_w_in', 'c_ln_g', 'c_ln_b', 'c_w_s', 'c_b_s',
           'c_w_out', 'ffn_w_up', 'ffn_conv_w', 'ffn_conv_b', 'ffn_w_down', 'final_norm_g']
SMALL_ORDER = ["attn_norm_g", "ffn_norm_g", "rel_bias", "ln_g", "ln_b", "w_s", "b_s", "conv_w", "conv_b", "final_g"]
PACK_ROW = 1024


def _pack(arrs):
    flat = jnp.concatenate([a.reshape(-1) for a in arrs])
    n = flat.shape[0]
    padded = -(-n // PACK_ROW) * PACK_ROW
    return jnp.pad(flat, (0, padded - n)).reshape(padded // 128, 128)


def _unpack(flat, shapes):
    out, off = [], 0
    for s in shapes:
        n = int(np.prod(s))
        out.append(flat[off:off + n].reshape(s))
        off += n
    return out


def _step(P):
    x, target = P["x"][0], P["loss_target"][0]
    me = 4 * lax.axis_index("x") + 2 * lax.axis_index("y") + lax.axis_index("c")
    n_up = P["ffn_w_up"].shape[0]
    Fc = P["ffn_conv_w"].shape[-1]
    Lc = P["c_ln_g"].shape[-1]

    first = _Comm("gather", [P["ab_w_in"][0].T.astype(BF16), _pack([P["ffn_conv_w"], P["c_ln_g"], P["c_ln_b"]])])
    hn0, full = _rms_fwd(x, P["attn_norm_g"][0], comm=first, name="rms_fwd_gather")
    wt = {"ab_in_t": full[0]}
    rest = {"ab_out": P["ab_w_out"][0], "c_in_t": P["c_w_in"][0].T, "c_out": P["c_w_out"][0],
            "up_t0": P["ffn_w_up"][0].T, "up_t1": P["ffn_w_up"][1].T, "down0": P["ffn_w_down"][0], "down1": P["ffn_w_down"][1]}
    rest_shards = [rest[k].astype(BF16) for k in REST]
    sm = full[-1].reshape(N_DEV, -1)
    conv_w = sm[:, :n_up * 3 * Fc].reshape(N_DEV, n_up, 3, Fc).transpose(1, 2, 0, 3).reshape(n_up, 3, N_DEV * Fc)
    off = n_up * 3 * Fc
    ln_g = sm[:, off:off + Lc].reshape(N_DEV * Lc)
    ln_b = sm[:, off + Lc:off + 2 * Lc].reshape(N_DEV * Lc)
    small = {"attn_norm_g": P["attn_norm_g"], "ffn_norm_g": P["ffn_norm_g"], "rel_bias": P["ab_rel_bias"][0],
             "ln_g": ln_g, "ln_b": ln_b, "w_s": P["c_w_s"][0], "b_s": P["c_b_s"][0], "conv_w": conv_w,
             "conv_b": P["ffn_conv_b"], "final_g": P["final_norm_g"]}

    loss_part, grad_x, gw, gs, recv = _local_step(x, target, wt, small, rest_shards=rest_shards, overlap=True, hn0=hn0)
    loss = lax.psum(loss_part, ("x", "y", "c"))

    s8 ={k: _sum8(recv[k], name="sum8") for k in ["ab_in_t"] + REST}
    g_big = {"ab_w_in": s8["ab_in_t"].T[None], "ab_w_out": s8["ab_out"][None], "c_w_in": s8["c_in_t"].T[None],
             "c_w_out": s8["c_out"][None], "ffn_w_up": jnp.stack([s8["up_t0"].T, s8["up_t1"].T]),
             "ffn_w_down": jnp.stack([s8["down0"], s8["down1"]])}

    packed = _pack([gs[k] for k in SMALL_ORDER])
    gathered = _all_gather([packed], name="gather_small_grads")[0]
    tot = _sum8(gathered.reshape(N_DEV, packed.shape[0], 128), name="sum8_small").reshape(-1)
    gsm = dict(zip(SMALL_ORDER, _unpack(tot, [gs[k].shape for k in SMALL_ORDER])))
    grads = dict(g_big)
    grads["attn_norm_g"] = gsm["attn_norm_g"]
    grads["ffn_norm_g"] = gsm["ffn_norm_g"]
    grads["ab_rel_bias"] = gsm["rel_bias"][None]
    grads["c_ln_g"] = lax.dynamic_slice(gsm["ln_g"], (me * Lc,), (Lc,))[None]
    grads["c_ln_b"] = lax.dynamic_slice(gsm["ln_b"], (me * Lc,), (Lc,))[None]
    grads["c_w_s"] = gsm["w_s"][None]
    grads["c_b_s"] = gsm["b_s"][None]
    grads["ffn_conv_w"] = lax.dynamic_slice(gsm["conv_w"], (0, 0, me * Fc), (n_up, 3, Fc))
    grads["ffn_conv_b"] = gsm["conv_b"]
    grads["final_norm_g"] = gsm["final_g"]

    delta, new_m, new_v = {}, {}, {}
    for k in WEIGHTS:
        delta[k], new_m[k], new_v[k] = _adamw(P[k], grads[k], P["m_" + k], P["v_" + k], name="adamw")
    return (loss, grad_x[None], *[grads[k] for k in WEIGHTS], *[delta[k] for k in WEIGHTS],
            *[new_m[k] for k in WEIGHTS], *[new_v[k] for k in WEIGHTS])


def kernel(x, attn_norm_g, ffn_norm_g, ab_w_in, ab_w_out, ab_rel_bias, c_w_in, c_ln_g, c_ln_b, c_w_s, c_b_s, c_w_out, ffn_w_up, ffn_conv_w, ffn_conv_b, ffn_w_down, final_norm_g, loss_target, m_attn_norm_g, m_ffn_norm_g, m_ab_w_in, m_ab_w_out, m_ab_rel_bias, m_c_w_in, m_c_ln_g, m_c_ln_b, m_c_w_s, m_c_b_s, m_c_w_out, m_ffn_w_up, m_ffn_conv_w, m_ffn_conv_b, m_ffn_w_down, m_final_norm_g, v_attn_norm_g, v_ffn_norm_g, v_ab_w_in, v_ab_w_out, v_ab_rel_bias, v_c_w_in, v_c_ln_g, v_c_ln_b, v_c_w_s, v_c_b_s, v_c_w_out, v_ffn_w_up, v_ffn_conv_w, v_ffn_conv_b, v_ffn_w_down, v_final_norm_g):
    return _step(dict(locals()))
```

```python
import functools

import numpy as np
import jax
import jax.numpy as jnp
from jax import lax
from jax.experimental import pallas as pl
from jax.experimental.pallas import tpu as pltpu

F32 = jnp.float32
BF16 = jnp.bfloat16

D_MODEL = 1024
CHUNK = 64
EPS = 1e-6
NEG_INF = -1e30
RET_HEADS = 4
RET_QK_DIM = 128
RET_V_DIM = 256
ATT_HEADS = 8
ATT_HEAD_DIM = 64
ATT_PAST = 8
ATT_BAND = (ATT_PAST + 1) * CHUNK
MAX_REL = 128
N_REL = 2 * MAX_REL + 1
N_REL_PAD = 384
SGU_BLOCK = 128
SGU_GROUPS = 8
SGU_WIDTH = 2048
SGU_GW = SGU_WIDTH // SGU_GROUPS
FFN_HIDDEN = 2816
RET_W = 2 * RET_HEADS * RET_QK_DIM + 2 * RET_HEADS * RET_V_DIM
ATT_W = 3 * ATT_HEADS * ATT_HEAD_DIM
N_DEV = 8

ADAM_LR = 0.001
ADAM_B1 = 0.9
ADAM_B2 = 0.999
ADAM_EPS = 1e-08
ADAM_WD = 0.01
ADAM_STEP = 10

VMEM_LIMIT = 52 * 1024 * 1024


def _cp(*sem):
    return pltpu.CompilerParams(dimension_semantics=sem if sem else None, vmem_limit_bytes=VMEM_LIMIT)


def _tile(n, target):
    if n <= target:
        return n
    best = None
    for t in range(128, target + 1, 128):
        if n % t == 0:
            best = t
    assert best is not None, (n, target)
    return best


def _gelu(x):
    c = 0.7978845608028654
    return 0.5 * x * (1.0 + jnp.tanh(c * (x + 0.044715 * x * x * x)))


def _gelu_and_grad(x):
    c = 0.7978845608028654
    x2 = x * x
    t = jnp.tanh(c * (x + 0.044715 * x * x2))
    cdf = 0.5 * (1.0 + t)
    grad = cdf + x * (0.5 * c) * (1.0 - t * t) * (1.0 + 3.0 * 0.044715 * x2)
    return x * cdf, grad


def _dot(a, b, dims):
    return lax.dot_general(a, b, (dims, ((), ())), preferred_element_type=F32)


NN = ((1,), (0,))
NT = ((1,), (1,))
TN = ((0,), (0,))


def _mm(a, b, mode, out_dtype, res=None, name="mm", tm_t=None, tn_t=None, tk_t=None, comm=None, rows=None):
    if mode == "nt":
        (M, K), N = a.shape, b.shape[0]
        dm, dn, dk = (1024, 2816, K) if N <= 2816 else (1024, 2048, K)
    elif mode == "nn":
        (M, K), N = a.shape, b.shape[1]
        dm, dn, dk = (1024 if K <= 3072 else 512), 1024, K
    else:
        (K, M), N = a.shape, b.shape[1]
        dm, dn, dk = 1536, 1024, 2048
    tm, tn, tk = _tile(M, tm_t or dm), _tile(N, tn_t or dn), _tile(K, tk_t or dk)
    nk = K // tk
    dims = {"nt": NT, "nn": NN, "tn": TN}[mode]
    a_spec = pl.BlockSpec((tk, tm), lambda i, j, k: (k, i)) if mode == "tn" else pl.BlockSpec((tm, tk), lambda i, j, k: (i, k))
    b_spec = pl.BlockSpec((tn, tk), lambda i, j, k: (j, k)) if mode == "nt" else pl.BlockSpec((tk, tn), lambda i, j, k: (k, j))
    gi, gj = M // tm, N // tn
    out_rows, row0, into = rows if rows else (M, 0, None)
    assert row0 % tm == 0 and not (comm and into is not None)
    o_spec = pl.BlockSpec((tm, tn), lambda i, j, k: (i + row0 // tm, j))
    has_res = res is not None
    nc = 1 if into is not None else (comm.n if comm else 0)
    n_in = 3 if has_res else 2

    def body(*refs):
        a_ref, b_ref = refs[:2]
        r_ref = refs[2] if has_res else None
        nco = comm.n if comm else 0
        cin = refs[n_in:n_in + nc]
        o_ref = refs[n_in + nc]
        cout = refs[n_in + nc + 1:n_in + nc + 1 + nco]
        scratch = refs[n_in + nc + 1 + nco:]
        csem = scratch[1:] if nk > 1 else scratch
        step = (pl.program_id(0) * gj + pl.program_id(1)) * nk + pl.program_id(2)
        _host_comm(comm, "early", step, gi * gj * nk, cin, cout, csem)
        p = _dot(a_ref[...].astype(BF16), b_ref[...].astype(BF16), dims)
        if nk == 1:
            if has_res:
                p = p + r_ref[...]
            o_ref[...] = p.astype(out_dtype)
        else:
            acc = scratch[0]
            k = pl.program_id(2)

            @pl.when(k == 0)
            def _():
                acc[...] = p

            @pl.when(k > 0)
            def _():
                acc[...] += p

            @pl.when(k == nk - 1)
            def _():
                t = acc[...]
                if has_res:
                    t = t + r_ref[...]
                o_ref[...] = t.astype(out_dtype)
        _host_comm(comm, "late", step, gi * gj * nk, cin, cout, csem)

    in_specs = [a_spec, b_spec] + ([o_spec] if has_res else []) + [ANY] * nc
    args = (a, b) + ((res,) if has_res else ()) + ((into,) if into is not None else tuple(comm.arrays if comm else ()))
    out = pl.pallas_call(
        body,
        grid=(gi, gj, nk),
        in_specs=in_specs,
        out_specs=[o_spec] + ([ANY] * nc if comm else []),
        out_shape=[jax.ShapeDtypeStruct((out_rows, N), out_dtype)] + (comm.out_shape if comm else []),
        scratch_shapes=([pltpu.VMEM((tm, tn), F32)] if nk > 1 else []) + (comm.scratch if comm else []),
        input_output_aliases={n_in: 0} if into is not None else {},
        compiler_params=_cp("arbitrary", "arbitrary", "arbitrary") if comm else _cp("parallel", "parallel", "arbitrary"),
        name=name,
    )(*args)
    return (out[0], list(out[1:])) if comm else out[0]


def _mm_rows(a, b, res, name, norm_g=None, bwd=None, loss=None, second=None, tm=512):
    M, K = a.shape
    Dm = b.shape[1]
    if norm_g is not None and K <= 2048:
        tm = 2 * tm
    tm = min(tm, M)
    row = pl.BlockSpec((tm, Dm), lambda i: (i, 0))
    vec = pl.BlockSpec((1, Dm), lambda i: (0, 0))
    a_spec = pl.BlockSpec((tm, K), lambda i: (i, 0))
    b_spec = pl.BlockSpec((K, Dm), lambda i: (0, 0))

    if loss is not None:
        target, g = loss

        def body(a_ref, b_ref, r_ref, g_ref, t_ref, loss_ref, dh_ref, dg_ref):
            @pl.when(pl.program_id(0) == 0)
            def _():
                loss_ref[...] = jnp.zeros_like(loss_ref)
                dg_ref[...] = jnp.zeros_like(dg_ref)

            x = _dot(a_ref[...].astype(BF16), b_ref[...].astype(BF16), NN) + r_ref[...]
            gv = g_ref[...]
            r = lax.rsqrt(jnp.mean(x * x, axis=-1, keepdims=True) + EPS)
            xhat = x * r
            e = xhat * gv - t_ref[...]
            loss_ref[...] += jnp.full((1, 128), 0.5 / Dm, F32) * jnp.sum(e * e)
            dy = e * (1.0 / Dm)
            dg_ref[...] += jnp.sum(dy * xhat, axis=0, keepdims=True)
            dx = dy * gv
            m = jnp.mean(dx * xhat, axis=-1, keepdims=True)
            dh_ref[...] = r * (dx - xhat * m)

        return pl.pallas_call(
            body, grid=(M // tm,), in_specs=[a_spec, b_spec, row, vec, row],
            out_specs=[pl.BlockSpec((1, 128), lambda i: (0, 0)), row, vec],
            out_shape=[jax.ShapeDtypeStruct((1, 128), F32), jax.ShapeDtypeStruct((M, Dm), F32), jax.ShapeDtypeStruct((1, Dm), F32)],
            compiler_params=_cp("arbitrary"), name=name,
        )(a, b, res, g.reshape(1, Dm), target)

    if bwd is None:
        def body(a_ref, b_ref, r_ref, g_ref, o_ref, n_ref):
            t = _dot(a_ref[...].astype(BF16), b_ref[...].astype(BF16), NN) + r_ref[...]
            o_ref[...] = t
            r = lax.rsqrt(jnp.mean(t * t, axis=-1, keepdims=True) + EPS)
            n_ref[...] = (t * r * g_ref[...]).astype(n_ref.dtype)

        return pl.pallas_call(
            body, grid=(M // tm,), in_specs=[a_spec, b_spec, row, vec], out_specs=[row, row],
            out_shape=[jax.ShapeDtypeStruct((M, Dm), F32), jax.ShapeDtypeStruct((M, Dm), BF16)],
            compiler_params=_cp("parallel"), name=name,
        )(a, b, res, norm_g.reshape(1, Dm))

    h, g, dres = bwd
    has_res = res is not None
    has2 = second is not None

    def body(*refs):
        a_ref, b_ref = refs[:2]
        h_ref, g_ref, dres_ref, dh_ref, dg_ref = refs[-5:]

        @pl.when(pl.program_id(0) == 0)
        def _():
            dg_ref[...] = jnp.zeros_like(dg_ref)

        d = _dot(a_ref[...].astype(BF16), b_ref[...].astype(BF16), NN)
        if has2:
            d = d + _dot(refs[2][...].astype(BF16), refs[3][...].astype(BF16), NN)
        if has_res:
            d = d + refs[4 if has2 else 2][...]
        x = h_ref[...]
        r = lax.rsqrt(jnp.mean(x * x, axis=-1, keepdims=True) + EPS)
        xhat = x * r
        dg_ref[...] += jnp.sum(d * xhat, axis=0, keepdims=True)
        dx = d * g_ref[...]
        m = jnp.mean(dx * xhat, axis=-1, keepdims=True)
        dh_ref[...] = dres_ref[...] + r * (dx - xhat * m)

    second_specs = [pl.BlockSpec((tm, second[0].shape[1]), lambda i: (i, 0)),
                    pl.BlockSpec(second[1].shape, lambda i: (0, 0))] if has2 else []
    return pl.pallas_call(
        body, grid=(M // tm,), in_specs=[a_spec, b_spec] + second_specs + ([row] if has_res else []) + [row, vec, row],
        out_specs=[row, vec],
        out_shape=[jax.ShapeDtypeStruct((M, Dm), F32), jax.ShapeDtypeStruct((1, Dm), F32)],
        compiler_params=_cp("arbitrary"), name=name,
    )(a, b, *(second if has2 else ()), *((res,) if has_res else ()), h, g.reshape(1, Dm), dres)


def _rms_fwd(h, g, comm=None, name="rms_fwd", tm=512):
    T, Dm = h.shape
    tm = min(tm, T)
    nc = comm.n if comm else 0
    ni = T // tm

    def body(*refs):
        h_ref, g_ref = refs[:2]
        cin, o_ref, cout, csem = refs[2:2 + nc], refs[2 + nc], refs[3 + nc:3 + 2 * nc], refs[3 + 2 * nc:]
        step = pl.program_id(0)
        _host_comm(comm, "early", step, ni, cin, cout, csem)
        x = h_ref[...]
        r = lax.rsqrt(jnp.mean(x * x, axis=-1, keepdims=True) + EPS)
        o_ref[...] = (x * r * g_ref[...]).astype(o_ref.dtype)
        _host_comm(comm, "late", step, ni, cin, cout, csem)

    out = pl.pallas_call(
        body,
        grid=(ni,),
        in_specs=[pl.BlockSpec((tm, Dm), lambda i: (i, 0)), pl.BlockSpec((1, Dm), lambda i: (0, 0))] + [ANY] * nc,
        out_specs=[pl.BlockSpec((tm, Dm), lambda i: (i, 0))] + [ANY] * nc,
        out_shape=[jax.ShapeDtypeStruct((T, Dm), BF16)] + (comm.out_shape if comm else []),
        scratch_shapes=comm.scratch if comm else [],
        compiler_params=_cp("arbitrary"),
        name=name,
    )(h, g.reshape(1, Dm), *(comm.arrays if comm else []))
    return (out[0], list(out[1:])) if comm else out[0]


HALO = 16


def _conv3(ext, w_ref, b_ref):
    return w_ref[0:1, :] * pltpu.roll(ext, 2, 0) + w_ref[1:2, :] * pltpu.roll(ext, 1, 0) + w_ref[2:3, :] * ext + b_ref[...]


def _ffn_up_mid(hf, w_up_t, cw, cb, name="ffn_up_mid", tm=512, tc=1408):
    T, Dm = hf.shape
    Fh = w_up_t.shape[0] // 2
    tm = min(tm, T)
    nj, ni = Fh // tc, T // tm

    rc = min(512, tm)

    def body(h_ref, wg_ref, wu_ref, cwg_ref, cwu_ref, cbg_ref, cbu_ref, zg_ref, zu_ref, a_ref, eg_ref, eu_ref, ng_ref, nu_ref):
        @pl.when(pl.program_id(1) == 0)
        def _():
            eg_ref[...] = jnp.zeros_like(eg_ref)
            eu_ref[...] = jnp.zeros_like(eu_ref)

        for r in range(tm // rc):
            ext = slice(r * rc, r * rc + HALO + rc)
            cg = _conv3(eg_ref[ext, :].astype(F32), cwg_ref, cbg_ref)[HALO:]
            cu = _conv3(eu_ref[ext, :].astype(F32), cwu_ref, cbu_ref)[HALO:]
            a_ref[r * rc:(r + 1) * rc, :] = _gelu(cg.astype(BF16)) * cu.astype(BF16)
            h = h_ref[r * rc:(r + 1) * rc, :]
            for w_ref, z_ref, n_ref in ((wg_ref, zg_ref, ng_ref), (wu_ref, zu_ref, nu_ref)):
                z = _dot(h, w_ref[...], NT).astype(BF16)
                z_ref[r * rc:(r + 1) * rc, :] = z
                n_ref[r * rc:(r + 1) * rc, :] = z
        for e_ref, n_ref in ((eg_ref, ng_ref), (eu_ref, nu_ref)):
            e_ref[0:HALO, :] = e_ref[tm:tm + HALO, :]
            e_ref[HALO:HALO + tm, :] = n_ref[...]

    last = ni - 1
    z_spec = pl.BlockSpec((tm, tc), lambda j, i: (jnp.minimum(i, last), j))
    return pl.pallas_call(
        body,
        grid=(nj, ni + 1),
        in_specs=[pl.BlockSpec((tm, Dm), lambda j, i: (jnp.minimum(i, last), 0)),
                  pl.BlockSpec((tc, Dm), lambda j, i: (j, 0)), pl.BlockSpec((tc, Dm), lambda j, i: (nj + j, 0)),
                  pl.BlockSpec((3, tc), lambda j, i: (0, j)), pl.BlockSpec((3, tc), lambda j, i: (0, nj + j)),
                  pl.BlockSpec((1, tc), lambda j, i: (0, j)), pl.BlockSpec((1, tc), lambda j, i: (0, nj + j))],
        out_specs=[z_spec, z_spec, pl.BlockSpec((tm, tc), lambda j, i: (jnp.maximum(i - 1, 0), j))],
        out_shape=[jax.ShapeDtypeStruct((T, Fh), BF16)] * 3,
        scratch_shapes=[pltpu.VMEM((HALO + tm, tc), BF16), pltpu.VMEM((HALO + tm, tc), BF16),
                        pltpu.VMEM((tm, tc), BF16), pltpu.VMEM((tm, tc), BF16)],
        compiler_params=_cp("arbitrary", "arbitrary"),
        name=name,
    )(hf, w_up_t, w_up_t, cw, cw, cb, cb)


def _ffn_mid_bwd(zg, zu, da, cw, cb, comm=None, name="ffn_mid_bwd", tm=2048, tc=256):
    T, Fh = zg.shape
    tm = min(tm, T)
    nj = Fh // tc
    hb = tm // HALO
    nhb = T // HALO

    nc = comm.n if comm else 0
    ni = T // tm

    def body(*refs):
        zg_ref, zu_ref, zgp_ref, zup_ref, zgn_ref, zun_ref, da_ref, dan_ref, wg_ref, wu_ref, bg_ref, bu_ref = refs[:12]
        cin = refs[12:12 + nc]
        dzg_ref, dzu_ref, dwg_ref, dwu_ref, dbg_ref, dbu_ref = refs[12 + nc:18 + nc]
        cout = refs[18 + nc:18 + 2 * nc]
        csem = refs[18 + 2 * nc:]
        i = pl.program_id(1)
        step = pl.program_id(0) * ni + i
        _host_comm(comm, "early", step, nj * ni, cin, cout, csem)
        first = i == 0
        last = i == ni - 1

        @pl.when(first)
        def _():
            dwg_ref[...] = jnp.zeros_like(dwg_ref)
            dwu_ref[...] = jnp.zeros_like(dwu_ref)
            dbg_ref[...] = jnp.zeros_like(dbg_ref)
            dbu_ref[...] = jnp.zeros_like(dbu_ref)

        def ext_of(p_ref, c_ref, n_ref):
            p = jnp.where(first, 0.0, p_ref[...].astype(F32))
            return jnp.concatenate([p, c_ref[...].astype(F32), n_ref[...].astype(F32)], axis=0)

        zge = ext_of(zgp_ref, zg_ref, zgn_ref)
        zue = ext_of(zup_ref, zu_ref, zun_ref)
        dan = jnp.where(last, 0.0, dan_ref[...].astype(F32))
        dae = jnp.concatenate([jnp.zeros((HALO, tc), F32), da_ref[...].astype(F32), dan], axis=0)
        zg1, zg2 = pltpu.roll(zge, 1, 0), pltpu.roll(zge, 2, 0)
        zu1, zu2 = pltpu.roll(zue, 1, 0), pltpu.roll(zue, 2, 0)
        cg = wg_ref[0:1, :] * zg2 + wg_ref[1:2, :] * zg1 + wg_ref[2:3, :] * zge + bg_ref[...]
        cu = wu_ref[0:1, :] * zu2 + wu_ref[1:2, :] * zu1 + wu_ref[2:3, :] * zue + bu_ref[...]
        gel, dgel = (t.astype(F32) for t in _gelu_and_grad(cg.astype(BF16)))
        dcg = dae * cu * dgel
        dcu = dae * gel
        lo, hi = HALO, HALO + tm

        def back(dc, taps, w_ref, dz_ref, dw_ref, db_ref):
            n = dc.shape[0]
            dz = w_ref[2:3, :] * dc + w_ref[1:2, :] * pltpu.roll(dc, n - 1, 0) + w_ref[0:1, :] * pltpu.roll(dc, n - 2, 0)
            dz_ref[...] = dz[lo:hi].astype(dz_ref.dtype)
            dcc = dc[lo:hi]
            db_ref[...] += jnp.sum(dcc, axis=0, keepdims=True)
            for k, tap in enumerate(taps):
                dw_ref[k:k + 1, :] += jnp.sum(dcc * tap[lo:hi], axis=0, keepdims=True)

        back(dcg, (zg2, zg1, zge), wg_ref, dzg_ref, dwg_ref, dbg_ref)
        back(dcu, (zu2, zu1, zue), wu_ref, dzu_ref, dwu_ref, dbu_ref)
        _host_comm(comm, "late", step, nj * ni, cin, cout, csem)

    cur = pl.BlockSpec((tm, tc), lambda j, i: (i, j))
    prev = pl.BlockSpec((HALO, tc), lambda j, i: (jnp.maximum(i * hb - 1, 0), j))
    nxt = pl.BlockSpec((HALO, tc), lambda j, i: (jnp.minimum((i + 1) * hb, nhb - 1), j))
    wg = pl.BlockSpec((3, tc), lambda j, i: (0, j))
    wu = pl.BlockSpec((3, tc), lambda j, i: (0, j + nj))
    bg = pl.BlockSpec((1, tc), lambda j, i: (0, j))
    bu = pl.BlockSpec((1, tc), lambda j, i: (0, j + nj))
    dw = pl.BlockSpec((3, tc), lambda j, i: (0, j))
    db = pl.BlockSpec((1, tc), lambda j, i: (0, j))
    out = pl.pallas_call(
        body,
        grid=(nj, ni),
        in_specs=[cur, cur, prev, prev, nxt, nxt, cur, nxt, wg, wu, bg, bu] + [ANY] * nc,
        out_specs=[cur, cur, dw, dw, db, db] + [ANY] * nc,
        out_shape=[jax.ShapeDtypeStruct((T, Fh), BF16), jax.ShapeDtypeStruct((T, Fh), BF16),
                   jax.ShapeDtypeStruct((3, Fh), F32), jax.ShapeDtypeStruct((3, Fh), F32),
                   jax.ShapeDtypeStruct((1, Fh), F32), jax.ShapeDtypeStruct((1, Fh), F32)] + (comm.out_shape if comm else []),
        scratch_shapes=comm.scratch if comm else [],
        compiler_params=_cp("arbitrary", "arbitrary"),
        name=name,
    )(zg, zu, zg, zu, zg, zu, da, da, cw, cw, cb, cb, *(comm.arrays if comm else []))
    dzg, dzu, dwg, dwu, dbg, dbu = out[:6]
    return dzg, dzu, jnp.concatenate([dwg, dwu], axis=1), jnp.concatenate([dbg, dbu], axis=1), list(out[6:])


def _sgu_mask():
    r = lax.broadcasted_iota(jnp.int32, (SGU_BLOCK, SGU_BLOCK), 0)
    c = lax.broadcasted_iota(jnp.int32, (SGU_BLOCK, SGU_BLOCK), 1)
    return (c < CHUNK) | (r >= CHUNK)


def _sgu_fwd(zz, ln_g, ln_b, w_s, b_s, name="sgu_fwd", tm=256):
    T = zz.shape[0]
    tm = min(tm, T)
    W = SGU_WIDTH

    def body(zu_ref, zv_ref, g_ref, b_ref, ws_ref, bs_ref, y_ref):
        v = _gelu(zv_ref[...]).astype(F32)
        mu = jnp.mean(v, axis=-1, keepdims=True)
        xc = v - mu
        rstd = lax.rsqrt(jnp.mean(xc * xc, axis=-1, keepdims=True) + EPS)
        vn = (xc * rstd * g_ref[...] + b_ref[...]).astype(BF16)
        mask = _sgu_mask()
        for g in range(SGU_GROUPS):
            wm = jnp.where(mask, ws_ref[g], 0.0).astype(BF16)
            cs = slice(g * SGU_GW, (g + 1) * SGU_GW)
            for blk in range(tm // SGU_BLOCK):
                rs = slice(blk * SGU_BLOCK, (blk + 1) * SGU_BLOCK)
                mixed = _dot(wm, vn[rs, cs], NN) + bs_ref[g]
                y_ref[rs, cs] = (_gelu(zu_ref[rs, cs]).astype(F32) * mixed).astype(y_ref.dtype)

    return pl.pallas_call(
        body,
        grid=(T // tm,),
        in_specs=[pl.BlockSpec((tm, W), lambda i: (i, 0)), pl.BlockSpec((tm, W), lambda i: (i, 1)),
                  pl.BlockSpec((1, W), lambda i: (0, 0)), pl.BlockSpec((1, W), lambda i: (0, 0)),
                  pl.BlockSpec((SGU_GROUPS, SGU_BLOCK, SGU_BLOCK), lambda i: (0, 0, 0)),
                  pl.BlockSpec((SGU_GROUPS, SGU_BLOCK, 1), lambda i: (0, 0, 0))],
        out_specs=pl.BlockSpec((tm, W), lambda i: (i, 0)),
        out_shape=jax.ShapeDtypeStruct((T, W), BF16),
        compiler_params=_cp("parallel"),
        name=name,
    )(zz, zz, ln_g.reshape(1, W), ln_b.reshape(1, W), w_s, b_s.reshape(SGU_GROUPS, SGU_BLOCK, 1))


def _sgu_bwd(zz, dy, ln_g, ln_b, w_s, b_s, name="sgu_bwd", tm=256):
    T = zz.shape[0]
    tm = min(tm, T)
    W = SGU_WIDTH

    def body(zu_ref, zv_ref, dy_ref, g_ref, b_ref, ws_ref, bs_ref, dzz_ref, dws_ref, dbs_ref, dg_ref, db_ref, dvn_ref):
        i = pl.program_id(0)

        @pl.when(i == 0)
        def _():
            dws_ref[...] = jnp.zeros_like(dws_ref)
            dbs_ref[...] = jnp.zeros_like(dbs_ref)
            dg_ref[...] = jnp.zeros_like(dg_ref)
            db_ref[...] = jnp.zeros_like(db_ref)

        v, dv_dz = (t.astype(F32) for t in _gelu_and_grad(zv_ref[...]))
        mu = jnp.mean(v, axis=-1, keepdims=True)
        xc = v - mu
        rstd = lax.rsqrt(jnp.mean(xc * xc, axis=-1, keepdims=True) + EPS)
        xhat = xc * rstd
        gv = g_ref[...]
        vn = (xhat * gv + b_ref[...]).astype(BF16)
        mask = _sgu_mask()
        for g in range(SGU_GROUPS):
            wm = jnp.where(mask, ws_ref[g], 0.0).astype(BF16)
            cs = slice(g * SGU_GW, (g + 1) * SGU_GW)
            dw_acc = jnp.zeros((SGU_BLOCK, SGU_BLOCK), F32)
            db_acc = jnp.zeros((SGU_BLOCK, 1), F32)
            for blk in range(tm // SGU_BLOCK):
                rs = slice(blk * SGU_BLOCK, (blk + 1) * SGU_BLOCK)
                vn_bg = vn[rs, cs]
                mixed = _dot(wm, vn_bg, NN) + bs_ref[g]
                u_bg, du_bg = (t.astype(F32) for t in _gelu_and_grad(zu_ref[rs, cs]))
                dy_bg = dy_ref[rs, cs].astype(F32)
                dmixed = dy_bg * u_bg
                dmb = dmixed.astype(BF16)
                dw_acc += _dot(dmb, vn_bg, NT)
                db_acc += jnp.sum(dmixed, axis=1, keepdims=True)
                dvn_ref[rs, cs] = _dot(wm, dmb, TN)
                dzz_ref[rs, cs] = (dy_bg * mixed * du_bg).astype(dzz_ref.dtype)
            dws_ref[g] += jnp.where(mask, dw_acc, 0.0)
            dbs_ref[g] += db_acc
        dvn = dvn_ref[...]
        dg_ref[...] += jnp.sum(dvn * xhat, axis=0, keepdims=True)
        db_ref[...] += jnp.sum(dvn, axis=0, keepdims=True)
        dxh = dvn * gv
        m1 = jnp.mean(dxh, axis=-1, keepdims=True)
        m2 = jnp.mean(dxh * xhat, axis=-1, keepdims=True)
        dv = rstd * (dxh - m1 - xhat * m2)
        dzz_ref[:, W:] = (dv * dv_dz).astype(dzz_ref.dtype)

    vec = pl.BlockSpec((1, W), lambda i: (0, 0))
    ws_spec = pl.BlockSpec((SGU_GROUPS, SGU_BLOCK, SGU_BLOCK), lambda i: (0, 0, 0))
    bs_spec = pl.BlockSpec((SGU_GROUPS, SGU_BLOCK, 1), lambda i: (0, 0, 0))
    return pl.pallas_call(
        body,
        grid=(T // tm,),
        in_specs=[pl.BlockSpec((tm, W), lambda i: (i, 0)), pl.BlockSpec((tm, W), lambda i: (i, 1)),
                  pl.BlockSpec((tm, W), lambda i: (i, 0)), vec, vec, ws_spec, bs_spec],
        out_specs=[pl.BlockSpec((tm, 2 * W), lambda i: (i, 0)), ws_spec, bs_spec, vec, vec],
        out_shape=[jax.ShapeDtypeStruct((T, 2 * W), BF16),
                   jax.ShapeDtypeStruct((SGU_GROUPS, SGU_BLOCK, SGU_BLOCK), F32),
                   jax.ShapeDtypeStruct((SGU_GROUPS, SGU_BLOCK, 1), F32),
                   jax.ShapeDtypeStruct((1, W), F32), jax.ShapeDtypeStruct((1, W), F32)],
        scratch_shapes=[pltpu.VMEM((tm, W), F32)],
        compiler_params=_cp("arbitrary"),
        name=name,
    )(zz, zz, dy, ln_g.reshape(1, W), ln_b.reshape(1, W), w_s, b_s.reshape(SGU_GROUPS, SGU_BLOCK, 1))


RET_TR = 256
RET_BLK = 256
QK_SCALE = RET_QK_DIM ** -0.5


def _ret_tables(T):
    half = RET_QK_DIM // 2
    inv = 1.0 / (10000.0 ** jnp.linspace(0.0, 1.0, half, dtype=F32))
    inv2 = jnp.concatenate([inv, inv])[None, :]
    sgn = jnp.concatenate([-jnp.ones((half,), F32), jnp.ones((half,), F32)])[None, :]
    tr = min(RET_TR, T)

    def trig(pos):
        ang = pos.astype(F32)[:, None] * inv2
        return jnp.stack([jnp.cos(ang), jnp.sin(ang), sgn * jnp.sin(ang)])

    tile_tab = jnp.pad(trig(jnp.arange(T // tr) * tr).transpose(1, 0, 2), ((0, 0), (0, 5), (0, 0)))
    row_tab = trig(jnp.arange(tr))
    log_g = jnp.log1p(-jnp.exp2(-5.0 - jnp.arange(RET_HEADS, dtype=F32)))
    idx = jnp.arange(RET_BLK, dtype=F32)
    dist = idx[:, None] - idx[None, :]
    cq, ck = jnp.arange(RET_BLK)[:, None] // CHUNK, jnp.arange(RET_BLK)[None, :] // CHUNK
    expo = jnp.where(ck == cq, jnp.abs(dist), dist)
    d_blk = jnp.where((ck <= cq)[None], jnp.exp(log_g[:, None, None] * expo[None]), 0.0)
    k_dec = jnp.exp(log_g[:, None] * (RET_BLK - 1 - idx)[None, :])[:, :, None]
    q_dec = jnp.exp(log_g[:, None] * (idx + 1.0)[None, :])[:, :, None]
    c_dec = jnp.exp(log_g * RET_BLK)[:, None, None]
    return tile_tab, row_tab, d_blk, q_dec, k_dec, c_dec


def _rot(x, c, s):
    return x * c + pltpu.roll(x, RET_QK_DIM // 2, 1) * s


def _rot_tables(tt_ref, rt_ref):
    ca, sa, ga = tt_ref[0:1, :], tt_ref[1:2, :], tt_ref[2:3, :]
    cb, sb, gb = rt_ref[0], rt_ref[1], rt_ref[2]
    return ca * cb - sa * sb, ga * cb + ca * gb


def _ret_specs(tr, rev, nb):
    ix = (lambda n: nb - 1 - n) if rev else (lambda n: n)
    tt = pl.BlockSpec((None, 8, RET_QK_DIM), lambda n: (ix(n), 0, 0))
    rt = pl.BlockSpec((3, tr, RET_QK_DIM), lambda n: (0, 0, 0))
    dm = pl.BlockSpec((RET_HEADS, RET_BLK, RET_BLK), lambda n: (0, 0, 0))
    dv = pl.BlockSpec((RET_HEADS, RET_BLK, 1), lambda n: (0, 0, 0))
    dc = pl.BlockSpec((RET_HEADS, 1, 1), lambda n: (0, 0, 0))
    return ix, [tt, rt, dm, dv, dv, dc]


def _ret_fwd(z_a, tables, name="ret_fwd"):
    T = z_a.shape[0]
    tr = min(RET_TR, T)
    cpb = tr // RET_BLK
    nb = T // tr
    QW, VW = RET_HEADS * RET_QK_DIM, RET_HEADS * RET_V_DIM
    ix, tab_specs = _ret_specs(tr, False, nb)

    def body(z_ref, tt_ref, rt_ref, dm_ref, qd_ref, kd_ref, cd_ref, y_ref, st_ref, state):
        @pl.when(pl.program_id(0) == 0)
        def _():
            state[...] = jnp.zeros_like(state)

        rot_c, rot_s = _rot_tables(tt_ref, rt_ref)
        for c in range(cpb):
            for h in range(RET_HEADS):
                rs = slice(c * RET_BLK, (c + 1) * RET_BLK)
                cc, ss = rot_c[rs, :], rot_s[rs, :]
                q = z_ref[rs, h * RET_QK_DIM:(h + 1) * RET_QK_DIM].astype(F32)
                k = z_ref[rs, QW + h * RET_QK_DIM:QW + (h + 1) * RET_QK_DIM].astype(F32)
                v = z_ref[rs, 2 * QW + h * RET_V_DIM:2 * QW + (h + 1) * RET_V_DIM]
                gt = z_ref[rs, 2 * QW + VW + h * RET_V_DIM:2 * QW + VW + (h + 1) * RET_V_DIM].astype(F32)
                qr = _rot(q, cc, ss)
                kr = _rot(k, cc, ss) * QK_SCALE
                s_old = state[h]
                sb = s_old.astype(BF16)
                st_ref[c, h] = sb
                s = _dot(qr.astype(BF16), kr.astype(BF16), NT) * dm_ref[h]
                o = _dot(s.astype(BF16), v, NN) + _dot((qr * qd_ref[h]).astype(BF16), sb, NN)
                state[h] = s_old * cd_ref[h] + _dot((kr * kd_ref[h]).astype(BF16), v, TN)
                mu = jnp.mean(o, axis=-1, keepdims=True)
                oc = o - mu
                rn = oc * lax.rsqrt(jnp.mean(oc * oc, axis=-1, keepdims=True) + EPS)
                silu = gt / (1.0 + jnp.exp(-gt))
                y_ref[rs, h * RET_V_DIM:(h + 1) * RET_V_DIM] = (silu * rn).astype(y_ref.dtype)

    return pl.pallas_call(
        body,
        grid=(nb,),
        in_specs=[pl.BlockSpec((tr, RET_W), lambda n: (n, 0))] + tab_specs,
        out_specs=[pl.BlockSpec((tr, VW), lambda n: (n, 0)),
                   pl.BlockSpec((cpb, RET_HEADS, RET_QK_DIM, RET_V_DIM), lambda n: (n, 0, 0, 0))],
        out_shape=[jax.ShapeDtypeStruct((T, Y_COLS), BF16),
                   jax.ShapeDtypeStruct((T // RET_BLK, RET_HEADS, RET_QK_DIM, RET_V_DIM), BF16)],
        scratch_shapes=[pltpu.VMEM((RET_HEADS, RET_QK_DIM, RET_V_DIM), F32)],
        compiler_params=_cp("arbitrary"),
        name=name,
    )(z_a, *tables)


def _ret_bwd(z_a, dy, states, tables, name="ret_bwd"):
    T = z_a.shape[0]
    tr = min(RET_TR, T)
    cpb = tr // RET_BLK
    nb = T // tr
    QW, VW = RET_HEADS * RET_QK_DIM, RET_HEADS * RET_V_DIM
    ix, tab_specs = _ret_specs(tr, True, nb)

    def body(z_ref, dy_ref, st_ref, tt_ref, rt_ref, dm_ref, qd_ref, kd_ref, cd_ref, dz_ref, dstate):
        @pl.when(pl.program_id(0) == 0)
        def _():
            dstate[...] = jnp.zeros_like(dstate)

        rot_c, rot_s = _rot_tables(tt_ref, rt_ref)
        for c in reversed(range(cpb)):
            for h in range(RET_HEADS):
                rs = slice(c * RET_BLK, (c + 1) * RET_BLK)
                cc, ss = rot_c[rs, :], rot_s[rs, :]
                q = z_ref[rs, h * RET_QK_DIM:(h + 1) * RET_QK_DIM].astype(F32)
                k = z_ref[rs, QW + h * RET_QK_DIM:QW + (h + 1) * RET_QK_DIM].astype(F32)
                v = z_ref[rs, 2 * QW + h * RET_V_DIM:2 * QW + (h + 1) * RET_V_DIM]
                gt = z_ref[rs, 2 * QW + VW + h * RET_V_DIM:2 * QW + VW + (h + 1) * RET_V_DIM].astype(F32)
                dyv = dy_ref[rs, h * RET_V_DIM:(h + 1) * RET_V_DIM].astype(F32)
                dmat, qd, kd = dm_ref[h], qd_ref[h], kd_ref[h]
                qr = _rot(q, cc, ss)
                kr = _rot(k, cc, ss) * QK_SCALE
                qrb, krb = qr.astype(BF16), kr.astype(BF16)
                sb = st_ref[c, h]
                sd = (_dot(qrb, krb, NT) * dmat).astype(BF16)
                qdb = (qr * qd).astype(BF16)
                kdb = (kr * kd).astype(BF16)
                o = _dot(sd, v, NN) + _dot(qdb, sb, NN)
                mu = jnp.mean(o, axis=-1, keepdims=True)
                oc = o - mu
                rstd = lax.rsqrt(jnp.mean(oc * oc, axis=-1, keepdims=True) + EPS)
                rn = oc * rstd
                sg = 1.0 / (1.0 + jnp.exp(-gt))
                dgt = dyv * rn * (sg * (1.0 + gt * (1.0 - sg)))
                drn = dyv * (gt * sg)
                do = rstd * (drn - jnp.mean(drn, axis=-1, keepdims=True) - rn * jnp.mean(drn * rn, axis=-1, keepdims=True))
                dob = do.astype(BF16)
                dsn = dstate[h]
                dsnb = dsn.astype(BF16)
                ds_raw = (_dot(dob, v, NT) * dmat).astype(BF16)
                dv = _dot(sd, dob, TN) + _dot(kdb, dsnb, NN)
                dqr = _dot(ds_raw, krb, NN) + qd * _dot(dob, sb, NT)
                dkr = (_dot(ds_raw, qrb, TN) + kd * _dot(v, dsnb, NT)) * QK_SCALE
                dstate[h] = dsn * cd_ref[h] + _dot(qdb, dob, TN)
                dq = dqr * cc + pltpu.roll(dqr * ss, RET_QK_DIM // 2, 1)
                dk = dkr * cc + pltpu.roll(dkr * ss, RET_QK_DIM // 2, 1)
                dz_ref[rs, h * RET_QK_DIM:(h + 1) * RET_QK_DIM] = dq.astype(dz_ref.dtype)
                dz_ref[rs, QW + h * RET_QK_DIM:QW + (h + 1) * RET_QK_DIM] = dk.astype(dz_ref.dtype)
                dz_ref[rs, 2 * QW + h * RET_V_DIM:2 * QW + (h + 1) * RET_V_DIM] = dv.astype(dz_ref.dtype)
                dz_ref[rs, 2 * QW + VW + h * RET_V_DIM:2 * QW + VW + (h + 1) * RET_V_DIM] = dgt.astype(dz_ref.dtype)

    return pl.pallas_call(
        body,
        grid=(nb,),
        in_specs=[pl.BlockSpec((tr, RET_W), lambda n: (ix(n), 0)),
                  pl.BlockSpec((tr, VW), lambda n: (ix(n), 0)),
                  pl.BlockSpec((cpb, RET_HEADS, RET_QK_DIM, RET_V_DIM), lambda n: (ix(n), 0, 0, 0))] + tab_specs,
        out_specs=pl.BlockSpec((tr, RET_W), lambda n: (ix(n), 0)),
        out_shape=jax.ShapeDtypeStruct((T, RET_W), BF16),
        scratch_shapes=[pltpu.VMEM((RET_HEADS, RET_QK_DIM, RET_V_DIM), F32)],
        compiler_params=_cp("arbitrary"),
        name=name,
    )(z_a, dy, states, *tables)


ATT_TQ = 256
ATT_CPB = ATT_TQ // CHUNK
ATT_SCALE = ATT_HEAD_DIM ** -0.5


ATT_WIN = 3 * ATT_TQ
ATT_NB = CHUNK * ATT_BAND


def _rel_index():
    i = np.arange(CHUNK)[:, None]
    j = np.arange(ATT_BAND)[None, :]
    rel = np.clip(i + ATT_PAST * CHUNK - j, -MAX_REL, MAX_REL) + MAX_REL
    return jnp.asarray(rel.reshape(1, ATT_NB).astype(np.int32))


def _split3(x):
    hi = x.astype(BF16)
    r1 = x - hi.astype(F32)
    mid = r1.astype(BF16)
    lo = (r1 - mid.astype(F32)).astype(BF16)
    return hi, mid, lo


REL_TILE = 4608


def _bias_expand(rel_bias, name="bias_expand"):
    H = rel_bias.shape[0]
    n = ATT_NB
    padded = jnp.pad(rel_bias, ((0, 0), (0, N_REL_PAD - N_REL)))

    def body(rb_ref, idx_ref, o_ref):
        onehot = (lax.broadcasted_iota(jnp.int32, (N_REL_PAD, REL_TILE), 0) == idx_ref[...]).astype(BF16)
        hi, mid, lo = _split3(rb_ref[...])
        o_ref[...] = _dot(hi, onehot, NN) + _dot(mid, onehot, NN) + _dot(lo, onehot, NN)

    out = pl.pallas_call(
        body,
        grid=(n // REL_TILE,),
        in_specs=[pl.BlockSpec((H, N_REL_PAD), lambda t: (0, 0)), pl.BlockSpec((1, REL_TILE), lambda t: (0, t))],
        out_specs=pl.BlockSpec((H, REL_TILE), lambda t: (0, t)),
        out_shape=jax.ShapeDtypeStruct((H, n), F32),
        compiler_params=_cp("parallel"),
        name=name,
    )(padded, _rel_index())
    return out.reshape(H, CHUNK, ATT_BAND)


def _bias_tile(band, name="bias_tile"):
    H = band.shape[0]
    padded = jnp.pad(band, ((0, 0), (0, 0), (0, ATT_WIN - ATT_BAND)), constant_values=NEG_INF)

    def body(b_ref, o_ref):
        b = b_ref[...]
        col = lax.broadcasted_iota(jnp.int32, (CHUNK, ATT_WIN), 1)
        keep = col >= (2 - pl.program_id(0)) * ATT_TQ
        for a in range(ATT_CPB):
            o_ref[a * CHUNK:(a + 1) * CHUNK, :] = jnp.where(keep, pltpu.roll(b, a * CHUNK, 1) if a else b, NEG_INF)

    return pl.pallas_call(
        body,
        grid=(3, H),
        in_specs=[pl.BlockSpec((None, CHUNK, ATT_WIN), lambda v, h: (h, 0, 0))],
        out_specs=pl.BlockSpec((None, None, ATT_TQ, ATT_WIN), lambda v, h: (v, h, 0, 0)),
        out_shape=jax.ShapeDtypeStruct((3, H, ATT_TQ, ATT_WIN), F32),
        compiler_params=_cp("parallel", "parallel"),
        name=name,
    )(padded)


def _bias_untile(dtile, name="bias_untile"):
    H = dtile.shape[0]

    def body(d_ref, o_ref):
        acc = d_ref[0:CHUNK, :]
        for a in range(1, ATT_CPB):
            acc = acc + pltpu.roll(d_ref[a * CHUNK:(a + 1) * CHUNK, :], ATT_WIN - a * CHUNK, 1)
        o_ref[...] = acc

    out = pl.pallas_call(
        body,
        grid=(H,),
        in_specs=[pl.BlockSpec((None, ATT_TQ, ATT_WIN), lambda h: (h, 0, 0))],
        out_specs=pl.BlockSpec((None, CHUNK, ATT_WIN), lambda h: (h, 0, 0)),
        out_shape=jax.ShapeDtypeStruct((H, CHUNK, ATT_WIN), F32),
        compiler_params=_cp("parallel"),
        name=name,
    )(dtile)
    return out[:, :, :ATT_BAND]


def _bias_reduce(dbias, name="bias_reduce"):
    H = dbias.shape[0]
    n = ATT_NB

    def body(db_ref, idx_ref, o_ref):
        @pl.when(pl.program_id(0) == 0)
        def _():
            o_ref[...] = jnp.zeros_like(o_ref)

        onehot = (lax.broadcasted_iota(jnp.int32, (N_REL_PAD, REL_TILE), 0) == idx_ref[...]).astype(BF16)
        hi, mid, lo = _split3(db_ref[...])
        o_ref[...] += _dot(hi, onehot, NT) + _dot(mid, onehot, NT) + _dot(lo, onehot, NT)

    out = pl.pallas_call(
        body,
        grid=(n // REL_TILE,),
        in_specs=[pl.BlockSpec((H, REL_TILE), lambda t: (0, t)), pl.BlockSpec((1, REL_TILE), lambda t: (0, t))],
        out_specs=pl.BlockSpec((H, N_REL_PAD), lambda t: (0, 0)),
        out_shape=jax.ShapeDtypeStruct((H, N_REL_PAD), F32),
        compiler_params=_cp("arbitrary"),
        name=name,
    )(dbias.reshape(H, n), _rel_index())
    return out[:, :N_REL]


def _att_probs(q, kwin, bias):
    s = _dot(q, kwin, NT) + bias
    e = jnp.exp(s - jnp.max(s, axis=-1, keepdims=True))
    return e * (1.0 / jnp.sum(e, axis=-1, keepdims=True))


ATT_PAIR = 2 * ATT_HEAD_DIM
ATT_NP = ATT_HEADS // 2
ATT_QW = ATT_HEADS * ATT_HEAD_DIM
Y_COLS = RET_HEADS * RET_V_DIM + ATT_QW


def _att_specs(tq, nq, clip_q, q_col0):
    cb = ATT_QW // ATT_PAIR
    qi = (lambda p, m: (jnp.minimum(m, nq - 1), q_col0 + p)) if clip_q else (lambda p, m: (m, q_col0 + p))
    q = pl.BlockSpec((tq, ATT_PAIR), qi)

    def win(col0):
        return [pl.BlockSpec((tq, ATT_PAIR), functools.partial(lambda p, m, back: (jnp.clip(m - back, 0, nq - 1), col0 + p), back=b))
                for b in (2, 1, 0)]

    bias = pl.BlockSpec((None, 2, ATT_TQ, ATT_WIN), lambda p, m: (jnp.minimum(m, 2), p, 0, 0))
    return q, win(cb), win(2 * cb), bias


def _head_masks(rows):
    lane = lax.broadcasted_iota(jnp.int32, (rows, ATT_PAIR), 1)
    return lane < ATT_HEAD_DIM


def _att_fwd(z_b, bias, y, comm=None, name="att_fwd"):
    T = z_b.shape[0]
    tq = ATT_TQ
    nq = T // tq
    qs, kwin, vwin, bs = _att_specs(tq, nq, False, 0)
    nc = comm.n if comm else 0
    total = ATT_NP * nq

    def body(*refs):
        q_ref, k0, k1, k2, v0, v1, v2, b_ref = refs[:8]
        cin = refs[9:9 + nc]
        o_ref = refs[9 + nc]
        cout = refs[10 + nc:10 + 2 * nc]
        csem = refs[10 + 2 * nc:]
        m = pl.program_id(1)
        step = pl.program_id(0) * nq + m
        _host_comm(comm, "early", step, total, cin, cout, csem)
        kw = jnp.concatenate([k0[...], k1[...], k2[...]], axis=0)
        vw = jnp.concatenate([v0[...], v1[...], v2[...]], axis=0)
        q2 = q_ref[...] * ATT_SCALE
        even = _head_masks(tq)
        outs = []
        for hh in range(2):
            qm = jnp.where(even if hh == 0 else ~even, q2, jnp.zeros_like(q2))
            p = _att_probs(qm, kw, b_ref[hh])
            outs.append(_dot(p.astype(BF16), vw, NN))
        o_ref[...] = jnp.where(even, outs[0], outs[1]).astype(o_ref.dtype)
        _host_comm(comm, "late", step, total, cin, cout, csem)

    y_cb = (Y_COLS - ATT_QW) // ATT_PAIR
    out = pl.pallas_call(
        body,
        grid=(ATT_NP, nq),
        in_specs=[qs] + kwin + vwin + [bs, ANY] + [ANY] * nc,
        out_specs=[pl.BlockSpec((tq, ATT_PAIR), lambda p, m: (m, y_cb + p))] + [ANY] * nc,
        out_shape=[jax.ShapeDtypeStruct((T, Y_COLS), BF16)] + (comm.out_shape if comm else []),
        scratch_shapes=comm.scratch if comm else [],
        input_output_aliases={8: 0},
        compiler_params=_cp("arbitrary", "arbitrary"),
        name=name,
    )(z_b, z_b, z_b, z_b, z_b, z_b, z_b, bias, y, *(comm.arrays if comm else []))
    return out[0], list(out[1:])


def _att_bwd(z_b, bias, dy, comm=None, name="att_bwd"):
    T = z_b.shape[0]
    tq = ATT_TQ
    nq = T // tq
    y_cb = (Y_COLS - ATT_QW) // ATT_PAIR
    qs, kwin, vwin, bs = _att_specs(tq, nq, True, 0)
    dos = _att_specs(tq, nq, True, y_cb)[0]
    kv_out = pl.BlockSpec((tq, ATT_PAIR), lambda p, m: (jnp.maximum(m - 2, 0), p))
    W3 = 3 * tq
    nc = comm.n if comm else 0
    total = ATT_NP * (nq + 2)

    def body(*refs):
        q_ref, k0, k1, k2, v0, v1, v2, b_ref, do_ref = refs[:9]
        cin = refs[9:9 + nc]
        dq_ref, dk_ref, dv_ref, db_ref = refs[9 + nc:13 + nc]
        cout = refs[13 + nc:13 + 2 * nc]
        dkc, dvc, dkw, dvw = refs[13 + 2 * nc:17 + 2 * nc]
        csem = refs[17 + 2 * nc:]
        m = pl.program_id(1)
        step = pl.program_id(0) * (nq + 2) + m
        _host_comm(comm, "early", step, total, cin, cout, csem)

        @pl.when(m == 0)
        def _():
            dkc[...] = jnp.zeros_like(dkc)
            dvc[...] = jnp.zeros_like(dvc)
            db_ref[...] = jnp.zeros_like(db_ref)

        @pl.when(m >= nq)
        def _():
            dkw[...] = jnp.zeros_like(dkw)
            dvw[...] = jnp.zeros_like(dvw)

        @pl.when(m < nq)
        def _():
            kw = jnp.concatenate([k0[...], k1[...], k2[...]], axis=0)
            vw = jnp.concatenate([v0[...], v1[...], v2[...]], axis=0)
            q2, do2 = q_ref[...] * ATT_SCALE, do_ref[...]
            even = _head_masks(tq)
            dqs, dks, dvs = [], [], []
            for hh in range(2):
                mine = even if hh == 0 else ~even
                p = _att_probs(jnp.where(mine, q2, jnp.zeros_like(q2)), kw, b_ref[hh])
                dp = _dot(jnp.where(mine, do2, jnp.zeros_like(do2)), vw, NT)
                ds = p * (dp - jnp.sum(dp * p, axis=-1, keepdims=True))
                db_ref[hh] += ds
                dsb = ds.astype(BF16)
                dqs.append(_dot(dsb, kw, NN))
                dks.append(_dot(dsb, q2, TN))
                dvs.append(_dot(p.astype(BF16), do2, TN))
            even_w = _head_masks(W3)
            dq_ref[...] = (jnp.where(even, dqs[0], dqs[1]) * ATT_SCALE).astype(dq_ref.dtype)
            dkw[...] = jnp.where(even_w, dks[0], dks[1])
            dvw[...] = jnp.where(even_w, dvs[0], dvs[1])

        dk_ref[...] = (dkc[0:tq, :] + dkw[0:tq, :]).astype(dk_ref.dtype)
        dv_ref[...] = (dvc[0:tq, :] + dvw[0:tq, :]).astype(dv_ref.dtype)
        dkc[0:tq, :] = dkc[tq:2 * tq, :] + dkw[tq:2 * tq, :]
        dvc[0:tq, :] = dvc[tq:2 * tq, :] + dvw[tq:2 * tq, :]
        dkc[tq:2 * tq, :] = dkw[2 * tq:W3, :]
        dvc[tq:2 * tq, :] = dvw[2 * tq:W3, :]
        _host_comm(comm, "late", step, total, cin, cout, csem)

    qo = pl.BlockSpec((tq, ATT_PAIR), lambda p, m: (jnp.minimum(m, nq - 1), p))
    dbs = pl.BlockSpec((2, ATT_TQ, ATT_WIN), lambda p, m: (p, 0, 0))
    hd = jax.ShapeDtypeStruct((T, ATT_QW), BF16)
    out = pl.pallas_call(
        body,
        grid=(ATT_NP, nq + 2),
        in_specs=[qs] + kwin + vwin + [bs, dos] + [ANY] * nc,
        out_specs=[qo, kv_out, kv_out, dbs] + [ANY] * nc,
        out_shape=[hd, hd, hd, jax.ShapeDtypeStruct((ATT_HEADS, ATT_TQ, ATT_WIN), F32)] + (comm.out_shape if comm else []),
        scratch_shapes=[pltpu.VMEM((2 * tq, ATT_PAIR), F32), pltpu.VMEM((2 * tq, ATT_PAIR), F32),
                        pltpu.VMEM((W3, ATT_PAIR), F32), pltpu.VMEM((W3, ATT_PAIR), F32)] + (comm.scratch if comm else []),
        compiler_params=_cp("arbitrary", "arbitrary"),
        name=name,
    )(z_b, z_b, z_b, z_b, z_b, z_b, z_b, bias, dy, *(comm.arrays if comm else []))
    return out[0], out[1], out[2], out[3], list(out[4:])


REST = ["ab_out", "c_in_t", "c_out", "up_t0", "up_t1", "down0", "down1"]


def _local_step(x, target, wt, small, rest_shards=None, overlap=False, hn0=None):
    T = x.shape[0]
    Fh = FFN_HIDDEN
    tables = _ret_tables(T)
    gw, gs, recv = {}, {}, {}
    wt = dict(wt)

    if hn0 is None:
        hn0 = _rms_fwd(x, small["attn_norm_g"][0], name="rms_fwd")
    z_a = _mm(hn0, wt["ab_in_t"][:RET_W], "nt", BF16, name="mm_ab_in_a")
    z_b = _mm(hn0, wt["ab_in_t"][RET_W:], "nt", BF16, name="mm_ab_in_b")
    y, states = _ret_fwd(z_a, tables)
    bias = _bias_tile(_bias_expand(small["rel_bias"]))
    y, rest = _att_fwd(z_b, bias, y, comm=_Comm("gather", rest_shards) if rest_shards is not None else None)
    if rest_shards is not None:
        full = dict(zip(REST, rest))
        wt.update(ab_out=full["ab_out"], c_in_t=full["c_in_t"], c_out=full["c_out"],
                  up_t=[full["up_t0"], full["up_t1"]], down=[full["down0"], full["down1"]])
    h1, hf0 = _mm_rows(y, wt["ab_out"], x, "mm_ab_out", norm_g=small["ffn_norm_g"][0])

    def ffn_fwd(h, hf, layer, next_g):
        zg, zu, a = _ffn_up_mid(hf, wt["up_t"][layer], small["conv_w"][layer], small["conv_b"][layer][None, :])
        if next_g is None:
            return _mm_rows(a, wt["down"][layer], h, "mm_down_loss", loss=(target, small["final_g"])), (hf, zg, zu, a)
        h_out, hn_next = _mm_rows(a, wt["down"][layer], h, "mm_down", norm_g=next_g)
        return h_out, hn_next, (hf, zg, zu, a)

    def ffn_bwd(dh_out, h, layer, saved, exchange=None):
        hf, zg, zu, a = saved
        da = _mm(dh_out, wt["down"][layer], "nt", BF16, name="mm_d_a")
        d_down = _mm(a, dh_out, "tn", BF16, name="mm_dw_down")
        comm = _Comm("exchange", [gw[k] for k in exchange]) if exchange else None
        dzg, dzu, dcw, dcb, got = _ffn_mid_bwd(zg, zu, da, small["conv_w"][layer], small["conv_b"][layer][None, :], comm=comm)
        recv.update(zip(exchange or [], got))
        dh, dg = _mm_rows(dzg, wt["up_t"][layer][:Fh], None, "mm_d_hf_norm", second=(dzu, wt["up_t"][layer][Fh:]),
                          bwd=(h, small["ffn_norm_g"][layer], dh_out))
        d_up = _mm(dzg, hf, "tn", BF16, name="mm_dw_up", rows=(2 * Fh, 0, None))
        d_up = _mm(dzu, hf, "tn", BF16, name="mm_dw_up_2", rows=(2 * Fh, Fh, d_up))
        return dh, dg, d_up, d_down, dcw, dcb

    h2, hn1, ffn0 = ffn_fwd(h1, hf0, 0, small["attn_norm_g"][1])

    zz = _mm(hn1, wt["c_in_t"], "nt", BF16, name="mm_c_in")
    ys = _sgu_fwd(zz, small["ln_g"], small["ln_b"], small["w_s"], small["b_s"])
    h3, hf1 = _mm_rows(ys, wt["c_out"], h2, "mm_c_out", norm_g=small["ffn_norm_g"][1])
    (loss_vec, dh4, gs["final_g"]), ffn1 = ffn_fwd(h3, hf1, 1, None)

    dh3, dgf1, gw["up_t1"], gw["down1"], dcw1, dcb1 = ffn_bwd(dh4, h3, 1, ffn1)
    dys = _mm(dh3, wt["c_out"], "nt", BF16, name="mm_d_ys")
    gw["c_out"] = _mm(ys, dh3, "tn", BF16, name="mm_dw_c_out")
    dzz, gs["w_s"], dbs, dlg, dlb = _sgu_bwd(zz, dys, small["ln_g"], small["ln_b"], small["w_s"], small["b_s"])
    gs["b_s"], gs["ln_g"], gs["ln_b"] = dbs[:, :, 0], dlg[0], dlb[0]
    dh2, dga1 = _mm_rows(dzz, wt["c_in_t"], None, "mm_d_hn1_norm", bwd=(h2, small["attn_norm_g"][1], dh3))
    gw["c_in_t"] = _mm(dzz, hn1, "tn", BF16, name="mm_dw_c_in")

    dh1, dgf0, gw["up_t0"], gw["down0"], dcw0, dcb0 = ffn_bwd(
        dh2, h1, 0, ffn0, exchange=["c_in_t", "c_out", "up_t1", "down1"] if overlap else None)

    dy = _mm(dh1, wt["ab_out"], "nt", BF16, name="mm_d_y")
    gw["ab_out"] = _mm(y, dh1, "tn", BF16, name="mm_dw_ab_out")
    dz_a = _ret_bwd(z_a, dy, states, tables)
    late = ["ab_out", "up_t0", "down0"] if overlap else []
    dq, dk, dv, dbias, got = _att_bwd(z_b, bias, dy, comm=_Comm("exchange", [gw[k] for k in late]) if late else None)
    recv.update(zip(late, got))
    dz_b = jnp.concatenate([dq, dk, dv], axis=1)
    gs["rel_bias"] = _bias_reduce(_bias_untile(dbias))
    gw["ab_in_t"] = _mm(dz_a, hn0, "tn", BF16, name="mm_dw_ab_in_a", rows=(RET_W + ATT_W, 0, None))
    gw["ab_in_t"] = _mm(dz_b, hn0, "tn", BF16, name="mm_dw_ab_in_b", rows=(RET_W + ATT_W, RET_W, gw["ab_in_t"]))
    if overlap:
        dhn0, got = _mm(dz_a, wt["ab_in_t"][:RET_W], "nn", F32, name="mm_d_hn0", comm=_Comm("exchange", [gw["ab_in_t"]]))
        recv["ab_in_t"] = got[0]
    else:
        dhn0 = _mm(dz_a, wt["ab_in_t"][:RET_W], "nn", F32, name="mm_d_hn0")
    grad_x, dga0 = _mm_rows(dz_b, wt["ab_in_t"][RET_W:], dhn0, "mm_d_hn0_norm", bwd=(x, small["attn_norm_g"][0], dh1))

    gs["attn_norm_g"] = jnp.concatenate([dga0, dga1], axis=0)
    gs["ffn_norm_g"] = jnp.concatenate([dgf0, dgf1], axis=0)
    gs["conv_w"] = jnp.stack([dcw0, dcw1])
    gs["conv_b"] = jnp.concatenate([dcb0, dcb1], axis=0)
    gs["final_g"] = gs["final_g"][0]
    return loss_vec[0, 0], grad_x, gw, gs, recv


MESH_ID = pl.DeviceIdType.MESH
ANY = pl.BlockSpec(memory_space=pl.ANY)


def _my_place():
    return lax.axis_index("x"), lax.axis_index("y"), lax.axis_index("c")


class _Comm:
    def __init__(self, kind, arrays):
        self.kind, self.arrays, self.n = kind, list(arrays), len(arrays)
        if kind == "gather":
            self.out_shape = [jax.ShapeDtypeStruct((N_DEV * s.shape[0], s.shape[1]), s.dtype) for s in arrays]
        else:
            self.out_shape = [jax.ShapeDtypeStruct((N_DEV, g.shape[0] // N_DEV, g.shape[1]), g.dtype) for g in arrays]
        n = self.n
        self.scratch = [pltpu.SemaphoreType.DMA((n, 7)), pltpu.SemaphoreType.DMA((n, 7)), pltpu.SemaphoreType.DMA((n,))]

    def phase(self, ph, in_refs, out_refs, sems):
        (self._gather if self.kind == "gather" else self._exchange)(ph, in_refs, out_refs, sems)

    def _gather(self, ph, x_refs, o_refs, sems):
        n = self.n
        send_sems, recv_sems, local_sems = sems
        x, y, c = _my_place()
        me, sibling = (x, y, c), (x, y, 1 - c)
        chips = [(1 - x, y), (x, 1 - y), (1 - x, 1 - y)]

        def rows(a, place):
            m = x_refs[a].shape[0]
            px, py, pc = place
            return o_refs[a].at[pl.ds((4 * px + 2 * py + pc) * m, m), :]

        def copy(a, k, block, to, own=False):
            return pltpu.make_async_remote_copy(
                src_ref=x_refs[a] if own else rows(a, block), dst_ref=rows(a, block),
                send_sem=send_sems.at[a, k], recv_sem=recv_sems.at[a, k], device_id=to, device_id_type=MESH_ID)

        def mine():
            return [pltpu.make_async_copy(x_refs[a], rows(a, me), local_sems.at[a]) for a in range(n)]

        def first():
            out = []
            for a in range(n):
                out.append(copy(a, 0, me, sibling, own=True))
                out += [copy(a, 1 + j, me, (*chip, c), own=True) for j, chip in enumerate(chips)]
            return out

        def passed():
            return [copy(a, 4 + j, (*chip, c), sibling) for j, chip in enumerate(chips) for a in range(n)]

        if ph == 0:
            for cp in mine() + first():
                cp.start()
        elif ph == 1:
            fw = passed()
            for j, chip in enumerate(chips):
                for a in range(n):
                    copy(a, 1 + j, (*chip, c), me).wait_recv()
                    fw[j * n + a].start()
        else:
            for a in range(n):
                copy(a, 0, sibling, me).wait_recv()
                for j, chip in enumerate(chips):
                    copy(a, 4 + j, (*chip, 1 - c), me).wait_recv()
            for cp in first() + passed():
                cp.wait_send()
            for cp in mine():
                cp.wait()

    def _exchange(self, ph, g_refs, o_refs, sems):
        n = self.n
        send_sems, recv_sems, local_sems = sems
        x, y, c = _my_place()
        me = 4 * x + 2 * y + c
        peers = [(x ^ ((k >> 2) & 1), y ^ ((k >> 1) & 1), c ^ (k & 1)) for k in range(1, N_DEV)]

        def block(a, idx):
            m = g_refs[a].shape[0] // N_DEV
            return g_refs[a].at[pl.ds(idx * m, m), :]

        def copy(a, k, slot):
            px, py, pc = peers[k]
            return pltpu.make_async_remote_copy(
                src_ref=block(a, 4 * px + 2 * py + pc), dst_ref=o_refs[a].at[slot],
                send_sem=send_sems.at[a, k], recv_sem=recv_sems.at[a, k], device_id=peers[k], device_id_type=MESH_ID)

        if ph == 1:
            return
        mine = [pltpu.make_async_copy(block(a, me), o_refs[a].at[me], local_sems.at[a]) for a in range(n)]
        sends = [copy(a, k, me) for k in range(N_DEV - 1) for a in range(n)]
        if ph == 0:
            for cp in mine + sends:
                cp.start()
        else:
            for k in range(N_DEV - 1):
                px, py, pc = peers[k]
                for a in range(n):
                    copy(a, k, 4 * px + 2 * py + pc).wait_recv()
            for cp in sends:
                cp.wait_send()
            for cp in mine:
                cp.wait()


def _comm_call(comm, name):
    n = comm.n

    def body(*refs):
        for ph in range(3):
            comm.phase(ph, refs[:n], refs[n:2 * n], refs[2 * n:])

    return pl.pallas_call(
        body, out_shape=comm.out_shape, in_specs=[ANY] * n, out_specs=[ANY] * n, scratch_shapes=comm.scratch, name=name,
    )(*comm.arrays)


def _host_comm(comm, when, step, total, cin, cout, csem):
    if comm is None:
        return
    sched = {0: 0, 1: (3 * total) // 4, 2: total - 1}
    phases = (0, 1) if when == "early" else (2,)
    for ph in phases:
        if ph == 1 and comm.kind == "exchange":
            continue

        @pl.when(step == sched[ph])
        def _(ph=ph):
            comm.phase(ph, cin, cout, csem)


def _all_gather(shards, name="all_gather"):
    return _comm_call(_Comm("gather", shards), name)


def _row_tile(r, target=256):
    best = None
    for t in range(8, min(r, target) + 1, 8):
        if r % t == 0:
            best = t
    return best if best is not None else r


def _sum8(parts, name="sum8"):
    _, M, N = parts.shape
    tr = _row_tile(M, 128)

    def body(p_ref, o_ref):
        acc = p_ref[0].astype(F32)
        for d in range(1, N_DEV):
            acc = acc + p_ref[d].astype(F32)
        o_ref[...] = acc

    return pl.pallas_call(
        body,
        grid=(M // tr,),
        in_specs=[pl.BlockSpec((N_DEV, tr, N), lambda i: (0, i, 0))],
        out_specs=pl.BlockSpec((tr, N), lambda i: (i, 0)),
        out_shape=jax.ShapeDtypeStruct((M, N), F32),
        compiler_params=_cp("parallel"),
        name=name,
    )(parts)


def _adamw(w, g, m, v, name="adamw"):
    shape = w.shape
    if w.ndim == 1:
        r2 = (1, shape[0])
    else:
        r2 = (int(np.prod(shape[:-1])), shape[-1])
    R, C = r2
    tr = _row_tile(R)
    bc1 = 1.0 - ADAM_B1 ** ADAM_STEP
    bc2 = 1.0 - ADAM_B2 ** ADAM_STEP

    def body(w_ref, g_ref, m_ref, v_ref, d_ref, nm_ref, nv_ref):
        gv = g_ref[...]
        nm = ADAM_B1 * m_ref[...] + (1.0 - ADAM_B1) * gv
        nv = ADAM_B2 * v_ref[...] + (1.0 - ADAM_B2) * (gv * gv)
        d_ref[...] = -ADAM_LR * ((nm / bc1) / (jnp.sqrt(nv / bc2) + ADAM_EPS) + ADAM_WD * w_ref[...])
        nm_ref[...] = nm
        nv_ref[...] = nv

    spec = pl.BlockSpec((tr, C), lambda i: (i, 0))
    out = pl.pallas_call(
        body,
        grid=(R // tr,),
        in_specs=[spec] * 4,
        out_specs=[spec] * 3,
        out_shape=[jax.ShapeDtypeStruct(r2, F32)] * 3,
        compiler_params=_cp("parallel"),
        name=name,
    )(w.reshape(r2), g.reshape(r2), m.reshape(r2), v.reshape(r2))
    return [o.reshape(shape) for o in out]


WEIGHTS = ['attn_norm_g', 'ffn_norm_g', 'ab_w_in', 'ab_w_out', 'ab_rel_bias', 'c_w_in', 'c_ln_g', 'c_ln_b', 'c_w_s', 'c_b_s',
           'c_w_out', 'ffn_w_up', 'ffn_conv_w', 'ffn_conv_b', 'ffn_w_down', 'final_norm_g']
SMALL_ORDER = ["attn_norm_g", "ffn_norm_g", "rel_bias", "ln_g", "ln_b", "w_s", "b_s", "conv_w", "conv_b", "final_g"]
PACK_ROW = 1024


def _pack(arrs):
    flat = jnp.concatenate([a.reshape(-1) for a in arrs])
    n = flat.shape[0]
    padded = -(-n // PACK_ROW) * PACK_ROW
    return jnp.pad(flat, (0, padded - n)).reshape(padded // 128, 128)


def _unpack(flat, shapes):
    out, off = [], 0
    for s in shapes:
        n = int(np.prod(s))
        out.append(flat[off:off + n].reshape(s))
        off += n
    return out


def _step(P):
    x, target = P["x"][0], P["loss_target"][0]
    me = 4 * lax.axis_index("x") + 2 * lax.axis_index("y") + lax.axis_index("c")
    n_up = P["ffn_w_up"].shape[0]
    Fc = P["ffn_conv_w"].shape[-1]
    Lc = P["c_ln_g"].shape[-1]

    first = _Comm("gather", [P["ab_w_in"][0].T.astype(BF16), _pack([P["ffn_conv_w"], P["c_ln_g"], P["c_ln_b"]])])
    hn0, full = _rms_fwd(x, P["attn_norm_g"][0], comm=first, name="rms_fwd_gather")
    wt = {"ab_in_t": full[0]}
    rest = {"ab_out": P["ab_w_out"][0], "c_in_t": P["c_w_in"][0].T, "c_out": P["c_w_out"][0],
            "up_t0": P["ffn_w_up"][0].T, "up_t1": P["ffn_w_up"][1].T, "down0": P["ffn_w_down"][0], "down1": P["ffn_w_down"][1]}
    rest_shards = [rest[k].astype(BF16) for k in REST]
    sm = full[-1].reshape(N_DEV, -1)
    conv_w = sm[:, :n_up * 3 * Fc].reshape(N_DEV, n_up, 3, Fc).transpose(1, 2, 0, 3).reshape(n_up, 3, N_DEV * Fc)
    off = n_up * 3 * Fc
    ln_g = sm[:, off:off + Lc].reshape(N_DEV * Lc)
    ln_b = sm[:, off + Lc:off + 2 * Lc].reshape(N_DEV * Lc)
    small = {"attn_norm_g": P["attn_norm_g"], "ffn_norm_g": P["ffn_norm_g"], "rel_bias": P["ab_rel_bias"][0],
             "ln_g": ln_g, "ln_b": ln_b, "w_s": P["c_w_s"][0], "b_s": P["c_b_s"][0], "conv_w": conv_w,
             "conv_b": P["ffn_conv_b"], "final_g": P["final_norm_g"]}

    loss_part, grad_x, gw, gs, recv = _local_step(x, target, wt, small, rest_shards=rest_shards, overlap=True, hn0=hn0)
    loss = lax.psum(loss_part, ("x", "y", "c"))

    s8 ={k: _sum8(recv[k], name="sum8") for k in ["ab_in_t"] + REST}
    g_big = {"ab_w_in": s8["ab_in_t"].T[None], "ab_w_out": s8["ab_out"][None], "c_w_in": s8["c_in_t"].T[None],
             "c_w_out": s8["c_out"][None], "ffn_w_up": jnp.stack([s8["up_t0"].T, s8["up_t1"].T]),
             "ffn_w_down": jnp.stack([s8["down0"], s8["down1"]])}

    packed = _pack([gs[k] for k in SMALL_ORDER])
    gathered = _all_gather([packed], name="gather_small_grads")[0]
    tot = _sum8(gathered.reshape(N_DEV, packed.shape[0], 128), name="sum8_small").reshape(-1)
    gsm = dict(zip(SMALL_ORDER, _unpack(tot, [gs[k].shape for k in SMALL_ORDER])))
    grads = dict(g_big)
    grads["attn_norm_g"] = gsm["attn_norm_g"]
    grads["ffn_norm_g"] = gsm["ffn_norm_g"]
    grads["ab_rel_bias"] = gsm["rel_bias"][None]
    grads["c_ln_g"] = lax.dynamic_slice(gsm["ln_g"], (me * Lc,), (Lc,))[None]
    grads["c_ln_b"] = lax.dynamic_slice(gsm["ln_b"], (me * Lc,), (Lc,))[None]
    grads["c_w_s"] = gsm["w_s"][None]
    grads["c_b_s"] = gsm["b_s"][None]
    grads["ffn_conv_w"] = lax.dynamic_slice(gsm["conv_w"], (0, 0, me * Fc), (n_up, 3, Fc))
    grads["ffn_conv_b"] = gsm["conv_b"]
    grads["final_norm_g"] = gsm["final_g"]

    delta, new_m, new_v = {}, {}, {}
    for k in WEIGHTS:
        delta[k], new_m[k], new_v[k] = _adamw(P[k], grads[k], P["m_" + k], P["v_" + k], name="adamw")
    return (loss, grad_x[None], *[grads[k] for k in WEIGHTS], *[delta[k] for k in WEIGHTS],
            *[new_m[k] for k in WEIGHTS], *[new_v[k] for k in WEIGHTS])


def kernel(x, attn_norm_g, ffn_norm_g, ab_w_in, ab_w_out, ab_rel_bias, c_w_in, c_ln_g, c_ln_b, c_w_s, c_b_s, c_w_out, ffn_w_up, ffn_conv_w, ffn_conv_b, ffn_w_down, final_norm_g, loss_target, m_attn_norm_g, m_ffn_norm_g, m_ab_w_in, m_ab_w_out, m_ab_rel_bias, m_c_w_in, m_c_ln_g, m_c_ln_b, m_c_w_s, m_c_b_s, m_c_w_out, m_ffn_w_up, m_ffn_conv_w, m_ffn_conv_b, m_ffn_w_down, m_final_norm_g, v_attn_norm_g, v_ffn_norm_g, v_ab_w_in, v_ab_w_out, v_ab_rel_bias, v_c_w_in, v_c_ln_g, v_c_ln_b, v_c_w_s, v_c_b_s, v_c_w_out, v_ffn_w_up, v_ffn_conv_w, v_ffn_conv_b, v_ffn_w_down, v_final_norm_g):
    return _step(dict(locals()))
```
